```python
import jax, jax.numpy as jnp
from jax import lax
import numpy as np

D_MODEL = 1024
BATCH = 2
SEQ = 8192
DEPTH = 1

GRID_W = 64
CTX_LEN = 256
N_DIR = 2
D_RWKV = 512
RWKV_HEAD = 64
RWKV_HEADS = D_RWKV // RWKV_HEAD
DECAY_LORA = 64
AAA_LORA = 64
GATE_LORA = 128
D_RET = D_MODEL - D_RWKV
RET_HEADS = 4
RET_HEAD = D_RET // RET_HEADS
RET_CHUNK = 128
RWKV_COLS = 3 * D_RWKV + N_DIR * (DECAY_LORA + AAA_LORA) + GATE_LORA
RET_COLS = 4 * D_RET
IN_COLS = RWKV_COLS + RET_COLS
RWKV_SPLITS = (D_RWKV, 2 * D_RWKV, 3 * D_RWKV, 3 * D_RWKV + N_DIR * DECAY_LORA, 3 * D_RWKV + N_DIR * (DECAY_LORA + AAA_LORA))
N_GROUPS = 4
EXPERTS_PER_GROUP = 8
N_EXPERTS = N_GROUPS * EXPERTS_PER_GROUP
TOP_K_IN_GROUP = 2
EXPERT_HIDDEN = D_MODEL // 2
MOE_BLOCK = 128
ROPE_BASE = 10000.0
LN_EPS = 1e-5
RWKV_GN_EPS = 64e-5
RET_GN_EPS = 1e-5
DEEPNORM_ALPHA = (2.0 * DEPTH) ** 0.25
DEEPNORM_BETA = (8.0 * DEPTH) ** -0.25

kernel_name = 'hybrid_rwkv7_retention_hmoe_dit'


def _layer_norm(z, g, b, eps=LN_EPS):
    zf = z.astype(jnp.float32)
    mu = jnp.mean(zf, axis=-1, keepdims=True)
    var = jnp.mean(jnp.square(zf - mu), axis=-1, keepdims=True)
    return ((zf - mu) * lax.rsqrt(var + eps)).astype(z.dtype) * g + b


def _heads(z, n_heads):
    return z.reshape(z.shape[:-1] + (n_heads, z.shape[-1] // n_heads))


def _head_norm(y, g, b, eps):
    n_heads, dh = y.shape[-2:]
    out = _layer_norm(y, g.reshape(n_heads, dh), b.reshape(n_heads, dh), eps)
    return out.reshape(y.shape[:-2] + (n_heads * dh,))


def _flip_time(z, rev):
    return jnp.flip(z, axis=1) if rev else z


def _qshift_grid(p, rows):
    b, n, ch = p.shape
    g = p.reshape(b, rows, GRID_W, ch // 4, 4)
    left = jnp.pad(g[:, :, :-1, :, 0], ((0, 0), (0, 0), (1, 0), (0, 0)))
    right = jnp.pad(g[:, :, 1:, :, 1], ((0, 0), (0, 0), (0, 1), (0, 0)))
    up = jnp.pad(g[:, :-1, :, :, 2], ((0, 0), (1, 0), (0, 0), (0, 0)))
    down = jnp.pad(g[:, 1:, :, :, 3], ((0, 0), (0, 1), (0, 0), (0, 0)))
    return jnp.stack([left, right, up, down], axis=-1).reshape(b, n, ch)


def _bishift_seq(p):
    b, n, ch = p.shape
    g = p.reshape(b, n, ch // 2, 2)
    prev = jnp.pad(g[:, :-1, :, 0], ((0, 0), (1, 0), (0, 0)))
    nxt = jnp.pad(g[:, 1:, :, 1], ((0, 0), (0, 1), (0, 0)))
    return jnp.stack([prev, nxt], axis=-1).reshape(b, n, ch)


def _rwkv_prepare(pm, w0, w2, a0, a2, g2, k_k, k_a):
    b, n = pm.shape[:2]
    r, k, v, lw, la, lg = jnp.split(pm, RWKV_SPLITS, axis=-1)
    lw = lw.reshape(b, n, N_DIR, DECAY_LORA)
    la = la.reshape(b, n, N_DIR, AAA_LORA)
    w = w0 + jnp.einsum('bldr,drc->bldc', jnp.tanh(lw), w2)
    decay = jnp.exp(-jnp.exp(-jax.nn.softplus(-w) - 0.5))
    a = jax.nn.sigmoid(a0 + jnp.einsum('bldr,drc->bldc', la, a2))
    g = jax.nn.sigmoid(lg) @ g2
    kk = _heads(k * k_k, RWKV_HEADS).astype(jnp.float32)
    kk = (kk / jnp.maximum(jnp.sqrt(jnp.sum(jnp.square(kk), axis=-1, keepdims=True)), 1e-12)).astype(k.dtype)
    k_dir = k[:, :, None, :] * (1.0 + (a - 1.0) * k_a)
    return r, v, kk, decay, a, k_dir, g


def _wkv7_scan(r, w, k, v, kk, a, s0, reverse, emit):
    xs = tuple(jnp.swapaxes(z, 0, 1) for z in (r, w, k, v, kk, a))

    def step(S, inp):
        r_t, w_t, k_t, v_t, kk_t, a_t = inp
        s_kk = jnp.einsum('bhvk,bhk->bhv', S, kk_t)
        S = (S * w_t[:, :, None, :] - s_kk[..., None] * (kk_t * a_t)[:, :, None, :]
             + v_t[..., None] * k_t[:, :, None, :])
        return S, (jnp.einsum('bhvk,bhk->bhv', S, r_t) if emit else None)

    S, ys = lax.scan(step, s0, xs, reverse=reverse)
    return S, (jnp.swapaxes(ys, 0, 1) if emit else None)


def _wkv7_direction(stream, d, s0, reverse, emit):
    r, v, kk, decay, a, k_dir, _ = stream
    H = RWKV_HEADS
    return _wkv7_scan(_heads(r, H), _heads(decay[:, :, d], H), _heads(k_dir[:, :, d], H),
                      _heads(v, H), kk, _heads(a[:, :, d], H), s0, reverse, emit)


def _rwkv_bidir(lat, ctx, emit_ctx):
    b = lat[0].shape[0]
    s0 = jnp.zeros((b, RWKV_HEADS, RWKV_HEAD, RWKV_HEAD), jnp.float32)
    y, yc = 0.0, 0.0
    for d in range(N_DIR):
        rev = d == 1
        s_ctx, y_c_d = _wkv7_direction(ctx, d, s0, rev, emit_ctx)
        _, y_d = _wkv7_direction(lat, d, s_ctx, rev, True)
        y = y + y_d
        if emit_ctx:
            yc = yc + y_c_d
    return y, (yc if emit_ctx else None)


def _rwkv_output(stream, y, r_k, gn_g, gn_b):
    r, v, _, _, _, k_dir, g = stream
    rh = _heads(r, RWKV_HEADS)
    kh = _heads(k_dir, RWKV_HEADS)
    bonus = jnp.sum(rh[:, :, None] * kh * _heads(r_k, RWKV_HEADS), axis=-1, keepdims=True)
    bonus = jnp.sum(bonus, axis=2) * _heads(v, RWKV_HEADS)
    o = _head_norm(y, gn_g, gn_b, RWKV_GN_EPS) + bonus.reshape(bonus.shape[:-2] + (D_RWKV,))
    return o * g


def _rope(z, ang):
    cos = jnp.cos(ang)[None, :, None, :].astype(z.dtype)
    sin = jnp.sin(ang)[None, :, None, :].astype(z.dtype)
    z1, z2 = jnp.split(z, 2, axis=-1)
    return jnp.concatenate([z1 * cos - z2 * sin, z1 * sin + z2 * cos], axis=-1)


def _axial_rope(z):
    n, dh = z.shape[1], z.shape[-1]
    nf = dh // 4
    inv = ROPE_BASE ** (-jnp.arange(nf, dtype=jnp.float32) / nf)
    t = jnp.arange(n)
    ang_row = (t // GRID_W).astype(jnp.float32)[:, None] * inv
    ang_col = (t % GRID_W).astype(jnp.float32)[:, None] * inv
    z_row, z_col = jnp.split(z, 2, axis=-1)
    return jnp.concatenate([_rope(z_row, ang_row), _rope(z_col, ang_col)], axis=-1)


def _retention_dir(q, k, v, log_gamma, s0, include_diag, emit):
    b, n_tok, H, dh = k.shape
    n = n_tok // RET_CHUNK
    kc = k.reshape(b, n, RET_CHUNK, H, dh)
    vc = v.reshape(b, n, RET_CHUNK, H, dh)
    j = jnp.arange(RET_CHUNK, dtype=jnp.float32)
    tail = jnp.exp(log_gamma[None, :] * (RET_CHUNK - 1 - j)[:, None])
    U = jnp.einsum('bnjhd,bnjhe->nbhde', kc * tail[:, :, None], vc)
    g_chunk = jnp.exp(log_gamma * RET_CHUNK)[None, :, None, None]

    def step(S, U_n):
        return S * g_chunk + U_n, S

    S_final, S_start = lax.scan(step, s0, U)
    if not emit:
        return S_final, None
    qc = q.reshape(b, n, RET_CHUNK, H, dh)
    rel = j[:, None] - j[None, :]
    mask = rel >= 0 if include_diag else rel > 0
    dmat = jnp.where(mask[None], jnp.exp(log_gamma[:, None, None] * jnp.maximum(rel, 0.0)[None]), 0.0)
    scores = jnp.einsum('bnihd,bnjhd->bnhij', qc, kc) * dmat
    inner = jnp.einsum('bnhij,bnjhe->bnihe', scores, vc)
    head = jnp.exp(log_gamma[None, :] * (j + 1.0)[:, None])
    cross = jnp.einsum('bnihd,nbhde->bnihe', qc * head[:, :, None], S_start)
    return S_final, (inner + cross).reshape(b, n_tok, H, dh)


def _retention_bidir(q, k, v, qc, kc, vc, emit_ctx, ret_decay):
    b = q.shape[0]
    scale = RET_HEAD ** -0.5
    q = _axial_rope(_heads(q, RET_HEADS))
    k = _axial_rope(_heads(k, RET_HEADS)) * scale
    v = _heads(v, RET_HEADS)
    qc = _heads(qc, RET_HEADS)
    kc = _heads(kc, RET_HEADS) * scale
    vc = _heads(vc, RET_HEADS)
    log_gamma = -jax.nn.softplus(ret_decay)
    s0 = jnp.zeros((b, RET_HEADS, RET_HEAD, RET_HEAD), jnp.float32)
    y, yc = 0.0, 0.0
    for d in range(N_DIR):
        rev = d == 1
        s_ctx, y_c_d = _retention_dir(_flip_time(qc, rev), _flip_time(kc, rev), _flip_time(vc, rev),
                                      log_gamma[d], s0, not rev, emit_ctx)
        _, y_d = _retention_dir(_flip_time(q, rev), _flip_time(k, rev), _flip_time(v, rev),
                                log_gamma[d], s_ctx, not rev, True)
        y = y + _flip_time(y_d, rev)
        if emit_ctx:
            yc = yc + _flip_time(y_c_d, rev)
    return y, (yc if emit_ctx else None)


def _token_mixer(u, uc, rows, emit_ctx, w_in, mu, w0, w2, a0, a2, g2, k_k, k_a, r_k,
                 rwkv_gn_g, rwkv_gn_b, ret_decay, ret_gn_g, ret_gn_b, w_out):
    p = u @ w_in
    pc = uc @ w_in
    pr, pt = p[..., :RWKV_COLS], p[..., RWKV_COLS:]
    prc, ptc = pc[..., :RWKV_COLS], pc[..., RWKV_COLS:]
    pr = pr + mu * (_qshift_grid(pr, rows) - pr)
    prc = prc + mu * (_bishift_seq(prc) - prc)
    lat = _rwkv_prepare(pr, w0, w2, a0, a2, g2, k_k, k_a)
    ctx = _rwkv_prepare(prc, w0, w2, a0, a2, g2, k_k, k_a)
    y_rwkv, yc_rwkv = _rwkv_bidir(lat, ctx, emit_ctx)
    q, k, v, gt = jnp.split(pt, 4, axis=-1)
    qc, kc, vc, gtc = jnp.split(ptc, 4, axis=-1)
    y_ret, yc_ret = _retention_bidir(q, k, v, qc, kc, vc, emit_ctx, ret_decay)
    o_rwkv = _rwkv_output(lat, y_rwkv, r_k, rwkv_gn_g, rwkv_gn_b)
    o_ret = _head_norm(y_ret, ret_gn_g, ret_gn_b, RET_GN_EPS) * jax.nn.silu(gt)
    out = jnp.concatenate([o_rwkv, o_ret], axis=-1) @ w_out
    if not emit_ctx:
        return out, None
    oc_rwkv = _rwkv_output(ctx, yc_rwkv, r_k, rwkv_gn_g, rwkv_gn_b)
    oc_ret = _head_norm(yc_ret, ret_gn_g, ret_gn_b, RET_GN_EPS) * jax.nn.silu(gtc)
    return out, jnp.concatenate([oc_rwkv, oc_ret], axis=-1) @ w_out


def _grouped_experts(u, expert_ids, gates, w_gate, w_up, w_down):
    n_tok, d = u.shape
    n_assign = n_tok * TOP_K_IN_GROUP
    n_blk = -(-n_assign // MOE_BLOCK) + N_EXPERTS
    n_slot = n_blk * MOE_BLOCK
    flat_e = expert_ids.reshape(-1).astype(jnp.int32)
    flat_gate = gates.reshape(-1)
    flat_tok = jnp.repeat(jnp.arange(n_tok, dtype=jnp.int32), TOP_K_IN_GROUP)
    order = jnp.argsort(flat_e)
    e_sorted = flat_e[order]
    counts = jax.ops.segment_sum(jnp.ones_like(flat_e), flat_e, num_segments=N_EXPERTS)
    padded = ((counts + MOE_BLOCK - 1) // MOE_BLOCK) * MOE_BLOCK
    starts = jnp.cumsum(counts) - counts
    pends = jnp.cumsum(padded)
    pstarts = pends - padded
    dest = pstarts[e_sorted] + (jnp.arange(n_assign, dtype=jnp.int32) - starts[e_sorted])
    slot_tok = jnp.zeros((n_slot,), jnp.int32).at[dest].set(flat_tok[order])
    slot_gate = jnp.zeros((n_slot,), gates.dtype).at[dest].set(flat_gate[order])
    block_start = jnp.arange(n_blk, dtype=jnp.int32) * MOE_BLOCK
    block_expert = jnp.minimum(jnp.searchsorted(pends, block_start, side='right'), N_EXPERTS - 1)

    def run_block(blk):
        toks, gts, e = blk
        h = u[toks]
        y = (jax.nn.silu(h @ w_gate[e]) * (h @ w_up[e])) @ w_down[e]
        return y * gts[:, None]

    ys = lax.map(run_block, (slot_tok.reshape(n_blk, MOE_BLOCK), slot_gate.reshape(n_blk, MOE_BLOCK), block_expert))
    return jnp.zeros_like(u).at[slot_tok].add(ys.reshape(n_slot, d))


def _hier_moe(u, router_g, bias_g, router_e, bias_e, w_gate, w_up, w_down):
    n_tok = u.shape[0]
    gp = jax.nn.softmax((u @ router_g + bias_g).astype(jnp.float32), axis=-1)
    g_w, g_i = lax.top_k(gp, 1)
    el = (u @ router_e).reshape(n_tok, N_GROUPS, EXPERTS_PER_GROUP) + bias_e
    el = jnp.take_along_axis(el, g_i[:, :, None], axis=1)[:, 0].astype(jnp.float32)
    ep = jax.nn.softmax(el, axis=-1)
    top_p, top_i = lax.top_k(ep, TOP_K_IN_GROUP)
    gates = g_w * top_p / jnp.sum(top_p, axis=-1, keepdims=True)
    expert_ids = g_i * EXPERTS_PER_GROUP + top_i
    return _grouped_experts(u, expert_ids, gates.astype(u.dtype), w_gate, w_up, w_down)


def setup_inputs(seed: int = 0) -> dict:
    key = jax.random.key(seed)
    ks = jax.random.split(key, 40)
    f32 = jnp.float32
    L = DEPTH

    def nrm(k, shape, scale):
        return jax.random.normal(k, shape, f32) * scale

    base_gamma = 1.0 - 2.0 ** (-5.0 - jnp.arange(RET_HEADS, dtype=f32))
    neg_log_gamma = -jnp.log(base_gamma) * (1.0 + nrm(ks[20], (L, N_DIR, RET_HEADS), 0.05))
    decay_base = jnp.linspace(-6.5, -1.5, D_RWKV, dtype=f32)
    return {
        'x': nrm(ks[0], (BATCH, SEQ, D_MODEL), 1.0),
        'c': nrm(ks[1], (BATCH, D_MODEL), 1.0),
        'ctx': nrm(ks[2], (BATCH, CTX_LEN, D_MODEL), 1.0),
        'c_ctx': nrm(ks[3], (D_MODEL,), 1.0),
        'emb_ln_g': 1.0 + nrm(ks[4], (D_MODEL,), 0.02),
        'emb_ln_b': nrm(ks[5], (D_MODEL,), 0.02),
        'w_mod': nrm(ks[6], (L, D_MODEL, 6 * D_MODEL), 0.5 * D_MODEL ** -0.5),
        'b_mod': nrm(ks[7], (L, 6 * D_MODEL), 0.02),
        'w_in': nrm(ks[8], (L, D_MODEL, IN_COLS), D_MODEL ** -0.5),
        'tshift_mu': jax.random.uniform(ks[9], (L, RWKV_COLS), f32, 0.0, 1.0),
        'rwkv_w0': decay_base + nrm(ks[10], (L, N_DIR, D_RWKV), 0.1),
        'rwkv_w2': nrm(ks[11], (L, N_DIR, DECAY_LORA, D_RWKV), 0.5 * DECAY_LORA ** -0.5),
        'rwkv_a0': nrm(ks[12], (L, N_DIR, D_RWKV), 0.1),
        'rwkv_a2': nrm(ks[13], (L, N_DIR, AAA_LORA, D_RWKV), 0.5 * AAA_LORA ** -0.5),
        'rwkv_g2': nrm(ks[14], (L, GATE_LORA, D_RWKV), GATE_LORA ** -0.5),
        'rwkv_k_k': 0.85 + nrm(ks[15], (L, D_RWKV), 0.02),
        'rwkv_k_a': 1.0 + nrm(ks[16], (L, D_RWKV), 0.02),
        'rwkv_r_k': nrm(ks[17], (L, D_RWKV), 0.1),
        'rwkv_gn_g': 1.0 + nrm(ks[18], (L, D_RWKV), 0.02),
        'rwkv_gn_b': nrm(ks[19], (L, D_RWKV), 0.02),
        'ret_decay': jnp.log(jnp.expm1(neg_log_gamma)),
        'ret_gn_g': 1.0 + nrm(ks[21], (L, D_RET), 0.02),
        'ret_gn_b': nrm(ks[22], (L, D_RET), 0.02),
        'w_out': nrm(ks[23], (L, D_MODEL, D_MODEL), DEEPNORM_BETA * D_MODEL ** -0.5),
        'ln1_g': 1.0 + nrm(ks[24], (L, D_MODEL), 0.02),
        'ln1_b': nrm(ks[25], (L, D_MODEL), 0.02),
        'router_group': nrm(ks[26], (L, D_MODEL, N_GROUPS), D_MODEL ** -0.5),
        'router_group_bias': nrm(ks[27], (L, N_GROUPS), 0.01),
        'router_expert': nrm(ks[28], (L, D_MODEL, N_EXPERTS), D_MODEL ** -0.5),
        'router_expert_bias': nrm(ks[29], (L, N_GROUPS, EXPERTS_PER_GROUP), 0.01),
        'expert_w_gate': nrm(ks[30], (L, N_EXPERTS, D_MODEL, EXPERT_HIDDEN), D_MODEL ** -0.5),
        'expert_w_up': nrm(ks[31], (L, N_EXPERTS, D_MODEL, EXPERT_HIDDEN), D_MODEL ** -0.5),
        'expert_w_down': nrm(ks[32], (L, N_EXPERTS, EXPERT_HIDDEN, D_MODEL), DEEPNORM_BETA * EXPERT_HIDDEN ** -0.5),
        'ln2_g': 1.0 + nrm(ks[33], (L, D_MODEL), 0.02),
        'ln2_b': nrm(ks[34], (L, D_MODEL), 0.02),
    }


def reference(x, c, ctx, c_ctx, emb_ln_g, emb_ln_b, w_mod, b_mod, w_in, tshift_mu,
              rwkv_w0, rwkv_w2, rwkv_a0, rwkv_a2, rwkv_g2, rwkv_k_k, rwkv_k_a, rwkv_r_k,
              rwkv_gn_g, rwkv_gn_b, ret_decay, ret_gn_g, ret_gn_b, w_out, ln1_g, ln1_b,
              router_group, router_group_bias, router_expert, router_expert_bias,
              expert_w_gate, expert_w_up, expert_w_down, ln2_g, ln2_b):
    b, n_tok, d = x.shape
    n_ctx = ctx.shape[1]
    rows = n_tok // GRID_W
    h = _layer_norm(x, emb_ln_g, emb_ln_b)
    hc = _layer_norm(ctx, emb_ln_g, emb_ln_b)
    sc = jax.nn.silu(c)
    scc = jax.nn.silu(c_ctx)
    for l in range(DEPTH):
        emit_ctx = l < DEPTH - 1
        mod = sc @ w_mod[l] + b_mod[l]
        modc = scc @ w_mod[l] + b_mod[l]
        sh1, s1, g1, sh2, s2, g2 = jnp.split(mod[:, None, :], 6, axis=-1)
        sh1c, s1c, g1c, sh2c, s2c, g2c = jnp.split(modc, 6)
        u = h * (1.0 + s1) + sh1
        uc = hc * (1.0 + s1c) + sh1c
        mix, mixc = _token_mixer(u, uc, rows, emit_ctx, w_in[l], tshift_mu[l], rwkv_w0[l], rwkv_w2[l],
                                 rwkv_a0[l], rwkv_a2[l], rwkv_g2[l], rwkv_k_k[l], rwkv_k_a[l], rwkv_r_k[l],
                                 rwkv_gn_g[l], rwkv_gn_b[l], ret_decay[l], ret_gn_g[l], ret_gn_b[l], w_out[l])
        h = _layer_norm(DEEPNORM_ALPHA * h + g1 * mix, ln1_g[l], ln1_b[l])
        u2 = (h * (1.0 + s2) + sh2).reshape(b * n_tok, d)
        moe_params = (router_group[l], router_group_bias[l], router_expert[l], router_expert_bias[l],
                      expert_w_gate[l], expert_w_up[l], expert_w_down[l])
        if emit_ctx:
            hc = _layer_norm(DEEPNORM_ALPHA * hc + g1c * mixc, ln1_g[l], ln1_b[l])
            u2c = (hc * (1.0 + s2c) + sh2c).reshape(b * n_ctx, d)
            f_all = _hier_moe(jnp.concatenate([u2, u2c], axis=0), *moe_params)
            f = f_all[:b * n_tok].reshape(b, n_tok, d)
            fc = f_all[b * n_tok:].reshape(b, n_ctx, d)
            hc = _layer_norm(DEEPNORM_ALPHA * hc + g2c * fc, ln2_g[l], ln2_b[l])
        else:
            f = _hier_moe(u2, *moe_params).reshape(b, n_tok, d)
        h = _layer_norm(DEEPNORM_ALPHA * h + g2 * f, ln2_g[l], ln2_b[l])
    return h
```

```python
import functools
import math

import jax
import jax.numpy as jnp
import numpy as np
from jax import lax
from jax.experimental import pallas as pl
from jax.experimental.pallas import tpu as pltpu

F32 = jnp.float32
BF16 = jnp.bfloat16
HIGHEST = lax.Precision.HIGHEST

GRID_W = 64
D_RWKV = 512
RWKV_HEAD = 64
RWKV_HEADS = D_RWKV // RWKV_HEAD
DECAY_LORA = 64
AAA_LORA = 64
GATE_LORA = 128
D_RET = 512
RET_HEADS = 4
RET_HEAD = D_RET // RET_HEADS
RET_CHUNK = 128
RWKV_COLS = 3 * D_RWKV + 2 * (DECAY_LORA + AAA_LORA) + GATE_LORA
RET_COLS = 4 * D_RET
N_GROUPS = 4
EXPERTS_PER_GROUP = 8
N_EXPERTS = N_GROUPS * EXPERTS_PER_GROUP
EXPERT_HIDDEN = 512
MOE_BLOCK = 128
ROPE_BASE = 10000.0
LN_EPS = 1e-5
RWKV_GN_EPS = 64e-5
RET_GN_EPS = 1e-5
DEEPNORM_ALPHA = 2.0 ** 0.25
EXP_NEG_HALF = math.exp(-0.5)

LANES = 128
SUBLANES = 8
VMEM_LIMIT_BYTES = 56 * 1024 * 1024

SCAN_VGROUPS = 4
SCAN_VROWS = RWKV_HEAD // SCAN_VGROUPS


def _cparams(sem):
    return pltpu.CompilerParams(dimension_semantics=sem, vmem_limit_bytes=VMEM_LIMIT_BYTES)


def _layer_norm(x, g, b, eps=LN_EPS):
    mu = jnp.mean(x, axis=-1, keepdims=True)
    xc = x - mu
    var = jnp.mean(xc * xc, axis=-1, keepdims=True)
    return xc * lax.rsqrt(var + eps) * g + b


def _sigmoid(x):
    return 1.0 / (1.0 + jnp.exp(-x))


def _segsum(x, ones_bf16):
    hi = x.astype(BF16)
    r1 = x - hi.astype(F32)
    mid = r1.astype(BF16)
    lo = (r1 - mid.astype(F32)).astype(BF16)
    acc = jnp.dot(hi, ones_bf16, preferred_element_type=F32)
    acc = acc + jnp.dot(mid, ones_bf16, preferred_element_type=F32)
    return acc + jnp.dot(lo, ones_bf16, preferred_element_type=F32)


def _segment_ones(width, seg):
    idx = np.arange(width) // seg
    return jnp.asarray(idx[:, None] == idx[None, :], dtype=BF16)


def _mod_kernel(c_ref, w_ref, b_ref, o_ref):
    c = c_ref[...]
    sc = c * _sigmoid(c)
    o_ref[...] = jnp.dot(sc, w_ref[...], precision=HIGHEST, preferred_element_type=F32) + b_ref[...]


def _modulation(c_rows, w_mod, b_mod):
    rows, d = c_rows.shape
    n = w_mod.shape[1]
    tn = 1536
    return pl.pallas_call(
        _mod_kernel,
        grid=(n // tn,),
        in_specs=[pl.BlockSpec((rows, d), lambda j: (0, 0)),
                  pl.BlockSpec((d, tn), lambda j: (0, j)),
                  pl.BlockSpec((1, tn), lambda j: (0, j))],
        out_specs=pl.BlockSpec((rows, tn), lambda j: (0, j)),
        out_shape=jax.ShapeDtypeStruct((rows, n), F32),
        compiler_params=_cparams(("arbitrary",)),
        name="modulation",
    )(c_rows, w_mod, b_mod)


def _in_proj_kernel(x_ref, g_ref, b_ref, s_ref, sh_ref, w_ref, pr_ref, pt_ref):
    h = _layer_norm(x_ref[0], g_ref[...], b_ref[...])
    u = h * (1.0 + s_ref[0]) + sh_ref[0]
    p = jnp.dot(u.astype(BF16), w_ref[...], preferred_element_type=F32)
    pr_ref[0] = p[:, :RWKV_COLS]
    pt_ref[0] = p[:, RWKV_COLS:]


def _in_proj(x, ln_g, ln_b, s1, sh1, w_in_bf16):
    b, n, d = x.shape
    tm = 256
    cols = w_in_bf16.shape[1]
    return pl.pallas_call(
        _in_proj_kernel,
        grid=(b, n // tm),
        in_specs=[pl.BlockSpec((1, tm, d), lambda bi, i: (bi, i, 0)),
                  pl.BlockSpec((1, d), lambda bi, i: (0, 0)),
                  pl.BlockSpec((1, d), lambda bi, i: (0, 0)),
                  pl.BlockSpec((1, 1, d), lambda bi, i: (bi, 0, 0)),
                  pl.BlockSpec((1, 1, d), lambda bi, i: (bi, 0, 0)),
                  pl.BlockSpec((d, cols), lambda bi, i: (0, 0))],
        out_specs=[pl.BlockSpec((1, tm, RWKV_COLS), lambda bi, i: (bi, i, 0)),
                   pl.BlockSpec((1, tm, RET_COLS), lambda bi, i: (bi, i, 0))],
        out_shape=[jax.ShapeDtypeStruct((b, n, RWKV_COLS), F32),
                   jax.ShapeDtypeStruct((b, n, RET_COLS), F32)],
        compiler_params=_cparams(("arbitrary", "arbitrary")),
        name="in_proj",
    )(x, ln_g, ln_b, s1, sh1, w_in_bf16)


def _rwkv_prepare_kernel(cur_ref, prev_ref, next_ref, mu_ref, w0_ref, w2_ref, a0_ref, a2_ref, g2_ref,
                         kk_scale_ref, ka_ref, rk_ref, ones_ref,
                         r_ref, v_ref, kk_ref, w_ref, kd_ref, bb_ref, g_ref, bonus_ref,
                         *, grid_shift, n_tok):
    cur = cur_ref[0]
    t, c = cur.shape
    row = lax.broadcasted_iota(jnp.int32, (t, c), 0)
    lane = lax.broadcasted_iota(jnp.int32, (t, c), 1)
    prev_tok = pltpu.roll(cur, 1, 0)
    next_tok = pltpu.roll(cur, t - 1, 0)
    if grid_shift:
        col = row & (GRID_W - 1)
        tok = row + pl.program_id(1) * t
        left = jnp.where(col > 0, prev_tok, 0.0)
        right = jnp.where(col < GRID_W - 1, next_tok, 0.0)
        up = jnp.where(tok >= GRID_W, jnp.concatenate([prev_ref[0], cur[:t - GRID_W]], axis=0), 0.0)
        down = jnp.where(tok < n_tok - GRID_W, jnp.concatenate([cur[GRID_W:], next_ref[0]], axis=0), 0.0)
        cm = lane & 3
        shifted = jnp.where(cm == 0, left, jnp.where(cm == 1, right, jnp.where(cm == 2, up, down)))
    else:
        prev_tok = jnp.where(row > 0, prev_tok, 0.0)
        next_tok = jnp.where(row < t - 1, next_tok, 0.0)
        shifted = jnp.where((lane & 1) == 0, prev_tok, next_tok)
    pm = cur + mu_ref[...] * (shifted - cur)

    r = pm[:, 0:D_RWKV]
    k = pm[:, D_RWKV:2 * D_RWKV]
    v = pm[:, 2 * D_RWKV:3 * D_RWKV]
    o = 3 * D_RWKV
    lw = pm[:, o:o + 2 * DECAY_LORA]
    la = pm[:, o + 2 * DECAY_LORA:o + 2 * (DECAY_LORA + AAA_LORA)]
    lg = pm[:, o + 2 * (DECAY_LORA + AAA_LORA):]

    w = w0_ref[...] + jnp.dot(jnp.tanh(lw), w2_ref[...], precision=HIGHEST, preferred_element_type=F32)
    decay = jnp.exp(-EXP_NEG_HALF * _sigmoid(w))
    a = _sigmoid(a0_ref[...] + jnp.dot(la, a2_ref[...], precision=HIGHEST, preferred_element_type=F32))
    gate = jnp.dot(_sigmoid(lg), g2_ref[...], precision=HIGHEST, preferred_element_type=F32)

    ones = ones_ref[...]
    kk_raw = k * kk_scale_ref[...]
    kk = kk_raw / jnp.maximum(jnp.sqrt(_segsum(kk_raw * kk_raw, ones)), 1e-12)
    ka = ka_ref[...]
    a0 = a[:, :D_RWKV]
    a1 = a[:, D_RWKV:]
    kd0 = k * (1.0 + (a0 - 1.0) * ka)
    kd1 = k * (1.0 + (a1 - 1.0) * ka)
    bonus = _segsum(r * (kd0 + kd1) * rk_ref[...], ones) * v

    r_ref[0] = r
    v_ref[0] = v
    kk_ref[0] = kk
    w_ref[0] = decay
    kd_ref[0] = jnp.concatenate([kd0, kd1], axis=-1)
    bb_ref[0] = jnp.concatenate([kk * a0, kk * a1], axis=-1)
    g_ref[0] = gate
    bonus_ref[0] = bonus


def _rwkv_prepare(pr, params, grid_shift):
    b, n, c = pr.shape
    t = 256
    if not grid_shift:
        assert n == t, "sequence token shift is written for a single tile"
    halo_blocks = n // GRID_W
    per_tile = t // GRID_W
    small = lambda shape: pl.BlockSpec(shape, lambda bi, i: (0,) * len(shape))
    tok_spec = lambda width: pl.BlockSpec((1, t, width), lambda bi, i: (bi, i, 0))
    out_widths = (D_RWKV, D_RWKV, D_RWKV, 2 * D_RWKV, 2 * D_RWKV, 2 * D_RWKV, D_RWKV, D_RWKV)
    kernel = functools.partial(_rwkv_prepare_kernel, grid_shift=grid_shift, n_tok=n)
    return pl.pallas_call(
        kernel,
        grid=(b, n // t),
        in_specs=[tok_spec(c),
                  pl.BlockSpec((1, GRID_W, c), lambda bi, i: (bi, jnp.maximum(i * per_tile - 1, 0), 0)),
                  pl.BlockSpec((1, GRID_W, c),
                               lambda bi, i: (bi, jnp.minimum((i + 1) * per_tile, halo_blocks - 1), 0)),
                  small((1, c)), small((1, 2 * D_RWKV)), small((2 * DECAY_LORA, 2 * D_RWKV)),
                  small((1, 2 * D_RWKV)), small((2 * AAA_LORA, 2 * D_RWKV)), small((GATE_LORA, D_RWKV)),
                  small((1, D_RWKV)), small((1, D_RWKV)), small((1, D_RWKV)), small((D_RWKV, D_RWKV))],
        out_specs=[tok_spec(wd) for wd in out_widths],
        out_shape=[jax.ShapeDtypeStruct((b, n, wd), F32) for wd in out_widths],
        compiler_params=_cparams(("arbitrary", "arbitrary")),
        name="rwkv_prepare",
    )(pr, pr, pr, *params)


def _wkv7_scan_kernel(kk_ref, w_ref, b_ref, k_ref, r_ref, v_ref, y_ref, state_ref):
    @pl.when(pl.program_id(0) == 0)
    def _():
        state_ref[...] = jnp.zeros_like(state_ref)

    def step(t, carry):
        kk = kk_ref[t]
        w = w_ref[t]
        b = b_ref[t]
        k = k_ref[t]
        r = r_ref[t]
        for vr in range(SCAN_VROWS):
            s = state_ref[vr]
            s_kk = jnp.sum(s * kk, axis=0, keepdims=True)
            s_new = s * w - s_kk * b + v_ref[t, pl.ds(vr, 1), :] * k
            state_ref[vr] = s_new
            y_ref[t, pl.ds(vr, 1), :] = jnp.sum(s_new * r, axis=0, keepdims=True)
        return carry

    lax.fori_loop(0, kk_ref.shape[0], step, 0)


def _wkv7_scan(kk, w, b, k, r, v):
    steps = kk.shape[0]
    tb = 64
    kspec = pl.BlockSpec((tb, RWKV_HEAD, LANES), lambda i: (i, 0, 0))
    vspec = pl.BlockSpec((tb, SCAN_VROWS, LANES), lambda i: (i, 0, 0))
    return pl.pallas_call(
        _wkv7_scan_kernel,
        grid=(steps // tb,),
        in_specs=[kspec, kspec, kspec, kspec, kspec, vspec],
        out_specs=vspec,
        out_shape=jax.ShapeDtypeStruct((steps, SCAN_VROWS, LANES), F32),
        scratch_shapes=[pltpu.VMEM((SCAN_VROWS, RWKV_HEAD, LANES), F32)],
        compiler_params=_cparams(("arbitrary",)),
        name="wkv7_scan",
    )(kk, w, b, k, r, v)


def _to_scan_lanes(x_dirs):
    d, b, s, _ = x_dirs.shape
    x = x_dirs.reshape(d, b, s, RWKV_HEADS, RWKV_HEAD)
    x = jnp.transpose(x, (2, 4, 0, 1, 3)).reshape(s, RWKV_HEAD, d * b * RWKV_HEADS)
    return jnp.tile(x, (1, 1, SCAN_VGROUPS))


def _v_to_scan_lanes(v_dirs):
    d, b, s, _ = v_dirs.shape
    x = v_dirs.reshape(d, b, s, RWKV_HEADS, SCAN_VGROUPS, SCAN_VROWS)
    return jnp.transpose(x, (2, 5, 4, 0, 1, 3)).reshape(s, SCAN_VROWS, SCAN_VGROUPS * d * b * RWKV_HEADS)


def _y_from_scan_lanes(y, d, b):
    s = y.shape[0]
    x = y.reshape(s, SCAN_VROWS, SCAN_VGROUPS, d, b, RWKV_HEADS)
    return jnp.transpose(x, (3, 4, 0, 5, 2, 1)).reshape(d, b, s, D_RWKV)


def _step_order(x_ctx, x_lat):
    fwd = jnp.concatenate([x_ctx, x_lat], axis=1)
    bwd = jnp.concatenate([x_ctx[:, ::-1], x_lat[:, ::-1]], axis=1)
    return fwd, bwd


def _rope(z, cos_t, sin_t):
    lane = lax.broadcasted_iota(jnp.int32, z.shape, 1)
    half = RET_HEAD // 4
    partner = jnp.where((lane & (2 * half - 1)) < half, pltpu.roll(z, RET_HEAD - half, 1), pltpu.roll(z, half, 1))
    return z * cos_t + partner * sin_t


def _retention_kernel(dec_ref, fwd_ref, bwd_ref, ctx_ref, cosf_ref, sinf_ref, cosb_ref, sinb_ref,
                      yf_ref, yb_ref, state_ref):
    c = RET_CHUNK
    scale = RET_HEAD ** -0.5
    ii = lax.broadcasted_iota(jnp.int32, (c, c), 0)
    jj = lax.broadcasted_iota(jnp.int32, (c, c), 1)
    pos = lax.broadcasted_iota(jnp.int32, (c, RET_HEAD), 0).astype(F32)
    n_ctx_chunks = ctx_ref.shape[1] // c

    def head_slices(ref_val, h):
        q = ref_val[:, h * RET_HEAD:(h + 1) * RET_HEAD]
        k = ref_val[:, D_RET + h * RET_HEAD:D_RET + (h + 1) * RET_HEAD]
        v = ref_val[:, 2 * D_RET + h * RET_HEAD:2 * D_RET + (h + 1) * RET_HEAD]
        return q, k, v

    for d in range(2):
        blk = fwd_ref[0] if d == 0 else bwd_ref[0]
        cos_t = cosf_ref[...] if d == 0 else cosb_ref[...]
        sin_t = sinf_ref[...] if d == 0 else sinb_ref[...]
        for h in range(RET_HEADS):
            x = jnp.full((1, RET_HEAD), dec_ref[d, h], F32)
            lg = -(jnp.maximum(x, 0.0) + jnp.log(1.0 + jnp.exp(-jnp.abs(x))))
            chunk_decay = jnp.exp(lg * float(c))
            tail = jnp.exp(lg * ((c - 1.0 - pos) if d == 0 else pos))
            head = jnp.exp(lg * ((pos + 1.0) if d == 0 else (c - pos)))

            @pl.when(pl.program_id(1) == 0)
            def _():
                s = jnp.zeros((RET_HEAD, RET_HEAD), F32)
                order = range(n_ctx_chunks) if d == 0 else range(n_ctx_chunks - 1, -1, -1)
                for cc in order:
                    _, kc, vc = head_slices(ctx_ref[0, cc * c:(cc + 1) * c, :], h)
                    kw = (kc * scale * tail).astype(BF16)
                    u = lax.dot_general(kw, vc.astype(BF16), (((0,), (0,)), ((), ())), preferred_element_type=F32)
                    s = s * chunk_decay + u
                state_ref[d, h] = s

            q, k, v = head_slices(blk, h)
            q = _rope(q, cos_t, sin_t)
            k = _rope(k, cos_t, sin_t) * scale
            s = state_ref[d, h]
            scores = lax.dot_general(q.astype(BF16), k.astype(BF16), (((1,), (1,)), ((), ())),
                                     preferred_element_type=F32)
            rel = (ii - jj) if d == 0 else (jj - ii)
            mask = (rel >= 0) if d == 0 else (rel > 0)
            dmat = jnp.where(mask, jnp.exp(lg * jnp.maximum(rel, 0).astype(F32)), 0.0)
            v_bf = v.astype(BF16)
            inner = jnp.dot((scores * dmat).astype(BF16), v_bf, preferred_element_type=F32)
            cross = jnp.dot((q * head).astype(BF16), s.astype(BF16), preferred_element_type=F32)
            u = lax.dot_general((k * tail).astype(BF16), v_bf, (((0,), (0,)), ((), ())), preferred_element_type=F32)
            state_ref[d, h] = s * chunk_decay + u
            out_ref = yf_ref if d == 0 else yb_ref
            out_ref[0, :, h * RET_HEAD:(h + 1) * RET_HEAD] = inner + cross


def _retention(pt, pt_ctx, ret_decay, cos_t, sin_t):
    b, n, _ = pt.shape
    c = RET_CHUNK
    nc = n // c
    qkv = 3 * D_RET
    fwd = lambda bi, i: (bi, i, 0)
    bwd = lambda bi, i: (bi, nc - 1 - i, 0)
    return pl.pallas_call(
        _retention_kernel,
        grid=(b, nc),
        in_specs=[pl.BlockSpec(memory_space=pltpu.SMEM),
                  pl.BlockSpec((1, c, qkv), fwd),
                  pl.BlockSpec((1, c, qkv), bwd),
                  pl.BlockSpec((1, pt_ctx.shape[1], qkv), lambda bi, i: (bi, 0, 0)),
                  pl.BlockSpec((c, RET_HEAD), lambda bi, i: (i, 0)),
                  pl.BlockSpec((c, RET_HEAD), lambda bi, i: (i, 0)),
                  pl.BlockSpec((c, RET_HEAD), lambda bi, i: (nc - 1 - i, 0)),
                  pl.BlockSpec((c, RET_HEAD), lambda bi, i: (nc - 1 - i, 0))],
        out_specs=[pl.BlockSpec((1, c, D_RET), fwd), pl.BlockSpec((1, c, D_RET), bwd)],
        out_shape=[jax.ShapeDtypeStruct((b, n, D_RET), F32), jax.ShapeDtypeStruct((b, n, D_RET), F32)],
        scratch_shapes=[pltpu.VMEM((2, RET_HEADS, RET_HEAD, RET_HEAD), F32)],
        compiler_params=_cparams(("arbitrary", "arbitrary")),
        name="retention",
    )(ret_decay, pt, pt, pt_ctx, cos_t, sin_t, cos_t, sin_t)


def _rope_tables(n_tok):
    nf = RET_HEAD // 4
    lane = np.arange(RET_HEAD)
    inv = ROPE_BASE ** (-jnp.arange(nf, dtype=F32) / nf)
    t = jnp.arange(n_tok)
    pos = jnp.where((lane // (2 * nf) == 0)[None, :], (t // GRID_W)[:, None], (t % GRID_W)[:, None]).astype(F32)
    ang = pos * inv[lane % nf][None, :]
    sign = jnp.where((lane % (2 * nf)) < nf, -1.0, 1.0).astype(F32)
    return jnp.cos(ang), jnp.sin(ang) * sign[None, :]


def _group_norm(y, ones, seg, eps, g, b):
    mu = _segsum(y, ones) * (1.0 / seg)
    yc = y - mu
    var = _segsum(yc * yc, ones) * (1.0 / seg)
    return yc * lax.rsqrt(var + eps) * g + b


def _out_proj_kernel(x_ref, yf_ref, yb_ref, bonus_ref, gate_ref, tf_ref, tb_ref, gt_ref,
                     embg_ref, embb_ref, g1_ref, s2_ref, sh2_ref, rgn_g_ref, rgn_b_ref, tgn_g_ref, tgn_b_ref,
                     ones_r_ref, ones_t_ref, wout_ref, ln1g_ref, ln1b_ref, wr_ref, br_ref,
                     h1_ref, u2_ref, logit_ref):
    y = yf_ref[0] + yb_ref[0]
    o_rwkv = _group_norm(y, ones_r_ref[...], RWKV_HEAD, RWKV_GN_EPS, rgn_g_ref[...], rgn_b_ref[...])
    o_rwkv = (o_rwkv + bonus_ref[0]) * gate_ref[0]
    yt = tf_ref[0] + tb_ref[0]
    gt = gt_ref[0]
    o_ret = _group_norm(yt, ones_t_ref[...], RET_HEAD, RET_GN_EPS, tgn_g_ref[...], tgn_b_ref[...])
    o_ret = o_ret * (gt * _sigmoid(gt))
    cat = jnp.concatenate([o_rwkv, o_ret], axis=-1).astype(BF16)
    mix = jnp.dot(cat, wout_ref[...], preferred_element_type=F32)
    h = _layer_norm(x_ref[0], embg_ref[...], embb_ref[...])
    h1 = _layer_norm(DEEPNORM_ALPHA * h + g1_ref[0] * mix, ln1g_ref[...], ln1b_ref[...])
    u2 = h1 * (1.0 + s2_ref[0]) + sh2_ref[0]
    h1_ref[0] = h1
    u2_ref[0] = u2
    logit_ref[0] = jnp.dot(u2, wr_ref[...], precision=HIGHEST, preferred_element_type=F32) + br_ref[...]


def _out_proj(x, y_f, y_b, bonus, gate, t_f, t_b, pt, vecs, mats):
    b, n, d = x.shape
    t = 256
    tok = lambda width: pl.BlockSpec((1, t, width), lambda bi, i: (bi, i, 0))
    per_b = pl.BlockSpec((1, 1, d), lambda bi, i: (bi, 0, 0))
    small = lambda arr: pl.BlockSpec(arr.shape, lambda bi, i: (0,) * arr.ndim)
    (embg, embb, g1, s2, sh2, rgn_g, rgn_b, tgn_g, tgn_b, ln1g, ln1b, br) = vecs
    (ones_r, ones_t, wout, wr) = mats
    gt_spec = pl.BlockSpec((1, t, D_RET), lambda bi, i: (bi, i, 3))
    args = (x, y_f, y_b, bonus, gate, t_f, t_b, pt, embg, embb, g1, s2, sh2, rgn_g, rgn_b, tgn_g, tgn_b,
            ones_r, ones_t, wout, ln1g, ln1b, wr, br)
    in_specs = [tok(d)] + [tok(D_RWKV)] * 6 + [gt_spec, small(embg), small(embb), per_b, per_b, per_b,
                                                small(rgn_g), small(rgn_b), small(tgn_g), small(tgn_b),
                                                small(ones_r), small(ones_t), small(wout), small(ln1g),
                                                small(ln1b), small(wr), small(br)]
    return pl.pallas_call(
        _out_proj_kernel,
        grid=(b, n // t),
        in_specs=in_specs,
        out_specs=[tok(d), tok(d), tok(LANES)],
        out_shape=[jax.ShapeDtypeStruct((b, n, d), F32), jax.ShapeDtypeStruct((b, n, d), F32),
                   jax.ShapeDtypeStruct((b, n, LANES), F32)],
        compiler_params=_cparams(("arbitrary", "arbitrary")),
        name="out_proj",
    )(*args)


ROUTE_E1, ROUTE_E2, ROUTE_G1, ROUTE_G2, ROUTE_RANK1, ROUTE_RANK2 = range(6)


def _lane_argmax(x, valid, lane):
    m = jnp.max(jnp.where(valid, x, -jnp.inf), axis=-1, keepdims=True)
    idx = jnp.min(jnp.where(valid & (x == m), lane, float(LANES)), axis=-1, keepdims=True)
    return m, idx


def _route_kernel(logit_ref, route_ref, count_ref, carry_ref):
    @pl.when(pl.program_id(0) == 0)
    def _():
        carry_ref[...] = jnp.zeros_like(carry_ref)

    lg = logit_ref[...]
    t = lg.shape[0]
    lane = lax.broadcasted_iota(jnp.int32, lg.shape, 1).astype(F32)
    gmask = lane < N_GROUPS
    gmax = jnp.max(jnp.where(gmask, lg, -jnp.inf), axis=-1, keepdims=True)
    gexp = jnp.where(gmask, jnp.exp(lg - gmax), 0.0)
    gp = gexp / jnp.sum(gexp, axis=-1, keepdims=True)
    g_w, g_i = _lane_argmax(gp, gmask, lane)

    lo = N_GROUPS + EXPERTS_PER_GROUP * g_i
    emask = (lane >= lo) & (lane < lo + EXPERTS_PER_GROUP)
    emax = jnp.max(jnp.where(emask, lg, -jnp.inf), axis=-1, keepdims=True)
    eexp = jnp.where(emask, jnp.exp(lg - emax), 0.0)
    ep = eexp / jnp.sum(eexp, axis=-1, keepdims=True)
    p1, i1 = _lane_argmax(ep, emask, lane)
    p2, i2 = _lane_argmax(ep, emask & (lane != i1), lane)
    denom = p1 + p2
    gate1 = g_w * p1 / denom
    gate2 = g_w * p2 / denom
    e1 = i1 - N_GROUPS
    e2 = i2 - N_GROUPS

    oh1 = (lane == e1).astype(F32)
    oh2 = (lane == e2).astype(F32)
    cnt = oh1 + oh2
    ri = lax.broadcasted_iota(jnp.int32, (t, t), 0)
    ci = lax.broadcasted_iota(jnp.int32, (t, t), 1)
    before = (ci < ri).astype(BF16)
    seen = jnp.dot(before, cnt.astype(BF16), preferred_element_type=F32) + carry_ref[0:1, :]
    rank1 = jnp.sum(oh1 * seen, axis=-1, keepdims=True)
    rank2 = jnp.sum(oh2 * seen, axis=-1, keepdims=True)
    carry_ref[0:1, :] = carry_ref[0:1, :] + jnp.sum(cnt, axis=0, keepdims=True)

    out = jnp.zeros(lg.shape, F32)
    for slot, val in ((ROUTE_E1, e1.astype(F32)), (ROUTE_E2, e2.astype(F32)), (ROUTE_G1, gate1),
                      (ROUTE_G2, gate2), (ROUTE_RANK1, rank1), (ROUTE_RANK2, rank2)):
        out = jnp.where(lane == slot, val, out)
    route_ref[...] = out
    count_ref[...] = carry_ref[...]


def _route(logits):
    n = logits.shape[0]
    t = 256
    return pl.pallas_call(
        _route_kernel,
        grid=(n // t,),
        in_specs=[pl.BlockSpec((t, LANES), lambda i: (i, 0))],
        out_specs=[pl.BlockSpec((t, LANES), lambda i: (i, 0)), pl.BlockSpec((SUBLANES, LANES), lambda i: (0, 0))],
        out_shape=[jax.ShapeDtypeStruct((n, LANES), F32), jax.ShapeDtypeStruct((SUBLANES, LANES), F32)],
        scratch_shapes=[pltpu.VMEM((SUBLANES, LANES), F32)],
        compiler_params=_cparams(("arbitrary",)),
        name="route",
    )(logits)


def _row_gather_copy(src_hbm, idx_ref, buf, sem, slot, r):
    return pltpu.make_async_copy(src_hbm.at[pl.ds(idx_ref[0, 0, r], 1), :], buf.at[slot, pl.ds(r, 1), :], sem.at[slot])


def _start_row_gather(src_hbm, idx_ref, buf, sem, slot, rows):
    def body(r, carry):
        _row_gather_copy(src_hbm, idx_ref, buf, sem, slot, r).start()
        return carry
    lax.fori_loop(0, rows, body, 0)


def _wait_row_gather(src_hbm, idx_ref, buf, sem, slot, rows):
    def body(r, carry):
        _row_gather_copy(src_hbm, idx_ref, buf, sem, slot, r).wait()
        return carry
    lax.fori_loop(0, rows, body, 0)


def _expert_kernel(blk_expert_ref, tok_ref, tok_next_ref, u_hbm, wg_ref, wu_ref, wd_ref, y_ref, xbuf, sem):
    i = pl.program_id(0)
    n = pl.num_programs(0)
    slot = i % 2

    @pl.when(i == 0)
    def _():
        _start_row_gather(u_hbm, tok_ref, xbuf, sem, 0, MOE_BLOCK)

    @pl.when(i + 1 < n)
    def _():
        _start_row_gather(u_hbm, tok_next_ref, xbuf, sem, 1 - slot, MOE_BLOCK)

    _wait_row_gather(u_hbm, tok_ref, xbuf, sem, slot, MOE_BLOCK)
    x = xbuf[slot].astype(BF16)
    hg = jnp.dot(x, wg_ref[0].astype(BF16), preferred_element_type=F32)
    hu = jnp.dot(x, wu_ref[0].astype(BF16), preferred_element_type=F32)
    act = (hg * _sigmoid(hg) * hu).astype(BF16)
    y_ref[...] = jnp.dot(act, wd_ref[0].astype(BF16), preferred_element_type=F32)


def _expert_mlp(u2, slot_tok, block_expert, w_gate, w_up, w_down):
    n_blk = block_expert.shape[0]
    d = u2.shape[1]
    hdim = w_gate.shape[2]
    tok3 = slot_tok.reshape(n_blk, 1, MOE_BLOCK)
    grid_spec = pltpu.PrefetchScalarGridSpec(
        num_scalar_prefetch=1,
        grid=(n_blk,),
        in_specs=[pl.BlockSpec((1, 1, MOE_BLOCK), lambda i, be: (i, 0, 0), memory_space=pltpu.SMEM),
                  pl.BlockSpec((1, 1, MOE_BLOCK), lambda i, be: (jnp.minimum(i + 1, n_blk - 1), 0, 0),
                               memory_space=pltpu.SMEM),
                  pl.BlockSpec(memory_space=pl.ANY),
                  pl.BlockSpec((1, d, hdim), lambda i, be: (be[i], 0, 0)),
                  pl.BlockSpec((1, d, hdim), lambda i, be: (be[i], 0, 0)),
                  pl.BlockSpec((1, hdim, d), lambda i, be: (be[i], 0, 0))],
        out_specs=pl.BlockSpec((MOE_BLOCK, d), lambda i, be: (i, 0)),
        scratch_shapes=[pltpu.VMEM((2, MOE_BLOCK, d), F32), pltpu.SemaphoreType.DMA((2,))],
    )
    return pl.pallas_call(
        _expert_kernel,
        grid_spec=grid_spec,
        out_shape=jax.ShapeDtypeStruct((n_blk * MOE_BLOCK, d), F32),
        compiler_params=_cparams(("arbitrary",)),
        name="expert_mlp",
    )(block_expert, tok3, tok3, u2, w_gate, w_up, w_down)


def _combine_kernel(d1_ref, d2_ref, d1n_ref, d2n_ref, y_hbm, route_ref, h1_ref, g2_ref, lng_ref, lnb_ref,
                    o_ref, abuf, bbuf, sem_a, sem_b):
    i = pl.program_id(0)
    n = pl.num_programs(0)
    slot = i % 2
    rows = o_ref.shape[0]

    @pl.when(i == 0)
    def _():
        _start_row_gather(y_hbm, d1_ref, abuf, sem_a, 0, rows)
        _start_row_gather(y_hbm, d2_ref, bbuf, sem_b, 0, rows)

    @pl.when(i + 1 < n)
    def _():
        _start_row_gather(y_hbm, d1n_ref, abuf, sem_a, 1 - slot, rows)
        _start_row_gather(y_hbm, d2n_ref, bbuf, sem_b, 1 - slot, rows)

    _wait_row_gather(y_hbm, d1_ref, abuf, sem_a, slot, rows)
    _wait_row_gather(y_hbm, d2_ref, bbuf, sem_b, slot, rows)
    route = route_ref[...]
    f = abuf[slot] * route[:, ROUTE_G1:ROUTE_G1 + 1] + bbuf[slot] * route[:, ROUTE_G2:ROUTE_G2 + 1]
    o_ref[...] = _layer_norm(DEEPNORM_ALPHA * h1_ref[...] + g2_ref[0] * f, lng_ref[...], lnb_ref[...])


def _combine(y_sorted, dest1, dest2, route, h1, g2, ln_g, ln_b, tokens_per_batch):
    n, d = h1.shape
    t = 128
    nt = n // t
    per_b = tokens_per_batch // t
    d1 = dest1.reshape(nt, 1, t)
    d2 = dest2.reshape(nt, 1, t)
    cur = pl.BlockSpec((1, 1, t), lambda i: (i, 0, 0), memory_space=pltpu.SMEM)
    nxt = pl.BlockSpec((1, 1, t), lambda i: (jnp.minimum(i + 1, nt - 1), 0, 0), memory_space=pltpu.SMEM)
    small = lambda arr: pl.BlockSpec(arr.shape, lambda i: (0,) * arr.ndim)
    return pl.pallas_call(
        _combine_kernel,
        grid=(nt,),
        in_specs=[cur, cur, nxt, nxt, pl.BlockSpec(memory_space=pl.ANY),
                  pl.BlockSpec((t, LANES), lambda i: (i, 0)),
                  pl.BlockSpec((t, d), lambda i: (i, 0)),
                  pl.BlockSpec((1, 1, d), lambda i: (i // per_b, 0, 0)),
                  small(ln_g), small(ln_b)],
        out_specs=pl.BlockSpec((t, d), lambda i: (i, 0)),
        out_shape=jax.ShapeDtypeStruct((n, d), F32),
        scratch_shapes=[pltpu.VMEM((2, t, d), F32), pltpu.VMEM((2, t, d), F32),
                        pltpu.SemaphoreType.DMA((2,)), pltpu.SemaphoreType.DMA((2,))],
        compiler_params=_cparams(("arbitrary",)),
        name="combine",
    )(d1, d2, d1, d2, y_sorted, route, h1, g2, ln_g, ln_b)


def _block_diag2(w):
    z = jnp.zeros_like(w[0])
    return jnp.concatenate([jnp.concatenate([w[0], z], axis=1), jnp.concatenate([z, w[1]], axis=1)], axis=0)


def kernel(x, c, ctx, c_ctx, emb_ln_g, emb_ln_b, w_mod, b_mod, w_in, tshift_mu, rwkv_w0, rwkv_w2, rwkv_a0, rwkv_a2, rwkv_g2, rwkv_k_k, rwkv_k_a, rwkv_r_k, rwkv_gn_g, rwkv_gn_b, ret_decay, ret_gn_g, ret_gn_b, w_out, ln1_g, ln1_b, router_group, router_group_bias, router_expert, router_expert_bias, expert_w_gate, expert_w_up, expert_w_down, ln2_g, ln2_b):
    assert w_mod.shape[0] == 1, "written for DEPTH == 1 (context outputs are never emitted)"
    b, n_tok, d = x.shape
    n_ctx = ctx.shape[1]
    row = lambda v: v.reshape(1, -1)

    c_rows = jnp.zeros((SUBLANES, d), F32).at[:b].set(c).at[b].set(c_ctx)
    mod = _modulation(c_rows, w_mod[0], row(b_mod[0]))
    sh1, s1, g1, sh2, s2, g2 = [mod[:b, j * d:(j + 1) * d].reshape(b, 1, d) for j in range(6)]
    sh1c, s1c = [jnp.broadcast_to(mod[b, j * d:(j + 1) * d].reshape(1, 1, d), (b, 1, d)) for j in range(2)]

    w_in_bf16 = w_in[0].astype(BF16)
    pr, pt = _in_proj(x, row(emb_ln_g), row(emb_ln_b), s1, sh1, w_in_bf16)
    pr_c, pt_c = _in_proj(ctx, row(emb_ln_g), row(emb_ln_b), s1c, sh1c, w_in_bf16)

    prep_params = (row(tshift_mu[0]), row(rwkv_w0[0]), _block_diag2(rwkv_w2[0]), row(rwkv_a0[0]),
                   _block_diag2(rwkv_a2[0]), rwkv_g2[0], row(rwkv_k_k[0]), row(rwkv_k_a[0]), row(rwkv_r_k[0]),
                   _segment_ones(D_RWKV, RWKV_HEAD))
    lat = _rwkv_prepare(pr, prep_params, grid_shift=True)
    cx = _rwkv_prepare(pr_c, prep_params, grid_shift=False)
    r_l, v_l, kk_l, w_l, kd_l, bb_l, gate_l, bonus_l = lat
    r_c, v_c, kk_c, w_c, kd_c, bb_c, _, _ = cx

    def shared(xc, xl):
        return jnp.stack(_step_order(xc, xl))

    def per_dir(xc, xl):
        f, _ = _step_order(xc[..., :D_RWKV], xl[..., :D_RWKV])
        _, bw = _step_order(xc[..., D_RWKV:], xl[..., D_RWKV:])
        return jnp.stack([f, bw])

    y_scan = _wkv7_scan(_to_scan_lanes(shared(kk_c, kk_l)), _to_scan_lanes(per_dir(w_c, w_l)),
                        _to_scan_lanes(per_dir(bb_c, bb_l)), _to_scan_lanes(per_dir(kd_c, kd_l)),
                        _to_scan_lanes(shared(r_c, r_l)), _v_to_scan_lanes(shared(v_c, v_l)))
    y_dirs = _y_from_scan_lanes(y_scan, 2, b)[:, :, n_ctx:]
    y_f = y_dirs[0]
    y_b = y_dirs[1][:, ::-1]

    cos_t, sin_t = _rope_tables(n_tok)
    t_f, t_b = _retention(pt, pt_c, ret_decay[0], cos_t, sin_t)

    wr = jnp.zeros((d, LANES), F32).at[:, :N_GROUPS].set(router_group[0])
    wr = wr.at[:, N_GROUPS:N_GROUPS + N_EXPERTS].set(router_expert[0])
    br = jnp.zeros((1, LANES), F32).at[0, :N_GROUPS].set(router_group_bias[0])
    br = br.at[0, N_GROUPS:N_GROUPS + N_EXPERTS].set(router_expert_bias[0].reshape(-1))
    vecs = (row(emb_ln_g), row(emb_ln_b), g1, s2, sh2, row(rwkv_gn_g[0]), row(rwkv_gn_b[0]),
            row(ret_gn_g[0]), row(ret_gn_b[0]), row(ln1_g[0]), row(ln1_b[0]), br)
    mats = (_segment_ones(D_RWKV, RWKV_HEAD), _segment_ones(D_RET, RET_HEAD), w_out[0].astype(BF16), wr)
    h1, u2, logits = _out_proj(x, y_f, y_b, bonus_l, gate_l, t_f, t_b, pt, vecs, mats)

    n_all = b * n_tok
    route, counts = _route(logits.reshape(n_all, LANES))

    e1 = route[:, ROUTE_E1].astype(jnp.int32)
    e2 = route[:, ROUTE_E2].astype(jnp.int32)
    cnt = counts[0, :N_EXPERTS].astype(jnp.int32)
    padded = ((cnt + MOE_BLOCK - 1) // MOE_BLOCK) * MOE_BLOCK
    pends = jnp.cumsum(padded)
    pstarts = pends - padded
    dest1 = pstarts[e1] + route[:, ROUTE_RANK1].astype(jnp.int32)
    dest2 = pstarts[e2] + route[:, ROUTE_RANK2].astype(jnp.int32)
    n_blk = -(-(n_all * 2) // MOE_BLOCK) + N_EXPERTS
    tok_ids = jnp.arange(n_all, dtype=jnp.int32)
    slot_tok = jnp.zeros((n_blk * MOE_BLOCK,), jnp.int32).at[dest1].set(tok_ids).at[dest2].set(tok_ids)
    block_start = jnp.arange(n_blk, dtype=jnp.int32) * MOE_BLOCK
    block_expert = jnp.minimum(jnp.searchsorted(pends, block_start, side='right'), N_EXPERTS - 1).astype(jnp.int32)

    y_sorted = _expert_mlp(u2.reshape(n_all, d), slot_tok, block_expert,
                           expert_w_gate[0], expert_w_up[0], expert_w_down[0])
    out = _combine(y_sorted, dest1, dest2, route, h1.reshape(n_all, d), g2, row(ln2_g[0]), row(ln2_b[0]), n_tok)
    return out.reshape(b, n_tok, d)
```

```python
import functools
import math

import jax
import jax.numpy as jnp
import numpy as np
from jax import lax
from jax.experimental import pallas as pl
from jax.experimental.pallas import tpu as pltpu

F32 = jnp.float32
BF16 = jnp.bfloat16
HIGHEST = lax.Precision.HIGHEST

GRID_W = 64
D_RWKV = 512
RWKV_HEAD = 64
RWKV_HEADS = D_RWKV // RWKV_HEAD
DECAY_LORA = 64
AAA_LORA = 64
GATE_LORA = 128
D_RET = 512
RET_HEADS = 4
RET_HEAD = D_RET // RET_HEADS
RET_CHUNK = 128
RWKV_COLS = 3 * D_RWKV + 2 * (DECAY_LORA + AAA_LORA) + GATE_LORA
RET_COLS = 4 * D_RET
N_GROUPS = 4
EXPERTS_PER_GROUP = 8
N_EXPERTS = N_GROUPS * EXPERTS_PER_GROUP
EXPERT_HIDDEN = 512
MOE_BLOCK = 128
ROPE_BASE = 10000.0
LN_EPS = 1e-5
RWKV_GN_EPS = 64e-5
RET_GN_EPS = 1e-5
DEEPNORM_ALPHA = 2.0 ** 0.25
EXP_NEG_HALF = math.exp(-0.5)

LANES = 128
SUBLANES = 8
VMEM_LIMIT_BYTES = 56 * 1024 * 1024

WKV_CHUNK = 64


def _cparams(sem):
    return pltpu.CompilerParams(dimension_semantics=sem, vmem_limit_bytes=VMEM_LIMIT_BYTES)


def _layer_norm(x, g, b, eps=LN_EPS):
    mu = jnp.mean(x, axis=-1, keepdims=True)
    xc = x - mu
    var = jnp.mean(xc * xc, axis=-1, keepdims=True)
    return xc * lax.rsqrt(var + eps) * g + b


def _sigmoid(x):
    return 1.0 / (1.0 + jnp.exp(-x))


def _segsum(x, ones_bf16):
    hi = x.astype(BF16)
    r1 = x - hi.astype(F32)
    mid = r1.astype(BF16)
    lo = (r1 - mid.astype(F32)).astype(BF16)
    acc = jnp.dot(hi, ones_bf16, preferred_element_type=F32)
    acc = acc + jnp.dot(mid, ones_bf16, preferred_element_type=F32)
    return acc + jnp.dot(lo, ones_bf16, preferred_element_type=F32)


def _segment_ones(width, seg):
    idx = np.arange(width) // seg
    return jnp.asarray(idx[:, None] == idx[None, :], dtype=BF16)


def _mod_kernel(c_ref, w_ref, b_ref, o_ref):
    c = c_ref[...]
    sc = c * _sigmoid(c)
    o_ref[...] = jnp.dot(sc, w_ref[...], precision=HIGHEST, preferred_element_type=F32) + b_ref[...]


def _modulation(c_rows, w_mod, b_mod):
    rows, d = c_rows.shape
    n = w_mod.shape[1]
    tn = 1536
    return pl.pallas_call(
        _mod_kernel,
        grid=(n // tn,),
        in_specs=[pl.BlockSpec((rows, d), lambda j: (0, 0)),
                  pl.BlockSpec((d, tn), lambda j: (0, j)),
                  pl.BlockSpec((1, tn), lambda j: (0, j))],
        out_specs=pl.BlockSpec((rows, tn), lambda j: (0, j)),
        out_shape=jax.ShapeDtypeStruct((rows, n), F32),
        compiler_params=_cparams(("arbitrary",)),
        name="modulation",
    )(c_rows, w_mod, b_mod)


def _in_proj_kernel(x_ref, g_ref, b_ref, s_ref, sh_ref, w_ref, pr_ref, pt_ref):
    h = _layer_norm(x_ref[0], g_ref[...], b_ref[...])
    u = h * (1.0 + s_ref[0]) + sh_ref[0]
    p = jnp.dot(u.astype(BF16), w_ref[...], preferred_element_type=F32)
    pr_ref[0] = p[:, :RWKV_COLS]
    pt_ref[0] = p[:, RWKV_COLS:]


def _in_proj(x, ln_g, ln_b, s1, sh1, w_in_bf16):
    b, n, d = x.shape
    tm = 256
    cols = w_in_bf16.shape[1]
    return pl.pallas_call(
        _in_proj_kernel,
        grid=(b, n // tm),
        in_specs=[pl.BlockSpec((1, tm, d), lambda bi, i: (bi, i, 0)),
                  pl.BlockSpec((1, d), lambda bi, i: (0, 0)),
                  pl.BlockSpec((1, d), lambda bi, i: (0, 0)),
                  pl.BlockSpec((1, 1, d), lambda bi, i: (bi, 0, 0)),
                  pl.BlockSpec((1, 1, d), lambda bi, i: (bi, 0, 0)),
                  pl.BlockSpec((d, cols), lambda bi, i: (0, 0))],
        out_specs=[pl.BlockSpec((1, tm, RWKV_COLS), lambda bi, i: (bi, i, 0)),
                   pl.BlockSpec((1, tm, RET_COLS), lambda bi, i: (bi, i, 0))],
        out_shape=[jax.ShapeDtypeStruct((b, n, RWKV_COLS), F32),
                   jax.ShapeDtypeStruct((b, n, RET_COLS), F32)],
        compiler_params=_cparams(("arbitrary", "arbitrary")),
        name="in_proj",
    )(x, ln_g, ln_b, s1, sh1, w_in_bf16)


def _rwkv_prepare_kernel(cur_ref, prev_ref, next_ref, mu_ref, w0_ref, w2_ref, a0_ref, a2_ref, g2_ref,
                         kk_scale_ref, ka_ref, rk_ref, ones_ref,
                         r_ref, v_ref, kk_ref, w_ref, kd_ref, bb_ref, g_ref, bonus_ref,
                         *, grid_shift, n_tok):
    cur = cur_ref[0]
    t, c = cur.shape
    row = lax.broadcasted_iota(jnp.int32, (t, c), 0)
    lane = lax.broadcasted_iota(jnp.int32, (t, c), 1)
    prev_tok = pltpu.roll(cur, 1, 0)
    next_tok = pltpu.roll(cur, t - 1, 0)
    if grid_shift:
        col = row & (GRID_W - 1)
        tok = row + pl.program_id(1) * t
        left = jnp.where(col > 0, prev_tok, 0.0)
        right = jnp.where(col < GRID_W - 1, next_tok, 0.0)
        up = jnp.where(tok >= GRID_W, jnp.concatenate([prev_ref[0], cur[:t - GRID_W]], axis=0), 0.0)
        down = jnp.where(tok < n_tok - GRID_W, jnp.concatenate([cur[GRID_W:], next_ref[0]], axis=0), 0.0)
        cm = lane & 3
        shifted = jnp.where(cm == 0, left, jnp.where(cm == 1, right, jnp.where(cm == 2, up, down)))
    else:
        prev_tok = jnp.where(row > 0, prev_tok, 0.0)
        next_tok = jnp.where(row < t - 1, next_tok, 0.0)
        shifted = jnp.where((lane & 1) == 0, prev_tok, next_tok)
    pm = cur + mu_ref[...] * (shifted - cur)

    r = pm[:, 0:D_RWKV]
    k = pm[:, D_RWKV:2 * D_RWKV]
    v = pm[:, 2 * D_RWKV:3 * D_RWKV]
    o = 3 * D_RWKV
    lw = pm[:, o:o + 2 * DECAY_LORA]
    la = pm[:, o + 2 * DECAY_LORA:o + 2 * (DECAY_LORA + AAA_LORA)]
    lg = pm[:, o + 2 * (DECAY_LORA + AAA_LORA):]

    w = w0_ref[...] + jnp.dot(jnp.tanh(lw), w2_ref[...], precision=HIGHEST, preferred_element_type=F32)
    log_decay = -EXP_NEG_HALF * _sigmoid(w)
    a = _sigmoid(a0_ref[...] + jnp.dot(la, a2_ref[...], precision=HIGHEST, preferred_element_type=F32))
    gate = jnp.dot(_sigmoid(lg), g2_ref[...], precision=HIGHEST, preferred_element_type=F32)

    ones = ones_ref[...]
    kk_raw = k * kk_scale_ref[...]
    kk = kk_raw / jnp.maximum(jnp.sqrt(_segsum(kk_raw * kk_raw, ones)), 1e-12)
    ka = ka_ref[...]
    a0 = a[:, :D_RWKV]
    a1 = a[:, D_RWKV:]
    kd0 = k * (1.0 + (a0 - 1.0) * ka)
    kd1 = k * (1.0 + (a1 - 1.0) * ka)
    bonus = _segsum(r * (kd0 + kd1) * rk_ref[...], ones) * v

    r_ref[0] = r
    v_ref[0] = v
    kk_ref[0] = kk
    w_ref[0] = log_decay
    kd_ref[0] = jnp.concatenate([kd0, kd1], axis=-1)
    bb_ref[0] = jnp.concatenate([kk * a0, kk * a1], axis=-1)
    g_ref[0] = gate
    bonus_ref[0] = bonus


def _rwkv_prepare(pr, params, grid_shift):
    b, n, c = pr.shape
    t = 256
    if not grid_shift:
        assert n == t, "sequence token shift is written for a single tile"
    halo_blocks = n // GRID_W
    per_tile = t // GRID_W
    small = lambda shape: pl.BlockSpec(shape, lambda bi, i: (0,) * len(shape))
    tok_spec = lambda width: pl.BlockSpec((1, t, width), lambda bi, i: (bi, i, 0))
    out_widths = (D_RWKV, D_RWKV, D_RWKV, 2 * D_RWKV, 2 * D_RWKV, 2 * D_RWKV, D_RWKV, D_RWKV)
    kernel = functools.partial(_rwkv_prepare_kernel, grid_shift=grid_shift, n_tok=n)
    return pl.pallas_call(
        kernel,
        grid=(b, n // t),
        in_specs=[tok_spec(c),
                  pl.BlockSpec((1, GRID_W, c), lambda bi, i: (bi, jnp.maximum(i * per_tile - 1, 0), 0)),
                  pl.BlockSpec((1, GRID_W, c),
                               lambda bi, i: (bi, jnp.minimum((i + 1) * per_tile, halo_blocks - 1), 0)),
                  small((1, c)), small((1, 2 * D_RWKV)), small((2 * DECAY_LORA, 2 * D_RWKV)),
                  small((1, 2 * D_RWKV)), small((2 * AAA_LORA, 2 * D_RWKV)), small((GATE_LORA, D_RWKV)),
                  small((1, D_RWKV)), small((1, D_RWKV)), small((1, D_RWKV)), small((D_RWKV, D_RWKV))],
        out_specs=[tok_spec(wd) for wd in out_widths],
        out_shape=[jax.ShapeDtypeStruct((b, n, wd), F32) for wd in out_widths],
        compiler_params=_cparams(("arbitrary", "arbitrary")),
        name="rwkv_prepare",
    )(pr, pr, pr, *params)


def _bdot(a, b):
    return jnp.dot(a.astype(BF16), b.astype(BF16), preferred_element_type=F32)


def _bdot_nt(a, b):
    return lax.dot_general(a.astype(BF16), b.astype(BF16), (((1,), (1,)), ((), ())), preferred_element_type=F32)


def _bdot_tn(a, b):
    return lax.dot_general(a.astype(BF16), b.astype(BF16), (((0,), (0,)), ((), ())), preferred_element_type=F32)


def _wkv7_chunk_kernel(r_l, v_l, kk_l, lw_l, kd_l, bb_l, r_c, v_c, kk_c, lw_c, kd_c, bb_c, y_ref, state_ref,
                       *, n_ctx_chunks):
    c = WKV_CHUNK
    d = pl.program_id(0)
    n = pl.program_id(2)
    sign = 1 - 2 * d

    @pl.when(n == 0)
    def _():
        state_ref[...] = jnp.zeros_like(state_ref)

    is_ctx = n < n_ctx_chunks
    pick = lambda xc, xl: jnp.where(is_ctx, xc[0], xl[0])
    r = pick(r_c, r_l)
    v = pick(v_c, v_l)
    kk = pick(kk_c, kk_l)
    lw = pick(lw_c, lw_l)
    kd = pick(kd_c, kd_l)
    bb = pick(bb_c, bb_l)

    ti = lax.broadcasted_iota(jnp.int32, (c, c), 0)
    tj = lax.broadcasted_iota(jnp.int32, (c, c), 1)
    upto = ((tj - ti) * sign <= 0).astype(BF16)
    hi = lw.astype(BF16)
    r1 = lw - hi.astype(F32)
    mid = r1.astype(BF16)
    lo = (r1 - mid.astype(F32)).astype(BF16)
    cum = (jnp.dot(upto, hi, preferred_element_type=F32) + jnp.dot(upto, mid, preferred_element_type=F32)
           + jnp.dot(upto, lo, preferred_element_type=F32))
    tot = jnp.sum(lw, axis=0, keepdims=True)
    e_neg = jnp.exp(-cum)
    e_rem = jnp.exp(tot - cum)
    alpha = kk * jnp.exp(cum - lw)
    rho = r * jnp.exp(cum)
    beta = bb * e_neg
    kappa = kd * e_neg
    kappa_rem = kd * e_rem
    beta_rem = bb * e_rem
    g_chunk = jnp.exp(tot)

    p = 2 * c
    ri = lax.broadcasted_iota(jnp.int32, (p, p), 0)
    ci = lax.broadcasted_iota(jnp.int32, (p, p), 1)
    same_head = (ri >= c) == (ci >= c)
    ii = ri & (c - 1)
    jj = ci & (c - 1)
    earlier = same_head & ((jj - ii) * sign < 0)
    upto_self = earlier | (ri == ci)
    eye = (ri == ci).astype(F32)
    lane = lax.broadcasted_iota(jnp.int32, (c, p), 1)
    first = lane < RWKV_HEAD

    def stack(x):
        return jnp.concatenate([jnp.where(first, x, 0.0), jnp.where(first, 0.0, x)], axis=0)

    def unstack(x):
        return x[:c] + x[c:]

    for hp in range(RWKV_HEADS // 2):
        sl = slice(hp * p, (hp + 1) * p)
        a_st, r_st, k_st, b_st = stack(alpha[:, sl]), stack(rho[:, sl]), stack(kappa[:, sl]), stack(beta[:, sl])
        k2_st, b2_st, v_st = stack(kappa_rem[:, sl]), stack(beta_rem[:, sl]), stack(v[:, sl])

        g = _bdot_nt(jnp.concatenate([a_st, r_st], axis=0), jnp.concatenate([k_st, b_st], axis=0))
        m1 = jnp.where(earlier, g[:p, :p], 0.0)
        m2 = jnp.where(earlier, g[:p, p:], 0.0)
        n1 = jnp.where(upto_self, g[p:, :p], 0.0)
        n2 = jnp.where(upto_self, g[p:, p:], 0.0)

        pw = -jnp.where((ii >> 3) == (jj >> 3), m2, 0.0)
        inv = eye + pw
        pw = _bdot(pw, pw)
        inv = inv + _bdot(inv, pw)
        pw = _bdot(pw, pw)
        inv = inv + _bdot(inv, pw)
        for sh in (3, 4, 5):
            off = ((ii >> (sh + 1)) == (jj >> (sh + 1))) & ((ii >> sh) != (jj >> sh))
            inv = inv - _bdot(_bdot(inv, jnp.where(off, m2, 0.0)), inv)

        au = _bdot(inv, jnp.concatenate([a_st, _bdot(m1, v_st)], axis=1))
        a_t, u_t = au[:, :p], au[:, p:]
        nn = _bdot(n2, au)
        rho_t = unstack(r_st - nn[:, :p])
        y_t = unstack(_bdot(n1, v_st) - nn[:, p:])
        pc = _bdot_tn(b2_st, a_t)
        qc_t = _bdot_tn(jnp.concatenate([v_st, -u_t], axis=0), jnp.concatenate([k2_st, b2_st], axis=0))

        s0 = state_ref[hp]
        y_ref[0, 0, :, sl] = _bdot_nt(rho_t, s0) + y_t
        state_ref[hp] = s0 * g_chunk[:, sl] - _bdot_nt(s0, pc) + qc_t


def _wkv7(lat, ctx, b, n_tok, n_ctx):
    c = WKV_CHUNK
    ncx = n_ctx // c
    nl = n_tok // c
    lat_idx = lambda d, n: jnp.where(d == 0, jnp.maximum(n - ncx, 0), nl - 1 - jnp.maximum(n - ncx, 0))
    ctx_idx = lambda d, n: jnp.where(d == 0, jnp.minimum(n, ncx - 1), ncx - 1 - jnp.minimum(n, ncx - 1))
    shared = lambda idx: pl.BlockSpec((1, c, D_RWKV), lambda d, bi, n: (bi, idx(d, n), 0))
    per_dir = lambda idx: pl.BlockSpec((1, c, D_RWKV), lambda d, bi, n: (bi, idx(d, n), d))
    specs = lambda idx: [shared(idx), shared(idx), shared(idx), per_dir(idx), per_dir(idx), per_dir(idx)]
    return pl.pallas_call(
        functools.partial(_wkv7_chunk_kernel, n_ctx_chunks=ncx),
        grid=(2, b, ncx + nl),
        in_specs=specs(lat_idx) + specs(ctx_idx),
        out_specs=pl.BlockSpec((1, 1, c, D_RWKV), lambda d, bi, n: (d, bi, lat_idx(d, n), 0)),
        out_shape=jax.ShapeDtypeStruct((2, b, n_tok, D_RWKV), F32),
        scratch_shapes=[pltpu.VMEM((RWKV_HEADS // 2, 2 * RWKV_HEAD, 2 * RWKV_HEAD), F32)],
        compiler_params=_cparams(("arbitrary", "arbitrary", "arbitrary")),
        name="wkv7_chunk",
    )(*lat, *ctx)


def _rope(z, cos_t, sin_t):
    lane = lax.broadcasted_iota(jnp.int32, z.shape, 1)
    half = RET_HEAD // 4
    partner = jnp.where((lane & (2 * half - 1)) < half, pltpu.roll(z, RET_HEAD - half, 1), pltpu.roll(z, half, 1))
    return z * cos_t + partner * sin_t


def _retention_kernel(dec_ref, fwd_ref, bwd_ref, ctx_ref, cosf_ref, sinf_ref, cosb_ref, sinb_ref,
                      yf_ref, yb_ref, state_ref):
    c = RET_CHUNK
    scale = RET_HEAD ** -0.5
    ii = lax.broadcasted_iota(jnp.int32, (c, c), 0)
    jj = lax.broadcasted_iota(jnp.int32, (c, c), 1)
    pos = lax.broadcasted_iota(jnp.int32, (c, RET_HEAD), 0).astype(F32)
    n_ctx_chunks = ctx_ref.shape[1] // c

    def head_slices(ref_val, h):
        q = ref_val[:, h * RET_HEAD:(h + 1) * RET_HEAD]
        k = ref_val[:, D_RET + h * RET_HEAD:D_RET + (h + 1) * RET_HEAD]
        v = ref_val[:, 2 * D_RET + h * RET_HEAD:2 * D_RET + (h + 1) * RET_HEAD]
        return q, k, v

    for d in range(2):
        blk = fwd_ref[0] if d == 0 else bwd_ref[0]
        cos_t = cosf_ref[...] if d == 0 else cosb_ref[...]
        sin_t = sinf_ref[...] if d == 0 else sinb_ref[...]
        for h in range(RET_HEADS):
            x = jnp.full((1, RET_HEAD), dec_ref[d, h], F32)
            lg = -(jnp.maximum(x, 0.0) + jnp.log(1.0 + jnp.exp(-jnp.abs(x))))
            chunk_decay = jnp.exp(lg * float(c))
            tail = jnp.exp(lg * ((c - 1.0 - pos) if d == 0 else pos))
            head = jnp.exp(lg * ((pos + 1.0) if d == 0 else (c - pos)))

            @pl.when(pl.program_id(1) == 0)
            def _():
                s = jnp.zeros((RET_HEAD, RET_HEAD), F32)
                order = range(n_ctx_chunks) if d == 0 else range(n_ctx_chunks - 1, -1, -1)
                for cc in order:
                    _, kc, vc = head_slices(ctx_ref[0, cc * c:(cc + 1) * c, :], h)
                    kw = (kc * scale * tail).astype(BF16)
                    u = lax.dot_general(kw, vc.astype(BF16), (((0,), (0,)), ((), ())), preferred_element_type=F32)
                    s = s * chunk_decay + u
                state_ref[d, h] = s

            q, k, v = head_slices(blk, h)
            q = _rope(q, cos_t, sin_t)
            k = _rope(k, cos_t, sin_t) * scale
            s = state_ref[d, h]
            scores = lax.dot_general(q.astype(BF16), k.astype(BF16), (((1,), (1,)), ((), ())),
                                     preferred_element_type=F32)
            rel = (ii - jj) if d == 0 else (jj - ii)
            mask = (rel >= 0) if d == 0 else (rel > 0)
            dmat = jnp.where(mask, jnp.exp(lg * jnp.maximum(rel, 0).astype(F32)), 0.0)
            v_bf = v.astype(BF16)
            inner = jnp.dot((scores * dmat).astype(BF16), v_bf, preferred_element_type=F32)
            cross = jnp.dot((q * head).astype(BF16), s.astype(BF16), preferred_element_type=F32)
            u = lax.dot_general((k * tail).astype(BF16), v_bf, (((0,), (0,)), ((), ())), preferred_element_type=F32)
            state_ref[d, h] = s * chunk_decay + u
            out_ref = yf_ref if d == 0 else yb_ref
            out_ref[0, :, h * RET_HEAD:(h + 1) * RET_HEAD] = inner + cross


def _retention(pt, pt_ctx, ret_decay, cos_t, sin_t):
    b, n, _ = pt.shape
    c = RET_CHUNK
    nc = n // c
    qkv = 3 * D_RET
    fwd = lambda bi, i: (bi, i, 0)
    bwd = lambda bi, i: (bi, nc - 1 - i, 0)
    return pl.pallas_call(
        _retention_kernel,
        grid=(b, nc),
        in_specs=[pl.BlockSpec(memory_space=pltpu.SMEM),
                  pl.BlockSpec((1, c, qkv), fwd),
                  pl.BlockSpec((1, c, qkv), bwd),
                  pl.BlockSpec((1, pt_ctx.shape[1], qkv), lambda bi, i: (bi, 0, 0)),
                  pl.BlockSpec((c, RET_HEAD), lambda bi, i: (i, 0)),
                  pl.BlockSpec((c, RET_HEAD), lambda bi, i: (i, 0)),
                  pl.BlockSpec((c, RET_HEAD), lambda bi, i: (nc - 1 - i, 0)),
                  pl.BlockSpec((c, RET_HEAD), lambda bi, i: (nc - 1 - i, 0))],
        out_specs=[pl.BlockSpec((1, c, D_RET), fwd), pl.BlockSpec((1, c, D_RET), bwd)],
        out_shape=[jax.ShapeDtypeStruct((b, n, D_RET), F32), jax.ShapeDtypeStruct((b, n, D_RET), F32)],
        scratch_shapes=[pltpu.VMEM((2, RET_HEADS, RET_HEAD, RET_HEAD), F32)],
        compiler_params=_cparams(("arbitrary", "arbitrary")),
        name="retention",
    )(ret_decay, pt, pt, pt_ctx, cos_t, sin_t, cos_t, sin_t)


def _rope_tables(n_tok):
    nf = RET_HEAD // 4
    lane = np.arange(RET_HEAD)
    inv = ROPE_BASE ** (-jnp.arange(nf, dtype=F32) / nf)
    t = jnp.arange(n_tok)
    pos = jnp.where((lane // (2 * nf) == 0)[None, :], (t // GRID_W)[:, None], (t % GRID_W)[:, None]).astype(F32)
    ang = pos * inv[lane % nf][None, :]
    sign = jnp.where((lane % (2 * nf)) < nf, -1.0, 1.0).astype(F32)
    return jnp.cos(ang), jnp.sin(ang) * sign[None, :]


def _group_norm(y, ones, seg, eps, g, b):
    mu = _segsum(y, ones) * (1.0 / seg)
    yc = y - mu
    var = _segsum(yc * yc, ones) * (1.0 / seg)
    return yc * lax.rsqrt(var + eps) * g + b


def _out_proj_kernel(x_ref, yf_ref, yb_ref, bonus_ref, gate_ref, tf_ref, tb_ref, gt_ref,
                     embg_ref, embb_ref, g1_ref, s2_ref, sh2_ref, rgn_g_ref, rgn_b_ref, tgn_g_ref, tgn_b_ref,
                     ones_r_ref, ones_t_ref, wout_ref, ln1g_ref, ln1b_ref, wr_ref, br_ref,
                     h1_ref, u2_ref, logit_ref):
    y = yf_ref[0] + yb_ref[0]
    o_rwkv = _group_norm(y, ones_r_ref[...], RWKV_HEAD, RWKV_GN_EPS, rgn_g_ref[...], rgn_b_ref[...])
    o_rwkv = (o_rwkv + bonus_ref[0]) * gate_ref[0]
    yt = tf_ref[0] + tb_ref[0]
    gt = gt_ref[0]
    o_ret = _group_norm(yt, ones_t_ref[...], RET_HEAD, RET_GN_EPS, tgn_g_ref[...], tgn_b_ref[...])
    o_ret = o_ret * (gt * _sigmoid(gt))
    cat = jnp.concatenate([o_rwkv, o_ret], axis=-1).astype(BF16)
    mix = jnp.dot(cat, wout_ref[...], preferred_element_type=F32)
    h = _layer_norm(x_ref[0], embg_ref[...], embb_ref[...])
    h1 = _layer_norm(DEEPNORM_ALPHA * h + g1_ref[0] * mix, ln1g_ref[...], ln1b_ref[...])
    u2 = h1 * (1.0 + s2_ref[0]) + sh2_ref[0]
    h1_ref[0] = h1
    u2_ref[0] = u2
    logit_ref[0] = jnp.dot(u2, wr_ref[...], precision=HIGHEST, preferred_element_type=F32) + br_ref[...]


def _out_proj(x, y_f, y_b, bonus, gate, t_f, t_b, pt, vecs, mats):
    b, n, d = x.shape
    t = 256
    tok = lambda width: pl.BlockSpec((1, t, width), lambda bi, i: (bi, i, 0))
    per_b = pl.BlockSpec((1, 1, d), lambda bi, i: (bi, 0, 0))
    small = lambda arr: pl.BlockSpec(arr.shape, lambda bi, i: (0,) * arr.ndim)
    (embg, embb, g1, s2, sh2, rgn_g, rgn_b, tgn_g, tgn_b, ln1g, ln1b, br) = vecs
    (ones_r, ones_t, wout, wr) = mats
    gt_spec = pl.BlockSpec((1, t, D_RET), lambda bi, i: (bi, i, 3))
    args = (x, y_f, y_b, bonus, gate, t_f, t_b, pt, embg, embb, g1, s2, sh2, rgn_g, rgn_b, tgn_g, tgn_b,
            ones_r, ones_t, wout, ln1g, ln1b, wr, br)
    in_specs = [tok(d)] + [tok(D_RWKV)] * 6 + [gt_spec, small(embg), small(embb), per_b, per_b, per_b,
                                                small(rgn_g), small(rgn_b), small(tgn_g), small(tgn_b),
                                                small(ones_r), small(ones_t), small(wout), small(ln1g),
                                                small(ln1b), small(wr), small(br)]
    return pl.pallas_call(
        _out_proj_kernel,
        grid=(b, n // t),
        in_specs=in_specs,
        out_specs=[tok(d), tok(d), tok(LANES)],
        out_shape=[jax.ShapeDtypeStruct((b, n, d), F32), jax.ShapeDtypeStruct((b, n, d), F32),
                   jax.ShapeDtypeStruct((b, n, LANES), F32)],
        compiler_params=_cparams(("arbitrary", "arbitrary")),
        name="out_proj",
    )(*args)


ROUTE_E1, ROUTE_E2, ROUTE_G1, ROUTE_G2, ROUTE_RANK1, ROUTE_RANK2 = range(6)


def _lane_argmax(x, valid, lane):
    m = jnp.max(jnp.where(valid, x, -jnp.inf), axis=-1, keepdims=True)
    idx = jnp.min(jnp.where(valid & (x == m), lane, float(LANES)), axis=-1, keepdims=True)
    return m, idx


def _route_kernel(logit_ref, route_ref, count_ref, carry_ref):
    @pl.when(pl.program_id(0) == 0)
    def _():
        carry_ref[...] = jnp.zeros_like(carry_ref)

    lg = logit_ref[...]
    t = lg.shape[0]
    lane = lax.broadcasted_iota(jnp.int32, lg.shape, 1).astype(F32)
    gmask = lane < N_GROUPS
    gmax = jnp.max(jnp.where(gmask, lg, -jnp.inf), axis=-1, keepdims=True)
    gexp = jnp.where(gmask, jnp.exp(lg - gmax), 0.0)
    gp = gexp / jnp.sum(gexp, axis=-1, keepdims=True)
    g_w, g_i = _lane_argmax(gp, gmask, lane)

    lo = N_GROUPS + EXPERTS_PER_GROUP * g_i
    emask = (lane >= lo) & (lane < lo + EXPERTS_PER_GROUP)
    emax = jnp.max(jnp.where(emask, lg, -jnp.inf), axis=-1, keepdims=True)
    eexp = jnp.where(emask, jnp.exp(lg - emax), 0.0)
    ep = eexp / jnp.sum(eexp, axis=-1, keepdims=True)
    p1, i1 = _lane_argmax(ep, emask, lane)
    p2, i2 = _lane_argmax(ep, emask & (lane != i1), lane)
    denom = p1 + p2
    gate1 = g_w * p1 / denom
    gate2 = g_w * p2 / denom
    e1 = i1 - N_GROUPS
    e2 = i2 - N_GROUPS

    oh1 = (lane == e1).astype(F32)
    oh2 = (lane == e2).astype(F32)
    cnt = oh1 + oh2
    ri = lax.broadcasted_iota(jnp.int32, (t, t), 0)
    ci = lax.broadcasted_iota(jnp.int32, (t, t), 1)
    before = (ci < ri).astype(BF16)
    seen = jnp.dot(before, cnt.astype(BF16), preferred_element_type=F32) + carry_ref[0:1, :]
    rank1 = jnp.sum(oh1 * seen, axis=-1, keepdims=True)
    rank2 = jnp.sum(oh2 * seen, axis=-1, keepdims=True)
    carry_ref[0:1, :] = carry_ref[0:1, :] + jnp.sum(cnt, axis=0, keepdims=True)

    out = jnp.zeros(lg.shape, F32)
    for slot, val in ((ROUTE_E1, e1.astype(F32)), (ROUTE_E2, e2.astype(F32)), (ROUTE_G1, gate1),
                      (ROUTE_G2, gate2), (ROUTE_RANK1, rank1), (ROUTE_RANK2, rank2)):
        out = jnp.where(lane == slot, val, out)
    route_ref[...] = out
    count_ref[...] = carry_ref[...]


def _route(logits):
    n = logits.shape[0]
    t = 256
    return pl.pallas_call(
        _route_kernel,
        grid=(n // t,),
        in_specs=[pl.BlockSpec((t, LANES), lambda i: (i, 0))],
        out_specs=[pl.BlockSpec((t, LANES), lambda i: (i, 0)), pl.BlockSpec((SUBLANES, LANES), lambda i: (0, 0))],
        out_shape=[jax.ShapeDtypeStruct((n, LANES), F32), jax.ShapeDtypeStruct((SUBLANES, LANES), F32)],
        scratch_shapes=[pltpu.VMEM((SUBLANES, LANES), F32)],
        compiler_params=_cparams(("arbitrary",)),
        name="route",
    )(logits)


def _row_gather_copy(src_hbm, idx_ref, buf, sem, slot, r):
    return pltpu.make_async_copy(src_hbm.at[pl.ds(idx_ref[0, 0, r], 1), :], buf.at[slot, pl.ds(r, 1), :], sem.at[slot])


def _start_row_gather(src_hbm, idx_ref, buf, sem, slot, rows):
    def body(r, carry):
        _row_gather_copy(src_hbm, idx_ref, buf, sem, slot, r).start()
        return carry
    lax.fori_loop(0, rows, body, 0)


def _wait_row_gather(src_hbm, idx_ref, buf, sem, slot, rows):
    def body(r, carry):
        _row_gather_copy(src_hbm, idx_ref, buf, sem, slot, r).wait()
        return carry
    lax.fori_loop(0, rows, body, 0)


def _expert_kernel(blk_expert_ref, tok_ref, tok_next_ref, u_hbm, wg_ref, wu_ref, wd_ref, y_ref, xbuf, sem):
    i = pl.program_id(0)
    n = pl.num_programs(0)
    slot = i % 2

    @pl.when(i == 0)
    def _():
        _start_row_gather(u_hbm, tok_ref, xbuf, sem, 0, MOE_BLOCK)

    @pl.when(i + 1 < n)
    def _():
        _start_row_gather(u_hbm, tok_next_ref, xbuf, sem, 1 - slot, MOE_BLOCK)

    _wait_row_gather(u_hbm, tok_ref, xbuf, sem, slot, MOE_BLOCK)
    x = xbuf[slot].astype(BF16)
    hg = jnp.dot(x, wg_ref[0].astype(BF16), preferred_element_type=F32)
    hu = jnp.dot(x, wu_ref[0].astype(BF16), preferred_element_type=F32)
    act = (hg * _sigmoid(hg) * hu).astype(BF16)
    y_ref[...] = jnp.dot(act, wd_ref[0].astype(BF16), preferred_element_type=F32)


def _expert_mlp(u2, slot_tok, block_expert, w_gate, w_up, w_down):
    n_blk = block_expert.shape[0]
    d = u2.shape[1]
    hdim = w_gate.shape[2]
    tok3 = slot_tok.reshape(n_blk, 1, MOE_BLOCK)
    grid_spec = pltpu.PrefetchScalarGridSpec(
        num_scalar_prefetch=1,
        grid=(n_blk,),
        in_specs=[pl.BlockSpec((1, 1, MOE_BLOCK), lambda i, be: (i, 0, 0), memory_space=pltpu.SMEM),
                  pl.BlockSpec((1, 1, MOE_BLOCK), lambda i, be: (jnp.minimum(i + 1, n_blk - 1), 0, 0),
                               memory_space=pltpu.SMEM),
                  pl.BlockSpec(memory_space=pl.ANY),
                  pl.BlockSpec((1, d, hdim), lambda i, be: (be[i], 0, 0)),
                  pl.BlockSpec((1, d, hdim), lambda i, be: (be[i], 0, 0)),
                  pl.BlockSpec((1, hdim, d), lambda i, be: (be[i], 0, 0))],
        out_specs=pl.BlockSpec((MOE_BLOCK, d), lambda i, be: (i, 0)),
        scratch_shapes=[pltpu.VMEM((2, MOE_BLOCK, d), F32), pltpu.SemaphoreType.DMA((2,))],
    )
    return pl.pallas_call(
        _expert_kernel,
        grid_spec=grid_spec,
        out_shape=jax.ShapeDtypeStruct((n_blk * MOE_BLOCK, d), F32),
        compiler_params=_cparams(("arbitrary",)),
        name="expert_mlp",
    )(block_expert, tok3, tok3, u2, w_gate, w_up, w_down)


def _combine_kernel(d1_ref, d2_ref, d1n_ref, d2n_ref, y_hbm, route_ref, h1_ref, g2_ref, lng_ref, lnb_ref,
                    o_ref, abuf, bbuf, sem_a, sem_b):
    i = pl.program_id(0)
    n = pl.num_programs(0)
    slot = i % 2
    rows = o_ref.shape[0]

    @pl.when(i == 0)
    def _():
        _start_row_gather(y_hbm, d1_ref, abuf, sem_a, 0, rows)
        _start_row_gather(y_hbm, d2_ref, bbuf, sem_b, 0, rows)

    @pl.when(i + 1 < n)
    def _():
        _start_row_gather(y_hbm, d1n_ref, abuf, sem_a, 1 - slot, rows)
        _start_row_gather(y_hbm, d2n_ref, bbuf, sem_b, 1 - slot, rows)

    _wait_row_gather(y_hbm, d1_ref, abuf, sem_a, slot, rows)
    _wait_row_gather(y_hbm, d2_ref, bbuf, sem_b, slot, rows)
    route = route_ref[...]
    f = abuf[slot] * route[:, ROUTE_G1:ROUTE_G1 + 1] + bbuf[slot] * route[:, ROUTE_G2:ROUTE_G2 + 1]
    o_ref[...] = _layer_norm(DEEPNORM_ALPHA * h1_ref[...] + g2_ref[0] * f, lng_ref[...], lnb_ref[...])


def _combine(y_sorted, dest1, dest2, route, h1, g2, ln_g, ln_b, tokens_per_batch):
    n, d = h1.shape
    t = 128
    nt = n // t
    per_b = tokens_per_batch // t
    d1 = dest1.reshape(nt, 1, t)
    d2 = dest2.reshape(nt, 1, t)
    cur = pl.BlockSpec((1, 1, t), lambda i: (i, 0, 0), memory_space=pltpu.SMEM)
    nxt = pl.BlockSpec((1, 1, t), lambda i: (jnp.minimum(i + 1, nt - 1), 0, 0), memory_space=pltpu.SMEM)
    small = lambda arr: pl.BlockSpec(arr.shape, lambda i: (0,) * arr.ndim)
    return pl.pallas_call(
        _combine_kernel,
        grid=(nt,),
        in_specs=[cur, cur, nxt, nxt, pl.BlockSpec(memory_space=pl.ANY),
                  pl.BlockSpec((t, LANES), lambda i: (i, 0)),
                  pl.BlockSpec((t, d), lambda i: (i, 0)),
                  pl.BlockSpec((1, 1, d), lambda i: (i // per_b, 0, 0)),
                  small(ln_g), small(ln_b)],
        out_specs=pl.BlockSpec((t, d), lambda i: (i, 0)),
        out_shape=jax.ShapeDtypeStruct((n, d), F32),
        scratch_shapes=[pltpu.VMEM((2, t, d), F32), pltpu.VMEM((2, t, d), F32),
                        pltpu.SemaphoreType.DMA((2,)), pltpu.SemaphoreType.DMA((2,))],
        compiler_params=_cparams(("arbitrary",)),
        name="combine",
    )(d1, d2, d1, d2, y_sorted, route, h1, g2, ln_g, ln_b)


def _block_diag2(w):
    z = jnp.zeros_like(w[0])
    return jnp.concatenate([jnp.concatenate([w[0], z], axis=1), jnp.concatenate([z, w[1]], axis=1)], axis=0)


def kernel(x, c, ctx, c_ctx, emb_ln_g, emb_ln_b, w_mod, b_mod, w_in, tshift_mu, rwkv_w0, rwkv_w2, rwkv_a0, rwkv_a2, rwkv_g2, rwkv_k_k, rwkv_k_a, rwkv_r_k, rwkv_gn_g, rwkv_gn_b, ret_decay, ret_gn_g, ret_gn_b, w_out, ln1_g, ln1_b, router_group, router_group_bias, router_expert, router_expert_bias, expert_w_gate, expert_w_up, expert_w_down, ln2_g, ln2_b):
    assert w_mod.shape[0] == 1, "written for DEPTH == 1 (context outputs are never emitted)"
    b, n_tok, d = x.shape
    n_ctx = ctx.shape[1]
    row = lambda v: v.reshape(1, -1)

    c_rows = jnp.zeros((SUBLANES, d), F32).at[:b].set(c).at[b].set(c_ctx)
    mod = _modulation(c_rows, w_mod[0], row(b_mod[0]))
    sh1, s1, g1, sh2, s2, g2 = [mod[:b, j * d:(j + 1) * d].reshape(b, 1, d) for j in range(6)]
    sh1c, s1c = [jnp.broadcast_to(mod[b, j * d:(j + 1) * d].reshape(1, 1, d), (b, 1, d)) for j in range(2)]

    w_in_bf16 = w_in[0].astype(BF16)
    pr, pt = _in_proj(x, row(emb_ln_g), row(emb_ln_b), s1, sh1, w_in_bf16)
    pr_c, pt_c = _in_proj(ctx, row(emb_ln_g), row(emb_ln_b), s1c, sh1c, w_in_bf16)

    prep_params = (row(tshift_mu[0]), row(rwkv_w0[0]), _block_diag2(rwkv_w2[0]), row(rwkv_a0[0]),
                   _block_diag2(rwkv_a2[0]), rwkv_g2[0], row(rwkv_k_k[0]), row(rwkv_k_a[0]), row(rwkv_r_k[0]),
                   _segment_ones(D_RWKV, RWKV_HEAD))
    lat = _rwkv_prepare(pr, prep_params, grid_shift=True)
    cx = _rwkv_prepare(pr_c, prep_params, grid_shift=False)
    r_l, v_l, kk_l, w_l, kd_l, bb_l, gate_l, bonus_l = lat
    r_c, v_c, kk_c, w_c, kd_c, bb_c, _, _ = cx

    y_dirs = _wkv7((r_l, v_l, kk_l, w_l, kd_l, bb_l), (r_c, v_c, kk_c, w_c, kd_c, bb_c), b, n_tok, n_ctx)
    y_f = y_dirs[0]
    y_b = y_dirs[1]

    cos_t, sin_t = _rope_tables(n_tok)
    t_f, t_b = _retention(pt, pt_c, ret_decay[0], cos_t, sin_t)

    wr = jnp.zeros((d, LANES), F32).at[:, :N_GROUPS].set(router_group[0])
    wr = wr.at[:, N_GROUPS:N_GROUPS + N_EXPERTS].set(router_expert[0])
    br = jnp.zeros((1, LANES), F32).at[0, :N_GROUPS].set(router_group_bias[0])
    br = br.at[0, N_GROUPS:N_GROUPS + N_EXPERTS].set(router_expert_bias[0].reshape(-1))
    vecs = (row(emb_ln_g), row(emb_ln_b), g1, s2, sh2, row(rwkv_gn_g[0]), row(rwkv_gn_b[0]),
            row(ret_gn_g[0]), row(ret_gn_b[0]), row(ln1_g[0]), row(ln1_b[0]), br)
    mats = (_segment_ones(D_RWKV, RWKV_HEAD), _segment_ones(D_RET, RET_HEAD), w_out[0].astype(BF16), wr)
    h1, u2, logits = _out_proj(x, y_f, y_b, bonus_l, gate_l, t_f, t_b, pt, vecs, mats)

    n_all = b * n_tok
    route, counts = _route(logits.reshape(n_all, LANES))

    e1 = route[:, ROUTE_E1].astype(jnp.int32)
    e2 = route[:, ROUTE_E2].astype(jnp.int32)
    cnt = counts[0, :N_EXPERTS].astype(jnp.int32)
    padded = ((cnt + MOE_BLOCK - 1) // MOE_BLOCK) * MOE_BLOCK
    pends = jnp.cumsum(padded)
    pstarts = pends - padded
    dest1 = pstarts[e1] + route[:, ROUTE_RANK1].astype(jnp.int32)
    dest2 = pstarts[e2] + route[:, ROUTE_RANK2].astype(jnp.int32)
    n_blk = -(-(n_all * 2) // MOE_BLOCK) + N_EXPERTS
    tok_ids = jnp.arange(n_all, dtype=jnp.int32)
    slot_tok = jnp.zeros((n_blk * MOE_BLOCK,), jnp.int32).at[dest1].set(tok_ids).at[dest2].set(tok_ids)
    block_start = jnp.arange(n_blk, dtype=jnp.int32) * MOE_BLOCK
    block_expert = jnp.minimum(jnp.searchsorted(pends, block_start, side='right'), N_EXPERTS - 1).astype(jnp.int32)

    y_sorted = _expert_mlp(u2.reshape(n_all, d), slot_tok, block_expert,
                           expert_w_gate[0], expert_w_up[0], expert_w_down[0])
    out = _combine(y_sorted, dest1, dest2, route, h1.reshape(n_all, d), g2, row(ln2_g[0]), row(ln2_b[0]), n_tok)
    return out.reshape(b, n_tok, d)
```

```python
import functools
import math

import jax
import jax.numpy as jnp
import numpy as np
from jax import lax
from jax.experimental import pallas as pl
from jax.experimental.pallas import tpu as pltpu

F32 = jnp.float32
BF16 = jnp.bfloat16
HIGHEST = lax.Precision.HIGHEST

GRID_W = 64
D_RWKV = 512
RWKV_HEAD = 64
RWKV_HEADS = D_RWKV // RWKV_HEAD
DECAY_LORA = 64
AAA_LORA = 64
GATE_LORA = 128
D_RET = 512
RET_HEADS = 4
RET_HEAD = D_RET // RET_HEADS
RET_CHUNK = 128
RWKV_COLS = 3 * D_RWKV + 2 * (DECAY_LORA + AAA_LORA) + GATE_LORA
RET_COLS = 4 * D_RET
N_GROUPS = 4
EXPERTS_PER_GROUP = 8
N_EXPERTS = N_GROUPS * EXPERTS_PER_GROUP
EXPERT_HIDDEN = 512
MOE_BLOCK = 128
ROPE_BASE = 10000.0
LN_EPS = 1e-5
RWKV_GN_EPS = 64e-5
RET_GN_EPS = 1e-5
DEEPNORM_ALPHA = 2.0 ** 0.25
EXP_NEG_HALF = math.exp(-0.5)

LANES = 128
SUBLANES = 8
VMEM_LIMIT_BYTES = 56 * 1024 * 1024

WKV_CHUNK = 64


def _cparams(sem):
    return pltpu.CompilerParams(dimension_semantics=sem, vmem_limit_bytes=VMEM_LIMIT_BYTES)


def _layer_norm(x, g, b, eps=LN_EPS):
    mu = jnp.mean(x, axis=-1, keepdims=True)
    xc = x - mu
    var = jnp.mean(xc * xc, axis=-1, keepdims=True)
    return xc * lax.rsqrt(var + eps) * g + b


def _sigmoid(x):
    return 1.0 / (1.0 + jnp.exp(-x))


def _segsum(x, ones_bf16):
    hi = x.astype(BF16)
    r1 = x - hi.astype(F32)
    mid = r1.astype(BF16)
    lo = (r1 - mid.astype(F32)).astype(BF16)
    acc = jnp.dot(hi, ones_bf16, preferred_element_type=F32)
    acc = acc + jnp.dot(mid, ones_bf16, preferred_element_type=F32)
    return acc + jnp.dot(lo, ones_bf16, preferred_element_type=F32)


def _segment_ones(width, seg):
    idx = np.arange(width) // seg
    return jnp.asarray(idx[:, None] == idx[None, :], dtype=BF16)


def _mod_kernel(c_ref, w_ref, b_ref, o_ref):
    c = c_ref[...]
    sc = c * _sigmoid(c)
    o_ref[...] = jnp.dot(sc, w_ref[...], precision=HIGHEST, preferred_element_type=F32) + b_ref[...]


def _modulation(c_rows, w_mod, b_mod):
    rows, d = c_rows.shape
    n = w_mod.shape[1]
    tn = 1536
    return pl.pallas_call(
        _mod_kernel,
        grid=(n // tn,),
        in_specs=[pl.BlockSpec((rows, d), lambda j: (0, 0)),
                  pl.BlockSpec((d, tn), lambda j: (0, j)),
                  pl.BlockSpec((1, tn), lambda j: (0, j))],
        out_specs=pl.BlockSpec((rows, tn), lambda j: (0, j)),
        out_shape=jax.ShapeDtypeStruct((rows, n), F32),
        compiler_params=_cparams(("arbitrary",)),
        name="modulation",
    )(c_rows, w_mod, b_mod)


def _in_proj_kernel(x_ref, g_ref, b_ref, s_ref, sh_ref, w_ref, pr_ref, pt_ref):
    h = _layer_norm(x_ref[0], g_ref[...], b_ref[...])
    u = h * (1.0 + s_ref[0]) + sh_ref[0]
    p = jnp.dot(u.astype(BF16), w_ref[...], preferred_element_type=F32)
    pr_ref[0] = p[:, :RWKV_COLS]
    pt_ref[0] = p[:, RWKV_COLS:]


def _in_proj(x, ln_g, ln_b, s1, sh1, w_in_bf16):
    b, n, d = x.shape
    tm = 256
    cols = w_in_bf16.shape[1]
    return pl.pallas_call(
        _in_proj_kernel,
        grid=(b, n // tm),
        in_specs=[pl.BlockSpec((1, tm, d), lambda bi, i: (bi, i, 0)),
                  pl.BlockSpec((1, d), lambda bi, i: (0, 0)),
                  pl.BlockSpec((1, d), lambda bi, i: (0, 0)),
                  pl.BlockSpec((1, 1, d), lambda bi, i: (bi, 0, 0)),
                  pl.BlockSpec((1, 1, d), lambda bi, i: (bi, 0, 0)),
                  pl.BlockSpec((d, cols), lambda bi, i: (0, 0))],
        out_specs=[pl.BlockSpec((1, tm, RWKV_COLS), lambda bi, i: (bi, i, 0)),
                   pl.BlockSpec((1, tm, RET_COLS), lambda bi, i: (bi, i, 0))],
        out_shape=[jax.ShapeDtypeStruct((b, n, RWKV_COLS), F32),
                   jax.ShapeDtypeStruct((b, n, RET_COLS), F32)],
        compiler_params=_cparams(("arbitrary", "arbitrary")),
        name="in_proj",
    )(x, ln_g, ln_b, s1, sh1, w_in_bf16)


def _rwkv_prepare_kernel(cur_ref, prev_ref, next_ref, mu_ref, w0_ref, w2_ref, a0_ref, a2_ref, g2_ref,
                         kk_scale_ref, ka_ref, rk_ref, ones_ref,
                         r_ref, v_ref, kk_ref, w_ref, kd_ref, bb_ref, g_ref, bonus_ref,
                         *, grid_shift, n_tok):
    cur = cur_ref[0]
    t, c = cur.shape
    row = lax.broadcasted_iota(jnp.int32, (t, c), 0)
    lane = lax.broadcasted_iota(jnp.int32, (t, c), 1)
    prev_tok = pltpu.roll(cur, 1, 0)
    next_tok = pltpu.roll(cur, t - 1, 0)
    if grid_shift:
        col = row & (GRID_W - 1)
        tok = row + pl.program_id(1) * t
        left = jnp.where(col > 0, prev_tok, 0.0)
        right = jnp.where(col < GRID_W - 1, next_tok, 0.0)
        up = jnp.where(tok >= GRID_W, jnp.concatenate([prev_ref[0], cur[:t - GRID_W]], axis=0), 0.0)
        down = jnp.where(tok < n_tok - GRID_W, jnp.concatenate([cur[GRID_W:], next_ref[0]], axis=0), 0.0)
        cm = lane & 3
        shifted = jnp.where(cm == 0, left, jnp.where(cm == 1, right, jnp.where(cm == 2, up, down)))
    else:
        prev_tok = jnp.where(row > 0, prev_tok, 0.0)
        next_tok = jnp.where(row < t - 1, next_tok, 0.0)
        shifted = jnp.where((lane & 1) == 0, prev_tok, next_tok)
    pm = cur + mu_ref[...] * (shifted - cur)

    r = pm[:, 0:D_RWKV]
    k = pm[:, D_RWKV:2 * D_RWKV]
    v = pm[:, 2 * D_RWKV:3 * D_RWKV]
    o = 3 * D_RWKV
    lw = pm[:, o:o + 2 * DECAY_LORA]
    la = pm[:, o + 2 * DECAY_LORA:o + 2 * (DECAY_LORA + AAA_LORA)]
    lg = pm[:, o + 2 * (DECAY_LORA + AAA_LORA):]

    w = w0_ref[...] + jnp.dot(jnp.tanh(lw), w2_ref[...], precision=HIGHEST, preferred_element_type=F32)
    log_decay = -EXP_NEG_HALF * _sigmoid(w)
    a = _sigmoid(a0_ref[...] + jnp.dot(la, a2_ref[...], precision=HIGHEST, preferred_element_type=F32))
    gate = jnp.dot(_sigmoid(lg), g2_ref[...], precision=HIGHEST, preferred_element_type=F32)

    ones = ones_ref[...]
    kk_raw = k * kk_scale_ref[...]
    kk = kk_raw / jnp.maximum(jnp.sqrt(_segsum(kk_raw * kk_raw, ones)), 1e-12)
    ka = ka_ref[...]
    a0 = a[:, :D_RWKV]
    a1 = a[:, D_RWKV:]
    kd0 = k * (1.0 + (a0 - 1.0) * ka)
    kd1 = k * (1.0 + (a1 - 1.0) * ka)
    bonus = _segsum(r * (kd0 + kd1) * rk_ref[...], ones) * v

    r_ref[0] = r
    v_ref[0] = v
    kk_ref[0] = kk
    w_ref[0] = log_decay
    kd_ref[0] = jnp.concatenate([kd0, kd1], axis=-1)
    bb_ref[0] = jnp.concatenate([kk * a0, kk * a1], axis=-1)
    g_ref[0] = gate
    bonus_ref[0] = bonus


def _rwkv_prepare(pr, params, grid_shift):
    b, n, c = pr.shape
    t = 256
    if not grid_shift:
        assert n == t, "sequence token shift is written for a single tile"
    halo_blocks = n // GRID_W
    per_tile = t // GRID_W
    small = lambda shape: pl.BlockSpec(shape, lambda bi, i: (0,) * len(shape))
    tok_spec = lambda width: pl.BlockSpec((1, t, width), lambda bi, i: (bi, i, 0))
    out_widths = (D_RWKV, D_RWKV, D_RWKV, 2 * D_RWKV, 2 * D_RWKV, 2 * D_RWKV, D_RWKV, D_RWKV)
    kernel = functools.partial(_rwkv_prepare_kernel, grid_shift=grid_shift, n_tok=n)
    return pl.pallas_call(
        kernel,
        grid=(b, n // t),
        in_specs=[tok_spec(c),
                  pl.BlockSpec((1, GRID_W, c), lambda bi, i: (bi, jnp.maximum(i * per_tile - 1, 0), 0)),
                  pl.BlockSpec((1, GRID_W, c),
                               lambda bi, i: (bi, jnp.minimum((i + 1) * per_tile, halo_blocks - 1), 0)),
                  small((1, c)), small((1, 2 * D_RWKV)), small((2 * DECAY_LORA, 2 * D_RWKV)),
                  small((1, 2 * D_RWKV)), small((2 * AAA_LORA, 2 * D_RWKV)), small((GATE_LORA, D_RWKV)),
                  small((1, D_RWKV)), small((1, D_RWKV)), small((1, D_RWKV)), small((D_RWKV, D_RWKV))],
        out_specs=[tok_spec(wd) for wd in out_widths],
        out_shape=[jax.ShapeDtypeStruct((b, n, wd), F32) for wd in out_widths],
        compiler_params=_cparams(("arbitrary", "arbitrary")),
        name="rwkv_prepare",
    )(pr, pr, pr, *params)


def _bdot(a, b):
    return jnp.dot(a.astype(BF16), b.astype(BF16), preferred_element_type=F32)


def _bdot_nt(a, b):
    return lax.dot_general(a.astype(BF16), b.astype(BF16), (((1,), (1,)), ((), ())), preferred_element_type=F32)


def _bdot_tn(a, b):
    return lax.dot_general(a.astype(BF16), b.astype(BF16), (((0,), (0,)), ((), ())), preferred_element_type=F32)


def _wkv7_chunk_kernel(r_l, v_l, kk_l, lw_l, kd_l, bb_l, r_c, v_c, kk_c, lw_c, kd_c, bb_c, y_ref, state_ref,
                       *, n_ctx_chunks):
    c = WKV_CHUNK
    d = pl.program_id(0)
    n = pl.program_id(1)
    n_batch = r_l.shape[0]
    sign = 1 - 2 * d

    @pl.when(n == 0)
    def _():
        state_ref[...] = jnp.zeros_like(state_ref)

    is_ctx = n < n_ctx_chunks
    pick = lambda xc, xl: jnp.concatenate([jnp.where(is_ctx, xc[bi], xl[bi]) for bi in range(n_batch)], axis=-1)
    r = pick(r_c, r_l)
    v = pick(v_c, v_l)
    kk = pick(kk_c, kk_l)
    lw = pick(lw_c, lw_l)
    kd = pick(kd_c, kd_l)
    bb = pick(bb_c, bb_l)

    ti = lax.broadcasted_iota(jnp.int32, (c, c), 0)
    tj = lax.broadcasted_iota(jnp.int32, (c, c), 1)
    upto = ((tj - ti) * sign <= 0).astype(BF16)
    hi = lw.astype(BF16)
    r1 = lw - hi.astype(F32)
    mid = r1.astype(BF16)
    lo = (r1 - mid.astype(F32)).astype(BF16)
    cum = (jnp.dot(upto, hi, preferred_element_type=F32) + jnp.dot(upto, mid, preferred_element_type=F32)
           + jnp.dot(upto, lo, preferred_element_type=F32))
    tot = jnp.sum(lw, axis=0, keepdims=True)
    e_neg = jnp.exp(-cum)
    e_rem = jnp.exp(tot - cum)
    alpha = kk * jnp.exp(cum - lw)
    rho = r * jnp.exp(cum)
    beta = bb * e_neg
    kappa = kd * e_neg
    kappa_rem = kd * e_rem
    beta_rem = bb * e_rem
    g_chunk = jnp.exp(tot)

    p = 2 * c
    ri = lax.broadcasted_iota(jnp.int32, (p, p), 0)
    ci = lax.broadcasted_iota(jnp.int32, (p, p), 1)
    same_head = (ri >= c) == (ci >= c)
    ii = ri & (c - 1)
    jj = ci & (c - 1)
    earlier = same_head & ((jj - ii) * sign < 0)
    upto_self = earlier | (ri == ci)
    eye = (ri == ci).astype(F32)
    lane = lax.broadcasted_iota(jnp.int32, (c, p), 1)
    first = lane < RWKV_HEAD

    def stack(x):
        return jnp.concatenate([jnp.where(first, x, 0.0), jnp.where(first, 0.0, x)], axis=0)

    def unstack(x):
        return x[:c] + x[c:]

    pairs_per_batch = RWKV_HEADS // 2
    pairs = range(n_batch * pairs_per_batch)
    sls = [slice(hp * p, (hp + 1) * p) for hp in pairs]
    a_st = [stack(alpha[:, sl]) for sl in sls]
    r_st = [stack(rho[:, sl]) for sl in sls]
    k_st = [stack(kappa[:, sl]) for sl in sls]
    b_st = [stack(beta[:, sl]) for sl in sls]
    k2_st = [stack(kappa_rem[:, sl]) for sl in sls]
    b2_st = [stack(beta_rem[:, sl]) for sl in sls]
    v_st = [stack(v[:, sl]) for sl in sls]

    g = [_bdot_nt(jnp.concatenate([a_st[h], r_st[h]], axis=0), jnp.concatenate([k_st[h], b_st[h]], axis=0))
         for h in pairs]
    m1 = [jnp.where(earlier, g[h][:p, :p], 0.0) for h in pairs]
    m2 = [jnp.where(earlier, g[h][:p, p:], 0.0) for h in pairs]
    n1 = [jnp.where(upto_self, g[h][p:, :p], 0.0) for h in pairs]
    n2 = [jnp.where(upto_self, g[h][p:, p:], 0.0) for h in pairs]

    in_block = (ii >> 3) == (jj >> 3)
    pw = [-jnp.where(in_block, m2[h], 0.0) for h in pairs]
    inv = [eye + pw[h] for h in pairs]
    for _ in range(2):
        pw = [_bdot(pw[h], pw[h]) for h in pairs]
        inv = [inv[h] + _bdot(inv[h], pw[h]) for h in pairs]
    for sh in (3, 4, 5):
        off = ((ii >> (sh + 1)) == (jj >> (sh + 1))) & ((ii >> sh) != (jj >> sh))
        left = [_bdot(inv[h], jnp.where(off, m2[h], 0.0)) for h in pairs]
        inv = [inv[h] - _bdot(left[h], inv[h]) for h in pairs]

    m1v = [_bdot(m1[h], v_st[h]) for h in pairs]
    n1v = [_bdot(n1[h], v_st[h]) for h in pairs]
    au = [_bdot(inv[h], jnp.concatenate([a_st[h], m1v[h]], axis=1)) for h in pairs]
    nn = [_bdot(n2[h], au[h]) for h in pairs]
    pc = [_bdot_tn(b2_st[h], au[h][:, :p]) for h in pairs]
    qc_t = [_bdot_tn(jnp.concatenate([v_st[h], -au[h][:, p:]], axis=0),
                     jnp.concatenate([k2_st[h], b2_st[h]], axis=0)) for h in pairs]
    s0 = [state_ref[h] for h in pairs]
    y = [_bdot_nt(unstack(r_st[h] - nn[h][:, :p]), s0[h]) + unstack(n1v[h] - nn[h][:, p:]) for h in pairs]
    s_dec = [_bdot_nt(s0[h], pc[h]) for h in pairs]
    for h in pairs:
        y_ref[0, h // pairs_per_batch, :, sls[h % pairs_per_batch]] = y[h]
        state_ref[h] = s0[h] * g_chunk[:, sls[h]] - s_dec[h] + qc_t[h]


def _wkv7(lat, ctx, b, n_tok, n_ctx):
    c = WKV_CHUNK
    ncx = n_ctx // c
    nl = n_tok // c
    lat_idx = lambda d, n: jnp.where(d == 0, jnp.maximum(n - ncx, 0), nl - 1 - jnp.maximum(n - ncx, 0))
    ctx_idx = lambda d, n: jnp.where(d == 0, jnp.minimum(n, ncx - 1), ncx - 1 - jnp.minimum(n, ncx - 1))
    shared = lambda idx: pl.BlockSpec((b, c, D_RWKV), lambda d, n: (0, idx(d, n), 0))
    per_dir = lambda idx: pl.BlockSpec((b, c, D_RWKV), lambda d, n: (0, idx(d, n), d))
    specs = lambda idx: [shared(idx), shared(idx), shared(idx), per_dir(idx), per_dir(idx), per_dir(idx)]
    return pl.pallas_call(
        functools.partial(_wkv7_chunk_kernel, n_ctx_chunks=ncx),
        grid=(2, ncx + nl),
        in_specs=specs(lat_idx) + specs(ctx_idx),
        out_specs=pl.BlockSpec((1, b, c, D_RWKV), lambda d, n: (d, 0, lat_idx(d, n), 0)),
        out_shape=jax.ShapeDtypeStruct((2, b, n_tok, D_RWKV), F32),
        scratch_shapes=[pltpu.VMEM((b * RWKV_HEADS // 2, 2 * RWKV_HEAD, 2 * RWKV_HEAD), F32)],
        compiler_params=_cparams(("arbitrary", "arbitrary")),
        name="wkv7_chunk",
    )(*lat, *ctx)


def _rope(z, cos_t, sin_t):
    lane = lax.broadcasted_iota(jnp.int32, z.shape, 1)
    half = RET_HEAD // 4
    partner = jnp.where((lane & (2 * half - 1)) < half, pltpu.roll(z, RET_HEAD - half, 1), pltpu.roll(z, half, 1))
    return z * cos_t + partner * sin_t


def _retention_kernel(dec_ref, fwd_ref, bwd_ref, ctx_ref, cosf_ref, sinf_ref, cosb_ref, sinb_ref,
                      yf_ref, yb_ref, state_ref):
    c = RET_CHUNK
    scale = RET_HEAD ** -0.5
    ii = lax.broadcasted_iota(jnp.int32, (c, c), 0)
    jj = lax.broadcasted_iota(jnp.int32, (c, c), 1)
    pos = lax.broadcasted_iota(jnp.int32, (c, RET_HEAD), 0).astype(F32)
    n_ctx_chunks = ctx_ref.shape[1] // c

    def head_slices(ref_val, h):
        q = ref_val[:, h * RET_HEAD:(h + 1) * RET_HEAD]
        k = ref_val[:, D_RET + h * RET_HEAD:D_RET + (h + 1) * RET_HEAD]
        v = ref_val[:, 2 * D_RET + h * RET_HEAD:2 * D_RET + (h + 1) * RET_HEAD]
        return q, k, v

    for d in range(2):
        blk = fwd_ref[0] if d == 0 else bwd_ref[0]
        cos_t = cosf_ref[...] if d == 0 else cosb_ref[...]
        sin_t = sinf_ref[...] if d == 0 else sinb_ref[...]
        for h in range(RET_HEADS):
            x = jnp.full((1, RET_HEAD), dec_ref[d, h], F32)
            lg = -(jnp.maximum(x, 0.0) + jnp.log(1.0 + jnp.exp(-jnp.abs(x))))
            chunk_decay = jnp.exp(lg * float(c))
            tail = jnp.exp(lg * ((c - 1.0 - pos) if d == 0 else pos))
            head = jnp.exp(lg * ((pos + 1.0) if d == 0 else (c - pos)))

            @pl.when(pl.program_id(1) == 0)
            def _():
                s = jnp.zeros((RET_HEAD, RET_HEAD), F32)
                order = range(n_ctx_chunks) if d == 0 else range(n_ctx_chunks - 1, -1, -1)
                for cc in order:
                    _, kc, vc = head_slices(ctx_ref[0, cc * c:(cc + 1) * c, :], h)
                    kw = (kc * scale * tail).astype(BF16)
                    u = lax.dot_general(kw, vc.astype(BF16), (((0,), (0,)), ((), ())), preferred_element_type=F32)
                    s = s * chunk_decay + u
                state_ref[d, h] = s

            q, k, v = head_slices(blk, h)
            q = _rope(q, cos_t, sin_t)
            k = _rope(k, cos_t, sin_t) * scale
            s = state_ref[d, h]
            scores = lax.dot_general(q.astype(BF16), k.astype(BF16), (((1,), (1,)), ((), ())),
                                     preferred_element_type=F32)
            rel = (ii - jj) if d == 0 else (jj - ii)
            mask = (rel >= 0) if d == 0 else (rel > 0)
            dmat = jnp.where(mask, jnp.exp(lg * jnp.maximum(rel, 0).astype(F32)), 0.0)
            v_bf = v.astype(BF16)
            inner = jnp.dot((scores * dmat).astype(BF16), v_bf, preferred_element_type=F32)
            cross = jnp.dot((q * head).astype(BF16), s.astype(BF16), preferred_element_type=F32)
            u = lax.dot_general((k * tail).astype(BF16), v_bf, (((0,), (0,)), ((), ())), preferred_element_type=F32)
            state_ref[d, h] = s * chunk_decay + u
            out_ref = yf_ref if d == 0 else yb_ref
            out_ref[0, :, h * RET_HEAD:(h + 1) * RET_HEAD] = inner + cross


def _retention(pt, pt_ctx, ret_decay, cos_t, sin_t):
    b, n, _ = pt.shape
    c = RET_CHUNK
    nc = n // c
    qkv = 3 * D_RET
    fwd = lambda bi, i: (bi, i, 0)
    bwd = lambda bi, i: (bi, nc - 1 - i, 0)
    return pl.pallas_call(
        _retention_kernel,
        grid=(b, nc),
        in_specs=[pl.BlockSpec(memory_space=pltpu.SMEM),
                  pl.BlockSpec((1, c, qkv), fwd),
                  pl.BlockSpec((1, c, qkv), bwd),
                  pl.BlockSpec((1, pt_ctx.shape[1], qkv), lambda bi, i: (bi, 0, 0)),
                  pl.BlockSpec((c, RET_HEAD), lambda bi, i: (i, 0)),
                  pl.BlockSpec((c, RET_HEAD), lambda bi, i: (i, 0)),
                  pl.BlockSpec((c, RET_HEAD), lambda bi, i: (nc - 1 - i, 0)),
                  pl.BlockSpec((c, RET_HEAD), lambda bi, i: (nc - 1 - i, 0))],
        out_specs=[pl.BlockSpec((1, c, D_RET), fwd), pl.BlockSpec((1, c, D_RET), bwd)],
        out_shape=[jax.ShapeDtypeStruct((b, n, D_RET), F32), jax.ShapeDtypeStruct((b, n, D_RET), F32)],
        scratch_shapes=[pltpu.VMEM((2, RET_HEADS, RET_HEAD, RET_HEAD), F32)],
        compiler_params=_cparams(("arbitrary", "arbitrary")),
        name="retention",
    )(ret_decay, pt, pt, pt_ctx, cos_t, sin_t, cos_t, sin_t)


def _rope_tables(n_tok):
    nf = RET_HEAD // 4
    lane = np.arange(RET_HEAD)
    inv = ROPE_BASE ** (-jnp.arange(nf, dtype=F32) / nf)
    t = jnp.arange(n_tok)
    pos = jnp.where((lane // (2 * nf) == 0)[None, :], (t // GRID_W)[:, None], (t % GRID_W)[:, None]).astype(F32)
    ang = pos * inv[lane % nf][None, :]
    sign = jnp.where((lane % (2 * nf)) < nf, -1.0, 1.0).astype(F32)
    return jnp.cos(ang), jnp.sin(ang) * sign[None, :]


def _group_norm(y, ones, seg, eps, g, b):
    mu = _segsum(y, ones) * (1.0 / seg)
    yc = y - mu
    var = _segsum(yc * yc, ones) * (1.0 / seg)
    return yc * lax.rsqrt(var + eps) * g + b


def _out_proj_kernel(x_ref, yf_ref, yb_ref, bonus_ref, gate_ref, tf_ref, tb_ref, gt_ref,
                     embg_ref, embb_ref, g1_ref, s2_ref, sh2_ref, rgn_g_ref, rgn_b_ref, tgn_g_ref, tgn_b_ref,
                     ones_r_ref, ones_t_ref, wout_ref, ln1g_ref, ln1b_ref, wr_ref, br_ref,
                     h1_ref, u2_ref, logit_ref):
    y = yf_ref[0] + yb_ref[0]
    o_rwkv = _group_norm(y, ones_r_ref[...], RWKV_HEAD, RWKV_GN_EPS, rgn_g_ref[...], rgn_b_ref[...])
    o_rwkv = (o_rwkv + bonus_ref[0]) * gate_ref[0]
    yt = tf_ref[0] + tb_ref[0]
    gt = gt_ref[0]
    o_ret = _group_norm(yt, ones_t_ref[...], RET_HEAD, RET_GN_EPS, tgn_g_ref[...], tgn_b_ref[...])
    o_ret = o_ret * (gt * _sigmoid(gt))
    cat = jnp.concatenate([o_rwkv, o_ret], axis=-1).astype(BF16)
    mix = jnp.dot(cat, wout_ref[...], preferred_element_type=F32)
    h = _layer_norm(x_ref[0], embg_ref[...], embb_ref[...])
    h1 = _layer_norm(DEEPNORM_ALPHA * h + g1_ref[0] * mix, ln1g_ref[...], ln1b_ref[...])
    u2 = h1 * (1.0 + s2_ref[0]) + sh2_ref[0]
    h1_ref[0] = h1
    u2_ref[0] = u2
    logit_ref[0] = jnp.dot(u2, wr_ref[...], precision=HIGHEST, preferred_element_type=F32) + br_ref[...]


def _out_proj(x, y_f, y_b, bonus, gate, t_f, t_b, pt, vecs, mats):
    b, n, d = x.shape
    t = 256
    tok = lambda width: pl.BlockSpec((1, t, width), lambda bi, i: (bi, i, 0))
    per_b = pl.BlockSpec((1, 1, d), lambda bi, i: (bi, 0, 0))
    small = lambda arr: pl.BlockSpec(arr.shape, lambda bi, i: (0,) * arr.ndim)
    (embg, embb, g1, s2, sh2, rgn_g, rgn_b, tgn_g, tgn_b, ln1g, ln1b, br) = vecs
    (ones_r, ones_t, wout, wr) = mats
    gt_spec = pl.BlockSpec((1, t, D_RET), lambda bi, i: (bi, i, 3))
    args = (x, y_f, y_b, bonus, gate, t_f, t_b, pt, embg, embb, g1, s2, sh2, rgn_g, rgn_b, tgn_g, tgn_b,
            ones_r, ones_t, wout, ln1g, ln1b, wr, br)
    in_specs = [tok(d)] + [tok(D_RWKV)] * 6 + [gt_spec, small(embg), small(embb), per_b, per_b, per_b,
                                                small(rgn_g), small(rgn_b), small(tgn_g), small(tgn_b),
                                                small(ones_r), small(ones_t), small(wout), small(ln1g),
                                                small(ln1b), small(wr), small(br)]
    return pl.pallas_call(
        _out_proj_kernel,
        grid=(b, n // t),
        in_specs=in_specs,
        out_specs=[tok(d), tok(d), tok(LANES)],
        out_shape=[jax.ShapeDtypeStruct((b, n, d), F32), jax.ShapeDtypeStruct((b, n, d), F32),
                   jax.ShapeDtypeStruct((b, n, LANES), F32)],
        compiler_params=_cparams(("arbitrary", "arbitrary")),
        name="out_proj",
    )(*args)


ROUTE_E1, ROUTE_E2, ROUTE_G1, ROUTE_G2, ROUTE_RANK1, ROUTE_RANK2 = range(6)


def _lane_argmax(x, valid, lane):
    m = jnp.max(jnp.where(valid, x, -jnp.inf), axis=-1, keepdims=True)
    idx = jnp.min(jnp.where(valid & (x == m), lane, float(LANES)), axis=-1, keepdims=True)
    return m, idx


def _route_kernel(logit_ref, route_ref, count_ref, carry_ref):
    @pl.when(pl.program_id(0) == 0)
    def _():
        carry_ref[...] = jnp.zeros_like(carry_ref)

    lg = logit_ref[...]
    t = lg.shape[0]
    lane = lax.broadcasted_iota(jnp.int32, lg.shape, 1).astype(F32)
    gmask = lane < N_GROUPS
    gmax = jnp.max(jnp.where(gmask, lg, -jnp.inf), axis=-1, keepdims=True)
    gexp = jnp.where(gmask, jnp.exp(lg - gmax), 0.0)
    gp = gexp / jnp.sum(gexp, axis=-1, keepdims=True)
    g_w, g_i = _lane_argmax(gp, gmask, lane)

    lo = N_GROUPS + EXPERTS_PER_GROUP * g_i
    emask = (lane >= lo) & (lane < lo + EXPERTS_PER_GROUP)
    emax = jnp.max(jnp.where(emask, lg, -jnp.inf), axis=-1, keepdims=True)
    eexp = jnp.where(emask, jnp.exp(lg - emax), 0.0)
    ep = eexp / jnp.sum(eexp, axis=-1, keepdims=True)
    p1, i1 = _lane_argmax(ep, emask, lane)
    p2, i2 = _lane_argmax(ep, emask & (lane != i1), lane)
    denom = p1 + p2
    gate1 = g_w * p1 / denom
    gate2 = g_w * p2 / denom
    e1 = i1 - N_GROUPS
    e2 = i2 - N_GROUPS

    oh1 = (lane == e1).astype(F32)
    oh2 = (lane == e2).astype(F32)
    cnt = oh1 + oh2
    ri = lax.broadcasted_iota(jnp.int32, (t, t), 0)
    ci = lax.broadcasted_iota(jnp.int32, (t, t), 1)
    before = (ci < ri).astype(BF16)
    seen = jnp.dot(before, cnt.astype(BF16), preferred_element_type=F32) + carry_ref[0:1, :]
    rank1 = jnp.sum(oh1 * seen, axis=-1, keepdims=True)
    rank2 = jnp.sum(oh2 * seen, axis=-1, keepdims=True)
    carry_ref[0:1, :] = carry_ref[0:1, :] + jnp.sum(cnt, axis=0, keepdims=True)

    out = jnp.zeros(lg.shape, F32)
    for slot, val in ((ROUTE_E1, e1.astype(F32)), (ROUTE_E2, e2.astype(F32)), (ROUTE_G1, gate1),
                      (ROUTE_G2, gate2), (ROUTE_RANK1, rank1), (ROUTE_RANK2, rank2)):
        out = jnp.where(lane == slot, val, out)
    route_ref[...] = out
    count_ref[...] = carry_ref[...]


def _route(logits):
    n = logits.shape[0]
    t = 256
    return pl.pallas_call(
        _route_kernel,
        grid=(n // t,),
        in_specs=[pl.BlockSpec((t, LANES), lambda i: (i, 0))],
        out_specs=[pl.BlockSpec((t, LANES), lambda i: (i, 0)), pl.BlockSpec((SUBLANES, LANES), lambda i: (0, 0))],
        out_shape=[jax.ShapeDtypeStruct((n, LANES), F32), jax.ShapeDtypeStruct((SUBLANES, LANES), F32)],
        scratch_shapes=[pltpu.VMEM((SUBLANES, LANES), F32)],
        compiler_params=_cparams(("arbitrary",)),
        name="route",
    )(logits)


def _row_gather_copy(src_hbm, idx_ref, buf, sem, slot, r):
    return pltpu.make_async_copy(src_hbm.at[pl.ds(idx_ref[0, 0, r], 1), :], buf.at[slot, pl.ds(r, 1), :], sem.at[slot])


def _start_row_gather(src_hbm, idx_ref, buf, sem, slot, rows):
    def body(r, carry):
        _row_gather_copy(src_hbm, idx_ref, buf, sem, slot, r).start()
        return carry
    lax.fori_loop(0, rows, body, 0)


def _wait_row_gather(src_hbm, idx_ref, buf, sem, slot, rows):
    def body(r, carry):
        _row_gather_copy(src_hbm, idx_ref, buf, sem, slot, r).wait()
        return carry
    lax.fori_loop(0, rows, body, 0)


def _expert_kernel(blk_expert_ref, tok_ref, tok_next_ref, u_hbm, wg_ref, wu_ref, wd_ref, y_ref, xbuf, sem):
    i = pl.program_id(0)
    n = pl.num_programs(0)
    slot = i % 2

    @pl.when(i == 0)
    def _():
        _start_row_gather(u_hbm, tok_ref, xbuf, sem, 0, MOE_BLOCK)

    @pl.when(i + 1 < n)
    def _():
        _start_row_gather(u_hbm, tok_next_ref, xbuf, sem, 1 - slot, MOE_BLOCK)

    _wait_row_gather(u_hbm, tok_ref, xbuf, sem, slot, MOE_BLOCK)
    x = xbuf[slot].astype(BF16)
    hg = jnp.dot(x, wg_ref[0].astype(BF16), preferred_element_type=F32)
    hu = jnp.dot(x, wu_ref[0].astype(BF16), preferred_element_type=F32)
    act = (hg * _sigmoid(hg) * hu).astype(BF16)
    y_ref[...] = jnp.dot(act, wd_ref[0].astype(BF16), preferred_element_type=F32)


def _expert_mlp(u2, slot_tok, block_expert, w_gate, w_up, w_down):
    n_blk = block_expert.shape[0]
    d = u2.shape[1]
    hdim = w_gate.shape[2]
    tok3 = slot_tok.reshape(n_blk, 1, MOE_BLOCK)
    grid_spec = pltpu.PrefetchScalarGridSpec(
        num_scalar_prefetch=1,
        grid=(n_blk,),
        in_specs=[pl.BlockSpec((1, 1, MOE_BLOCK), lambda i, be: (i, 0, 0), memory_space=pltpu.SMEM),
                  pl.BlockSpec((1, 1, MOE_BLOCK), lambda i, be: (jnp.minimum(i + 1, n_blk - 1), 0, 0),
                               memory_space=pltpu.SMEM),
                  pl.BlockSpec(memory_space=pl.ANY),
                  pl.BlockSpec((1, d, hdim), lambda i, be: (be[i], 0, 0)),
                  pl.BlockSpec((1, d, hdim), lambda i, be: (be[i], 0, 0)),
                  pl.BlockSpec((1, hdim, d), lambda i, be: (be[i], 0, 0))],
        out_specs=pl.BlockSpec((MOE_BLOCK, d), lambda i, be: (i, 0)),
        scratch_shapes=[pltpu.VMEM((2, MOE_BLOCK, d), F32), pltpu.SemaphoreType.DMA((2,))],
    )
    return pl.pallas_call(
        _expert_kernel,
        grid_spec=grid_spec,
        out_shape=jax.ShapeDtypeStruct((n_blk * MOE_BLOCK, d), F32),
        compiler_params=_cparams(("arbitrary",)),
        name="expert_mlp",
    )(block_expert, tok3, tok3, u2, w_gate, w_up, w_down)


def _combine_kernel(d1_ref, d2_ref, d1n_ref, d2n_ref, y_hbm, route_ref, h1_ref, g2_ref, lng_ref, lnb_ref,
                    o_ref, abuf, bbuf, sem_a, sem_b):
    i = pl.program_id(0)
    n = pl.num_programs(0)
    slot = i % 2
    rows = o_ref.shape[0]

    @pl.when(i == 0)
    def _():
        _start_row_gather(y_hbm, d1_ref, abuf, sem_a, 0, rows)
        _start_row_gather(y_hbm, d2_ref, bbuf, sem_b, 0, rows)

    @pl.when(i + 1 < n)
    def _():
        _start_row_gather(y_hbm, d1n_ref, abuf, sem_a, 1 - slot, rows)
        _start_row_gather(y_hbm, d2n_ref, bbuf, sem_b, 1 - slot, rows)

    _wait_row_gather(y_hbm, d1_ref, abuf, sem_a, slot, rows)
    _wait_row_gather(y_hbm, d2_ref, bbuf, sem_b, slot, rows)
    route = route_ref[...]
    f = abuf[slot] * route[:, ROUTE_G1:ROUTE_G1 + 1] + bbuf[slot] * route[:, ROUTE_G2:ROUTE_G2 + 1]
    o_ref[...] = _layer_norm(DEEPNORM_ALPHA * h1_ref[...] + g2_ref[0] * f, lng_ref[...], lnb_ref[...])


def _combine(y_sorted, dest1, dest2, route, h1, g2, ln_g, ln_b, tokens_per_batch):
    n, d = h1.shape
    t = 128
    nt = n // t
    per_b = tokens_per_batch // t
    d1 = dest1.reshape(nt, 1, t)
    d2 = dest2.reshape(nt, 1, t)
    cur = pl.BlockSpec((1, 1, t), lambda i: (i, 0, 0), memory_space=pltpu.SMEM)
    nxt = pl.BlockSpec((1, 1, t), lambda i: (jnp.minimum(i + 1, nt - 1), 0, 0), memory_space=pltpu.SMEM)
    small = lambda arr: pl.BlockSpec(arr.shape, lambda i: (0,) * arr.ndim)
    return pl.pallas_call(
        _combine_kernel,
        grid=(nt,),
        in_specs=[cur, cur, nxt, nxt, pl.BlockSpec(memory_space=pl.ANY),
                  pl.BlockSpec((t, LANES), lambda i: (i, 0)),
                  pl.BlockSpec((t, d), lambda i: (i, 0)),
                  pl.BlockSpec((1, 1, d), lambda i: (i // per_b, 0, 0)),
                  small(ln_g), small(ln_b)],
        out_specs=pl.BlockSpec((t, d), lambda i: (i, 0)),
        out_shape=jax.ShapeDtypeStruct((n, d), F32),
        scratch_shapes=[pltpu.VMEM((2, t, d), F32), pltpu.VMEM((2, t, d), F32),
                        pltpu.SemaphoreType.DMA((2,)), pltpu.SemaphoreType.DMA((2,))],
        compiler_params=_cparams(("arbitrary",)),
        name="combine",
    )(d1, d2, d1, d2, y_sorted, route, h1, g2, ln_g, ln_b)


def _block_diag2(w):
    z = jnp.zeros_like(w[0])
    return jnp.concatenate([jnp.concatenate([w[0], z], axis=1), jnp.concatenate([z, w[1]], axis=1)], axis=0)


def kernel(x, c, ctx, c_ctx, emb_ln_g, emb_ln_b, w_mod, b_mod, w_in, tshift_mu, rwkv_w0, rwkv_w2, rwkv_a0, rwkv_a2, rwkv_g2, rwkv_k_k, rwkv_k_a, rwkv_r_k, rwkv_gn_g, rwkv_gn_b, ret_decay, ret_gn_g, ret_gn_b, w_out, ln1_g, ln1_b, router_group, router_group_bias, router_expert, router_expert_bias, expert_w_gate, expert_w_up, expert_w_down, ln2_g, ln2_b):
    assert w_mod.shape[0] == 1, "written for DEPTH == 1 (context outputs are never emitted)"
    b, n_tok, d = x.shape
    n_ctx = ctx.shape[1]
    row = lambda v: v.reshape(1, -1)

    c_rows = jnp.zeros((SUBLANES, d), F32).at[:b].set(c).at[b].set(c_ctx)
    mod = _modulation(c_rows, w_mod[0], row(b_mod[0]))
    sh1, s1, g1, sh2, s2, g2 = [mod[:b, j * d:(j + 1) * d].reshape(b, 1, d) for j in range(6)]
    sh1c, s1c = [jnp.broadcast_to(mod[b, j * d:(j + 1) * d].reshape(1, 1, d), (b, 1, d)) for j in range(2)]

    w_in_bf16 = w_in[0].astype(BF16)
    pr, pt = _in_proj(x, row(emb_ln_g), row(emb_ln_b), s1, sh1, w_in_bf16)
    pr_c, pt_c = _in_proj(ctx, row(emb_ln_g), row(emb_ln_b), s1c, sh1c, w_in_bf16)

    prep_params = (row(tshift_mu[0]), row(rwkv_w0[0]), _block_diag2(rwkv_w2[0]), row(rwkv_a0[0]),
                   _block_diag2(rwkv_a2[0]), rwkv_g2[0], row(rwkv_k_k[0]), row(rwkv_k_a[0]), row(rwkv_r_k[0]),
                   _segment_ones(D_RWKV, RWKV_HEAD))
    lat = _rwkv_prepare(pr, prep_params, grid_shift=True)
    cx = _rwkv_prepare(pr_c, prep_params, grid_shift=False)
    r_l, v_l, kk_l, w_l, kd_l, bb_l, gate_l, bonus_l = lat
    r_c, v_c, kk_c, w_c, kd_c, bb_c, _, _ = cx

    y_dirs = _wkv7((r_l, v_l, kk_l, w_l, kd_l, bb_l), (r_c, v_c, kk_c, w_c, kd_c, bb_c), b, n_tok, n_ctx)
    y_f = y_dirs[0]
    y_b = y_dirs[1]

    cos_t, sin_t = _rope_tables(n_tok)
    t_f, t_b = _retention(pt, pt_c, ret_decay[0], cos_t, sin_t)

    wr = jnp.zeros((d, LANES), F32).at[:, :N_GROUPS].set(router_group[0])
    wr = wr.at[:, N_GROUPS:N_GROUPS + N_EXPERTS].set(router_expert[0])
    br = jnp.zeros((1, LANES), F32).at[0, :N_GROUPS].set(router_group_bias[0])
    br = br.at[0, N_GROUPS:N_GROUPS + N_EXPERTS].set(router_expert_bias[0].reshape(-1))
    vecs = (row(emb_ln_g), row(emb_ln_b), g1, s2, sh2, row(rwkv_gn_g[0]), row(rwkv_gn_b[0]),
            row(ret_gn_g[0]), row(ret_gn_b[0]), row(ln1_g[0]), row(ln1_b[0]), br)
    mats = (_segment_ones(D_RWKV, RWKV_HEAD), _segment_ones(D_RET, RET_HEAD), w_out[0].astype(BF16), wr)
    h1, u2, logits = _out_proj(x, y_f, y_b, bonus_l, gate_l, t_f, t_b, pt, vecs, mats)

    n_all = b * n_tok
    route, counts = _route(logits.reshape(n_all, LANES))

    e1 = route[:, ROUTE_E1].astype(jnp.int32)
    e2 = route[:, ROUTE_E2].astype(jnp.int32)
    cnt = counts[0, :N_EXPERTS].astype(jnp.int32)
    padded = ((cnt + MOE_BLOCK - 1) // MOE_BLOCK) * MOE_BLOCK
    pends = jnp.cumsum(padded)
    pstarts = pends - padded
    dest1 = pstarts[e1] + route[:, ROUTE_RANK1].astype(jnp.int32)
    dest2 = pstarts[e2] + route[:, ROUTE_RANK2].astype(jnp.int32)
    n_blk = -(-(n_all * 2) // MOE_BLOCK) + N_EXPERTS
    tok_ids = jnp.arange(n_all, dtype=jnp.int32)
    slot_tok = jnp.zeros((n_blk * MOE_BLOCK,), jnp.int32).at[dest1].set(tok_ids).at[dest2].set(tok_ids)
    block_start = jnp.arange(n_blk, dtype=jnp.int32) * MOE_BLOCK
    block_expert = jnp.minimum(jnp.searchsorted(pends, block_start, side='right'), N_EXPERTS - 1).astype(jnp.int32)

    y_sorted = _expert_mlp(u2.reshape(n_all, d), slot_tok, block_expert,
                           expert_w_gate[0], expert_w_up[0], expert_w_down[0])
    out = _combine(y_sorted, dest1, dest2, route, h1.reshape(n_all, d), g2, row(ln2_g[0]), row(ln2_b[0]), n_tok)
    return out.reshape(b, n_tok, d)
```

```python
import functools
import math

import jax
import jax.numpy as jnp
import numpy as np
from jax import lax
from jax.experimental import pallas as pl
from jax.experimental.pallas import tpu as pltpu

F32 = jnp.float32
BF16 = jnp.bfloat16
HIGHEST = lax.Precision.HIGHEST

GRID_W = 64
D_RWKV = 512
RWKV_HEAD = 64
RWKV_HEADS = D_RWKV // RWKV_HEAD
DECAY_LORA = 64
AAA_LORA = 64
GATE_LORA = 128
D_RET = 512
RET_HEADS = 4
RET_HEAD = D_RET // RET_HEADS
RET_CHUNK = 128
RWKV_COLS = 3 * D_RWKV + 2 * (DECAY_LORA + AAA_LORA) + GATE_LORA
RET_COLS = 4 * D_RET
N_GROUPS = 4
EXPERTS_PER_GROUP = 8
N_EXPERTS = N_GROUPS * EXPERTS_PER_GROUP
EXPERT_HIDDEN = 512
MOE_BLOCK = 128
ROPE_BASE = 10000.0
LN_EPS = 1e-5
RWKV_GN_EPS = 64e-5
RET_GN_EPS = 1e-5
DEEPNORM_ALPHA = 2.0 ** 0.25
EXP_NEG_HALF = math.exp(-0.5)

LANES = 128
SUBLANES = 8
VMEM_LIMIT_BYTES = 56 * 1024 * 1024

WKV_CHUNK = 64


EXPERT_ROWS = 256
COMBINE_ROWS = 128
GATHER_UNROLL = 8


def _store_token_tiles(ref, x):
    rows = x.shape[0]
    for j in range(x.shape[1] // LANES):
        ref[pl.ds(j, rows, stride=SUBLANES), :] = x[:, j * LANES:(j + 1) * LANES]


def _load_token_tiles(ref, rows):
    return jnp.concatenate([ref[pl.ds(j, rows, stride=SUBLANES), :] for j in range(SUBLANES)], axis=-1)


def _cparams(sem):
    return pltpu.CompilerParams(dimension_semantics=sem, vmem_limit_bytes=VMEM_LIMIT_BYTES)


def _layer_norm(x, g, b, eps=LN_EPS):
    mu = jnp.mean(x, axis=-1, keepdims=True)
    xc = x - mu
    var = jnp.mean(xc * xc, axis=-1, keepdims=True)
    return xc * lax.rsqrt(var + eps) * g + b


def _sigmoid(x):
    return 1.0 / (1.0 + jnp.exp(-x))


def _segsum(x, ones_bf16):
    hi = x.astype(BF16)
    r1 = x - hi.astype(F32)
    mid = r1.astype(BF16)
    lo = (r1 - mid.astype(F32)).astype(BF16)
    acc = jnp.dot(hi, ones_bf16, preferred_element_type=F32)
    acc = acc + jnp.dot(mid, ones_bf16, preferred_element_type=F32)
    return acc + jnp.dot(lo, ones_bf16, preferred_element_type=F32)


def _segment_ones(width, seg):
    idx = np.arange(width) // seg
    return jnp.asarray(idx[:, None] == idx[None, :], dtype=BF16)


def _mod_kernel(c_ref, w_ref, b_ref, o_ref):
    c = c_ref[...]
    sc = c * _sigmoid(c)
    o_ref[...] = jnp.dot(sc, w_ref[...], precision=HIGHEST, preferred_element_type=F32) + b_ref[...]


def _modulation(c_rows, w_mod, b_mod):
    rows, d = c_rows.shape
    n = w_mod.shape[1]
    tn = 1536
    return pl.pallas_call(
        _mod_kernel,
        grid=(n // tn,),
        in_specs=[pl.BlockSpec((rows, d), lambda j: (0, 0)),
                  pl.BlockSpec((d, tn), lambda j: (0, j)),
                  pl.BlockSpec((1, tn), lambda j: (0, j))],
        out_specs=pl.BlockSpec((rows, tn), lambda j: (0, j)),
        out_shape=jax.ShapeDtypeStruct((rows, n), F32),
        compiler_params=_cparams(("arbitrary",)),
        name="modulation",
    )(c_rows, w_mod, b_mod)


def _in_proj_kernel(x_ref, g_ref, b_ref, s_ref, sh_ref, w_ref, pr_ref, pt_ref):
    h = _layer_norm(x_ref[0], g_ref[...], b_ref[...])
    u = h * (1.0 + s_ref[0]) + sh_ref[0]
    p = jnp.dot(u.astype(BF16), w_ref[...], preferred_element_type=F32)
    pr_ref[0] = p[:, :RWKV_COLS]
    pt_ref[0] = p[:, RWKV_COLS:]


def _in_proj(x, ln_g, ln_b, s1, sh1, w_in_bf16):
    b, n, d = x.shape
    tm = 256
    cols = w_in_bf16.shape[1]
    return pl.pallas_call(
        _in_proj_kernel,
        grid=(b, n // tm),
        in_specs=[pl.BlockSpec((1, tm, d), lambda bi, i: (bi, i, 0)),
                  pl.BlockSpec((1, d), lambda bi, i: (0, 0)),
                  pl.BlockSpec((1, d), lambda bi, i: (0, 0)),
                  pl.BlockSpec((1, 1, d), lambda bi, i: (bi, 0, 0)),
                  pl.BlockSpec((1, 1, d), lambda bi, i: (bi, 0, 0)),
                  pl.BlockSpec((d, cols), lambda bi, i: (0, 0))],
        out_specs=[pl.BlockSpec((1, tm, RWKV_COLS), lambda bi, i: (bi, i, 0)),
                   pl.BlockSpec((1, tm, RET_COLS), lambda bi, i: (bi, i, 0))],
        out_shape=[jax.ShapeDtypeStruct((b, n, RWKV_COLS), F32),
                   jax.ShapeDtypeStruct((b, n, RET_COLS), F32)],
        compiler_params=_cparams(("arbitrary", "arbitrary")),
        name="in_proj",
    )(x, ln_g, ln_b, s1, sh1, w_in_bf16)


def _rwkv_prepare_kernel(cur_ref, prev_ref, next_ref, mu_ref, w0_ref, w2_ref, a0_ref, a2_ref, g2_ref,
                         kk_scale_ref, ka_ref, rk_ref, ones_ref,
                         r_ref, v_ref, kk_ref, w_ref, kd_ref, bb_ref, g_ref, bonus_ref,
                         *, grid_shift, n_tok):
    cur = cur_ref[0]
    t, c = cur.shape
    row = lax.broadcasted_iota(jnp.int32, (t, c), 0)
    lane = lax.broadcasted_iota(jnp.int32, (t, c), 1)
    prev_tok = pltpu.roll(cur, 1, 0)
    next_tok = pltpu.roll(cur, t - 1, 0)
    if grid_shift:
        col = row & (GRID_W - 1)
        tok = row + pl.program_id(1) * t
        left = jnp.where(col > 0, prev_tok, 0.0)
        right = jnp.where(col < GRID_W - 1, next_tok, 0.0)
        up = jnp.where(tok >= GRID_W, jnp.concatenate([prev_ref[0], cur[:t - GRID_W]], axis=0), 0.0)
        down = jnp.where(tok < n_tok - GRID_W, jnp.concatenate([cur[GRID_W:], next_ref[0]], axis=0), 0.0)
        cm = lane & 3
        shifted = jnp.where(cm == 0, left, jnp.where(cm == 1, right, jnp.where(cm == 2, up, down)))
    else:
        prev_tok = jnp.where(row > 0, prev_tok, 0.0)
        next_tok = jnp.where(row < t - 1, next_tok, 0.0)
        shifted = jnp.where((lane & 1) == 0, prev_tok, next_tok)
    pm = cur + mu_ref[...] * (shifted - cur)

    r = pm[:, 0:D_RWKV]
    k = pm[:, D_RWKV:2 * D_RWKV]
    v = pm[:, 2 * D_RWKV:3 * D_RWKV]
    o = 3 * D_RWKV
    lw = pm[:, o:o + 2 * DECAY_LORA]
    la = pm[:, o + 2 * DECAY_LORA:o + 2 * (DECAY_LORA + AAA_LORA)]
    lg = pm[:, o + 2 * (DECAY_LORA + AAA_LORA):]

    w = w0_ref[...] + jnp.dot(jnp.tanh(lw), w2_ref[...], precision=HIGHEST, preferred_element_type=F32)
    log_decay = -EXP_NEG_HALF * _sigmoid(w)
    a = _sigmoid(a0_ref[...] + jnp.dot(la, a2_ref[...], precision=HIGHEST, preferred_element_type=F32))
    gate = jnp.dot(_sigmoid(lg), g2_ref[...], precision=HIGHEST, preferred_element_type=F32)

    ones = ones_ref[...]
    kk_raw = k * kk_scale_ref[...]
    kk = kk_raw / jnp.maximum(jnp.sqrt(_segsum(kk_raw * kk_raw, ones)), 1e-12)
    ka = ka_ref[...]
    a0 = a[:, :D_RWKV]
    a1 = a[:, D_RWKV:]
    kd0 = k * (1.0 + (a0 - 1.0) * ka)
    kd1 = k * (1.0 + (a1 - 1.0) * ka)
    bonus = _segsum(r * (kd0 + kd1) * rk_ref[...], ones) * v

    r_ref[0] = r
    v_ref[0] = v
    kk_ref[0] = kk
    w_ref[0] = log_decay
    kd_ref[0] = jnp.concatenate([kd0, kd1], axis=-1)
    bb_ref[0] = jnp.concatenate([kk * a0, kk * a1], axis=-1)
    g_ref[0] = gate
    bonus_ref[0] = bonus


def _rwkv_prepare(pr, params, grid_shift):
    b, n, c = pr.shape
    t = 256
    if not grid_shift:
        assert n == t, "sequence token shift is written for a single tile"
    halo_blocks = n // GRID_W
    per_tile = t // GRID_W
    small = lambda shape: pl.BlockSpec(shape, lambda bi, i: (0,) * len(shape))
    tok_spec = lambda width: pl.BlockSpec((1, t, width), lambda bi, i: (bi, i, 0))
    out_widths = (D_RWKV, D_RWKV, D_RWKV, 2 * D_RWKV, 2 * D_RWKV, 2 * D_RWKV, D_RWKV, D_RWKV)
    kernel = functools.partial(_rwkv_prepare_kernel, grid_shift=grid_shift, n_tok=n)
    return pl.pallas_call(
        kernel,
        grid=(b, n // t),
        in_specs=[tok_spec(c),
                  pl.BlockSpec((1, GRID_W, c), lambda bi, i: (bi, jnp.maximum(i * per_tile - 1, 0), 0)),
                  pl.BlockSpec((1, GRID_W, c),
                               lambda bi, i: (bi, jnp.minimum((i + 1) * per_tile, halo_blocks - 1), 0)),
                  small((1, c)), small((1, 2 * D_RWKV)), small((2 * DECAY_LORA, 2 * D_RWKV)),
                  small((1, 2 * D_RWKV)), small((2 * AAA_LORA, 2 * D_RWKV)), small((GATE_LORA, D_RWKV)),
                  small((1, D_RWKV)), small((1, D_RWKV)), small((1, D_RWKV)), small((D_RWKV, D_RWKV))],
        out_specs=[tok_spec(wd) for wd in out_widths],
        out_shape=[jax.ShapeDtypeStruct((b, n, wd), F32) for wd in out_widths],
        compiler_params=_cparams(("arbitrary", "arbitrary")),
        name="rwkv_prepare",
    )(pr, pr, pr, *params)


def _bdot(a, b):
    return jnp.dot(a.astype(BF16), b.astype(BF16), preferred_element_type=F32)


def _bdot_nt(a, b):
    return lax.dot_general(a.astype(BF16), b.astype(BF16), (((1,), (1,)), ((), ())), preferred_element_type=F32)


def _bdot_tn(a, b):
    return lax.dot_general(a.astype(BF16), b.astype(BF16), (((0,), (0,)), ((), ())), preferred_element_type=F32)


def _wkv7_chunk_kernel(r_l, v_l, kk_l, lw_l, kd_l, bb_l, r_c, v_c, kk_c, lw_c, kd_c, bb_c, y_ref, state_ref,
                       *, n_ctx_chunks):
    c = WKV_CHUNK
    d = pl.program_id(0)
    n = pl.program_id(1)
    n_batch = r_l.shape[0]
    sign = 1 - 2 * d

    @pl.when(n == 0)
    def _():
        state_ref[...] = jnp.zeros_like(state_ref)

    is_ctx = n < n_ctx_chunks
    pick = lambda xc, xl: jnp.concatenate([jnp.where(is_ctx, xc[bi], xl[bi]) for bi in range(n_batch)], axis=-1)
    r = pick(r_c, r_l)
    v = pick(v_c, v_l)
    kk = pick(kk_c, kk_l)
    lw = pick(lw_c, lw_l)
    kd = pick(kd_c, kd_l)
    bb = pick(bb_c, bb_l)

    ti = lax.broadcasted_iota(jnp.int32, (c, c), 0)
    tj = lax.broadcasted_iota(jnp.int32, (c, c), 1)
    upto = ((tj - ti) * sign <= 0).astype(BF16)
    hi = lw.astype(BF16)
    r1 = lw - hi.astype(F32)
    mid = r1.astype(BF16)
    lo = (r1 - mid.astype(F32)).astype(BF16)
    cum = (jnp.dot(upto, hi, preferred_element_type=F32) + jnp.dot(upto, mid, preferred_element_type=F32)
           + jnp.dot(upto, lo, preferred_element_type=F32))
    tot = jnp.sum(lw, axis=0, keepdims=True)
    e_neg = jnp.exp(-cum)
    e_rem = jnp.exp(tot - cum)
    alpha = kk * jnp.exp(cum - lw)
    rho = r * jnp.exp(cum)
    beta = bb * e_neg
    kappa = kd * e_neg
    kappa_rem = kd * e_rem
    beta_rem = bb * e_rem
    g_chunk = jnp.exp(tot)

    p = 2 * c
    ri = lax.broadcasted_iota(jnp.int32, (p, p), 0)
    ci = lax.broadcasted_iota(jnp.int32, (p, p), 1)
    same_head = (ri >= c) == (ci >= c)
    ii = ri & (c - 1)
    jj = ci & (c - 1)
    earlier = same_head & ((jj - ii) * sign < 0)
    upto_self = earlier | (ri == ci)
    eye = (ri == ci).astype(F32)
    lane = lax.broadcasted_iota(jnp.int32, (c, p), 1)
    first = lane < RWKV_HEAD

    def stack(x):
        return jnp.concatenate([jnp.where(first, x, 0.0), jnp.where(first, 0.0, x)], axis=0)

    def unstack(x):
        return x[:c] + x[c:]

    pairs_per_batch = RWKV_HEADS // 2
    pairs = range(n_batch * pairs_per_batch)
    sls = [slice(hp * p, (hp + 1) * p) for hp in pairs]
    a_st = [stack(alpha[:, sl]) for sl in sls]
    r_st = [stack(rho[:, sl]) for sl in sls]
    k_st = [stack(kappa[:, sl]) for sl in sls]
    b_st = [stack(beta[:, sl]) for sl in sls]
    k2_st = [stack(kappa_rem[:, sl]) for sl in sls]
    b2_st = [stack(beta_rem[:, sl]) for sl in sls]
    v_st = [stack(v[:, sl]) for sl in sls]

    g = [_bdot_nt(jnp.concatenate([a_st[h], r_st[h]], axis=0), jnp.concatenate([k_st[h], b_st[h]], axis=0))
         for h in pairs]
    m1 = [jnp.where(earlier, g[h][:p, :p], 0.0) for h in pairs]
    m2 = [jnp.where(earlier, g[h][:p, p:], 0.0) for h in pairs]
    n1 = [jnp.where(upto_self, g[h][p:, :p], 0.0) for h in pairs]
    n2 = [jnp.where(upto_self, g[h][p:, p:], 0.0) for h in pairs]

    in_block = (ii >> 3) == (jj >> 3)
    pw = [-jnp.where(in_block, m2[h], 0.0) for h in pairs]
    inv = [eye + pw[h] for h in pairs]
    for _ in range(2):
        pw = [_bdot(pw[h], pw[h]) for h in pairs]
        inv = [inv[h] + _bdot(inv[h], pw[h]) for h in pairs]
    for sh in (3, 4, 5):
        off = ((ii >> (sh + 1)) == (jj >> (sh + 1))) & ((ii >> sh) != (jj >> sh))
        left = [_bdot(inv[h], jnp.where(off, m2[h], 0.0)) for h in pairs]
        inv = [inv[h] - _bdot(left[h], inv[h]) for h in pairs]

    m1v = [_bdot(m1[h], v_st[h]) for h in pairs]
    n1v = [_bdot(n1[h], v_st[h]) for h in pairs]
    au = [_bdot(inv[h], jnp.concatenate([a_st[h], m1v[h]], axis=1)) for h in pairs]
    nn = [_bdot(n2[h], au[h]) for h in pairs]
    pc = [_bdot_tn(b2_st[h], au[h][:, :p]) for h in pairs]
    qc_t = [_bdot_tn(jnp.concatenate([v_st[h], -au[h][:, p:]], axis=0),
                     jnp.concatenate([k2_st[h], b2_st[h]], axis=0)) for h in pairs]
    s0 = [state_ref[h] for h in pairs]
    y = [_bdot_nt(unstack(r_st[h] - nn[h][:, :p]), s0[h]) + unstack(n1v[h] - nn[h][:, p:]) for h in pairs]
    s_dec = [_bdot_nt(s0[h], pc[h]) for h in pairs]
    for h in pairs:
        y_ref[0, h // pairs_per_batch, :, sls[h % pairs_per_batch]] = y[h]
        state_ref[h] = s0[h] * g_chunk[:, sls[h]] - s_dec[h] + qc_t[h]


def _wkv7(lat, ctx, b, n_tok, n_ctx):
    c = WKV_CHUNK
    ncx = n_ctx // c
    nl = n_tok // c
    lat_idx = lambda d, n: jnp.where(d == 0, jnp.maximum(n - ncx, 0), nl - 1 - jnp.maximum(n - ncx, 0))
    ctx_idx = lambda d, n: jnp.where(d == 0, jnp.minimum(n, ncx - 1), ncx - 1 - jnp.minimum(n, ncx - 1))
    shared = lambda idx: pl.BlockSpec((b, c, D_RWKV), lambda d, n: (0, idx(d, n), 0))
    per_dir = lambda idx: pl.BlockSpec((b, c, D_RWKV), lambda d, n: (0, idx(d, n), d))
    specs = lambda idx: [shared(idx), shared(idx), shared(idx), per_dir(idx), per_dir(idx), per_dir(idx)]
    return pl.pallas_call(
        functools.partial(_wkv7_chunk_kernel, n_ctx_chunks=ncx),
        grid=(2, ncx + nl),
        in_specs=specs(lat_idx) + specs(ctx_idx),
        out_specs=pl.BlockSpec((1, b, c, D_RWKV), lambda d, n: (d, 0, lat_idx(d, n), 0)),
        out_shape=jax.ShapeDtypeStruct((2, b, n_tok, D_RWKV), F32),
        scratch_shapes=[pltpu.VMEM((b * RWKV_HEADS // 2, 2 * RWKV_HEAD, 2 * RWKV_HEAD), F32)],
        compiler_params=_cparams(("arbitrary", "arbitrary")),
        name="wkv7_chunk",
    )(*lat, *ctx)


def _rope(z, cos_t, sin_t):
    lane = lax.broadcasted_iota(jnp.int32, z.shape, 1)
    half = RET_HEAD // 4
    partner = jnp.where((lane & (2 * half - 1)) < half, pltpu.roll(z, RET_HEAD - half, 1), pltpu.roll(z, half, 1))
    return z * cos_t + partner * sin_t


def _retention_kernel(dec_ref, fwd_ref, bwd_ref, ctx_ref, cosf_ref, sinf_ref, cosb_ref, sinb_ref,
                      yf_ref, yb_ref, state_ref):
    c = RET_CHUNK
    scale = RET_HEAD ** -0.5
    ii = lax.broadcasted_iota(jnp.int32, (c, c), 0)
    jj = lax.broadcasted_iota(jnp.int32, (c, c), 1)
    pos = lax.broadcasted_iota(jnp.int32, (c, RET_HEAD), 0).astype(F32)
    n_ctx_chunks = ctx_ref.shape[1] // c

    def head_slices(ref_val, h):
        q = ref_val[:, h * RET_HEAD:(h + 1) * RET_HEAD]
        k = ref_val[:, D_RET + h * RET_HEAD:D_RET + (h + 1) * RET_HEAD]
        v = ref_val[:, 2 * D_RET + h * RET_HEAD:2 * D_RET + (h + 1) * RET_HEAD]
        return q, k, v

    for d in range(2):
        blk = fwd_ref[0] if d == 0 else bwd_ref[0]
        cos_t = cosf_ref[...] if d == 0 else cosb_ref[...]
        sin_t = sinf_ref[...] if d == 0 else sinb_ref[...]
        for h in range(RET_HEADS):
            x = jnp.full((1, RET_HEAD), dec_ref[d, h], F32)
            lg = -(jnp.maximum(x, 0.0) + jnp.log(1.0 + jnp.exp(-jnp.abs(x))))
            chunk_decay = jnp.exp(lg * float(c))
            tail = jnp.exp(lg * ((c - 1.0 - pos) if d == 0 else pos))
            head = jnp.exp(lg * ((pos + 1.0) if d == 0 else (c - pos)))

            @pl.when(pl.program_id(1) == 0)
            def _():
                s = jnp.zeros((RET_HEAD, RET_HEAD), F32)
                order = range(n_ctx_chunks) if d == 0 else range(n_ctx_chunks - 1, -1, -1)
                for cc in order:
                    _, kc, vc = head_slices(ctx_ref[0, cc * c:(cc + 1) * c, :], h)
                    kw = (kc * scale * tail).astype(BF16)
                    u = lax.dot_general(kw, vc.astype(BF16), (((0,), (0,)), ((), ())), preferred_element_type=F32)
                    s = s * chunk_decay + u
                state_ref[d, h] = s

            q, k, v = head_slices(blk, h)
            q = _rope(q, cos_t, sin_t)
            k = _rope(k, cos_t, sin_t) * scale
            s = state_ref[d, h]
            scores = lax.dot_general(q.astype(BF16), k.astype(BF16), (((1,), (1,)), ((), ())),
                                     preferred_element_type=F32)
            rel = (ii - jj) if d == 0 else (jj - ii)
            mask = (rel >= 0) if d == 0 else (rel > 0)
            dmat = jnp.where(mask, jnp.exp(lg * jnp.maximum(rel, 0).astype(F32)), 0.0)
            v_bf = v.astype(BF16)
            inner = jnp.dot((scores * dmat).astype(BF16), v_bf, preferred_element_type=F32)
            cross = jnp.dot((q * head).astype(BF16), s.astype(BF16), preferred_element_type=F32)
            u = lax.dot_general((k * tail).astype(BF16), v_bf, (((0,), (0,)), ((), ())), preferred_element_type=F32)
            state_ref[d, h] = s * chunk_decay + u
            out_ref = yf_ref if d == 0 else yb_ref
            out_ref[0, :, h * RET_HEAD:(h + 1) * RET_HEAD] = inner + cross


def _retention(pt, pt_ctx, ret_decay, cos_t, sin_t):
    b, n, _ = pt.shape
    c = RET_CHUNK
    nc = n // c
    qkv = 3 * D_RET
    fwd = lambda bi, i: (bi, i, 0)
    bwd = lambda bi, i: (bi, nc - 1 - i, 0)
    return pl.pallas_call(
        _retention_kernel,
        grid=(b, nc),
        in_specs=[pl.BlockSpec(memory_space=pltpu.SMEM),
                  pl.BlockSpec((1, c, qkv), fwd),
                  pl.BlockSpec((1, c, qkv), bwd),
                  pl.BlockSpec((1, pt_ctx.shape[1], qkv), lambda bi, i: (bi, 0, 0)),
                  pl.BlockSpec((c, RET_HEAD), lambda bi, i: (i, 0)),
                  pl.BlockSpec((c, RET_HEAD), lambda bi, i: (i, 0)),
                  pl.BlockSpec((c, RET_HEAD), lambda bi, i: (nc - 1 - i, 0)),
                  pl.BlockSpec((c, RET_HEAD), lambda bi, i: (nc - 1 - i, 0))],
        out_specs=[pl.BlockSpec((1, c, D_RET), fwd), pl.BlockSpec((1, c, D_RET), bwd)],
        out_shape=[jax.ShapeDtypeStruct((b, n, D_RET), F32), jax.ShapeDtypeStruct((b, n, D_RET), F32)],
        scratch_shapes=[pltpu.VMEM((2, RET_HEADS, RET_HEAD, RET_HEAD), F32)],
        compiler_params=_cparams(("arbitrary", "arbitrary")),
        name="retention",
    )(ret_decay, pt, pt, pt_ctx, cos_t, sin_t, cos_t, sin_t)


def _rope_tables(n_tok):
    nf = RET_HEAD // 4
    lane = np.arange(RET_HEAD)
    inv = ROPE_BASE ** (-jnp.arange(nf, dtype=F32) / nf)
    t = jnp.arange(n_tok)
    pos = jnp.where((lane // (2 * nf) == 0)[None, :], (t // GRID_W)[:, None], (t % GRID_W)[:, None]).astype(F32)
    ang = pos * inv[lane % nf][None, :]
    sign = jnp.where((lane % (2 * nf)) < nf, -1.0, 1.0).astype(F32)
    return jnp.cos(ang), jnp.sin(ang) * sign[None, :]


def _group_norm(y, ones, seg, eps, g, b):
    mu = _segsum(y, ones) * (1.0 / seg)
    yc = y - mu
    var = _segsum(yc * yc, ones) * (1.0 / seg)
    return yc * lax.rsqrt(var + eps) * g + b


def _out_proj_kernel(x_ref, yf_ref, yb_ref, bonus_ref, gate_ref, tf_ref, tb_ref, gt_ref,
                     embg_ref, embb_ref, g1_ref, s2_ref, sh2_ref, rgn_g_ref, rgn_b_ref, tgn_g_ref, tgn_b_ref,
                     ones_r_ref, ones_t_ref, wout_ref, ln1g_ref, ln1b_ref, wr_ref, br_ref,
                     h1_ref, u2_ref, logit_ref):
    y = yf_ref[0] + yb_ref[0]
    o_rwkv = _group_norm(y, ones_r_ref[...], RWKV_HEAD, RWKV_GN_EPS, rgn_g_ref[...], rgn_b_ref[...])
    o_rwkv = (o_rwkv + bonus_ref[0]) * gate_ref[0]
    yt = tf_ref[0] + tb_ref[0]
    gt = gt_ref[0]
    o_ret = _group_norm(yt, ones_t_ref[...], RET_HEAD, RET_GN_EPS, tgn_g_ref[...], tgn_b_ref[...])
    o_ret = o_ret * (gt * _sigmoid(gt))
    cat = jnp.concatenate([o_rwkv, o_ret], axis=-1).astype(BF16)
    mix = jnp.dot(cat, wout_ref[...], preferred_element_type=F32)
    h = _layer_norm(x_ref[0], embg_ref[...], embb_ref[...])
    h1 = _layer_norm(DEEPNORM_ALPHA * h + g1_ref[0] * mix, ln1g_ref[...], ln1b_ref[...])
    u2 = h1 * (1.0 + s2_ref[0]) + sh2_ref[0]
    h1_ref[0] = h1
    _store_token_tiles(u2_ref, u2)
    logit_ref[0] = jnp.dot(u2, wr_ref[...], precision=HIGHEST, preferred_element_type=F32) + br_ref[...]


def _out_proj(x, y_f, y_b, bonus, gate, t_f, t_b, pt, vecs, mats):
    b, n, d = x.shape
    t = 256
    tok = lambda width: pl.BlockSpec((1, t, width), lambda bi, i: (bi, i, 0))
    per_b = pl.BlockSpec((1, 1, d), lambda bi, i: (bi, 0, 0))
    small = lambda arr: pl.BlockSpec(arr.shape, lambda bi, i: (0,) * arr.ndim)
    (embg, embb, g1, s2, sh2, rgn_g, rgn_b, tgn_g, tgn_b, ln1g, ln1b, br) = vecs
    (ones_r, ones_t, wout, wr) = mats
    gt_spec = pl.BlockSpec((1, t, D_RET), lambda bi, i: (bi, i, 3))
    args = (x, y_f, y_b, bonus, gate, t_f, t_b, pt, embg, embb, g1, s2, sh2, rgn_g, rgn_b, tgn_g, tgn_b,
            ones_r, ones_t, wout, ln1g, ln1b, wr, br)
    in_specs = [tok(d)] + [tok(D_RWKV)] * 6 + [gt_spec, small(embg), small(embb), per_b, per_b, per_b,
                                                small(rgn_g), small(rgn_b), small(tgn_g), small(tgn_b),
                                                small(ones_r), small(ones_t), small(wout), small(ln1g),
                                                small(ln1b), small(wr), small(br)]
    return pl.pallas_call(
        _out_proj_kernel,
        grid=(b, n // t),
        in_specs=in_specs,
        out_specs=[tok(d), pl.BlockSpec((t * SUBLANES, LANES), lambda bi, i: (bi * (n // t) + i, 0)), tok(LANES)],
        out_shape=[jax.ShapeDtypeStruct((b, n, d), F32), jax.ShapeDtypeStruct((b * n * SUBLANES, LANES), F32),
                   jax.ShapeDtypeStruct((b, n, LANES), F32)],
        compiler_params=_cparams(("arbitrary", "arbitrary")),
        name="out_proj",
    )(*args)


ROUTE_E1, ROUTE_E2, ROUTE_G1, ROUTE_G2, ROUTE_RANK1, ROUTE_RANK2 = range(6)


def _lane_argmax(x, valid, lane):
    m = jnp.max(jnp.where(valid, x, -jnp.inf), axis=-1, keepdims=True)
    idx = jnp.min(jnp.where(valid & (x == m), lane, float(LANES)), axis=-1, keepdims=True)
    return m, idx


def _route_kernel(logit_ref, route_ref, count_ref, carry_ref):
    @pl.when(pl.program_id(0) == 0)
    def _():
        carry_ref[...] = jnp.zeros_like(carry_ref)

    lg = logit_ref[...]
    t = lg.shape[0]
    lane = lax.broadcasted_iota(jnp.int32, lg.shape, 1).astype(F32)
    gmask = lane < N_GROUPS
    gmax = jnp.max(jnp.where(gmask, lg, -jnp.inf), axis=-1, keepdims=True)
    gexp = jnp.where(gmask, jnp.exp(lg - gmax), 0.0)
    gp = gexp / jnp.sum(gexp, axis=-1, keepdims=True)
    g_w, g_i = _lane_argmax(gp, gmask, lane)

    lo = N_GROUPS + EXPERTS_PER_GROUP * g_i
    emask = (lane >= lo) & (lane < lo + EXPERTS_PER_GROUP)
    emax = jnp.max(jnp.where(emask, lg, -jnp.inf), axis=-1, keepdims=True)
    eexp = jnp.where(emask, jnp.exp(lg - emax), 0.0)
    ep = eexp / jnp.sum(eexp, axis=-1, keepdims=True)
    p1, i1 = _lane_argmax(ep, emask, lane)
    p2, i2 = _lane_argmax(ep, emask & (lane != i1), lane)
    denom = p1 + p2
    gate1 = g_w * p1 / denom
    gate2 = g_w * p2 / denom
    e1 = i1 - N_GROUPS
    e2 = i2 - N_GROUPS

    oh1 = (lane == e1).astype(F32)
    oh2 = (lane == e2).astype(F32)
    cnt = oh1 + oh2
    ri = lax.broadcasted_iota(jnp.int32, (t, t), 0)
    ci = lax.broadcasted_iota(jnp.int32, (t, t), 1)
    before = (ci < ri).astype(BF16)
    seen = jnp.dot(before, cnt.astype(BF16), preferred_element_type=F32) + carry_ref[0:1, :]
    rank1 = jnp.sum(oh1 * seen, axis=-1, keepdims=True)
    rank2 = jnp.sum(oh2 * seen, axis=-1, keepdims=True)
    carry_ref[0:1, :] = carry_ref[0:1, :] + jnp.sum(cnt, axis=0, keepdims=True)

    out = jnp.zeros(lg.shape, F32)
    for slot, val in ((ROUTE_E1, e1.astype(F32)), (ROUTE_E2, e2.astype(F32)), (ROUTE_G1, gate1),
                      (ROUTE_G2, gate2), (ROUTE_RANK1, rank1), (ROUTE_RANK2, rank2)):
        out = jnp.where(lane == slot, val, out)
    route_ref[...] = out
    count_ref[...] = carry_ref[...]


def _route(logits):
    n = logits.shape[0]
    t = 256
    return pl.pallas_call(
        _route_kernel,
        grid=(n // t,),
        in_specs=[pl.BlockSpec((t, LANES), lambda i: (i, 0))],
        out_specs=[pl.BlockSpec((t, LANES), lambda i: (i, 0)), pl.BlockSpec((SUBLANES, LANES), lambda i: (0, 0))],
        out_shape=[jax.ShapeDtypeStruct((n, LANES), F32), jax.ShapeDtypeStruct((SUBLANES, LANES), F32)],
        scratch_shapes=[pltpu.VMEM((SUBLANES, LANES), F32)],
        compiler_params=_cparams(("arbitrary",)),
        name="route",
    )(logits)


def _tile_gather_copy(src_hbm, idx_ref, buf, sem, slot, r):
    dst = buf.at[slot, pl.ds(pl.multiple_of(r * SUBLANES, SUBLANES), SUBLANES), :]
    return pltpu.make_async_copy(src_hbm.at[idx_ref[0, 0, r]], dst, sem.at[slot])


def _start_tile_gather(src_hbm, idx_ref, buf, sem, slot, rows):
    def body(r, carry):
        _tile_gather_copy(src_hbm, idx_ref, buf, sem, slot, r).start()
        return carry
    lax.fori_loop(0, rows, body, 0, unroll=GATHER_UNROLL)


def _wait_tile_gather(src_hbm, idx_ref, buf, sem, slot, rows):
    def body(r, carry):
        _tile_gather_copy(src_hbm, idx_ref, buf, sem, slot, r).wait()
        return carry
    lax.fori_loop(0, rows, body, 0, unroll=GATHER_UNROLL)


def _expert_kernel(blk_expert_ref, tok_ref, tok_next_ref, u_hbm, wg_ref, wu_ref, wd_ref, y_ref,
                   xbuf, sem, wg_s, wu_s, wd_s):
    i = pl.program_id(0)
    n = pl.num_programs(0)
    slot = i % 2

    @pl.when(i == 0)
    def _():
        _start_tile_gather(u_hbm, tok_ref, xbuf, sem, 0, EXPERT_ROWS)

    @pl.when(i + 1 < n)
    def _():
        _start_tile_gather(u_hbm, tok_next_ref, xbuf, sem, 1 - slot, EXPERT_ROWS)

    @pl.when((i == 0) | (blk_expert_ref[i] != blk_expert_ref[jnp.maximum(i - 1, 0)]))
    def _():
        wg_s[...] = wg_ref[0].astype(BF16)
        wu_s[...] = wu_ref[0].astype(BF16)
        wd_s[...] = wd_ref[0].astype(BF16)

    _wait_tile_gather(u_hbm, tok_ref, xbuf, sem, slot, EXPERT_ROWS)
    x = _load_token_tiles(xbuf.at[slot], EXPERT_ROWS).astype(BF16)
    hg = jnp.dot(x, wg_s[...], preferred_element_type=F32)
    hu = jnp.dot(x, wu_s[...], preferred_element_type=F32)
    act = (hg * _sigmoid(hg) * hu).astype(BF16)
    _store_token_tiles(y_ref, jnp.dot(act, wd_s[...], preferred_element_type=F32))


def _expert_mlp(u2_tiles, slot_tok, block_expert, w_gate, w_up, w_down):
    n_blk = block_expert.shape[0]
    d = w_gate.shape[1]
    hdim = w_gate.shape[2]
    rows = EXPERT_ROWS
    tok3 = slot_tok.reshape(n_blk, 1, rows)
    grid_spec = pltpu.PrefetchScalarGridSpec(
        num_scalar_prefetch=1,
        grid=(n_blk,),
        in_specs=[pl.BlockSpec((1, 1, rows), lambda i, be: (i, 0, 0), memory_space=pltpu.SMEM),
                  pl.BlockSpec((1, 1, rows), lambda i, be: (jnp.minimum(i + 1, n_blk - 1), 0, 0),
                               memory_space=pltpu.SMEM),
                  pl.BlockSpec(memory_space=pl.ANY),
                  pl.BlockSpec((1, d, hdim), lambda i, be: (be[i], 0, 0)),
                  pl.BlockSpec((1, d, hdim), lambda i, be: (be[i], 0, 0)),
                  pl.BlockSpec((1, hdim, d), lambda i, be: (be[i], 0, 0))],
        out_specs=pl.BlockSpec((rows * SUBLANES, LANES), lambda i, be: (i, 0)),
        scratch_shapes=[pltpu.VMEM((2, rows * SUBLANES, LANES), F32), pltpu.SemaphoreType.DMA((2,)),
                        pltpu.VMEM((d, hdim), BF16), pltpu.VMEM((d, hdim), BF16), pltpu.VMEM((hdim, d), BF16)],
    )
    return pl.pallas_call(
        _expert_kernel,
        grid_spec=grid_spec,
        out_shape=jax.ShapeDtypeStruct((n_blk * rows * SUBLANES, LANES), F32),
        compiler_params=_cparams(("arbitrary",)),
        name="expert_mlp",
    )(block_expert, tok3, tok3, u2_tiles, w_gate, w_up, w_down)


def _combine_kernel(d1_ref, d2_ref, d1n_ref, d2n_ref, y_hbm, route_ref, h1_ref, g2_ref, lng_ref, lnb_ref,
                    o_ref, abuf, bbuf, sem_a, sem_b):
    i = pl.program_id(0)
    n = pl.num_programs(0)
    slot = i % 2
    rows = o_ref.shape[0]

    @pl.when(i == 0)
    def _():
        _start_tile_gather(y_hbm, d1_ref, abuf, sem_a, 0, rows)
        _start_tile_gather(y_hbm, d2_ref, bbuf, sem_b, 0, rows)

    @pl.when(i + 1 < n)
    def _():
        _start_tile_gather(y_hbm, d1n_ref, abuf, sem_a, 1 - slot, rows)
        _start_tile_gather(y_hbm, d2n_ref, bbuf, sem_b, 1 - slot, rows)

    _wait_tile_gather(y_hbm, d1_ref, abuf, sem_a, slot, rows)
    _wait_tile_gather(y_hbm, d2_ref, bbuf, sem_b, slot, rows)
    route = route_ref[...]
    f = (_load_token_tiles(abuf.at[slot], rows) * route[:, ROUTE_G1:ROUTE_G1 + 1]
         + _load_token_tiles(bbuf.at[slot], rows) * route[:, ROUTE_G2:ROUTE_G2 + 1])
    o_ref[...] = _layer_norm(DEEPNORM_ALPHA * h1_ref[...] + g2_ref[0] * f, lng_ref[...], lnb_ref[...])


def _combine(y_tiles, dest1, dest2, route, h1, g2, ln_g, ln_b, tokens_per_batch):
    n, d = h1.shape
    t = COMBINE_ROWS
    nt = n // t
    per_b = tokens_per_batch // t
    d1 = dest1.reshape(nt, 1, t)
    d2 = dest2.reshape(nt, 1, t)
    cur = pl.BlockSpec((1, 1, t), lambda i: (i, 0, 0), memory_space=pltpu.SMEM)
    nxt = pl.BlockSpec((1, 1, t), lambda i: (jnp.minimum(i + 1, nt - 1), 0, 0), memory_space=pltpu.SMEM)
    small = lambda arr: pl.BlockSpec(arr.shape, lambda i: (0,) * arr.ndim)
    return pl.pallas_call(
        _combine_kernel,
        grid=(nt,),
        in_specs=[cur, cur, nxt, nxt, pl.BlockSpec(memory_space=pl.ANY),
                  pl.BlockSpec((t, LANES), lambda i: (i, 0)),
                  pl.BlockSpec((t, d), lambda i: (i, 0)),
                  pl.BlockSpec((1, 1, d), lambda i: (i // per_b, 0, 0)),
                  small(ln_g), small(ln_b)],
        out_specs=pl.BlockSpec((t, d), lambda i: (i, 0)),
        out_shape=jax.ShapeDtypeStruct((n, d), F32),
        scratch_shapes=[pltpu.VMEM((2, t * SUBLANES, LANES), F32), pltpu.VMEM((2, t * SUBLANES, LANES), F32),
                        pltpu.SemaphoreType.DMA((2,)), pltpu.SemaphoreType.DMA((2,))],
        compiler_params=_cparams(("arbitrary",)),
        name="combine",
    )(d1, d2, d1, d2, y_tiles.reshape(-1, SUBLANES, LANES), route, h1, g2, ln_g, ln_b)


def _block_diag2(w):
    z = jnp.zeros_like(w[0])
    return jnp.concatenate([jnp.concatenate([w[0], z], axis=1), jnp.concatenate([z, w[1]], axis=1)], axis=0)


def kernel(x, c, ctx, c_ctx, emb_ln_g, emb_ln_b, w_mod, b_mod, w_in, tshift_mu, rwkv_w0, rwkv_w2, rwkv_a0, rwkv_a2, rwkv_g2, rwkv_k_k, rwkv_k_a, rwkv_r_k, rwkv_gn_g, rwkv_gn_b, ret_decay, ret_gn_g, ret_gn_b, w_out, ln1_g, ln1_b, router_group, router_group_bias, router_expert, router_expert_bias, expert_w_gate, expert_w_up, expert_w_down, ln2_g, ln2_b):
    assert w_mod.shape[0] == 1, "written for DEPTH == 1 (context outputs are never emitted)"
    b, n_tok, d = x.shape
    n_ctx = ctx.shape[1]
    row = lambda v: v.reshape(1, -1)

    c_rows = jnp.zeros((SUBLANES, d), F32).at[:b].set(c).at[b].set(c_ctx)
    mod = _modulation(c_rows, w_mod[0], row(b_mod[0]))
    sh1, s1, g1, sh2, s2, g2 = [mod[:b, j * d:(j + 1) * d].reshape(b, 1, d) for j in range(6)]
    sh1c, s1c = [jnp.broadcast_to(mod[b, j * d:(j + 1) * d].reshape(1, 1, d), (b, 1, d)) for j in range(2)]

    w_in_bf16 = w_in[0].astype(BF16)
    pr, pt = _in_proj(x, row(emb_ln_g), row(emb_ln_b), s1, sh1, w_in_bf16)
    pr_c, pt_c = _in_proj(ctx, row(emb_ln_g), row(emb_ln_b), s1c, sh1c, w_in_bf16)

    prep_params = (row(tshift_mu[0]), row(rwkv_w0[0]), _block_diag2(rwkv_w2[0]), row(rwkv_a0[0]),
                   _block_diag2(rwkv_a2[0]), rwkv_g2[0], row(rwkv_k_k[0]), row(rwkv_k_a[0]), row(rwkv_r_k[0]),
                   _segment_ones(D_RWKV, RWKV_HEAD))
    lat = _rwkv_prepare(pr, prep_params, grid_shift=True)
    cx = _rwkv_prepare(pr_c, prep_params, grid_shift=False)
    r_l, v_l, kk_l, w_l, kd_l, bb_l, gate_l, bonus_l = lat
    r_c, v_c, kk_c, w_c, kd_c, bb_c, _, _ = cx

    y_dirs = _wkv7((r_l, v_l, kk_l, w_l, kd_l, bb_l), (r_c, v_c, kk_c, w_c, kd_c, bb_c), b, n_tok, n_ctx)
    y_f = y_dirs[0]
    y_b = y_dirs[1]

    cos_t, sin_t = _rope_tables(n_tok)
    t_f, t_b = _retention(pt, pt_c, ret_decay[0], cos_t, sin_t)

    wr = jnp.zeros((d, LANES), F32).at[:, :N_GROUPS].set(router_group[0])
    wr = wr.at[:, N_GROUPS:N_GROUPS + N_EXPERTS].set(router_expert[0])
    br = jnp.zeros((1, LANES), F32).at[0, :N_GROUPS].set(router_group_bias[0])
    br = br.at[0, N_GROUPS:N_GROUPS + N_EXPERTS].set(router_expert_bias[0].reshape(-1))
    vecs = (row(emb_ln_g), row(emb_ln_b), g1, s2, sh2, row(rwkv_gn_g[0]), row(rwkv_gn_b[0]),
            row(ret_gn_g[0]), row(ret_gn_b[0]), row(ln1_g[0]), row(ln1_b[0]), br)
    mats = (_segment_ones(D_RWKV, RWKV_HEAD), _segment_ones(D_RET, RET_HEAD), w_out[0].astype(BF16), wr)
    h1, u2, logits = _out_proj(x, y_f, y_b, bonus_l, gate_l, t_f, t_b, pt, vecs, mats)

    n_all = b * n_tok
    route, counts = _route(logits.reshape(n_all, LANES))

    e1 = route[:, ROUTE_E1].astype(jnp.int32)
    e2 = route[:, ROUTE_E2].astype(jnp.int32)
    cnt = counts[0, :N_EXPERTS].astype(jnp.int32)
    padded = ((cnt + EXPERT_ROWS - 1) // EXPERT_ROWS) * EXPERT_ROWS
    pends = jnp.cumsum(padded)
    pstarts = pends - padded
    expert_ids = jnp.arange(N_EXPERTS, dtype=jnp.int32)
    start_of = lambda e: jnp.sum(jnp.where(e[:, None] == expert_ids[None, :], pstarts[None, :], 0), axis=1)
    dest1 = start_of(e1) + route[:, ROUTE_RANK1].astype(jnp.int32)
    dest2 = start_of(e2) + route[:, ROUTE_RANK2].astype(jnp.int32)
    n_blk = -(-(n_all * 2) // EXPERT_ROWS) + N_EXPERTS
    tok_ids = jnp.arange(n_all, dtype=jnp.int32)
    slot_tok = jnp.zeros((n_blk * EXPERT_ROWS,), jnp.int32).at[dest1].set(tok_ids).at[dest2].set(tok_ids)
    block_start = jnp.arange(n_blk, dtype=jnp.int32) * EXPERT_ROWS
    block_expert = jnp.minimum(jnp.sum((block_start[:, None] >= pends[None, :]).astype(jnp.int32), axis=1),
                               N_EXPERTS - 1)

    y_tiles = _expert_mlp(u2.reshape(n_all, SUBLANES, LANES), slot_tok, block_expert,
                          expert_w_gate[0], expert_w_up[0], expert_w_down[0])
    out = _combine(y_tiles, dest1, dest2, route, h1.reshape(n_all, d), g2, row(ln2_g[0]), row(ln2_b[0]), n_tok)
    return out.reshape(b, n_tok, d)
```

```python
import functools
import math

import jax
import jax.numpy as jnp
import numpy as np
from jax import lax
from jax.experimental import pallas as pl
from jax.experimental.pallas import tpu as pltpu

F32 = jnp.float32
BF16 = jnp.bfloat16
HIGHEST = lax.Precision.HIGHEST

GRID_W = 64
D_RWKV = 512
RWKV_HEAD = 64
RWKV_HEADS = D_RWKV // RWKV_HEAD
DECAY_LORA = 64
AAA_LORA = 64
GATE_LORA = 128
D_RET = 512
RET_HEADS = 4
RET_HEAD = D_RET // RET_HEADS
RET_CHUNK = 128
RWKV_COLS = 3 * D_RWKV + 2 * (DECAY_LORA + AAA_LORA) + GATE_LORA
RET_COLS = 4 * D_RET
N_GROUPS = 4
EXPERTS_PER_GROUP = 8
N_EXPERTS = N_GROUPS * EXPERTS_PER_GROUP
EXPERT_HIDDEN = 512
MOE_BLOCK = 128
ROPE_BASE = 10000.0
LN_EPS = 1e-5
RWKV_GN_EPS = 64e-5
RET_GN_EPS = 1e-5
DEEPNORM_ALPHA = 2.0 ** 0.25
EXP_NEG_HALF = math.exp(-0.5)

LANES = 128
SUBLANES = 8
VMEM_LIMIT_BYTES = 56 * 1024 * 1024

WKV_CHUNK = 64


EXPERT_ROWS = 256
COMBINE_ROWS = 128
GATHER_UNROLL = 8
COMBINE_PRIORITIES = (0, 1)


def _store_token_tiles(ref, x):
    rows = x.shape[0]
    for j in range(x.shape[1] // LANES):
        ref[pl.ds(j, rows, stride=SUBLANES), :] = x[:, j * LANES:(j + 1) * LANES]


def _load_token_tiles(ref, rows):
    return jnp.concatenate([ref[pl.ds(j, rows, stride=SUBLANES), :] for j in range(SUBLANES)], axis=-1)


def _cparams(sem):
    return pltpu.CompilerParams(dimension_semantics=sem, vmem_limit_bytes=VMEM_LIMIT_BYTES)


def _layer_norm(x, g, b, eps=LN_EPS):
    mu = jnp.mean(x, axis=-1, keepdims=True)
    xc = x - mu
    var = jnp.mean(xc * xc, axis=-1, keepdims=True)
    return xc * lax.rsqrt(var + eps) * g + b


def _sigmoid(x):
    return 1.0 / (1.0 + jnp.exp(-x))


def _split_bf16(x):
    hi = x.astype(BF16)
    return hi, (x - hi.astype(F32)).astype(BF16)


def _segsum(x, ones_bf16):
    t = x.shape[0]
    s = jnp.dot(jnp.concatenate(_split_bf16(x), axis=0), ones_bf16, preferred_element_type=F32)
    return s[:t] + s[t:]


def _dot_split(x, w_hi, w_lo):
    hi, lo = _split_bf16(x)
    acc = jnp.dot(hi, w_hi, preferred_element_type=F32)
    acc = acc + jnp.dot(lo, w_hi, preferred_element_type=F32)
    return acc + jnp.dot(hi, w_lo, preferred_element_type=F32)


def _segment_ones(width, seg):
    idx = np.arange(width) // seg
    return jnp.asarray(idx[:, None] == idx[None, :], dtype=BF16)


def _mod_kernel(c_ref, w_ref, b_ref, o_ref):
    c = c_ref[...]
    sc = c * _sigmoid(c)
    o_ref[...] = jnp.dot(sc, w_ref[...], precision=HIGHEST, preferred_element_type=F32) + b_ref[...]


def _modulation(c_rows, w_mod, b_mod):
    rows, d = c_rows.shape
    n = w_mod.shape[1]
    tn = 1536
    return pl.pallas_call(
        _mod_kernel,
        grid=(n // tn,),
        in_specs=[pl.BlockSpec((rows, d), lambda j: (0, 0)),
                  pl.BlockSpec((d, tn), lambda j: (0, j)),
                  pl.BlockSpec((1, tn), lambda j: (0, j))],
        out_specs=pl.BlockSpec((rows, tn), lambda j: (0, j)),
        out_shape=jax.ShapeDtypeStruct((rows, n), F32),
        compiler_params=_cparams(("arbitrary",)),
        name="modulation",
    )(c_rows, w_mod, b_mod)


def _in_proj_kernel(x_ref, g_ref, b_ref, s_ref, sh_ref, w_ref, pr_ref, pt_ref):
    h = _layer_norm(x_ref[0], g_ref[...], b_ref[...])
    u = h * (1.0 + s_ref[0]) + sh_ref[0]
    p = jnp.dot(u.astype(BF16), w_ref[...], preferred_element_type=F32)
    pr_ref[0] = p[:, :RWKV_COLS]
    pt_ref[0] = p[:, RWKV_COLS:]


def _in_proj(x, ln_g, ln_b, s1, sh1, w_in_bf16):
    b, n, d = x.shape
    tm = 256
    cols = w_in_bf16.shape[1]
    return pl.pallas_call(
        _in_proj_kernel,
        grid=(b, n // tm),
        in_specs=[pl.BlockSpec((1, tm, d), lambda bi, i: (bi, i, 0)),
                  pl.BlockSpec((1, d), lambda bi, i: (0, 0)),
                  pl.BlockSpec((1, d), lambda bi, i: (0, 0)),
                  pl.BlockSpec((1, 1, d), lambda bi, i: (bi, 0, 0)),
                  pl.BlockSpec((1, 1, d), lambda bi, i: (bi, 0, 0)),
                  pl.BlockSpec((d, cols), lambda bi, i: (0, 0))],
        out_specs=[pl.BlockSpec((1, tm, RWKV_COLS), lambda bi, i: (bi, i, 0)),
                   pl.BlockSpec((1, tm, RET_COLS), lambda bi, i: (bi, i, 0))],
        out_shape=[jax.ShapeDtypeStruct((b, n, RWKV_COLS), F32),
                   jax.ShapeDtypeStruct((b, n, RET_COLS), F32)],
        compiler_params=_cparams(("arbitrary", "arbitrary")),
        name="in_proj",
    )(x, ln_g, ln_b, s1, sh1, w_in_bf16)


def _rwkv_prepare_kernel(cur_ref, prev_ref, next_ref, mu_ref, w0_ref, w2_ref, a0_ref, a2_ref, g2_ref,
                         kk_scale_ref, ka_ref, rk_ref, ones_ref,
                         r_ref, v_ref, kk_ref, w_ref, kd_ref, bb_ref, g_ref, bonus_ref,
                         *, grid_shift, n_tok):
    cur = cur_ref[0]
    t, c = cur.shape
    row = lax.broadcasted_iota(jnp.int32, (t, c), 0)
    lane = lax.broadcasted_iota(jnp.int32, (t, c), 1)
    prev_tok = pltpu.roll(cur, 1, 0)
    next_tok = pltpu.roll(cur, t - 1, 0)
    if grid_shift:
        col = row & (GRID_W - 1)
        tok = row + pl.program_id(1) * t
        left = jnp.where(col > 0, prev_tok, 0.0)
        right = jnp.where(col < GRID_W - 1, next_tok, 0.0)
        up = jnp.where(tok >= GRID_W, jnp.concatenate([prev_ref[0], cur[:t - GRID_W]], axis=0), 0.0)
        down = jnp.where(tok < n_tok - GRID_W, jnp.concatenate([cur[GRID_W:], next_ref[0]], axis=0), 0.0)
        cm = lane & 3
        shifted = jnp.where(cm == 0, left, jnp.where(cm == 1, right, jnp.where(cm == 2, up, down)))
    else:
        prev_tok = jnp.where(row > 0, prev_tok, 0.0)
        next_tok = jnp.where(row < t - 1, next_tok, 0.0)
        shifted = jnp.where((lane & 1) == 0, prev_tok, next_tok)
    pm = cur + mu_ref[...] * (shifted - cur)

    r = pm[:, 0:D_RWKV]
    k = pm[:, D_RWKV:2 * D_RWKV]
    v = pm[:, 2 * D_RWKV:3 * D_RWKV]
    o = 3 * D_RWKV
    lw = pm[:, o:o + 2 * DECAY_LORA]
    la = pm[:, o + 2 * DECAY_LORA:o + 2 * (DECAY_LORA + AAA_LORA)]
    lg = pm[:, o + 2 * (DECAY_LORA + AAA_LORA):]

    w = w0_ref[...] + _dot_split(jnp.tanh(lw), w2_ref[0], w2_ref[1])
    log_decay = -EXP_NEG_HALF * _sigmoid(w)
    a = _sigmoid(a0_ref[...] + _dot_split(la, a2_ref[0], a2_ref[1]))
    gate = _dot_split(_sigmoid(lg), g2_ref[0], g2_ref[1])

    ones = ones_ref[...]
    kk_raw = k * kk_scale_ref[...]
    kk = kk_raw / jnp.maximum(jnp.sqrt(_segsum(kk_raw * kk_raw, ones)), 1e-12)
    ka = ka_ref[...]
    a0 = a[:, :D_RWKV]
    a1 = a[:, D_RWKV:]
    kd0 = k * (1.0 + (a0 - 1.0) * ka)
    kd1 = k * (1.0 + (a1 - 1.0) * ka)
    bonus = _segsum(r * (kd0 + kd1) * rk_ref[...], ones) * v

    r_ref[0] = r
    v_ref[0] = v
    kk_ref[0] = kk
    w_ref[0] = log_decay
    kd_ref[0] = jnp.concatenate([kd0, kd1], axis=-1)
    bb_ref[0] = jnp.concatenate([kk * a0, kk * a1], axis=-1)
    g_ref[0] = gate
    bonus_ref[0] = bonus


def _rwkv_prepare(pr, params, grid_shift):
    b, n, c = pr.shape
    t = 256
    if not grid_shift:
        assert n == t, "sequence token shift is written for a single tile"
    halo_blocks = n // GRID_W
    per_tile = t // GRID_W
    small = lambda shape: pl.BlockSpec(shape, lambda bi, i: (0,) * len(shape))
    tok_spec = lambda width: pl.BlockSpec((1, t, width), lambda bi, i: (bi, i, 0))
    out_widths = (D_RWKV, D_RWKV, D_RWKV, 2 * D_RWKV, 2 * D_RWKV, 2 * D_RWKV, D_RWKV, D_RWKV)
    kernel = functools.partial(_rwkv_prepare_kernel, grid_shift=grid_shift, n_tok=n)
    return pl.pallas_call(
        kernel,
        grid=(b, n // t),
        in_specs=[tok_spec(c),
                  pl.BlockSpec((1, GRID_W, c), lambda bi, i: (bi, jnp.maximum(i * per_tile - 1, 0), 0)),
                  pl.BlockSpec((1, GRID_W, c),
                               lambda bi, i: (bi, jnp.minimum((i + 1) * per_tile, halo_blocks - 1), 0)),
                  small((1, c)), small((1, 2 * D_RWKV)), small((2, 2 * DECAY_LORA, 2 * D_RWKV)),
                  small((1, 2 * D_RWKV)), small((2, 2 * AAA_LORA, 2 * D_RWKV)), small((2, GATE_LORA, D_RWKV)),
                  small((1, D_RWKV)), small((1, D_RWKV)), small((1, D_RWKV)), small((D_RWKV, D_RWKV))],
        out_specs=[tok_spec(wd) for wd in out_widths],
        out_shape=[jax.ShapeDtypeStruct((b, n, wd), F32) for wd in out_widths],
        compiler_params=_cparams(("arbitrary", "arbitrary")),
        name="rwkv_prepare",
    )(pr, pr, pr, *params)


def _bdot(a, b):
    return jnp.dot(a.astype(BF16), b.astype(BF16), preferred_element_type=F32)


def _bdot_nt(a, b):
    return lax.dot_general(a.astype(BF16), b.astype(BF16), (((1,), (1,)), ((), ())), preferred_element_type=F32)


def _bdot_tn(a, b):
    return lax.dot_general(a.astype(BF16), b.astype(BF16), (((0,), (0,)), ((), ())), preferred_element_type=F32)


def _wkv7_chunk_kernel(r_l, v_l, kk_l, lw_l, kd_l, bb_l, r_c, v_c, kk_c, lw_c, kd_c, bb_c, y_ref, state_ref,
                       *, n_ctx_chunks):
    c = WKV_CHUNK
    d = pl.program_id(0)
    n = pl.program_id(1)
    n_batch = r_l.shape[0]
    sign = 1 - 2 * d

    @pl.when(n == 0)
    def _():
        state_ref[...] = jnp.zeros_like(state_ref)

    is_ctx = n < n_ctx_chunks
    pick = lambda xc, xl: jnp.concatenate([jnp.where(is_ctx, xc[bi], xl[bi]) for bi in range(n_batch)], axis=-1)
    r = pick(r_c, r_l)
    v = pick(v_c, v_l)
    kk = pick(kk_c, kk_l)
    lw = pick(lw_c, lw_l)
    kd = pick(kd_c, kd_l)
    bb = pick(bb_c, bb_l)

    ti = lax.broadcasted_iota(jnp.int32, (c, c), 0)
    tj = lax.broadcasted_iota(jnp.int32, (c, c), 1)
    upto = ((tj - ti) * sign <= 0).astype(BF16)
    hi = lw.astype(BF16)
    r1 = lw - hi.astype(F32)
    mid = r1.astype(BF16)
    lo = (r1 - mid.astype(F32)).astype(BF16)
    cum = (jnp.dot(upto, hi, preferred_element_type=F32) + jnp.dot(upto, mid, preferred_element_type=F32)
           + jnp.dot(upto, lo, preferred_element_type=F32))
    tot = jnp.sum(lw, axis=0, keepdims=True)
    e_neg = jnp.exp(-cum)
    e_rem = jnp.exp(tot - cum)
    alpha = kk * jnp.exp(cum - lw)
    rho = r * jnp.exp(cum)
    beta = bb * e_neg
    kappa = kd * e_neg
    kappa_rem = kd * e_rem
    beta_rem = bb * e_rem
    g_chunk = jnp.exp(tot)

    p = 2 * c
    ri = lax.broadcasted_iota(jnp.int32, (p, p), 0)
    ci = lax.broadcasted_iota(jnp.int32, (p, p), 1)
    same_head = (ri >= c) == (ci >= c)
    ii = ri & (c - 1)
    jj = ci & (c - 1)
    earlier = same_head & ((jj - ii) * sign < 0)
    upto_self = earlier | (ri == ci)
    eye = (ri == ci).astype(F32)
    lane = lax.broadcasted_iota(jnp.int32, (c, p), 1)
    first = lane < RWKV_HEAD

    def stack(x):
        return jnp.concatenate([jnp.where(first, x, 0.0), jnp.where(first, 0.0, x)], axis=0)

    def unstack(x):
        return x[:c] + x[c:]

    pairs_per_batch = RWKV_HEADS // 2
    pairs = range(n_batch * pairs_per_batch)
    sls = [slice(hp * p, (hp + 1) * p) for hp in pairs]
    a_st = [stack(alpha[:, sl]) for sl in sls]
    r_st = [stack(rho[:, sl]) for sl in sls]
    k_st = [stack(kappa[:, sl]) for sl in sls]
    b_st = [stack(beta[:, sl]) for sl in sls]
    k2_st = [stack(kappa_rem[:, sl]) for sl in sls]
    b2_st = [stack(beta_rem[:, sl]) for sl in sls]
    v_st = [stack(v[:, sl]) for sl in sls]

    g = [_bdot_nt(jnp.concatenate([a_st[h], r_st[h]], axis=0), jnp.concatenate([k_st[h], b_st[h]], axis=0))
         for h in pairs]
    m1 = [jnp.where(earlier, g[h][:p, :p], 0.0) for h in pairs]
    m2 = [jnp.where(earlier, g[h][:p, p:], 0.0) for h in pairs]
    n1 = [jnp.where(upto_self, g[h][p:, :p], 0.0) for h in pairs]
    n2 = [jnp.where(upto_self, g[h][p:, p:], 0.0) for h in pairs]

    in_block = (ii >> 3) == (jj >> 3)
    pw = [-jnp.where(in_block, m2[h], 0.0) for h in pairs]
    inv = [eye + pw[h] for h in pairs]
    for _ in range(2):
        pw = [_bdot(pw[h], pw[h]) for h in pairs]
        inv = [inv[h] + _bdot(inv[h], pw[h]) for h in pairs]
    for sh in (3, 4, 5):
        off = ((ii >> (sh + 1)) == (jj >> (sh + 1))) & ((ii >> sh) != (jj >> sh))
        left = [_bdot(inv[h], jnp.where(off, m2[h], 0.0)) for h in pairs]
        inv = [inv[h] - _bdot(left[h], inv[h]) for h in pairs]

    m1v = [_bdot(m1[h], v_st[h]) for h in pairs]
    n1v = [_bdot(n1[h], v_st[h]) for h in pairs]
    au = [_bdot(inv[h], jnp.concatenate([a_st[h], m1v[h]], axis=1)) for h in pairs]
    nn = [_bdot(n2[h], au[h]) for h in pairs]
    pc = [_bdot_tn(b2_st[h], au[h][:, :p]) for h in pairs]
    qc_t = [_bdot_tn(jnp.concatenate([v_st[h], -au[h][:, p:]], axis=0),
                     jnp.concatenate([k2_st[h], b2_st[h]], axis=0)) for h in pairs]
    s0 = [state_ref[h] for h in pairs]
    y = [_bdot_nt(unstack(r_st[h] - nn[h][:, :p]), s0[h]) + unstack(n1v[h] - nn[h][:, p:]) for h in pairs]
    s_dec = [_bdot_nt(s0[h], pc[h]) for h in pairs]
    for h in pairs:
        y_ref[0, h // pairs_per_batch, :, sls[h % pairs_per_batch]] = y[h]
        state_ref[h] = s0[h] * g_chunk[:, sls[h]] - s_dec[h] + qc_t[h]


def _wkv7(lat, ctx, b, n_tok, n_ctx):
    c = WKV_CHUNK
    ncx = n_ctx // c
    nl = n_tok // c
    lat_idx = lambda d, n: jnp.where(d == 0, jnp.maximum(n - ncx, 0), nl - 1 - jnp.maximum(n - ncx, 0))
    ctx_idx = lambda d, n: jnp.where(d == 0, jnp.minimum(n, ncx - 1), ncx - 1 - jnp.minimum(n, ncx - 1))
    shared = lambda idx: pl.BlockSpec((b, c, D_RWKV), lambda d, n: (0, idx(d, n), 0))
    per_dir = lambda idx: pl.BlockSpec((b, c, D_RWKV), lambda d, n: (0, idx(d, n), d))
    specs = lambda idx: [shared(idx), shared(idx), shared(idx), per_dir(idx), per_dir(idx), per_dir(idx)]
    return pl.pallas_call(
        functools.partial(_wkv7_chunk_kernel, n_ctx_chunks=ncx),
        grid=(2, ncx + nl),
        in_specs=specs(lat_idx) + specs(ctx_idx),
        out_specs=pl.BlockSpec((1, b, c, D_RWKV), lambda d, n: (d, 0, lat_idx(d, n), 0)),
        out_shape=jax.ShapeDtypeStruct((2, b, n_tok, D_RWKV), F32),
        scratch_shapes=[pltpu.VMEM((b * RWKV_HEADS // 2, 2 * RWKV_HEAD, 2 * RWKV_HEAD), F32)],
        compiler_params=_cparams(("arbitrary", "arbitrary")),
        name="wkv7_chunk",
    )(*lat, *ctx)


def _rope(z, cos_t, sin_t):
    lane = lax.broadcasted_iota(jnp.int32, z.shape, 1)
    half = RET_HEAD // 4
    partner = jnp.where((lane & (2 * half - 1)) < half, pltpu.roll(z, RET_HEAD - half, 1), pltpu.roll(z, half, 1))
    return z * cos_t + partner * sin_t


def _retention_kernel(dec_ref, fwd_ref, bwd_ref, ctx_ref, cosf_ref, sinf_ref, cosb_ref, sinb_ref,
                      yf_ref, yb_ref, state_ref, dmat_ref, tail_ref, head_ref, cdec_ref):
    c = RET_CHUNK
    scale = RET_HEAD ** -0.5
    ii = lax.broadcasted_iota(jnp.int32, (c, c), 0)
    jj = lax.broadcasted_iota(jnp.int32, (c, c), 1)
    pos = lax.broadcasted_iota(jnp.int32, (c, RET_HEAD), 0).astype(F32)
    n_ctx_chunks = ctx_ref.shape[1] // c

    def head_slices(ref_val, h):
        q = ref_val[:, h * RET_HEAD:(h + 1) * RET_HEAD]
        k = ref_val[:, D_RET + h * RET_HEAD:D_RET + (h + 1) * RET_HEAD]
        v = ref_val[:, 2 * D_RET + h * RET_HEAD:2 * D_RET + (h + 1) * RET_HEAD]
        return q, k, v

    chains = [(d, h) for d in range(2) for h in range(RET_HEADS)]

    @pl.when(pl.program_id(1) == 0)
    def _():
        for d, h in chains:
            x = jnp.full((1, RET_HEAD), dec_ref[d, h], F32)
            lg = -(jnp.maximum(x, 0.0) + jnp.log(1.0 + jnp.exp(-jnp.abs(x))))
            chunk_decay = jnp.exp(lg * float(c))
            tail = jnp.exp(lg * ((c - 1.0 - pos) if d == 0 else pos))
            rel = (ii - jj) if d == 0 else (jj - ii)
            mask = (rel >= 0) if d == 0 else (rel > 0)
            dmat_ref[d, h] = jnp.where(mask, jnp.exp(lg * jnp.maximum(rel, 0).astype(F32)), 0.0)
            tail_ref[d, h] = tail
            head_ref[d, h] = jnp.exp(lg * ((pos + 1.0) if d == 0 else (c - pos)))
            cdec_ref[d, h] = jnp.broadcast_to(chunk_decay, (SUBLANES, RET_HEAD))
            s = jnp.zeros((RET_HEAD, RET_HEAD), F32)
            order = range(n_ctx_chunks) if d == 0 else range(n_ctx_chunks - 1, -1, -1)
            for cc in order:
                _, kc, vc = head_slices(ctx_ref[0, cc * c:(cc + 1) * c, :], h)
                s = s * chunk_decay + _bdot_tn(kc * scale * tail, vc)
            state_ref[d, h] = s

    qkv = []
    for d, h in chains:
        blk = fwd_ref[0] if d == 0 else bwd_ref[0]
        cos_t = cosf_ref[...] if d == 0 else cosb_ref[...]
        sin_t = sinf_ref[...] if d == 0 else sinb_ref[...]
        q, k, v = head_slices(blk, h)
        qkv.append((_rope(q, cos_t, sin_t), _rope(k, cos_t, sin_t) * scale, v.astype(BF16)))
    s0 = [state_ref[d, h] for d, h in chains]
    scores = [_bdot_nt(q, k) for q, k, _ in qkv]
    inner = [_bdot(scores[i] * dmat_ref[d, h], qkv[i][2]) for i, (d, h) in enumerate(chains)]
    cross = [_bdot(qkv[i][0] * head_ref[d, h], s0[i]) for i, (d, h) in enumerate(chains)]
    upd = [_bdot_tn(qkv[i][1] * tail_ref[d, h], qkv[i][2]) for i, (d, h) in enumerate(chains)]
    for i, (d, h) in enumerate(chains):
        state_ref[d, h] = s0[i] * cdec_ref[d, h, 0:1, :] + upd[i]
        out_ref = yf_ref if d == 0 else yb_ref
        out_ref[0, :, h * RET_HEAD:(h + 1) * RET_HEAD] = inner[i] + cross[i]


def _retention(pt, pt_ctx, ret_decay, cos_t, sin_t):
    b, n, _ = pt.shape
    c = RET_CHUNK
    nc = n // c
    qkv = 3 * D_RET
    fwd = lambda bi, i: (bi, i, 0)
    bwd = lambda bi, i: (bi, nc - 1 - i, 0)
    return pl.pallas_call(
        _retention_kernel,
        grid=(b, nc),
        in_specs=[pl.BlockSpec(memory_space=pltpu.SMEM),
                  pl.BlockSpec((1, c, qkv), fwd),
                  pl.BlockSpec((1, c, qkv), bwd),
                  pl.BlockSpec((1, pt_ctx.shape[1], qkv), lambda bi, i: (bi, 0, 0)),
                  pl.BlockSpec((c, RET_HEAD), lambda bi, i: (i, 0)),
                  pl.BlockSpec((c, RET_HEAD), lambda bi, i: (i, 0)),
                  pl.BlockSpec((c, RET_HEAD), lambda bi, i: (nc - 1 - i, 0)),
                  pl.BlockSpec((c, RET_HEAD), lambda bi, i: (nc - 1 - i, 0))],
        out_specs=[pl.BlockSpec((1, c, D_RET), fwd), pl.BlockSpec((1, c, D_RET), bwd)],
        out_shape=[jax.ShapeDtypeStruct((b, n, D_RET), F32), jax.ShapeDtypeStruct((b, n, D_RET), F32)],
        scratch_shapes=[pltpu.VMEM((2, RET_HEADS, RET_HEAD, RET_HEAD), F32),
                        pltpu.VMEM((2, RET_HEADS, c, c), F32),
                        pltpu.VMEM((2, RET_HEADS, c, RET_HEAD), F32),
                        pltpu.VMEM((2, RET_HEADS, c, RET_HEAD), F32),
                        pltpu.VMEM((2, RET_HEADS, SUBLANES, RET_HEAD), F32)],
        compiler_params=_cparams(("arbitrary", "arbitrary")),
        name="retention",
    )(ret_decay, pt, pt, pt_ctx, cos_t, sin_t, cos_t, sin_t)


def _rope_tables(n_tok):
    nf = RET_HEAD // 4
    lane = np.arange(RET_HEAD)
    inv = ROPE_BASE ** (-jnp.arange(nf, dtype=F32) / nf)
    t = jnp.arange(n_tok)
    pos = jnp.where((lane // (2 * nf) == 0)[None, :], (t // GRID_W)[:, None], (t % GRID_W)[:, None]).astype(F32)
    ang = pos * inv[lane % nf][None, :]
    sign = jnp.where((lane % (2 * nf)) < nf, -1.0, 1.0).astype(F32)
    return jnp.cos(ang), jnp.sin(ang) * sign[None, :]


def _group_norm(y, ones, seg, eps, g, b):
    mu = _segsum(y, ones) * (1.0 / seg)
    yc = y - mu
    var = _segsum(yc * yc, ones) * (1.0 / seg)
    return yc * lax.rsqrt(var + eps) * g + b


def _out_proj_kernel(x_ref, yf_ref, yb_ref, bonus_ref, gate_ref, tf_ref, tb_ref, gt_ref,
                     embg_ref, embb_ref, g1_ref, s2_ref, sh2_ref, rgn_g_ref, rgn_b_ref, tgn_g_ref, tgn_b_ref,
                     ones_r_ref, wout_ref, ln1g_ref, ln1b_ref, wrh_ref, wrl_ref, br_ref,
                     h1_ref, u2_ref, logit_ref):
    y = yf_ref[0] + yb_ref[0]
    o_rwkv = _group_norm(y, ones_r_ref[...], RWKV_HEAD, RWKV_GN_EPS, rgn_g_ref[...], rgn_b_ref[...])
    o_rwkv = (o_rwkv + bonus_ref[0]) * gate_ref[0]
    yt = tf_ref[0] + tb_ref[0]
    gt = gt_ref[0]
    tgn_g = tgn_g_ref[...]
    tgn_b = tgn_b_ref[...]
    o_ret = jnp.concatenate(
        [_layer_norm(yt[:, h * RET_HEAD:(h + 1) * RET_HEAD], tgn_g[:, h * RET_HEAD:(h + 1) * RET_HEAD],
                     tgn_b[:, h * RET_HEAD:(h + 1) * RET_HEAD], RET_GN_EPS) for h in range(RET_HEADS)], axis=-1)
    o_ret = o_ret * (gt * _sigmoid(gt))
    cat = jnp.concatenate([o_rwkv, o_ret], axis=-1).astype(BF16)
    mix = jnp.dot(cat, wout_ref[...], preferred_element_type=F32)
    h = _layer_norm(x_ref[0], embg_ref[...], embb_ref[...])
    h1 = _layer_norm(DEEPNORM_ALPHA * h + g1_ref[0] * mix, ln1g_ref[...], ln1b_ref[...])
    u2 = h1 * (1.0 + s2_ref[0]) + sh2_ref[0]
    h1_ref[0] = h1
    _store_token_tiles(u2_ref, u2)
    logit_ref[0] = _dot_split(u2, wrh_ref[...], wrl_ref[...]) + br_ref[...]


def _out_proj(x, y_f, y_b, bonus, gate, t_f, t_b, pt, vecs, mats):
    b, n, d = x.shape
    t = 256
    tok = lambda width: pl.BlockSpec((1, t, width), lambda bi, i: (bi, i, 0))
    per_b = pl.BlockSpec((1, 1, d), lambda bi, i: (bi, 0, 0))
    small = lambda arr: pl.BlockSpec(arr.shape, lambda bi, i: (0,) * arr.ndim)
    (embg, embb, g1, s2, sh2, rgn_g, rgn_b, tgn_g, tgn_b, ln1g, ln1b, br) = vecs
    (ones_r, wout, wr_hi, wr_lo) = mats
    gt_spec = pl.BlockSpec((1, t, D_RET), lambda bi, i: (bi, i, 3))
    args = (x, y_f, y_b, bonus, gate, t_f, t_b, pt, embg, embb, g1, s2, sh2, rgn_g, rgn_b, tgn_g, tgn_b,
            ones_r, wout, ln1g, ln1b, wr_hi, wr_lo, br)
    in_specs = [tok(d)] + [tok(D_RWKV)] * 6 + [gt_spec, small(embg), small(embb), per_b, per_b, per_b,
                                                small(rgn_g), small(rgn_b), small(tgn_g), small(tgn_b),
                                                small(ones_r), small(wout), small(ln1g),
                                                small(ln1b), small(wr_hi), small(wr_lo), small(br)]
    return pl.pallas_call(
        _out_proj_kernel,
        grid=(b, n // t),
        in_specs=in_specs,
        out_specs=[tok(d), pl.BlockSpec((t * SUBLANES, LANES), lambda bi, i: (bi * (n // t) + i, 0)), tok(LANES)],
        out_shape=[jax.ShapeDtypeStruct((b, n, d), F32), jax.ShapeDtypeStruct((b * n * SUBLANES, LANES), F32),
                   jax.ShapeDtypeStruct((b, n, LANES), F32)],
        compiler_params=_cparams(("arbitrary", "arbitrary")),
        name="out_proj",
    )(*args)


ROUTE_E1, ROUTE_E2, ROUTE_G1, ROUTE_G2, ROUTE_RANK1, ROUTE_RANK2 = range(6)


def _lane_argmax(x, valid, lane):
    m = jnp.max(jnp.where(valid, x, -jnp.inf), axis=-1, keepdims=True)
    idx = jnp.min(jnp.where(valid & (x == m), lane, float(LANES)), axis=-1, keepdims=True)
    return m, idx


def _route_kernel(logit_ref, route_ref, count_ref, carry_ref):
    @pl.when(pl.program_id(0) == 0)
    def _():
        carry_ref[...] = jnp.zeros_like(carry_ref)

    lg = logit_ref[...]
    t = lg.shape[0]
    lane = lax.broadcasted_iota(jnp.int32, lg.shape, 1).astype(F32)
    gmask = lane < N_GROUPS
    gmax = jnp.max(jnp.where(gmask, lg, -jnp.inf), axis=-1, keepdims=True)
    gexp = jnp.where(gmask, jnp.exp(lg - gmax), 0.0)
    gp = gexp / jnp.sum(gexp, axis=-1, keepdims=True)
    g_w, g_i = _lane_argmax(gp, gmask, lane)

    lo = N_GROUPS + EXPERTS_PER_GROUP * g_i
    emask = (lane >= lo) & (lane < lo + EXPERTS_PER_GROUP)
    emax = jnp.max(jnp.where(emask, lg, -jnp.inf), axis=-1, keepdims=True)
    eexp = jnp.where(emask, jnp.exp(lg - emax), 0.0)
    ep = eexp / jnp.sum(eexp, axis=-1, keepdims=True)
    p1, i1 = _lane_argmax(ep, emask, lane)
    p2, i2 = _lane_argmax(ep, emask & (lane != i1), lane)
    denom = p1 + p2
    gate1 = g_w * p1 / denom
    gate2 = g_w * p2 / denom
    e1 = i1 - N_GROUPS
    e2 = i2 - N_GROUPS

    oh1 = (lane == e1).astype(F32)
    oh2 = (lane == e2).astype(F32)
    cnt = oh1 + oh2
    ri = lax.broadcasted_iota(jnp.int32, (t, t), 0)
    ci = lax.broadcasted_iota(jnp.int32, (t, t), 1)
    before = (ci < ri).astype(BF16)
    seen = jnp.dot(before, cnt.astype(BF16), preferred_element_type=F32) + carry_ref[0:1, :]
    rank1 = jnp.sum(oh1 * seen, axis=-1, keepdims=True)
    rank2 = jnp.sum(oh2 * seen, axis=-1, keepdims=True)
    carry_ref[0:1, :] = carry_ref[0:1, :] + jnp.sum(cnt, axis=0, keepdims=True)

    out = jnp.zeros(lg.shape, F32)
    for slot, val in ((ROUTE_E1, e1.astype(F32)), (ROUTE_E2, e2.astype(F32)), (ROUTE_G1, gate1),
                      (ROUTE_G2, gate2), (ROUTE_RANK1, rank1), (ROUTE_RANK2, rank2)):
        out = jnp.where(lane == slot, val, out)
    route_ref[...] = out
    count_ref[...] = carry_ref[...]


def _route(logits):
    n = logits.shape[0]
    t = 256
    return pl.pallas_call(
        _route_kernel,
        grid=(n // t,),
        in_specs=[pl.BlockSpec((t, LANES), lambda i: (i, 0))],
        out_specs=[pl.BlockSpec((t, LANES), lambda i: (i, 0)), pl.BlockSpec((SUBLANES, LANES), lambda i: (0, 0))],
        out_shape=[jax.ShapeDtypeStruct((n, LANES), F32), jax.ShapeDtypeStruct((SUBLANES, LANES), F32)],
        scratch_shapes=[pltpu.VMEM((SUBLANES, LANES), F32)],
        compiler_params=_cparams(("arbitrary",)),
        name="route",
    )(logits)


def _tile_gather_copy(src_hbm, idx_ref, buf, sem, slot, r):
    dst = buf.at[slot, pl.ds(pl.multiple_of(r * SUBLANES, SUBLANES), SUBLANES), :]
    return pltpu.make_async_copy(src_hbm.at[idx_ref[0, 0, r]], dst, sem.at[slot])


def _start_tile_gather(src_hbm, idx_ref, buf, sem, slot, rows, priorities):
    def body(g, carry):
        for j in range(GATHER_UNROLL):
            copy = _tile_gather_copy(src_hbm, idx_ref, buf, sem, slot, g * GATHER_UNROLL + j)
            copy.start(priority=priorities[j % len(priorities)])
        return carry
    lax.fori_loop(0, rows // GATHER_UNROLL, body, 0)


def _wait_tile_gather(src_hbm, idx_ref, buf, sem, slot, rows):
    def body(r, carry):
        _tile_gather_copy(src_hbm, idx_ref, buf, sem, slot, r).wait()
        return carry
    lax.fori_loop(0, rows, body, 0, unroll=GATHER_UNROLL)


def _expert_kernel(blk_expert_ref, n_used_ref, tok_ref, tok_next_ref, u_hbm, wg_ref, wu_ref, wd_ref, y_ref,
                   xbuf, sem, wg_s, wu_s, wd_s):
    i = pl.program_id(0)
    n_used = n_used_ref[0]
    slot = i % 2
    gather_priority = (1,)

    @pl.when(i == 0)
    def _():
        _start_tile_gather(u_hbm, tok_ref, xbuf, sem, 0, EXPERT_ROWS, gather_priority)

    @pl.when(i + 1 < n_used)
    def _():
        _start_tile_gather(u_hbm, tok_next_ref, xbuf, sem, 1 - slot, EXPERT_ROWS, gather_priority)

    @pl.when(i >= n_used)
    def _():
        y_ref[...] = jnp.zeros_like(y_ref)

    @pl.when(i < n_used)
    def _():
        @pl.when((i == 0) | (blk_expert_ref[i] != blk_expert_ref[jnp.maximum(i - 1, 0)]))
        def _():
            wg_s[...] = wg_ref[0].astype(BF16)
            wu_s[...] = wu_ref[0].astype(BF16)
            wd_s[...] = wd_ref[0].astype(BF16)

        _wait_tile_gather(u_hbm, tok_ref, xbuf, sem, slot, EXPERT_ROWS)
        x = _load_token_tiles(xbuf.at[slot], EXPERT_ROWS).astype(BF16)
        hg = jnp.dot(x, wg_s[...], preferred_element_type=F32)
        hu = jnp.dot(x, wu_s[...], preferred_element_type=F32)
        act = (hg * _sigmoid(hg) * hu).astype(BF16)
        _store_token_tiles(y_ref, jnp.dot(act, wd_s[...], preferred_element_type=F32))


def _expert_mlp(u2_tiles, slot_tok, block_expert, n_used, w_gate, w_up, w_down):
    n_blk = block_expert.shape[0]
    d = w_gate.shape[1]
    hdim = w_gate.shape[2]
    rows = EXPERT_ROWS
    tok3 = slot_tok.reshape(n_blk, 1, rows)
    used = lambda i, nu: jnp.minimum(i, nu[0] - 1)
    weight = lambda i, be, nu: (be[used(i, nu)], 0, 0)
    grid_spec = pltpu.PrefetchScalarGridSpec(
        num_scalar_prefetch=2,
        grid=(n_blk,),
        in_specs=[pl.BlockSpec((1, 1, rows), lambda i, be, nu: (used(i, nu), 0, 0), memory_space=pltpu.SMEM),
                  pl.BlockSpec((1, 1, rows), lambda i, be, nu: (used(i + 1, nu), 0, 0), memory_space=pltpu.SMEM),
                  pl.BlockSpec(memory_space=pl.ANY),
                  pl.BlockSpec((1, d, hdim), weight),
                  pl.BlockSpec((1, d, hdim), weight),
                  pl.BlockSpec((1, hdim, d), weight)],
        out_specs=pl.BlockSpec((rows * SUBLANES, LANES), lambda i, be, nu: (i, 0)),
        scratch_shapes=[pltpu.VMEM((2, rows * SUBLANES, LANES), F32), pltpu.SemaphoreType.DMA((2,)),
                        pltpu.VMEM((d, hdim), BF16), pltpu.VMEM((d, hdim), BF16), pltpu.VMEM((hdim, d), BF16)],
    )
    return pl.pallas_call(
        _expert_kernel,
        grid_spec=grid_spec,
        out_shape=jax.ShapeDtypeStruct((n_blk * rows * SUBLANES, LANES), F32),
        compiler_params=_cparams(("arbitrary",)),
        name="expert_mlp",
    )(block_expert, n_used, tok3, tok3, u2_tiles, w_gate, w_up, w_down)


def _combine_kernel(d1_ref, d2_ref, d1n_ref, d2n_ref, y_hbm, route_ref, h1_ref, g2_ref, lng_ref, lnb_ref,
                    o_ref, abuf, bbuf, sem_a, sem_b):
    i = pl.program_id(0)
    n = pl.num_programs(0)
    slot = i % 2
    rows = o_ref.shape[0]

    @pl.when(i == 0)
    def _():
        _start_tile_gather(y_hbm, d1_ref, abuf, sem_a, 0, rows, COMBINE_PRIORITIES)
        _start_tile_gather(y_hbm, d2_ref, bbuf, sem_b, 0, rows, COMBINE_PRIORITIES)

    @pl.when(i + 1 < n)
    def _():
        _start_tile_gather(y_hbm, d1n_ref, abuf, sem_a, 1 - slot, rows, COMBINE_PRIORITIES)
        _start_tile_gather(y_hbm, d2n_ref, bbuf, sem_b, 1 - slot, rows, COMBINE_PRIORITIES)

    _wait_tile_gather(y_hbm, d1_ref, abuf, sem_a, slot, rows)
    _wait_tile_gather(y_hbm, d2_ref, bbuf, sem_b, slot, rows)
    route = route_ref[...]
    f = (_load_token_tiles(abuf.at[slot], rows) * route[:, ROUTE_G1:ROUTE_G1 + 1]
         + _load_token_tiles(bbuf.at[slot], rows) * route[:, ROUTE_G2:ROUTE_G2 + 1])
    o_ref[...] = _layer_norm(DEEPNORM_ALPHA * h1_ref[...] + g2_ref[0] * f, lng_ref[...], lnb_ref[...])


def _combine(y_tiles, dest1, dest2, route, h1, g2, ln_g, ln_b, tokens_per_batch):
    n, d = h1.shape
    t = COMBINE_ROWS
    nt = n // t
    per_b = tokens_per_batch // t
    d1 = dest1.reshape(nt, 1, t)
    d2 = dest2.reshape(nt, 1, t)
    cur = pl.BlockSpec((1, 1, t), lambda i: (i, 0, 0), memory_space=pltpu.SMEM)
    nxt = pl.BlockSpec((1, 1, t), lambda i: (jnp.minimum(i + 1, nt - 1), 0, 0), memory_space=pltpu.SMEM)
    small = lambda arr: pl.BlockSpec(arr.shape, lambda i: (0,) * arr.ndim)
    return pl.pallas_call(
        _combine_kernel,
        grid=(nt,),
        in_specs=[cur, cur, nxt, nxt, pl.BlockSpec(memory_space=pl.ANY),
                  pl.BlockSpec((t, LANES), lambda i: (i, 0)),
                  pl.BlockSpec((t, d), lambda i: (i, 0)),
                  pl.BlockSpec((1, 1, d), lambda i: (i // per_b, 0, 0)),
                  small(ln_g), small(ln_b)],
        out_specs=pl.BlockSpec((t, d), lambda i: (i, 0)),
        out_shape=jax.ShapeDtypeStruct((n, d), F32),
        scratch_shapes=[pltpu.VMEM((2, t * SUBLANES, LANES), F32), pltpu.VMEM((2, t * SUBLANES, LANES), F32),
                        pltpu.SemaphoreType.DMA((2,)), pltpu.SemaphoreType.DMA((2,))],
        compiler_params=_cparams(("arbitrary",)),
        name="combine",
    )(d1, d2, d1, d2, y_tiles.reshape(-1, SUBLANES, LANES), route, h1, g2, ln_g, ln_b)


def _hi_lo(w):
    hi = w.astype(BF16)
    return jnp.stack([hi, (w - hi.astype(F32)).astype(BF16)])


def _block_diag2(w):
    z = jnp.zeros_like(w[0])
    return jnp.concatenate([jnp.concatenate([w[0], z], axis=1), jnp.concatenate([z, w[1]], axis=1)], axis=0)


def kernel(x, c, ctx, c_ctx, emb_ln_g, emb_ln_b, w_mod, b_mod, w_in, tshift_mu, rwkv_w0, rwkv_w2, rwkv_a0, rwkv_a2, rwkv_g2, rwkv_k_k, rwkv_k_a, rwkv_r_k, rwkv_gn_g, rwkv_gn_b, ret_decay, ret_gn_g, ret_gn_b, w_out, ln1_g, ln1_b, router_group, router_group_bias, router_expert, router_expert_bias, expert_w_gate, expert_w_up, expert_w_down, ln2_g, ln2_b):
    assert w_mod.shape[0] == 1, "written for DEPTH == 1 (context outputs are never emitted)"
    b, n_tok, d = x.shape
    n_ctx = ctx.shape[1]
    row = lambda v: v.reshape(1, -1)

    c_rows = jnp.zeros((SUBLANES, d), F32).at[:b].set(c).at[b].set(c_ctx)
    mod = _modulation(c_rows, w_mod[0], row(b_mod[0]))
    sh1, s1, g1, sh2, s2, g2 = [mod[:b, j * d:(j + 1) * d].reshape(b, 1, d) for j in range(6)]
    sh1c, s1c = [jnp.broadcast_to(mod[b, j * d:(j + 1) * d].reshape(1, 1, d), (b, 1, d)) for j in range(2)]

    w_in_bf16 = w_in[0].astype(BF16)
    pr, pt = _in_proj(x, row(emb_ln_g), row(emb_ln_b), s1, sh1, w_in_bf16)
    pr_c, pt_c = _in_proj(ctx, row(emb_ln_g), row(emb_ln_b), s1c, sh1c, w_in_bf16)

    prep_params = (row(tshift_mu[0]), row(rwkv_w0[0]), _hi_lo(_block_diag2(rwkv_w2[0])), row(rwkv_a0[0]),
                   _hi_lo(_block_diag2(rwkv_a2[0])), _hi_lo(rwkv_g2[0]), row(rwkv_k_k[0]), row(rwkv_k_a[0]),
                   row(rwkv_r_k[0]),
                   _segment_ones(D_RWKV, RWKV_HEAD))
    lat = _rwkv_prepare(pr, prep_params, grid_shift=True)
    cx = _rwkv_prepare(pr_c, prep_params, grid_shift=False)
    r_l, v_l, kk_l, w_l, kd_l, bb_l, gate_l, bonus_l = lat
    r_c, v_c, kk_c, w_c, kd_c, bb_c, _, _ = cx

    y_dirs = _wkv7((r_l, v_l, kk_l, w_l, kd_l, bb_l), (r_c, v_c, kk_c, w_c, kd_c, bb_c), b, n_tok, n_ctx)
    y_f = y_dirs[0]
    y_b = y_dirs[1]

    cos_t, sin_t = _rope_tables(n_tok)
    t_f, t_b = _retention(pt, pt_c, ret_decay[0], cos_t, sin_t)

    wr = jnp.zeros((d, LANES), F32).at[:, :N_GROUPS].set(router_group[0])
    wr = wr.at[:, N_GROUPS:N_GROUPS + N_EXPERTS].set(router_expert[0])
    br = jnp.zeros((1, LANES), F32).at[0, :N_GROUPS].set(router_group_bias[0])
    br = br.at[0, N_GROUPS:N_GROUPS + N_EXPERTS].set(router_expert_bias[0].reshape(-1))
    vecs = (row(emb_ln_g), row(emb_ln_b), g1, s2, sh2, row(rwkv_gn_g[0]), row(rwkv_gn_b[0]),
            row(ret_gn_g[0]), row(ret_gn_b[0]), row(ln1_g[0]), row(ln1_b[0]), br)
    wr_hi_lo = _hi_lo(wr)
    mats = (_segment_ones(D_RWKV, RWKV_HEAD), w_out[0].astype(BF16), wr_hi_lo[0], wr_hi_lo[1])
    h1, u2, logits = _out_proj(x, y_f, y_b, bonus_l, gate_l, t_f, t_b, pt, vecs, mats)

    n_all = b * n_tok
    route, counts = _route(logits.reshape(n_all, LANES))

    e1 = route[:, ROUTE_E1].astype(jnp.int32)
    e2 = route[:, ROUTE_E2].astype(jnp.int32)
    cnt = counts[0, :N_EXPERTS].astype(jnp.int32)
    padded = ((cnt + EXPERT_ROWS - 1) // EXPERT_ROWS) * EXPERT_ROWS
    pends = jnp.cumsum(padded)
    pstarts = pends - padded
    expert_ids = jnp.arange(N_EXPERTS, dtype=jnp.int32)
    start_of = lambda e: jnp.sum(jnp.where(e[:, None] == expert_ids[None, :], pstarts[None, :], 0), axis=1)
    dest1 = start_of(e1) + route[:, ROUTE_RANK1].astype(jnp.int32)
    dest2 = start_of(e2) + route[:, ROUTE_RANK2].astype(jnp.int32)
    n_blk = -(-(n_all * 2) // EXPERT_ROWS) + N_EXPERTS
    tok_ids = jnp.arange(n_all, dtype=jnp.int32)
    slot_tok = jnp.zeros((n_blk * EXPERT_ROWS,), jnp.int32).at[dest1].set(tok_ids).at[dest2].set(tok_ids)
    block_start = jnp.arange(n_blk, dtype=jnp.int32) * EXPERT_ROWS
    block_expert = jnp.minimum(jnp.sum((block_start[:, None] >= pends[None, :]).astype(jnp.int32), axis=1),
                               N_EXPERTS - 1)

    n_used = (pends[N_EXPERTS - 1:] // EXPERT_ROWS).astype(jnp.int32)
    y_tiles = _expert_mlp(u2.reshape(n_all, SUBLANES, LANES), slot_tok, block_expert, n_used,
                          expert_w_gate[0], expert_w_up[0], expert_w_down[0])
    out = _combine(y_tiles, dest1, dest2, route, h1.reshape(n_all, d), g2, row(ln2_g[0]), row(ln2_b[0]), n_tok)
    return out.reshape(b, n_tok, d)
```

```python
import functools
import math

import jax
import jax.numpy as jnp
import numpy as np
from jax import lax
from jax.experimental import pallas as pl
from jax.experimental.pallas import tpu as pltpu

F32 = jnp.float32
BF16 = jnp.bfloat16
HIGHEST = lax.Precision.HIGHEST

GRID_W = 64
D_RWKV = 512
RWKV_HEAD = 64
RWKV_HEADS = D_RWKV // RWKV_HEAD
DECAY_LORA = 64
AAA_LORA = 64
GATE_LORA = 128
D_RET = 512
RET_HEADS = 4
RET_HEAD = D_RET // RET_HEADS
RET_CHUNK = 128
RWKV_COLS = 3 * D_RWKV + 2 * (DECAY_LORA + AAA_LORA) + GATE_LORA
RET_COLS = 4 * D_RET
N_GROUPS = 4
EXPERTS_PER_GROUP = 8
N_EXPERTS = N_GROUPS * EXPERTS_PER_GROUP
EXPERT_HIDDEN = 512
MOE_BLOCK = 128
ROPE_BASE = 10000.0
LN_EPS = 1e-5
RWKV_GN_EPS = 64e-5
RET_GN_EPS = 1e-5
DEEPNORM_ALPHA = 2.0 ** 0.25
EXP_NEG_HALF = math.exp(-0.5)

LANES = 128
SUBLANES = 8
VMEM_LIMIT_BYTES = 56 * 1024 * 1024

WKV_CHUNK = 64


EXPERT_ROWS = 256
COMBINE_ROWS = 128
GATHER_UNROLL = 8
GATHER_PRIORITIES = (0, 1)


def _store_token_tiles(ref, x):
    rows = x.shape[0]
    for j in range(x.shape[1] // LANES):
        ref[pl.ds(j, rows, stride=SUBLANES), :] = x[:, j * LANES:(j + 1) * LANES]


def _load_token_tiles(ref, rows):
    return jnp.concatenate([ref[pl.ds(j, rows, stride=SUBLANES), :] for j in range(SUBLANES)], axis=-1)


def _cparams(sem):
    return pltpu.CompilerParams(dimension_semantics=sem, vmem_limit_bytes=VMEM_LIMIT_BYTES)


def _layer_norm(x, g, b, eps=LN_EPS):
    mu = jnp.mean(x, axis=-1, keepdims=True)
    xc = x - mu
    var = jnp.mean(xc * xc, axis=-1, keepdims=True)
    return xc * lax.rsqrt(var + eps) * g + b


def _sigmoid(x):
    return 1.0 / (1.0 + jnp.exp(-x))


def _split_bf16(x):
    hi = x.astype(BF16)
    return hi, (x - hi.astype(F32)).astype(BF16)


def _segsum(x, ones_bf16):
    t = x.shape[0]
    s = jnp.dot(jnp.concatenate(_split_bf16(x), axis=0), ones_bf16, preferred_element_type=F32)
    return s[:t] + s[t:]


def _dot_split(x, w_hi, w_lo):
    hi, lo = _split_bf16(x)
    acc = jnp.dot(hi, w_hi, preferred_element_type=F32)
    acc = acc + jnp.dot(lo, w_hi, preferred_element_type=F32)
    return acc + jnp.dot(hi, w_lo, preferred_element_type=F32)


def _segment_ones(width, seg):
    idx = np.arange(width) // seg
    return jnp.asarray(idx[:, None] == idx[None, :], dtype=BF16)


def _mod_kernel(c_ref, w_ref, b_ref, o_ref):
    c = c_ref[...]
    sc = c * _sigmoid(c)
    o_ref[...] = jnp.dot(sc, w_ref[...], precision=HIGHEST, preferred_element_type=F32) + b_ref[...]


def _modulation(c_rows, w_mod, b_mod):
    rows, d = c_rows.shape
    n = w_mod.shape[1]
    tn = 1536
    return pl.pallas_call(
        _mod_kernel,
        grid=(n // tn,),
        in_specs=[pl.BlockSpec((rows, d), lambda j: (0, 0)),
                  pl.BlockSpec((d, tn), lambda j: (0, j)),
                  pl.BlockSpec((1, tn), lambda j: (0, j))],
        out_specs=pl.BlockSpec((rows, tn), lambda j: (0, j)),
        out_shape=jax.ShapeDtypeStruct((rows, n), F32),
        compiler_params=_cparams(("arbitrary",)),
        name="modulation",
    )(c_rows, w_mod, b_mod)


def _in_proj_kernel(x_ref, g_ref, b_ref, s_ref, sh_ref, w_ref, pr_ref, pt_ref):
    h = _layer_norm(x_ref[0], g_ref[...], b_ref[...])
    u = h * (1.0 + s_ref[0]) + sh_ref[0]
    p = jnp.dot(u.astype(BF16), w_ref[...], preferred_element_type=F32)
    pr_ref[0] = p[:, :RWKV_COLS]
    pt_ref[0] = p[:, RWKV_COLS:]


def _in_proj(x, ln_g, ln_b, s1, sh1, w_in_bf16):
    b, n, d = x.shape
    tm = 256
    cols = w_in_bf16.shape[1]
    return pl.pallas_call(
        _in_proj_kernel,
        grid=(b, n // tm),
        in_specs=[pl.BlockSpec((1, tm, d), lambda bi, i: (bi, i, 0)),
                  pl.BlockSpec((1, d), lambda bi, i: (0, 0)),
                  pl.BlockSpec((1, d), lambda bi, i: (0, 0)),
                  pl.BlockSpec((1, 1, d), lambda bi, i: (bi, 0, 0)),
                  pl.BlockSpec((1, 1, d), lambda bi, i: (bi, 0, 0)),
                  pl.BlockSpec((d, cols), lambda bi, i: (0, 0))],
        out_specs=[pl.BlockSpec((1, tm, RWKV_COLS), lambda bi, i: (bi, i, 0)),
                   pl.BlockSpec((1, tm, RET_COLS), lambda bi, i: (bi, i, 0))],
        out_shape=[jax.ShapeDtypeStruct((b, n, RWKV_COLS), F32),
                   jax.ShapeDtypeStruct((b, n, RET_COLS), F32)],
        compiler_params=_cparams(("arbitrary", "arbitrary")),
        name="in_proj",
    )(x, ln_g, ln_b, s1, sh1, w_in_bf16)


def _rwkv_prepare_kernel(cur_ref, prev_ref, next_ref, mu_ref, w0_ref, w2_ref, a0_ref, a2_ref, g2_ref,
                         kk_scale_ref, ka_ref, rk_ref, ones_ref,
                         r_ref, v_ref, kk_ref, w_ref, kd_ref, bb_ref, g_ref, bonus_ref,
                         *, grid_shift, n_tok):
    cur = cur_ref[0]
    t, c = cur.shape
    row = lax.broadcasted_iota(jnp.int32, (t, c), 0)
    lane = lax.broadcasted_iota(jnp.int32, (t, c), 1)
    prev_tok = pltpu.roll(cur, 1, 0)
    next_tok = pltpu.roll(cur, t - 1, 0)
    if grid_shift:
        col = row & (GRID_W - 1)
        tok = row + pl.program_id(1) * t
        left = jnp.where(col > 0, prev_tok, 0.0)
        right = jnp.where(col < GRID_W - 1, next_tok, 0.0)
        up = jnp.where(tok >= GRID_W, jnp.concatenate([prev_ref[0], cur[:t - GRID_W]], axis=0), 0.0)
        down = jnp.where(tok < n_tok - GRID_W, jnp.concatenate([cur[GRID_W:], next_ref[0]], axis=0), 0.0)
        cm = lane & 3
        shifted = jnp.where(cm == 0, left, jnp.where(cm == 1, right, jnp.where(cm == 2, up, down)))
    else:
        prev_tok = jnp.where(row > 0, prev_tok, 0.0)
        next_tok = jnp.where(row < t - 1, next_tok, 0.0)
        shifted = jnp.where((lane & 1) == 0, prev_tok, next_tok)
    pm = cur + mu_ref[...] * (shifted - cur)

    r = pm[:, 0:D_RWKV]
    k = pm[:, D_RWKV:2 * D_RWKV]
    v = pm[:, 2 * D_RWKV:3 * D_RWKV]
    o = 3 * D_RWKV
    lw = pm[:, o:o + 2 * DECAY_LORA]
    la = pm[:, o + 2 * DECAY_LORA:o + 2 * (DECAY_LORA + AAA_LORA)]
    lg = pm[:, o + 2 * (DECAY_LORA + AAA_LORA):]

    w = w0_ref[...] + _dot_split(jnp.tanh(lw), w2_ref[0], w2_ref[1])
    log_decay = -EXP_NEG_HALF * _sigmoid(w)
    a = _sigmoid(a0_ref[...] + _dot_split(la, a2_ref[0], a2_ref[1]))
    gate = _dot_split(_sigmoid(lg), g2_ref[0], g2_ref[1])

    ones = ones_ref[...]
    kk_raw = k * kk_scale_ref[...]
    kk = kk_raw / jnp.maximum(jnp.sqrt(_segsum(kk_raw * kk_raw, ones)), 1e-12)
    ka = ka_ref[...]
    a0 = a[:, :D_RWKV]
    a1 = a[:, D_RWKV:]
    kd0 = k * (1.0 + (a0 - 1.0) * ka)
    kd1 = k * (1.0 + (a1 - 1.0) * ka)
    bonus = _segsum(r * (kd0 + kd1) * rk_ref[...], ones) * v

    r_ref[0] = r
    v_ref[0] = v
    kk_ref[0] = kk
    w_ref[0] = log_decay
    kd_ref[0] = jnp.concatenate([kd0, kd1], axis=-1)
    bb_ref[0] = jnp.concatenate([kk * a0, kk * a1], axis=-1)
    g_ref[0] = gate
    bonus_ref[0] = bonus


def _rwkv_prepare(pr, params, grid_shift):
    b, n, c = pr.shape
    t = 256
    if not grid_shift:
        assert n == t, "sequence token shift is written for a single tile"
    halo_blocks = n // GRID_W
    per_tile = t // GRID_W
    small = lambda shape: pl.BlockSpec(shape, lambda bi, i: (0,) * len(shape))
    tok_spec = lambda width: pl.BlockSpec((1, t, width), lambda bi, i: (bi, i, 0))
    out_widths = (D_RWKV, D_RWKV, D_RWKV, 2 * D_RWKV, 2 * D_RWKV, 2 * D_RWKV, D_RWKV, D_RWKV)
    kernel = functools.partial(_rwkv_prepare_kernel, grid_shift=grid_shift, n_tok=n)
    return pl.pallas_call(
        kernel,
        grid=(b, n // t),
        in_specs=[tok_spec(c),
                  pl.BlockSpec((1, GRID_W, c), lambda bi, i: (bi, jnp.maximum(i * per_tile - 1, 0), 0)),
                  pl.BlockSpec((1, GRID_W, c),
                               lambda bi, i: (bi, jnp.minimum((i + 1) * per_tile, halo_blocks - 1), 0)),
                  small((1, c)), small((1, 2 * D_RWKV)), small((2, 2 * DECAY_LORA, 2 * D_RWKV)),
                  small((1, 2 * D_RWKV)), small((2, 2 * AAA_LORA, 2 * D_RWKV)), small((2, GATE_LORA, D_RWKV)),
                  small((1, D_RWKV)), small((1, D_RWKV)), small((1, D_RWKV)), small((D_RWKV, D_RWKV))],
        out_specs=[tok_spec(wd) for wd in out_widths],
        out_shape=[jax.ShapeDtypeStruct((b, n, wd), F32) for wd in out_widths],
        compiler_params=_cparams(("arbitrary", "arbitrary")),
        name="rwkv_prepare",
    )(pr, pr, pr, *params)


def _bdot(a, b):
    return jnp.dot(a.astype(BF16), b.astype(BF16), preferred_element_type=F32)


def _bdot_nt(a, b):
    return lax.dot_general(a.astype(BF16), b.astype(BF16), (((1,), (1,)), ((), ())), preferred_element_type=F32)


def _bdot_tn(a, b):
    return lax.dot_general(a.astype(BF16), b.astype(BF16), (((0,), (0,)), ((), ())), preferred_element_type=F32)


def _wkv7_chunk_kernel(*refs, n_ctx_chunks):
    c = WKV_CHUNK
    p = 2 * c
    n_in = 12
    in_refs = (refs[:n_in], refs[n_in:2 * n_in])
    y_refs = refs[2 * n_in:2 * n_in + 2]
    state_ref = refs[2 * n_in + 2]
    n = pl.program_id(0)
    n_batch = in_refs[0][0].shape[0]
    pairs_per_batch = RWKV_HEADS // 2
    pairs_per_dir = n_batch * pairs_per_batch

    @pl.when(n == 0)
    def _():
        state_ref[...] = jnp.zeros_like(state_ref)

    is_ctx = n < n_ctx_chunks
    ti = lax.broadcasted_iota(jnp.int32, (c, c), 0)
    tj = lax.broadcasted_iota(jnp.int32, (c, c), 1)
    ri = lax.broadcasted_iota(jnp.int32, (p, p), 0)
    ci = lax.broadcasted_iota(jnp.int32, (p, p), 1)
    same_head = (ri >= c) == (ci >= c)
    ii = ri & (c - 1)
    jj = ci & (c - 1)
    eye = (ri == ci).astype(F32)
    first = lax.broadcasted_iota(jnp.int32, (c, p), 1) < RWKV_HEAD

    def stack(x):
        return jnp.concatenate([jnp.where(first, x, 0.0), jnp.where(first, 0.0, x)], axis=0)

    def unstack(x):
        return x[:c] + x[c:]

    a_st, r_st, k_st, b_st, k2_st, b2_st, v_st, g_chunk, earlier, upto_self = ([] for _ in range(10))
    for d in range(2):
        r_l, v_l, kk_l, lw_l, kd_l, bb_l, r_c, v_c, kk_c, lw_c, kd_c, bb_c = in_refs[d]
        pick = lambda xc, xl: jnp.concatenate(
            [jnp.where(is_ctx, xc[bi], xl[bi]) for bi in range(n_batch)], axis=-1)
        r, v, kk, lw, kd, bb = (pick(r_c, r_l), pick(v_c, v_l), pick(kk_c, kk_l), pick(lw_c, lw_l),
                                pick(kd_c, kd_l), pick(bb_c, bb_l))
        before = (tj < ti) if d == 0 else (tj > ti)
        upto = (before | (ti == tj)).astype(BF16)
        hi = lw.astype(BF16)
        r1 = lw - hi.astype(F32)
        mid = r1.astype(BF16)
        lo = (r1 - mid.astype(F32)).astype(BF16)
        cum = (jnp.dot(upto, hi, preferred_element_type=F32) + jnp.dot(upto, mid, preferred_element_type=F32)
               + jnp.dot(upto, lo, preferred_element_type=F32))
        tot = jnp.sum(lw, axis=0, keepdims=True)
        e_neg = jnp.exp(-cum)
        e_rem = jnp.exp(tot - cum)
        alpha = kk * jnp.exp(cum - lw)
        rho = r * jnp.exp(cum)
        beta = bb * e_neg
        kappa = kd * e_neg
        kappa_rem = kd * e_rem
        beta_rem = bb * e_rem
        g_all = jnp.exp(tot)
        pair_before = same_head & ((jj < ii) if d == 0 else (jj > ii))
        pair_upto = pair_before | (ri == ci)
        for hp in range(pairs_per_dir):
            sl = slice(hp * p, (hp + 1) * p)
            a_st.append(stack(alpha[:, sl]))
            r_st.append(stack(rho[:, sl]))
            k_st.append(stack(kappa[:, sl]))
            b_st.append(stack(beta[:, sl]))
            k2_st.append(stack(kappa_rem[:, sl]))
            b2_st.append(stack(beta_rem[:, sl]))
            v_st.append(stack(v[:, sl]))
            g_chunk.append(g_all[:, sl])
            earlier.append(pair_before)
            upto_self.append(pair_upto)

    pairs = range(2 * pairs_per_dir)
    g = [_bdot_nt(jnp.concatenate([a_st[h], r_st[h]], axis=0), jnp.concatenate([k_st[h], b_st[h]], axis=0))
         for h in pairs]
    m1 = [jnp.where(earlier[h], g[h][:p, :p], 0.0) for h in pairs]
    m2 = [jnp.where(earlier[h], g[h][:p, p:], 0.0) for h in pairs]
    n1 = [jnp.where(upto_self[h], g[h][p:, :p], 0.0) for h in pairs]
    n2 = [jnp.where(upto_self[h], g[h][p:, p:], 0.0) for h in pairs]

    in_block = (ii >> 3) == (jj >> 3)
    pw = [-jnp.where(in_block, m2[h], 0.0) for h in pairs]
    inv = [eye + pw[h] for h in pairs]
    for _ in range(2):
        pw = [_bdot(pw[h], pw[h]) for h in pairs]
        inv = [inv[h] + _bdot(inv[h], pw[h]) for h in pairs]
    for sh in (3, 4, 5):
        off = ((ii >> (sh + 1)) == (jj >> (sh + 1))) & ((ii >> sh) != (jj >> sh))
        left = [_bdot(inv[h], jnp.where(off, m2[h], 0.0)) for h in pairs]
        inv = [inv[h] - _bdot(left[h], inv[h]) for h in pairs]

    m1v = [_bdot(m1[h], v_st[h]) for h in pairs]
    n1v = [_bdot(n1[h], v_st[h]) for h in pairs]
    au = [_bdot(inv[h], jnp.concatenate([a_st[h], m1v[h]], axis=1)) for h in pairs]
    nn = [_bdot(n2[h], au[h]) for h in pairs]
    pc = [_bdot_tn(b2_st[h], au[h][:, :p]) for h in pairs]
    qc_t = [_bdot_tn(jnp.concatenate([v_st[h], -au[h][:, p:]], axis=0),
                     jnp.concatenate([k2_st[h], b2_st[h]], axis=0)) for h in pairs]
    s0 = [state_ref[h] for h in pairs]
    y = [_bdot_nt(unstack(r_st[h] - nn[h][:, :p]), s0[h]) + unstack(n1v[h] - nn[h][:, p:]) for h in pairs]
    s_dec = [_bdot_nt(s0[h], pc[h]) for h in pairs]
    for h in pairs:
        d, hp = divmod(h, pairs_per_dir)
        bi, hpb = divmod(hp, pairs_per_batch)
        y_refs[d][bi, :, hpb * p:(hpb + 1) * p] = y[h]
        state_ref[h] = s0[h] * g_chunk[h] - s_dec[h] + qc_t[h]


def _wkv7(lat, ctx, b, n_tok, n_ctx):
    c = WKV_CHUNK
    ncx = n_ctx // c
    nl = n_tok // c
    lat_idx = (lambda n: jnp.maximum(n - ncx, 0), lambda n: nl - 1 - jnp.maximum(n - ncx, 0))
    ctx_idx = (lambda n: jnp.minimum(n, ncx - 1), lambda n: ncx - 1 - jnp.minimum(n, ncx - 1))

    def specs(idx, d):
        shared = pl.BlockSpec((b, c, D_RWKV), lambda n: (0, idx(n), 0))
        per_dir = pl.BlockSpec((b, c, D_RWKV), lambda n: (0, idx(n), d))
        return [shared, shared, shared, per_dir, per_dir, per_dir]

    in_specs, args = [], []
    for d in range(2):
        in_specs += specs(lat_idx[d], d) + specs(ctx_idx[d], d)
        args += list(lat) + list(ctx)
    return pl.pallas_call(
        functools.partial(_wkv7_chunk_kernel, n_ctx_chunks=ncx),
        grid=(ncx + nl,),
        in_specs=in_specs,
        out_specs=[pl.BlockSpec((b, c, D_RWKV), lambda n, d=d: (0, lat_idx[d](n), 0)) for d in range(2)],
        out_shape=[jax.ShapeDtypeStruct((b, n_tok, D_RWKV), F32)] * 2,
        scratch_shapes=[pltpu.VMEM((2 * b * RWKV_HEADS // 2, 2 * RWKV_HEAD, 2 * RWKV_HEAD), F32)],
        compiler_params=_cparams(("arbitrary",)),
        name="wkv7_chunk",
    )(*args)


def _rope(z, cos_t, sin_t):
    lane = lax.broadcasted_iota(jnp.int32, z.shape, 1)
    half = RET_HEAD // 4
    partner = jnp.where((lane & (2 * half - 1)) < half, pltpu.roll(z, RET_HEAD - half, 1), pltpu.roll(z, half, 1))
    return z * cos_t + partner * sin_t


def _retention_kernel(dec_ref, fwd_ref, bwd_ref, ctx_ref, cosf_ref, sinf_ref, cosb_ref, sinb_ref,
                      yf_ref, yb_ref, state_ref, dmat_ref, tail_ref, head_ref, cdec_ref):
    c = RET_CHUNK
    scale = RET_HEAD ** -0.5
    ii = lax.broadcasted_iota(jnp.int32, (c, c), 0)
    jj = lax.broadcasted_iota(jnp.int32, (c, c), 1)
    pos = lax.broadcasted_iota(jnp.int32, (c, RET_HEAD), 0).astype(F32)
    n_ctx_chunks = ctx_ref.shape[1] // c

    def head_slices(ref_val, h):
        q = ref_val[:, h * RET_HEAD:(h + 1) * RET_HEAD]
        k = ref_val[:, D_RET + h * RET_HEAD:D_RET + (h + 1) * RET_HEAD]
        v = ref_val[:, 2 * D_RET + h * RET_HEAD:2 * D_RET + (h + 1) * RET_HEAD]
        return q, k, v

    chains = [(d, h) for d in range(2) for h in range(RET_HEADS)]

    @pl.when(pl.program_id(1) == 0)
    def _():
        for d, h in chains:
            x = jnp.full((1, RET_HEAD), dec_ref[d, h], F32)
            lg = -(jnp.maximum(x, 0.0) + jnp.log(1.0 + jnp.exp(-jnp.abs(x))))
            chunk_decay = jnp.exp(lg * float(c))
            tail = jnp.exp(lg * ((c - 1.0 - pos) if d == 0 else pos))
            rel = (ii - jj) if d == 0 else (jj - ii)
            mask = (rel >= 0) if d == 0 else (rel > 0)
            dmat_ref[d, h] = jnp.where(mask, jnp.exp(lg * jnp.maximum(rel, 0).astype(F32)), 0.0)
            tail_ref[d, h] = tail
            head_ref[d, h] = jnp.exp(lg * ((pos + 1.0) if d == 0 else (c - pos)))
            cdec_ref[d, h] = jnp.broadcast_to(chunk_decay, (SUBLANES, RET_HEAD))
            s = jnp.zeros((RET_HEAD, RET_HEAD), F32)
            order = range(n_ctx_chunks) if d == 0 else range(n_ctx_chunks - 1, -1, -1)
            for cc in order:
                _, kc, vc = head_slices(ctx_ref[0, cc * c:(cc + 1) * c, :], h)
                s = s * chunk_decay + _bdot_tn(kc * scale * tail, vc)
            state_ref[d, h] = s

    qkv = []
    for d, h in chains:
        blk = fwd_ref[0] if d == 0 else bwd_ref[0]
        cos_t = cosf_ref[...] if d == 0 else cosb_ref[...]
        sin_t = sinf_ref[...] if d == 0 else sinb_ref[...]
        q, k, v = head_slices(blk, h)
        qkv.append((_rope(q, cos_t, sin_t), _rope(k, cos_t, sin_t) * scale, v.astype(BF16)))
    s0 = [state_ref[d, h] for d, h in chains]
    scores = [_bdot_nt(q, k) for q, k, _ in qkv]
    inner = [_bdot(scores[i] * dmat_ref[d, h], qkv[i][2]) for i, (d, h) in enumerate(chains)]
    cross = [_bdot(qkv[i][0] * head_ref[d, h], s0[i]) for i, (d, h) in enumerate(chains)]
    upd = [_bdot_tn(qkv[i][1] * tail_ref[d, h], qkv[i][2]) for i, (d, h) in enumerate(chains)]
    for i, (d, h) in enumerate(chains):
        state_ref[d, h] = s0[i] * cdec_ref[d, h, 0:1, :] + upd[i]
        out_ref = yf_ref if d == 0 else yb_ref
        out_ref[0, :, h * RET_HEAD:(h + 1) * RET_HEAD] = inner[i] + cross[i]


def _retention(pt, pt_ctx, ret_decay, cos_t, sin_t):
    b, n, _ = pt.shape
    c = RET_CHUNK
    nc = n // c
    qkv = 3 * D_RET
    fwd = lambda bi, i: (bi, i, 0)
    bwd = lambda bi, i: (bi, nc - 1 - i, 0)
    return pl.pallas_call(
        _retention_kernel,
        grid=(b, nc),
        in_specs=[pl.BlockSpec(memory_space=pltpu.SMEM),
                  pl.BlockSpec((1, c, qkv), fwd),
                  pl.BlockSpec((1, c, qkv), bwd),
                  pl.BlockSpec((1, pt_ctx.shape[1], qkv), lambda bi, i: (bi, 0, 0)),
                  pl.BlockSpec((c, RET_HEAD), lambda bi, i: (i, 0)),
                  pl.BlockSpec((c, RET_HEAD), lambda bi, i: (i, 0)),
                  pl.BlockSpec((c, RET_HEAD), lambda bi, i: (nc - 1 - i, 0)),
                  pl.BlockSpec((c, RET_HEAD), lambda bi, i: (nc - 1 - i, 0))],
        out_specs=[pl.BlockSpec((1, c, D_RET), fwd), pl.BlockSpec((1, c, D_RET), bwd)],
        out_shape=[jax.ShapeDtypeStruct((b, n, D_RET), F32), jax.ShapeDtypeStruct((b, n, D_RET), F32)],
        scratch_shapes=[pltpu.VMEM((2, RET_HEADS, RET_HEAD, RET_HEAD), F32),
                        pltpu.VMEM((2, RET_HEADS, c, c), F32),
                        pltpu.VMEM((2, RET_HEADS, c, RET_HEAD), F32),
                        pltpu.VMEM((2, RET_HEADS, c, RET_HEAD), F32),
                        pltpu.VMEM((2, RET_HEADS, SUBLANES, RET_HEAD), F32)],
        compiler_params=_cparams(("arbitrary", "arbitrary")),
        name="retention",
    )(ret_decay, pt, pt, pt_ctx, cos_t, sin_t, cos_t, sin_t)


def _rope_tables(n_tok):
    nf = RET_HEAD // 4
    lane = np.arange(RET_HEAD)
    inv = ROPE_BASE ** (-jnp.arange(nf, dtype=F32) / nf)
    t = jnp.arange(n_tok)
    pos = jnp.where((lane // (2 * nf) == 0)[None, :], (t // GRID_W)[:, None], (t % GRID_W)[:, None]).astype(F32)
    ang = pos * inv[lane % nf][None, :]
    sign = jnp.where((lane % (2 * nf)) < nf, -1.0, 1.0).astype(F32)
    return jnp.cos(ang), jnp.sin(ang) * sign[None, :]


def _group_norm(y, ones, seg, eps, g, b):
    mu = _segsum(y, ones) * (1.0 / seg)
    yc = y - mu
    var = _segsum(yc * yc, ones) * (1.0 / seg)
    return yc * lax.rsqrt(var + eps) * g + b


def _out_proj_kernel(x_ref, yf_ref, yb_ref, bonus_ref, gate_ref, tf_ref, tb_ref, gt_ref,
                     embg_ref, embb_ref, g1_ref, s2_ref, sh2_ref, rgn_g_ref, rgn_b_ref, tgn_g_ref, tgn_b_ref,
                     ones_r_ref, wout_ref, ln1g_ref, ln1b_ref, wrh_ref, wrl_ref, br_ref,
                     h1_ref, u2_ref, logit_ref):
    y = yf_ref[0] + yb_ref[0]
    o_rwkv = _group_norm(y, ones_r_ref[...], RWKV_HEAD, RWKV_GN_EPS, rgn_g_ref[...], rgn_b_ref[...])
    o_rwkv = (o_rwkv + bonus_ref[0]) * gate_ref[0]
    yt = tf_ref[0] + tb_ref[0]
    gt = gt_ref[0]
    tgn_g = tgn_g_ref[...]
    tgn_b = tgn_b_ref[...]
    o_ret = jnp.concatenate(
        [_layer_norm(yt[:, h * RET_HEAD:(h + 1) * RET_HEAD], tgn_g[:, h * RET_HEAD:(h + 1) * RET_HEAD],
                     tgn_b[:, h * RET_HEAD:(h + 1) * RET_HEAD], RET_GN_EPS) for h in range(RET_HEADS)], axis=-1)
    o_ret = o_ret * (gt * _sigmoid(gt))
    cat = jnp.concatenate([o_rwkv, o_ret], axis=-1).astype(BF16)
    mix = jnp.dot(cat, wout_ref[...], preferred_element_type=F32)
    h = _layer_norm(x_ref[0], embg_ref[...], embb_ref[...])
    h1 = _layer_norm(DEEPNORM_ALPHA * h + g1_ref[0] * mix, ln1g_ref[...], ln1b_ref[...])
    u2 = h1 * (1.0 + s2_ref[0]) + sh2_ref[0]
    h1_ref[0] = h1
    _store_token_tiles(u2_ref, u2)
    logit_ref[0] = _dot_split(u2, wrh_ref[...], wrl_ref[...]) + br_ref[...]


def _out_proj(x, y_f, y_b, bonus, gate, t_f, t_b, pt, vecs, mats):
    b, n, d = x.shape
    t = 256
    tok = lambda width: pl.BlockSpec((1, t, width), lambda bi, i: (bi, i, 0))
    per_b = pl.BlockSpec((1, 1, d), lambda bi, i: (bi, 0, 0))
    small = lambda arr: pl.BlockSpec(arr.shape, lambda bi, i: (0,) * arr.ndim)
    (embg, embb, g1, s2, sh2, rgn_g, rgn_b, tgn_g, tgn_b, ln1g, ln1b, br) = vecs
    (ones_r, wout, wr_hi, wr_lo) = mats
    gt_spec = pl.BlockSpec((1, t, D_RET), lambda bi, i: (bi, i, 3))
    args = (x, y_f, y_b, bonus, gate, t_f, t_b, pt, embg, embb, g1, s2, sh2, rgn_g, rgn_b, tgn_g, tgn_b,
            ones_r, wout, ln1g, ln1b, wr_hi, wr_lo, br)
    in_specs = [tok(d)] + [tok(D_RWKV)] * 6 + [gt_spec, small(embg), small(embb), per_b, per_b, per_b,
                                                small(rgn_g), small(rgn_b), small(tgn_g), small(tgn_b),
                                                small(ones_r), small(wout), small(ln1g),
                                                small(ln1b), small(wr_hi), small(wr_lo), small(br)]
    return pl.pallas_call(
        _out_proj_kernel,
        grid=(b, n // t),
        in_specs=in_specs,
        out_specs=[tok(d), pl.BlockSpec((t * SUBLANES, LANES), lambda bi, i: (bi * (n // t) + i, 0)), tok(LANES)],
        out_shape=[jax.ShapeDtypeStruct((b, n, d), F32), jax.ShapeDtypeStruct((b * n * SUBLANES, LANES), F32),
                   jax.ShapeDtypeStruct((b, n, LANES), F32)],
        compiler_params=_cparams(("arbitrary", "arbitrary")),
        name="out_proj",
    )(*args)


ROUTE_E1, ROUTE_E2, ROUTE_G1, ROUTE_G2, ROUTE_RANK1, ROUTE_RANK2 = range(6)


def _lane_argmax(x, valid, lane):
    m = jnp.max(jnp.where(valid, x, -jnp.inf), axis=-1, keepdims=True)
    idx = jnp.min(jnp.where(valid & (x == m), lane, float(LANES)), axis=-1, keepdims=True)
    return m, idx


def _route_kernel(logit_ref, route_ref, route_t_ref, count_ref, carry_ref):
    @pl.when(pl.program_id(0) == 0)
    def _():
        carry_ref[...] = jnp.zeros_like(carry_ref)

    lg = logit_ref[...]
    t = lg.shape[0]
    lane = lax.broadcasted_iota(jnp.int32, lg.shape, 1).astype(F32)
    gmask = lane < N_GROUPS
    gmax = jnp.max(jnp.where(gmask, lg, -jnp.inf), axis=-1, keepdims=True)
    gexp = jnp.where(gmask, jnp.exp(lg - gmax), 0.0)
    gp = gexp / jnp.sum(gexp, axis=-1, keepdims=True)
    g_w, g_i = _lane_argmax(gp, gmask, lane)

    lo = N_GROUPS + EXPERTS_PER_GROUP * g_i
    emask = (lane >= lo) & (lane < lo + EXPERTS_PER_GROUP)
    emax = jnp.max(jnp.where(emask, lg, -jnp.inf), axis=-1, keepdims=True)
    eexp = jnp.where(emask, jnp.exp(lg - emax), 0.0)
    ep = eexp / jnp.sum(eexp, axis=-1, keepdims=True)
    p1, i1 = _lane_argmax(ep, emask, lane)
    p2, i2 = _lane_argmax(ep, emask & (lane != i1), lane)
    denom = p1 + p2
    gate1 = g_w * p1 / denom
    gate2 = g_w * p2 / denom
    e1 = i1 - N_GROUPS
    e2 = i2 - N_GROUPS

    oh1 = (lane == e1).astype(F32)
    oh2 = (lane == e2).astype(F32)
    cnt = oh1 + oh2
    ri = lax.broadcasted_iota(jnp.int32, (t, t), 0)
    ci = lax.broadcasted_iota(jnp.int32, (t, t), 1)
    before = (ci < ri).astype(BF16)
    seen = jnp.dot(before, cnt.astype(BF16), preferred_element_type=F32) + carry_ref[0:1, :]
    rank1 = jnp.sum(oh1 * seen, axis=-1, keepdims=True)
    rank2 = jnp.sum(oh2 * seen, axis=-1, keepdims=True)
    carry_ref[0:1, :] = carry_ref[0:1, :] + jnp.sum(cnt, axis=0, keepdims=True)

    out = jnp.zeros(lg.shape, F32)
    for slot, val in ((ROUTE_E1, e1.astype(F32)), (ROUTE_E2, e2.astype(F32)), (ROUTE_G1, gate1),
                      (ROUTE_G2, gate2), (ROUTE_RANK1, rank1), (ROUTE_RANK2, rank2)):
        out = jnp.where(lane == slot, val, out)
    route_ref[...] = out
    route_t_ref[...] = out.T[:SUBLANES]
    count_ref[...] = carry_ref[...]


def _route(logits):
    n = logits.shape[0]
    t = 256
    return pl.pallas_call(
        _route_kernel,
        grid=(n // t,),
        in_specs=[pl.BlockSpec((t, LANES), lambda i: (i, 0))],
        out_specs=[pl.BlockSpec((t, LANES), lambda i: (i, 0)), pl.BlockSpec((SUBLANES, t), lambda i: (0, i)),
                   pl.BlockSpec((SUBLANES, LANES), lambda i: (0, 0))],
        out_shape=[jax.ShapeDtypeStruct((n, LANES), F32), jax.ShapeDtypeStruct((SUBLANES, n), F32),
                   jax.ShapeDtypeStruct((SUBLANES, LANES), F32)],
        scratch_shapes=[pltpu.VMEM((SUBLANES, LANES), F32)],
        compiler_params=_cparams(("arbitrary",)),
        name="route",
    )(logits)


def _tile_gather_copy(src_hbm, idx_ref, buf, sem, slot, r):
    dst = buf.at[slot, pl.ds(pl.multiple_of(r * SUBLANES, SUBLANES), SUBLANES), :]
    return pltpu.make_async_copy(src_hbm.at[idx_ref[0, 0, r]], dst, sem.at[slot])


def _start_tile_gather(src_hbm, idx_ref, buf, sem, slot, rows, priorities):
    def body(g, carry):
        for j in range(GATHER_UNROLL):
            copy = _tile_gather_copy(src_hbm, idx_ref, buf, sem, slot, g * GATHER_UNROLL + j)
            copy.start(priority=priorities[j % len(priorities)])
        return carry
    lax.fori_loop(0, rows // GATHER_UNROLL, body, 0)


def _wait_tile_gather(src_hbm, idx_ref, buf, sem, slot, rows):
    def body(r, carry):
        _tile_gather_copy(src_hbm, idx_ref, buf, sem, slot, r).wait()
        return carry
    lax.fori_loop(0, rows, body, 0, unroll=GATHER_UNROLL)


def _expert_kernel(blk_expert_ref, n_used_ref, tok_ref, tok_next_ref, u_hbm, wg_ref, wu_ref, wd_ref, y_ref,
                   xbuf, sem, wg_s, wu_s, wd_s):
    i = pl.program_id(0)
    n_used = n_used_ref[0]
    slot = i % 2

    @pl.when(i == 0)
    def _():
        _start_tile_gather(u_hbm, tok_ref, xbuf, sem, 0, EXPERT_ROWS, GATHER_PRIORITIES)

    @pl.when(i + 1 < n_used)
    def _():
        _start_tile_gather(u_hbm, tok_next_ref, xbuf, sem, 1 - slot, EXPERT_ROWS, GATHER_PRIORITIES)

    @pl.when(i >= n_used)
    def _():
        y_ref[...] = jnp.zeros_like(y_ref)

    @pl.when(i < n_used)
    def _():
        @pl.when((i == 0) | (blk_expert_ref[i] != blk_expert_ref[jnp.maximum(i - 1, 0)]))
        def _():
            wg_s[...] = wg_ref[0].astype(BF16)
            wu_s[...] = wu_ref[0].astype(BF16)
            wd_s[...] = wd_ref[0].astype(BF16)

        _wait_tile_gather(u_hbm, tok_ref, xbuf, sem, slot, EXPERT_ROWS)
        x = _load_token_tiles(xbuf.at[slot], EXPERT_ROWS).astype(BF16)
        hg = jnp.dot(x, wg_s[...], preferred_element_type=F32)
        hu = jnp.dot(x, wu_s[...], preferred_element_type=F32)
        act = (hg * _sigmoid(hg) * hu).astype(BF16)
        _store_token_tiles(y_ref, jnp.dot(act, wd_s[...], preferred_element_type=F32))


def _expert_mlp(u2_tiles, slot_tok, block_expert, n_used, w_gate, w_up, w_down):
    n_blk = block_expert.shape[0]
    d = w_gate.shape[1]
    hdim = w_gate.shape[2]
    rows = EXPERT_ROWS
    tok3 = slot_tok.reshape(n_blk, 1, rows)
    used = lambda i, nu: jnp.minimum(i, nu[0] - 1)
    weight = lambda i, be, nu: (be[used(i, nu)], 0, 0)
    grid_spec = pltpu.PrefetchScalarGridSpec(
        num_scalar_prefetch=2,
        grid=(n_blk,),
        in_specs=[pl.BlockSpec((1, 1, rows), lambda i, be, nu: (used(i, nu), 0, 0), memory_space=pltpu.SMEM),
                  pl.BlockSpec((1, 1, rows), lambda i, be, nu: (used(i + 1, nu), 0, 0), memory_space=pltpu.SMEM),
                  pl.BlockSpec(memory_space=pl.ANY),
                  pl.BlockSpec((1, d, hdim), weight),
                  pl.BlockSpec((1, d, hdim), weight),
                  pl.BlockSpec((1, hdim, d), weight)],
        out_specs=pl.BlockSpec((rows * SUBLANES, LANES), lambda i, be, nu: (i, 0)),
        scratch_shapes=[pltpu.VMEM((2, rows * SUBLANES, LANES), F32), pltpu.SemaphoreType.DMA((2,)),
                        pltpu.VMEM((d, hdim), BF16), pltpu.VMEM((d, hdim), BF16), pltpu.VMEM((hdim, d), BF16)],
    )
    return pl.pallas_call(
        _expert_kernel,
        grid_spec=grid_spec,
        out_shape=jax.ShapeDtypeStruct((n_blk * rows * SUBLANES, LANES), F32),
        compiler_params=_cparams(("arbitrary",)),
        name="expert_mlp",
    )(block_expert, n_used, tok3, tok3, u2_tiles, w_gate, w_up, w_down)


def _combine_kernel(d1_ref, d2_ref, d1n_ref, d2n_ref, y_hbm, route_ref, h1_ref, g2_ref, lng_ref, lnb_ref,
                    o_ref, abuf, bbuf, sem_a, sem_b):
    i = pl.program_id(0)
    n = pl.num_programs(0)
    slot = i % 2
    rows = o_ref.shape[0]

    @pl.when(i == 0)
    def _():
        _start_tile_gather(y_hbm, d1_ref, abuf, sem_a, 0, rows, GATHER_PRIORITIES)
        _start_tile_gather(y_hbm, d2_ref, bbuf, sem_b, 0, rows, GATHER_PRIORITIES)

    @pl.when(i + 1 < n)
    def _():
        _start_tile_gather(y_hbm, d1n_ref, abuf, sem_a, 1 - slot, rows, GATHER_PRIORITIES)
        _start_tile_gather(y_hbm, d2n_ref, bbuf, sem_b, 1 - slot, rows, GATHER_PRIORITIES)

    _wait_tile_gather(y_hbm, d1_ref, abuf, sem_a, slot, rows)
    _wait_tile_gather(y_hbm, d2_ref, bbuf, sem_b, slot, rows)
    route = route_ref[...]
    f = (_load_token_tiles(abuf.at[slot], rows) * route[:, ROUTE_G1:ROUTE_G1 + 1]
         + _load_token_tiles(bbuf.at[slot], rows) * route[:, ROUTE_G2:ROUTE_G2 + 1])
    o_ref[...] = _layer_norm(DEEPNORM_ALPHA * h1_ref[...] + g2_ref[0] * f, lng_ref[...], lnb_ref[...])


def _combine(y_tiles, dest1, dest2, route, h1, g2, ln_g, ln_b, tokens_per_batch):
    n, d = h1.shape
    t = COMBINE_ROWS
    nt = n // t
    per_b = tokens_per_batch // t
    d1 = dest1.reshape(nt, 1, t)
    d2 = dest2.reshape(nt, 1, t)
    cur = pl.BlockSpec((1, 1, t), lambda i: (i, 0, 0), memory_space=pltpu.SMEM)
    nxt = pl.BlockSpec((1, 1, t), lambda i: (jnp.minimum(i + 1, nt - 1), 0, 0), memory_space=pltpu.SMEM)
    small = lambda arr: pl.BlockSpec(arr.shape, lambda i: (0,) * arr.ndim)
    return pl.pallas_call(
        _combine_kernel,
        grid=(nt,),
        in_specs=[cur, cur, nxt, nxt, pl.BlockSpec(memory_space=pl.ANY),
                  pl.BlockSpec((t, LANES), lambda i: (i, 0)),
                  pl.BlockSpec((t, d), lambda i: (i, 0)),
                  pl.BlockSpec((1, 1, d), lambda i: (i // per_b, 0, 0)),
                  small(ln_g), small(ln_b)],
        out_specs=pl.BlockSpec((t, d), lambda i: (i, 0)),
        out_shape=jax.ShapeDtypeStruct((n, d), F32),
        scratch_shapes=[pltpu.VMEM((2, t * SUBLANES, LANES), F32), pltpu.VMEM((2, t * SUBLANES, LANES), F32),
                        pltpu.SemaphoreType.DMA((2,)), pltpu.SemaphoreType.DMA((2,))],
        compiler_params=_cparams(("arbitrary",)),
        name="combine",
    )(d1, d2, d1, d2, y_tiles.reshape(-1, SUBLANES, LANES), route, h1, g2, ln_g, ln_b)


def _hi_lo(w):
    hi = w.astype(BF16)
    return jnp.stack([hi, (w - hi.astype(F32)).astype(BF16)])


def _block_diag2(w):
    z = jnp.zeros_like(w[0])
    return jnp.concatenate([jnp.concatenate([w[0], z], axis=1), jnp.concatenate([z, w[1]], axis=1)], axis=0)


def kernel(x, c, ctx, c_ctx, emb_ln_g, emb_ln_b, w_mod, b_mod, w_in, tshift_mu, rwkv_w0, rwkv_w2, rwkv_a0, rwkv_a2, rwkv_g2, rwkv_k_k, rwkv_k_a, rwkv_r_k, rwkv_gn_g, rwkv_gn_b, ret_decay, ret_gn_g, ret_gn_b, w_out, ln1_g, ln1_b, router_group, router_group_bias, router_expert, router_expert_bias, expert_w_gate, expert_w_up, expert_w_down, ln2_g, ln2_b):
    assert w_mod.shape[0] == 1, "written for DEPTH == 1 (context outputs are never emitted)"
    b, n_tok, d = x.shape
    n_ctx = ctx.shape[1]
    row = lambda v: v.reshape(1, -1)

    c_rows = jnp.zeros((SUBLANES, d), F32).at[:b].set(c).at[b].set(c_ctx)
    mod = _modulation(c_rows, w_mod[0], row(b_mod[0]))
    sh1, s1, g1, sh2, s2, g2 = [mod[:b, j * d:(j + 1) * d].reshape(b, 1, d) for j in range(6)]
    sh1c, s1c = [jnp.broadcast_to(mod[b, j * d:(j + 1) * d].reshape(1, 1, d), (b, 1, d)) for j in range(2)]

    w_in_bf16 = w_in[0].astype(BF16)
    pr, pt = _in_proj(x, row(emb_ln_g), row(emb_ln_b), s1, sh1, w_in_bf16)
    pr_c, pt_c = _in_proj(ctx, row(emb_ln_g), row(emb_ln_b), s1c, sh1c, w_in_bf16)

    prep_params = (row(tshift_mu[0]), row(rwkv_w0[0]), _hi_lo(_block_diag2(rwkv_w2[0])), row(rwkv_a0[0]),
                   _hi_lo(_block_diag2(rwkv_a2[0])), _hi_lo(rwkv_g2[0]), row(rwkv_k_k[0]), row(rwkv_k_a[0]),
                   row(rwkv_r_k[0]),
                   _segment_ones(D_RWKV, RWKV_HEAD))
    lat = _rwkv_prepare(pr, prep_params, grid_shift=True)
    cx = _rwkv_prepare(pr_c, prep_params, grid_shift=False)
    r_l, v_l, kk_l, w_l, kd_l, bb_l, gate_l, bonus_l = lat
    r_c, v_c, kk_c, w_c, kd_c, bb_c, _, _ = cx

    y_f, y_b = _wkv7((r_l, v_l, kk_l, w_l, kd_l, bb_l), (r_c, v_c, kk_c, w_c, kd_c, bb_c), b, n_tok, n_ctx)

    cos_t, sin_t = _rope_tables(n_tok)
    t_f, t_b = _retention(pt, pt_c, ret_decay[0], cos_t, sin_t)

    wr = jnp.zeros((d, LANES), F32).at[:, :N_GROUPS].set(router_group[0])
    wr = wr.at[:, N_GROUPS:N_GROUPS + N_EXPERTS].set(router_expert[0])
    br = jnp.zeros((1, LANES), F32).at[0, :N_GROUPS].set(router_group_bias[0])
    br = br.at[0, N_GROUPS:N_GROUPS + N_EXPERTS].set(router_expert_bias[0].reshape(-1))
    vecs = (row(emb_ln_g), row(emb_ln_b), g1, s2, sh2, row(rwkv_gn_g[0]), row(rwkv_gn_b[0]),
            row(ret_gn_g[0]), row(ret_gn_b[0]), row(ln1_g[0]), row(ln1_b[0]), br)
    wr_hi_lo = _hi_lo(wr)
    mats = (_segment_ones(D_RWKV, RWKV_HEAD), w_out[0].astype(BF16), wr_hi_lo[0], wr_hi_lo[1])
    h1, u2, logits = _out_proj(x, y_f, y_b, bonus_l, gate_l, t_f, t_b, pt, vecs, mats)

    n_all = b * n_tok
    route, route_t, counts = _route(logits.reshape(n_all, LANES))

    e1 = route_t[ROUTE_E1].astype(jnp.int32)
    e2 = route_t[ROUTE_E2].astype(jnp.int32)
    cnt = counts[0, :N_EXPERTS].astype(jnp.int32)
    padded = ((cnt + EXPERT_ROWS - 1) // EXPERT_ROWS) * EXPERT_ROWS
    pends = jnp.cumsum(padded)
    pstarts = pends - padded
    expert_ids = jnp.arange(N_EXPERTS, dtype=jnp.int32)
    start_of = lambda e: jnp.sum(jnp.where(e[:, None] == expert_ids[None, :], pstarts[None, :], 0), axis=1)
    dest1 = start_of(e1) + route_t[ROUTE_RANK1].astype(jnp.int32)
    dest2 = start_of(e2) + route_t[ROUTE_RANK2].astype(jnp.int32)
    n_blk = -(-(n_all * 2) // EXPERT_ROWS) + N_EXPERTS
    tok_ids = jnp.arange(n_all, dtype=jnp.int32)
    slot_tok = jnp.zeros((n_blk * EXPERT_ROWS,), jnp.int32).at[jnp.concatenate([dest1, dest2])].set(
        jnp.concatenate([tok_ids, tok_ids]))
    block_start = jnp.arange(n_blk, dtype=jnp.int32) * EXPERT_ROWS
    block_expert = jnp.minimum(jnp.sum((block_start[:, None] >= pends[None, :]).astype(jnp.int32), axis=1),
                               N_EXPERTS - 1)

    n_used = (pends[N_EXPERTS - 1:] // EXPERT_ROWS).astype(jnp.int32)
    y_tiles = _expert_mlp(u2.reshape(n_all, SUBLANES, LANES), slot_tok, block_expert, n_used,
                          expert_w_gate[0], expert_w_up[0], expert_w_down[0])
    out = _combine(y_tiles, dest1, dest2, route, h1.reshape(n_all, d), g2, row(ln2_g[0]), row(ln2_b[0]), n_tok)
    return out.reshape(b, n_tok, d)
```

```python
import functools
import math

import jax
import jax.numpy as jnp
import numpy as np
from jax import lax
from jax.experimental import pallas as pl
from jax.experimental.pallas import tpu as pltpu
from jax.experimental.pallas import tpu_sc as plsc

F32 = jnp.float32
BF16 = jnp.bfloat16
HIGHEST = lax.Precision.HIGHEST

GRID_W = 64
D_RWKV = 512
RWKV_HEAD = 64
RWKV_HEADS = D_RWKV // RWKV_HEAD
DECAY_LORA = 64
AAA_LORA = 64
GATE_LORA = 128
D_RET = 512
RET_HEADS = 4
RET_HEAD = D_RET // RET_HEADS
RET_CHUNK = 128
RWKV_COLS = 3 * D_RWKV + 2 * (DECAY_LORA + AAA_LORA) + GATE_LORA
RET_COLS = 4 * D_RET
N_GROUPS = 4
EXPERTS_PER_GROUP = 8
N_EXPERTS = N_GROUPS * EXPERTS_PER_GROUP
EXPERT_HIDDEN = 512
MOE_BLOCK = 128
ROPE_BASE = 10000.0
LN_EPS = 1e-5
RWKV_GN_EPS = 64e-5
RET_GN_EPS = 1e-5
DEEPNORM_ALPHA = 2.0 ** 0.25
EXP_NEG_HALF = math.exp(-0.5)

LANES = 128
SUBLANES = 8
VMEM_LIMIT_BYTES = 56 * 1024 * 1024

WKV_CHUNK = 64


EXPERT_ROWS = 256
COMBINE_ROWS = 128
GATHER_UNROLL = 8
SC_GATHER_GROUP = 4
GATHER_PRIORITIES = (0, 1)


def _store_token_tiles(ref, x):
    rows = x.shape[0]
    for j in range(x.shape[1] // LANES):
        ref[pl.ds(j, rows, stride=SUBLANES), :] = x[:, j * LANES:(j + 1) * LANES]


def _load_token_tiles(ref, rows):
    return jnp.concatenate([ref[pl.ds(j, rows, stride=SUBLANES), :] for j in range(SUBLANES)], axis=-1)


def _cparams(sem):
    return pltpu.CompilerParams(dimension_semantics=sem, vmem_limit_bytes=VMEM_LIMIT_BYTES)


def _layer_norm(x, g, b, eps=LN_EPS):
    mu = jnp.mean(x, axis=-1, keepdims=True)
    xc = x - mu
    var = jnp.mean(xc * xc, axis=-1, keepdims=True)
    return xc * lax.rsqrt(var + eps) * g + b


def _sigmoid(x):
    return 1.0 / (1.0 + jnp.exp(-x))


def _split_bf16(x):
    hi = x.astype(BF16)
    return hi, (x - hi.astype(F32)).astype(BF16)


def _segsum(x, ones_bf16):
    t = x.shape[0]
    s = jnp.dot(jnp.concatenate(_split_bf16(x), axis=0), ones_bf16, preferred_element_type=F32)
    return s[:t] + s[t:]


def _dot_split(x, w_hi, w_lo):
    hi, lo = _split_bf16(x)
    acc = jnp.dot(hi, w_hi, preferred_element_type=F32)
    acc = acc + jnp.dot(lo, w_hi, preferred_element_type=F32)
    return acc + jnp.dot(hi, w_lo, preferred_element_type=F32)


def _segment_ones(width, seg):
    idx = np.arange(width) // seg
    return jnp.asarray(idx[:, None] == idx[None, :], dtype=BF16)


def _mod_kernel(c_ref, w_ref, b_ref, o_ref):
    c = c_ref[...]
    sc = c * _sigmoid(c)
    o_ref[...] = jnp.dot(sc, w_ref[...], precision=HIGHEST, preferred_element_type=F32) + b_ref[...]


def _modulation(c_rows, w_mod, b_mod):
    rows, d = c_rows.shape
    n = w_mod.shape[1]
    tn = 1536
    return pl.pallas_call(
        _mod_kernel,
        grid=(n // tn,),
        in_specs=[pl.BlockSpec((rows, d), lambda j: (0, 0)),
                  pl.BlockSpec((d, tn), lambda j: (0, j)),
                  pl.BlockSpec((1, tn), lambda j: (0, j))],
        out_specs=pl.BlockSpec((rows, tn), lambda j: (0, j)),
        out_shape=jax.ShapeDtypeStruct((rows, n), F32),
        compiler_params=_cparams(("arbitrary",)),
        name="modulation",
    )(c_rows, w_mod, b_mod)


def _in_proj_kernel(x_ref, g_ref, b_ref, s_ref, sh_ref, w_ref, pr_ref, pt_ref):
    h = _layer_norm(x_ref[0], g_ref[...], b_ref[...])
    u = h * (1.0 + s_ref[0]) + sh_ref[0]
    p = jnp.dot(u.astype(BF16), w_ref[...], preferred_element_type=F32)
    pr_ref[0] = p[:, :RWKV_COLS]
    pt_ref[0] = p[:, RWKV_COLS:]


def _in_proj(x, ln_g, ln_b, s1, sh1, w_in_bf16):
    b, n, d = x.shape
    tm = 256
    cols = w_in_bf16.shape[1]
    return pl.pallas_call(
        _in_proj_kernel,
        grid=(b, n // tm),
        in_specs=[pl.BlockSpec((1, tm, d), lambda bi, i: (bi, i, 0)),
                  pl.BlockSpec((1, d), lambda bi, i: (0, 0)),
                  pl.BlockSpec((1, d), lambda bi, i: (0, 0)),
                  pl.BlockSpec((1, 1, d), lambda bi, i: (bi, 0, 0)),
                  pl.BlockSpec((1, 1, d), lambda bi, i: (bi, 0, 0)),
                  pl.BlockSpec((d, cols), lambda bi, i: (0, 0))],
        out_specs=[pl.BlockSpec((1, tm, RWKV_COLS), lambda bi, i: (bi, i, 0)),
                   pl.BlockSpec((1, tm, RET_COLS), lambda bi, i: (bi, i, 0))],
        out_shape=[jax.ShapeDtypeStruct((b, n, RWKV_COLS), F32),
                   jax.ShapeDtypeStruct((b, n, RET_COLS), F32)],
        compiler_params=_cparams(("arbitrary", "arbitrary")),
        name="in_proj",
    )(x, ln_g, ln_b, s1, sh1, w_in_bf16)


def _rwkv_prepare_kernel(cur_ref, prev_ref, next_ref, mu_ref, w0_ref, w2_ref, a0_ref, a2_ref, g2_ref,
                         kk_scale_ref, ka_ref, rk_ref, ones_ref,
                         r_ref, v_ref, kk_ref, w_ref, kd_ref, bb_ref, g_ref, bonus_ref,
                         *, grid_shift, n_tok):
    cur = cur_ref[0]
    t, c = cur.shape
    row = lax.broadcasted_iota(jnp.int32, (t, c), 0)
    lane = lax.broadcasted_iota(jnp.int32, (t, c), 1)
    prev_tok = pltpu.roll(cur, 1, 0)
    next_tok = pltpu.roll(cur, t - 1, 0)
    if grid_shift:
        col = row & (GRID_W - 1)
        tok = row + pl.program_id(1) * t
        left = jnp.where(col > 0, prev_tok, 0.0)
        right = jnp.where(col < GRID_W - 1, next_tok, 0.0)
        up = jnp.where(tok >= GRID_W, jnp.concatenate([prev_ref[0], cur[:t - GRID_W]], axis=0), 0.0)
        down = jnp.where(tok < n_tok - GRID_W, jnp.concatenate([cur[GRID_W:], next_ref[0]], axis=0), 0.0)
        cm = lane & 3
        shifted = jnp.where(cm == 0, left, jnp.where(cm == 1, right, jnp.where(cm == 2, up, down)))
    else:
        prev_tok = jnp.where(row > 0, prev_tok, 0.0)
        next_tok = jnp.where(row < t - 1, next_tok, 0.0)
        shifted = jnp.where((lane & 1) == 0, prev_tok, next_tok)
    pm = cur + mu_ref[...] * (shifted - cur)

    r = pm[:, 0:D_RWKV]
    k = pm[:, D_RWKV:2 * D_RWKV]
    v = pm[:, 2 * D_RWKV:3 * D_RWKV]
    o = 3 * D_RWKV
    lw = pm[:, o:o + 2 * DECAY_LORA]
    la = pm[:, o + 2 * DECAY_LORA:o + 2 * (DECAY_LORA + AAA_LORA)]
    lg = pm[:, o + 2 * (DECAY_LORA + AAA_LORA):]

    w = w0_ref[...] + _dot_split(jnp.tanh(lw), w2_ref[0], w2_ref[1])
    log_decay = -EXP_NEG_HALF * _sigmoid(w)
    a = _sigmoid(a0_ref[...] + _dot_split(la, a2_ref[0], a2_ref[1]))
    gate = _dot_split(_sigmoid(lg), g2_ref[0], g2_ref[1])

    ones = ones_ref[...]
    kk_raw = k * kk_scale_ref[...]
    kk = kk_raw / jnp.maximum(jnp.sqrt(_segsum(kk_raw * kk_raw, ones)), 1e-12)
    ka = ka_ref[...]
    a0 = a[:, :D_RWKV]
    a1 = a[:, D_RWKV:]
    kd0 = k * (1.0 + (a0 - 1.0) * ka)
    kd1 = k * (1.0 + (a1 - 1.0) * ka)
    bonus = _segsum(r * (kd0 + kd1) * rk_ref[...], ones) * v

    r_ref[0] = r
    v_ref[0] = v
    kk_ref[0] = kk
    w_ref[0] = log_decay
    kd_ref[0] = jnp.concatenate([kd0, kd1], axis=-1)
    bb_ref[0] = jnp.concatenate([kk * a0, kk * a1], axis=-1)
    g_ref[0] = gate
    bonus_ref[0] = bonus


def _rwkv_prepare(pr, params, grid_shift):
    b, n, c = pr.shape
    t = 256
    if not grid_shift:
        assert n == t, "sequence token shift is written for a single tile"
    halo_blocks = n // GRID_W
    per_tile = t // GRID_W
    small = lambda shape: pl.BlockSpec(shape, lambda bi, i: (0,) * len(shape))
    tok_spec = lambda width: pl.BlockSpec((1, t, width), lambda bi, i: (bi, i, 0))
    out_widths = (D_RWKV, D_RWKV, D_RWKV, 2 * D_RWKV, 2 * D_RWKV, 2 * D_RWKV, D_RWKV, D_RWKV)
    kernel = functools.partial(_rwkv_prepare_kernel, grid_shift=grid_shift, n_tok=n)
    return pl.pallas_call(
        kernel,
        grid=(b, n // t),
        in_specs=[tok_spec(c),
                  pl.BlockSpec((1, GRID_W, c), lambda bi, i: (bi, jnp.maximum(i * per_tile - 1, 0), 0)),
                  pl.BlockSpec((1, GRID_W, c),
                               lambda bi, i: (bi, jnp.minimum((i + 1) * per_tile, halo_blocks - 1), 0)),
                  small((1, c)), small((1, 2 * D_RWKV)), small((2, 2 * DECAY_LORA, 2 * D_RWKV)),
                  small((1, 2 * D_RWKV)), small((2, 2 * AAA_LORA, 2 * D_RWKV)), small((2, GATE_LORA, D_RWKV)),
                  small((1, D_RWKV)), small((1, D_RWKV)), small((1, D_RWKV)), small((D_RWKV, D_RWKV))],
        out_specs=[tok_spec(wd) for wd in out_widths],
        out_shape=[jax.ShapeDtypeStruct((b, n, wd), F32) for wd in out_widths],
        compiler_params=_cparams(("arbitrary", "arbitrary")),
        name="rwkv_prepare",
    )(pr, pr, pr, *params)


def _bdot(a, b):
    return jnp.dot(a.astype(BF16), b.astype(BF16), preferred_element_type=F32)


def _bdot_nt(a, b):
    return lax.dot_general(a.astype(BF16), b.astype(BF16), (((1,), (1,)), ((), ())), preferred_element_type=F32)


def _bdot_tn(a, b):
    return lax.dot_general(a.astype(BF16), b.astype(BF16), (((0,), (0,)), ((), ())), preferred_element_type=F32)


def _wkv7_chunk_kernel(*refs, n_ctx_chunks):
    c = WKV_CHUNK
    p = 2 * c
    n_in = 12
    in_refs = (refs[:n_in], refs[n_in:2 * n_in])
    y_refs = refs[2 * n_in:2 * n_in + 2]
    state_ref = refs[2 * n_in + 2]
    n = pl.program_id(0)
    n_batch = in_refs[0][0].shape[0]
    pairs_per_batch = RWKV_HEADS // 2
    pairs_per_dir = n_batch * pairs_per_batch

    @pl.when(n == 0)
    def _():
        state_ref[...] = jnp.zeros_like(state_ref)

    is_ctx = n < n_ctx_chunks
    ti = lax.broadcasted_iota(jnp.int32, (c, c), 0)
    tj = lax.broadcasted_iota(jnp.int32, (c, c), 1)
    ri = lax.broadcasted_iota(jnp.int32, (p, p), 0)
    ci = lax.broadcasted_iota(jnp.int32, (p, p), 1)
    same_head = (ri >= c) == (ci >= c)
    ii = ri & (c - 1)
    jj = ci & (c - 1)
    eye = (ri == ci).astype(F32)
    first = lax.broadcasted_iota(jnp.int32, (c, p), 1) < RWKV_HEAD

    def stack(x):
        return jnp.concatenate([jnp.where(first, x, 0.0), jnp.where(first, 0.0, x)], axis=0)

    def unstack(x):
        return x[:c] + x[c:]

    a_st, r_st, k_st, b_st, k2_st, b2_st, v_st, g_chunk, earlier, upto_self = ([] for _ in range(10))
    for d in range(2):
        r_l, v_l, kk_l, lw_l, kd_l, bb_l, r_c, v_c, kk_c, lw_c, kd_c, bb_c = in_refs[d]
        pick = lambda xc, xl: jnp.concatenate(
            [jnp.where(is_ctx, xc[bi], xl[bi]) for bi in range(n_batch)], axis=-1)
        r, v, kk, lw, kd, bb = (pick(r_c, r_l), pick(v_c, v_l), pick(kk_c, kk_l), pick(lw_c, lw_l),
                                pick(kd_c, kd_l), pick(bb_c, bb_l))
        before = (tj < ti) if d == 0 else (tj > ti)
        upto = (before | (ti == tj)).astype(BF16)
        hi = lw.astype(BF16)
        r1 = lw - hi.astype(F32)
        mid = r1.astype(BF16)
        lo = (r1 - mid.astype(F32)).astype(BF16)
        cum = (jnp.dot(upto, hi, preferred_element_type=F32) + jnp.dot(upto, mid, preferred_element_type=F32)
               + jnp.dot(upto, lo, preferred_element_type=F32))
        tot = jnp.sum(lw, axis=0, keepdims=True)
        e_neg = jnp.exp(-cum)
        e_rem = jnp.exp(tot - cum)
        alpha = kk * jnp.exp(cum - lw)
        rho = r * jnp.exp(cum)
        beta = bb * e_neg
        kappa = kd * e_neg
        kappa_rem = kd * e_rem
        beta_rem = bb * e_rem
        g_all = jnp.exp(tot)
        pair_before = same_head & ((jj < ii) if d == 0 else (jj > ii))
        pair_upto = pair_before | (ri == ci)
        for hp in range(pairs_per_dir):
            sl = slice(hp * p, (hp + 1) * p)
            a_st.append(stack(alpha[:, sl]))
            r_st.append(stack(rho[:, sl]))
            k_st.append(stack(kappa[:, sl]))
            b_st.append(stack(beta[:, sl]))
            k2_st.append(stack(kappa_rem[:, sl]))
            b2_st.append(stack(beta_rem[:, sl]))
            v_st.append(stack(v[:, sl]))
            g_chunk.append(g_all[:, sl])
            earlier.append(pair_before)
            upto_self.append(pair_upto)

    pairs = range(2 * pairs_per_dir)
    g = [_bdot_nt(jnp.concatenate([a_st[h], r_st[h]], axis=0), jnp.concatenate([k_st[h], b_st[h]], axis=0))
         for h in pairs]
    m1 = [jnp.where(earlier[h], g[h][:p, :p], 0.0) for h in pairs]
    m2 = [jnp.where(earlier[h], g[h][:p, p:], 0.0) for h in pairs]
    n1 = [jnp.where(upto_self[h], g[h][p:, :p], 0.0) for h in pairs]
    n2 = [jnp.where(upto_self[h], g[h][p:, p:], 0.0) for h in pairs]

    in_block = (ii >> 3) == (jj >> 3)
    pw = [-jnp.where(in_block, m2[h], 0.0) for h in pairs]
    inv = [eye + pw[h] for h in pairs]
    for _ in range(2):
        pw = [_bdot(pw[h], pw[h]) for h in pairs]
        inv = [inv[h] + _bdot(inv[h], pw[h]) for h in pairs]
    for sh in (3, 4, 5):
        off = ((ii >> (sh + 1)) == (jj >> (sh + 1))) & ((ii >> sh) != (jj >> sh))
        left = [_bdot(inv[h], jnp.where(off, m2[h], 0.0)) for h in pairs]
        inv = [inv[h] - _bdot(left[h], inv[h]) for h in pairs]

    m1v = [_bdot(m1[h], v_st[h]) for h in pairs]
    n1v = [_bdot(n1[h], v_st[h]) for h in pairs]
    au = [_bdot(inv[h], jnp.concatenate([a_st[h], m1v[h]], axis=1)) for h in pairs]
    nn = [_bdot(n2[h], au[h]) for h in pairs]
    pc = [_bdot_tn(b2_st[h], au[h][:, :p]) for h in pairs]
    qc_t = [_bdot_tn(jnp.concatenate([v_st[h], -au[h][:, p:]], axis=0),
                     jnp.concatenate([k2_st[h], b2_st[h]], axis=0)) for h in pairs]
    s0 = [state_ref[h] for h in pairs]
    y = [_bdot_nt(unstack(r_st[h] - nn[h][:, :p]), s0[h]) + unstack(n1v[h] - nn[h][:, p:]) for h in pairs]
    s_dec = [_bdot_nt(s0[h], pc[h]) for h in pairs]
    for h in pairs:
        d, hp = divmod(h, pairs_per_dir)
        bi, hpb = divmod(hp, pairs_per_batch)
        y_refs[d][bi, :, hpb * p:(hpb + 1) * p] = y[h]
        state_ref[h] = s0[h] * g_chunk[h] - s_dec[h] + qc_t[h]


def _wkv7(lat, ctx, b, n_tok, n_ctx):
    c = WKV_CHUNK
    ncx = n_ctx // c
    nl = n_tok // c
    lat_idx = (lambda n: jnp.maximum(n - ncx, 0), lambda n: nl - 1 - jnp.maximum(n - ncx, 0))
    ctx_idx = (lambda n: jnp.minimum(n, ncx - 1), lambda n: ncx - 1 - jnp.minimum(n, ncx - 1))

    def specs(idx, d):
        shared = pl.BlockSpec((b, c, D_RWKV), lambda n: (0, idx(n), 0))
        per_dir = pl.BlockSpec((b, c, D_RWKV), lambda n: (0, idx(n), d))
        return [shared, shared, shared, per_dir, per_dir, per_dir]

    in_specs, args = [], []
    for d in range(2):
        in_specs += specs(lat_idx[d], d) + specs(ctx_idx[d], d)
        args += list(lat) + list(ctx)
    return pl.pallas_call(
        functools.partial(_wkv7_chunk_kernel, n_ctx_chunks=ncx),
        grid=(ncx + nl,),
        in_specs=in_specs,
        out_specs=[pl.BlockSpec((b, c, D_RWKV), lambda n, d=d: (0, lat_idx[d](n), 0)) for d in range(2)],
        out_shape=[jax.ShapeDtypeStruct((b, n_tok, D_RWKV), F32)] * 2,
        scratch_shapes=[pltpu.VMEM((2 * b * RWKV_HEADS // 2, 2 * RWKV_HEAD, 2 * RWKV_HEAD), F32)],
        compiler_params=_cparams(("arbitrary",)),
        name="wkv7_chunk",
    )(*args)


def _rope(z, cos_t, sin_t):
    lane = lax.broadcasted_iota(jnp.int32, z.shape, 1)
    half = RET_HEAD // 4
    partner = jnp.where((lane & (2 * half - 1)) < half, pltpu.roll(z, RET_HEAD - half, 1), pltpu.roll(z, half, 1))
    return z * cos_t + partner * sin_t


def _retention_kernel(dec_ref, fwd_ref, bwd_ref, ctx_ref, cosf_ref, sinf_ref, cosb_ref, sinb_ref,
                      yf_ref, yb_ref, state_ref, dmat_ref, tail_ref, head_ref, cdec_ref):
    c = RET_CHUNK
    scale = RET_HEAD ** -0.5
    ii = lax.broadcasted_iota(jnp.int32, (c, c), 0)
    jj = lax.broadcasted_iota(jnp.int32, (c, c), 1)
    pos = lax.broadcasted_iota(jnp.int32, (c, RET_HEAD), 0).astype(F32)
    n_ctx_chunks = ctx_ref.shape[1] // c

    def head_slices(ref_val, h):
        q = ref_val[:, h * RET_HEAD:(h + 1) * RET_HEAD]
        k = ref_val[:, D_RET + h * RET_HEAD:D_RET + (h + 1) * RET_HEAD]
        v = ref_val[:, 2 * D_RET + h * RET_HEAD:2 * D_RET + (h + 1) * RET_HEAD]
        return q, k, v

    chains = [(d, h) for d in range(2) for h in range(RET_HEADS)]

    @pl.when(pl.program_id(1) == 0)
    def _():
        for d, h in chains:
            x = jnp.full((1, RET_HEAD), dec_ref[d, h], F32)
            lg = -(jnp.maximum(x, 0.0) + jnp.log(1.0 + jnp.exp(-jnp.abs(x))))
            chunk_decay = jnp.exp(lg * float(c))
            tail = jnp.exp(lg * ((c - 1.0 - pos) if d == 0 else pos))
            rel = (ii - jj) if d == 0 else (jj - ii)
            mask = (rel >= 0) if d == 0 else (rel > 0)
            dmat_ref[d, h] = jnp.where(mask, jnp.exp(lg * jnp.maximum(rel, 0).astype(F32)), 0.0)
            tail_ref[d, h] = tail
            head_ref[d, h] = jnp.exp(lg * ((pos + 1.0) if d == 0 else (c - pos)))
            cdec_ref[d, h] = jnp.broadcast_to(chunk_decay, (SUBLANES, RET_HEAD))
            s = jnp.zeros((RET_HEAD, RET_HEAD), F32)
            order = range(n_ctx_chunks) if d == 0 else range(n_ctx_chunks - 1, -1, -1)
            for cc in order:
                _, kc, vc = head_slices(ctx_ref[0, cc * c:(cc + 1) * c, :], h)
                s = s * chunk_decay + _bdot_tn(kc * scale * tail, vc)
            state_ref[d, h] = s

    qkv = []
    for d, h in chains:
        blk = fwd_ref[0] if d == 0 else bwd_ref[0]
        cos_t = cosf_ref[...] if d == 0 else cosb_ref[...]
        sin_t = sinf_ref[...] if d == 0 else sinb_ref[...]
        q, k, v = head_slices(blk, h)
        qkv.append((_rope(q, cos_t, sin_t), _rope(k, cos_t, sin_t) * scale, v.astype(BF16)))
    s0 = [state_ref[d, h] for d, h in chains]
    scores = [_bdot_nt(q, k) for q, k, _ in qkv]
    inner = [_bdot(scores[i] * dmat_ref[d, h], qkv[i][2]) for i, (d, h) in enumerate(chains)]
    cross = [_bdot(qkv[i][0] * head_ref[d, h], s0[i]) for i, (d, h) in enumerate(chains)]
    upd = [_bdot_tn(qkv[i][1] * tail_ref[d, h], qkv[i][2]) for i, (d, h) in enumerate(chains)]
    for i, (d, h) in enumerate(chains):
        state_ref[d, h] = s0[i] * cdec_ref[d, h, 0:1, :] + upd[i]
        out_ref = yf_ref if d == 0 else yb_ref
        out_ref[0, :, h * RET_HEAD:(h + 1) * RET_HEAD] = inner[i] + cross[i]


def _retention(pt, pt_ctx, ret_decay, cos_t, sin_t):
    b, n, _ = pt.shape
    c = RET_CHUNK
    nc = n // c
    qkv = 3 * D_RET
    fwd = lambda bi, i: (bi, i, 0)
    bwd = lambda bi, i: (bi, nc - 1 - i, 0)
    return pl.pallas_call(
        _retention_kernel,
        grid=(b, nc),
        in_specs=[pl.BlockSpec(memory_space=pltpu.SMEM),
                  pl.BlockSpec((1, c, qkv), fwd),
                  pl.BlockSpec((1, c, qkv), bwd),
                  pl.BlockSpec((1, pt_ctx.shape[1], qkv), lambda bi, i: (bi, 0, 0)),
                  pl.BlockSpec((c, RET_HEAD), lambda bi, i: (i, 0)),
                  pl.BlockSpec((c, RET_HEAD), lambda bi, i: (i, 0)),
                  pl.BlockSpec((c, RET_HEAD), lambda bi, i: (nc - 1 - i, 0)),
                  pl.BlockSpec((c, RET_HEAD), lambda bi, i: (nc - 1 - i, 0))],
        out_specs=[pl.BlockSpec((1, c, D_RET), fwd), pl.BlockSpec((1, c, D_RET), bwd)],
        out_shape=[jax.ShapeDtypeStruct((b, n, D_RET), F32), jax.ShapeDtypeStruct((b, n, D_RET), F32)],
        scratch_shapes=[pltpu.VMEM((2, RET_HEADS, RET_HEAD, RET_HEAD), F32),
                        pltpu.VMEM((2, RET_HEADS, c, c), F32),
                        pltpu.VMEM((2, RET_HEADS, c, RET_HEAD), F32),
                        pltpu.VMEM((2, RET_HEADS, c, RET_HEAD), F32),
                        pltpu.VMEM((2, RET_HEADS, SUBLANES, RET_HEAD), F32)],
        compiler_params=_cparams(("arbitrary", "arbitrary")),
        name="retention",
    )(ret_decay, pt, pt, pt_ctx, cos_t, sin_t, cos_t, sin_t)


def _rope_tables(n_tok):
    nf = RET_HEAD // 4
    lane = np.arange(RET_HEAD)
    inv = ROPE_BASE ** (-jnp.arange(nf, dtype=F32) / nf)
    t = jnp.arange(n_tok)
    pos = jnp.where((lane // (2 * nf) == 0)[None, :], (t // GRID_W)[:, None], (t % GRID_W)[:, None]).astype(F32)
    ang = pos * inv[lane % nf][None, :]
    sign = jnp.where((lane % (2 * nf)) < nf, -1.0, 1.0).astype(F32)
    return jnp.cos(ang), jnp.sin(ang) * sign[None, :]


def _group_norm(y, ones, seg, eps, g, b):
    mu = _segsum(y, ones) * (1.0 / seg)
    yc = y - mu
    var = _segsum(yc * yc, ones) * (1.0 / seg)
    return yc * lax.rsqrt(var + eps) * g + b


def _out_proj_kernel(x_ref, yf_ref, yb_ref, bonus_ref, gate_ref, tf_ref, tb_ref, gt_ref,
                     embg_ref, embb_ref, g1_ref, s2_ref, sh2_ref, rgn_g_ref, rgn_b_ref, tgn_g_ref, tgn_b_ref,
                     ones_r_ref, wout_ref, ln1g_ref, ln1b_ref, wrh_ref, wrl_ref, br_ref,
                     h1_ref, u2_ref, logit_ref):
    y = yf_ref[0] + yb_ref[0]
    o_rwkv = _group_norm(y, ones_r_ref[...], RWKV_HEAD, RWKV_GN_EPS, rgn_g_ref[...], rgn_b_ref[...])
    o_rwkv = (o_rwkv + bonus_ref[0]) * gate_ref[0]
    yt = tf_ref[0] + tb_ref[0]
    gt = gt_ref[0]
    tgn_g = tgn_g_ref[...]
    tgn_b = tgn_b_ref[...]
    o_ret = jnp.concatenate(
        [_layer_norm(yt[:, h * RET_HEAD:(h + 1) * RET_HEAD], tgn_g[:, h * RET_HEAD:(h + 1) * RET_HEAD],
                     tgn_b[:, h * RET_HEAD:(h + 1) * RET_HEAD], RET_GN_EPS) for h in range(RET_HEADS)], axis=-1)
    o_ret = o_ret * (gt * _sigmoid(gt))
    cat = jnp.concatenate([o_rwkv, o_ret], axis=-1).astype(BF16)
    mix = jnp.dot(cat, wout_ref[...], preferred_element_type=F32)
    h = _layer_norm(x_ref[0], embg_ref[...], embb_ref[...])
    h1 = _layer_norm(DEEPNORM_ALPHA * h + g1_ref[0] * mix, ln1g_ref[...], ln1b_ref[...])
    u2 = h1 * (1.0 + s2_ref[0]) + sh2_ref[0]
    h1_ref[0] = h1
    _store_token_tiles(u2_ref, u2)
    logit_ref[0] = _dot_split(u2, wrh_ref[...], wrl_ref[...]) + br_ref[...]


def _out_proj(x, y_f, y_b, bonus, gate, t_f, t_b, pt, vecs, mats):
    b, n, d = x.shape
    t = 256
    tok = lambda width: pl.BlockSpec((1, t, width), lambda bi, i: (bi, i, 0))
    per_b = pl.BlockSpec((1, 1, d), lambda bi, i: (bi, 0, 0))
    small = lambda arr: pl.BlockSpec(arr.shape, lambda bi, i: (0,) * arr.ndim)
    (embg, embb, g1, s2, sh2, rgn_g, rgn_b, tgn_g, tgn_b, ln1g, ln1b, br) = vecs
    (ones_r, wout, wr_hi, wr_lo) = mats
    gt_spec = pl.BlockSpec((1, t, D_RET), lambda bi, i: (bi, i, 3))
    args = (x, y_f, y_b, bonus, gate, t_f, t_b, pt, embg, embb, g1, s2, sh2, rgn_g, rgn_b, tgn_g, tgn_b,
            ones_r, wout, ln1g, ln1b, wr_hi, wr_lo, br)
    in_specs = [tok(d)] + [tok(D_RWKV)] * 6 + [gt_spec, small(embg), small(embb), per_b, per_b, per_b,
                                                small(rgn_g), small(rgn_b), small(tgn_g), small(tgn_b),
                                                small(ones_r), small(wout), small(ln1g),
                                                small(ln1b), small(wr_hi), small(wr_lo), small(br)]
    return pl.pallas_call(
        _out_proj_kernel,
        grid=(b, n // t),
        in_specs=in_specs,
        out_specs=[tok(d), pl.BlockSpec((t * SUBLANES, LANES), lambda bi, i: (bi * (n // t) + i, 0)), tok(LANES)],
        out_shape=[jax.ShapeDtypeStruct((b, n, d), F32), jax.ShapeDtypeStruct((b * n * SUBLANES, LANES), F32),
                   jax.ShapeDtypeStruct((b, n, LANES), F32)],
        compiler_params=_cparams(("arbitrary", "arbitrary")),
        name="out_proj",
    )(*args)


ROUTE_E1, ROUTE_E2, ROUTE_G1, ROUTE_G2, ROUTE_RANK1, ROUTE_RANK2 = range(6)


def _lane_argmax(x, valid, lane):
    m = jnp.max(jnp.where(valid, x, -jnp.inf), axis=-1, keepdims=True)
    idx = jnp.min(jnp.where(valid & (x == m), lane, float(LANES)), axis=-1, keepdims=True)
    return m, idx


def _route_kernel(logit_ref, route_ref, route_t_ref, count_ref, carry_ref):
    @pl.when(pl.program_id(0) == 0)
    def _():
        carry_ref[...] = jnp.zeros_like(carry_ref)

    lg = logit_ref[...]
    t = lg.shape[0]
    lane = lax.broadcasted_iota(jnp.int32, lg.shape, 1).astype(F32)
    gmask = lane < N_GROUPS
    gmax = jnp.max(jnp.where(gmask, lg, -jnp.inf), axis=-1, keepdims=True)
    gexp = jnp.where(gmask, jnp.exp(lg - gmax), 0.0)
    gp = gexp / jnp.sum(gexp, axis=-1, keepdims=True)
    g_w, g_i = _lane_argmax(gp, gmask, lane)

    lo = N_GROUPS + EXPERTS_PER_GROUP * g_i
    emask = (lane >= lo) & (lane < lo + EXPERTS_PER_GROUP)
    emax = jnp.max(jnp.where(emask, lg, -jnp.inf), axis=-1, keepdims=True)
    eexp = jnp.where(emask, jnp.exp(lg - emax), 0.0)
    ep = eexp / jnp.sum(eexp, axis=-1, keepdims=True)
    p1, i1 = _lane_argmax(ep, emask, lane)
    p2, i2 = _lane_argmax(ep, emask & (lane != i1), lane)
    denom = p1 + p2
    gate1 = g_w * p1 / denom
    gate2 = g_w * p2 / denom
    e1 = i1 - N_GROUPS
    e2 = i2 - N_GROUPS

    oh1 = (lane == e1).astype(F32)
    oh2 = (lane == e2).astype(F32)
    cnt = oh1 + oh2
    ri = lax.broadcasted_iota(jnp.int32, (t, t), 0)
    ci = lax.broadcasted_iota(jnp.int32, (t, t), 1)
    before = (ci < ri).astype(BF16)
    seen = jnp.dot(before, cnt.astype(BF16), preferred_element_type=F32) + carry_ref[0:1, :]
    rank1 = jnp.sum(oh1 * seen, axis=-1, keepdims=True)
    rank2 = jnp.sum(oh2 * seen, axis=-1, keepdims=True)
    carry_ref[0:1, :] = carry_ref[0:1, :] + jnp.sum(cnt, axis=0, keepdims=True)

    out = jnp.zeros(lg.shape, F32)
    for slot, val in ((ROUTE_E1, e1.astype(F32)), (ROUTE_E2, e2.astype(F32)), (ROUTE_G1, gate1),
                      (ROUTE_G2, gate2), (ROUTE_RANK1, rank1), (ROUTE_RANK2, rank2)):
        out = jnp.where(lane == slot, val, out)
    route_ref[...] = out
    route_t_ref[...] = out.T[:SUBLANES]
    count_ref[...] = carry_ref[...]


def _route(logits):
    n = logits.shape[0]
    t = 256
    return pl.pallas_call(
        _route_kernel,
        grid=(n // t,),
        in_specs=[pl.BlockSpec((t, LANES), lambda i: (i, 0))],
        out_specs=[pl.BlockSpec((t, LANES), lambda i: (i, 0)), pl.BlockSpec((SUBLANES, t), lambda i: (0, i)),
                   pl.BlockSpec((SUBLANES, LANES), lambda i: (0, 0))],
        out_shape=[jax.ShapeDtypeStruct((n, LANES), F32), jax.ShapeDtypeStruct((SUBLANES, n), F32),
                   jax.ShapeDtypeStruct((SUBLANES, LANES), F32)],
        scratch_shapes=[pltpu.VMEM((SUBLANES, LANES), F32)],
        compiler_params=_cparams(("arbitrary",)),
        name="route",
    )(logits)


def _tile_gather_copy(src_hbm, idx_ref, buf, sem, slot, r):
    dst = buf.at[slot, pl.ds(pl.multiple_of(r * SUBLANES, SUBLANES), SUBLANES), :]
    return pltpu.make_async_copy(src_hbm.at[idx_ref[0, 0, r]], dst, sem.at[slot])


def _start_tile_gather(src_hbm, idx_ref, buf, sem, slot, rows, priorities):
    def body(g, carry):
        for j in range(GATHER_UNROLL):
            copy = _tile_gather_copy(src_hbm, idx_ref, buf, sem, slot, g * GATHER_UNROLL + j)
            copy.start(priority=priorities[j % len(priorities)])
        return carry
    lax.fori_loop(0, rows // GATHER_UNROLL, body, 0)


def _wait_tile_gather(src_hbm, idx_ref, buf, sem, slot, rows):
    def body(r, carry):
        _tile_gather_copy(src_hbm, idx_ref, buf, sem, slot, r).wait()
        return carry
    lax.fori_loop(0, rows, body, 0, unroll=GATHER_UNROLL)


def _sc_gather_rows(table, idx):
    m = idx.shape[0]
    info = plsc.get_sparse_core_info()
    n_workers = info.num_cores * info.num_subcores
    chunk = LANES
    per_worker = m // n_workers
    n_chunks = per_worker // chunk
    assert m == n_workers * n_chunks * chunk and n_chunks % SC_GATHER_GROUP == 0
    mesh = plsc.VectorSubcoreMesh(core_axis_name="c", subcore_axis_name="s")

    @functools.partial(
        pl.kernel, mesh=mesh,
        out_type=jax.ShapeDtypeStruct((m, LANES), F32),
        scratch_types=[pltpu.VMEM((n_chunks, chunk), jnp.int32),
                       pltpu.VMEM((SC_GATHER_GROUP * chunk, LANES), F32),
                       pltpu.SemaphoreType.DMA],
        name="sc_gather_rows",
    )
    def gather(table_hbm, idx_hbm, out_hbm, idx_v, rows_v, sem):
        worker = lax.axis_index("s") * info.num_cores + lax.axis_index("c")
        pltpu.sync_copy(idx_hbm.at[worker], idx_v)

        @pl.loop(0, n_chunks, step=SC_GATHER_GROUP)
        def _(c0):
            copies = [pltpu.async_copy(table_hbm.at[idx_v.at[c0 + j]], rows_v.at[pl.ds(j * chunk, chunk)], sem)
                      for j in range(SC_GATHER_GROUP)]
            for copy in copies:
                copy.wait()
            base = pl.multiple_of(worker * per_worker + c0 * chunk, chunk)
            pltpu.sync_copy(rows_v, out_hbm.at[pl.ds(base, SC_GATHER_GROUP * chunk)])

    return gather(table, idx.reshape(n_workers, n_chunks, chunk))


def _expert_kernel(blk_expert_ref, n_used_ref, x_ref, wg_ref, wu_ref, wd_ref, y_ref, wg_s, wu_s, wd_s):
    i = pl.program_id(0)
    n_used = n_used_ref[0]

    @pl.when(i >= n_used)
    def _():
        y_ref[...] = jnp.zeros_like(y_ref)

    @pl.when(i < n_used)
    def _():
        @pl.when((i == 0) | (blk_expert_ref[i] != blk_expert_ref[jnp.maximum(i - 1, 0)]))
        def _():
            wg_s[...] = wg_ref[0].astype(BF16)
            wu_s[...] = wu_ref[0].astype(BF16)
            wd_s[...] = wd_ref[0].astype(BF16)

        x = _load_token_tiles(x_ref, EXPERT_ROWS).astype(BF16)
        hg = jnp.dot(x, wg_s[...], preferred_element_type=F32)
        hu = jnp.dot(x, wu_s[...], preferred_element_type=F32)
        act = (hg * _sigmoid(hg) * hu).astype(BF16)
        _store_token_tiles(y_ref, jnp.dot(act, wd_s[...], preferred_element_type=F32))


def _expert_mlp(x_tiles, block_expert, n_used, w_gate, w_up, w_down):
    n_blk = block_expert.shape[0]
    d = w_gate.shape[1]
    hdim = w_gate.shape[2]
    rows = EXPERT_ROWS
    used = lambda i, nu: jnp.minimum(i, nu[0] - 1)
    weight = lambda i, be, nu: (be[used(i, nu)], 0, 0)
    grid_spec = pltpu.PrefetchScalarGridSpec(
        num_scalar_prefetch=2,
        grid=(n_blk,),
        in_specs=[pl.BlockSpec((rows * SUBLANES, LANES), lambda i, be, nu: (used(i, nu), 0)),
                  pl.BlockSpec((1, d, hdim), weight),
                  pl.BlockSpec((1, d, hdim), weight),
                  pl.BlockSpec((1, hdim, d), weight)],
        out_specs=pl.BlockSpec((rows * SUBLANES, LANES), lambda i, be, nu: (i, 0)),
        scratch_shapes=[pltpu.VMEM((d, hdim), BF16), pltpu.VMEM((d, hdim), BF16), pltpu.VMEM((hdim, d), BF16)],
    )
    return pl.pallas_call(
        _expert_kernel,
        grid_spec=grid_spec,
        out_shape=jax.ShapeDtypeStruct((n_blk * rows * SUBLANES, LANES), F32),
        compiler_params=_cparams(("arbitrary",)),
        name="expert_mlp",
    )(block_expert, n_used, x_tiles, w_gate, w_up, w_down)


def _combine_kernel(d1_ref, d2_ref, d1n_ref, d2n_ref, y_hbm, route_ref, h1_ref, g2_ref, lng_ref, lnb_ref,
                    o_ref, abuf, bbuf, sem_a, sem_b):
    i = pl.program_id(0)
    n = pl.num_programs(0)
    slot = i % 2
    rows = o_ref.shape[0]

    @pl.when(i == 0)
    def _():
        _start_tile_gather(y_hbm, d1_ref, abuf, sem_a, 0, rows, GATHER_PRIORITIES)
        _start_tile_gather(y_hbm, d2_ref, bbuf, sem_b, 0, rows, GATHER_PRIORITIES)

    @pl.when(i + 1 < n)
    def _():
        _start_tile_gather(y_hbm, d1n_ref, abuf, sem_a, 1 - slot, rows, GATHER_PRIORITIES)
        _start_tile_gather(y_hbm, d2n_ref, bbuf, sem_b, 1 - slot, rows, GATHER_PRIORITIES)

    _wait_tile_gather(y_hbm, d1_ref, abuf, sem_a, slot, rows)
    _wait_tile_gather(y_hbm, d2_ref, bbuf, sem_b, slot, rows)
    route = route_ref[...]
    f = (_load_token_tiles(abuf.at[slot], rows) * route[:, ROUTE_G1:ROUTE_G1 + 1]
         + _load_token_tiles(bbuf.at[slot], rows) * route[:, ROUTE_G2:ROUTE_G2 + 1])
    o_ref[...] = _layer_norm(DEEPNORM_ALPHA * h1_ref[...] + g2_ref[0] * f, lng_ref[...], lnb_ref[...])


def _combine(y_tiles, dest1, dest2, route, h1, g2, ln_g, ln_b, tokens_per_batch):
    n, d = h1.shape
    t = COMBINE_ROWS
    nt = n // t
    per_b = tokens_per_batch // t
    d1 = dest1.reshape(nt, 1, t)
    d2 = dest2.reshape(nt, 1, t)
    cur = pl.BlockSpec((1, 1, t), lambda i: (i, 0, 0), memory_space=pltpu.SMEM)
    nxt = pl.BlockSpec((1, 1, t), lambda i: (jnp.minimum(i + 1, nt - 1), 0, 0), memory_space=pltpu.SMEM)
    small = lambda arr: pl.BlockSpec(arr.shape, lambda i: (0,) * arr.ndim)
    return pl.pallas_call(
        _combine_kernel,
        grid=(nt,),
        in_specs=[cur, cur, nxt, nxt, pl.BlockSpec(memory_space=pl.ANY),
                  pl.BlockSpec((t, LANES), lambda i: (i, 0)),
                  pl.BlockSpec((t, d), lambda i: (i, 0)),
                  pl.BlockSpec((1, 1, d), lambda i: (i // per_b, 0, 0)),
                  small(ln_g), small(ln_b)],
        out_specs=pl.BlockSpec((t, d), lambda i: (i, 0)),
        out_shape=jax.ShapeDtypeStruct((n, d), F32),
        scratch_shapes=[pltpu.VMEM((2, t * SUBLANES, LANES), F32), pltpu.VMEM((2, t * SUBLANES, LANES), F32),
                        pltpu.SemaphoreType.DMA((2,)), pltpu.SemaphoreType.DMA((2,))],
        compiler_params=_cparams(("arbitrary",)),
        name="combine",
    )(d1, d2, d1, d2, y_tiles.reshape(-1, SUBLANES, LANES), route, h1, g2, ln_g, ln_b)


def _hi_lo(w):
    hi = w.astype(BF16)
    return jnp.stack([hi, (w - hi.astype(F32)).astype(BF16)])


def _block_diag2(w):
    z = jnp.zeros_like(w[0])
    return jnp.concatenate([jnp.concatenate([w[0], z], axis=1), jnp.concatenate([z, w[1]], axis=1)], axis=0)


def kernel(x, c, ctx, c_ctx, emb_ln_g, emb_ln_b, w_mod, b_mod, w_in, tshift_mu, rwkv_w0, rwkv_w2, rwkv_a0, rwkv_a2, rwkv_g2, rwkv_k_k, rwkv_k_a, rwkv_r_k, rwkv_gn_g, rwkv_gn_b, ret_decay, ret_gn_g, ret_gn_b, w_out, ln1_g, ln1_b, router_group, router_group_bias, router_expert, router_expert_bias, expert_w_gate, expert_w_up, expert_w_down, ln2_g, ln2_b):
    assert w_mod.shape[0] == 1, "written for DEPTH == 1 (context outputs are never emitted)"
    b, n_tok, d = x.shape
    n_ctx = ctx.shape[1]
    row = lambda v: v.reshape(1, -1)

    c_rows = jnp.zeros((SUBLANES, d), F32).at[:b].set(c).at[b].set(c_ctx)
    mod = _modulation(c_rows, w_mod[0], row(b_mod[0]))
    sh1, s1, g1, sh2, s2, g2 = [mod[:b, j * d:(j + 1) * d].reshape(b, 1, d) for j in range(6)]
    sh1c, s1c = [jnp.broadcast_to(mod[b, j * d:(j + 1) * d].reshape(1, 1, d), (b, 1, d)) for j in range(2)]

    w_in_bf16 = w_in[0].astype(BF16)
    pr, pt = _in_proj(x, row(emb_ln_g), row(emb_ln_b), s1, sh1, w_in_bf16)
    pr_c, pt_c = _in_proj(ctx, row(emb_ln_g), row(emb_ln_b), s1c, sh1c, w_in_bf16)

    prep_params = (row(tshift_mu[0]), row(rwkv_w0[0]), _hi_lo(_block_diag2(rwkv_w2[0])), row(rwkv_a0[0]),
                   _hi_lo(_block_diag2(rwkv_a2[0])), _hi_lo(rwkv_g2[0]), row(rwkv_k_k[0]), row(rwkv_k_a[0]),
                   row(rwkv_r_k[0]),
                   _segment_ones(D_RWKV, RWKV_HEAD))
    lat = _rwkv_prepare(pr, prep_params, grid_shift=True)
    cx = _rwkv_prepare(pr_c, prep_params, grid_shift=False)
    r_l, v_l, kk_l, w_l, kd_l, bb_l, gate_l, bonus_l = lat
    r_c, v_c, kk_c, w_c, kd_c, bb_c, _, _ = cx

    y_f, y_b = _wkv7((r_l, v_l, kk_l, w_l, kd_l, bb_l), (r_c, v_c, kk_c, w_c, kd_c, bb_c), b, n_tok, n_ctx)

    cos_t, sin_t = _rope_tables(n_tok)
    t_f, t_b = _retention(pt, pt_c, ret_decay[0], cos_t, sin_t)

    wr = jnp.zeros((d, LANES), F32).at[:, :N_GROUPS].set(router_group[0])
    wr = wr.at[:, N_GROUPS:N_GROUPS + N_EXPERTS].set(router_expert[0])
    br = jnp.zeros((1, LANES), F32).at[0, :N_GROUPS].set(router_group_bias[0])
    br = br.at[0, N_GROUPS:N_GROUPS + N_EXPERTS].set(router_expert_bias[0].reshape(-1))
    vecs = (row(emb_ln_g), row(emb_ln_b), g1, s2, sh2, row(rwkv_gn_g[0]), row(rwkv_gn_b[0]),
            row(ret_gn_g[0]), row(ret_gn_b[0]), row(ln1_g[0]), row(ln1_b[0]), br)
    wr_hi_lo = _hi_lo(wr)
    mats = (_segment_ones(D_RWKV, RWKV_HEAD), w_out[0].astype(BF16), wr_hi_lo[0], wr_hi_lo[1])
    h1, u2, logits = _out_proj(x, y_f, y_b, bonus_l, gate_l, t_f, t_b, pt, vecs, mats)

    n_all = b * n_tok
    route, route_t, counts = _route(logits.reshape(n_all, LANES))

    e1 = route_t[ROUTE_E1].astype(jnp.int32)
    e2 = route_t[ROUTE_E2].astype(jnp.int32)
    cnt = counts[0, :N_EXPERTS].astype(jnp.int32)
    padded = ((cnt + EXPERT_ROWS - 1) // EXPERT_ROWS) * EXPERT_ROWS
    pends = jnp.cumsum(padded)
    pstarts = pends - padded
    expert_ids = jnp.arange(N_EXPERTS, dtype=jnp.int32)
    start_of = lambda e: jnp.sum(jnp.where(e[:, None] == expert_ids[None, :], pstarts[None, :], 0), axis=1)
    dest1 = start_of(e1) + route_t[ROUTE_RANK1].astype(jnp.int32)
    dest2 = start_of(e2) + route_t[ROUTE_RANK2].astype(jnp.int32)
    n_blk = -(-(n_all * 2) // EXPERT_ROWS) + N_EXPERTS
    tok_ids = jnp.arange(n_all, dtype=jnp.int32)
    slot_tok = jnp.zeros((n_blk * EXPERT_ROWS,), jnp.int32).at[jnp.concatenate([dest1, dest2])].set(
        jnp.concatenate([tok_ids, tok_ids]))
    block_start = jnp.arange(n_blk, dtype=jnp.int32) * EXPERT_ROWS
    block_expert = jnp.minimum(jnp.sum((block_start[:, None] >= pends[None, :]).astype(jnp.int32), axis=1),
                               N_EXPERTS - 1)

    n_used = (pends[N_EXPERTS - 1:] // EXPERT_ROWS).astype(jnp.int32)
    tile_rows = jnp.arange(SUBLANES, dtype=jnp.int32)
    slot_rows = (slot_tok[:, None] * SUBLANES + tile_rows[None, :]).reshape(-1)
    x_tiles = _sc_gather_rows(u2, slot_rows)
    y_tiles = _expert_mlp(x_tiles, block_expert, n_used, expert_w_gate[0], expert_w_up[0], expert_w_down[0])
    out = _combine(y_tiles, dest1, dest2, route, h1.reshape(n_all, d), g2, row(ln2_g[0]), row(ln2_b[0]), n_tok)
    return out.reshape(b, n_tok, d)
```

```python
import functools
import math

import jax
import jax.numpy as jnp
import numpy as np
from jax import lax
from jax.experimental import pallas as pl
from jax.experimental.pallas import tpu as pltpu
from jax.experimental.pallas import tpu_sc as plsc

F32 = jnp.float32
BF16 = jnp.bfloat16
HIGHEST = lax.Precision.HIGHEST

GRID_W = 64
D_RWKV = 512
RWKV_HEAD = 64
RWKV_HEADS = D_RWKV // RWKV_HEAD
DECAY_LORA = 64
AAA_LORA = 64
GATE_LORA = 128
D_RET = 512
RET_HEADS = 4
RET_HEAD = D_RET // RET_HEADS
RET_CHUNK = 128
RWKV_COLS = 3 * D_RWKV + 2 * (DECAY_LORA + AAA_LORA) + GATE_LORA
RET_COLS = 4 * D_RET
N_GROUPS = 4
EXPERTS_PER_GROUP = 8
N_EXPERTS = N_GROUPS * EXPERTS_PER_GROUP
EXPERT_HIDDEN = 512
MOE_BLOCK = 128
ROPE_BASE = 10000.0
LN_EPS = 1e-5
RWKV_GN_EPS = 64e-5
RET_GN_EPS = 1e-5
DEEPNORM_ALPHA = 2.0 ** 0.25
EXP_NEG_HALF = math.exp(-0.5)

LANES = 128
SUBLANES = 8
VMEM_LIMIT_BYTES = 56 * 1024 * 1024

WKV_CHUNK = 64


EXPERT_ROWS = 256
COMBINE_ROWS = 128
GATHER_UNROLL = 8
SC_GATHER_CHUNK = 32
SC_GATHER_GROUP = 2
GATHER_PRIORITIES = (0, 1)


def _store_token_tiles(ref, x):
    rows = x.shape[0]
    for j in range(x.shape[1] // LANES):
        ref[pl.ds(j, rows, stride=SUBLANES), :] = x[:, j * LANES:(j + 1) * LANES]


def _load_token_tiles(ref, rows):
    return jnp.concatenate([ref[pl.ds(j, rows, stride=SUBLANES), :] for j in range(SUBLANES)], axis=-1)


def _cparams(sem):
    return pltpu.CompilerParams(dimension_semantics=sem, vmem_limit_bytes=VMEM_LIMIT_BYTES)


def _layer_norm(x, g, b, eps=LN_EPS):
    mu = jnp.mean(x, axis=-1, keepdims=True)
    xc = x - mu
    var = jnp.mean(xc * xc, axis=-1, keepdims=True)
    return xc * lax.rsqrt(var + eps) * g + b


def _sigmoid(x):
    return 1.0 / (1.0 + jnp.exp(-x))


def _split_bf16(x):
    hi = x.astype(BF16)
    return hi, (x - hi.astype(F32)).astype(BF16)


def _segsum(x, ones_bf16):
    t = x.shape[0]
    s = jnp.dot(jnp.concatenate(_split_bf16(x), axis=0), ones_bf16, preferred_element_type=F32)
    return s[:t] + s[t:]


def _dot_split(x, w_hi, w_lo):
    hi, lo = _split_bf16(x)
    acc = jnp.dot(hi, w_hi, preferred_element_type=F32)
    acc = acc + jnp.dot(lo, w_hi, preferred_element_type=F32)
    return acc + jnp.dot(hi, w_lo, preferred_element_type=F32)


def _segment_ones(width, seg):
    idx = np.arange(width) // seg
    return jnp.asarray(idx[:, None] == idx[None, :], dtype=BF16)


def _mod_kernel(c_ref, w_ref, b_ref, o_ref):
    c = c_ref[...]
    sc = c * _sigmoid(c)
    o_ref[...] = jnp.dot(sc, w_ref[...], precision=HIGHEST, preferred_element_type=F32) + b_ref[...]


def _modulation(c_rows, w_mod, b_mod):
    rows, d = c_rows.shape
    n = w_mod.shape[1]
    tn = 1536
    return pl.pallas_call(
        _mod_kernel,
        grid=(n // tn,),
        in_specs=[pl.BlockSpec((rows, d), lambda j: (0, 0)),
                  pl.BlockSpec((d, tn), lambda j: (0, j)),
                  pl.BlockSpec((1, tn), lambda j: (0, j))],
        out_specs=pl.BlockSpec((rows, tn), lambda j: (0, j)),
        out_shape=jax.ShapeDtypeStruct((rows, n), F32),
        compiler_params=_cparams(("arbitrary",)),
        name="modulation",
    )(c_rows, w_mod, b_mod)


def _in_proj_kernel(x_ref, g_ref, b_ref, s_ref, sh_ref, w_ref, pr_ref, pt_ref):
    h = _layer_norm(x_ref[0], g_ref[...], b_ref[...])
    u = h * (1.0 + s_ref[0]) + sh_ref[0]
    p = jnp.dot(u.astype(BF16), w_ref[...], preferred_element_type=F32)
    pr_ref[0] = p[:, :RWKV_COLS]
    pt_ref[0] = p[:, RWKV_COLS:]


def _in_proj(x, ln_g, ln_b, s1, sh1, w_in_bf16):
    b, n, d = x.shape
    tm = 256
    cols = w_in_bf16.shape[1]
    return pl.pallas_call(
        _in_proj_kernel,
        grid=(b, n // tm),
        in_specs=[pl.BlockSpec((1, tm, d), lambda bi, i: (bi, i, 0)),
                  pl.BlockSpec((1, d), lambda bi, i: (0, 0)),
                  pl.BlockSpec((1, d), lambda bi, i: (0, 0)),
                  pl.BlockSpec((1, 1, d), lambda bi, i: (bi, 0, 0)),
                  pl.BlockSpec((1, 1, d), lambda bi, i: (bi, 0, 0)),
                  pl.BlockSpec((d, cols), lambda bi, i: (0, 0))],
        out_specs=[pl.BlockSpec((1, tm, RWKV_COLS), lambda bi, i: (bi, i, 0)),
                   pl.BlockSpec((1, tm, RET_COLS), lambda bi, i: (bi, i, 0))],
        out_shape=[jax.ShapeDtypeStruct((b, n, RWKV_COLS), F32),
                   jax.ShapeDtypeStruct((b, n, RET_COLS), F32)],
        compiler_params=_cparams(("arbitrary", "arbitrary")),
        name="in_proj",
    )(x, ln_g, ln_b, s1, sh1, w_in_bf16)


def _rwkv_prepare_kernel(cur_ref, prev_ref, next_ref, mu_ref, w0_ref, w2_ref, a0_ref, a2_ref, g2_ref,
                         kk_scale_ref, ka_ref, rk_ref, ones_ref,
                         r_ref, v_ref, kk_ref, w_ref, kd_ref, bb_ref, g_ref, bonus_ref,
                         *, grid_shift, n_tok):
    cur = cur_ref[0]
    t, c = cur.shape
    row = lax.broadcasted_iota(jnp.int32, (t, c), 0)
    lane = lax.broadcasted_iota(jnp.int32, (t, c), 1)
    prev_tok = pltpu.roll(cur, 1, 0)
    next_tok = pltpu.roll(cur, t - 1, 0)
    if grid_shift:
        col = row & (GRID_W - 1)
        tok = row + pl.program_id(1) * t
        left = jnp.where(col > 0, prev_tok, 0.0)
        right = jnp.where(col < GRID_W - 1, next_tok, 0.0)
        up = jnp.where(tok >= GRID_W, jnp.concatenate([prev_ref[0], cur[:t - GRID_W]], axis=0), 0.0)
        down = jnp.where(tok < n_tok - GRID_W, jnp.concatenate([cur[GRID_W:], next_ref[0]], axis=0), 0.0)
        cm = lane & 3
        shifted = jnp.where(cm == 0, left, jnp.where(cm == 1, right, jnp.where(cm == 2, up, down)))
    else:
        prev_tok = jnp.where(row > 0, prev_tok, 0.0)
        next_tok = jnp.where(row < t - 1, next_tok, 0.0)
        shifted = jnp.where((lane & 1) == 0, prev_tok, next_tok)
    pm = cur + mu_ref[...] * (shifted - cur)

    r = pm[:, 0:D_RWKV]
    k = pm[:, D_RWKV:2 * D_RWKV]
    v = pm[:, 2 * D_RWKV:3 * D_RWKV]
    o = 3 * D_RWKV
    lw = pm[:, o:o + 2 * DECAY_LORA]
    la = pm[:, o + 2 * DECAY_LORA:o + 2 * (DECAY_LORA + AAA_LORA)]
    lg = pm[:, o + 2 * (DECAY_LORA + AAA_LORA):]

    w = w0_ref[...] + _dot_split(jnp.tanh(lw), w2_ref[0], w2_ref[1])
    log_decay = -EXP_NEG_HALF * _sigmoid(w)
    a = _sigmoid(a0_ref[...] + _dot_split(la, a2_ref[0], a2_ref[1]))
    gate = _dot_split(_sigmoid(lg), g2_ref[0], g2_ref[1])

    ones = ones_ref[...]
    kk_raw = k * kk_scale_ref[...]
    kk = kk_raw / jnp.maximum(jnp.sqrt(_segsum(kk_raw * kk_raw, ones)), 1e-12)
    ka = ka_ref[...]
    a0 = a[:, :D_RWKV]
    a1 = a[:, D_RWKV:]
    kd0 = k * (1.0 + (a0 - 1.0) * ka)
    kd1 = k * (1.0 + (a1 - 1.0) * ka)
    bonus = _segsum(r * (kd0 + kd1) * rk_ref[...], ones) * v

    r_ref[0] = r
    v_ref[0] = v
    kk_ref[0] = kk
    w_ref[0] = log_decay
    kd_ref[0] = jnp.concatenate([kd0, kd1], axis=-1)
    bb_ref[0] = jnp.concatenate([kk * a0, kk * a1], axis=-1)
    g_ref[0] = gate
    bonus_ref[0] = bonus


def _rwkv_prepare(pr, params, grid_shift):
    b, n, c = pr.shape
    t = 256
    if not grid_shift:
        assert n == t, "sequence token shift is written for a single tile"
    halo_blocks = n // GRID_W
    per_tile = t // GRID_W
    small = lambda shape: pl.BlockSpec(shape, lambda bi, i: (0,) * len(shape))
    tok_spec = lambda width: pl.BlockSpec((1, t, width), lambda bi, i: (bi, i, 0))
    out_widths = (D_RWKV, D_RWKV, D_RWKV, 2 * D_RWKV, 2 * D_RWKV, 2 * D_RWKV, D_RWKV, D_RWKV)
    kernel = functools.partial(_rwkv_prepare_kernel, grid_shift=grid_shift, n_tok=n)
    return pl.pallas_call(
        kernel,
        grid=(b, n // t),
        in_specs=[tok_spec(c),
                  pl.BlockSpec((1, GRID_W, c), lambda bi, i: (bi, jnp.maximum(i * per_tile - 1, 0), 0)),
                  pl.BlockSpec((1, GRID_W, c),
                               lambda bi, i: (bi, jnp.minimum((i + 1) * per_tile, halo_blocks - 1), 0)),
                  small((1, c)), small((1, 2 * D_RWKV)), small((2, 2 * DECAY_LORA, 2 * D_RWKV)),
                  small((1, 2 * D_RWKV)), small((2, 2 * AAA_LORA, 2 * D_RWKV)), small((2, GATE_LORA, D_RWKV)),
                  small((1, D_RWKV)), small((1, D_RWKV)), small((1, D_RWKV)), small((D_RWKV, D_RWKV))],
        out_specs=[tok_spec(wd) for wd in out_widths],
        out_shape=[jax.ShapeDtypeStruct((b, n, wd), F32) for wd in out_widths],
        compiler_params=_cparams(("arbitrary", "arbitrary")),
        name="rwkv_prepare",
    )(pr, pr, pr, *params)


def _bdot(a, b):
    return jnp.dot(a.astype(BF16), b.astype(BF16), preferred_element_type=F32)


def _bdot_nt(a, b):
    return lax.dot_general(a.astype(BF16), b.astype(BF16), (((1,), (1,)), ((), ())), preferred_element_type=F32)


def _bdot_tn(a, b):
    return lax.dot_general(a.astype(BF16), b.astype(BF16), (((0,), (0,)), ((), ())), preferred_element_type=F32)


def _wkv7_chunk_kernel(*refs, n_ctx_chunks):
    c = WKV_CHUNK
    p = 2 * c
    n_in = 12
    in_refs = (refs[:n_in], refs[n_in:2 * n_in])
    y_refs = refs[2 * n_in:2 * n_in + 2]
    state_ref = refs[2 * n_in + 2]
    n = pl.program_id(0)
    n_batch = in_refs[0][0].shape[0]
    pairs_per_batch = RWKV_HEADS // 2
    pairs_per_dir = n_batch * pairs_per_batch

    @pl.when(n == 0)
    def _():
        state_ref[...] = jnp.zeros_like(state_ref)

    is_ctx = n < n_ctx_chunks
    ti = lax.broadcasted_iota(jnp.int32, (c, c), 0)
    tj = lax.broadcasted_iota(jnp.int32, (c, c), 1)
    ri = lax.broadcasted_iota(jnp.int32, (p, p), 0)
    ci = lax.broadcasted_iota(jnp.int32, (p, p), 1)
    same_head = (ri >= c) == (ci >= c)
    ii = ri & (c - 1)
    jj = ci & (c - 1)
    eye = (ri == ci).astype(F32)
    first = lax.broadcasted_iota(jnp.int32, (c, p), 1) < RWKV_HEAD

    def stack(x):
        return jnp.concatenate([jnp.where(first, x, 0.0), jnp.where(first, 0.0, x)], axis=0)

    def unstack(x):
        return x[:c] + x[c:]

    a_st, r_st, k_st, b_st, k2_st, b2_st, v_st, g_chunk, earlier, upto_self = ([] for _ in range(10))
    for d in range(2):
        r_l, v_l, kk_l, lw_l, kd_l, bb_l, r_c, v_c, kk_c, lw_c, kd_c, bb_c = in_refs[d]
        pick = lambda xc, xl: jnp.concatenate(
            [jnp.where(is_ctx, xc[bi], xl[bi]) for bi in range(n_batch)], axis=-1)
        r, v, kk, lw, kd, bb = (pick(r_c, r_l), pick(v_c, v_l), pick(kk_c, kk_l), pick(lw_c, lw_l),
                                pick(kd_c, kd_l), pick(bb_c, bb_l))
        before = (tj < ti) if d == 0 else (tj > ti)
        upto = (before | (ti == tj)).astype(BF16)
        hi = lw.astype(BF16)
        r1 = lw - hi.astype(F32)
        mid = r1.astype(BF16)
        lo = (r1 - mid.astype(F32)).astype(BF16)
        cum = (jnp.dot(upto, hi, preferred_element_type=F32) + jnp.dot(upto, mid, preferred_element_type=F32)
               + jnp.dot(upto, lo, preferred_element_type=F32))
        tot = jnp.sum(lw, axis=0, keepdims=True)
        e_neg = jnp.exp(-cum)
        e_rem = jnp.exp(tot - cum)
        alpha = kk * jnp.exp(cum - lw)
        rho = r * jnp.exp(cum)
        beta = bb * e_neg
        kappa = kd * e_neg
        kappa_rem = kd * e_rem
        beta_rem = bb * e_rem
        g_all = jnp.exp(tot)
        pair_before = same_head & ((jj < ii) if d == 0 else (jj > ii))
        pair_upto = pair_before | (ri == ci)
        for hp in range(pairs_per_dir):
            sl = slice(hp * p, (hp + 1) * p)
            a_st.append(stack(alpha[:, sl]))
            r_st.append(stack(rho[:, sl]))
            k_st.append(stack(kappa[:, sl]))
            b_st.append(stack(beta[:, sl]))
            k2_st.append(stack(kappa_rem[:, sl]))
            b2_st.append(stack(beta_rem[:, sl]))
            v_st.append(stack(v[:, sl]))
            g_chunk.append(g_all[:, sl])
            earlier.append(pair_before)
            upto_self.append(pair_upto)

    pairs = range(2 * pairs_per_dir)
    g = [_bdot_nt(jnp.concatenate([a_st[h], r_st[h]], axis=0), jnp.concatenate([k_st[h], b_st[h]], axis=0))
         for h in pairs]
    m1 = [jnp.where(earlier[h], g[h][:p, :p], 0.0) for h in pairs]
    m2 = [jnp.where(earlier[h], g[h][:p, p:], 0.0) for h in pairs]
    n1 = [jnp.where(upto_self[h], g[h][p:, :p], 0.0) for h in pairs]
    n2 = [jnp.where(upto_self[h], g[h][p:, p:], 0.0) for h in pairs]

    in_block = (ii >> 3) == (jj >> 3)
    pw = [-jnp.where(in_block, m2[h], 0.0) for h in pairs]
    inv = [eye + pw[h] for h in pairs]
    for _ in range(2):
        pw = [_bdot(pw[h], pw[h]) for h in pairs]
        inv = [inv[h] + _bdot(inv[h], pw[h]) for h in pairs]
    for sh in (3, 4, 5):
        off = ((ii >> (sh + 1)) == (jj >> (sh + 1))) & ((ii >> sh) != (jj >> sh))
        left = [_bdot(inv[h], jnp.where(off, m2[h], 0.0)) for h in pairs]
        inv = [inv[h] - _bdot(left[h], inv[h]) for h in pairs]

    m1v = [_bdot(m1[h], v_st[h]) for h in pairs]
    n1v = [_bdot(n1[h], v_st[h]) for h in pairs]
    au = [_bdot(inv[h], jnp.concatenate([a_st[h], m1v[h]], axis=1)) for h in pairs]
    nn = [_bdot(n2[h], au[h]) for h in pairs]
    pc = [_bdot_tn(b2_st[h], au[h][:, :p]) for h in pairs]
    qc_t = [_bdot_tn(jnp.concatenate([v_st[h], -au[h][:, p:]], axis=0),
                     jnp.concatenate([k2_st[h], b2_st[h]], axis=0)) for h in pairs]
    s0 = [state_ref[h] for h in pairs]
    y = [_bdot_nt(unstack(r_st[h] - nn[h][:, :p]), s0[h]) + unstack(n1v[h] - nn[h][:, p:]) for h in pairs]
    s_dec = [_bdot_nt(s0[h], pc[h]) for h in pairs]
    for h in pairs:
        d, hp = divmod(h, pairs_per_dir)
        bi, hpb = divmod(hp, pairs_per_batch)
        y_refs[d][bi, :, hpb * p:(hpb + 1) * p] = y[h]
        state_ref[h] = s0[h] * g_chunk[h] - s_dec[h] + qc_t[h]


def _wkv7(lat, ctx, b, n_tok, n_ctx):
    c = WKV_CHUNK
    ncx = n_ctx // c
    nl = n_tok // c
    lat_idx = (lambda n: jnp.maximum(n - ncx, 0), lambda n: nl - 1 - jnp.maximum(n - ncx, 0))
    ctx_idx = (lambda n: jnp.minimum(n, ncx - 1), lambda n: ncx - 1 - jnp.minimum(n, ncx - 1))

    def specs(idx, d):
        shared = pl.BlockSpec((b, c, D_RWKV), lambda n: (0, idx(n), 0))
        per_dir = pl.BlockSpec((b, c, D_RWKV), lambda n: (0, idx(n), d))
        return [shared, shared, shared, per_dir, per_dir, per_dir]

    in_specs, args = [], []
    for d in range(2):
        in_specs += specs(lat_idx[d], d) + specs(ctx_idx[d], d)
        args += list(lat) + list(ctx)
    return pl.pallas_call(
        functools.partial(_wkv7_chunk_kernel, n_ctx_chunks=ncx),
        grid=(ncx + nl,),
        in_specs=in_specs,
        out_specs=[pl.BlockSpec((b, c, D_RWKV), lambda n, d=d: (0, lat_idx[d](n), 0)) for d in range(2)],
        out_shape=[jax.ShapeDtypeStruct((b, n_tok, D_RWKV), F32)] * 2,
        scratch_shapes=[pltpu.VMEM((2 * b * RWKV_HEADS // 2, 2 * RWKV_HEAD, 2 * RWKV_HEAD), F32)],
        compiler_params=_cparams(("arbitrary",)),
        name="wkv7_chunk",
    )(*args)


def _rope(z, cos_t, sin_t):
    lane = lax.broadcasted_iota(jnp.int32, z.shape, 1)
    half = RET_HEAD // 4
    partner = jnp.where((lane & (2 * half - 1)) < half, pltpu.roll(z, RET_HEAD - half, 1), pltpu.roll(z, half, 1))
    return z * cos_t + partner * sin_t


def _retention_kernel(dec_ref, fwd_ref, bwd_ref, ctx_ref, cosf_ref, sinf_ref, cosb_ref, sinb_ref,
                      yf_ref, yb_ref, state_ref, dmat_ref, tail_ref, head_ref, cdec_ref):
    c = RET_CHUNK
    scale = RET_HEAD ** -0.5
    ii = lax.broadcasted_iota(jnp.int32, (c, c), 0)
    jj = lax.broadcasted_iota(jnp.int32, (c, c), 1)
    pos = lax.broadcasted_iota(jnp.int32, (c, RET_HEAD), 0).astype(F32)
    n_ctx_chunks = ctx_ref.shape[1] // c

    def head_slices(ref_val, h):
        q = ref_val[:, h * RET_HEAD:(h + 1) * RET_HEAD]
        k = ref_val[:, D_RET + h * RET_HEAD:D_RET + (h + 1) * RET_HEAD]
        v = ref_val[:, 2 * D_RET + h * RET_HEAD:2 * D_RET + (h + 1) * RET_HEAD]
        return q, k, v

    chains = [(d, h) for d in range(2) for h in range(RET_HEADS)]

    @pl.when(pl.program_id(1) == 0)
    def _():
        for d, h in chains:
            x = jnp.full((1, RET_HEAD), dec_ref[d, h], F32)
            lg = -(jnp.maximum(x, 0.0) + jnp.log(1.0 + jnp.exp(-jnp.abs(x))))
            chunk_decay = jnp.exp(lg * float(c))
            tail = jnp.exp(lg * ((c - 1.0 - pos) if d == 0 else pos))
            rel = (ii - jj) if d == 0 else (jj - ii)
            mask = (rel >= 0) if d == 0 else (rel > 0)
            dmat_ref[d, h] = jnp.where(mask, jnp.exp(lg * jnp.maximum(rel, 0).astype(F32)), 0.0)
            tail_ref[d, h] = tail
            head_ref[d, h] = jnp.exp(lg * ((pos + 1.0) if d == 0 else (c - pos)))
            cdec_ref[d, h] = jnp.broadcast_to(chunk_decay, (SUBLANES, RET_HEAD))
            s = jnp.zeros((RET_HEAD, RET_HEAD), F32)
            order = range(n_ctx_chunks) if d == 0 else range(n_ctx_chunks - 1, -1, -1)
            for cc in order:
                _, kc, vc = head_slices(ctx_ref[0, cc * c:(cc + 1) * c, :], h)
                s = s * chunk_decay + _bdot_tn(kc * scale * tail, vc)
            state_ref[d, h] = s

    qkv = []
    for d, h in chains:
        blk = fwd_ref[0] if d == 0 else bwd_ref[0]
        cos_t = cosf_ref[...] if d == 0 else cosb_ref[...]
        sin_t = sinf_ref[...] if d == 0 else sinb_ref[...]
        q, k, v = head_slices(blk, h)
        qkv.append((_rope(q, cos_t, sin_t), _rope(k, cos_t, sin_t) * scale, v.astype(BF16)))
    s0 = [state_ref[d, h] for d, h in chains]
    scores = [_bdot_nt(q, k) for q, k, _ in qkv]
    inner = [_bdot(scores[i] * dmat_ref[d, h], qkv[i][2]) for i, (d, h) in enumerate(chains)]
    cross = [_bdot(qkv[i][0] * head_ref[d, h], s0[i]) for i, (d, h) in enumerate(chains)]
    upd = [_bdot_tn(qkv[i][1] * tail_ref[d, h], qkv[i][2]) for i, (d, h) in enumerate(chains)]
    for i, (d, h) in enumerate(chains):
        state_ref[d, h] = s0[i] * cdec_ref[d, h, 0:1, :] + upd[i]
        out_ref = yf_ref if d == 0 else yb_ref
        out_ref[0, :, h * RET_HEAD:(h + 1) * RET_HEAD] = inner[i] + cross[i]


def _retention(pt, pt_ctx, ret_decay, cos_t, sin_t):
    b, n, _ = pt.shape
    c = RET_CHUNK
    nc = n // c
    qkv = 3 * D_RET
    fwd = lambda bi, i: (bi, i, 0)
    bwd = lambda bi, i: (bi, nc - 1 - i, 0)
    return pl.pallas_call(
        _retention_kernel,
        grid=(b, nc),
        in_specs=[pl.BlockSpec(memory_space=pltpu.SMEM),
                  pl.BlockSpec((1, c, qkv), fwd),
                  pl.BlockSpec((1, c, qkv), bwd),
                  pl.BlockSpec((1, pt_ctx.shape[1], qkv), lambda bi, i: (bi, 0, 0)),
                  pl.BlockSpec((c, RET_HEAD), lambda bi, i: (i, 0)),
                  pl.BlockSpec((c, RET_HEAD), lambda bi, i: (i, 0)),
                  pl.BlockSpec((c, RET_HEAD), lambda bi, i: (nc - 1 - i, 0)),
                  pl.BlockSpec((c, RET_HEAD), lambda bi, i: (nc - 1 - i, 0))],
        out_specs=[pl.BlockSpec((1, c, D_RET), fwd), pl.BlockSpec((1, c, D_RET), bwd)],
        out_shape=[jax.ShapeDtypeStruct((b, n, D_RET), F32), jax.ShapeDtypeStruct((b, n, D_RET), F32)],
        scratch_shapes=[pltpu.VMEM((2, RET_HEADS, RET_HEAD, RET_HEAD), F32),
                        pltpu.VMEM((2, RET_HEADS, c, c), F32),
                        pltpu.VMEM((2, RET_HEADS, c, RET_HEAD), F32),
                        pltpu.VMEM((2, RET_HEADS, c, RET_HEAD), F32),
                        pltpu.VMEM((2, RET_HEADS, SUBLANES, RET_HEAD), F32)],
        compiler_params=_cparams(("arbitrary", "arbitrary")),
        name="retention",
    )(ret_decay, pt, pt, pt_ctx, cos_t, sin_t, cos_t, sin_t)


def _rope_tables(n_tok):
    nf = RET_HEAD // 4
    lane = np.arange(RET_HEAD)
    inv = ROPE_BASE ** (-jnp.arange(nf, dtype=F32) / nf)
    t = jnp.arange(n_tok)
    pos = jnp.where((lane // (2 * nf) == 0)[None, :], (t // GRID_W)[:, None], (t % GRID_W)[:, None]).astype(F32)
    ang = pos * inv[lane % nf][None, :]
    sign = jnp.where((lane % (2 * nf)) < nf, -1.0, 1.0).astype(F32)
    return jnp.cos(ang), jnp.sin(ang) * sign[None, :]


def _group_norm(y, ones, seg, eps, g, b):
    mu = _segsum(y, ones) * (1.0 / seg)
    yc = y - mu
    var = _segsum(yc * yc, ones) * (1.0 / seg)
    return yc * lax.rsqrt(var + eps) * g + b


def _out_proj_kernel(x_ref, yf_ref, yb_ref, bonus_ref, gate_ref, tf_ref, tb_ref, gt_ref,
                     embg_ref, embb_ref, g1_ref, s2_ref, sh2_ref, rgn_g_ref, rgn_b_ref, tgn_g_ref, tgn_b_ref,
                     ones_r_ref, wout_ref, ln1g_ref, ln1b_ref, wrh_ref, wrl_ref, br_ref,
                     h1_ref, u2_ref, logit_ref):
    y = yf_ref[0] + yb_ref[0]
    o_rwkv = _group_norm(y, ones_r_ref[...], RWKV_HEAD, RWKV_GN_EPS, rgn_g_ref[...], rgn_b_ref[...])
    o_rwkv = (o_rwkv + bonus_ref[0]) * gate_ref[0]
    yt = tf_ref[0] + tb_ref[0]
    gt = gt_ref[0]
    tgn_g = tgn_g_ref[...]
    tgn_b = tgn_b_ref[...]
    o_ret = jnp.concatenate(
        [_layer_norm(yt[:, h * RET_HEAD:(h + 1) * RET_HEAD], tgn_g[:, h * RET_HEAD:(h + 1) * RET_HEAD],
                     tgn_b[:, h * RET_HEAD:(h + 1) * RET_HEAD], RET_GN_EPS) for h in range(RET_HEADS)], axis=-1)
    o_ret = o_ret * (gt * _sigmoid(gt))
    cat = jnp.concatenate([o_rwkv, o_ret], axis=-1).astype(BF16)
    mix = jnp.dot(cat, wout_ref[...], preferred_element_type=F32)
    h = _layer_norm(x_ref[0], embg_ref[...], embb_ref[...])
    h1 = _layer_norm(DEEPNORM_ALPHA * h + g1_ref[0] * mix, ln1g_ref[...], ln1b_ref[...])
    u2 = h1 * (1.0 + s2_ref[0]) + sh2_ref[0]
    h1_ref[0] = h1
    _store_token_tiles(u2_ref, u2)
    logit_ref[0] = _dot_split(u2, wrh_ref[...], wrl_ref[...]) + br_ref[...]


def _out_proj(x, y_f, y_b, bonus, gate, t_f, t_b, pt, vecs, mats):
    b, n, d = x.shape
    t = 256
    tok = lambda width: pl.BlockSpec((1, t, width), lambda bi, i: (bi, i, 0))
    per_b = pl.BlockSpec((1, 1, d), lambda bi, i: (bi, 0, 0))
    small = lambda arr: pl.BlockSpec(arr.shape, lambda bi, i: (0,) * arr.ndim)
    (embg, embb, g1, s2, sh2, rgn_g, rgn_b, tgn_g, tgn_b, ln1g, ln1b, br) = vecs
    (ones_r, wout, wr_hi, wr_lo) = mats
    gt_spec = pl.BlockSpec((1, t, D_RET), lambda bi, i: (bi, i, 3))
    args = (x, y_f, y_b, bonus, gate, t_f, t_b, pt, embg, embb, g1, s2, sh2, rgn_g, rgn_b, tgn_g, tgn_b,
            ones_r, wout, ln1g, ln1b, wr_hi, wr_lo, br)
    in_specs = [tok(d)] + [tok(D_RWKV)] * 6 + [gt_spec, small(embg), small(embb), per_b, per_b, per_b,
                                                small(rgn_g), small(rgn_b), small(tgn_g), small(tgn_b),
                                                small(ones_r), small(wout), small(ln1g),
                                                small(ln1b), small(wr_hi), small(wr_lo), small(br)]
    return pl.pallas_call(
        _out_proj_kernel,
        grid=(b, n // t),
        in_specs=in_specs,
        out_specs=[tok(d), pl.BlockSpec((t * SUBLANES, LANES), lambda bi, i: (bi * (n // t) + i, 0)), tok(LANES)],
        out_shape=[jax.ShapeDtypeStruct((b, n, d), F32), jax.ShapeDtypeStruct((b * n * SUBLANES, LANES), F32),
                   jax.ShapeDtypeStruct((b, n, LANES), F32)],
        compiler_params=_cparams(("arbitrary", "arbitrary")),
        name="out_proj",
    )(*args)


ROUTE_E1, ROUTE_E2, ROUTE_G1, ROUTE_G2, ROUTE_RANK1, ROUTE_RANK2 = range(6)


def _lane_argmax(x, valid, lane):
    m = jnp.max(jnp.where(valid, x, -jnp.inf), axis=-1, keepdims=True)
    idx = jnp.min(jnp.where(valid & (x == m), lane, float(LANES)), axis=-1, keepdims=True)
    return m, idx


def _route_kernel(logit_ref, route_ref, route_t_ref, count_ref, carry_ref):
    @pl.when(pl.program_id(0) == 0)
    def _():
        carry_ref[...] = jnp.zeros_like(carry_ref)

    lg = logit_ref[...]
    t = lg.shape[0]
    lane = lax.broadcasted_iota(jnp.int32, lg.shape, 1).astype(F32)
    gmask = lane < N_GROUPS
    gmax = jnp.max(jnp.where(gmask, lg, -jnp.inf), axis=-1, keepdims=True)
    gexp = jnp.where(gmask, jnp.exp(lg - gmax), 0.0)
    gp = gexp / jnp.sum(gexp, axis=-1, keepdims=True)
    g_w, g_i = _lane_argmax(gp, gmask, lane)

    lo = N_GROUPS + EXPERTS_PER_GROUP * g_i
    emask = (lane >= lo) & (lane < lo + EXPERTS_PER_GROUP)
    emax = jnp.max(jnp.where(emask, lg, -jnp.inf), axis=-1, keepdims=True)
    eexp = jnp.where(emask, jnp.exp(lg - emax), 0.0)
    ep = eexp / jnp.sum(eexp, axis=-1, keepdims=True)
    p1, i1 = _lane_argmax(ep, emask, lane)
    p2, i2 = _lane_argmax(ep, emask & (lane != i1), lane)
    denom = p1 + p2
    gate1 = g_w * p1 / denom
    gate2 = g_w * p2 / denom
    e1 = i1 - N_GROUPS
    e2 = i2 - N_GROUPS

    oh1 = (lane == e1).astype(F32)
    oh2 = (lane == e2).astype(F32)
    cnt = oh1 + oh2
    ri = lax.broadcasted_iota(jnp.int32, (t, t), 0)
    ci = lax.broadcasted_iota(jnp.int32, (t, t), 1)
    before = (ci < ri).astype(BF16)
    seen = jnp.dot(before, cnt.astype(BF16), preferred_element_type=F32) + carry_ref[0:1, :]
    rank1 = jnp.sum(oh1 * seen, axis=-1, keepdims=True)
    rank2 = jnp.sum(oh2 * seen, axis=-1, keepdims=True)
    carry_ref[0:1, :] = carry_ref[0:1, :] + jnp.sum(cnt, axis=0, keepdims=True)

    out = jnp.zeros(lg.shape, F32)
    for slot, val in ((ROUTE_E1, e1.astype(F32)), (ROUTE_E2, e2.astype(F32)), (ROUTE_G1, gate1),
                      (ROUTE_G2, gate2), (ROUTE_RANK1, rank1), (ROUTE_RANK2, rank2)):
        out = jnp.where(lane == slot, val, out)
    route_ref[...] = out
    route_t_ref[...] = out.T[:SUBLANES]
    count_ref[...] = carry_ref[...]


def _route(logits):
    n = logits.shape[0]
    t = 256
    return pl.pallas_call(
        _route_kernel,
        grid=(n // t,),
        in_specs=[pl.BlockSpec((t, LANES), lambda i: (i, 0))],
        out_specs=[pl.BlockSpec((t, LANES), lambda i: (i, 0)), pl.BlockSpec((SUBLANES, t), lambda i: (0, i)),
                   pl.BlockSpec((SUBLANES, LANES), lambda i: (0, 0))],
        out_shape=[jax.ShapeDtypeStruct((n, LANES), F32), jax.ShapeDtypeStruct((SUBLANES, n), F32),
                   jax.ShapeDtypeStruct((SUBLANES, LANES), F32)],
        scratch_shapes=[pltpu.VMEM((SUBLANES, LANES), F32)],
        compiler_params=_cparams(("arbitrary",)),
        name="route",
    )(logits)


def _tile_gather_copy(src_hbm, idx_ref, buf, sem, slot, r):
    dst = buf.at[slot, pl.ds(pl.multiple_of(r * SUBLANES, SUBLANES), SUBLANES), :]
    return pltpu.make_async_copy(src_hbm.at[idx_ref[0, 0, r]], dst, sem.at[slot])


def _start_tile_gather(src_hbm, idx_ref, buf, sem, slot, rows, priorities):
    def body(g, carry):
        for j in range(GATHER_UNROLL):
            copy = _tile_gather_copy(src_hbm, idx_ref, buf, sem, slot, g * GATHER_UNROLL + j)
            copy.start(priority=priorities[j % len(priorities)])
        return carry
    lax.fori_loop(0, rows // GATHER_UNROLL, body, 0)


def _wait_tile_gather(src_hbm, idx_ref, buf, sem, slot, rows):
    def body(r, carry):
        _tile_gather_copy(src_hbm, idx_ref, buf, sem, slot, r).wait()
        return carry
    lax.fori_loop(0, rows, body, 0, unroll=GATHER_UNROLL)


def _sc_gather_rows(table, idx):
    m = idx.shape[0]
    info = plsc.get_sparse_core_info()
    n_workers = info.num_cores * info.num_subcores
    chunk = SC_GATHER_CHUNK
    per_worker = m // n_workers
    n_chunks = per_worker // chunk
    assert m == n_workers * n_chunks * chunk and n_chunks % SC_GATHER_GROUP == 0
    mesh = plsc.VectorSubcoreMesh(core_axis_name="c", subcore_axis_name="s")

    @functools.partial(
        pl.kernel, mesh=mesh,
        out_type=jax.ShapeDtypeStruct((m, SUBLANES, LANES), F32),
        scratch_types=[pltpu.VMEM((n_chunks, chunk), jnp.int32),
                       pltpu.VMEM((SC_GATHER_GROUP * chunk, SUBLANES, LANES), F32),
                       pltpu.SemaphoreType.DMA],
        name="sc_gather_rows",
    )
    def gather(table_hbm, idx_hbm, out_hbm, idx_v, rows_v, sem):
        worker = lax.axis_index("s") * info.num_cores + lax.axis_index("c")
        pltpu.sync_copy(idx_hbm.at[worker], idx_v)

        @pl.loop(0, n_chunks, step=SC_GATHER_GROUP)
        def _(c0):
            copies = [pltpu.async_copy(table_hbm.at[idx_v.at[c0 + j]], rows_v.at[pl.ds(j * chunk, chunk)], sem)
                      for j in range(SC_GATHER_GROUP)]
            for copy in copies:
                copy.wait()
            base = pl.multiple_of(worker * per_worker + c0 * chunk, chunk)
            pltpu.sync_copy(rows_v, out_hbm.at[pl.ds(base, SC_GATHER_GROUP * chunk)])

    return gather(table, idx.reshape(n_workers, n_chunks, chunk))


def _expert_kernel(blk_expert_ref, n_used_ref, x_ref, wg_ref, wu_ref, wd_ref, y_ref, wg_s, wu_s, wd_s):
    i = pl.program_id(0)
    n_used = n_used_ref[0]

    @pl.when(i >= n_used)
    def _():
        y_ref[...] = jnp.zeros_like(y_ref)

    @pl.when(i < n_used)
    def _():
        @pl.when((i == 0) | (blk_expert_ref[i] != blk_expert_ref[jnp.maximum(i - 1, 0)]))
        def _():
            wg_s[...] = wg_ref[0].astype(BF16)
            wu_s[...] = wu_ref[0].astype(BF16)
            wd_s[...] = wd_ref[0].astype(BF16)

        x = _load_token_tiles(x_ref, EXPERT_ROWS).astype(BF16)
        hg = jnp.dot(x, wg_s[...], preferred_element_type=F32)
        hu = jnp.dot(x, wu_s[...], preferred_element_type=F32)
        act = (hg * _sigmoid(hg) * hu).astype(BF16)
        _store_token_tiles(y_ref, jnp.dot(act, wd_s[...], preferred_element_type=F32))


def _expert_mlp(x_tiles, block_expert, n_used, w_gate, w_up, w_down):
    n_blk = block_expert.shape[0]
    d = w_gate.shape[1]
    hdim = w_gate.shape[2]
    rows = EXPERT_ROWS
    used = lambda i, nu: jnp.minimum(i, nu[0] - 1)
    weight = lambda i, be, nu: (be[used(i, nu)], 0, 0)
    grid_spec = pltpu.PrefetchScalarGridSpec(
        num_scalar_prefetch=2,
        grid=(n_blk,),
        in_specs=[pl.BlockSpec((rows * SUBLANES, LANES), lambda i, be, nu: (used(i, nu), 0)),
                  pl.BlockSpec((1, d, hdim), weight),
                  pl.BlockSpec((1, d, hdim), weight),
                  pl.BlockSpec((1, hdim, d), weight)],
        out_specs=pl.BlockSpec((rows * SUBLANES, LANES), lambda i, be, nu: (i, 0)),
        scratch_shapes=[pltpu.VMEM((d, hdim), BF16), pltpu.VMEM((d, hdim), BF16), pltpu.VMEM((hdim, d), BF16)],
    )
    return pl.pallas_call(
        _expert_kernel,
        grid_spec=grid_spec,
        out_shape=jax.ShapeDtypeStruct((n_blk * rows * SUBLANES, LANES), F32),
        compiler_params=_cparams(("arbitrary",)),
        name="expert_mlp",
    )(block_expert, n_used, x_tiles, w_gate, w_up, w_down)


def _combine_kernel(d1_ref, d2_ref, d1n_ref, d2n_ref, y_hbm, route_ref, h1_ref, g2_ref, lng_ref, lnb_ref,
                    o_ref, abuf, bbuf, sem_a, sem_b):
    i = pl.program_id(0)
    n = pl.num_programs(0)
    slot = i % 2
    rows = o_ref.shape[0]

    @pl.when(i == 0)
    def _():
        _start_tile_gather(y_hbm, d1_ref, abuf, sem_a, 0, rows, GATHER_PRIORITIES)
        _start_tile_gather(y_hbm, d2_ref, bbuf, sem_b, 0, rows, GATHER_PRIORITIES)

    @pl.when(i + 1 < n)
    def _():
        _start_tile_gather(y_hbm, d1n_ref, abuf, sem_a, 1 - slot, rows, GATHER_PRIORITIES)
        _start_tile_gather(y_hbm, d2n_ref, bbuf, sem_b, 1 - slot, rows, GATHER_PRIORITIES)

    _wait_tile_gather(y_hbm, d1_ref, abuf, sem_a, slot, rows)
    _wait_tile_gather(y_hbm, d2_ref, bbuf, sem_b, slot, rows)
    route = route_ref[...]
    f = (_load_token_tiles(abuf.at[slot], rows) * route[:, ROUTE_G1:ROUTE_G1 + 1]
         + _load_token_tiles(bbuf.at[slot], rows) * route[:, ROUTE_G2:ROUTE_G2 + 1])
    o_ref[...] = _layer_norm(DEEPNORM_ALPHA * h1_ref[...] + g2_ref[0] * f, lng_ref[...], lnb_ref[...])


def _combine(y_tiles, dest1, dest2, route, h1, g2, ln_g, ln_b, tokens_per_batch):
    n, d = h1.shape
    t = COMBINE_ROWS
    nt = n // t
    per_b = tokens_per_batch // t
    d1 = dest1.reshape(nt, 1, t)
    d2 = dest2.reshape(nt, 1, t)
    cur = pl.BlockSpec((1, 1, t), lambda i: (i, 0, 0), memory_space=pltpu.SMEM)
    nxt = pl.BlockSpec((1, 1, t), lambda i: (jnp.minimum(i + 1, nt - 1), 0, 0), memory_space=pltpu.SMEM)
    small = lambda arr: pl.BlockSpec(arr.shape, lambda i: (0,) * arr.ndim)
    return pl.pallas_call(
        _combine_kernel,
        grid=(nt,),
        in_specs=[cur, cur, nxt, nxt, pl.BlockSpec(memory_space=pl.ANY),
                  pl.BlockSpec((t, LANES), lambda i: (i, 0)),
                  pl.BlockSpec((t, d), lambda i: (i, 0)),
                  pl.BlockSpec((1, 1, d), lambda i: (i // per_b, 0, 0)),
                  small(ln_g), small(ln_b)],
        out_specs=pl.BlockSpec((t, d), lambda i: (i, 0)),
        out_shape=jax.ShapeDtypeStruct((n, d), F32),
        scratch_shapes=[pltpu.VMEM((2, t * SUBLANES, LANES), F32), pltpu.VMEM((2, t * SUBLANES, LANES), F32),
                        pltpu.SemaphoreType.DMA((2,)), pltpu.SemaphoreType.DMA((2,))],
        compiler_params=_cparams(("arbitrary",)),
        name="combine",
    )(d1, d2, d1, d2, y_tiles.reshape(-1, SUBLANES, LANES), route, h1, g2, ln_g, ln_b)


def _hi_lo(w):
    hi = w.astype(BF16)
    return jnp.stack([hi, (w - hi.astype(F32)).astype(BF16)])


def _block_diag2(w):
    z = jnp.zeros_like(w[0])
    return jnp.concatenate([jnp.concatenate([w[0], z], axis=1), jnp.concatenate([z, w[1]], axis=1)], axis=0)


def kernel(x, c, ctx, c_ctx, emb_ln_g, emb_ln_b, w_mod, b_mod, w_in, tshift_mu, rwkv_w0, rwkv_w2, rwkv_a0, rwkv_a2, rwkv_g2, rwkv_k_k, rwkv_k_a, rwkv_r_k, rwkv_gn_g, rwkv_gn_b, ret_decay, ret_gn_g, ret_gn_b, w_out, ln1_g, ln1_b, router_group, router_group_bias, router_expert, router_expert_bias, expert_w_gate, expert_w_up, expert_w_down, ln2_g, ln2_b):
    assert w_mod.shape[0] == 1, "written for DEPTH == 1 (context outputs are never emitted)"
    b, n_tok, d = x.shape
    n_ctx = ctx.shape[1]
    row = lambda v: v.reshape(1, -1)

    c_rows = jnp.zeros((SUBLANES, d), F32).at[:b].set(c).at[b].set(c_ctx)
    mod = _modulation(c_rows, w_mod[0], row(b_mod[0]))
    sh1, s1, g1, sh2, s2, g2 = [mod[:b, j * d:(j + 1) * d].reshape(b, 1, d) for j in range(6)]
    sh1c, s1c = [jnp.broadcast_to(mod[b, j * d:(j + 1) * d].reshape(1, 1, d), (b, 1, d)) for j in range(2)]

    w_in_bf16 = w_in[0].astype(BF16)
    pr, pt = _in_proj(x, row(emb_ln_g), row(emb_ln_b), s1, sh1, w_in_bf16)
    pr_c, pt_c = _in_proj(ctx, row(emb_ln_g), row(emb_ln_b), s1c, sh1c, w_in_bf16)

    prep_params = (row(tshift_mu[0]), row(rwkv_w0[0]), _hi_lo(_block_diag2(rwkv_w2[0])), row(rwkv_a0[0]),
                   _hi_lo(_block_diag2(rwkv_a2[0])), _hi_lo(rwkv_g2[0]), row(rwkv_k_k[0]), row(rwkv_k_a[0]),
                   row(rwkv_r_k[0]),
                   _segment_ones(D_RWKV, RWKV_HEAD))
    lat = _rwkv_prepare(pr, prep_params, grid_shift=True)
    cx = _rwkv_prepare(pr_c, prep_params, grid_shift=False)
    r_l, v_l, kk_l, w_l, kd_l, bb_l, gate_l, bonus_l = lat
    r_c, v_c, kk_c, w_c, kd_c, bb_c, _, _ = cx

    y_f, y_b = _wkv7((r_l, v_l, kk_l, w_l, kd_l, bb_l), (r_c, v_c, kk_c, w_c, kd_c, bb_c), b, n_tok, n_ctx)

    cos_t, sin_t = _rope_tables(n_tok)
    t_f, t_b = _retention(pt, pt_c, ret_decay[0], cos_t, sin_t)

    wr = jnp.zeros((d, LANES), F32).at[:, :N_GROUPS].set(router_group[0])
    wr = wr.at[:, N_GROUPS:N_GROUPS + N_EXPERTS].set(router_expert[0])
    br = jnp.zeros((1, LANES), F32).at[0, :N_GROUPS].set(router_group_bias[0])
    br = br.at[0, N_GROUPS:N_GROUPS + N_EXPERTS].set(router_expert_bias[0].reshape(-1))
    vecs = (row(emb_ln_g), row(emb_ln_b), g1, s2, sh2, row(rwkv_gn_g[0]), row(rwkv_gn_b[0]),
            row(ret_gn_g[0]), row(ret_gn_b[0]), row(ln1_g[0]), row(ln1_b[0]), br)
    wr_hi_lo = _hi_lo(wr)
    mats = (_segment_ones(D_RWKV, RWKV_HEAD), w_out[0].astype(BF16), wr_hi_lo[0], wr_hi_lo[1])
    h1, u2, logits = _out_proj(x, y_f, y_b, bonus_l, gate_l, t_f, t_b, pt, vecs, mats)

    n_all = b * n_tok
    route, route_t, counts = _route(logits.reshape(n_all, LANES))

    e1 = route_t[ROUTE_E1].astype(jnp.int32)
    e2 = route_t[ROUTE_E2].astype(jnp.int32)
    cnt = counts[0, :N_EXPERTS].astype(jnp.int32)
    padded = ((cnt + EXPERT_ROWS - 1) // EXPERT_ROWS) * EXPERT_ROWS
    pends = jnp.cumsum(padded)
    pstarts = pends - padded
    expert_ids = jnp.arange(N_EXPERTS, dtype=jnp.int32)
    start_of = lambda e: jnp.sum(jnp.where(e[:, None] == expert_ids[None, :], pstarts[None, :], 0), axis=1)
    dest1 = start_of(e1) + route_t[ROUTE_RANK1].astype(jnp.int32)
    dest2 = start_of(e2) + route_t[ROUTE_RANK2].astype(jnp.int32)
    n_blk = -(-(n_all * 2) // EXPERT_ROWS) + N_EXPERTS
    tok_ids = jnp.arange(n_all, dtype=jnp.int32)
    slot_tok = jnp.zeros((n_blk * EXPERT_ROWS,), jnp.int32).at[jnp.concatenate([dest1, dest2])].set(
        jnp.concatenate([tok_ids, tok_ids]))
    block_start = jnp.arange(n_blk, dtype=jnp.int32) * EXPERT_ROWS
    block_expert = jnp.minimum(jnp.sum((block_start[:, None] >= pends[None, :]).astype(jnp.int32), axis=1),
                               N_EXPERTS - 1)

    n_used = (pends[N_EXPERTS - 1:] // EXPERT_ROWS).astype(jnp.int32)
    x_tiles = _sc_gather_rows(u2.reshape(n_all, SUBLANES, LANES), slot_tok)
    y_tiles = _expert_mlp(x_tiles.reshape(-1, LANES), block_expert, n_used,
                          expert_w_gate[0], expert_w_up[0], expert_w_down[0])
    out = _combine(y_tiles, dest1, dest2, route, h1.reshape(n_all, d), g2, row(ln2_g[0]), row(ln2_b[0]), n_tok)
    return out.reshape(b, n_tok, d)
```

```python
import functools
import math

import jax
import jax.numpy as jnp
import numpy as np
from jax import lax
from jax.experimental import pallas as pl
from jax.experimental.pallas import tpu as pltpu

F32 = jnp.float32
BF16 = jnp.bfloat16
HIGHEST = lax.Precision.HIGHEST

GRID_W = 64
D_RWKV = 512
RWKV_HEAD = 64
RWKV_HEADS = D_RWKV // RWKV_HEAD
DECAY_LORA = 64
AAA_LORA = 64
GATE_LORA = 128
D_RET = 512
RET_HEADS = 4
RET_HEAD = D_RET // RET_HEADS
RET_CHUNK = 128
RWKV_COLS = 3 * D_RWKV + 2 * (DECAY_LORA + AAA_LORA) + GATE_LORA
RET_COLS = 4 * D_RET
N_GROUPS = 4
EXPERTS_PER_GROUP = 8
N_EXPERTS = N_GROUPS * EXPERTS_PER_GROUP
EXPERT_HIDDEN = 512
MOE_BLOCK = 128
ROPE_BASE = 10000.0
LN_EPS = 1e-5
RWKV_GN_EPS = 64e-5
RET_GN_EPS = 1e-5
DEEPNORM_ALPHA = 2.0 ** 0.25
EXP_NEG_HALF = math.exp(-0.5)

LANES = 128
SUBLANES = 8
VMEM_LIMIT_BYTES = 56 * 1024 * 1024

WKV_CHUNK = 64


TOKEN_SLAB = 4
EXPERT_ROWS = 256
COMBINE_ROWS = 128
GATHER_UNROLL = 8
GATHER_PRIORITIES = (0, 1)


def _pack_bf16_pairs(x):
    half = x.shape[1] // 2

    def bf16_bits(v):
        b = lax.bitcast_convert_type(v, jnp.uint32)
        return (b + jnp.uint32(0x7FFF) + ((b >> 16) & jnp.uint32(1))) >> 16

    return bf16_bits(x[:, :half]) | (bf16_bits(x[:, half:]) << 16)


def _unpack_bf16_pairs(p):
    lo = lax.bitcast_convert_type(p << 16, F32)
    hi = lax.bitcast_convert_type(p & jnp.uint32(0xFFFF0000), F32)
    return jnp.concatenate([lo, hi], axis=-1)


def _store_token_slabs(ref, x):
    rows = x.shape[0]
    for j in range(TOKEN_SLAB):
        ref[pl.ds(j, rows, stride=TOKEN_SLAB), :] = x[:, j * LANES:(j + 1) * LANES]


def _load_token_slabs(ref, rows):
    return jnp.concatenate([ref[pl.ds(j, rows, stride=TOKEN_SLAB), :] for j in range(TOKEN_SLAB)], axis=-1)


def _cparams(sem):
    return pltpu.CompilerParams(dimension_semantics=sem, vmem_limit_bytes=VMEM_LIMIT_BYTES)


def _layer_norm(x, g, b, eps=LN_EPS):
    mu = jnp.mean(x, axis=-1, keepdims=True)
    xc = x - mu
    var = jnp.mean(xc * xc, axis=-1, keepdims=True)
    return xc * lax.rsqrt(var + eps) * g + b


def _sigmoid(x):
    return 1.0 / (1.0 + jnp.exp(-x))


def _split_bf16(x):
    hi = x.astype(BF16)
    return hi, (x - hi.astype(F32)).astype(BF16)


def _segsum(x, ones_bf16):
    t = x.shape[0]
    s = jnp.dot(jnp.concatenate(_split_bf16(x), axis=0), ones_bf16, preferred_element_type=F32)
    return s[:t] + s[t:]


def _dot_split(x, w_hi, w_lo):
    hi, lo = _split_bf16(x)
    acc = jnp.dot(hi, w_hi, preferred_element_type=F32)
    acc = acc + jnp.dot(lo, w_hi, preferred_element_type=F32)
    return acc + jnp.dot(hi, w_lo, preferred_element_type=F32)


def _segment_ones(width, seg):
    idx = np.arange(width) // seg
    return jnp.asarray(idx[:, None] == idx[None, :], dtype=BF16)


def _mod_kernel(c_ref, w_ref, b_ref, o_ref):
    c = c_ref[...]
    sc = c * _sigmoid(c)
    o_ref[...] = jnp.dot(sc, w_ref[...], precision=HIGHEST, preferred_element_type=F32) + b_ref[...]


def _modulation(c_rows, w_mod, b_mod):
    rows, d = c_rows.shape
    n = w_mod.shape[1]
    tn = 1536
    return pl.pallas_call(
        _mod_kernel,
        grid=(n // tn,),
        in_specs=[pl.BlockSpec((rows, d), lambda j: (0, 0)),
                  pl.BlockSpec((d, tn), lambda j: (0, j)),
                  pl.BlockSpec((1, tn), lambda j: (0, j))],
        out_specs=pl.BlockSpec((rows, tn), lambda j: (0, j)),
        out_shape=jax.ShapeDtypeStruct((rows, n), F32),
        compiler_params=_cparams(("arbitrary",)),
        name="modulation",
    )(c_rows, w_mod, b_mod)


def _in_proj_kernel(x_ref, g_ref, b_ref, s_ref, sh_ref, w_ref, pr_ref, pt_ref):
    h = _layer_norm(x_ref[0], g_ref[...], b_ref[...])
    u = h * (1.0 + s_ref[0]) + sh_ref[0]
    p = jnp.dot(u.astype(BF16), w_ref[...], preferred_element_type=F32)
    pr_ref[0] = p[:, :RWKV_COLS]
    pt_ref[0] = p[:, RWKV_COLS:]


def _in_proj(x, ln_g, ln_b, s1, sh1, w_in_bf16):
    b, n, d = x.shape
    tm = 256
    cols = w_in_bf16.shape[1]
    return pl.pallas_call(
        _in_proj_kernel,
        grid=(b, n // tm),
        in_specs=[pl.BlockSpec((1, tm, d), lambda bi, i: (bi, i, 0)),
                  pl.BlockSpec((1, d), lambda bi, i: (0, 0)),
                  pl.BlockSpec((1, d), lambda bi, i: (0, 0)),
                  pl.BlockSpec((1, 1, d), lambda bi, i: (bi, 0, 0)),
                  pl.BlockSpec((1, 1, d), lambda bi, i: (bi, 0, 0)),
                  pl.BlockSpec((d, cols), lambda bi, i: (0, 0))],
        out_specs=[pl.BlockSpec((1, tm, RWKV_COLS), lambda bi, i: (bi, i, 0)),
                   pl.BlockSpec((1, tm, RET_COLS), lambda bi, i: (bi, i, 0))],
        out_shape=[jax.ShapeDtypeStruct((b, n, RWKV_COLS), F32),
                   jax.ShapeDtypeStruct((b, n, RET_COLS), F32)],
        compiler_params=_cparams(("arbitrary", "arbitrary")),
        name="in_proj",
    )(x, ln_g, ln_b, s1, sh1, w_in_bf16)


def _rwkv_prepare_kernel(cur_ref, prev_ref, next_ref, mu_ref, w0_ref, w2_ref, a0_ref, a2_ref, g2_ref,
                         kk_scale_ref, ka_ref, rk_ref, ones_ref,
                         r_ref, v_ref, kk_ref, w_ref, kd_ref, bb_ref, g_ref, bonus_ref,
                         *, grid_shift, n_tok):
    cur = cur_ref[0]
    t, c = cur.shape
    row = lax.broadcasted_iota(jnp.int32, (t, c), 0)
    lane = lax.broadcasted_iota(jnp.int32, (t, c), 1)
    prev_tok = pltpu.roll(cur, 1, 0)
    next_tok = pltpu.roll(cur, t - 1, 0)
    if grid_shift:
        col = row & (GRID_W - 1)
        tok = row + pl.program_id(1) * t
        left = jnp.where(col > 0, prev_tok, 0.0)
        right = jnp.where(col < GRID_W - 1, next_tok, 0.0)
        up = jnp.where(tok >= GRID_W, jnp.concatenate([prev_ref[0], cur[:t - GRID_W]], axis=0), 0.0)
        down = jnp.where(tok < n_tok - GRID_W, jnp.concatenate([cur[GRID_W:], next_ref[0]], axis=0), 0.0)
        cm = lane & 3
        shifted = jnp.where(cm == 0, left, jnp.where(cm == 1, right, jnp.where(cm == 2, up, down)))
    else:
        prev_tok = jnp.where(row > 0, prev_tok, 0.0)
        next_tok = jnp.where(row < t - 1, next_tok, 0.0)
        shifted = jnp.where((lane & 1) == 0, prev_tok, next_tok)
    pm = cur + mu_ref[...] * (shifted - cur)

    r = pm[:, 0:D_RWKV]
    k = pm[:, D_RWKV:2 * D_RWKV]
    v = pm[:, 2 * D_RWKV:3 * D_RWKV]
    o = 3 * D_RWKV
    lw = pm[:, o:o + 2 * DECAY_LORA]
    la = pm[:, o + 2 * DECAY_LORA:o + 2 * (DECAY_LORA + AAA_LORA)]
    lg = pm[:, o + 2 * (DECAY_LORA + AAA_LORA):]

    w = w0_ref[...] + _dot_split(jnp.tanh(lw), w2_ref[0], w2_ref[1])
    log_decay = -EXP_NEG_HALF * _sigmoid(w)
    a = _sigmoid(a0_ref[...] + _dot_split(la, a2_ref[0], a2_ref[1]))
    gate = _dot_split(_sigmoid(lg), g2_ref[0], g2_ref[1])

    ones = ones_ref[...]
    kk_raw = k * kk_scale_ref[...]
    kk = kk_raw / jnp.maximum(jnp.sqrt(_segsum(kk_raw * kk_raw, ones)), 1e-12)
    ka = ka_ref[...]
    a0 = a[:, :D_RWKV]
    a1 = a[:, D_RWKV:]
    kd0 = k * (1.0 + (a0 - 1.0) * ka)
    kd1 = k * (1.0 + (a1 - 1.0) * ka)
    bonus = _segsum(r * (kd0 + kd1) * rk_ref[...], ones) * v

    r_ref[0] = r
    v_ref[0] = v
    kk_ref[0] = kk
    w_ref[0] = log_decay
    kd_ref[0] = jnp.concatenate([kd0, kd1], axis=-1)
    bb_ref[0] = jnp.concatenate([kk * a0, kk * a1], axis=-1)
    g_ref[0] = gate
    bonus_ref[0] = bonus


def _rwkv_prepare(pr, params, grid_shift):
    b, n, c = pr.shape
    t = 256
    if not grid_shift:
        assert n == t, "sequence token shift is written for a single tile"
    halo_blocks = n // GRID_W
    per_tile = t // GRID_W
    small = lambda shape: pl.BlockSpec(shape, lambda bi, i: (0,) * len(shape))
    tok_spec = lambda width: pl.BlockSpec((1, t, width), lambda bi, i: (bi, i, 0))
    out_widths = (D_RWKV, D_RWKV, D_RWKV, 2 * D_RWKV, 2 * D_RWKV, 2 * D_RWKV, D_RWKV, D_RWKV)
    kernel = functools.partial(_rwkv_prepare_kernel, grid_shift=grid_shift, n_tok=n)
    return pl.pallas_call(
        kernel,
        grid=(b, n // t),
        in_specs=[tok_spec(c),
                  pl.BlockSpec((1, GRID_W, c), lambda bi, i: (bi, jnp.maximum(i * per_tile - 1, 0), 0)),
                  pl.BlockSpec((1, GRID_W, c),
                               lambda bi, i: (bi, jnp.minimum((i + 1) * per_tile, halo_blocks - 1), 0)),
                  small((1, c)), small((1, 2 * D_RWKV)), small((2, 2 * DECAY_LORA, 2 * D_RWKV)),
                  small((1, 2 * D_RWKV)), small((2, 2 * AAA_LORA, 2 * D_RWKV)), small((2, GATE_LORA, D_RWKV)),
                  small((1, D_RWKV)), small((1, D_RWKV)), small((1, D_RWKV)), small((D_RWKV, D_RWKV))],
        out_specs=[tok_spec(wd) for wd in out_widths],
        out_shape=[jax.ShapeDtypeStruct((b, n, wd), F32) for wd in out_widths],
        compiler_params=_cparams(("arbitrary", "arbitrary")),
        name="rwkv_prepare",
    )(pr, pr, pr, *params)


def _bdot(a, b):
    return jnp.dot(a.astype(BF16), b.astype(BF16), preferred_element_type=F32)


def _bdot_nt(a, b):
    return lax.dot_general(a.astype(BF16), b.astype(BF16), (((1,), (1,)), ((), ())), preferred_element_type=F32)


def _bdot_tn(a, b):
    return lax.dot_general(a.astype(BF16), b.astype(BF16), (((0,), (0,)), ((), ())), preferred_element_type=F32)


def _wkv7_chunk_kernel(*refs, n_ctx_chunks):
    c = WKV_CHUNK
    p = 2 * c
    n_in = 12
    in_refs = (refs[:n_in], refs[n_in:2 * n_in])
    y_refs = refs[2 * n_in:2 * n_in + 2]
    state_ref = refs[2 * n_in + 2]
    n = pl.program_id(0)
    n_batch = in_refs[0][0].shape[0]
    pairs_per_batch = RWKV_HEADS // 2
    pairs_per_dir = n_batch * pairs_per_batch

    @pl.when(n == 0)
    def _():
        state_ref[...] = jnp.zeros_like(state_ref)

    is_ctx = n < n_ctx_chunks
    ti = lax.broadcasted_iota(jnp.int32, (c, c), 0)
    tj = lax.broadcasted_iota(jnp.int32, (c, c), 1)
    ri = lax.broadcasted_iota(jnp.int32, (p, p), 0)
    ci = lax.broadcasted_iota(jnp.int32, (p, p), 1)
    same_head = (ri >= c) == (ci >= c)
    ii = ri & (c - 1)
    jj = ci & (c - 1)
    eye = (ri == ci).astype(F32)
    first = lax.broadcasted_iota(jnp.int32, (c, p), 1) < RWKV_HEAD

    def stack(x):
        return jnp.concatenate([jnp.where(first, x, 0.0), jnp.where(first, 0.0, x)], axis=0)

    def unstack(x):
        return x[:c] + x[c:]

    a_st, r_st, k_st, b_st, k2_st, b2_st, v_st, g_chunk, earlier, upto_self = ([] for _ in range(10))
    for d in range(2):
        r_l, v_l, kk_l, lw_l, kd_l, bb_l, r_c, v_c, kk_c, lw_c, kd_c, bb_c = in_refs[d]
        pick = lambda xc, xl: jnp.concatenate(
            [jnp.where(is_ctx, xc[bi], xl[bi]) for bi in range(n_batch)], axis=-1)
        r, v, kk, lw, kd, bb = (pick(r_c, r_l), pick(v_c, v_l), pick(kk_c, kk_l), pick(lw_c, lw_l),
                                pick(kd_c, kd_l), pick(bb_c, bb_l))
        before = (tj < ti) if d == 0 else (tj > ti)
        upto = (before | (ti == tj)).astype(BF16)
        hi = lw.astype(BF16)
        r1 = lw - hi.astype(F32)
        mid = r1.astype(BF16)
        lo = (r1 - mid.astype(F32)).astype(BF16)
        cum = (jnp.dot(upto, hi, preferred_element_type=F32) + jnp.dot(upto, mid, preferred_element_type=F32)
               + jnp.dot(upto, lo, preferred_element_type=F32))
        tot = jnp.sum(lw, axis=0, keepdims=True)
        e_neg = jnp.exp(-cum)
        e_rem = jnp.exp(tot - cum)
        alpha = kk * jnp.exp(cum - lw)
        rho = r * jnp.exp(cum)
        beta = bb * e_neg
        kappa = kd * e_neg
        kappa_rem = kd * e_rem
        beta_rem = bb * e_rem
        g_all = jnp.exp(tot)
        pair_before = same_head & ((jj < ii) if d == 0 else (jj > ii))
        pair_upto = pair_before | (ri == ci)
        for hp in range(pairs_per_dir):
            sl = slice(hp * p, (hp + 1) * p)
            a_st.append(stack(alpha[:, sl]))
            r_st.append(stack(rho[:, sl]))
            k_st.append(stack(kappa[:, sl]))
            b_st.append(stack(beta[:, sl]))
            k2_st.append(stack(kappa_rem[:, sl]))
            b2_st.append(stack(beta_rem[:, sl]))
            v_st.append(stack(v[:, sl]))
            g_chunk.append(g_all[:, sl])
            earlier.append(pair_before)
            upto_self.append(pair_upto)

    pairs = range(2 * pairs_per_dir)
    g = [_bdot_nt(jnp.concatenate([a_st[h], r_st[h]], axis=0), jnp.concatenate([k_st[h], b_st[h]], axis=0))
         for h in pairs]
    m1 = [jnp.where(earlier[h], g[h][:p, :p], 0.0) for h in pairs]
    m2 = [jnp.where(earlier[h], g[h][:p, p:], 0.0) for h in pairs]
    n1 = [jnp.where(upto_self[h], g[h][p:, :p], 0.0) for h in pairs]
    n2 = [jnp.where(upto_self[h], g[h][p:, p:], 0.0) for h in pairs]

    in_block = (ii >> 3) == (jj >> 3)
    pw = [-jnp.where(in_block, m2[h], 0.0) for h in pairs]
    inv = [eye + pw[h] for h in pairs]
    pw = [_bdot(pw[h], pw[h]) for h in pairs]
    both = [_bdot(jnp.concatenate([inv[h], pw[h]], axis=0), pw[h]) for h in pairs]
    inv = [inv[h] + both[h][:p] for h in pairs]
    inv = [inv[h] + _bdot(inv[h], both[h][p:]) for h in pairs]
    for sh in (3, 4, 5):
        off = ((ii >> (sh + 1)) == (jj >> (sh + 1))) & ((ii >> sh) != (jj >> sh))
        left = [_bdot(inv[h], jnp.where(off, m2[h], 0.0)) for h in pairs]
        inv = [inv[h] - _bdot(left[h], inv[h]) for h in pairs]

    mnv = [_bdot(jnp.concatenate([m1[h], n1[h]], axis=0), v_st[h]) for h in pairs]
    m1v = [mnv[h][:p] for h in pairs]
    n1v = [mnv[h][p:] for h in pairs]
    au = [_bdot(inv[h], jnp.concatenate([a_st[h], m1v[h]], axis=1)) for h in pairs]
    nn = [_bdot(n2[h], au[h]) for h in pairs]
    pc = [_bdot_tn(b2_st[h], au[h][:, :p]) for h in pairs]
    qc_t = [_bdot_tn(jnp.concatenate([v_st[h], -au[h][:, p:]], axis=0),
                     jnp.concatenate([k2_st[h], b2_st[h]], axis=0)) for h in pairs]
    s0 = [state_ref[h] for h in pairs]
    y = [_bdot_nt(unstack(r_st[h] - nn[h][:, :p]), s0[h]) + unstack(n1v[h] - nn[h][:, p:]) for h in pairs]
    s_dec = [_bdot_nt(s0[h], pc[h]) for h in pairs]
    for h in pairs:
        d, hp = divmod(h, pairs_per_dir)
        bi, hpb = divmod(hp, pairs_per_batch)
        y_refs[d][bi, :, hpb * p:(hpb + 1) * p] = y[h]
        state_ref[h] = s0[h] * g_chunk[h] - s_dec[h] + qc_t[h]


def _wkv7(lat, ctx, b, n_tok, n_ctx):
    c = WKV_CHUNK
    ncx = n_ctx // c
    nl = n_tok // c
    lat_idx = (lambda n: jnp.maximum(n - ncx, 0), lambda n: nl - 1 - jnp.maximum(n - ncx, 0))
    ctx_idx = (lambda n: jnp.minimum(n, ncx - 1), lambda n: ncx - 1 - jnp.minimum(n, ncx - 1))

    def specs(idx, d):
        shared = pl.BlockSpec((b, c, D_RWKV), lambda n: (0, idx(n), 0))
        per_dir = pl.BlockSpec((b, c, D_RWKV), lambda n: (0, idx(n), d))
        return [shared, shared, shared, per_dir, per_dir, per_dir]

    in_specs, args = [], []
    for d in range(2):
        in_specs += specs(lat_idx[d], d) + specs(ctx_idx[d], d)
        args += list(lat) + list(ctx)
    return pl.pallas_call(
        functools.partial(_wkv7_chunk_kernel, n_ctx_chunks=ncx),
        grid=(ncx + nl,),
        in_specs=in_specs,
        out_specs=[pl.BlockSpec((b, c, D_RWKV), lambda n, d=d: (0, lat_idx[d](n), 0)) for d in range(2)],
        out_shape=[jax.ShapeDtypeStruct((b, n_tok, D_RWKV), F32)] * 2,
        scratch_shapes=[pltpu.VMEM((2 * b * RWKV_HEADS // 2, 2 * RWKV_HEAD, 2 * RWKV_HEAD), F32)],
        compiler_params=_cparams(("arbitrary",)),
        name="wkv7_chunk",
    )(*args)


def _rope(z, cos_t, sin_t):
    lane = lax.broadcasted_iota(jnp.int32, z.shape, 1)
    half = RET_HEAD // 4
    partner = jnp.where((lane & (2 * half - 1)) < half, pltpu.roll(z, RET_HEAD - half, 1), pltpu.roll(z, half, 1))
    return z * cos_t + partner * sin_t


def _retention_kernel(dec_ref, fwd_ref, bwd_ref, ctx_ref, cosf_ref, sinf_ref, cosb_ref, sinb_ref,
                      yf_ref, yb_ref, state_ref, dmat_ref, tail_ref, head_ref, cdec_ref):
    c = RET_CHUNK
    scale = RET_HEAD ** -0.5
    ii = lax.broadcasted_iota(jnp.int32, (c, c), 0)
    jj = lax.broadcasted_iota(jnp.int32, (c, c), 1)
    pos = lax.broadcasted_iota(jnp.int32, (c, RET_HEAD), 0).astype(F32)
    n_ctx_chunks = ctx_ref.shape[1] // c

    def head_slices(ref_val, h):
        q = ref_val[:, h * RET_HEAD:(h + 1) * RET_HEAD]
        k = ref_val[:, D_RET + h * RET_HEAD:D_RET + (h + 1) * RET_HEAD]
        v = ref_val[:, 2 * D_RET + h * RET_HEAD:2 * D_RET + (h + 1) * RET_HEAD]
        return q, k, v

    n_batch = fwd_ref.shape[0]
    heads = [(d, h) for d in range(2) for h in range(RET_HEADS)]
    chains = [(bi, d, h) for bi in range(n_batch) for d, h in heads]

    @pl.when(pl.program_id(0) == 0)
    def _():
        for d, h in heads:
            x = jnp.full((1, RET_HEAD), dec_ref[d, h], F32)
            lg = -(jnp.maximum(x, 0.0) + jnp.log(1.0 + jnp.exp(-jnp.abs(x))))
            chunk_decay = jnp.exp(lg * float(c))
            tail = jnp.exp(lg * ((c - 1.0 - pos) if d == 0 else pos))
            rel = (ii - jj) if d == 0 else (jj - ii)
            mask = (rel >= 0) if d == 0 else (rel > 0)
            dmat_ref[d, h] = jnp.where(mask, jnp.exp(lg * jnp.maximum(rel, 0).astype(F32)), 0.0)
            tail_ref[d, h] = tail
            head_ref[d, h] = jnp.exp(lg * ((pos + 1.0) if d == 0 else (c - pos)))
            cdec_ref[d, h] = jnp.broadcast_to(chunk_decay, (SUBLANES, RET_HEAD))
            order = range(n_ctx_chunks) if d == 0 else range(n_ctx_chunks - 1, -1, -1)
            for bi in range(n_batch):
                s = jnp.zeros((RET_HEAD, RET_HEAD), F32)
                for cc in order:
                    _, kc, vc = head_slices(ctx_ref[bi, cc * c:(cc + 1) * c, :], h)
                    s = s * chunk_decay + _bdot_tn(kc * scale * tail, vc)
                state_ref[bi, d, h] = s

    qkv = []
    for bi, d, h in chains:
        blk = fwd_ref[bi] if d == 0 else bwd_ref[bi]
        cos_t = cosf_ref[...] if d == 0 else cosb_ref[...]
        sin_t = sinf_ref[...] if d == 0 else sinb_ref[...]
        q, k, v = head_slices(blk, h)
        qkv.append((_rope(q, cos_t, sin_t), _rope(k, cos_t, sin_t) * scale, v.astype(BF16)))
    s0 = [state_ref[bi, d, h] for bi, d, h in chains]
    scores = [_bdot_nt(q, k) for q, k, _ in qkv]
    inner = [_bdot(scores[i] * dmat_ref[d, h], qkv[i][2]) for i, (_, d, h) in enumerate(chains)]
    cross = [_bdot(qkv[i][0] * head_ref[d, h], s0[i]) for i, (_, d, h) in enumerate(chains)]
    upd = [_bdot_tn(qkv[i][1] * tail_ref[d, h], qkv[i][2]) for i, (_, d, h) in enumerate(chains)]
    for i, (bi, d, h) in enumerate(chains):
        state_ref[bi, d, h] = s0[i] * cdec_ref[d, h, 0:1, :] + upd[i]
        out_ref = yf_ref if d == 0 else yb_ref
        out_ref[bi, :, h * RET_HEAD:(h + 1) * RET_HEAD] = inner[i] + cross[i]


def _retention(pt, pt_ctx, ret_decay, cos_t, sin_t):
    b, n, _ = pt.shape
    c = RET_CHUNK
    nc = n // c
    qkv = 3 * D_RET
    fwd = lambda i: (0, i, 0)
    bwd = lambda i: (0, nc - 1 - i, 0)
    return pl.pallas_call(
        _retention_kernel,
        grid=(nc,),
        in_specs=[pl.BlockSpec(memory_space=pltpu.SMEM),
                  pl.BlockSpec((b, c, qkv), fwd),
                  pl.BlockSpec((b, c, qkv), bwd),
                  pl.BlockSpec((b, pt_ctx.shape[1], qkv), lambda i: (0, 0, 0)),
                  pl.BlockSpec((c, RET_HEAD), lambda i: (i, 0)),
                  pl.BlockSpec((c, RET_HEAD), lambda i: (i, 0)),
                  pl.BlockSpec((c, RET_HEAD), lambda i: (nc - 1 - i, 0)),
                  pl.BlockSpec((c, RET_HEAD), lambda i: (nc - 1 - i, 0))],
        out_specs=[pl.BlockSpec((b, c, D_RET), fwd), pl.BlockSpec((b, c, D_RET), bwd)],
        out_shape=[jax.ShapeDtypeStruct((b, n, D_RET), F32), jax.ShapeDtypeStruct((b, n, D_RET), F32)],
        scratch_shapes=[pltpu.VMEM((b, 2, RET_HEADS, RET_HEAD, RET_HEAD), F32),
                        pltpu.VMEM((2, RET_HEADS, c, c), F32),
                        pltpu.VMEM((2, RET_HEADS, c, RET_HEAD), F32),
                        pltpu.VMEM((2, RET_HEADS, c, RET_HEAD), F32),
                        pltpu.VMEM((2, RET_HEADS, SUBLANES, RET_HEAD), F32)],
        compiler_params=_cparams(("arbitrary",)),
        name="retention",
    )(ret_decay, pt, pt, pt_ctx, cos_t, sin_t, cos_t, sin_t)


def _rope_tables(n_tok):
    nf = RET_HEAD // 4
    lane = np.arange(RET_HEAD)
    inv = ROPE_BASE ** (-jnp.arange(nf, dtype=F32) / nf)
    t = jnp.arange(n_tok)
    pos = jnp.where((lane // (2 * nf) == 0)[None, :], (t // GRID_W)[:, None], (t % GRID_W)[:, None]).astype(F32)
    ang = pos * inv[lane % nf][None, :]
    sign = jnp.where((lane % (2 * nf)) < nf, -1.0, 1.0).astype(F32)
    return jnp.cos(ang), jnp.sin(ang) * sign[None, :]


def _group_norm(y, ones, seg, eps, g, b):
    mu = _segsum(y, ones) * (1.0 / seg)
    yc = y - mu
    var = _segsum(yc * yc, ones) * (1.0 / seg)
    return yc * lax.rsqrt(var + eps) * g + b


def _out_proj_kernel(x_ref, yf_ref, yb_ref, bonus_ref, gate_ref, tf_ref, tb_ref, gt_ref,
                     embg_ref, embb_ref, g1_ref, s2_ref, sh2_ref, rgn_g_ref, rgn_b_ref, tgn_g_ref, tgn_b_ref,
                     ones_r_ref, wout_ref, ln1g_ref, ln1b_ref, wrh_ref, wrl_ref, br_ref,
                     h1_ref, u2_ref, logit_ref):
    y = yf_ref[0] + yb_ref[0]
    o_rwkv = _group_norm(y, ones_r_ref[...], RWKV_HEAD, RWKV_GN_EPS, rgn_g_ref[...], rgn_b_ref[...])
    o_rwkv = (o_rwkv + bonus_ref[0]) * gate_ref[0]
    yt = tf_ref[0] + tb_ref[0]
    gt = gt_ref[0]
    tgn_g = tgn_g_ref[...]
    tgn_b = tgn_b_ref[...]
    o_ret = jnp.concatenate(
        [_layer_norm(yt[:, h * RET_HEAD:(h + 1) * RET_HEAD], tgn_g[:, h * RET_HEAD:(h + 1) * RET_HEAD],
                     tgn_b[:, h * RET_HEAD:(h + 1) * RET_HEAD], RET_GN_EPS) for h in range(RET_HEADS)], axis=-1)
    o_ret = o_ret * (gt * _sigmoid(gt))
    cat = jnp.concatenate([o_rwkv, o_ret], axis=-1).astype(BF16)
    mix = jnp.dot(cat, wout_ref[...], preferred_element_type=F32)
    h = _layer_norm(x_ref[0], embg_ref[...], embb_ref[...])
    h1 = _layer_norm(DEEPNORM_ALPHA * h + g1_ref[0] * mix, ln1g_ref[...], ln1b_ref[...])
    u2 = h1 * (1.0 + s2_ref[0]) + sh2_ref[0]
    h1_ref[0] = h1
    _store_token_slabs(u2_ref, _pack_bf16_pairs(u2))
    logit_ref[0] = _dot_split(u2, wrh_ref[...], wrl_ref[...]) + br_ref[...]


def _out_proj(x, y_f, y_b, bonus, gate, t_f, t_b, pt, vecs, mats):
    b, n, d = x.shape
    t = 256
    tok = lambda width: pl.BlockSpec((1, t, width), lambda bi, i: (bi, i, 0))
    per_b = pl.BlockSpec((1, 1, d), lambda bi, i: (bi, 0, 0))
    small = lambda arr: pl.BlockSpec(arr.shape, lambda bi, i: (0,) * arr.ndim)
    (embg, embb, g1, s2, sh2, rgn_g, rgn_b, tgn_g, tgn_b, ln1g, ln1b, br) = vecs
    (ones_r, wout, wr_hi, wr_lo) = mats
    gt_spec = pl.BlockSpec((1, t, D_RET), lambda bi, i: (bi, i, 3))
    args = (x, y_f, y_b, bonus, gate, t_f, t_b, pt, embg, embb, g1, s2, sh2, rgn_g, rgn_b, tgn_g, tgn_b,
            ones_r, wout, ln1g, ln1b, wr_hi, wr_lo, br)
    in_specs = [tok(d)] + [tok(D_RWKV)] * 6 + [gt_spec, small(embg), small(embb), per_b, per_b, per_b,
                                                small(rgn_g), small(rgn_b), small(tgn_g), small(tgn_b),
                                                small(ones_r), small(wout), small(ln1g),
                                                small(ln1b), small(wr_hi), small(wr_lo), small(br)]
    return pl.pallas_call(
        _out_proj_kernel,
        grid=(b, n // t),
        in_specs=in_specs,
        out_specs=[tok(d), pl.BlockSpec((t * TOKEN_SLAB, LANES), lambda bi, i: (bi * (n // t) + i, 0)), tok(LANES)],
        out_shape=[jax.ShapeDtypeStruct((b, n, d), F32),
                   jax.ShapeDtypeStruct((b * n * TOKEN_SLAB, LANES), jnp.uint32),
                   jax.ShapeDtypeStruct((b, n, LANES), F32)],
        compiler_params=_cparams(("arbitrary", "arbitrary")),
        name="out_proj",
    )(*args)


ROUTE_E1, ROUTE_E2, ROUTE_G1, ROUTE_G2, ROUTE_RANK1, ROUTE_RANK2 = range(6)


def _lane_argmax(x, valid, lane):
    m = jnp.max(jnp.where(valid, x, -jnp.inf), axis=-1, keepdims=True)
    idx = jnp.min(jnp.where(valid & (x == m), lane, float(LANES)), axis=-1, keepdims=True)
    return m, idx


def _route_kernel(logit_ref, route_ref, route_t_ref, count_ref, carry_ref):
    @pl.when(pl.program_id(0) == 0)
    def _():
        carry_ref[...] = jnp.zeros_like(carry_ref)

    lg = logit_ref[...]
    t = lg.shape[0]
    lane = lax.broadcasted_iota(jnp.int32, lg.shape, 1).astype(F32)
    gmask = lane < N_GROUPS
    gmax = jnp.max(jnp.where(gmask, lg, -jnp.inf), axis=-1, keepdims=True)
    gexp = jnp.where(gmask, jnp.exp(lg - gmax), 0.0)
    gp = gexp / jnp.sum(gexp, axis=-1, keepdims=True)
    g_w, g_i = _lane_argmax(gp, gmask, lane)

    lo = N_GROUPS + EXPERTS_PER_GROUP * g_i
    emask = (lane >= lo) & (lane < lo + EXPERTS_PER_GROUP)
    emax = jnp.max(jnp.where(emask, lg, -jnp.inf), axis=-1, keepdims=True)
    eexp = jnp.where(emask, jnp.exp(lg - emax), 0.0)
    ep = eexp / jnp.sum(eexp, axis=-1, keepdims=True)
    p1, i1 = _lane_argmax(ep, emask, lane)
    p2, i2 = _lane_argmax(ep, emask & (lane != i1), lane)
    denom = p1 + p2
    gate1 = g_w * p1 / denom
    gate2 = g_w * p2 / denom
    e1 = i1 - N_GROUPS
    e2 = i2 - N_GROUPS

    oh1 = (lane == e1).astype(F32)
    oh2 = (lane == e2).astype(F32)
    cnt = oh1 + oh2
    ri = lax.broadcasted_iota(jnp.int32, (t, t), 0)
    ci = lax.broadcasted_iota(jnp.int32, (t, t), 1)
    before = (ci < ri).astype(BF16)
    seen = jnp.dot(before, cnt.astype(BF16), preferred_element_type=F32) + carry_ref[0:1, :]
    rank1 = jnp.sum(oh1 * seen, axis=-1, keepdims=True)
    rank2 = jnp.sum(oh2 * seen, axis=-1, keepdims=True)
    carry_ref[0:1, :] = carry_ref[0:1, :] + jnp.sum(cnt, axis=0, keepdims=True)

    out = jnp.zeros(lg.shape, F32)
    for slot, val in ((ROUTE_E1, e1.astype(F32)), (ROUTE_E2, e2.astype(F32)), (ROUTE_G1, gate1),
                      (ROUTE_G2, gate2), (ROUTE_RANK1, rank1), (ROUTE_RANK2, rank2)):
        out = jnp.where(lane == slot, val, out)
    route_ref[...] = out
    route_t_ref[...] = out.T[:SUBLANES]
    count_ref[...] = carry_ref[...]


def _route(logits):
    n = logits.shape[0]
    t = 256
    return pl.pallas_call(
        _route_kernel,
        grid=(n // t,),
        in_specs=[pl.BlockSpec((t, LANES), lambda i: (i, 0))],
        out_specs=[pl.BlockSpec((t, LANES), lambda i: (i, 0)), pl.BlockSpec((SUBLANES, t), lambda i: (0, i)),
                   pl.BlockSpec((SUBLANES, LANES), lambda i: (0, 0))],
        out_shape=[jax.ShapeDtypeStruct((n, LANES), F32), jax.ShapeDtypeStruct((SUBLANES, n), F32),
                   jax.ShapeDtypeStruct((SUBLANES, LANES), F32)],
        scratch_shapes=[pltpu.VMEM((SUBLANES, LANES), F32)],
        compiler_params=_cparams(("arbitrary",)),
        name="route",
    )(logits)


def _tile_gather_copy(src_hbm, idx_ref, buf, sem, slot, r):
    src = src_hbm.at[pl.ds(pl.multiple_of(idx_ref[0, 0, r] * TOKEN_SLAB, TOKEN_SLAB), TOKEN_SLAB), :]
    dst = buf.at[slot, pl.ds(pl.multiple_of(r * TOKEN_SLAB, TOKEN_SLAB), TOKEN_SLAB), :]
    return pltpu.make_async_copy(src, dst, sem.at[slot])


def _start_tile_gather(src_hbm, idx_ref, buf, sem, slot, rows, priorities):
    def body(g, carry):
        for j in range(GATHER_UNROLL):
            copy = _tile_gather_copy(src_hbm, idx_ref, buf, sem, slot, g * GATHER_UNROLL + j)
            copy.start(priority=priorities[j % len(priorities)])
        return carry
    lax.fori_loop(0, rows // GATHER_UNROLL, body, 0)


def _wait_tile_gather(src_hbm, idx_ref, buf, sem, slot, rows):
    def body(r, carry):
        _tile_gather_copy(src_hbm, idx_ref, buf, sem, slot, r).wait()
        return carry
    lax.fori_loop(0, rows, body, 0, unroll=GATHER_UNROLL)


def _expert_kernel(blk_expert_ref, n_used_ref, tok_ref, tok_next_ref, u_hbm, wg_ref, wu_ref, wd_ref, y_ref,
                   xbuf, sem, wg_s, wu_s, wd_s):
    i = pl.program_id(0)
    n_used = n_used_ref[0]
    slot = i % 2

    @pl.when(i == 0)
    def _():
        _start_tile_gather(u_hbm, tok_ref, xbuf, sem, 0, EXPERT_ROWS, GATHER_PRIORITIES)

    @pl.when(i + 1 < n_used)
    def _():
        _start_tile_gather(u_hbm, tok_next_ref, xbuf, sem, 1 - slot, EXPERT_ROWS, GATHER_PRIORITIES)

    @pl.when(i >= n_used)
    def _():
        y_ref[...] = jnp.zeros_like(y_ref)

    @pl.when(i < n_used)
    def _():
        @pl.when((i == 0) | (blk_expert_ref[i] != blk_expert_ref[jnp.maximum(i - 1, 0)]))
        def _():
            wg_s[...] = wg_ref[0].astype(BF16)
            wu_s[...] = wu_ref[0].astype(BF16)
            wd_s[...] = wd_ref[0].astype(BF16)

        _wait_tile_gather(u_hbm, tok_ref, xbuf, sem, slot, EXPERT_ROWS)
        x = _unpack_bf16_pairs(_load_token_slabs(xbuf.at[slot], EXPERT_ROWS)).astype(BF16)
        hg = jnp.dot(x, wg_s[...], preferred_element_type=F32)
        hu = jnp.dot(x, wu_s[...], preferred_element_type=F32)
        act = (hg * _sigmoid(hg) * hu).astype(BF16)
        _store_token_slabs(y_ref, _pack_bf16_pairs(jnp.dot(act, wd_s[...], preferred_element_type=F32)))


def _expert_mlp(u2_tiles, slot_tok, block_expert, n_used, w_gate, w_up, w_down):
    n_blk = block_expert.shape[0]
    d = w_gate.shape[1]
    hdim = w_gate.shape[2]
    rows = EXPERT_ROWS
    tok3 = slot_tok.reshape(n_blk, 1, rows)
    used = lambda i, nu: jnp.minimum(i, nu[0] - 1)
    weight = lambda i, be, nu: (be[used(i, nu)], 0, 0)
    grid_spec = pltpu.PrefetchScalarGridSpec(
        num_scalar_prefetch=2,
        grid=(n_blk,),
        in_specs=[pl.BlockSpec((1, 1, rows), lambda i, be, nu: (used(i, nu), 0, 0), memory_space=pltpu.SMEM),
                  pl.BlockSpec((1, 1, rows), lambda i, be, nu: (used(i + 1, nu), 0, 0), memory_space=pltpu.SMEM),
                  pl.BlockSpec(memory_space=pl.ANY),
                  pl.BlockSpec((1, d, hdim), weight),
                  pl.BlockSpec((1, d, hdim), weight),
                  pl.BlockSpec((1, hdim, d), weight)],
        out_specs=pl.BlockSpec((rows * TOKEN_SLAB, LANES), lambda i, be, nu: (i, 0)),
        scratch_shapes=[pltpu.VMEM((2, rows * TOKEN_SLAB, LANES), jnp.uint32), pltpu.SemaphoreType.DMA((2,)),
                        pltpu.VMEM((d, hdim), BF16), pltpu.VMEM((d, hdim), BF16), pltpu.VMEM((hdim, d), BF16)],
    )
    return pl.pallas_call(
        _expert_kernel,
        grid_spec=grid_spec,
        out_shape=jax.ShapeDtypeStruct((n_blk * rows * TOKEN_SLAB, LANES), jnp.uint32),
        compiler_params=_cparams(("arbitrary",)),
        name="expert_mlp",
    )(block_expert, n_used, tok3, tok3, u2_tiles, w_gate, w_up, w_down)


def _combine_kernel(d1_ref, d2_ref, d1n_ref, d2n_ref, y_hbm, route_ref, h1_ref, g2_ref, lng_ref, lnb_ref,
                    o_ref, abuf, bbuf, sem_a, sem_b):
    i = pl.program_id(0)
    n = pl.num_programs(0)
    slot = i % 2
    rows = o_ref.shape[0]

    @pl.when(i == 0)
    def _():
        _start_tile_gather(y_hbm, d1_ref, abuf, sem_a, 0, rows, GATHER_PRIORITIES)
        _start_tile_gather(y_hbm, d2_ref, bbuf, sem_b, 0, rows, GATHER_PRIORITIES)

    @pl.when(i + 1 < n)
    def _():
        _start_tile_gather(y_hbm, d1n_ref, abuf, sem_a, 1 - slot, rows, GATHER_PRIORITIES)
        _start_tile_gather(y_hbm, d2n_ref, bbuf, sem_b, 1 - slot, rows, GATHER_PRIORITIES)

    _wait_tile_gather(y_hbm, d1_ref, abuf, sem_a, slot, rows)
    _wait_tile_gather(y_hbm, d2_ref, bbuf, sem_b, slot, rows)
    route = route_ref[...]
    f = (_unpack_bf16_pairs(_load_token_slabs(abuf.at[slot], rows)) * route[:, ROUTE_G1:ROUTE_G1 + 1]
         + _unpack_bf16_pairs(_load_token_slabs(bbuf.at[slot], rows)) * route[:, ROUTE_G2:ROUTE_G2 + 1])
    o_ref[...] = _layer_norm(DEEPNORM_ALPHA * h1_ref[...] + g2_ref[0] * f, lng_ref[...], lnb_ref[...])


def _combine(y_tiles, dest1, dest2, route, h1, g2, ln_g, ln_b, tokens_per_batch):
    n, d = h1.shape
    t = COMBINE_ROWS
    nt = n // t
    per_b = tokens_per_batch // t
    d1 = dest1.reshape(nt, 1, t)
    d2 = dest2.reshape(nt, 1, t)
    cur = pl.BlockSpec((1, 1, t), lambda i: (i, 0, 0), memory_space=pltpu.SMEM)
    nxt = pl.BlockSpec((1, 1, t), lambda i: (jnp.minimum(i + 1, nt - 1), 0, 0), memory_space=pltpu.SMEM)
    small = lambda arr: pl.BlockSpec(arr.shape, lambda i: (0,) * arr.ndim)
    return pl.pallas_call(
        _combine_kernel,
        grid=(nt,),
        in_specs=[cur, cur, nxt, nxt, pl.BlockSpec(memory_space=pl.ANY),
                  pl.BlockSpec((t, LANES), lambda i: (i, 0)),
                  pl.BlockSpec((t, d), lambda i: (i, 0)),
                  pl.BlockSpec((1, 1, d), lambda i: (i // per_b, 0, 0)),
                  small(ln_g), small(ln_b)],
        out_specs=pl.BlockSpec((t, d), lambda i: (i, 0)),
        out_shape=jax.ShapeDtypeStruct((n, d), F32),
        scratch_shapes=[pltpu.VMEM((2, t * TOKEN_SLAB, LANES), jnp.uint32),
                        pltpu.VMEM((2, t * TOKEN_SLAB, LANES), jnp.uint32),
                        pltpu.SemaphoreType.DMA((2,)), pltpu.SemaphoreType.DMA((2,))],
        compiler_params=_cparams(("arbitrary",)),
        name="combine",
    )(d1, d2, d1, d2, y_tiles, route, h1, g2, ln_g, ln_b)


def _hi_lo(w):
    hi = w.astype(BF16)
    return jnp.stack([hi, (w - hi.astype(F32)).astype(BF16)])


def _block_diag2(w):
    z = jnp.zeros_like(w[0])
    return jnp.concatenate([jnp.concatenate([w[0], z], axis=1), jnp.concatenate([z, w[1]], axis=1)], axis=0)


def kernel(x, c, ctx, c_ctx, emb_ln_g, emb_ln_b, w_mod, b_mod, w_in, tshift_mu, rwkv_w0, rwkv_w2, rwkv_a0, rwkv_a2, rwkv_g2, rwkv_k_k, rwkv_k_a, rwkv_r_k, rwkv_gn_g, rwkv_gn_b, ret_decay, ret_gn_g, ret_gn_b, w_out, ln1_g, ln1_b, router_group, router_group_bias, router_expert, router_expert_bias, expert_w_gate, expert_w_up, expert_w_down, ln2_g, ln2_b):
    assert w_mod.shape[0] == 1, "written for DEPTH == 1 (context outputs are never emitted)"
    b, n_tok, d = x.shape
    n_ctx = ctx.shape[1]
    row = lambda v: v.reshape(1, -1)

    c_rows = jnp.zeros((SUBLANES, d), F32).at[:b].set(c).at[b].set(c_ctx)
    mod = _modulation(c_rows, w_mod[0], row(b_mod[0]))
    sh1, s1, g1, sh2, s2, g2 = [mod[:b, j * d:(j + 1) * d].reshape(b, 1, d) for j in range(6)]
    sh1c, s1c = [jnp.broadcast_to(mod[b, j * d:(j + 1) * d].reshape(1, 1, d), (b, 1, d)) for j in range(2)]

    w_in_bf16 = w_in[0].astype(BF16)
    pr, pt = _in_proj(x, row(emb_ln_g), row(emb_ln_b), s1, sh1, w_in_bf16)
    pr_c, pt_c = _in_proj(ctx, row(emb_ln_g), row(emb_ln_b), s1c, sh1c, w_in_bf16)

    prep_params = (row(tshift_mu[0]), row(rwkv_w0[0]), _hi_lo(_block_diag2(rwkv_w2[0])), row(rwkv_a0[0]),
                   _hi_lo(_block_diag2(rwkv_a2[0])), _hi_lo(rwkv_g2[0]), row(rwkv_k_k[0]), row(rwkv_k_a[0]),
                   row(rwkv_r_k[0]),
                   _segment_ones(D_RWKV, RWKV_HEAD))
    lat = _rwkv_prepare(pr, prep_params, grid_shift=True)
    cx = _rwkv_prepare(pr_c, prep_params, grid_shift=False)
    r_l, v_l, kk_l, w_l, kd_l, bb_l, gate_l, bonus_l = lat
    r_c, v_c, kk_c, w_c, kd_c, bb_c, _, _ = cx

    y_f, y_b = _wkv7((r_l, v_l, kk_l, w_l, kd_l, bb_l), (r_c, v_c, kk_c, w_c, kd_c, bb_c), b, n_tok, n_ctx)

    cos_t, sin_t = _rope_tables(n_tok)
    t_f, t_b = _retention(pt, pt_c, ret_decay[0], cos_t, sin_t)

    wr = jnp.zeros((d, LANES), F32).at[:, :N_GROUPS].set(router_group[0])
    wr = wr.at[:, N_GROUPS:N_GROUPS + N_EXPERTS].set(router_expert[0])
    br = jnp.zeros((1, LANES), F32).at[0, :N_GROUPS].set(router_group_bias[0])
    br = br.at[0, N_GROUPS:N_GROUPS + N_EXPERTS].set(router_expert_bias[0].reshape(-1))
    vecs = (row(emb_ln_g), row(emb_ln_b), g1, s2, sh2, row(rwkv_gn_g[0]), row(rwkv_gn_b[0]),
            row(ret_gn_g[0]), row(ret_gn_b[0]), row(ln1_g[0]), row(ln1_b[0]), br)
    wr_hi_lo = _hi_lo(wr)
    mats = (_segment_ones(D_RWKV, RWKV_HEAD), w_out[0].astype(BF16), wr_hi_lo[0], wr_hi_lo[1])
    h1, u2, logits = _out_proj(x, y_f, y_b, bonus_l, gate_l, t_f, t_b, pt, vecs, mats)

    n_all = b * n_tok
    route, route_t, counts = _route(logits.reshape(n_all, LANES))

    e1 = route_t[ROUTE_E1].astype(jnp.int32)
    e2 = route_t[ROUTE_E2].astype(jnp.int32)
    cnt = counts[0, :N_EXPERTS].astype(jnp.int32)
    padded = ((cnt + EXPERT_ROWS - 1) // EXPERT_ROWS) * EXPERT_ROWS
    pends = jnp.cumsum(padded)
    pstarts = pends - padded
    expert_ids = jnp.arange(N_EXPERTS, dtype=jnp.int32)
    start_of = lambda e: jnp.sum(jnp.where(e[:, None] == expert_ids[None, :], pstarts[None, :], 0), axis=1)
    dest1 = start_of(e1) + route_t[ROUTE_RANK1].astype(jnp.int32)
    dest2 = start_of(e2) + route_t[ROUTE_RANK2].astype(jnp.int32)
    n_blk = -(-(n_all * 2) // EXPERT_ROWS) + N_EXPERTS
    tok_ids = jnp.arange(n_all, dtype=jnp.int32)
    slot_tok = jnp.zeros((n_blk * EXPERT_ROWS,), jnp.int32).at[jnp.concatenate([dest1, dest2])].set(
        jnp.concatenate([tok_ids, tok_ids]))
    block_start = jnp.arange(n_blk, dtype=jnp.int32) * EXPERT_ROWS
    block_expert = jnp.minimum(jnp.sum((block_start[:, None] >= pends[None, :]).astype(jnp.int32), axis=1),
                               N_EXPERTS - 1)

    n_used = (pends[N_EXPERTS - 1:] // EXPERT_ROWS).astype(jnp.int32)
    y_tiles = _expert_mlp(u2, slot_tok, block_expert, n_used,
                          expert_w_gate[0], expert_w_up[0], expert_w_down[0])
    out = _combine(y_tiles, dest1, dest2, route, h1.reshape(n_all, d), g2, row(ln2_g[0]), row(ln2_b[0]), n_tok)
    return out.reshape(b, n_tok, d)
```

```python
import functools
import math

import jax
import jax.numpy as jnp
import numpy as np
from jax import lax
from jax.experimental import pallas as pl
from jax.experimental.pallas import tpu as pltpu

F32 = jnp.float32
BF16 = jnp.bfloat16
HIGHEST = lax.Precision.HIGHEST

GRID_W = 64
D_RWKV = 512
RWKV_HEAD = 64
RWKV_HEADS = D_RWKV // RWKV_HEAD
DECAY_LORA = 64
AAA_LORA = 64
GATE_LORA = 128
D_RET = 512
RET_HEADS = 4
RET_HEAD = D_RET // RET_HEADS
RET_CHUNK = 128
RWKV_COLS = 3 * D_RWKV + 2 * (DECAY_LORA + AAA_LORA) + GATE_LORA
RET_COLS = 4 * D_RET
N_GROUPS = 4
EXPERTS_PER_GROUP = 8
N_EXPERTS = N_GROUPS * EXPERTS_PER_GROUP
EXPERT_HIDDEN = 512
MOE_BLOCK = 128
ROPE_BASE = 10000.0
LN_EPS = 1e-5
RWKV_GN_EPS = 64e-5
RET_GN_EPS = 1e-5
DEEPNORM_ALPHA = 2.0 ** 0.25
EXP_NEG_HALF = math.exp(-0.5)

LANES = 128
SUBLANES = 8
VMEM_LIMIT_BYTES = 56 * 1024 * 1024

WKV_CHUNK = 64


TOKEN_SLAB = 4
EXPERT_ROWS = 256
COMBINE_ROWS = 128
GATHER_UNROLL = 8
GATHER_PRIORITIES = (0, 1)


def _pack_bf16_pairs(x):
    half = x.shape[1] // 2

    def bf16_bits(v):
        b = lax.bitcast_convert_type(v, jnp.uint32)
        return (b + jnp.uint32(0x7FFF) + ((b >> 16) & jnp.uint32(1))) >> 16

    return bf16_bits(x[:, :half]) | (bf16_bits(x[:, half:]) << 16)


def _unpack_bf16_pairs(p):
    lo = lax.bitcast_convert_type(p << 16, F32)
    hi = lax.bitcast_convert_type(p & jnp.uint32(0xFFFF0000), F32)
    return jnp.concatenate([lo, hi], axis=-1)


def _store_token_slabs(ref, x):
    rows = x.shape[0]
    for j in range(TOKEN_SLAB):
        ref[pl.ds(j, rows, stride=TOKEN_SLAB), :] = x[:, j * LANES:(j + 1) * LANES]


def _load_token_slabs(ref, rows):
    return jnp.concatenate([ref[pl.ds(j, rows, stride=TOKEN_SLAB), :] for j in range(TOKEN_SLAB)], axis=-1)


def _cparams(sem):
    return pltpu.CompilerParams(dimension_semantics=sem, vmem_limit_bytes=VMEM_LIMIT_BYTES)


def _layer_norm(x, g, b, eps=LN_EPS):
    mu = jnp.mean(x, axis=-1, keepdims=True)
    xc = x - mu
    var = jnp.mean(xc * xc, axis=-1, keepdims=True)
    return xc * lax.rsqrt(var + eps) * g + b


def _sigmoid(x):
    return 1.0 / (1.0 + jnp.exp(-x))


def _split_bf16(x):
    hi = x.astype(BF16)
    return hi, (x - hi.astype(F32)).astype(BF16)


def _segsum(x, ones_bf16):
    t = x.shape[0]
    s = jnp.dot(jnp.concatenate(_split_bf16(x), axis=0), ones_bf16, preferred_element_type=F32)
    return s[:t] + s[t:]


def _dot_split(x, w_hi, w_lo):
    hi, lo = _split_bf16(x)
    acc = jnp.dot(hi, w_hi, preferred_element_type=F32)
    acc = acc + jnp.dot(lo, w_hi, preferred_element_type=F32)
    return acc + jnp.dot(hi, w_lo, preferred_element_type=F32)


def _segment_ones(width, seg):
    idx = np.arange(width) // seg
    return jnp.asarray(idx[:, None] == idx[None, :], dtype=BF16)


def _mod_kernel(c_ref, w_ref, b_ref, o_ref):
    c = c_ref[...]
    sc = c * _sigmoid(c)
    o_ref[...] = jnp.dot(sc, w_ref[...], precision=HIGHEST, preferred_element_type=F32) + b_ref[...]


def _modulation(c_rows, w_mod, b_mod):
    rows, d = c_rows.shape
    n = w_mod.shape[1]
    tn = 1536
    return pl.pallas_call(
        _mod_kernel,
        grid=(n // tn,),
        in_specs=[pl.BlockSpec((rows, d), lambda j: (0, 0)),
                  pl.BlockSpec((d, tn), lambda j: (0, j)),
                  pl.BlockSpec((1, tn), lambda j: (0, j))],
        out_specs=pl.BlockSpec((rows, tn), lambda j: (0, j)),
        out_shape=jax.ShapeDtypeStruct((rows, n), F32),
        compiler_params=_cparams(("arbitrary",)),
        name="modulation",
    )(c_rows, w_mod, b_mod)


def _in_proj_kernel(x_ref, g_ref, b_ref, s_ref, sh_ref, w_ref, pr_ref, pt_ref):
    h = _layer_norm(x_ref[0], g_ref[...], b_ref[...])
    u = h * (1.0 + s_ref[0]) + sh_ref[0]
    p = jnp.dot(u.astype(BF16), w_ref[...], preferred_element_type=F32)
    pr_ref[0] = p[:, :RWKV_COLS]
    pt_ref[0] = p[:, RWKV_COLS:]


def _in_proj(x, ln_g, ln_b, s1, sh1, w_in_bf16):
    b, n, d = x.shape
    tm = 256
    cols = w_in_bf16.shape[1]
    return pl.pallas_call(
        _in_proj_kernel,
        grid=(b, n // tm),
        in_specs=[pl.BlockSpec((1, tm, d), lambda bi, i: (bi, i, 0)),
                  pl.BlockSpec((1, d), lambda bi, i: (0, 0)),
                  pl.BlockSpec((1, d), lambda bi, i: (0, 0)),
                  pl.BlockSpec((1, 1, d), lambda bi, i: (bi, 0, 0)),
                  pl.BlockSpec((1, 1, d), lambda bi, i: (bi, 0, 0)),
                  pl.BlockSpec((d, cols), lambda bi, i: (0, 0))],
        out_specs=[pl.BlockSpec((1, tm, RWKV_COLS), lambda bi, i: (bi, i, 0)),
                   pl.BlockSpec((1, tm, RET_COLS), lambda bi, i: (bi, i, 0))],
        out_shape=[jax.ShapeDtypeStruct((b, n, RWKV_COLS), F32),
                   jax.ShapeDtypeStruct((b, n, RET_COLS), F32)],
        compiler_params=_cparams(("arbitrary", "arbitrary")),
        name="in_proj",
    )(x, ln_g, ln_b, s1, sh1, w_in_bf16)


def _rwkv_prepare_kernel(cur_ref, prev_ref, next_ref, mu_ref, w0_ref, w2_ref, a0_ref, a2_ref, g2_ref,
                         kk_scale_ref, ka_ref, rk_ref, ones_ref,
                         r_ref, v_ref, kk_ref, w_ref, kd_ref, bb_ref, g_ref, bonus_ref,
                         *, grid_shift, n_tok):
    cur = cur_ref[0]
    t, c = cur.shape
    row = lax.broadcasted_iota(jnp.int32, (t, c), 0)
    lane = lax.broadcasted_iota(jnp.int32, (t, c), 1)
    prev_tok = pltpu.roll(cur, 1, 0)
    next_tok = pltpu.roll(cur, t - 1, 0)
    if grid_shift:
        col = row & (GRID_W - 1)
        tok = row + pl.program_id(1) * t
        left = jnp.where(col > 0, prev_tok, 0.0)
        right = jnp.where(col < GRID_W - 1, next_tok, 0.0)
        up = jnp.where(tok >= GRID_W, jnp.concatenate([prev_ref[0], cur[:t - GRID_W]], axis=0), 0.0)
        down = jnp.where(tok < n_tok - GRID_W, jnp.concatenate([cur[GRID_W:], next_ref[0]], axis=0), 0.0)
        cm = lane & 3
        shifted = jnp.where(cm == 0, left, jnp.where(cm == 1, right, jnp.where(cm == 2, up, down)))
    else:
        prev_tok = jnp.where(row > 0, prev_tok, 0.0)
        next_tok = jnp.where(row < t - 1, next_tok, 0.0)
        shifted = jnp.where((lane & 1) == 0, prev_tok, next_tok)
    pm = cur + mu_ref[...] * (shifted - cur)

    r = pm[:, 0:D_RWKV]
    k = pm[:, D_RWKV:2 * D_RWKV]
    v = pm[:, 2 * D_RWKV:3 * D_RWKV]
    o = 3 * D_RWKV
    lw = pm[:, o:o + 2 * DECAY_LORA]
    la = pm[:, o + 2 * DECAY_LORA:o + 2 * (DECAY_LORA + AAA_LORA)]
    lg = pm[:, o + 2 * (DECAY_LORA + AAA_LORA):]

    w = w0_ref[...] + _dot_split(jnp.tanh(lw), w2_ref[0], w2_ref[1])
    log_decay = -EXP_NEG_HALF * _sigmoid(w)
    a = _sigmoid(a0_ref[...] + _dot_split(la, a2_ref[0], a2_ref[1]))
    gate = _dot_split(_sigmoid(lg), g2_ref[0], g2_ref[1])

    ones = ones_ref[...]
    kk_raw = k * kk_scale_ref[...]
    kk = kk_raw / jnp.maximum(jnp.sqrt(_segsum(kk_raw * kk_raw, ones)), 1e-12)
    ka = ka_ref[...]
    a0 = a[:, :D_RWKV]
    a1 = a[:, D_RWKV:]
    kd0 = k * (1.0 + (a0 - 1.0) * ka)
    kd1 = k * (1.0 + (a1 - 1.0) * ka)
    bonus = _segsum(r * (kd0 + kd1) * rk_ref[...], ones) * v

    r_ref[0] = r
    v_ref[0] = v
    kk_ref[0] = kk
    w_ref[0] = log_decay
    kd_ref[0] = jnp.concatenate([kd0, kd1], axis=-1)
    bb_ref[0] = jnp.concatenate([kk * a0, kk * a1], axis=-1)
    g_ref[0] = gate
    bonus_ref[0] = bonus


def _rwkv_prepare(pr, params, grid_shift):
    b, n, c = pr.shape
    t = 256
    if not grid_shift:
        assert n == t, "sequence token shift is written for a single tile"
    halo_blocks = n // GRID_W
    per_tile = t // GRID_W
    small = lambda shape: pl.BlockSpec(shape, lambda bi, i: (0,) * len(shape))
    tok_spec = lambda width: pl.BlockSpec((1, t, width), lambda bi, i: (bi, i, 0))
    out_widths = (D_RWKV, D_RWKV, D_RWKV, 2 * D_RWKV, 2 * D_RWKV, 2 * D_RWKV, D_RWKV, D_RWKV)
    kernel = functools.partial(_rwkv_prepare_kernel, grid_shift=grid_shift, n_tok=n)
    return pl.pallas_call(
        kernel,
        grid=(b, n // t),
        in_specs=[tok_spec(c),
                  pl.BlockSpec((1, GRID_W, c), lambda bi, i: (bi, jnp.maximum(i * per_tile - 1, 0), 0)),
                  pl.BlockSpec((1, GRID_W, c),
                               lambda bi, i: (bi, jnp.minimum((i + 1) * per_tile, halo_blocks - 1), 0)),
                  small((1, c)), small((1, 2 * D_RWKV)), small((2, 2 * DECAY_LORA, 2 * D_RWKV)),
                  small((1, 2 * D_RWKV)), small((2, 2 * AAA_LORA, 2 * D_RWKV)), small((2, GATE_LORA, D_RWKV)),
                  small((1, D_RWKV)), small((1, D_RWKV)), small((1, D_RWKV)), small((D_RWKV, D_RWKV))],
        out_specs=[tok_spec(wd) for wd in out_widths],
        out_shape=[jax.ShapeDtypeStruct((b, n, wd), F32) for wd in out_widths],
        compiler_params=_cparams(("arbitrary", "arbitrary")),
        name="rwkv_prepare",
    )(pr, pr, pr, *params)


def _bdot(a, b):
    return jnp.dot(a.astype(BF16), b.astype(BF16), preferred_element_type=F32)


def _bdot_nt(a, b):
    return lax.dot_general(a.astype(BF16), b.astype(BF16), (((1,), (1,)), ((), ())), preferred_element_type=F32)


def _bdot_tn(a, b):
    return lax.dot_general(a.astype(BF16), b.astype(BF16), (((0,), (0,)), ((), ())), preferred_element_type=F32)


def _wkv7_chunk_kernel(*refs, n_ctx_chunks):
    c = WKV_CHUNK
    p = 2 * c
    n_in = 12
    in_refs = (refs[:n_in], refs[n_in:2 * n_in])
    y_refs = refs[2 * n_in:2 * n_in + 2]
    state_ref = refs[2 * n_in + 2]
    n = pl.program_id(0)
    n_batch = in_refs[0][0].shape[0]
    pairs_per_batch = RWKV_HEADS // 2
    pairs_per_dir = n_batch * pairs_per_batch

    @pl.when(n == 0)
    def _():
        state_ref[...] = jnp.zeros_like(state_ref)

    is_ctx = n < n_ctx_chunks
    ti = lax.broadcasted_iota(jnp.int32, (c, c), 0)
    tj = lax.broadcasted_iota(jnp.int32, (c, c), 1)
    ri = lax.broadcasted_iota(jnp.int32, (p, p), 0)
    ci = lax.broadcasted_iota(jnp.int32, (p, p), 1)
    same_head = (ri >= c) == (ci >= c)
    ii = ri & (c - 1)
    jj = ci & (c - 1)
    eye = (ri == ci).astype(F32)
    first = lax.broadcasted_iota(jnp.int32, (c, p), 1) < RWKV_HEAD

    def stack(x):
        return jnp.concatenate([jnp.where(first, x, 0.0), jnp.where(first, 0.0, x)], axis=0)

    def unstack(x):
        return x[:c] + x[c:]

    a_st, r_st, k_st, b_st, k2_st, b2_st, v_st, g_chunk, earlier, upto_self = ([] for _ in range(10))
    for d in range(2):
        r_l, v_l, kk_l, lw_l, kd_l, bb_l, r_c, v_c, kk_c, lw_c, kd_c, bb_c = in_refs[d]
        pick = lambda xc, xl: jnp.concatenate(
            [jnp.where(is_ctx, xc[bi], xl[bi]) for bi in range(n_batch)], axis=-1)
        r, v, kk, lw, kd, bb = (pick(r_c, r_l), pick(v_c, v_l), pick(kk_c, kk_l), pick(lw_c, lw_l),
                                pick(kd_c, kd_l), pick(bb_c, bb_l))
        before = (tj < ti) if d == 0 else (tj > ti)
        upto = (before | (ti == tj)).astype(BF16)
        hi = lw.astype(BF16)
        r1 = lw - hi.astype(F32)
        mid = r1.astype(BF16)
        lo = (r1 - mid.astype(F32)).astype(BF16)
        cum = (jnp.dot(upto, hi, preferred_element_type=F32) + jnp.dot(upto, mid, preferred_element_type=F32)
               + jnp.dot(upto, lo, preferred_element_type=F32))
        tot = jnp.sum(lw, axis=0, keepdims=True)
        e_neg = jnp.exp(-cum)
        e_rem = jnp.exp(tot - cum)
        alpha = kk * jnp.exp(cum - lw)
        rho = r * jnp.exp(cum)
        beta = bb * e_neg
        kappa = kd * e_neg
        kappa_rem = kd * e_rem
        beta_rem = bb * e_rem
        g_all = jnp.exp(tot)
        pair_before = same_head & ((jj < ii) if d == 0 else (jj > ii))
        pair_upto = pair_before | (ri == ci)
        for hp in range(pairs_per_dir):
            sl = slice(hp * p, (hp + 1) * p)
            a_st.append(stack(alpha[:, sl]))
            r_st.append(stack(rho[:, sl]))
            k_st.append(stack(kappa[:, sl]))
            b_st.append(stack(beta[:, sl]))
            k2_st.append(stack(kappa_rem[:, sl]))
            b2_st.append(stack(beta_rem[:, sl]))
            v_st.append(stack(v[:, sl]))
            g_chunk.append(g_all[:, sl])
            earlier.append(pair_before)
            upto_self.append(pair_upto)

    pairs = range(2 * pairs_per_dir)
    g = [_bdot_nt(jnp.concatenate([a_st[h], r_st[h]], axis=0), jnp.concatenate([k_st[h], b_st[h]], axis=0))
         for h in pairs]
    m1 = [jnp.where(earlier[h], g[h][:p, :p], 0.0) for h in pairs]
    m2 = [jnp.where(earlier[h], g[h][:p, p:], 0.0) for h in pairs]
    n1 = [jnp.where(upto_self[h], g[h][p:, :p], 0.0) for h in pairs]
    n2 = [jnp.where(upto_self[h], g[h][p:, p:], 0.0) for h in pairs]

    in_block = (ii >> 3) == (jj >> 3)
    pw = [-jnp.where(in_block, m2[h], 0.0) for h in pairs]
    inv = [eye + pw[h] for h in pairs]
    pw = [_bdot(pw[h], pw[h]) for h in pairs]
    both = [_bdot(jnp.concatenate([inv[h], pw[h]], axis=0), pw[h]) for h in pairs]
    inv = [inv[h] + both[h][:p] for h in pairs]
    inv = [inv[h] + _bdot(inv[h], both[h][p:]) for h in pairs]
    for sh in (3, 4, 5):
        off = ((ii >> (sh + 1)) == (jj >> (sh + 1))) & ((ii >> sh) != (jj >> sh))
        left = [_bdot(inv[h], jnp.where(off, m2[h], 0.0)) for h in pairs]
        inv = [inv[h] - _bdot(left[h], inv[h]) for h in pairs]

    mnv = [_bdot(jnp.concatenate([m1[h], n1[h]], axis=0), v_st[h]) for h in pairs]
    m1v = [mnv[h][:p] for h in pairs]
    n1v = [mnv[h][p:] for h in pairs]
    au = [_bdot(inv[h], jnp.concatenate([a_st[h], m1v[h]], axis=1)) for h in pairs]
    nn = [_bdot(n2[h], au[h]) for h in pairs]
    pc = [_bdot_tn(b2_st[h], au[h][:, :p]) for h in pairs]
    qc_t = [_bdot_tn(jnp.concatenate([v_st[h], -au[h][:, p:]], axis=0),
                     jnp.concatenate([k2_st[h], b2_st[h]], axis=0)) for h in pairs]
    s0 = [state_ref[h] for h in pairs]
    y = [_bdot_nt(unstack(r_st[h] - nn[h][:, :p]), s0[h]) + unstack(n1v[h] - nn[h][:, p:]) for h in pairs]
    s_dec = [_bdot_nt(s0[h], pc[h]) for h in pairs]
    for h in pairs:
        d, hp = divmod(h, pairs_per_dir)
        bi, hpb = divmod(hp, pairs_per_batch)
        y_refs[d][bi, :, hpb * p:(hpb + 1) * p] = y[h]
        state_ref[h] = s0[h] * g_chunk[h] - s_dec[h] + qc_t[h]


def _wkv7(lat, ctx, b, n_tok, n_ctx):
    c = WKV_CHUNK
    ncx = n_ctx // c
    nl = n_tok // c
    lat_idx = (lambda n: jnp.maximum(n - ncx, 0), lambda n: nl - 1 - jnp.maximum(n - ncx, 0))
    ctx_idx = (lambda n: jnp.minimum(n, ncx - 1), lambda n: ncx - 1 - jnp.minimum(n, ncx - 1))

    def specs(idx, d):
        shared = pl.BlockSpec((b, c, D_RWKV), lambda n: (0, idx(n), 0))
        per_dir = pl.BlockSpec((b, c, D_RWKV), lambda n: (0, idx(n), d))
        return [shared, shared, shared, per_dir, per_dir, per_dir]

    in_specs, args = [], []
    for d in range(2):
        in_specs += specs(lat_idx[d], d) + specs(ctx_idx[d], d)
        args += list(lat) + list(ctx)
    return pl.pallas_call(
        functools.partial(_wkv7_chunk_kernel, n_ctx_chunks=ncx),
        grid=(ncx + nl,),
        in_specs=in_specs,
        out_specs=[pl.BlockSpec((b, c, D_RWKV), lambda n, d=d: (0, lat_idx[d](n), 0)) for d in range(2)],
        out_shape=[jax.ShapeDtypeStruct((b, n_tok, D_RWKV), F32)] * 2,
        scratch_shapes=[pltpu.VMEM((2 * b * RWKV_HEADS // 2, 2 * RWKV_HEAD, 2 * RWKV_HEAD), F32)],
        compiler_params=_cparams(("arbitrary",)),
        name="wkv7_chunk",
    )(*args)


def _rope(z, cos_t, sin_t):
    lane = lax.broadcasted_iota(jnp.int32, z.shape, 1)
    half = RET_HEAD // 4
    partner = jnp.where((lane & (2 * half - 1)) < half, pltpu.roll(z, RET_HEAD - half, 1), pltpu.roll(z, half, 1))
    return z * cos_t + partner * sin_t


def _retention_kernel(dec_ref, fwd_ref, bwd_ref, ctx_ref, cosf_ref, sinf_ref, cosb_ref, sinb_ref,
                      yf_ref, yb_ref, state_ref, dmat_ref, tail_ref, head_ref, cdec_ref):
    c = RET_CHUNK
    scale = RET_HEAD ** -0.5
    ii = lax.broadcasted_iota(jnp.int32, (c, c), 0)
    jj = lax.broadcasted_iota(jnp.int32, (c, c), 1)
    pos = lax.broadcasted_iota(jnp.int32, (c, RET_HEAD), 0).astype(F32)
    n_ctx_chunks = ctx_ref.shape[1] // c

    def head_slices(ref_val, h):
        q = ref_val[:, h * RET_HEAD:(h + 1) * RET_HEAD]
        k = ref_val[:, D_RET + h * RET_HEAD:D_RET + (h + 1) * RET_HEAD]
        v = ref_val[:, 2 * D_RET + h * RET_HEAD:2 * D_RET + (h + 1) * RET_HEAD]
        return q, k, v

    n_batch = fwd_ref.shape[0]
    heads = [(d, h) for d in range(2) for h in range(RET_HEADS)]
    chains = [(bi, d, h) for bi in range(n_batch) for d, h in heads]

    @pl.when(pl.program_id(0) == 0)
    def _():
        for d, h in heads:
            x = jnp.full((1, RET_HEAD), dec_ref[d, h], F32)
            lg = -(jnp.maximum(x, 0.0) + jnp.log(1.0 + jnp.exp(-jnp.abs(x))))
            chunk_decay = jnp.exp(lg * float(c))
            tail = jnp.exp(lg * ((c - 1.0 - pos) if d == 0 else pos))
            rel = (ii - jj) if d == 0 else (jj - ii)
            mask = (rel >= 0) if d == 0 else (rel > 0)
            dmat_ref[d, h] = jnp.where(mask, jnp.exp(lg * jnp.maximum(rel, 0).astype(F32)), 0.0)
            tail_ref[d, h] = tail
            head_ref[d, h] = jnp.exp(lg * ((pos + 1.0) if d == 0 else (c - pos)))
            cdec_ref[d, h] = jnp.broadcast_to(chunk_decay, (SUBLANES, RET_HEAD))
            order = range(n_ctx_chunks) if d == 0 else range(n_ctx_chunks - 1, -1, -1)
            for bi in range(n_batch):
                s = jnp.zeros((RET_HEAD, RET_HEAD), F32)
                for cc in order:
                    _, kc, vc = head_slices(ctx_ref[bi, cc * c:(cc + 1) * c, :], h)
                    s = s * chunk_decay + _bdot_tn(kc * scale * tail, vc)
                state_ref[bi, d, h] = s

    qkv = []
    for bi, d, h in chains:
        blk = fwd_ref[bi] if d == 0 else bwd_ref[bi]
        cos_t = cosf_ref[...] if d == 0 else cosb_ref[...]
        sin_t = sinf_ref[...] if d == 0 else sinb_ref[...]
        q, k, v = head_slices(blk, h)
        qkv.append((_rope(q, cos_t, sin_t), _rope(k, cos_t, sin_t) * scale, v.astype(BF16)))
    s0 = [state_ref[bi, d, h] for bi, d, h in chains]
    scores = [_bdot_nt(q, k) for q, k, _ in qkv]
    inner = [_bdot(scores[i] * dmat_ref[d, h], qkv[i][2]) for i, (_, d, h) in enumerate(chains)]
    cross = [_bdot(qkv[i][0] * head_ref[d, h], s0[i]) for i, (_, d, h) in enumerate(chains)]
    upd = [_bdot_tn(qkv[i][1] * tail_ref[d, h], qkv[i][2]) for i, (_, d, h) in enumerate(chains)]
    for i, (bi, d, h) in enumerate(chains):
        state_ref[bi, d, h] = s0[i] * cdec_ref[d, h, 0:1, :] + upd[i]
        out_ref = yf_ref if d == 0 else yb_ref
        out_ref[bi, :, h * RET_HEAD:(h + 1) * RET_HEAD] = inner[i] + cross[i]


def _retention(pt, pt_ctx, ret_decay, cos_t, sin_t):
    b, n, _ = pt.shape
    c = RET_CHUNK
    nc = n // c
    qkv = 3 * D_RET
    fwd = lambda i: (0, i, 0)
    bwd = lambda i: (0, nc - 1 - i, 0)
    return pl.pallas_call(
        _retention_kernel,
        grid=(nc,),
        in_specs=[pl.BlockSpec(memory_space=pltpu.SMEM),
                  pl.BlockSpec((b, c, qkv), fwd),
                  pl.BlockSpec((b, c, qkv), bwd),
                  pl.BlockSpec((b, pt_ctx.shape[1], qkv), lambda i: (0, 0, 0)),
                  pl.BlockSpec((c, RET_HEAD), lambda i: (i, 0)),
                  pl.BlockSpec((c, RET_HEAD), lambda i: (i, 0)),
                  pl.BlockSpec((c, RET_HEAD), lambda i: (nc - 1 - i, 0)),
                  pl.BlockSpec((c, RET_HEAD), lambda i: (nc - 1 - i, 0))],
        out_specs=[pl.BlockSpec((b, c, D_RET), fwd), pl.BlockSpec((b, c, D_RET), bwd)],
        out_shape=[jax.ShapeDtypeStruct((b, n, D_RET), F32), jax.ShapeDtypeStruct((b, n, D_RET), F32)],
        scratch_shapes=[pltpu.VMEM((b, 2, RET_HEADS, RET_HEAD, RET_HEAD), F32),
                        pltpu.VMEM((2, RET_HEADS, c, c), F32),
                        pltpu.VMEM((2, RET_HEADS, c, RET_HEAD), F32),
                        pltpu.VMEM((2, RET_HEADS, c, RET_HEAD), F32),
                        pltpu.VMEM((2, RET_HEADS, SUBLANES, RET_HEAD), F32)],
        compiler_params=_cparams(("arbitrary",)),
        name="retention",
    )(ret_decay, pt, pt, pt_ctx, cos_t, sin_t, cos_t, sin_t)


def _rope_tables(n_tok):
    nf = RET_HEAD // 4
    lane = np.arange(RET_HEAD)
    inv = ROPE_BASE ** (-jnp.arange(nf, dtype=F32) / nf)
    t = jnp.arange(n_tok)
    pos = jnp.where((lane // (2 * nf) == 0)[None, :], (t // GRID_W)[:, None], (t % GRID_W)[:, None]).astype(F32)
    ang = pos * inv[lane % nf][None, :]
    sign = jnp.where((lane % (2 * nf)) < nf, -1.0, 1.0).astype(F32)
    return jnp.cos(ang), jnp.sin(ang) * sign[None, :]


def _group_norm(y, ones, seg, eps, g, b):
    mu = _segsum(y, ones) * (1.0 / seg)
    yc = y - mu
    var = _segsum(yc * yc, ones) * (1.0 / seg)
    return yc * lax.rsqrt(var + eps) * g + b


def _out_proj_kernel(x_ref, yf_ref, yb_ref, bonus_ref, gate_ref, tf_ref, tb_ref, gt_ref,
                     embg_ref, embb_ref, g1_ref, s2_ref, sh2_ref, rgn_g_ref, rgn_b_ref, tgn_g_ref, tgn_b_ref,
                     ones_r_ref, wout_ref, ln1g_ref, ln1b_ref, wrh_ref, wrl_ref, br_ref,
                     h1_ref, u2_ref, logit_ref):
    y = yf_ref[0] + yb_ref[0]
    o_rwkv = _group_norm(y, ones_r_ref[...], RWKV_HEAD, RWKV_GN_EPS, rgn_g_ref[...], rgn_b_ref[...])
    o_rwkv = (o_rwkv + bonus_ref[0]) * gate_ref[0]
    yt = tf_ref[0] + tb_ref[0]
    gt = gt_ref[0]
    tgn_g = tgn_g_ref[...]
    tgn_b = tgn_b_ref[...]
    o_ret = jnp.concatenate(
        [_layer_norm(yt[:, h * RET_HEAD:(h + 1) * RET_HEAD], tgn_g[:, h * RET_HEAD:(h + 1) * RET_HEAD],
                     tgn_b[:, h * RET_HEAD:(h + 1) * RET_HEAD], RET_GN_EPS) for h in range(RET_HEADS)], axis=-1)
    o_ret = o_ret * (gt * _sigmoid(gt))
    cat = jnp.concatenate([o_rwkv, o_ret], axis=-1).astype(BF16)
    mix = jnp.dot(cat, wout_ref[...], preferred_element_type=F32)
    h = _layer_norm(x_ref[0], embg_ref[...], embb_ref[...])
    h1 = _layer_norm(DEEPNORM_ALPHA * h + g1_ref[0] * mix, ln1g_ref[...], ln1b_ref[...])
    u2 = h1 * (1.0 + s2_ref[0]) + sh2_ref[0]
    h1_ref[0] = h1
    _store_token_slabs(u2_ref, _pack_bf16_pairs(u2))
    logit_ref[0] = _dot_split(u2, wrh_ref[...], wrl_ref[...]) + br_ref[...]


def _out_proj(x, y_f, y_b, bonus, gate, t_f, t_b, pt, vecs, mats):
    b, n, d = x.shape
    t = 256
    tok = lambda width: pl.BlockSpec((1, t, width), lambda bi, i: (bi, i, 0))
    per_b = pl.BlockSpec((1, 1, d), lambda bi, i: (bi, 0, 0))
    small = lambda arr: pl.BlockSpec(arr.shape, lambda bi, i: (0,) * arr.ndim)
    (embg, embb, g1, s2, sh2, rgn_g, rgn_b, tgn_g, tgn_b, ln1g, ln1b, br) = vecs
    (ones_r, wout, wr_hi, wr_lo) = mats
    gt_spec = pl.BlockSpec((1, t, D_RET), lambda bi, i: (bi, i, 3))
    args = (x, y_f, y_b, bonus, gate, t_f, t_b, pt, embg, embb, g1, s2, sh2, rgn_g, rgn_b, tgn_g, tgn_b,
            ones_r, wout, ln1g, ln1b, wr_hi, wr_lo, br)
    in_specs = [tok(d)] + [tok(D_RWKV)] * 6 + [gt_spec, small(embg), small(embb), per_b, per_b, per_b,
                                                small(rgn_g), small(rgn_b), small(tgn_g), small(tgn_b),
                                                small(ones_r), small(wout), small(ln1g),
                                                small(ln1b), small(wr_hi), small(wr_lo), small(br)]
    return pl.pallas_call(
        _out_proj_kernel,
        grid=(b, n // t),
        in_specs=in_specs,
        out_specs=[tok(d), pl.BlockSpec((t * TOKEN_SLAB, LANES), lambda bi, i: (bi * (n // t) + i, 0)), tok(LANES)],
        out_shape=[jax.ShapeDtypeStruct((b, n, d), F32),
                   jax.ShapeDtypeStruct((b * n * TOKEN_SLAB, LANES), jnp.uint32),
                   jax.ShapeDtypeStruct((b, n, LANES), F32)],
        compiler_params=_cparams(("arbitrary", "arbitrary")),
        name="out_proj",
    )(*args)


ROUTE_E1, ROUTE_E2, ROUTE_G1, ROUTE_G2, ROUTE_RANK1, ROUTE_RANK2 = range(6)


def _lane_argmax(x, valid, lane):
    m = jnp.max(jnp.where(valid, x, -jnp.inf), axis=-1, keepdims=True)
    idx = jnp.min(jnp.where(valid & (x == m), lane, float(LANES)), axis=-1, keepdims=True)
    return m, idx


def _route_kernel(logit_ref, route_ref, route_t_ref, count_ref, carry_ref):
    @pl.when(pl.program_id(0) == 0)
    def _():
        carry_ref[...] = jnp.zeros_like(carry_ref)

    lg = logit_ref[...]
    t = lg.shape[0]
    lane = lax.broadcasted_iota(jnp.int32, lg.shape, 1).astype(F32)
    gmask = lane < N_GROUPS
    gmax = jnp.max(jnp.where(gmask, lg, -jnp.inf), axis=-1, keepdims=True)
    gexp = jnp.where(gmask, jnp.exp(lg - gmax), 0.0)
    gp = gexp / jnp.sum(gexp, axis=-1, keepdims=True)
    g_w, g_i = _lane_argmax(gp, gmask, lane)

    lo = N_GROUPS + EXPERTS_PER_GROUP * g_i
    emask = (lane >= lo) & (lane < lo + EXPERTS_PER_GROUP)
    emax = jnp.max(jnp.where(emask, lg, -jnp.inf), axis=-1, keepdims=True)
    eexp = jnp.where(emask, jnp.exp(lg - emax), 0.0)
    ep = eexp / jnp.sum(eexp, axis=-1, keepdims=True)
    p1, i1 = _lane_argmax(ep, emask, lane)
    p2, i2 = _lane_argmax(ep, emask & (lane != i1), lane)
    denom = p1 + p2
    gate1 = g_w * p1 / denom
    gate2 = g_w * p2 / denom
    e1 = i1 - N_GROUPS
    e2 = i2 - N_GROUPS

    oh1 = (lane == e1).astype(F32)
    oh2 = (lane == e2).astype(F32)
    cnt = oh1 + oh2
    ri = lax.broadcasted_iota(jnp.int32, (t, t), 0)
    ci = lax.broadcasted_iota(jnp.int32, (t, t), 1)
    before = (ci < ri).astype(BF16)
    seen = jnp.dot(before, cnt.astype(BF16), preferred_element_type=F32) + carry_ref[0:1, :]
    rank1 = jnp.sum(oh1 * seen, axis=-1, keepdims=True)
    rank2 = jnp.sum(oh2 * seen, axis=-1, keepdims=True)
    carry_ref[0:1, :] = carry_ref[0:1, :] + jnp.sum(cnt, axis=0, keepdims=True)

    out = jnp.zeros(lg.shape, F32)
    for slot, val in ((ROUTE_E1, e1.astype(F32)), (ROUTE_E2, e2.astype(F32)), (ROUTE_G1, gate1),
                      (ROUTE_G2, gate2), (ROUTE_RANK1, rank1), (ROUTE_RANK2, rank2)):
        out = jnp.where(lane == slot, val, out)
    route_ref[...] = out
    route_t_ref[...] = out.T[:SUBLANES]
    count_ref[...] = carry_ref[...]


def _route(logits):
    n = logits.shape[0]
    t = 256
    return pl.pallas_call(
        _route_kernel,
        grid=(n // t,),
        in_specs=[pl.BlockSpec((t, LANES), lambda i: (i, 0))],
        out_specs=[pl.BlockSpec((t, LANES), lambda i: (i, 0)), pl.BlockSpec((SUBLANES, t), lambda i: (0, i)),
                   pl.BlockSpec((SUBLANES, LANES), lambda i: (0, 0))],
        out_shape=[jax.ShapeDtypeStruct((n, LANES), F32), jax.ShapeDtypeStruct((SUBLANES, n), F32),
                   jax.ShapeDtypeStruct((SUBLANES, LANES), F32)],
        scratch_shapes=[pltpu.VMEM((SUBLANES, LANES), F32)],
        compiler_params=_cparams(("arbitrary",)),
        name="route",
    )(logits)


def _tile_gather_copy(src_hbm, idx_ref, buf, sem, slot, r):
    src = src_hbm.at[pl.ds(pl.multiple_of(idx_ref[0, 0, r] * TOKEN_SLAB, TOKEN_SLAB), TOKEN_SLAB), :]
    dst = buf.at[slot, pl.ds(pl.multiple_of(r * TOKEN_SLAB, TOKEN_SLAB), TOKEN_SLAB), :]
    return pltpu.make_async_copy(src, dst, sem.at[slot])


def _start_tile_gather(src_hbm, idx_ref, buf, sem, slot, rows, priorities):
    def body(g, carry):
        for j in range(GATHER_UNROLL):
            copy = _tile_gather_copy(src_hbm, idx_ref, buf, sem, slot, g * GATHER_UNROLL + j)
            copy.start(priority=priorities[j % len(priorities)])
        return carry
    lax.fori_loop(0, rows // GATHER_UNROLL, body, 0)


def _wait_tile_gather(src_hbm, idx_ref, buf, sem, slot, rows):
    del idx_ref
    whole = src_hbm.at[pl.ds(0, rows * TOKEN_SLAB), :]
    pltpu.make_async_copy(whole, buf.at[slot], sem.at[slot]).wait()


def _expert_kernel(blk_expert_ref, n_used_ref, tok_ref, tok_next_ref, u_hbm, wg_ref, wu_ref, wd_ref, y_ref,
                   xbuf, sem, wg_s, wu_s, wd_s):
    i = pl.program_id(0)
    n_used = n_used_ref[0]
    slot = i % 2

    @pl.when(i == 0)
    def _():
        _start_tile_gather(u_hbm, tok_ref, xbuf, sem, 0, EXPERT_ROWS, GATHER_PRIORITIES)

    @pl.when(i + 1 < n_used)
    def _():
        _start_tile_gather(u_hbm, tok_next_ref, xbuf, sem, 1 - slot, EXPERT_ROWS, GATHER_PRIORITIES)

    @pl.when(i >= n_used)
    def _():
        y_ref[...] = jnp.zeros_like(y_ref)

    @pl.when(i < n_used)
    def _():
        @pl.when((i == 0) | (blk_expert_ref[i] != blk_expert_ref[jnp.maximum(i - 1, 0)]))
        def _():
            wg_s[...] = wg_ref[0].astype(BF16)
            wu_s[...] = wu_ref[0].astype(BF16)
            wd_s[...] = wd_ref[0].astype(BF16)

        _wait_tile_gather(u_hbm, tok_ref, xbuf, sem, slot, EXPERT_ROWS)
        x = _unpack_bf16_pairs(_load_token_slabs(xbuf.at[slot], EXPERT_ROWS)).astype(BF16)
        hg = jnp.dot(x, wg_s[...], preferred_element_type=F32)
        hu = jnp.dot(x, wu_s[...], preferred_element_type=F32)
        act = (hg * _sigmoid(hg) * hu).astype(BF16)
        _store_token_slabs(y_ref, _pack_bf16_pairs(jnp.dot(act, wd_s[...], preferred_element_type=F32)))


def _expert_mlp(u2_tiles, slot_tok, block_expert, n_used, w_gate, w_up, w_down):
    n_blk = block_expert.shape[0]
    d = w_gate.shape[1]
    hdim = w_gate.shape[2]
    rows = EXPERT_ROWS
    tok3 = slot_tok.reshape(n_blk, 1, rows)
    used = lambda i, nu: jnp.minimum(i, nu[0] - 1)
    weight = lambda i, be, nu: (be[used(i, nu)], 0, 0)
    grid_spec = pltpu.PrefetchScalarGridSpec(
        num_scalar_prefetch=2,
        grid=(n_blk,),
        in_specs=[pl.BlockSpec((1, 1, rows), lambda i, be, nu: (used(i, nu), 0, 0), memory_space=pltpu.SMEM),
                  pl.BlockSpec((1, 1, rows), lambda i, be, nu: (used(i + 1, nu), 0, 0), memory_space=pltpu.SMEM),
                  pl.BlockSpec(memory_space=pl.ANY),
                  pl.BlockSpec((1, d, hdim), weight),
                  pl.BlockSpec((1, d, hdim), weight),
                  pl.BlockSpec((1, hdim, d), weight)],
        out_specs=pl.BlockSpec((rows * TOKEN_SLAB, LANES), lambda i, be, nu: (i, 0)),
        scratch_shapes=[pltpu.VMEM((2, rows * TOKEN_SLAB, LANES), jnp.uint32), pltpu.SemaphoreType.DMA((2,)),
                        pltpu.VMEM((d, hdim), BF16), pltpu.VMEM((d, hdim), BF16), pltpu.VMEM((hdim, d), BF16)],
    )
    return pl.pallas_call(
        _expert_kernel,
        grid_spec=grid_spec,
        out_shape=jax.ShapeDtypeStruct((n_blk * rows * TOKEN_SLAB, LANES), jnp.uint32),
        compiler_params=_cparams(("arbitrary",)),
        name="expert_mlp",
    )(block_expert, n_used, tok3, tok3, u2_tiles, w_gate, w_up, w_down)


def _combine_kernel(d1_ref, d2_ref, d1n_ref, d2n_ref, y_hbm, route_ref, h1_ref, g2_ref, lng_ref, lnb_ref,
                    o_ref, abuf, bbuf, sem_a, sem_b):
    i = pl.program_id(0)
    n = pl.num_programs(0)
    slot = i % 2
    rows = o_ref.shape[0]

    @pl.when(i == 0)
    def _():
        _start_tile_gather(y_hbm, d1_ref, abuf, sem_a, 0, rows, GATHER_PRIORITIES)
        _start_tile_gather(y_hbm, d2_ref, bbuf, sem_b, 0, rows, GATHER_PRIORITIES)

    @pl.when(i + 1 < n)
    def _():
        _start_tile_gather(y_hbm, d1n_ref, abuf, sem_a, 1 - slot, rows, GATHER_PRIORITIES)
        _start_tile_gather(y_hbm, d2n_ref, bbuf, sem_b, 1 - slot, rows, GATHER_PRIORITIES)

    _wait_tile_gather(y_hbm, d1_ref, abuf, sem_a, slot, rows)
    _wait_tile_gather(y_hbm, d2_ref, bbuf, sem_b, slot, rows)
    route = route_ref[...]
    f = (_unpack_bf16_pairs(_load_token_slabs(abuf.at[slot], rows)) * route[:, ROUTE_G1:ROUTE_G1 + 1]
         + _unpack_bf16_pairs(_load_token_slabs(bbuf.at[slot], rows)) * route[:, ROUTE_G2:ROUTE_G2 + 1])
    o_ref[...] = _layer_norm(DEEPNORM_ALPHA * h1_ref[...] + g2_ref[0] * f, lng_ref[...], lnb_ref[...])


def _combine(y_tiles, dest1, dest2, route, h1, g2, ln_g, ln_b, tokens_per_batch):
    n, d = h1.shape
    t = COMBINE_ROWS
    nt = n // t
    per_b = tokens_per_batch // t
    d1 = dest1.reshape(nt, 1, t)
    d2 = dest2.reshape(nt, 1, t)
    cur = pl.BlockSpec((1, 1, t), lambda i: (i, 0, 0), memory_space=pltpu.SMEM)
    nxt = pl.BlockSpec((1, 1, t), lambda i: (jnp.minimum(i + 1, nt - 1), 0, 0), memory_space=pltpu.SMEM)
    small = lambda arr: pl.BlockSpec(arr.shape, lambda i: (0,) * arr.ndim)
    return pl.pallas_call(
        _combine_kernel,
        grid=(nt,),
        in_specs=[cur, cur, nxt, nxt, pl.BlockSpec(memory_space=pl.ANY),
                  pl.BlockSpec((t, LANES), lambda i: (i, 0)),
                  pl.BlockSpec((t, d), lambda i: (i, 0)),
                  pl.BlockSpec((1, 1, d), lambda i: (i // per_b, 0, 0)),
                  small(ln_g), small(ln_b)],
        out_specs=pl.BlockSpec((t, d), lambda i: (i, 0)),
        out_shape=jax.ShapeDtypeStruct((n, d), F32),
        scratch_shapes=[pltpu.VMEM((2, t * TOKEN_SLAB, LANES), jnp.uint32),
                        pltpu.VMEM((2, t * TOKEN_SLAB, LANES), jnp.uint32),
                        pltpu.SemaphoreType.DMA((2,)), pltpu.SemaphoreType.DMA((2,))],
        compiler_params=_cparams(("arbitrary",)),
        name="combine",
    )(d1, d2, d1, d2, y_tiles, route, h1, g2, ln_g, ln_b)


def _hi_lo(w):
    hi = w.astype(BF16)
    return jnp.stack([hi, (w - hi.astype(F32)).astype(BF16)])


def _block_diag2(w):
    z = jnp.zeros_like(w[0])
    return jnp.concatenate([jnp.concatenate([w[0], z], axis=1), jnp.concatenate([z, w[1]], axis=1)], axis=0)


def kernel(x, c, ctx, c_ctx, emb_ln_g, emb_ln_b, w_mod, b_mod, w_in, tshift_mu, rwkv_w0, rwkv_w2, rwkv_a0, rwkv_a2, rwkv_g2, rwkv_k_k, rwkv_k_a, rwkv_r_k, rwkv_gn_g, rwkv_gn_b, ret_decay, ret_gn_g, ret_gn_b, w_out, ln1_g, ln1_b, router_group, router_group_bias, router_expert, router_expert_bias, expert_w_gate, expert_w_up, expert_w_down, ln2_g, ln2_b):
    assert w_mod.shape[0] == 1, "written for DEPTH == 1 (context outputs are never emitted)"
    b, n_tok, d = x.shape
    n_ctx = ctx.shape[1]
    row = lambda v: v.reshape(1, -1)

    c_rows = jnp.zeros((SUBLANES, d), F32).at[:b].set(c).at[b].set(c_ctx)
    mod = _modulation(c_rows, w_mod[0], row(b_mod[0]))
    sh1, s1, g1, sh2, s2, g2 = [mod[:b, j * d:(j + 1) * d].reshape(b, 1, d) for j in range(6)]
    sh1c, s1c = [jnp.broadcast_to(mod[b, j * d:(j + 1) * d].reshape(1, 1, d), (b, 1, d)) for j in range(2)]

    w_in_bf16 = w_in[0].astype(BF16)
    pr, pt = _in_proj(x, row(emb_ln_g), row(emb_ln_b), s1, sh1, w_in_bf16)
    pr_c, pt_c = _in_proj(ctx, row(emb_ln_g), row(emb_ln_b), s1c, sh1c, w_in_bf16)

    prep_params = (row(tshift_mu[0]), row(rwkv_w0[0]), _hi_lo(_block_diag2(rwkv_w2[0])), row(rwkv_a0[0]),
                   _hi_lo(_block_diag2(rwkv_a2[0])), _hi_lo(rwkv_g2[0]), row(rwkv_k_k[0]), row(rwkv_k_a[0]),
                   row(rwkv_r_k[0]),
                   _segment_ones(D_RWKV, RWKV_HEAD))
    lat = _rwkv_prepare(pr, prep_params, grid_shift=True)
    cx = _rwkv_prepare(pr_c, prep_params, grid_shift=False)
    r_l, v_l, kk_l, w_l, kd_l, bb_l, gate_l, bonus_l = lat
    r_c, v_c, kk_c, w_c, kd_c, bb_c, _, _ = cx

    y_f, y_b = _wkv7((r_l, v_l, kk_l, w_l, kd_l, bb_l), (r_c, v_c, kk_c, w_c, kd_c, bb_c), b, n_tok, n_ctx)

    cos_t, sin_t = _rope_tables(n_tok)
    t_f, t_b = _retention(pt, pt_c, ret_decay[0], cos_t, sin_t)

    wr = jnp.zeros((d, LANES), F32).at[:, :N_GROUPS].set(router_group[0])
    wr = wr.at[:, N_GROUPS:N_GROUPS + N_EXPERTS].set(router_expert[0])
    br = jnp.zeros((1, LANES), F32).at[0, :N_GROUPS].set(router_group_bias[0])
    br = br.at[0, N_GROUPS:N_GROUPS + N_EXPERTS].set(router_expert_bias[0].reshape(-1))
    vecs = (row(emb_ln_g), row(emb_ln_b), g1, s2, sh2, row(rwkv_gn_g[0]), row(rwkv_gn_b[0]),
            row(ret_gn_g[0]), row(ret_gn_b[0]), row(ln1_g[0]), row(ln1_b[0]), br)
    wr_hi_lo = _hi_lo(wr)
    mats = (_segment_ones(D_RWKV, RWKV_HEAD), w_out[0].astype(BF16), wr_hi_lo[0], wr_hi_lo[1])
    h1, u2, logits = _out_proj(x, y_f, y_b, bonus_l, gate_l, t_f, t_b, pt, vecs, mats)

    n_all = b * n_tok
    route, route_t, counts = _route(logits.reshape(n_all, LANES))

    e1 = route_t[ROUTE_E1].astype(jnp.int32)
    e2 = route_t[ROUTE_E2].astype(jnp.int32)
    cnt = counts[0, :N_EXPERTS].astype(jnp.int32)
    padded = ((cnt + EXPERT_ROWS - 1) // EXPERT_ROWS) * EXPERT_ROWS
    pends = jnp.cumsum(padded)
    pstarts = pends - padded
    expert_ids = jnp.arange(N_EXPERTS, dtype=jnp.int32)
    start_of = lambda e: jnp.sum(jnp.where(e[:, None] == expert_ids[None, :], pstarts[None, :], 0), axis=1)
    dest1 = start_of(e1) + route_t[ROUTE_RANK1].astype(jnp.int32)
    dest2 = start_of(e2) + route_t[ROUTE_RANK2].astype(jnp.int32)
    n_blk = -(-(n_all * 2) // EXPERT_ROWS) + N_EXPERTS
    tok_ids = jnp.arange(n_all, dtype=jnp.int32)
    slot_tok = jnp.zeros((n_blk * EXPERT_ROWS,), jnp.int32).at[jnp.concatenate([dest1, dest2])].set(
        jnp.concatenate([tok_ids, tok_ids]))
    block_start = jnp.arange(n_blk, dtype=jnp.int32) * EXPERT_ROWS
    block_expert = jnp.minimum(jnp.sum((block_start[:, None] >= pends[None, :]).astype(jnp.int32), axis=1),
                               N_EXPERTS - 1)

    n_used = (pends[N_EXPERTS - 1:] // EXPERT_ROWS).astype(jnp.int32)
    y_tiles = _expert_mlp(u2, slot_tok, block_expert, n_used,
                          expert_w_gate[0], expert_w_up[0], expert_w_down[0])
    out = _combine(y_tiles, dest1, dest2, route, h1.reshape(n_all, d), g2, row(ln2_g[0]), row(ln2_b[0]), n_tok)
    return out.reshape(b, n_tok, d)
```

```python
import functools
import math

import jax
import jax.numpy as jnp
import numpy as np
from jax import lax
from jax.experimental import pallas as pl
from jax.experimental.pallas import tpu as pltpu

F32 = jnp.float32
BF16 = jnp.bfloat16
HIGHEST = lax.Precision.HIGHEST

GRID_W = 64
D_RWKV = 512
RWKV_HEAD = 64
RWKV_HEADS = D_RWKV // RWKV_HEAD
DECAY_LORA = 64
AAA_LORA = 64
GATE_LORA = 128
D_RET = 512
RET_HEADS = 4
RET_HEAD = D_RET // RET_HEADS
RET_CHUNK = 128
RWKV_COLS = 3 * D_RWKV + 2 * (DECAY_LORA + AAA_LORA) + GATE_LORA
RET_COLS = 4 * D_RET
N_GROUPS = 4
EXPERTS_PER_GROUP = 8
N_EXPERTS = N_GROUPS * EXPERTS_PER_GROUP
EXPERT_HIDDEN = 512
MOE_BLOCK = 128
ROPE_BASE = 10000.0
LN_EPS = 1e-5
RWKV_GN_EPS = 64e-5
RET_GN_EPS = 1e-5
DEEPNORM_ALPHA = 2.0 ** 0.25
EXP_NEG_HALF = math.exp(-0.5)

LANES = 128
SUBLANES = 8
VMEM_LIMIT_BYTES = 56 * 1024 * 1024

WKV_CHUNK = 64


IN_PROJ_ROWS = 512
OUT_PROJ_ROWS = 512

TOKEN_SLAB = 4
EXPERT_ROWS = 256
COMBINE_ROWS = 128
GATHER_UNROLL = 8
GATHER_PRIORITIES = (0, 1)


def _pack_bf16_pairs(x):
    half = x.shape[1] // 2

    def bf16_bits(v):
        b = lax.bitcast_convert_type(v, jnp.uint32)
        return (b + jnp.uint32(0x7FFF) + ((b >> 16) & jnp.uint32(1))) >> 16

    return bf16_bits(x[:, :half]) | (bf16_bits(x[:, half:]) << 16)


def _unpack_bf16_pairs(p):
    lo = lax.bitcast_convert_type(p << 16, F32)
    hi = lax.bitcast_convert_type(p & jnp.uint32(0xFFFF0000), F32)
    return jnp.concatenate([lo, hi], axis=-1)


def _store_token_slabs(ref, x):
    rows = x.shape[0]
    for j in range(TOKEN_SLAB):
        ref[pl.ds(j, rows, stride=TOKEN_SLAB), :] = x[:, j * LANES:(j + 1) * LANES]


def _load_token_slabs(ref, rows):
    return jnp.concatenate([ref[pl.ds(j, rows, stride=TOKEN_SLAB), :] for j in range(TOKEN_SLAB)], axis=-1)


def _cparams(sem):
    return pltpu.CompilerParams(dimension_semantics=sem, vmem_limit_bytes=VMEM_LIMIT_BYTES)


def _layer_norm(x, g, b, eps=LN_EPS):
    mu = jnp.mean(x, axis=-1, keepdims=True)
    xc = x - mu
    var = jnp.mean(xc * xc, axis=-1, keepdims=True)
    return xc * lax.rsqrt(var + eps) * g + b


def _sigmoid(x):
    return 1.0 / (1.0 + jnp.exp(-x))


def _split_bf16(x):
    hi = x.astype(BF16)
    return hi, (x - hi.astype(F32)).astype(BF16)


def _segsum(x, ones_bf16):
    t = x.shape[0]
    s = jnp.dot(jnp.concatenate(_split_bf16(x), axis=0), ones_bf16, preferred_element_type=F32)
    return s[:t] + s[t:]


def _dot_split(x, w_hi, w_lo):
    hi, lo = _split_bf16(x)
    acc = jnp.dot(hi, w_hi, preferred_element_type=F32)
    acc = acc + jnp.dot(lo, w_hi, preferred_element_type=F32)
    return acc + jnp.dot(hi, w_lo, preferred_element_type=F32)


def _segment_ones(width, seg):
    idx = np.arange(width) // seg
    return jnp.asarray(idx[:, None] == idx[None, :], dtype=BF16)


def _mod_kernel(c_ref, w_ref, b_ref, o_ref):
    c = c_ref[...]
    sc = c * _sigmoid(c)
    o_ref[...] = jnp.dot(sc, w_ref[...], precision=HIGHEST, preferred_element_type=F32) + b_ref[...]


def _modulation(c_rows, w_mod, b_mod):
    rows, d = c_rows.shape
    n = w_mod.shape[1]
    tn = 1536
    return pl.pallas_call(
        _mod_kernel,
        grid=(n // tn,),
        in_specs=[pl.BlockSpec((rows, d), lambda j: (0, 0)),
                  pl.BlockSpec((d, tn), lambda j: (0, j)),
                  pl.BlockSpec((1, tn), lambda j: (0, j))],
        out_specs=pl.BlockSpec((rows, tn), lambda j: (0, j)),
        out_shape=jax.ShapeDtypeStruct((rows, n), F32),
        compiler_params=_cparams(("arbitrary",)),
        name="modulation",
    )(c_rows, w_mod, b_mod)


def _in_proj_kernel(x_ref, g_ref, b_ref, s_ref, sh_ref, w_ref, pr_ref, pt_ref):
    h = _layer_norm(x_ref[0], g_ref[...], b_ref[...])
    u = h * (1.0 + s_ref[0]) + sh_ref[0]
    p = jnp.dot(u.astype(BF16), w_ref[...], preferred_element_type=F32)
    pr_ref[0] = p[:, :RWKV_COLS]
    pt_ref[0] = p[:, RWKV_COLS:]


def _in_proj(x, ln_g, ln_b, s1, sh1, w_in_bf16):
    b, n, d = x.shape
    tm = min(IN_PROJ_ROWS, n)
    cols = w_in_bf16.shape[1]
    return pl.pallas_call(
        _in_proj_kernel,
        grid=(b, n // tm),
        in_specs=[pl.BlockSpec((1, tm, d), lambda bi, i: (bi, i, 0)),
                  pl.BlockSpec((1, d), lambda bi, i: (0, 0)),
                  pl.BlockSpec((1, d), lambda bi, i: (0, 0)),
                  pl.BlockSpec((1, 1, d), lambda bi, i: (bi, 0, 0)),
                  pl.BlockSpec((1, 1, d), lambda bi, i: (bi, 0, 0)),
                  pl.BlockSpec((d, cols), lambda bi, i: (0, 0))],
        out_specs=[pl.BlockSpec((1, tm, RWKV_COLS), lambda bi, i: (bi, i, 0)),
                   pl.BlockSpec((1, tm, RET_COLS), lambda bi, i: (bi, i, 0))],
        out_shape=[jax.ShapeDtypeStruct((b, n, RWKV_COLS), F32),
                   jax.ShapeDtypeStruct((b, n, RET_COLS), F32)],
        compiler_params=_cparams(("arbitrary", "arbitrary")),
        name="in_proj",
    )(x, ln_g, ln_b, s1, sh1, w_in_bf16)


def _rwkv_prepare_kernel(cur_ref, prev_ref, next_ref, mu_ref, w0_ref, w2_ref, a0_ref, a2_ref, g2_ref,
                         kk_scale_ref, ka_ref, rk_ref, ones_ref,
                         r_ref, v_ref, kk_ref, w_ref, kd_ref, bb_ref, g_ref, bonus_ref,
                         *, grid_shift, n_tok):
    cur = cur_ref[0]
    t, c = cur.shape
    row = lax.broadcasted_iota(jnp.int32, (t, c), 0)
    lane = lax.broadcasted_iota(jnp.int32, (t, c), 1)
    prev_tok = pltpu.roll(cur, 1, 0)
    next_tok = pltpu.roll(cur, t - 1, 0)
    if grid_shift:
        col = row & (GRID_W - 1)
        tok = row + pl.program_id(1) * t
        left = jnp.where(col > 0, prev_tok, 0.0)
        right = jnp.where(col < GRID_W - 1, next_tok, 0.0)
        up = jnp.where(tok >= GRID_W, jnp.concatenate([prev_ref[0], cur[:t - GRID_W]], axis=0), 0.0)
        down = jnp.where(tok < n_tok - GRID_W, jnp.concatenate([cur[GRID_W:], next_ref[0]], axis=0), 0.0)
        cm = lane & 3
        shifted = jnp.where(cm == 0, left, jnp.where(cm == 1, right, jnp.where(cm == 2, up, down)))
    else:
        prev_tok = jnp.where(row > 0, prev_tok, 0.0)
        next_tok = jnp.where(row < t - 1, next_tok, 0.0)
        shifted = jnp.where((lane & 1) == 0, prev_tok, next_tok)
    pm = cur + mu_ref[...] * (shifted - cur)

    r = pm[:, 0:D_RWKV]
    k = pm[:, D_RWKV:2 * D_RWKV]
    v = pm[:, 2 * D_RWKV:3 * D_RWKV]
    o = 3 * D_RWKV
    lw = pm[:, o:o + 2 * DECAY_LORA]
    la = pm[:, o + 2 * DECAY_LORA:o + 2 * (DECAY_LORA + AAA_LORA)]
    lg = pm[:, o + 2 * (DECAY_LORA + AAA_LORA):]

    w = w0_ref[...] + _dot_split(jnp.tanh(lw), w2_ref[0], w2_ref[1])
    log_decay = -EXP_NEG_HALF * _sigmoid(w)
    a = _sigmoid(a0_ref[...] + _dot_split(la, a2_ref[0], a2_ref[1]))
    gate = _dot_split(_sigmoid(lg), g2_ref[0], g2_ref[1])

    ones = ones_ref[...]
    kk_raw = k * kk_scale_ref[...]
    kk = kk_raw / jnp.maximum(jnp.sqrt(_segsum(kk_raw * kk_raw, ones)), 1e-12)
    ka = ka_ref[...]
    a0 = a[:, :D_RWKV]
    a1 = a[:, D_RWKV:]
    kd0 = k * (1.0 + (a0 - 1.0) * ka)
    kd1 = k * (1.0 + (a1 - 1.0) * ka)
    bonus = _segsum(r * (kd0 + kd1) * rk_ref[...], ones) * v

    r_ref[0] = r
    v_ref[0] = v
    kk_ref[0] = kk
    w_ref[0] = log_decay
    kd_ref[0] = jnp.concatenate([kd0, kd1], axis=-1)
    bb_ref[0] = jnp.concatenate([kk * a0, kk * a1], axis=-1)
    g_ref[0] = gate
    bonus_ref[0] = bonus


def _rwkv_prepare(pr, params, grid_shift):
    b, n, c = pr.shape
    t = 256
    if not grid_shift:
        assert n == t, "sequence token shift is written for a single tile"
    halo_blocks = n // GRID_W
    per_tile = t // GRID_W
    small = lambda shape: pl.BlockSpec(shape, lambda bi, i: (0,) * len(shape))
    tok_spec = lambda width: pl.BlockSpec((1, t, width), lambda bi, i: (bi, i, 0))
    out_widths = (D_RWKV, D_RWKV, D_RWKV, 2 * D_RWKV, 2 * D_RWKV, 2 * D_RWKV, D_RWKV, D_RWKV)
    kernel = functools.partial(_rwkv_prepare_kernel, grid_shift=grid_shift, n_tok=n)
    return pl.pallas_call(
        kernel,
        grid=(b, n // t),
        in_specs=[tok_spec(c),
                  pl.BlockSpec((1, GRID_W, c), lambda bi, i: (bi, jnp.maximum(i * per_tile - 1, 0), 0)),
                  pl.BlockSpec((1, GRID_W, c),
                               lambda bi, i: (bi, jnp.minimum((i + 1) * per_tile, halo_blocks - 1), 0)),
                  small((1, c)), small((1, 2 * D_RWKV)), small((2, 2 * DECAY_LORA, 2 * D_RWKV)),
                  small((1, 2 * D_RWKV)), small((2, 2 * AAA_LORA, 2 * D_RWKV)), small((2, GATE_LORA, D_RWKV)),
                  small((1, D_RWKV)), small((1, D_RWKV)), small((1, D_RWKV)), small((D_RWKV, D_RWKV))],
        out_specs=[tok_spec(wd) for wd in out_widths],
        out_shape=[jax.ShapeDtypeStruct((b, n, wd), F32) for wd in out_widths],
        compiler_params=_cparams(("arbitrary", "arbitrary")),
        name="rwkv_prepare",
    )(pr, pr, pr, *params)


def _bdot(a, b):
    return jnp.dot(a.astype(BF16), b.astype(BF16), preferred_element_type=F32)


def _bdot_nt(a, b):
    return lax.dot_general(a.astype(BF16), b.astype(BF16), (((1,), (1,)), ((), ())), preferred_element_type=F32)


def _bdot_tn(a, b):
    return lax.dot_general(a.astype(BF16), b.astype(BF16), (((0,), (0,)), ((), ())), preferred_element_type=F32)


def _wkv7_chunk_kernel(*refs, n_ctx_chunks):
    c = WKV_CHUNK
    p = 2 * c
    n_in = 12
    in_refs = (refs[:n_in], refs[n_in:2 * n_in])
    y_refs = refs[2 * n_in:2 * n_in + 2]
    state_ref = refs[2 * n_in + 2]
    n = pl.program_id(0)
    n_batch = in_refs[0][0].shape[0]
    pairs_per_batch = RWKV_HEADS // 2
    pairs_per_dir = n_batch * pairs_per_batch

    @pl.when(n == 0)
    def _():
        state_ref[...] = jnp.zeros_like(state_ref)

    is_ctx = n < n_ctx_chunks
    ti = lax.broadcasted_iota(jnp.int32, (c, c), 0)
    tj = lax.broadcasted_iota(jnp.int32, (c, c), 1)
    ri = lax.broadcasted_iota(jnp.int32, (p, p), 0)
    ci = lax.broadcasted_iota(jnp.int32, (p, p), 1)
    same_head = (ri >= c) == (ci >= c)
    ii = ri & (c - 1)
    jj = ci & (c - 1)
    eye = (ri == ci).astype(F32)
    first = lax.broadcasted_iota(jnp.int32, (c, p), 1) < RWKV_HEAD

    def stack(x):
        return jnp.concatenate([jnp.where(first, x, 0.0), jnp.where(first, 0.0, x)], axis=0)

    def unstack(x):
        return x[:c] + x[c:]

    a_st, r_st, k_st, b_st, k2_st, b2_st, v_st, g_chunk, earlier, upto_self = ([] for _ in range(10))
    for d in range(2):
        r_l, v_l, kk_l, lw_l, kd_l, bb_l, r_c, v_c, kk_c, lw_c, kd_c, bb_c = in_refs[d]
        pick = lambda xc, xl: jnp.concatenate(
            [jnp.where(is_ctx, xc[bi], xl[bi]) for bi in range(n_batch)], axis=-1)
        r, v, kk, lw, kd, bb = (pick(r_c, r_l), pick(v_c, v_l), pick(kk_c, kk_l), pick(lw_c, lw_l),
                                pick(kd_c, kd_l), pick(bb_c, bb_l))
        before = (tj < ti) if d == 0 else (tj > ti)
        upto = (before | (ti == tj)).astype(BF16)
        hi = lw.astype(BF16)
        r1 = lw - hi.astype(F32)
        mid = r1.astype(BF16)
        lo = (r1 - mid.astype(F32)).astype(BF16)
        cum = (jnp.dot(upto, hi, preferred_element_type=F32) + jnp.dot(upto, mid, preferred_element_type=F32)
               + jnp.dot(upto, lo, preferred_element_type=F32))
        tot = jnp.sum(lw, axis=0, keepdims=True)
        e_neg = jnp.exp(-cum)
        e_rem = jnp.exp(tot - cum)
        alpha = kk * jnp.exp(cum - lw)
        rho = r * jnp.exp(cum)
        beta = bb * e_neg
        kappa = kd * e_neg
        kappa_rem = kd * e_rem
        beta_rem = bb * e_rem
        g_all = jnp.exp(tot)
        pair_before = same_head & ((jj < ii) if d == 0 else (jj > ii))
        pair_upto = pair_before | (ri == ci)
        for hp in range(pairs_per_dir):
            sl = slice(hp * p, (hp + 1) * p)
            a_st.append(stack(alpha[:, sl]))
            r_st.append(stack(rho[:, sl]))
            k_st.append(stack(kappa[:, sl]))
            b_st.append(stack(beta[:, sl]))
            k2_st.append(stack(kappa_rem[:, sl]))
            b2_st.append(stack(beta_rem[:, sl]))
            v_st.append(stack(v[:, sl]))
            g_chunk.append(g_all[:, sl])
            earlier.append(pair_before)
            upto_self.append(pair_upto)

    pairs = range(2 * pairs_per_dir)
    g = [_bdot_nt(jnp.concatenate([a_st[h], r_st[h]], axis=0), jnp.concatenate([k_st[h], b_st[h]], axis=0))
         for h in pairs]
    m1 = [jnp.where(earlier[h], g[h][:p, :p], 0.0) for h in pairs]
    m2 = [jnp.where(earlier[h], g[h][:p, p:], 0.0) for h in pairs]
    n1 = [jnp.where(upto_self[h], g[h][p:, :p], 0.0) for h in pairs]
    n2 = [jnp.where(upto_self[h], g[h][p:, p:], 0.0) for h in pairs]

    in_block = (ii >> 3) == (jj >> 3)
    pw = [-jnp.where(in_block, m2[h], 0.0) for h in pairs]
    inv = [eye + pw[h] for h in pairs]
    pw = [_bdot(pw[h], pw[h]) for h in pairs]
    both = [_bdot(jnp.concatenate([inv[h], pw[h]], axis=0), pw[h]) for h in pairs]
    inv = [inv[h] + both[h][:p] for h in pairs]
    inv = [inv[h] + _bdot(inv[h], both[h][p:]) for h in pairs]
    for sh in (3, 4, 5):
        off = ((ii >> (sh + 1)) == (jj >> (sh + 1))) & ((ii >> sh) != (jj >> sh))
        left = [_bdot(inv[h], jnp.where(off, m2[h], 0.0)) for h in pairs]
        inv = [inv[h] - _bdot(left[h], inv[h]) for h in pairs]

    mnv = [_bdot(jnp.concatenate([m1[h], n1[h]], axis=0), v_st[h]) for h in pairs]
    m1v = [mnv[h][:p] for h in pairs]
    n1v = [mnv[h][p:] for h in pairs]
    au = [_bdot(inv[h], jnp.concatenate([a_st[h], m1v[h]], axis=1)) for h in pairs]
    nn = [_bdot(n2[h], au[h]) for h in pairs]
    pc = [_bdot_tn(b2_st[h], au[h][:, :p]) for h in pairs]
    qc_t = [_bdot_tn(jnp.concatenate([v_st[h], -au[h][:, p:]], axis=0),
                     jnp.concatenate([k2_st[h], b2_st[h]], axis=0)) for h in pairs]
    s0 = [state_ref[h] for h in pairs]
    y = [_bdot_nt(unstack(r_st[h] - nn[h][:, :p]), s0[h]) + unstack(n1v[h] - nn[h][:, p:]) for h in pairs]
    s_dec = [_bdot_nt(s0[h], pc[h]) for h in pairs]
    for h in pairs:
        d, hp = divmod(h, pairs_per_dir)
        bi, hpb = divmod(hp, pairs_per_batch)
        y_refs[d][bi, :, hpb * p:(hpb + 1) * p] = y[h]
        state_ref[h] = s0[h] * g_chunk[h] - s_dec[h] + qc_t[h]


def _wkv7(lat, ctx, b, n_tok, n_ctx):
    c = WKV_CHUNK
    ncx = n_ctx // c
    nl = n_tok // c
    lat_idx = (lambda n: jnp.maximum(n - ncx, 0), lambda n: nl - 1 - jnp.maximum(n - ncx, 0))
    ctx_idx = (lambda n: jnp.minimum(n, ncx - 1), lambda n: ncx - 1 - jnp.minimum(n, ncx - 1))

    def specs(idx, d):
        shared = pl.BlockSpec((b, c, D_RWKV), lambda n: (0, idx(n), 0))
        per_dir = pl.BlockSpec((b, c, D_RWKV), lambda n: (0, idx(n), d))
        return [shared, shared, shared, per_dir, per_dir, per_dir]

    in_specs, args = [], []
    for d in range(2):
        in_specs += specs(lat_idx[d], d) + specs(ctx_idx[d], d)
        args += list(lat) + list(ctx)
    return pl.pallas_call(
        functools.partial(_wkv7_chunk_kernel, n_ctx_chunks=ncx),
        grid=(ncx + nl,),
        in_specs=in_specs,
        out_specs=[pl.BlockSpec((b, c, D_RWKV), lambda n, d=d: (0, lat_idx[d](n), 0)) for d in range(2)],
        out_shape=[jax.ShapeDtypeStruct((b, n_tok, D_RWKV), F32)] * 2,
        scratch_shapes=[pltpu.VMEM((2 * b * RWKV_HEADS // 2, 2 * RWKV_HEAD, 2 * RWKV_HEAD), F32)],
        compiler_params=_cparams(("arbitrary",)),
        name="wkv7_chunk",
    )(*args)


def _rope(z, cos_t, sin_t):
    lane = lax.broadcasted_iota(jnp.int32, z.shape, 1)
    half = RET_HEAD // 4
    partner = jnp.where((lane & (2 * half - 1)) < half, pltpu.roll(z, RET_HEAD - half, 1), pltpu.roll(z, half, 1))
    return z * cos_t + partner * sin_t


def _retention_kernel(dec_ref, fwd_ref, bwd_ref, ctx_ref, cosf_ref, sinf_ref, cosb_ref, sinb_ref,
                      yf_ref, yb_ref, state_ref, dmat_ref, tail_ref, head_ref, cdec_ref):
    c = RET_CHUNK
    scale = RET_HEAD ** -0.5
    ii = lax.broadcasted_iota(jnp.int32, (c, c), 0)
    jj = lax.broadcasted_iota(jnp.int32, (c, c), 1)
    pos = lax.broadcasted_iota(jnp.int32, (c, RET_HEAD), 0).astype(F32)
    n_ctx_chunks = ctx_ref.shape[1] // c

    def head_slices(ref_val, h):
        q = ref_val[:, h * RET_HEAD:(h + 1) * RET_HEAD]
        k = ref_val[:, D_RET + h * RET_HEAD:D_RET + (h + 1) * RET_HEAD]
        v = ref_val[:, 2 * D_RET + h * RET_HEAD:2 * D_RET + (h + 1) * RET_HEAD]
        return q, k, v

    n_batch = fwd_ref.shape[0]
    heads = [(d, h) for d in range(2) for h in range(RET_HEADS)]
    chains = [(bi, d, h) for bi in range(n_batch) for d, h in heads]

    @pl.when(pl.program_id(0) == 0)
    def _():
        for d, h in heads:
            x = jnp.full((1, RET_HEAD), dec_ref[d, h], F32)
            lg = -(jnp.maximum(x, 0.0) + jnp.log(1.0 + jnp.exp(-jnp.abs(x))))
            chunk_decay = jnp.exp(lg * float(c))
            tail = jnp.exp(lg * ((c - 1.0 - pos) if d == 0 else pos))
            rel = (ii - jj) if d == 0 else (jj - ii)
            mask = (rel >= 0) if d == 0 else (rel > 0)
            dmat_ref[d, h] = jnp.where(mask, jnp.exp(lg * jnp.maximum(rel, 0).astype(F32)), 0.0)
            tail_ref[d, h] = tail
            head_ref[d, h] = jnp.exp(lg * ((pos + 1.0) if d == 0 else (c - pos)))
            cdec_ref[d, h] = jnp.broadcast_to(chunk_decay, (SUBLANES, RET_HEAD))
            order = range(n_ctx_chunks) if d == 0 else range(n_ctx_chunks - 1, -1, -1)
            for bi in range(n_batch):
                s = jnp.zeros((RET_HEAD, RET_HEAD), F32)
                for cc in order:
                    _, kc, vc = head_slices(ctx_ref[bi, cc * c:(cc + 1) * c, :], h)
                    s = s * chunk_decay + _bdot_tn(kc * scale * tail, vc)
                state_ref[bi, d, h] = s

    qkv = []
    for bi, d, h in chains:
        blk = fwd_ref[bi] if d == 0 else bwd_ref[bi]
        cos_t = cosf_ref[...] if d == 0 else cosb_ref[...]
        sin_t = sinf_ref[...] if d == 0 else sinb_ref[...]
        q, k, v = head_slices(blk, h)
        qkv.append((_rope(q, cos_t, sin_t), _rope(k, cos_t, sin_t) * scale, v.astype(BF16)))
    s0 = [state_ref[bi, d, h] for bi, d, h in chains]
    scores = [_bdot_nt(q, k) for q, k, _ in qkv]
    inner = [_bdot(scores[i] * dmat_ref[d, h], qkv[i][2]) for i, (_, d, h) in enumerate(chains)]
    cross = [_bdot(qkv[i][0] * head_ref[d, h], s0[i]) for i, (_, d, h) in enumerate(chains)]
    upd = [_bdot_tn(qkv[i][1] * tail_ref[d, h], qkv[i][2]) for i, (_, d, h) in enumerate(chains)]
    for i, (bi, d, h) in enumerate(chains):
        state_ref[bi, d, h] = s0[i] * cdec_ref[d, h, 0:1, :] + upd[i]
        out_ref = yf_ref if d == 0 else yb_ref
        out_ref[bi, :, h * RET_HEAD:(h + 1) * RET_HEAD] = inner[i] + cross[i]


def _retention(pt, pt_ctx, ret_decay, cos_t, sin_t):
    b, n, _ = pt.shape
    c = RET_CHUNK
    nc = n // c
    qkv = 3 * D_RET
    fwd = lambda i: (0, i, 0)
    bwd = lambda i: (0, nc - 1 - i, 0)
    return pl.pallas_call(
        _retention_kernel,
        grid=(nc,),
        in_specs=[pl.BlockSpec(memory_space=pltpu.SMEM),
                  pl.BlockSpec((b, c, qkv), fwd),
                  pl.BlockSpec((b, c, qkv), bwd),
                  pl.BlockSpec((b, pt_ctx.shape[1], qkv), lambda i: (0, 0, 0)),
                  pl.BlockSpec((c, RET_HEAD), lambda i: (i, 0)),
                  pl.BlockSpec((c, RET_HEAD), lambda i: (i, 0)),
                  pl.BlockSpec((c, RET_HEAD), lambda i: (nc - 1 - i, 0)),
                  pl.BlockSpec((c, RET_HEAD), lambda i: (nc - 1 - i, 0))],
        out_specs=[pl.BlockSpec((b, c, D_RET), fwd), pl.BlockSpec((b, c, D_RET), bwd)],
        out_shape=[jax.ShapeDtypeStruct((b, n, D_RET), F32), jax.ShapeDtypeStruct((b, n, D_RET), F32)],
        scratch_shapes=[pltpu.VMEM((b, 2, RET_HEADS, RET_HEAD, RET_HEAD), F32),
                        pltpu.VMEM((2, RET_HEADS, c, c), F32),
                        pltpu.VMEM((2, RET_HEADS, c, RET_HEAD), F32),
                        pltpu.VMEM((2, RET_HEADS, c, RET_HEAD), F32),
                        pltpu.VMEM((2, RET_HEADS, SUBLANES, RET_HEAD), F32)],
        compiler_params=_cparams(("arbitrary",)),
        name="retention",
    )(ret_decay, pt, pt, pt_ctx, cos_t, sin_t, cos_t, sin_t)


def _rope_tables(n_tok):
    nf = RET_HEAD // 4
    lane = np.arange(RET_HEAD)
    inv = ROPE_BASE ** (-jnp.arange(nf, dtype=F32) / nf)
    t = jnp.arange(n_tok)
    pos = jnp.where((lane // (2 * nf) == 0)[None, :], (t // GRID_W)[:, None], (t % GRID_W)[:, None]).astype(F32)
    ang = pos * inv[lane % nf][None, :]
    sign = jnp.where((lane % (2 * nf)) < nf, -1.0, 1.0).astype(F32)
    return jnp.cos(ang), jnp.sin(ang) * sign[None, :]


def _group_norm(y, ones, seg, eps, g, b):
    mu = _segsum(y, ones) * (1.0 / seg)
    yc = y - mu
    var = _segsum(yc * yc, ones) * (1.0 / seg)
    return yc * lax.rsqrt(var + eps) * g + b


def _out_proj_kernel(x_ref, yf_ref, yb_ref, bonus_ref, gate_ref, tf_ref, tb_ref, gt_ref,
                     embg_ref, embb_ref, g1_ref, s2_ref, sh2_ref, rgn_g_ref, rgn_b_ref, tgn_g_ref, tgn_b_ref,
                     ones_r_ref, wout_ref, ln1g_ref, ln1b_ref, wrh_ref, wrl_ref, br_ref,
                     h1_ref, u2_ref, route_ref, route_t_ref, count_ref, carry_ref):
    @pl.when((pl.program_id(0) == 0) & (pl.program_id(1) == 0))
    def _():
        carry_ref[...] = jnp.zeros_like(carry_ref)

    y = yf_ref[0] + yb_ref[0]
    o_rwkv = _group_norm(y, ones_r_ref[...], RWKV_HEAD, RWKV_GN_EPS, rgn_g_ref[...], rgn_b_ref[...])
    o_rwkv = (o_rwkv + bonus_ref[0]) * gate_ref[0]
    yt = tf_ref[0] + tb_ref[0]
    gt = gt_ref[0]
    tgn_g = tgn_g_ref[...]
    tgn_b = tgn_b_ref[...]
    o_ret = jnp.concatenate(
        [_layer_norm(yt[:, h * RET_HEAD:(h + 1) * RET_HEAD], tgn_g[:, h * RET_HEAD:(h + 1) * RET_HEAD],
                     tgn_b[:, h * RET_HEAD:(h + 1) * RET_HEAD], RET_GN_EPS) for h in range(RET_HEADS)], axis=-1)
    o_ret = o_ret * (gt * _sigmoid(gt))
    cat = jnp.concatenate([o_rwkv, o_ret], axis=-1).astype(BF16)
    mix = jnp.dot(cat, wout_ref[...], preferred_element_type=F32)
    h = _layer_norm(x_ref[0], embg_ref[...], embb_ref[...])
    h1 = _layer_norm(DEEPNORM_ALPHA * h + g1_ref[0] * mix, ln1g_ref[...], ln1b_ref[...])
    u2 = h1 * (1.0 + s2_ref[0]) + sh2_ref[0]
    h1_ref[0] = h1
    _store_token_slabs(u2_ref, _pack_bf16_pairs(u2))
    route = _route_tile(_dot_split(u2, wrh_ref[...], wrl_ref[...]) + br_ref[...], carry_ref)
    route_ref[...] = route
    route_t_ref[...] = route.T[:SUBLANES]
    count_ref[...] = carry_ref[...]


def _out_proj(x, y_f, y_b, bonus, gate, t_f, t_b, pt, vecs, mats):
    b, n, d = x.shape
    t = OUT_PROJ_ROWS
    tok = lambda width: pl.BlockSpec((1, t, width), lambda bi, i: (bi, i, 0))
    per_b = pl.BlockSpec((1, 1, d), lambda bi, i: (bi, 0, 0))
    small = lambda arr: pl.BlockSpec(arr.shape, lambda bi, i: (0,) * arr.ndim)
    (embg, embb, g1, s2, sh2, rgn_g, rgn_b, tgn_g, tgn_b, ln1g, ln1b, br) = vecs
    (ones_r, wout, wr_hi, wr_lo) = mats
    gt_spec = pl.BlockSpec((1, t, D_RET), lambda bi, i: (bi, i, 3))
    tiles = n // t
    flat = lambda bi, i: (bi * tiles + i, 0)
    args = (x, y_f, y_b, bonus, gate, t_f, t_b, pt, embg, embb, g1, s2, sh2, rgn_g, rgn_b, tgn_g, tgn_b,
            ones_r, wout, ln1g, ln1b, wr_hi, wr_lo, br)
    in_specs = [tok(d)] + [tok(D_RWKV)] * 6 + [gt_spec, small(embg), small(embb), per_b, per_b, per_b,
                                                small(rgn_g), small(rgn_b), small(tgn_g), small(tgn_b),
                                                small(ones_r), small(wout), small(ln1g),
                                                small(ln1b), small(wr_hi), small(wr_lo), small(br)]
    return pl.pallas_call(
        _out_proj_kernel,
        grid=(b, n // t),
        in_specs=in_specs,
        out_specs=[tok(d), pl.BlockSpec((t * TOKEN_SLAB, LANES), flat),
                   pl.BlockSpec((t, LANES), flat),
                   pl.BlockSpec((SUBLANES, t), lambda bi, i: (0, bi * tiles + i)),
                   pl.BlockSpec((SUBLANES, LANES), lambda bi, i: (0, 0))],
        out_shape=[jax.ShapeDtypeStruct((b, n, d), F32),
                   jax.ShapeDtypeStruct((b * n * TOKEN_SLAB, LANES), jnp.uint32),
                   jax.ShapeDtypeStruct((b * n, LANES), F32),
                   jax.ShapeDtypeStruct((SUBLANES, b * n), F32),
                   jax.ShapeDtypeStruct((SUBLANES, LANES), F32)],
        scratch_shapes=[pltpu.VMEM((SUBLANES, LANES), F32)],
        compiler_params=_cparams(("arbitrary", "arbitrary")),
        name="out_proj",
    )(*args)


ROUTE_E1, ROUTE_E2, ROUTE_G1, ROUTE_G2, ROUTE_RANK1, ROUTE_RANK2 = range(6)


def _lane_argmax(x, valid, lane):
    m = jnp.max(jnp.where(valid, x, -jnp.inf), axis=-1, keepdims=True)
    idx = jnp.min(jnp.where(valid & (x == m), lane, float(LANES)), axis=-1, keepdims=True)
    return m, idx


def _route_tile(lg, carry_ref):
    t = lg.shape[0]
    lane = lax.broadcasted_iota(jnp.int32, lg.shape, 1).astype(F32)
    gmask = lane < N_GROUPS
    gmax = jnp.max(jnp.where(gmask, lg, -jnp.inf), axis=-1, keepdims=True)
    gexp = jnp.where(gmask, jnp.exp(lg - gmax), 0.0)
    gp = gexp / jnp.sum(gexp, axis=-1, keepdims=True)
    g_w, g_i = _lane_argmax(gp, gmask, lane)

    lo = N_GROUPS + EXPERTS_PER_GROUP * g_i
    emask = (lane >= lo) & (lane < lo + EXPERTS_PER_GROUP)
    emax = jnp.max(jnp.where(emask, lg, -jnp.inf), axis=-1, keepdims=True)
    eexp = jnp.where(emask, jnp.exp(lg - emax), 0.0)
    ep = eexp / jnp.sum(eexp, axis=-1, keepdims=True)
    p1, i1 = _lane_argmax(ep, emask, lane)
    p2, i2 = _lane_argmax(ep, emask & (lane != i1), lane)
    denom = p1 + p2
    gate1 = g_w * p1 / denom
    gate2 = g_w * p2 / denom
    e1 = i1 - N_GROUPS
    e2 = i2 - N_GROUPS

    oh1 = (lane == e1).astype(F32)
    oh2 = (lane == e2).astype(F32)
    cnt = oh1 + oh2
    ri = lax.broadcasted_iota(jnp.int32, (t, t), 0)
    ci = lax.broadcasted_iota(jnp.int32, (t, t), 1)
    before = (ci < ri).astype(BF16)
    seen = jnp.dot(before, cnt.astype(BF16), preferred_element_type=F32) + carry_ref[0:1, :]
    rank1 = jnp.sum(oh1 * seen, axis=-1, keepdims=True)
    rank2 = jnp.sum(oh2 * seen, axis=-1, keepdims=True)
    carry_ref[0:1, :] = carry_ref[0:1, :] + jnp.sum(cnt, axis=0, keepdims=True)

    out = jnp.zeros(lg.shape, F32)
    for slot, val in ((ROUTE_E1, e1.astype(F32)), (ROUTE_E2, e2.astype(F32)), (ROUTE_G1, gate1),
                      (ROUTE_G2, gate2), (ROUTE_RANK1, rank1), (ROUTE_RANK2, rank2)):
        out = jnp.where(lane == slot, val, out)
    return out


def _tile_gather_copy(src_hbm, idx_ref, buf, sem, slot, r):
    src = src_hbm.at[pl.ds(pl.multiple_of(idx_ref[0, 0, r] * TOKEN_SLAB, TOKEN_SLAB), TOKEN_SLAB), :]
    dst = buf.at[slot, pl.ds(pl.multiple_of(r * TOKEN_SLAB, TOKEN_SLAB), TOKEN_SLAB), :]
    return pltpu.make_async_copy(src, dst, sem.at[slot])


def _start_tile_gather(src_hbm, idx_ref, buf, sem, slot, rows, priorities):
    def body(g, carry):
        for j in range(GATHER_UNROLL):
            copy = _tile_gather_copy(src_hbm, idx_ref, buf, sem, slot, g * GATHER_UNROLL + j)
            copy.start(priority=priorities[j % len(priorities)])
        return carry
    lax.fori_loop(0, rows // GATHER_UNROLL, body, 0)


def _wait_tile_gather(src_hbm, idx_ref, buf, sem, slot, rows):
    del idx_ref
    whole = src_hbm.at[pl.ds(0, rows * TOKEN_SLAB), :]
    pltpu.make_async_copy(whole, buf.at[slot], sem.at[slot]).wait()


def _expert_kernel(blk_expert_ref, n_used_ref, tok_ref, tok_next_ref, u_hbm, wg_ref, wu_ref, wd_ref, y_ref,
                   xbuf, sem, wg_s, wu_s, wd_s):
    i = pl.program_id(0)
    n_used = n_used_ref[0]
    slot = i % 2

    @pl.when(i == 0)
    def _():
        _start_tile_gather(u_hbm, tok_ref, xbuf, sem, 0, EXPERT_ROWS, GATHER_PRIORITIES)

    @pl.when(i + 1 < n_used)
    def _():
        _start_tile_gather(u_hbm, tok_next_ref, xbuf, sem, 1 - slot, EXPERT_ROWS, GATHER_PRIORITIES)

    @pl.when(i >= n_used)
    def _():
        y_ref[...] = jnp.zeros_like(y_ref)

    @pl.when(i < n_used)
    def _():
        @pl.when((i == 0) | (blk_expert_ref[i] != blk_expert_ref[jnp.maximum(i - 1, 0)]))
        def _():
            wg_s[...] = wg_ref[0].astype(BF16)
            wu_s[...] = wu_ref[0].astype(BF16)
            wd_s[...] = wd_ref[0].astype(BF16)

        _wait_tile_gather(u_hbm, tok_ref, xbuf, sem, slot, EXPERT_ROWS)
        x = _unpack_bf16_pairs(_load_token_slabs(xbuf.at[slot], EXPERT_ROWS)).astype(BF16)
        hg = jnp.dot(x, wg_s[...], preferred_element_type=F32)
        hu = jnp.dot(x, wu_s[...], preferred_element_type=F32)
        act = (hg * _sigmoid(hg) * hu).astype(BF16)
        _store_token_slabs(y_ref, _pack_bf16_pairs(jnp.dot(act, wd_s[...], preferred_element_type=F32)))


def _expert_mlp(u2_tiles, slot_tok, block_expert, n_used, w_gate, w_up, w_down):
    n_blk = block_expert.shape[0]
    d = w_gate.shape[1]
    hdim = w_gate.shape[2]
    rows = EXPERT_ROWS
    tok3 = slot_tok.reshape(n_blk, 1, rows)
    used = lambda i, nu: jnp.minimum(i, nu[0] - 1)
    weight = lambda i, be, nu: (be[used(i, nu)], 0, 0)
    grid_spec = pltpu.PrefetchScalarGridSpec(
        num_scalar_prefetch=2,
        grid=(n_blk,),
        in_specs=[pl.BlockSpec((1, 1, rows), lambda i, be, nu: (used(i, nu), 0, 0), memory_space=pltpu.SMEM),
                  pl.BlockSpec((1, 1, rows), lambda i, be, nu: (used(i + 1, nu), 0, 0), memory_space=pltpu.SMEM),
                  pl.BlockSpec(memory_space=pl.ANY),
                  pl.BlockSpec((1, d, hdim), weight),
                  pl.BlockSpec((1, d, hdim), weight),
                  pl.BlockSpec((1, hdim, d), weight)],
        out_specs=pl.BlockSpec((rows * TOKEN_SLAB, LANES), lambda i, be, nu: (i, 0)),
        scratch_shapes=[pltpu.VMEM((2, rows * TOKEN_SLAB, LANES), jnp.uint32), pltpu.SemaphoreType.DMA((2,)),
                        pltpu.VMEM((d, hdim), BF16), pltpu.VMEM((d, hdim), BF16), pltpu.VMEM((hdim, d), BF16)],
    )
    return pl.pallas_call(
        _expert_kernel,
        grid_spec=grid_spec,
        out_shape=jax.ShapeDtypeStruct((n_blk * rows * TOKEN_SLAB, LANES), jnp.uint32),
        compiler_params=_cparams(("arbitrary",)),
        name="expert_mlp",
    )(block_expert, n_used, tok3, tok3, u2_tiles, w_gate, w_up, w_down)


def _combine_kernel(d1_ref, d2_ref, d1n_ref, d2n_ref, y_hbm, route_ref, h1_ref, g2_ref, lng_ref, lnb_ref,
                    o_ref, abuf, bbuf, sem_a, sem_b):
    i = pl.program_id(0)
    n = pl.num_programs(0)
    slot = i % 2
    rows = o_ref.shape[0]

    @pl.when(i == 0)
    def _():
        _start_tile_gather(y_hbm, d1_ref, abuf, sem_a, 0, rows, GATHER_PRIORITIES)
        _start_tile_gather(y_hbm, d2_ref, bbuf, sem_b, 0, rows, GATHER_PRIORITIES)

    @pl.when(i + 1 < n)
    def _():
        _start_tile_gather(y_hbm, d1n_ref, abuf, sem_a, 1 - slot, rows, GATHER_PRIORITIES)
        _start_tile_gather(y_hbm, d2n_ref, bbuf, sem_b, 1 - slot, rows, GATHER_PRIORITIES)

    _wait_tile_gather(y_hbm, d1_ref, abuf, sem_a, slot, rows)
    _wait_tile_gather(y_hbm, d2_ref, bbuf, sem_b, slot, rows)
    route = route_ref[...]
    f = (_unpack_bf16_pairs(_load_token_slabs(abuf.at[slot], rows)) * route[:, ROUTE_G1:ROUTE_G1 + 1]
         + _unpack_bf16_pairs(_load_token_slabs(bbuf.at[slot], rows)) * route[:, ROUTE_G2:ROUTE_G2 + 1])
    o_ref[...] = _layer_norm(DEEPNORM_ALPHA * h1_ref[...] + g2_ref[0] * f, lng_ref[...], lnb_ref[...])


def _combine(y_tiles, dest1, dest2, route, h1, g2, ln_g, ln_b, tokens_per_batch):
    n, d = h1.shape
    t = COMBINE_ROWS
    nt = n // t
    per_b = tokens_per_batch // t
    d1 = dest1.reshape(nt, 1, t)
    d2 = dest2.reshape(nt, 1, t)
    cur = pl.BlockSpec((1, 1, t), lambda i: (i, 0, 0), memory_space=pltpu.SMEM)
    nxt = pl.BlockSpec((1, 1, t), lambda i: (jnp.minimum(i + 1, nt - 1), 0, 0), memory_space=pltpu.SMEM)
    small = lambda arr: pl.BlockSpec(arr.shape, lambda i: (0,) * arr.ndim)
    return pl.pallas_call(
        _combine_kernel,
        grid=(nt,),
        in_specs=[cur, cur, nxt, nxt, pl.BlockSpec(memory_space=pl.ANY),
                  pl.BlockSpec((t, LANES), lambda i: (i, 0)),
                  pl.BlockSpec((t, d), lambda i: (i, 0)),
                  pl.BlockSpec((1, 1, d), lambda i: (i // per_b, 0, 0)),
                  small(ln_g), small(ln_b)],
        out_specs=pl.BlockSpec((t, d), lambda i: (i, 0)),
        out_shape=jax.ShapeDtypeStruct((n, d), F32),
        scratch_shapes=[pltpu.VMEM((2, t * TOKEN_SLAB, LANES), jnp.uint32),
                        pltpu.VMEM((2, t * TOKEN_SLAB, LANES), jnp.uint32),
                        pltpu.SemaphoreType.DMA((2,)), pltpu.SemaphoreType.DMA((2,))],
        compiler_params=_cparams(("arbitrary",)),
        name="combine",
    )(d1, d2, d1, d2, y_tiles, route, h1, g2, ln_g, ln_b)


def _hi_lo(w):
    hi = w.astype(BF16)
    return jnp.stack([hi, (w - hi.astype(F32)).astype(BF16)])


def _block_diag2(w):
    z = jnp.zeros_like(w[0])
    return jnp.concatenate([jnp.concatenate([w[0], z], axis=1), jnp.concatenate([z, w[1]], axis=1)], axis=0)


def kernel(x, c, ctx, c_ctx, emb_ln_g, emb_ln_b, w_mod, b_mod, w_in, tshift_mu, rwkv_w0, rwkv_w2, rwkv_a0, rwkv_a2, rwkv_g2, rwkv_k_k, rwkv_k_a, rwkv_r_k, rwkv_gn_g, rwkv_gn_b, ret_decay, ret_gn_g, ret_gn_b, w_out, ln1_g, ln1_b, router_group, router_group_bias, router_expert, router_expert_bias, expert_w_gate, expert_w_up, expert_w_down, ln2_g, ln2_b):
    assert w_mod.shape[0] == 1, "written for DEPTH == 1 (context outputs are never emitted)"
    b, n_tok, d = x.shape
    n_ctx = ctx.shape[1]
    row = lambda v: v.reshape(1, -1)

    c_rows = jnp.zeros((SUBLANES, d), F32).at[:b].set(c).at[b].set(c_ctx)
    mod = _modulation(c_rows, w_mod[0], row(b_mod[0]))
    sh1, s1, g1, sh2, s2, g2 = [mod[:b, j * d:(j + 1) * d].reshape(b, 1, d) for j in range(6)]
    sh1c, s1c = [jnp.broadcast_to(mod[b, j * d:(j + 1) * d].reshape(1, 1, d), (b, 1, d)) for j in range(2)]

    w_in_bf16 = w_in[0].astype(BF16)
    pr, pt = _in_proj(x, row(emb_ln_g), row(emb_ln_b), s1, sh1, w_in_bf16)
    pr_c, pt_c = _in_proj(ctx, row(emb_ln_g), row(emb_ln_b), s1c, sh1c, w_in_bf16)

    prep_params = (row(tshift_mu[0]), row(rwkv_w0[0]), _hi_lo(_block_diag2(rwkv_w2[0])), row(rwkv_a0[0]),
                   _hi_lo(_block_diag2(rwkv_a2[0])), _hi_lo(rwkv_g2[0]), row(rwkv_k_k[0]), row(rwkv_k_a[0]),
                   row(rwkv_r_k[0]),
                   _segment_ones(D_RWKV, RWKV_HEAD))
    lat = _rwkv_prepare(pr, prep_params, grid_shift=True)
    cx = _rwkv_prepare(pr_c, prep_params, grid_shift=False)
    r_l, v_l, kk_l, w_l, kd_l, bb_l, gate_l, bonus_l = lat
    r_c, v_c, kk_c, w_c, kd_c, bb_c, _, _ = cx

    y_f, y_b = _wkv7((r_l, v_l, kk_l, w_l, kd_l, bb_l), (r_c, v_c, kk_c, w_c, kd_c, bb_c), b, n_tok, n_ctx)

    cos_t, sin_t = _rope_tables(n_tok)
    t_f, t_b = _retention(pt, pt_c, ret_decay[0], cos_t, sin_t)

    wr = jnp.zeros((d, LANES), F32).at[:, :N_GROUPS].set(router_group[0])
    wr = wr.at[:, N_GROUPS:N_GROUPS + N_EXPERTS].set(router_expert[0])
    br = jnp.zeros((1, LANES), F32).at[0, :N_GROUPS].set(router_group_bias[0])
    br = br.at[0, N_GROUPS:N_GROUPS + N_EXPERTS].set(router_expert_bias[0].reshape(-1))
    vecs = (row(emb_ln_g), row(emb_ln_b), g1, s2, sh2, row(rwkv_gn_g[0]), row(rwkv_gn_b[0]),
            row(ret_gn_g[0]), row(ret_gn_b[0]), row(ln1_g[0]), row(ln1_b[0]), br)
    wr_hi_lo = _hi_lo(wr)
    mats = (_segment_ones(D_RWKV, RWKV_HEAD), w_out[0].astype(BF16), wr_hi_lo[0], wr_hi_lo[1])
    h1, u2, route, route_t, counts = _out_proj(x, y_f, y_b, bonus_l, gate_l, t_f, t_b, pt, vecs, mats)

    n_all = b * n_tok

    e1 = route_t[ROUTE_E1].astype(jnp.int32)
    e2 = route_t[ROUTE_E2].astype(jnp.int32)
    cnt = counts[0, :N_EXPERTS].astype(jnp.int32)
    padded = ((cnt + EXPERT_ROWS - 1) // EXPERT_ROWS) * EXPERT_ROWS
    pends = jnp.cumsum(padded)
    pstarts = pends - padded
    expert_ids = jnp.arange(N_EXPERTS, dtype=jnp.int32)
    start_of = lambda e: jnp.sum(jnp.where(e[:, None] == expert_ids[None, :], pstarts[None, :], 0), axis=1)
    dest1 = start_of(e1) + route_t[ROUTE_RANK1].astype(jnp.int32)
    dest2 = start_of(e2) + route_t[ROUTE_RANK2].astype(jnp.int32)
    n_blk = -(-(n_all * 2) // EXPERT_ROWS) + N_EXPERTS
    tok_ids = jnp.arange(n_all, dtype=jnp.int32)
    slot_tok = jnp.zeros((n_blk * EXPERT_ROWS,), jnp.int32).at[jnp.concatenate([dest1, dest2])].set(
        jnp.concatenate([tok_ids, tok_ids]))
    block_start = jnp.arange(n_blk, dtype=jnp.int32) * EXPERT_ROWS
    block_expert = jnp.minimum(jnp.sum((block_start[:, None] >= pends[None, :]).astype(jnp.int32), axis=1),
                               N_EXPERTS - 1)

    n_used = (pends[N_EXPERTS - 1:] // EXPERT_ROWS).astype(jnp.int32)
    y_tiles = _expert_mlp(u2, slot_tok, block_expert, n_used,
                          expert_w_gate[0], expert_w_up[0], expert_w_down[0])
    out = _combine(y_tiles, dest1, dest2, route, h1.reshape(n_all, d), g2, row(ln2_g[0]), row(ln2_b[0]), n_tok)
    return out.reshape(b, n_tok, d)
```

```python
import functools
import math

import jax
import jax.numpy as jnp
import numpy as np
from jax import lax
from jax.experimental import pallas as pl
from jax.experimental.pallas import tpu as pltpu

F32 = jnp.float32
BF16 = jnp.bfloat16
HIGHEST = lax.Precision.HIGHEST

GRID_W = 64
D_RWKV = 512
RWKV_HEAD = 64
RWKV_HEADS = D_RWKV // RWKV_HEAD
DECAY_LORA = 64
AAA_LORA = 64
GATE_LORA = 128
D_RET = 512
RET_HEADS = 4
RET_HEAD = D_RET // RET_HEADS
RET_CHUNK = 128
RWKV_COLS = 3 * D_RWKV + 2 * (DECAY_LORA + AAA_LORA) + GATE_LORA
RET_COLS = 4 * D_RET
N_GROUPS = 4
EXPERTS_PER_GROUP = 8
N_EXPERTS = N_GROUPS * EXPERTS_PER_GROUP
EXPERT_HIDDEN = 512
MOE_BLOCK = 128
ROPE_BASE = 10000.0
LN_EPS = 1e-5
RWKV_GN_EPS = 64e-5
RET_GN_EPS = 1e-5
DEEPNORM_ALPHA = 2.0 ** 0.25
EXP_NEG_HALF = math.exp(-0.5)

LANES = 128
SUBLANES = 8
VMEM_LIMIT_BYTES = 56 * 1024 * 1024

WKV_CHUNK = 64


IN_PROJ_ROWS = 512
OUT_PROJ_ROWS = 512

TOKEN_SLAB = 4
EXPERT_ROWS = 256
COMBINE_ROWS = 256
DISPATCH_ROWS = 256
GATHER_UNROLL = 16
GATHER_PRIORITIES = (0, 1)


def _pack_bf16_pairs(x):
    half = x.shape[1] // 2

    def bf16_bits(v):
        b = lax.bitcast_convert_type(v, jnp.uint32)
        return (b + jnp.uint32(0x7FFF) + ((b >> 16) & jnp.uint32(1))) >> 16

    return bf16_bits(x[:, :half]) | (bf16_bits(x[:, half:]) << 16)


def _unpack_bf16_pairs(p):
    lo = lax.bitcast_convert_type(p << 16, F32)
    hi = lax.bitcast_convert_type(p & jnp.uint32(0xFFFF0000), F32)
    return jnp.concatenate([lo, hi], axis=-1)


def _store_token_slabs(ref, x):
    rows = x.shape[0]
    for j in range(TOKEN_SLAB):
        ref[pl.ds(j, rows, stride=TOKEN_SLAB), :] = x[:, j * LANES:(j + 1) * LANES]


def _load_token_slabs(ref, rows):
    return jnp.concatenate([ref[pl.ds(j, rows, stride=TOKEN_SLAB), :] for j in range(TOKEN_SLAB)], axis=-1)


def _cparams(sem):
    return pltpu.CompilerParams(dimension_semantics=sem, vmem_limit_bytes=VMEM_LIMIT_BYTES)


def _layer_norm(x, g, b, eps=LN_EPS):
    mu = jnp.mean(x, axis=-1, keepdims=True)
    xc = x - mu
    var = jnp.mean(xc * xc, axis=-1, keepdims=True)
    return xc * lax.rsqrt(var + eps) * g + b


def _sigmoid(x):
    return 1.0 / (1.0 + jnp.exp(-x))


def _split_bf16(x):
    hi = x.astype(BF16)
    return hi, (x - hi.astype(F32)).astype(BF16)


def _segsum(x, ones_bf16):
    t = x.shape[0]
    s = jnp.dot(jnp.concatenate(_split_bf16(x), axis=0), ones_bf16, preferred_element_type=F32)
    return s[:t] + s[t:]


def _dot_split(x, w_hi, w_lo):
    hi, lo = _split_bf16(x)
    acc = jnp.dot(hi, w_hi, preferred_element_type=F32)
    acc = acc + jnp.dot(lo, w_hi, preferred_element_type=F32)
    return acc + jnp.dot(hi, w_lo, preferred_element_type=F32)


def _segment_ones(width, seg):
    idx = np.arange(width) // seg
    return jnp.asarray(idx[:, None] == idx[None, :], dtype=BF16)


def _mod_kernel(c_ref, w_ref, b_ref, o_ref):
    c = c_ref[...]
    sc = c * _sigmoid(c)
    o_ref[...] = jnp.dot(sc, w_ref[...], precision=HIGHEST, preferred_element_type=F32) + b_ref[...]


def _modulation(c_rows, w_mod, b_mod):
    rows, d = c_rows.shape
    n = w_mod.shape[1]
    tn = 1536
    return pl.pallas_call(
        _mod_kernel,
        grid=(n // tn,),
        in_specs=[pl.BlockSpec((rows, d), lambda j: (0, 0)),
                  pl.BlockSpec((d, tn), lambda j: (0, j)),
                  pl.BlockSpec((1, tn), lambda j: (0, j))],
        out_specs=pl.BlockSpec((rows, tn), lambda j: (0, j)),
        out_shape=jax.ShapeDtypeStruct((rows, n), F32),
        compiler_params=_cparams(("arbitrary",)),
        name="modulation",
    )(c_rows, w_mod, b_mod)


def _in_proj_kernel(x_ref, g_ref, b_ref, s_ref, sh_ref, w_ref, pr_ref, pt_ref):
    h = _layer_norm(x_ref[0], g_ref[...], b_ref[...])
    u = h * (1.0 + s_ref[0]) + sh_ref[0]
    p = jnp.dot(u.astype(BF16), w_ref[...], preferred_element_type=F32)
    pr_ref[0] = p[:, :RWKV_COLS]
    pt_ref[0] = p[:, RWKV_COLS:]


def _in_proj(x, ln_g, ln_b, s1, sh1, w_in_bf16):
    b, n, d = x.shape
    tm = min(IN_PROJ_ROWS, n)
    cols = w_in_bf16.shape[1]
    return pl.pallas_call(
        _in_proj_kernel,
        grid=(b, n // tm),
        in_specs=[pl.BlockSpec((1, tm, d), lambda bi, i: (bi, i, 0)),
                  pl.BlockSpec((1, d), lambda bi, i: (0, 0)),
                  pl.BlockSpec((1, d), lambda bi, i: (0, 0)),
                  pl.BlockSpec((1, 1, d), lambda bi, i: (bi, 0, 0)),
                  pl.BlockSpec((1, 1, d), lambda bi, i: (bi, 0, 0)),
                  pl.BlockSpec((d, cols), lambda bi, i: (0, 0))],
        out_specs=[pl.BlockSpec((1, tm, RWKV_COLS), lambda bi, i: (bi, i, 0)),
                   pl.BlockSpec((1, tm, RET_COLS), lambda bi, i: (bi, i, 0))],
        out_shape=[jax.ShapeDtypeStruct((b, n, RWKV_COLS), F32),
                   jax.ShapeDtypeStruct((b, n, RET_COLS), F32)],
        compiler_params=_cparams(("arbitrary", "arbitrary")),
        name="in_proj",
    )(x, ln_g, ln_b, s1, sh1, w_in_bf16)


def _rwkv_prepare_kernel(cur_ref, prev_ref, next_ref, mu_ref, w0_ref, w2_ref, a0_ref, a2_ref, g2_ref,
                         kk_scale_ref, ka_ref, rk_ref, ones_ref,
                         r_ref, v_ref, kk_ref, w_ref, kd_ref, bb_ref, g_ref, bonus_ref,
                         *, grid_shift, n_tok):
    cur = cur_ref[0]
    t, c = cur.shape
    row = lax.broadcasted_iota(jnp.int32, (t, c), 0)
    lane = lax.broadcasted_iota(jnp.int32, (t, c), 1)
    prev_tok = pltpu.roll(cur, 1, 0)
    next_tok = pltpu.roll(cur, t - 1, 0)
    if grid_shift:
        col = row & (GRID_W - 1)
        tok = row + pl.program_id(1) * t
        left = jnp.where(col > 0, prev_tok, 0.0)
        right = jnp.where(col < GRID_W - 1, next_tok, 0.0)
        up = jnp.where(tok >= GRID_W, jnp.concatenate([prev_ref[0], cur[:t - GRID_W]], axis=0), 0.0)
        down = jnp.where(tok < n_tok - GRID_W, jnp.concatenate([cur[GRID_W:], next_ref[0]], axis=0), 0.0)
        cm = lane & 3
        shifted = jnp.where(cm == 0, left, jnp.where(cm == 1, right, jnp.where(cm == 2, up, down)))
    else:
        prev_tok = jnp.where(row > 0, prev_tok, 0.0)
        next_tok = jnp.where(row < t - 1, next_tok, 0.0)
        shifted = jnp.where((lane & 1) == 0, prev_tok, next_tok)
    pm = cur + mu_ref[...] * (shifted - cur)

    r = pm[:, 0:D_RWKV]
    k = pm[:, D_RWKV:2 * D_RWKV]
    v = pm[:, 2 * D_RWKV:3 * D_RWKV]
    o = 3 * D_RWKV
    lw = pm[:, o:o + 2 * DECAY_LORA]
    la = pm[:, o + 2 * DECAY_LORA:o + 2 * (DECAY_LORA + AAA_LORA)]
    lg = pm[:, o + 2 * (DECAY_LORA + AAA_LORA):]

    w = w0_ref[...] + _dot_split(jnp.tanh(lw), w2_ref[0], w2_ref[1])
    log_decay = -EXP_NEG_HALF * _sigmoid(w)
    a = _sigmoid(a0_ref[...] + _dot_split(la, a2_ref[0], a2_ref[1]))
    gate = _dot_split(_sigmoid(lg), g2_ref[0], g2_ref[1])

    ones = ones_ref[...]
    kk_raw = k * kk_scale_ref[...]
    kk = kk_raw / jnp.maximum(jnp.sqrt(_segsum(kk_raw * kk_raw, ones)), 1e-12)
    ka = ka_ref[...]
    a0 = a[:, :D_RWKV]
    a1 = a[:, D_RWKV:]
    kd0 = k * (1.0 + (a0 - 1.0) * ka)
    kd1 = k * (1.0 + (a1 - 1.0) * ka)
    bonus = _segsum(r * (kd0 + kd1) * rk_ref[...], ones) * v

    r_ref[0] = r
    v_ref[0] = v
    kk_ref[0] = kk
    w_ref[0] = log_decay
    kd_ref[0] = jnp.concatenate([kd0, kd1], axis=-1)
    bb_ref[0] = jnp.concatenate([kk * a0, kk * a1], axis=-1)
    g_ref[0] = gate
    bonus_ref[0] = bonus


def _rwkv_prepare(pr, params, grid_shift):
    b, n, c = pr.shape
    t = 256
    if not grid_shift:
        assert n == t, "sequence token shift is written for a single tile"
    halo_blocks = n // GRID_W
    per_tile = t // GRID_W
    small = lambda shape: pl.BlockSpec(shape, lambda bi, i: (0,) * len(shape))
    tok_spec = lambda width: pl.BlockSpec((1, t, width), lambda bi, i: (bi, i, 0))
    out_widths = (D_RWKV, D_RWKV, D_RWKV, 2 * D_RWKV, 2 * D_RWKV, 2 * D_RWKV, D_RWKV, D_RWKV)
    kernel = functools.partial(_rwkv_prepare_kernel, grid_shift=grid_shift, n_tok=n)
    return pl.pallas_call(
        kernel,
        grid=(b, n // t),
        in_specs=[tok_spec(c),
                  pl.BlockSpec((1, GRID_W, c), lambda bi, i: (bi, jnp.maximum(i * per_tile - 1, 0), 0)),
                  pl.BlockSpec((1, GRID_W, c),
                               lambda bi, i: (bi, jnp.minimum((i + 1) * per_tile, halo_blocks - 1), 0)),
                  small((1, c)), small((1, 2 * D_RWKV)), small((2, 2 * DECAY_LORA, 2 * D_RWKV)),
                  small((1, 2 * D_RWKV)), small((2, 2 * AAA_LORA, 2 * D_RWKV)), small((2, GATE_LORA, D_RWKV)),
                  small((1, D_RWKV)), small((1, D_RWKV)), small((1, D_RWKV)), small((D_RWKV, D_RWKV))],
        out_specs=[tok_spec(wd) for wd in out_widths],
        out_shape=[jax.ShapeDtypeStruct((b, n, wd), F32) for wd in out_widths],
        compiler_params=_cparams(("arbitrary", "arbitrary")),
        name="rwkv_prepare",
    )(pr, pr, pr, *params)


def _bdot(a, b):
    return jnp.dot(a.astype(BF16), b.astype(BF16), preferred_element_type=F32)


def _bdot_nt(a, b):
    return lax.dot_general(a.astype(BF16), b.astype(BF16), (((1,), (1,)), ((), ())), preferred_element_type=F32)


def _bdot_tn(a, b):
    return lax.dot_general(a.astype(BF16), b.astype(BF16), (((0,), (0,)), ((), ())), preferred_element_type=F32)


def _wkv7_chunk_kernel(*refs, n_ctx_chunks):
    c = WKV_CHUNK
    p = 2 * c
    n_in = 12
    in_refs = (refs[:n_in], refs[n_in:2 * n_in])
    y_refs = refs[2 * n_in:2 * n_in + 2]
    state_ref = refs[2 * n_in + 2]
    n = pl.program_id(0)
    n_batch = in_refs[0][0].shape[0]
    pairs_per_batch = RWKV_HEADS // 2
    pairs_per_dir = n_batch * pairs_per_batch

    @pl.when(n == 0)
    def _():
        state_ref[...] = jnp.zeros_like(state_ref)

    is_ctx = n < n_ctx_chunks
    ti = lax.broadcasted_iota(jnp.int32, (c, c), 0)
    tj = lax.broadcasted_iota(jnp.int32, (c, c), 1)
    ri = lax.broadcasted_iota(jnp.int32, (p, p), 0)
    ci = lax.broadcasted_iota(jnp.int32, (p, p), 1)
    same_head = (ri >= c) == (ci >= c)
    ii = ri & (c - 1)
    jj = ci & (c - 1)
    eye = (ri == ci).astype(F32)
    first = lax.broadcasted_iota(jnp.int32, (c, p), 1) < RWKV_HEAD

    def stack(x):
        return jnp.concatenate([jnp.where(first, x, 0.0), jnp.where(first, 0.0, x)], axis=0)

    def unstack(x):
        return x[:c] + x[c:]

    a_st, r_st, k_st, b_st, k2_st, b2_st, v_st, g_chunk, earlier, upto_self = ([] for _ in range(10))
    for d in range(2):
        r_l, v_l, kk_l, lw_l, kd_l, bb_l, r_c, v_c, kk_c, lw_c, kd_c, bb_c = in_refs[d]
        pick = lambda xc, xl: jnp.concatenate(
            [jnp.where(is_ctx, xc[bi], xl[bi]) for bi in range(n_batch)], axis=-1)
        r, v, kk, lw, kd, bb = (pick(r_c, r_l), pick(v_c, v_l), pick(kk_c, kk_l), pick(lw_c, lw_l),
                                pick(kd_c, kd_l), pick(bb_c, bb_l))
        before = (tj < ti) if d == 0 else (tj > ti)
        upto = (before | (ti == tj)).astype(BF16)
        hi = lw.astype(BF16)
        r1 = lw - hi.astype(F32)
        mid = r1.astype(BF16)
        lo = (r1 - mid.astype(F32)).astype(BF16)
        cum = (jnp.dot(upto, hi, preferred_element_type=F32) + jnp.dot(upto, mid, preferred_element_type=F32)
               + jnp.dot(upto, lo, preferred_element_type=F32))
        tot = jnp.sum(lw, axis=0, keepdims=True)
        e_neg = jnp.exp(-cum)
        e_rem = jnp.exp(tot - cum)
        alpha = kk * jnp.exp(cum - lw)
        rho = r * jnp.exp(cum)
        beta = bb * e_neg
        kappa = kd * e_neg
        kappa_rem = kd * e_rem
        beta_rem = bb * e_rem
        g_all = jnp.exp(tot)
        pair_before = same_head & ((jj < ii) if d == 0 else (jj > ii))
        pair_upto = pair_before | (ri == ci)
        for hp in range(pairs_per_dir):
            sl = slice(hp * p, (hp + 1) * p)
            a_st.append(stack(alpha[:, sl]))
            r_st.append(stack(rho[:, sl]))
            k_st.append(stack(kappa[:, sl]))
            b_st.append(stack(beta[:, sl]))
            k2_st.append(stack(kappa_rem[:, sl]))
            b2_st.append(stack(beta_rem[:, sl]))
            v_st.append(stack(v[:, sl]))
            g_chunk.append(g_all[:, sl])
            earlier.append(pair_before)
            upto_self.append(pair_upto)

    pairs = range(2 * pairs_per_dir)
    g = [_bdot_nt(jnp.concatenate([a_st[h], r_st[h]], axis=0), jnp.concatenate([k_st[h], b_st[h]], axis=0))
         for h in pairs]
    m1 = [jnp.where(earlier[h], g[h][:p, :p], 0.0) for h in pairs]
    m2 = [jnp.where(earlier[h], g[h][:p, p:], 0.0) for h in pairs]
    n1 = [jnp.where(upto_self[h], g[h][p:, :p], 0.0) for h in pairs]
    n2 = [jnp.where(upto_self[h], g[h][p:, p:], 0.0) for h in pairs]

    in_block = (ii >> 3) == (jj >> 3)
    pw = [-jnp.where(in_block, m2[h], 0.0) for h in pairs]
    inv = [eye + pw[h] for h in pairs]
    pw = [_bdot(pw[h], pw[h]) for h in pairs]
    both = [_bdot(jnp.concatenate([inv[h], pw[h]], axis=0), pw[h]) for h in pairs]
    inv = [inv[h] + both[h][:p] for h in pairs]
    inv = [inv[h] + _bdot(inv[h], both[h][p:]) for h in pairs]
    for sh in (3, 4, 5):
        off = ((ii >> (sh + 1)) == (jj >> (sh + 1))) & ((ii >> sh) != (jj >> sh))
        left = [_bdot(inv[h], jnp.where(off, m2[h], 0.0)) for h in pairs]
        inv = [inv[h] - _bdot(left[h], inv[h]) for h in pairs]

    mnv = [_bdot(jnp.concatenate([m1[h], n1[h]], axis=0), v_st[h]) for h in pairs]
    m1v = [mnv[h][:p] for h in pairs]
    n1v = [mnv[h][p:] for h in pairs]
    au = [_bdot(inv[h], jnp.concatenate([a_st[h], m1v[h]], axis=1)) for h in pairs]
    nn = [_bdot(n2[h], au[h]) for h in pairs]
    pc = [_bdot_tn(b2_st[h], au[h][:, :p]) for h in pairs]
    qc_t = [_bdot_tn(jnp.concatenate([v_st[h], -au[h][:, p:]], axis=0),
                     jnp.concatenate([k2_st[h], b2_st[h]], axis=0)) for h in pairs]
    s0 = [state_ref[h] for h in pairs]
    y = [_bdot_nt(unstack(r_st[h] - nn[h][:, :p]), s0[h]) + unstack(n1v[h] - nn[h][:, p:]) for h in pairs]
    s_dec = [_bdot_nt(s0[h], pc[h]) for h in pairs]
    for h in pairs:
        d, hp = divmod(h, pairs_per_dir)
        bi, hpb = divmod(hp, pairs_per_batch)
        y_refs[d][bi, :, hpb * p:(hpb + 1) * p] = y[h]
        state_ref[h] = s0[h] * g_chunk[h] - s_dec[h] + qc_t[h]


def _wkv7(lat, ctx, b, n_tok, n_ctx):
    c = WKV_CHUNK
    ncx = n_ctx // c
    nl = n_tok // c
    lat_idx = (lambda n: jnp.maximum(n - ncx, 0), lambda n: nl - 1 - jnp.maximum(n - ncx, 0))
    ctx_idx = (lambda n: jnp.minimum(n, ncx - 1), lambda n: ncx - 1 - jnp.minimum(n, ncx - 1))

    def specs(idx, d):
        shared = pl.BlockSpec((b, c, D_RWKV), lambda n: (0, idx(n), 0))
        per_dir = pl.BlockSpec((b, c, D_RWKV), lambda n: (0, idx(n), d))
        return [shared, shared, shared, per_dir, per_dir, per_dir]

    in_specs, args = [], []
    for d in range(2):
        in_specs += specs(lat_idx[d], d) + specs(ctx_idx[d], d)
        args += list(lat) + list(ctx)
    return pl.pallas_call(
        functools.partial(_wkv7_chunk_kernel, n_ctx_chunks=ncx),
        grid=(ncx + nl,),
        in_specs=in_specs,
        out_specs=[pl.BlockSpec((b, c, D_RWKV), lambda n, d=d: (0, lat_idx[d](n), 0)) for d in range(2)],
        out_shape=[jax.ShapeDtypeStruct((b, n_tok, D_RWKV), F32)] * 2,
        scratch_shapes=[pltpu.VMEM((2 * b * RWKV_HEADS // 2, 2 * RWKV_HEAD, 2 * RWKV_HEAD), F32)],
        compiler_params=_cparams(("arbitrary",)),
        name="wkv7_chunk",
    )(*args)


def _rope(z, cos_t, sin_t):
    lane = lax.broadcasted_iota(jnp.int32, z.shape, 1)
    half = RET_HEAD // 4
    partner = jnp.where((lane & (2 * half - 1)) < half, pltpu.roll(z, RET_HEAD - half, 1), pltpu.roll(z, half, 1))
    return z * cos_t + partner * sin_t


def _retention_kernel(dec_ref, fwd_ref, bwd_ref, ctx_ref, cosf_ref, sinf_ref, cosb_ref, sinb_ref,
                      yf_ref, yb_ref, state_ref, dmat_ref, tail_ref, head_ref, cdec_ref):
    c = RET_CHUNK
    scale = RET_HEAD ** -0.5
    ii = lax.broadcasted_iota(jnp.int32, (c, c), 0)
    jj = lax.broadcasted_iota(jnp.int32, (c, c), 1)
    pos = lax.broadcasted_iota(jnp.int32, (c, RET_HEAD), 0).astype(F32)
    n_ctx_chunks = ctx_ref.shape[1] // c

    def head_slices(ref_val, h):
        q = ref_val[:, h * RET_HEAD:(h + 1) * RET_HEAD]
        k = ref_val[:, D_RET + h * RET_HEAD:D_RET + (h + 1) * RET_HEAD]
        v = ref_val[:, 2 * D_RET + h * RET_HEAD:2 * D_RET + (h + 1) * RET_HEAD]
        return q, k, v

    n_batch = fwd_ref.shape[0]
    heads = [(d, h) for d in range(2) for h in range(RET_HEADS)]
    chains = [(bi, d, h) for bi in range(n_batch) for d, h in heads]

    @pl.when(pl.program_id(0) == 0)
    def _():
        for d, h in heads:
            x = jnp.full((1, RET_HEAD), dec_ref[d, h], F32)
            lg = -(jnp.maximum(x, 0.0) + jnp.log(1.0 + jnp.exp(-jnp.abs(x))))
            chunk_decay = jnp.exp(lg * float(c))
            tail = jnp.exp(lg * ((c - 1.0 - pos) if d == 0 else pos))
            rel = (ii - jj) if d == 0 else (jj - ii)
            mask = (rel >= 0) if d == 0 else (rel > 0)
            dmat_ref[d, h] = jnp.where(mask, jnp.exp(lg * jnp.maximum(rel, 0).astype(F32)), 0.0)
            tail_ref[d, h] = tail
            head_ref[d, h] = jnp.exp(lg * ((pos + 1.0) if d == 0 else (c - pos)))
            cdec_ref[d, h] = jnp.broadcast_to(chunk_decay, (SUBLANES, RET_HEAD))
            order = range(n_ctx_chunks) if d == 0 else range(n_ctx_chunks - 1, -1, -1)
            for bi in range(n_batch):
                s = jnp.zeros((RET_HEAD, RET_HEAD), F32)
                for cc in order:
                    _, kc, vc = head_slices(ctx_ref[bi, cc * c:(cc + 1) * c, :], h)
                    s = s * chunk_decay + _bdot_tn(kc * scale * tail, vc)
                state_ref[bi, d, h] = s

    qkv = []
    for bi, d, h in chains:
        blk = fwd_ref[bi] if d == 0 else bwd_ref[bi]
        cos_t = cosf_ref[...] if d == 0 else cosb_ref[...]
        sin_t = sinf_ref[...] if d == 0 else sinb_ref[...]
        q, k, v = head_slices(blk, h)
        qkv.append((_rope(q, cos_t, sin_t), _rope(k, cos_t, sin_t) * scale, v.astype(BF16)))
    s0 = [state_ref[bi, d, h] for bi, d, h in chains]
    scores = [_bdot_nt(q, k) for q, k, _ in qkv]
    inner = [_bdot(scores[i] * dmat_ref[d, h], qkv[i][2]) for i, (_, d, h) in enumerate(chains)]
    cross = [_bdot(qkv[i][0] * head_ref[d, h], s0[i]) for i, (_, d, h) in enumerate(chains)]
    upd = [_bdot_tn(qkv[i][1] * tail_ref[d, h], qkv[i][2]) for i, (_, d, h) in enumerate(chains)]
    for i, (bi, d, h) in enumerate(chains):
        state_ref[bi, d, h] = s0[i] * cdec_ref[d, h, 0:1, :] + upd[i]
        out_ref = yf_ref if d == 0 else yb_ref
        out_ref[bi, :, h * RET_HEAD:(h + 1) * RET_HEAD] = inner[i] + cross[i]


def _retention(pt, pt_ctx, ret_decay, cos_t, sin_t):
    b, n, _ = pt.shape
    c = RET_CHUNK
    nc = n // c
    qkv = 3 * D_RET
    fwd = lambda i: (0, i, 0)
    bwd = lambda i: (0, nc - 1 - i, 0)
    return pl.pallas_call(
        _retention_kernel,
        grid=(nc,),
        in_specs=[pl.BlockSpec(memory_space=pltpu.SMEM),
                  pl.BlockSpec((b, c, qkv), fwd),
                  pl.BlockSpec((b, c, qkv), bwd),
                  pl.BlockSpec((b, pt_ctx.shape[1], qkv), lambda i: (0, 0, 0)),
                  pl.BlockSpec((c, RET_HEAD), lambda i: (i, 0)),
                  pl.BlockSpec((c, RET_HEAD), lambda i: (i, 0)),
                  pl.BlockSpec((c, RET_HEAD), lambda i: (nc - 1 - i, 0)),
                  pl.BlockSpec((c, RET_HEAD), lambda i: (nc - 1 - i, 0))],
        out_specs=[pl.BlockSpec((b, c, D_RET), fwd), pl.BlockSpec((b, c, D_RET), bwd)],
        out_shape=[jax.ShapeDtypeStruct((b, n, D_RET), F32), jax.ShapeDtypeStruct((b, n, D_RET), F32)],
        scratch_shapes=[pltpu.VMEM((b, 2, RET_HEADS, RET_HEAD, RET_HEAD), F32),
                        pltpu.VMEM((2, RET_HEADS, c, c), F32),
                        pltpu.VMEM((2, RET_HEADS, c, RET_HEAD), F32),
                        pltpu.VMEM((2, RET_HEADS, c, RET_HEAD), F32),
                        pltpu.VMEM((2, RET_HEADS, SUBLANES, RET_HEAD), F32)],
        compiler_params=_cparams(("arbitrary",)),
        name="retention",
    )(ret_decay, pt, pt, pt_ctx, cos_t, sin_t, cos_t, sin_t)


def _rope_tables(n_tok):
    nf = RET_HEAD // 4
    lane = np.arange(RET_HEAD)
    inv = ROPE_BASE ** (-jnp.arange(nf, dtype=F32) / nf)
    t = jnp.arange(n_tok)
    pos = jnp.where((lane // (2 * nf) == 0)[None, :], (t // GRID_W)[:, None], (t % GRID_W)[:, None]).astype(F32)
    ang = pos * inv[lane % nf][None, :]
    sign = jnp.where((lane % (2 * nf)) < nf, -1.0, 1.0).astype(F32)
    return jnp.cos(ang), jnp.sin(ang) * sign[None, :]


def _group_norm(y, ones, seg, eps, g, b):
    mu = _segsum(y, ones) * (1.0 / seg)
    yc = y - mu
    var = _segsum(yc * yc, ones) * (1.0 / seg)
    return yc * lax.rsqrt(var + eps) * g + b


def _out_proj_kernel(x_ref, yf_ref, yb_ref, bonus_ref, gate_ref, tf_ref, tb_ref, gt_ref,
                     embg_ref, embb_ref, g1_ref, s2_ref, sh2_ref, rgn_g_ref, rgn_b_ref, tgn_g_ref, tgn_b_ref,
                     ones_r_ref, wout_ref, ln1g_ref, ln1b_ref, wrh_ref, wrl_ref, br_ref,
                     h1_ref, u2_ref, route_ref, route_t_ref, count_ref, carry_ref):
    @pl.when((pl.program_id(0) == 0) & (pl.program_id(1) == 0))
    def _():
        carry_ref[...] = jnp.zeros_like(carry_ref)

    y = yf_ref[0] + yb_ref[0]
    o_rwkv = _group_norm(y, ones_r_ref[...], RWKV_HEAD, RWKV_GN_EPS, rgn_g_ref[...], rgn_b_ref[...])
    o_rwkv = (o_rwkv + bonus_ref[0]) * gate_ref[0]
    yt = tf_ref[0] + tb_ref[0]
    gt = gt_ref[0]
    tgn_g = tgn_g_ref[...]
    tgn_b = tgn_b_ref[...]
    o_ret = jnp.concatenate(
        [_layer_norm(yt[:, h * RET_HEAD:(h + 1) * RET_HEAD], tgn_g[:, h * RET_HEAD:(h + 1) * RET_HEAD],
                     tgn_b[:, h * RET_HEAD:(h + 1) * RET_HEAD], RET_GN_EPS) for h in range(RET_HEADS)], axis=-1)
    o_ret = o_ret * (gt * _sigmoid(gt))
    cat = jnp.concatenate([o_rwkv, o_ret], axis=-1).astype(BF16)
    mix = jnp.dot(cat, wout_ref[...], preferred_element_type=F32)
    h = _layer_norm(x_ref[0], embg_ref[...], embb_ref[...])
    h1 = _layer_norm(DEEPNORM_ALPHA * h + g1_ref[0] * mix, ln1g_ref[...], ln1b_ref[...])
    u2 = h1 * (1.0 + s2_ref[0]) + sh2_ref[0]
    h1_ref[0] = h1
    _store_token_slabs(u2_ref, _pack_bf16_pairs(u2))
    route = _route_tile(_dot_split(u2, wrh_ref[...], wrl_ref[...]) + br_ref[...], carry_ref)
    route_ref[...] = route
    route_t_ref[...] = route.T[:SUBLANES]
    count_ref[...] = carry_ref[...]


def _out_proj(x, y_f, y_b, bonus, gate, t_f, t_b, pt, vecs, mats):
    b, n, d = x.shape
    t = OUT_PROJ_ROWS
    tok = lambda width: pl.BlockSpec((1, t, width), lambda bi, i: (bi, i, 0))
    per_b = pl.BlockSpec((1, 1, d), lambda bi, i: (bi, 0, 0))
    small = lambda arr: pl.BlockSpec(arr.shape, lambda bi, i: (0,) * arr.ndim)
    (embg, embb, g1, s2, sh2, rgn_g, rgn_b, tgn_g, tgn_b, ln1g, ln1b, br) = vecs
    (ones_r, wout, wr_hi, wr_lo) = mats
    gt_spec = pl.BlockSpec((1, t, D_RET), lambda bi, i: (bi, i, 3))
    tiles = n // t
    flat = lambda bi, i: (bi * tiles + i, 0)
    args = (x, y_f, y_b, bonus, gate, t_f, t_b, pt, embg, embb, g1, s2, sh2, rgn_g, rgn_b, tgn_g, tgn_b,
            ones_r, wout, ln1g, ln1b, wr_hi, wr_lo, br)
    in_specs = [tok(d)] + [tok(D_RWKV)] * 6 + [gt_spec, small(embg), small(embb), per_b, per_b, per_b,
                                                small(rgn_g), small(rgn_b), small(tgn_g), small(tgn_b),
                                                small(ones_r), small(wout), small(ln1g),
                                                small(ln1b), small(wr_hi), small(wr_lo), small(br)]
    return pl.pallas_call(
        _out_proj_kernel,
        grid=(b, n // t),
        in_specs=in_specs,
        out_specs=[tok(d), pl.BlockSpec((t * TOKEN_SLAB, LANES), flat),
                   pl.BlockSpec((t, LANES), flat),
                   pl.BlockSpec((SUBLANES, t), lambda bi, i: (0, bi * tiles + i)),
                   pl.BlockSpec((SUBLANES, LANES), lambda bi, i: (0, 0))],
        out_shape=[jax.ShapeDtypeStruct((b, n, d), F32),
                   jax.ShapeDtypeStruct((b * n * TOKEN_SLAB, LANES), jnp.uint32),
                   jax.ShapeDtypeStruct((b * n, LANES), F32),
                   jax.ShapeDtypeStruct((SUBLANES, b * n), F32),
                   jax.ShapeDtypeStruct((SUBLANES, LANES), F32)],
        scratch_shapes=[pltpu.VMEM((SUBLANES, LANES), F32)],
        compiler_params=_cparams(("arbitrary", "arbitrary")),
        name="out_proj",
    )(*args)


ROUTE_E1, ROUTE_E2, ROUTE_G1, ROUTE_G2, ROUTE_RANK1, ROUTE_RANK2 = range(6)


def _lane_argmax(x, valid, lane):
    m = jnp.max(jnp.where(valid, x, -jnp.inf), axis=-1, keepdims=True)
    idx = jnp.min(jnp.where(valid & (x == m), lane, float(LANES)), axis=-1, keepdims=True)
    return m, idx


def _route_tile(lg, carry_ref):
    t = lg.shape[0]
    lane = lax.broadcasted_iota(jnp.int32, lg.shape, 1).astype(F32)
    gmask = lane < N_GROUPS
    gmax = jnp.max(jnp.where(gmask, lg, -jnp.inf), axis=-1, keepdims=True)
    gexp = jnp.where(gmask, jnp.exp(lg - gmax), 0.0)
    gp = gexp / jnp.sum(gexp, axis=-1, keepdims=True)
    g_w, g_i = _lane_argmax(gp, gmask, lane)

    lo = N_GROUPS + EXPERTS_PER_GROUP * g_i
    emask = (lane >= lo) & (lane < lo + EXPERTS_PER_GROUP)
    emax = jnp.max(jnp.where(emask, lg, -jnp.inf), axis=-1, keepdims=True)
    eexp = jnp.where(emask, jnp.exp(lg - emax), 0.0)
    ep = eexp / jnp.sum(eexp, axis=-1, keepdims=True)
    p1, i1 = _lane_argmax(ep, emask, lane)
    p2, i2 = _lane_argmax(ep, emask & (lane != i1), lane)
    denom = p1 + p2
    gate1 = g_w * p1 / denom
    gate2 = g_w * p2 / denom
    e1 = i1 - N_GROUPS
    e2 = i2 - N_GROUPS

    oh1 = (lane == e1).astype(F32)
    oh2 = (lane == e2).astype(F32)
    cnt = oh1 + oh2
    ri = lax.broadcasted_iota(jnp.int32, (t, t), 0)
    ci = lax.broadcasted_iota(jnp.int32, (t, t), 1)
    before = (ci < ri).astype(BF16)
    seen = jnp.dot(before, cnt.astype(BF16), preferred_element_type=F32) + carry_ref[0:1, :]
    rank1 = jnp.sum(oh1 * seen, axis=-1, keepdims=True)
    rank2 = jnp.sum(oh2 * seen, axis=-1, keepdims=True)
    carry_ref[0:1, :] = carry_ref[0:1, :] + jnp.sum(cnt, axis=0, keepdims=True)

    out = jnp.zeros(lg.shape, F32)
    for slot, val in ((ROUTE_E1, e1.astype(F32)), (ROUTE_E2, e2.astype(F32)), (ROUTE_G1, gate1),
                      (ROUTE_G2, gate2), (ROUTE_RANK1, rank1), (ROUTE_RANK2, rank2)):
        out = jnp.where(lane == slot, val, out)
    return out


def _tile_gather_copy(src_hbm, idx_ref, buf, sem, slot, r):
    src = src_hbm.at[pl.ds(pl.multiple_of(idx_ref[0, 0, r] * TOKEN_SLAB, TOKEN_SLAB), TOKEN_SLAB), :]
    dst = buf.at[slot, pl.ds(pl.multiple_of(r * TOKEN_SLAB, TOKEN_SLAB), TOKEN_SLAB), :]
    return pltpu.make_async_copy(src, dst, sem.at[slot])


def _start_tile_gather(src_hbm, idx_ref, buf, sem, slot, rows, priorities):
    def body(g, carry):
        for j in range(GATHER_UNROLL):
            copy = _tile_gather_copy(src_hbm, idx_ref, buf, sem, slot, g * GATHER_UNROLL + j)
            copy.start(priority=priorities[j % len(priorities)])
        return carry
    lax.fori_loop(0, rows // GATHER_UNROLL, body, 0)


def _wait_tile_gather(src_hbm, idx_ref, buf, sem, slot, rows):
    del idx_ref
    whole = src_hbm.at[pl.ds(0, rows * TOKEN_SLAB), :]
    pltpu.make_async_copy(whole, buf.at[slot], sem.at[slot]).wait()


def _dispatch_kernel(d1_ref, d2_ref, u_hbm, x_init_hbm, x_hbm, sem):
    del x_init_hbm
    i = pl.program_id(0)
    rows = d1_ref.shape[2]

    def slab(ref, index):
        return ref.at[pl.ds(pl.multiple_of(index * TOKEN_SLAB, TOKEN_SLAB), TOKEN_SLAB), :]

    def body(g, carry):
        for j in range(GATHER_UNROLL):
            r = g * GATHER_UNROLL + j
            src = slab(u_hbm, i * rows + r)
            pltpu.make_async_copy(src, slab(x_hbm, d1_ref[0, 0, r]), sem).start(priority=j % 2)
            pltpu.make_async_copy(src, slab(x_hbm, d2_ref[0, 0, r]), sem).start(priority=(j + 1) % 2)
        return carry

    lax.fori_loop(0, rows // GATHER_UNROLL, body, 0)

    def drain_one_step():
        n_rows = 2 * rows * TOKEN_SLAB
        pltpu.make_async_copy(u_hbm.at[pl.ds(0, n_rows), :], x_hbm.at[pl.ds(0, n_rows), :], sem).wait()

    @pl.when(i > 0)
    def _():
        drain_one_step()

    @pl.when(i == pl.num_programs(0) - 1)
    def _():
        drain_one_step()


def _dispatch(u2_slabs, dest1, dest2, n_slot):
    n = dest1.shape[0]
    t = DISPATCH_ROWS
    nt = n // t
    idx = pl.BlockSpec((1, 1, t), lambda i: (i, 0, 0), memory_space=pltpu.SMEM)
    any_spec = pl.BlockSpec(memory_space=pl.ANY)
    x_init = jnp.zeros((n_slot * TOKEN_SLAB, LANES), jnp.uint32)
    return pl.pallas_call(
        _dispatch_kernel,
        grid=(nt,),
        in_specs=[idx, idx, any_spec, any_spec],
        out_specs=any_spec,
        out_shape=jax.ShapeDtypeStruct(x_init.shape, x_init.dtype),
        scratch_shapes=[pltpu.SemaphoreType.DMA(())],
        input_output_aliases={3: 0},
        compiler_params=_cparams(("arbitrary",)),
        name="dispatch",
    )(dest1.reshape(nt, 1, t), dest2.reshape(nt, 1, t), u2_slabs, x_init)


def _expert_kernel(blk_expert_ref, n_used_ref, x_ref, wg_ref, wu_ref, wd_ref, y_ref, wg_s, wu_s, wd_s):
    i = pl.program_id(0)
    n_used = n_used_ref[0]

    @pl.when(i >= n_used)
    def _():
        y_ref[...] = jnp.zeros_like(y_ref)

    @pl.when(i < n_used)
    def _():
        @pl.when((i == 0) | (blk_expert_ref[i] != blk_expert_ref[jnp.maximum(i - 1, 0)]))
        def _():
            wg_s[...] = wg_ref[0].astype(BF16)
            wu_s[...] = wu_ref[0].astype(BF16)
            wd_s[...] = wd_ref[0].astype(BF16)

        x = _unpack_bf16_pairs(_load_token_slabs(x_ref, EXPERT_ROWS)).astype(BF16)
        hg = jnp.dot(x, wg_s[...], preferred_element_type=F32)
        hu = jnp.dot(x, wu_s[...], preferred_element_type=F32)
        act = (hg * _sigmoid(hg) * hu).astype(BF16)
        _store_token_slabs(y_ref, _pack_bf16_pairs(jnp.dot(act, wd_s[...], preferred_element_type=F32)))


def _expert_mlp(x_slabs, block_expert, n_used, w_gate, w_up, w_down):
    n_blk = block_expert.shape[0]
    d = w_gate.shape[1]
    hdim = w_gate.shape[2]
    rows = EXPERT_ROWS
    used = lambda i, nu: jnp.minimum(i, nu[0] - 1)
    weight = lambda i, be, nu: (be[used(i, nu)], 0, 0)
    grid_spec = pltpu.PrefetchScalarGridSpec(
        num_scalar_prefetch=2,
        grid=(n_blk,),
        in_specs=[pl.BlockSpec((rows * TOKEN_SLAB, LANES), lambda i, be, nu: (used(i, nu), 0)),
                  pl.BlockSpec((1, d, hdim), weight),
                  pl.BlockSpec((1, d, hdim), weight),
                  pl.BlockSpec((1, hdim, d), weight)],
        out_specs=pl.BlockSpec((rows * TOKEN_SLAB, LANES), lambda i, be, nu: (i, 0)),
        scratch_shapes=[pltpu.VMEM((d, hdim), BF16), pltpu.VMEM((d, hdim), BF16), pltpu.VMEM((hdim, d), BF16)],
    )
    return pl.pallas_call(
        _expert_kernel,
        grid_spec=grid_spec,
        out_shape=jax.ShapeDtypeStruct((n_blk * rows * TOKEN_SLAB, LANES), jnp.uint32),
        compiler_params=_cparams(("arbitrary",)),
        name="expert_mlp",
    )(block_expert, n_used, x_slabs, w_gate, w_up, w_down)


def _combine_kernel(d1_ref, d2_ref, d1n_ref, d2n_ref, y_hbm, route_ref, h1_ref, g2_ref, lng_ref, lnb_ref,
                    o_ref, abuf, bbuf, sem_a, sem_b):
    i = pl.program_id(0)
    n = pl.num_programs(0)
    slot = i % 2
    rows = o_ref.shape[0]

    @pl.when(i == 0)
    def _():
        _start_tile_gather(y_hbm, d1_ref, abuf, sem_a, 0, rows, GATHER_PRIORITIES)
        _start_tile_gather(y_hbm, d2_ref, bbuf, sem_b, 0, rows, GATHER_PRIORITIES)

    @pl.when(i + 1 < n)
    def _():
        _start_tile_gather(y_hbm, d1n_ref, abuf, sem_a, 1 - slot, rows, GATHER_PRIORITIES)
        _start_tile_gather(y_hbm, d2n_ref, bbuf, sem_b, 1 - slot, rows, GATHER_PRIORITIES)

    _wait_tile_gather(y_hbm, d1_ref, abuf, sem_a, slot, rows)
    _wait_tile_gather(y_hbm, d2_ref, bbuf, sem_b, slot, rows)
    route = route_ref[...]
    f = (_unpack_bf16_pairs(_load_token_slabs(abuf.at[slot], rows)) * route[:, ROUTE_G1:ROUTE_G1 + 1]
         + _unpack_bf16_pairs(_load_token_slabs(bbuf.at[slot], rows)) * route[:, ROUTE_G2:ROUTE_G2 + 1])
    o_ref[...] = _layer_norm(DEEPNORM_ALPHA * h1_ref[...] + g2_ref[0] * f, lng_ref[...], lnb_ref[...])


def _combine(y_tiles, dest1, dest2, route, h1, g2, ln_g, ln_b, tokens_per_batch):
    n, d = h1.shape
    t = COMBINE_ROWS
    nt = n // t
    per_b = tokens_per_batch // t
    d1 = dest1.reshape(nt, 1, t)
    d2 = dest2.reshape(nt, 1, t)
    cur = pl.BlockSpec((1, 1, t), lambda i: (i, 0, 0), memory_space=pltpu.SMEM)
    nxt = pl.BlockSpec((1, 1, t), lambda i: (jnp.minimum(i + 1, nt - 1), 0, 0), memory_space=pltpu.SMEM)
    small = lambda arr: pl.BlockSpec(arr.shape, lambda i: (0,) * arr.ndim)
    return pl.pallas_call(
        _combine_kernel,
        grid=(nt,),
        in_specs=[cur, cur, nxt, nxt, pl.BlockSpec(memory_space=pl.ANY),
                  pl.BlockSpec((t, LANES), lambda i: (i, 0)),
                  pl.BlockSpec((t, d), lambda i: (i, 0)),
                  pl.BlockSpec((1, 1, d), lambda i: (i // per_b, 0, 0)),
                  small(ln_g), small(ln_b)],
        out_specs=pl.BlockSpec((t, d), lambda i: (i, 0)),
        out_shape=jax.ShapeDtypeStruct((n, d), F32),
        scratch_shapes=[pltpu.VMEM((2, t * TOKEN_SLAB, LANES), jnp.uint32),
                        pltpu.VMEM((2, t * TOKEN_SLAB, LANES), jnp.uint32),
                        pltpu.SemaphoreType.DMA((2,)), pltpu.SemaphoreType.DMA((2,))],
        compiler_params=_cparams(("arbitrary",)),
        name="combine",
    )(d1, d2, d1, d2, y_tiles, route, h1, g2, ln_g, ln_b)


def _hi_lo(w):
    hi = w.astype(BF16)
    return jnp.stack([hi, (w - hi.astype(F32)).astype(BF16)])


def _block_diag2(w):
    z = jnp.zeros_like(w[0])
    return jnp.concatenate([jnp.concatenate([w[0], z], axis=1), jnp.concatenate([z, w[1]], axis=1)], axis=0)


def kernel(x, c, ctx, c_ctx, emb_ln_g, emb_ln_b, w_mod, b_mod, w_in, tshift_mu, rwkv_w0, rwkv_w2, rwkv_a0, rwkv_a2, rwkv_g2, rwkv_k_k, rwkv_k_a, rwkv_r_k, rwkv_gn_g, rwkv_gn_b, ret_decay, ret_gn_g, ret_gn_b, w_out, ln1_g, ln1_b, router_group, router_group_bias, router_expert, router_expert_bias, expert_w_gate, expert_w_up, expert_w_down, ln2_g, ln2_b):
    assert w_mod.shape[0] == 1, "written for DEPTH == 1 (context outputs are never emitted)"
    b, n_tok, d = x.shape
    n_ctx = ctx.shape[1]
    row = lambda v: v.reshape(1, -1)

    c_rows = jnp.zeros((SUBLANES, d), F32).at[:b].set(c).at[b].set(c_ctx)
    mod = _modulation(c_rows, w_mod[0], row(b_mod[0]))
    sh1, s1, g1, sh2, s2, g2 = [mod[:b, j * d:(j + 1) * d].reshape(b, 1, d) for j in range(6)]
    sh1c, s1c = [jnp.broadcast_to(mod[b, j * d:(j + 1) * d].reshape(1, 1, d), (b, 1, d)) for j in range(2)]

    w_in_bf16 = w_in[0].astype(BF16)
    pr, pt = _in_proj(x, row(emb_ln_g), row(emb_ln_b), s1, sh1, w_in_bf16)
    pr_c, pt_c = _in_proj(ctx, row(emb_ln_g), row(emb_ln_b), s1c, sh1c, w_in_bf16)

    prep_params = (row(tshift_mu[0]), row(rwkv_w0[0]), _hi_lo(_block_diag2(rwkv_w2[0])), row(rwkv_a0[0]),
                   _hi_lo(_block_diag2(rwkv_a2[0])), _hi_lo(rwkv_g2[0]), row(rwkv_k_k[0]), row(rwkv_k_a[0]),
                   row(rwkv_r_k[0]),
                   _segment_ones(D_RWKV, RWKV_HEAD))
    lat = _rwkv_prepare(pr, prep_params, grid_shift=True)
    cx = _rwkv_prepare(pr_c, prep_params, grid_shift=False)
    r_l, v_l, kk_l, w_l, kd_l, bb_l, gate_l, bonus_l = lat
    r_c, v_c, kk_c, w_c, kd_c, bb_c, _, _ = cx

    y_f, y_b = _wkv7((r_l, v_l, kk_l, w_l, kd_l, bb_l), (r_c, v_c, kk_c, w_c, kd_c, bb_c), b, n_tok, n_ctx)

    cos_t, sin_t = _rope_tables(n_tok)
    t_f, t_b = _retention(pt, pt_c, ret_decay[0], cos_t, sin_t)

    wr = jnp.zeros((d, LANES), F32).at[:, :N_GROUPS].set(router_group[0])
    wr = wr.at[:, N_GROUPS:N_GROUPS + N_EXPERTS].set(router_expert[0])
    br = jnp.zeros((1, LANES), F32).at[0, :N_GROUPS].set(router_group_bias[0])
    br = br.at[0, N_GROUPS:N_GROUPS + N_EXPERTS].set(router_expert_bias[0].reshape(-1))
    vecs = (row(emb_ln_g), row(emb_ln_b), g1, s2, sh2, row(rwkv_gn_g[0]), row(rwkv_gn_b[0]),
            row(ret_gn_g[0]), row(ret_gn_b[0]), row(ln1_g[0]), row(ln1_b[0]), br)
    wr_hi_lo = _hi_lo(wr)
    mats = (_segment_ones(D_RWKV, RWKV_HEAD), w_out[0].astype(BF16), wr_hi_lo[0], wr_hi_lo[1])
    h1, u2, route, route_t, counts = _out_proj(x, y_f, y_b, bonus_l, gate_l, t_f, t_b, pt, vecs, mats)

    n_all = b * n_tok

    e1 = route_t[ROUTE_E1].astype(jnp.int32)
    e2 = route_t[ROUTE_E2].astype(jnp.int32)
    cnt = counts[0, :N_EXPERTS].astype(jnp.int32)
    padded = ((cnt + EXPERT_ROWS - 1) // EXPERT_ROWS) * EXPERT_ROWS
    pends = jnp.cumsum(padded)
    pstarts = pends - padded
    expert_ids = jnp.arange(N_EXPERTS, dtype=jnp.int32)
    start_of = lambda e: jnp.sum(jnp.where(e[:, None] == expert_ids[None, :], pstarts[None, :], 0), axis=1)
    dest1 = start_of(e1) + route_t[ROUTE_RANK1].astype(jnp.int32)
    dest2 = start_of(e2) + route_t[ROUTE_RANK2].astype(jnp.int32)
    n_blk = -(-(n_all * 2) // EXPERT_ROWS) + N_EXPERTS
    block_start = jnp.arange(n_blk, dtype=jnp.int32) * EXPERT_ROWS
    block_expert = jnp.minimum(jnp.sum((block_start[:, None] >= pends[None, :]).astype(jnp.int32), axis=1),
                               N_EXPERTS - 1)

    n_used = (pends[N_EXPERTS - 1:] // EXPERT_ROWS).astype(jnp.int32)
    x_slabs = _dispatch(u2, dest1, dest2, n_blk * EXPERT_ROWS)
    y_tiles = _expert_mlp(x_slabs, block_expert, n_used, expert_w_gate[0], expert_w_up[0], expert_w_down[0])
    out = _combine(y_tiles, dest1, dest2, route, h1.reshape(n_all, d), g2, row(ln2_g[0]), row(ln2_b[0]), n_tok)
    return out.reshape(b, n_tok, d)
```

```python
import functools
import math

import jax
import jax.numpy as jnp
import numpy as np
from jax import lax
from jax.experimental import pallas as pl
from jax.experimental.pallas import tpu as pltpu

F32 = jnp.float32
BF16 = jnp.bfloat16
HIGHEST = lax.Precision.HIGHEST

GRID_W = 64
D_RWKV = 512
RWKV_HEAD = 64
RWKV_HEADS = D_RWKV // RWKV_HEAD
DECAY_LORA = 64
AAA_LORA = 64
GATE_LORA = 128
D_RET = 512
RET_HEADS = 4
RET_HEAD = D_RET // RET_HEADS
RET_CHUNK = 128
RWKV_COLS = 3 * D_RWKV + 2 * (DECAY_LORA + AAA_LORA) + GATE_LORA
RET_COLS = 4 * D_RET
N_GROUPS = 4
EXPERTS_PER_GROUP = 8
N_EXPERTS = N_GROUPS * EXPERTS_PER_GROUP
EXPERT_HIDDEN = 512
MOE_BLOCK = 128
ROPE_BASE = 10000.0
LN_EPS = 1e-5
RWKV_GN_EPS = 64e-5
RET_GN_EPS = 1e-5
DEEPNORM_ALPHA = 2.0 ** 0.25
EXP_NEG_HALF = math.exp(-0.5)

LANES = 128
SUBLANES = 8
VMEM_LIMIT_BYTES = 56 * 1024 * 1024

WKV_CHUNK = 64


IN_PROJ_ROWS = 512
OUT_PROJ_ROWS = 512

TOKEN_SLAB = 4
EXPERT_ROWS = 256
COMBINE_ROWS = 256
DISPATCH_ROWS = 256
GATHER_UNROLL = 16
GATHER_PRIORITIES = (0, 1)


def _pack_bf16_pairs(x):
    half = x.shape[1] // 2

    def bf16_bits(v):
        b = lax.bitcast_convert_type(v, jnp.uint32)
        return (b + jnp.uint32(0x7FFF) + ((b >> 16) & jnp.uint32(1))) >> 16

    return bf16_bits(x[:, :half]) | (bf16_bits(x[:, half:]) << 16)


def _unpack_bf16_pairs(p):
    lo = lax.bitcast_convert_type(p << 16, F32)
    hi = lax.bitcast_convert_type(p & jnp.uint32(0xFFFF0000), F32)
    return jnp.concatenate([lo, hi], axis=-1)


def _store_token_slabs(ref, x):
    rows = x.shape[0]
    for j in range(TOKEN_SLAB):
        ref[pl.ds(j, rows, stride=TOKEN_SLAB), :] = x[:, j * LANES:(j + 1) * LANES]


def _load_token_slabs(ref, rows):
    return jnp.concatenate([ref[pl.ds(j, rows, stride=TOKEN_SLAB), :] for j in range(TOKEN_SLAB)], axis=-1)


def _cparams(sem):
    return pltpu.CompilerParams(dimension_semantics=sem, vmem_limit_bytes=VMEM_LIMIT_BYTES)


def _layer_norm(x, g, b, eps=LN_EPS):
    mu = jnp.mean(x, axis=-1, keepdims=True)
    xc = x - mu
    var = jnp.mean(xc * xc, axis=-1, keepdims=True)
    return xc * lax.rsqrt(var + eps) * g + b


def _sigmoid(x):
    return 1.0 / (1.0 + jnp.exp(-x))


def _split_bf16(x):
    hi = x.astype(BF16)
    return hi, (x - hi.astype(F32)).astype(BF16)


def _segsum(x, ones_bf16):
    t = x.shape[0]
    s = jnp.dot(jnp.concatenate(_split_bf16(x), axis=0), ones_bf16, preferred_element_type=F32)
    return s[:t] + s[t:]


def _dot_split(x, w_hi, w_lo):
    hi, lo = _split_bf16(x)
    acc = jnp.dot(hi, w_hi, preferred_element_type=F32)
    acc = acc + jnp.dot(lo, w_hi, preferred_element_type=F32)
    return acc + jnp.dot(hi, w_lo, preferred_element_type=F32)


def _segment_ones(width, seg):
    idx = np.arange(width) // seg
    return jnp.asarray(idx[:, None] == idx[None, :], dtype=BF16)


def _mod_kernel(c_ref, w_ref, b_ref, o_ref):
    c = c_ref[...]
    sc = c * _sigmoid(c)
    o_ref[...] = jnp.dot(sc, w_ref[...], precision=HIGHEST, preferred_element_type=F32) + b_ref[...]


def _modulation(c_rows, w_mod, b_mod):
    rows, d = c_rows.shape
    n = w_mod.shape[1]
    tn = 1536
    return pl.pallas_call(
        _mod_kernel,
        grid=(n // tn,),
        in_specs=[pl.BlockSpec((rows, d), lambda j: (0, 0)),
                  pl.BlockSpec((d, tn), lambda j: (0, j)),
                  pl.BlockSpec((1, tn), lambda j: (0, j))],
        out_specs=pl.BlockSpec((rows, tn), lambda j: (0, j)),
        out_shape=jax.ShapeDtypeStruct((rows, n), F32),
        compiler_params=_cparams(("arbitrary",)),
        name="modulation",
    )(c_rows, w_mod, b_mod)


def _in_proj_kernel(x_ref, g_ref, b_ref, s_ref, sh_ref, w_ref, pr_ref, pt_ref):
    h = _layer_norm(x_ref[0], g_ref[...], b_ref[...])
    u = h * (1.0 + s_ref[0]) + sh_ref[0]
    p = jnp.dot(u.astype(BF16), w_ref[...], preferred_element_type=F32)
    pr_ref[0] = p[:, :RWKV_COLS]
    pt_ref[0] = p[:, RWKV_COLS:]


def _in_proj(x, ln_g, ln_b, s1, sh1, w_in_bf16):
    b, n, d = x.shape
    tm = min(IN_PROJ_ROWS, n)
    cols = w_in_bf16.shape[1]
    return pl.pallas_call(
        _in_proj_kernel,
        grid=(b, n // tm),
        in_specs=[pl.BlockSpec((1, tm, d), lambda bi, i: (bi, i, 0)),
                  pl.BlockSpec((1, d), lambda bi, i: (0, 0)),
                  pl.BlockSpec((1, d), lambda bi, i: (0, 0)),
                  pl.BlockSpec((1, 1, d), lambda bi, i: (bi, 0, 0)),
                  pl.BlockSpec((1, 1, d), lambda bi, i: (bi, 0, 0)),
                  pl.BlockSpec((d, cols), lambda bi, i: (0, 0))],
        out_specs=[pl.BlockSpec((1, tm, RWKV_COLS), lambda bi, i: (bi, i, 0)),
                   pl.BlockSpec((1, tm, RET_COLS), lambda bi, i: (bi, i, 0))],
        out_shape=[jax.ShapeDtypeStruct((b, n, RWKV_COLS), F32),
                   jax.ShapeDtypeStruct((b, n, RET_COLS), F32)],
        compiler_params=_cparams(("arbitrary", "arbitrary")),
        name="in_proj",
    )(x, ln_g, ln_b, s1, sh1, w_in_bf16)


def _rwkv_prepare_kernel(cur_ref, prev_ref, next_ref, mu_ref, w0_ref, w2_ref, a0_ref, a2_ref, g2_ref,
                         kk_scale_ref, ka_ref, rk_ref, ones_ref,
                         r_ref, v_ref, kk_ref, w_ref, kd_ref, bb_ref, g_ref, bonus_ref,
                         *, grid_shift, n_tok):
    cur = cur_ref[0]
    t, c = cur.shape
    row = lax.broadcasted_iota(jnp.int32, (t, c), 0)
    lane = lax.broadcasted_iota(jnp.int32, (t, c), 1)
    prev_tok = pltpu.roll(cur, 1, 0)
    next_tok = pltpu.roll(cur, t - 1, 0)
    if grid_shift:
        col = row & (GRID_W - 1)
        tok = row + pl.program_id(1) * t
        left = jnp.where(col > 0, prev_tok, 0.0)
        right = jnp.where(col < GRID_W - 1, next_tok, 0.0)
        up = jnp.where(tok >= GRID_W, jnp.concatenate([prev_ref[0], cur[:t - GRID_W]], axis=0), 0.0)
        down = jnp.where(tok < n_tok - GRID_W, jnp.concatenate([cur[GRID_W:], next_ref[0]], axis=0), 0.0)
        cm = lane & 3
        shifted = jnp.where(cm == 0, left, jnp.where(cm == 1, right, jnp.where(cm == 2, up, down)))
    else:
        prev_tok = jnp.where(row > 0, prev_tok, 0.0)
        next_tok = jnp.where(row < t - 1, next_tok, 0.0)
        shifted = jnp.where((lane & 1) == 0, prev_tok, next_tok)
    pm = cur + mu_ref[...] * (shifted - cur)

    r = pm[:, 0:D_RWKV]
    k = pm[:, D_RWKV:2 * D_RWKV]
    v = pm[:, 2 * D_RWKV:3 * D_RWKV]
    o = 3 * D_RWKV
    lw = pm[:, o:o + 2 * DECAY_LORA]
    la = pm[:, o + 2 * DECAY_LORA:o + 2 * (DECAY_LORA + AAA_LORA)]
    lg = pm[:, o + 2 * (DECAY_LORA + AAA_LORA):]

    w = w0_ref[...] + _dot_split(jnp.tanh(lw), w2_ref[0], w2_ref[1])
    log_decay = -EXP_NEG_HALF * _sigmoid(w)
    a = _sigmoid(a0_ref[...] + _dot_split(la, a2_ref[0], a2_ref[1]))
    gate = _dot_split(_sigmoid(lg), g2_ref[0], g2_ref[1])

    ones = ones_ref[...]
    kk_raw = k * kk_scale_ref[...]
    kk = kk_raw / jnp.maximum(jnp.sqrt(_segsum(kk_raw * kk_raw, ones)), 1e-12)
    ka = ka_ref[...]
    a0 = a[:, :D_RWKV]
    a1 = a[:, D_RWKV:]
    kd0 = k * (1.0 + (a0 - 1.0) * ka)
    kd1 = k * (1.0 + (a1 - 1.0) * ka)
    bonus = _segsum(r * (kd0 + kd1) * rk_ref[...], ones) * v

    r_ref[0] = r
    v_ref[0] = v
    kk_ref[0] = kk
    w_ref[0] = log_decay
    kd_ref[0] = jnp.concatenate([kd0, kd1], axis=-1)
    bb_ref[0] = jnp.concatenate([kk * a0, kk * a1], axis=-1)
    g_ref[0] = gate
    bonus_ref[0] = bonus


def _rwkv_prepare(pr, params, grid_shift):
    b, n, c = pr.shape
    t = 256
    if not grid_shift:
        assert n == t, "sequence token shift is written for a single tile"
    halo_blocks = n // GRID_W
    per_tile = t // GRID_W
    small = lambda shape: pl.BlockSpec(shape, lambda bi, i: (0,) * len(shape))
    tok_spec = lambda width: pl.BlockSpec((1, t, width), lambda bi, i: (bi, i, 0))
    out_widths = (D_RWKV, D_RWKV, D_RWKV, 2 * D_RWKV, 2 * D_RWKV, 2 * D_RWKV, D_RWKV, D_RWKV)
    kernel = functools.partial(_rwkv_prepare_kernel, grid_shift=grid_shift, n_tok=n)
    return pl.pallas_call(
        kernel,
        grid=(b, n // t),
        in_specs=[tok_spec(c),
                  pl.BlockSpec((1, GRID_W, c), lambda bi, i: (bi, jnp.maximum(i * per_tile - 1, 0), 0)),
                  pl.BlockSpec((1, GRID_W, c),
                               lambda bi, i: (bi, jnp.minimum((i + 1) * per_tile, halo_blocks - 1), 0)),
                  small((1, c)), small((1, 2 * D_RWKV)), small((2, 2 * DECAY_LORA, 2 * D_RWKV)),
                  small((1, 2 * D_RWKV)), small((2, 2 * AAA_LORA, 2 * D_RWKV)), small((2, GATE_LORA, D_RWKV)),
                  small((1, D_RWKV)), small((1, D_RWKV)), small((1, D_RWKV)), small((D_RWKV, D_RWKV))],
        out_specs=[tok_spec(wd) for wd in out_widths],
        out_shape=[jax.ShapeDtypeStruct((b, n, wd), F32) for wd in out_widths],
        compiler_params=_cparams(("arbitrary", "arbitrary")),
        name="rwkv_prepare",
    )(pr, pr, pr, *params)


def _bdot(a, b):
    return jnp.dot(a.astype(BF16), b.astype(BF16), preferred_element_type=F32)


def _bdot_nt(a, b):
    return lax.dot_general(a.astype(BF16), b.astype(BF16), (((1,), (1,)), ((), ())), preferred_element_type=F32)


def _bdot_tn(a, b):
    return lax.dot_general(a.astype(BF16), b.astype(BF16), (((0,), (0,)), ((), ())), preferred_element_type=F32)


def _wkv7_chunk_kernel(*refs, n_ctx_chunks):
    c = WKV_CHUNK
    p = 2 * c
    n_in = 12
    in_refs = (refs[:n_in], refs[n_in:2 * n_in])
    y_refs = refs[2 * n_in:2 * n_in + 2]
    state_ref = refs[2 * n_in + 2]
    n = pl.program_id(0)
    n_batch = in_refs[0][0].shape[0]
    pairs_per_batch = RWKV_HEADS // 2
    pairs_per_dir = n_batch * pairs_per_batch

    @pl.when(n == 0)
    def _():
        state_ref[...] = jnp.zeros_like(state_ref)

    is_ctx = n < n_ctx_chunks
    ti = lax.broadcasted_iota(jnp.int32, (c, c), 0)
    tj = lax.broadcasted_iota(jnp.int32, (c, c), 1)
    ri = lax.broadcasted_iota(jnp.int32, (p, p), 0)
    ci = lax.broadcasted_iota(jnp.int32, (p, p), 1)
    same_head = (ri >= c) == (ci >= c)
    ii = ri & (c - 1)
    jj = ci & (c - 1)
    eye = (ri == ci).astype(F32)
    first = lax.broadcasted_iota(jnp.int32, (c, p), 1) < RWKV_HEAD

    def stack(x):
        return jnp.concatenate([jnp.where(first, x, 0.0), jnp.where(first, 0.0, x)], axis=0)

    def unstack(x):
        return x[:c] + x[c:]

    a_st, r_st, k_st, b_st, k2_st, b2_st, v_st, g_chunk, earlier, upto_self = ([] for _ in range(10))
    for d in range(2):
        r_l, v_l, kk_l, lw_l, kd_l, bb_l, r_c, v_c, kk_c, lw_c, kd_c, bb_c = in_refs[d]
        pick = lambda xc, xl: jnp.concatenate(
            [jnp.where(is_ctx, xc[bi], xl[bi]) for bi in range(n_batch)], axis=-1)
        r, v, kk, lw, kd, bb = (pick(r_c, r_l), pick(v_c, v_l), pick(kk_c, kk_l), pick(lw_c, lw_l),
                                pick(kd_c, kd_l), pick(bb_c, bb_l))
        before = (tj < ti) if d == 0 else (tj > ti)
        upto = (before | (ti == tj)).astype(BF16)
        hi = lw.astype(BF16)
        r1 = lw - hi.astype(F32)
        mid = r1.astype(BF16)
        lo = (r1 - mid.astype(F32)).astype(BF16)
        cum = (jnp.dot(upto, hi, preferred_element_type=F32) + jnp.dot(upto, mid, preferred_element_type=F32)
               + jnp.dot(upto, lo, preferred_element_type=F32))
        tot = jnp.sum(lw, axis=0, keepdims=True)
        e_neg = jnp.exp(-cum)
        e_rem = jnp.exp(tot - cum)
        alpha = kk * jnp.exp(cum - lw)
        rho = r * jnp.exp(cum)
        beta = bb * e_neg
        kappa = kd * e_neg
        kappa_rem = kd * e_rem
        beta_rem = bb * e_rem
        g_all = jnp.exp(tot)
        pair_before = same_head & ((jj < ii) if d == 0 else (jj > ii))
        pair_upto = pair_before | (ri == ci)
        for hp in range(pairs_per_dir):
            sl = slice(hp * p, (hp + 1) * p)
            a_st.append(stack(alpha[:, sl]))
            r_st.append(stack(rho[:, sl]))
            k_st.append(stack(kappa[:, sl]))
            b_st.append(stack(beta[:, sl]))
            k2_st.append(stack(kappa_rem[:, sl]))
            b2_st.append(stack(beta_rem[:, sl]))
            v_st.append(stack(v[:, sl]))
            g_chunk.append(g_all[:, sl])
            earlier.append(pair_before)
            upto_self.append(pair_upto)

    pairs = range(2 * pairs_per_dir)
    g = [_bdot_nt(jnp.concatenate([a_st[h], r_st[h]], axis=0), jnp.concatenate([k_st[h], b_st[h]], axis=0))
         for h in pairs]
    m1 = [jnp.where(earlier[h], g[h][:p, :p], 0.0) for h in pairs]
    m2 = [jnp.where(earlier[h], g[h][:p, p:], 0.0) for h in pairs]
    n1 = [jnp.where(upto_self[h], g[h][p:, :p], 0.0) for h in pairs]
    n2 = [jnp.where(upto_self[h], g[h][p:, p:], 0.0) for h in pairs]

    in_block = (ii >> 3) == (jj >> 3)
    pw = [-jnp.where(in_block, m2[h], 0.0) for h in pairs]
    inv = [eye + pw[h] for h in pairs]
    pw = [_bdot(pw[h], pw[h]) for h in pairs]
    both = [_bdot(jnp.concatenate([inv[h], pw[h]], axis=0), pw[h]) for h in pairs]
    inv = [inv[h] + both[h][:p] for h in pairs]
    inv = [inv[h] + _bdot(inv[h], both[h][p:]) for h in pairs]
    for sh in (3, 4, 5):
        off = ((ii >> (sh + 1)) == (jj >> (sh + 1))) & ((ii >> sh) != (jj >> sh))
        left = [_bdot(inv[h], jnp.where(off, m2[h], 0.0)) for h in pairs]
        inv = [inv[h] - _bdot(left[h], inv[h]) for h in pairs]

    mnv = [_bdot(jnp.concatenate([m1[h], n1[h]], axis=0), v_st[h]) for h in pairs]
    m1v = [mnv[h][:p] for h in pairs]
    n1v = [mnv[h][p:] for h in pairs]
    au = [_bdot(inv[h], jnp.concatenate([a_st[h], m1v[h]], axis=1)) for h in pairs]
    nn = [_bdot(n2[h], au[h]) for h in pairs]
    pc = [_bdot_tn(b2_st[h], au[h][:, :p]) for h in pairs]
    qc_t = [_bdot_tn(jnp.concatenate([v_st[h], -au[h][:, p:]], axis=0),
                     jnp.concatenate([k2_st[h], b2_st[h]], axis=0)) for h in pairs]
    s0 = [state_ref[h] for h in pairs]
    y = [_bdot_nt(unstack(r_st[h] - nn[h][:, :p]), s0[h]) + unstack(n1v[h] - nn[h][:, p:]) for h in pairs]
    s_dec = [_bdot_nt(s0[h], pc[h]) for h in pairs]
    for h in pairs:
        d, hp = divmod(h, pairs_per_dir)
        bi, hpb = divmod(hp, pairs_per_batch)
        y_refs[d][bi, :, hpb * p:(hpb + 1) * p] = y[h]
        state_ref[h] = s0[h] * g_chunk[h] - s_dec[h] + qc_t[h]


def _wkv7(lat, ctx, b, n_tok, n_ctx):
    c = WKV_CHUNK
    ncx = n_ctx // c
    nl = n_tok // c
    lat_idx = (lambda n: jnp.maximum(n - ncx, 0), lambda n: nl - 1 - jnp.maximum(n - ncx, 0))
    ctx_idx = (lambda n: jnp.minimum(n, ncx - 1), lambda n: ncx - 1 - jnp.minimum(n, ncx - 1))

    def specs(idx, d):
        shared = pl.BlockSpec((b, c, D_RWKV), lambda n: (0, idx(n), 0))
        per_dir = pl.BlockSpec((b, c, D_RWKV), lambda n: (0, idx(n), d))
        return [shared, shared, shared, per_dir, per_dir, per_dir]

    in_specs, args = [], []
    for d in range(2):
        in_specs += specs(lat_idx[d], d) + specs(ctx_idx[d], d)
        args += list(lat) + list(ctx)
    return pl.pallas_call(
        functools.partial(_wkv7_chunk_kernel, n_ctx_chunks=ncx),
        grid=(ncx + nl,),
        in_specs=in_specs,
        out_specs=[pl.BlockSpec((b, c, D_RWKV), lambda n, d=d: (0, lat_idx[d](n), 0)) for d in range(2)],
        out_shape=[jax.ShapeDtypeStruct((b, n_tok, D_RWKV), F32)] * 2,
        scratch_shapes=[pltpu.VMEM((2 * b * RWKV_HEADS // 2, 2 * RWKV_HEAD, 2 * RWKV_HEAD), F32)],
        compiler_params=_cparams(("arbitrary",)),
        name="wkv7_chunk",
    )(*args)


def _rope(z, cos_t, sin_t):
    lane = lax.broadcasted_iota(jnp.int32, z.shape, 1)
    half = RET_HEAD // 4
    partner = jnp.where((lane & (2 * half - 1)) < half, pltpu.roll(z, RET_HEAD - half, 1), pltpu.roll(z, half, 1))
    return z * cos_t + partner * sin_t


def _retention_kernel(dec_ref, fwd_ref, bwd_ref, ctx_ref, cosf_ref, sinf_ref, cosb_ref, sinb_ref,
                      yf_ref, yb_ref, state_ref, dmat_ref, tail_ref, head_ref, cdec_ref):
    c = RET_CHUNK
    scale = RET_HEAD ** -0.5
    ii = lax.broadcasted_iota(jnp.int32, (c, c), 0)
    jj = lax.broadcasted_iota(jnp.int32, (c, c), 1)
    pos = lax.broadcasted_iota(jnp.int32, (c, RET_HEAD), 0).astype(F32)
    n_ctx_chunks = ctx_ref.shape[1] // c

    def head_slices(ref_val, h):
        q = ref_val[:, h * RET_HEAD:(h + 1) * RET_HEAD]
        k = ref_val[:, D_RET + h * RET_HEAD:D_RET + (h + 1) * RET_HEAD]
        v = ref_val[:, 2 * D_RET + h * RET_HEAD:2 * D_RET + (h + 1) * RET_HEAD]
        return q, k, v

    n_batch = fwd_ref.shape[0]
    heads = [(d, h) for d in range(2) for h in range(RET_HEADS)]
    chains = [(bi, d, h) for bi in range(n_batch) for d, h in heads]

    @pl.when(pl.program_id(0) == 0)
    def _():
        for d, h in heads:
            x = jnp.full((1, RET_HEAD), dec_ref[d, h], F32)
            lg = -(jnp.maximum(x, 0.0) + jnp.log(1.0 + jnp.exp(-jnp.abs(x))))
            chunk_decay = jnp.exp(lg * float(c))
            tail = jnp.exp(lg * ((c - 1.0 - pos) if d == 0 else pos))
            rel = (ii - jj) if d == 0 else (jj - ii)
            mask = (rel >= 0) if d == 0 else (rel > 0)
            dmat_ref[d, h] = jnp.where(mask, jnp.exp(lg * jnp.maximum(rel, 0).astype(F32)), 0.0)
            tail_ref[d, h] = tail
            head_ref[d, h] = jnp.exp(lg * ((pos + 1.0) if d == 0 else (c - pos)))
            cdec_ref[d, h] = jnp.broadcast_to(chunk_decay, (SUBLANES, RET_HEAD))
            order = range(n_ctx_chunks) if d == 0 else range(n_ctx_chunks - 1, -1, -1)
            for bi in range(n_batch):
                s = jnp.zeros((RET_HEAD, RET_HEAD), F32)
                for cc in order:
                    _, kc, vc = head_slices(ctx_ref[bi, cc * c:(cc + 1) * c, :], h)
                    s = s * chunk_decay + _bdot_tn(kc * scale * tail, vc)
                state_ref[bi, d, h] = s

    qkv = []
    for bi, d, h in chains:
        blk = fwd_ref[bi] if d == 0 else bwd_ref[bi]
        cos_t = cosf_ref[...] if d == 0 else cosb_ref[...]
        sin_t = sinf_ref[...] if d == 0 else sinb_ref[...]
        q, k, v = head_slices(blk, h)
        qkv.append((_rope(q, cos_t, sin_t), _rope(k, cos_t, sin_t) * scale, v.astype(BF16)))
    s0 = [state_ref[bi, d, h] for bi, d, h in chains]
    scores = [_bdot_nt(q, k) for q, k, _ in qkv]
    inner = [_bdot(scores[i] * dmat_ref[d, h], qkv[i][2]) for i, (_, d, h) in enumerate(chains)]
    cross = [_bdot(qkv[i][0] * head_ref[d, h], s0[i]) for i, (_, d, h) in enumerate(chains)]
    upd = [_bdot_tn(qkv[i][1] * tail_ref[d, h], qkv[i][2]) for i, (_, d, h) in enumerate(chains)]
    for i, (bi, d, h) in enumerate(chains):
        state_ref[bi, d, h] = s0[i] * cdec_ref[d, h, 0:1, :] + upd[i]
        out_ref = yf_ref if d == 0 else yb_ref
        out_ref[bi, :, h * RET_HEAD:(h + 1) * RET_HEAD] = inner[i] + cross[i]


def _retention(pt, pt_ctx, ret_decay, cos_t, sin_t):
    b, n, _ = pt.shape
    c = RET_CHUNK
    nc = n // c
    qkv = 3 * D_RET
    fwd = lambda i: (0, i, 0)
    bwd = lambda i: (0, nc - 1 - i, 0)
    return pl.pallas_call(
        _retention_kernel,
        grid=(nc,),
        in_specs=[pl.BlockSpec(memory_space=pltpu.SMEM),
                  pl.BlockSpec((b, c, qkv), fwd),
                  pl.BlockSpec((b, c, qkv), bwd),
                  pl.BlockSpec((b, pt_ctx.shape[1], qkv), lambda i: (0, 0, 0)),
                  pl.BlockSpec((c, RET_HEAD), lambda i: (i, 0)),
                  pl.BlockSpec((c, RET_HEAD), lambda i: (i, 0)),
                  pl.BlockSpec((c, RET_HEAD), lambda i: (nc - 1 - i, 0)),
                  pl.BlockSpec((c, RET_HEAD), lambda i: (nc - 1 - i, 0))],
        out_specs=[pl.BlockSpec((b, c, D_RET), fwd), pl.BlockSpec((b, c, D_RET), bwd)],
        out_shape=[jax.ShapeDtypeStruct((b, n, D_RET), F32), jax.ShapeDtypeStruct((b, n, D_RET), F32)],
        scratch_shapes=[pltpu.VMEM((b, 2, RET_HEADS, RET_HEAD, RET_HEAD), F32),
                        pltpu.VMEM((2, RET_HEADS, c, c), F32),
                        pltpu.VMEM((2, RET_HEADS, c, RET_HEAD), F32),
                        pltpu.VMEM((2, RET_HEADS, c, RET_HEAD), F32),
                        pltpu.VMEM((2, RET_HEADS, SUBLANES, RET_HEAD), F32)],
        compiler_params=_cparams(("arbitrary",)),
        name="retention",
    )(ret_decay, pt, pt, pt_ctx, cos_t, sin_t, cos_t, sin_t)


def _rope_tables(n_tok):
    nf = RET_HEAD // 4
    lane = np.arange(RET_HEAD)
    inv = ROPE_BASE ** (-jnp.arange(nf, dtype=F32) / nf)
    t = jnp.arange(n_tok)
    pos = jnp.where((lane // (2 * nf) == 0)[None, :], (t // GRID_W)[:, None], (t % GRID_W)[:, None]).astype(F32)
    ang = pos * inv[lane % nf][None, :]
    sign = jnp.where((lane % (2 * nf)) < nf, -1.0, 1.0).astype(F32)
    return jnp.cos(ang), jnp.sin(ang) * sign[None, :]


def _group_norm(y, ones, seg, eps, g, b):
    mu = _segsum(y, ones) * (1.0 / seg)
    yc = y - mu
    var = _segsum(yc * yc, ones) * (1.0 / seg)
    return yc * lax.rsqrt(var + eps) * g + b


def _out_proj_kernel(x_ref, yf_ref, yb_ref, bonus_ref, gate_ref, tf_ref, tb_ref, gt_ref,
                     embg_ref, embb_ref, g1_ref, s2_ref, sh2_ref, rgn_g_ref, rgn_b_ref, tgn_g_ref, tgn_b_ref,
                     ones_r_ref, wout_ref, ln1g_ref, ln1b_ref, wrh_ref, wrl_ref, br_ref,
                     h1_ref, u2_ref, route_ref, route_t_ref, count_ref, carry_ref):
    @pl.when((pl.program_id(0) == 0) & (pl.program_id(1) == 0))
    def _():
        carry_ref[...] = jnp.zeros_like(carry_ref)

    y = yf_ref[0] + yb_ref[0]
    o_rwkv = _group_norm(y, ones_r_ref[...], RWKV_HEAD, RWKV_GN_EPS, rgn_g_ref[...], rgn_b_ref[...])
    o_rwkv = (o_rwkv + bonus_ref[0]) * gate_ref[0]
    yt = tf_ref[0] + tb_ref[0]
    gt = gt_ref[0]
    tgn_g = tgn_g_ref[...]
    tgn_b = tgn_b_ref[...]
    o_ret = jnp.concatenate(
        [_layer_norm(yt[:, h * RET_HEAD:(h + 1) * RET_HEAD], tgn_g[:, h * RET_HEAD:(h + 1) * RET_HEAD],
                     tgn_b[:, h * RET_HEAD:(h + 1) * RET_HEAD], RET_GN_EPS) for h in range(RET_HEADS)], axis=-1)
    o_ret = o_ret * (gt * _sigmoid(gt))
    cat = jnp.concatenate([o_rwkv, o_ret], axis=-1).astype(BF16)
    mix = jnp.dot(cat, wout_ref[...], preferred_element_type=F32)
    h = _layer_norm(x_ref[0], embg_ref[...], embb_ref[...])
    h1 = _layer_norm(DEEPNORM_ALPHA * h + g1_ref[0] * mix, ln1g_ref[...], ln1b_ref[...])
    u2 = h1 * (1.0 + s2_ref[0]) + sh2_ref[0]
    h1_ref[0] = h1
    _store_token_slabs(u2_ref, _pack_bf16_pairs(u2))
    route = _route_tile(_dot_split(u2, wrh_ref[...], wrl_ref[...]) + br_ref[...], carry_ref)
    route_ref[...] = route
    route_t_ref[...] = route.T[:SUBLANES]
    count_ref[...] = carry_ref[...]


def _out_proj(x, y_f, y_b, bonus, gate, t_f, t_b, pt, vecs, mats):
    b, n, d = x.shape
    t = OUT_PROJ_ROWS
    tok = lambda width: pl.BlockSpec((1, t, width), lambda bi, i: (bi, i, 0))
    per_b = pl.BlockSpec((1, 1, d), lambda bi, i: (bi, 0, 0))
    small = lambda arr: pl.BlockSpec(arr.shape, lambda bi, i: (0,) * arr.ndim)
    (embg, embb, g1, s2, sh2, rgn_g, rgn_b, tgn_g, tgn_b, ln1g, ln1b, br) = vecs
    (ones_r, wout, wr_hi, wr_lo) = mats
    gt_spec = pl.BlockSpec((1, t, D_RET), lambda bi, i: (bi, i, 3))
    tiles = n // t
    flat = lambda bi, i: (bi * tiles + i, 0)
    args = (x, y_f, y_b, bonus, gate, t_f, t_b, pt, embg, embb, g1, s2, sh2, rgn_g, rgn_b, tgn_g, tgn_b,
            ones_r, wout, ln1g, ln1b, wr_hi, wr_lo, br)
    in_specs = [tok(d)] + [tok(D_RWKV)] * 6 + [gt_spec, small(embg), small(embb), per_b, per_b, per_b,
                                                small(rgn_g), small(rgn_b), small(tgn_g), small(tgn_b),
                                                small(ones_r), small(wout), small(ln1g),
                                                small(ln1b), small(wr_hi), small(wr_lo), small(br)]
    return pl.pallas_call(
        _out_proj_kernel,
        grid=(b, n // t),
        in_specs=in_specs,
        out_specs=[tok(d), pl.BlockSpec((t * TOKEN_SLAB, LANES), flat),
                   pl.BlockSpec((t, LANES), flat),
                   pl.BlockSpec((SUBLANES, t), lambda bi, i: (0, bi * tiles + i)),
                   pl.BlockSpec((SUBLANES, LANES), lambda bi, i: (0, 0))],
        out_shape=[jax.ShapeDtypeStruct((b, n, d), F32),
                   jax.ShapeDtypeStruct((b * n * TOKEN_SLAB, LANES), jnp.uint32),
                   jax.ShapeDtypeStruct((b * n, LANES), F32),
                   jax.ShapeDtypeStruct((SUBLANES, b * n), F32),
                   jax.ShapeDtypeStruct((SUBLANES, LANES), F32)],
        scratch_shapes=[pltpu.VMEM((SUBLANES, LANES), F32)],
        compiler_params=_cparams(("arbitrary", "arbitrary")),
        name="out_proj",
    )(*args)


ROUTE_E1, ROUTE_E2, ROUTE_G1, ROUTE_G2, ROUTE_RANK1, ROUTE_RANK2 = range(6)


def _lane_argmax(x, valid, lane):
    m = jnp.max(jnp.where(valid, x, -jnp.inf), axis=-1, keepdims=True)
    idx = jnp.min(jnp.where(valid & (x == m), lane, float(LANES)), axis=-1, keepdims=True)
    return m, idx


def _route_tile(lg, carry_ref):
    t = lg.shape[0]
    lane = lax.broadcasted_iota(jnp.int32, lg.shape, 1).astype(F32)
    gmask = lane < N_GROUPS
    gmax = jnp.max(jnp.where(gmask, lg, -jnp.inf), axis=-1, keepdims=True)
    gexp = jnp.where(gmask, jnp.exp(lg - gmax), 0.0)
    gp = gexp / jnp.sum(gexp, axis=-1, keepdims=True)
    g_w, g_i = _lane_argmax(gp, gmask, lane)

    lo = N_GROUPS + EXPERTS_PER_GROUP * g_i
    emask = (lane >= lo) & (lane < lo + EXPERTS_PER_GROUP)
    emax = jnp.max(jnp.where(emask, lg, -jnp.inf), axis=-1, keepdims=True)
    eexp = jnp.where(emask, jnp.exp(lg - emax), 0.0)
    ep = eexp / jnp.sum(eexp, axis=-1, keepdims=True)
    p1, i1 = _lane_argmax(ep, emask, lane)
    p2, i2 = _lane_argmax(ep, emask & (lane != i1), lane)
    denom = p1 + p2
    gate1 = g_w * p1 / denom
    gate2 = g_w * p2 / denom
    e1 = i1 - N_GROUPS
    e2 = i2 - N_GROUPS

    oh1 = (lane == e1).astype(F32)
    oh2 = (lane == e2).astype(F32)
    cnt = oh1 + oh2
    ri = lax.broadcasted_iota(jnp.int32, (t, t), 0)
    ci = lax.broadcasted_iota(jnp.int32, (t, t), 1)
    before = (ci < ri).astype(BF16)
    seen = jnp.dot(before, cnt.astype(BF16), preferred_element_type=F32) + carry_ref[0:1, :]
    rank1 = jnp.sum(oh1 * seen, axis=-1, keepdims=True)
    rank2 = jnp.sum(oh2 * seen, axis=-1, keepdims=True)
    carry_ref[0:1, :] = carry_ref[0:1, :] + jnp.sum(cnt, axis=0, keepdims=True)

    out = jnp.zeros(lg.shape, F32)
    for slot, val in ((ROUTE_E1, e1.astype(F32)), (ROUTE_E2, e2.astype(F32)), (ROUTE_G1, gate1),
                      (ROUTE_G2, gate2), (ROUTE_RANK1, rank1), (ROUTE_RANK2, rank2)):
        out = jnp.where(lane == slot, val, out)
    return out


def _tile_gather_copy(src_hbm, idx_ref, buf, sem, slot, r):
    src = src_hbm.at[pl.ds(pl.multiple_of(idx_ref[0, 0, r] * TOKEN_SLAB, TOKEN_SLAB), TOKEN_SLAB), :]
    dst = buf.at[slot, pl.ds(pl.multiple_of(r * TOKEN_SLAB, TOKEN_SLAB), TOKEN_SLAB), :]
    return pltpu.make_async_copy(src, dst, sem.at[slot])


def _start_tile_gather(src_hbm, idx_ref, buf, sem, slot, rows, priorities):
    def body(g, carry):
        for j in range(GATHER_UNROLL):
            copy = _tile_gather_copy(src_hbm, idx_ref, buf, sem, slot, g * GATHER_UNROLL + j)
            copy.start(priority=priorities[j % len(priorities)])
        return carry
    lax.fori_loop(0, rows // GATHER_UNROLL, body, 0)


def _wait_tile_gather(src_hbm, idx_ref, buf, sem, slot, rows):
    del idx_ref
    whole = src_hbm.at[pl.ds(0, rows * TOKEN_SLAB), :]
    pltpu.make_async_copy(whole, buf.at[slot], sem.at[slot]).wait()


def _dispatch_kernel(d1_ref, d2_ref, u_ref, x_init_hbm, x_hbm, sem):
    del x_init_hbm
    rows = d1_ref.shape[2]

    def slab(ref, index):
        return ref.at[pl.ds(pl.multiple_of(index * TOKEN_SLAB, TOKEN_SLAB), TOKEN_SLAB), :]

    def body(g, carry):
        for j in range(GATHER_UNROLL):
            r = g * GATHER_UNROLL + j
            src = slab(u_ref, r)
            pltpu.make_async_copy(src, slab(x_hbm, d1_ref[0, 0, r]), sem).start(priority=j % 2)
            pltpu.make_async_copy(src, slab(x_hbm, d2_ref[0, 0, r]), sem).start(priority=(j + 1) % 2)
        return carry

    lax.fori_loop(0, rows // GATHER_UNROLL, body, 0)
    whole = x_hbm.at[pl.ds(0, rows * TOKEN_SLAB), :]
    pltpu.make_async_copy(u_ref, whole, sem).wait()
    pltpu.make_async_copy(u_ref, whole, sem).wait()


def _dispatch(u2_slabs, dest1, dest2, n_slot):
    n = dest1.shape[0]
    t = DISPATCH_ROWS
    nt = n // t
    idx = pl.BlockSpec((1, 1, t), lambda i: (i, 0, 0), memory_space=pltpu.SMEM)
    any_spec = pl.BlockSpec(memory_space=pl.ANY)
    x_init = jnp.zeros((n_slot * TOKEN_SLAB, LANES), jnp.uint32)
    return pl.pallas_call(
        _dispatch_kernel,
        grid=(nt,),
        in_specs=[idx, idx, pl.BlockSpec((t * TOKEN_SLAB, LANES), lambda i: (i, 0)), any_spec],
        out_specs=any_spec,
        out_shape=jax.ShapeDtypeStruct(x_init.shape, x_init.dtype),
        scratch_shapes=[pltpu.SemaphoreType.DMA(())],
        input_output_aliases={3: 0},
        compiler_params=_cparams(("arbitrary",)),
        name="dispatch",
    )(dest1.reshape(nt, 1, t), dest2.reshape(nt, 1, t), u2_slabs, x_init)


def _expert_kernel(blk_expert_ref, n_used_ref, x_ref, wg_ref, wu_ref, wd_ref, y_ref, wg_s, wu_s, wd_s):
    i = pl.program_id(0)
    n_used = n_used_ref[0]

    @pl.when(i >= n_used)
    def _():
        y_ref[...] = jnp.zeros_like(y_ref)

    @pl.when(i < n_used)
    def _():
        @pl.when((i == 0) | (blk_expert_ref[i] != blk_expert_ref[jnp.maximum(i - 1, 0)]))
        def _():
            wg_s[...] = wg_ref[0].astype(BF16)
            wu_s[...] = wu_ref[0].astype(BF16)
            wd_s[...] = wd_ref[0].astype(BF16)

        x = _unpack_bf16_pairs(_load_token_slabs(x_ref, EXPERT_ROWS)).astype(BF16)
        hg = jnp.dot(x, wg_s[...], preferred_element_type=F32)
        hu = jnp.dot(x, wu_s[...], preferred_element_type=F32)
        act = (hg * _sigmoid(hg) * hu).astype(BF16)
        _store_token_slabs(y_ref, _pack_bf16_pairs(jnp.dot(act, wd_s[...], preferred_element_type=F32)))


def _expert_mlp(x_slabs, block_expert, n_used, w_gate, w_up, w_down):
    n_blk = block_expert.shape[0]
    d = w_gate.shape[1]
    hdim = w_gate.shape[2]
    rows = EXPERT_ROWS
    used = lambda i, nu: jnp.minimum(i, nu[0] - 1)
    weight = lambda i, be, nu: (be[used(i, nu)], 0, 0)
    grid_spec = pltpu.PrefetchScalarGridSpec(
        num_scalar_prefetch=2,
        grid=(n_blk,),
        in_specs=[pl.BlockSpec((rows * TOKEN_SLAB, LANES), lambda i, be, nu: (used(i, nu), 0)),
                  pl.BlockSpec((1, d, hdim), weight),
                  pl.BlockSpec((1, d, hdim), weight),
                  pl.BlockSpec((1, hdim, d), weight)],
        out_specs=pl.BlockSpec((rows * TOKEN_SLAB, LANES), lambda i, be, nu: (i, 0)),
        scratch_shapes=[pltpu.VMEM((d, hdim), BF16), pltpu.VMEM((d, hdim), BF16), pltpu.VMEM((hdim, d), BF16)],
    )
    return pl.pallas_call(
        _expert_kernel,
        grid_spec=grid_spec,
        out_shape=jax.ShapeDtypeStruct((n_blk * rows * TOKEN_SLAB, LANES), jnp.uint32),
        compiler_params=_cparams(("arbitrary",)),
        name="expert_mlp",
    )(block_expert, n_used, x_slabs, w_gate, w_up, w_down)


def _combine_kernel(d1_ref, d2_ref, d1n_ref, d2n_ref, y_hbm, route_ref, h1_ref, g2_ref, lng_ref, lnb_ref,
                    o_ref, abuf, bbuf, sem_a, sem_b):
    i = pl.program_id(0)
    n = pl.num_programs(0)
    slot = i % 2
    rows = o_ref.shape[0]

    @pl.when(i == 0)
    def _():
        _start_tile_gather(y_hbm, d1_ref, abuf, sem_a, 0, rows, GATHER_PRIORITIES)
        _start_tile_gather(y_hbm, d2_ref, bbuf, sem_b, 0, rows, GATHER_PRIORITIES)

    @pl.when(i + 1 < n)
    def _():
        _start_tile_gather(y_hbm, d1n_ref, abuf, sem_a, 1 - slot, rows, GATHER_PRIORITIES)
        _start_tile_gather(y_hbm, d2n_ref, bbuf, sem_b, 1 - slot, rows, GATHER_PRIORITIES)

    _wait_tile_gather(y_hbm, d1_ref, abuf, sem_a, slot, rows)
    _wait_tile_gather(y_hbm, d2_ref, bbuf, sem_b, slot, rows)
    route = route_ref[...]
    f = (_unpack_bf16_pairs(_load_token_slabs(abuf.at[slot], rows)) * route[:, ROUTE_G1:ROUTE_G1 + 1]
         + _unpack_bf16_pairs(_load_token_slabs(bbuf.at[slot], rows)) * route[:, ROUTE_G2:ROUTE_G2 + 1])
    o_ref[...] = _layer_norm(DEEPNORM_ALPHA * h1_ref[...] + g2_ref[0] * f, lng_ref[...], lnb_ref[...])


def _combine(y_tiles, dest1, dest2, route, h1, g2, ln_g, ln_b, tokens_per_batch):
    n, d = h1.shape
    t = COMBINE_ROWS
    nt = n // t
    per_b = tokens_per_batch // t
    d1 = dest1.reshape(nt, 1, t)
    d2 = dest2.reshape(nt, 1, t)
    cur = pl.BlockSpec((1, 1, t), lambda i: (i, 0, 0), memory_space=pltpu.SMEM)
    nxt = pl.BlockSpec((1, 1, t), lambda i: (jnp.minimum(i + 1, nt - 1), 0, 0), memory_space=pltpu.SMEM)
    small = lambda arr: pl.BlockSpec(arr.shape, lambda i: (0,) * arr.ndim)
    return pl.pallas_call(
        _combine_kernel,
        grid=(nt,),
        in_specs=[cur, cur, nxt, nxt, pl.BlockSpec(memory_space=pl.ANY),
                  pl.BlockSpec((t, LANES), lambda i: (i, 0)),
                  pl.BlockSpec((t, d), lambda i: (i, 0)),
                  pl.BlockSpec((1, 1, d), lambda i: (i // per_b, 0, 0)),
                  small(ln_g), small(ln_b)],
        out_specs=pl.BlockSpec((t, d), lambda i: (i, 0)),
        out_shape=jax.ShapeDtypeStruct((n, d), F32),
        scratch_shapes=[pltpu.VMEM((2, t * TOKEN_SLAB, LANES), jnp.uint32),
                        pltpu.VMEM((2, t * TOKEN_SLAB, LANES), jnp.uint32),
                        pltpu.SemaphoreType.DMA((2,)), pltpu.SemaphoreType.DMA((2,))],
        compiler_params=_cparams(("arbitrary",)),
        name="combine",
    )(d1, d2, d1, d2, y_tiles, route, h1, g2, ln_g, ln_b)


def _hi_lo(w):
    hi = w.astype(BF16)
    return jnp.stack([hi, (w - hi.astype(F32)).astype(BF16)])


def _block_diag2(w):
    z = jnp.zeros_like(w[0])
    return jnp.concatenate([jnp.concatenate([w[0], z], axis=1), jnp.concatenate([z, w[1]], axis=1)], axis=0)


def kernel(x, c, ctx, c_ctx, emb_ln_g, emb_ln_b, w_mod, b_mod, w_in, tshift_mu, rwkv_w0, rwkv_w2, rwkv_a0, rwkv_a2, rwkv_g2, rwkv_k_k, rwkv_k_a, rwkv_r_k, rwkv_gn_g, rwkv_gn_b, ret_decay, ret_gn_g, ret_gn_b, w_out, ln1_g, ln1_b, router_group, router_group_bias, router_expert, router_expert_bias, expert_w_gate, expert_w_up, expert_w_down, ln2_g, ln2_b):
    assert w_mod.shape[0] == 1, "written for DEPTH == 1 (context outputs are never emitted)"
    b, n_tok, d = x.shape
    n_ctx = ctx.shape[1]
    row = lambda v: v.reshape(1, -1)

    c_rows = jnp.zeros((SUBLANES, d), F32).at[:b].set(c).at[b].set(c_ctx)
    mod = _modulation(c_rows, w_mod[0], row(b_mod[0]))
    sh1, s1, g1, sh2, s2, g2 = [mod[:b, j * d:(j + 1) * d].reshape(b, 1, d) for j in range(6)]
    sh1c, s1c = [jnp.broadcast_to(mod[b, j * d:(j + 1) * d].reshape(1, 1, d), (b, 1, d)) for j in range(2)]

    w_in_bf16 = w_in[0].astype(BF16)
    pr, pt = _in_proj(x, row(emb_ln_g), row(emb_ln_b), s1, sh1, w_in_bf16)
    pr_c, pt_c = _in_proj(ctx, row(emb_ln_g), row(emb_ln_b), s1c, sh1c, w_in_bf16)

    prep_params = (row(tshift_mu[0]), row(rwkv_w0[0]), _hi_lo(_block_diag2(rwkv_w2[0])), row(rwkv_a0[0]),
                   _hi_lo(_block_diag2(rwkv_a2[0])), _hi_lo(rwkv_g2[0]), row(rwkv_k_k[0]), row(rwkv_k_a[0]),
                   row(rwkv_r_k[0]),
                   _segment_ones(D_RWKV, RWKV_HEAD))
    lat = _rwkv_prepare(pr, prep_params, grid_shift=True)
    cx = _rwkv_prepare(pr_c, prep_params, grid_shift=False)
    r_l, v_l, kk_l, w_l, kd_l, bb_l, gate_l, bonus_l = lat
    r_c, v_c, kk_c, w_c, kd_c, bb_c, _, _ = cx

    y_f, y_b = _wkv7((r_l, v_l, kk_l, w_l, kd_l, bb_l), (r_c, v_c, kk_c, w_c, kd_c, bb_c), b, n_tok, n_ctx)

    cos_t, sin_t = _rope_tables(n_tok)
    t_f, t_b = _retention(pt, pt_c, ret_decay[0], cos_t, sin_t)

    wr = jnp.zeros((d, LANES), F32).at[:, :N_GROUPS].set(router_group[0])
    wr = wr.at[:, N_GROUPS:N_GROUPS + N_EXPERTS].set(router_expert[0])
    br = jnp.zeros((1, LANES), F32).at[0, :N_GROUPS].set(router_group_bias[0])
    br = br.at[0, N_GROUPS:N_GROUPS + N_EXPERTS].set(router_expert_bias[0].reshape(-1))
    vecs = (row(emb_ln_g), row(emb_ln_b), g1, s2, sh2, row(rwkv_gn_g[0]), row(rwkv_gn_b[0]),
            row(ret_gn_g[0]), row(ret_gn_b[0]), row(ln1_g[0]), row(ln1_b[0]), br)
    wr_hi_lo = _hi_lo(wr)
    mats = (_segment_ones(D_RWKV, RWKV_HEAD), w_out[0].astype(BF16), wr_hi_lo[0], wr_hi_lo[1])
    h1, u2, route, route_t, counts = _out_proj(x, y_f, y_b, bonus_l, gate_l, t_f, t_b, pt, vecs, mats)

    n_all = b * n_tok

    e1 = route_t[ROUTE_E1].astype(jnp.int32)
    e2 = route_t[ROUTE_E2].astype(jnp.int32)
    cnt = counts[0, :N_EXPERTS].astype(jnp.int32)
    padded = ((cnt + EXPERT_ROWS - 1) // EXPERT_ROWS) * EXPERT_ROWS
    pends = jnp.cumsum(padded)
    pstarts = pends - padded
    expert_ids = jnp.arange(N_EXPERTS, dtype=jnp.int32)
    start_of = lambda e: jnp.sum(jnp.where(e[:, None] == expert_ids[None, :], pstarts[None, :], 0), axis=1)
    dest1 = start_of(e1) + route_t[ROUTE_RANK1].astype(jnp.int32)
    dest2 = start_of(e2) + route_t[ROUTE_RANK2].astype(jnp.int32)
    n_blk = -(-(n_all * 2) // EXPERT_ROWS) + N_EXPERTS
    block_start = jnp.arange(n_blk, dtype=jnp.int32) * EXPERT_ROWS
    block_expert = jnp.minimum(jnp.sum((block_start[:, None] >= pends[None, :]).astype(jnp.int32), axis=1),
                               N_EXPERTS - 1)

    n_used = (pends[N_EXPERTS - 1:] // EXPERT_ROWS).astype(jnp.int32)
    x_slabs = _dispatch(u2, dest1, dest2, n_blk * EXPERT_ROWS)
    y_tiles = _expert_mlp(x_slabs, block_expert, n_used, expert_w_gate[0], expert_w_up[0], expert_w_down[0])
    out = _combine(y_tiles, dest1, dest2, route, h1.reshape(n_all, d), g2, row(ln2_g[0]), row(ln2_b[0]), n_tok)
    return out.reshape(b, n_tok, d)
```

```python
import functools
import math

import jax
import jax.numpy as jnp
import numpy as np
from jax import lax
from jax.experimental import pallas as pl
from jax.experimental.pallas import tpu as pltpu

F32 = jnp.float32
BF16 = jnp.bfloat16
HIGHEST = lax.Precision.HIGHEST

GRID_W = 64
D_RWKV = 512
RWKV_HEAD = 64
RWKV_HEADS = D_RWKV // RWKV_HEAD
DECAY_LORA = 64
AAA_LORA = 64
GATE_LORA = 128
D_RET = 512
RET_HEADS = 4
RET_HEAD = D_RET // RET_HEADS
RET_CHUNK = 128
RWKV_COLS = 3 * D_RWKV + 2 * (DECAY_LORA + AAA_LORA) + GATE_LORA
RET_COLS = 4 * D_RET
N_GROUPS = 4
EXPERTS_PER_GROUP = 8
N_EXPERTS = N_GROUPS * EXPERTS_PER_GROUP
EXPERT_HIDDEN = 512
MOE_BLOCK = 128
ROPE_BASE = 10000.0
LN_EPS = 1e-5
RWKV_GN_EPS = 64e-5
RET_GN_EPS = 1e-5
DEEPNORM_ALPHA = 2.0 ** 0.25
EXP_NEG_HALF = math.exp(-0.5)

LANES = 128
SUBLANES = 8
VMEM_LIMIT_BYTES = 56 * 1024 * 1024

WKV_CHUNK = 64


IN_PROJ_ROWS = 512
OUT_PROJ_ROWS = 512

TOKEN_SLAB = 4
EXPERT_ROWS = 512
COMBINE_ROWS = 256
DISPATCH_ROWS = 256
GATHER_UNROLL = 16
GATHER_PRIORITIES = (0, 1)


def _pack_bf16_pairs(x):
    half = x.shape[1] // 2

    def bf16_bits(v):
        b = lax.bitcast_convert_type(v, jnp.uint32)
        return (b + jnp.uint32(0x7FFF) + ((b >> 16) & jnp.uint32(1))) >> 16

    return bf16_bits(x[:, :half]) | (bf16_bits(x[:, half:]) << 16)


def _unpack_bf16_pairs(p):
    lo = lax.bitcast_convert_type(p << 16, F32)
    hi = lax.bitcast_convert_type(p & jnp.uint32(0xFFFF0000), F32)
    return jnp.concatenate([lo, hi], axis=-1)


def _store_token_slabs(ref, x):
    rows = x.shape[0]
    for j in range(TOKEN_SLAB):
        ref[pl.ds(j, rows, stride=TOKEN_SLAB), :] = x[:, j * LANES:(j + 1) * LANES]


def _load_token_slabs(ref, rows):
    return jnp.concatenate([ref[pl.ds(j, rows, stride=TOKEN_SLAB), :] for j in range(TOKEN_SLAB)], axis=-1)


def _cparams(sem):
    return pltpu.CompilerParams(dimension_semantics=sem, vmem_limit_bytes=VMEM_LIMIT_BYTES)


def _layer_norm(x, g, b, eps=LN_EPS):
    mu = jnp.mean(x, axis=-1, keepdims=True)
    xc = x - mu
    var = jnp.mean(xc * xc, axis=-1, keepdims=True)
    return xc * lax.rsqrt(var + eps) * g + b


def _sigmoid(x):
    return 1.0 / (1.0 + jnp.exp(-x))


def _split_bf16(x):
    hi = x.astype(BF16)
    return hi, (x - hi.astype(F32)).astype(BF16)


def _segsum(x, ones_bf16):
    t = x.shape[0]
    s = jnp.dot(jnp.concatenate(_split_bf16(x), axis=0), ones_bf16, preferred_element_type=F32)
    return s[:t] + s[t:]


def _dot_split(x, w_hi, w_lo):
    hi, lo = _split_bf16(x)
    acc = jnp.dot(hi, w_hi, preferred_element_type=F32)
    acc = acc + jnp.dot(lo, w_hi, preferred_element_type=F32)
    return acc + jnp.dot(hi, w_lo, preferred_element_type=F32)


def _segment_ones(width, seg):
    idx = np.arange(width) // seg
    return jnp.asarray(idx[:, None] == idx[None, :], dtype=BF16)


def _mod_kernel(c_ref, w_ref, b_ref, o_ref):
    c = c_ref[...]
    sc = c * _sigmoid(c)
    o_ref[...] = jnp.dot(sc, w_ref[...], precision=HIGHEST, preferred_element_type=F32) + b_ref[...]


def _modulation(c_rows, w_mod, b_mod):
    rows, d = c_rows.shape
    n = w_mod.shape[1]
    tn = 1536
    return pl.pallas_call(
        _mod_kernel,
        grid=(n // tn,),
        in_specs=[pl.BlockSpec((rows, d), lambda j: (0, 0)),
                  pl.BlockSpec((d, tn), lambda j: (0, j)),
                  pl.BlockSpec((1, tn), lambda j: (0, j))],
        out_specs=pl.BlockSpec((rows, tn), lambda j: (0, j)),
        out_shape=jax.ShapeDtypeStruct((rows, n), F32),
        compiler_params=_cparams(("arbitrary",)),
        name="modulation",
    )(c_rows, w_mod, b_mod)


def _in_proj_kernel(x_ref, g_ref, b_ref, s_ref, sh_ref, w_ref, pr_ref, pt_ref):
    h = _layer_norm(x_ref[0], g_ref[...], b_ref[...])
    u = h * (1.0 + s_ref[0]) + sh_ref[0]
    p = jnp.dot(u.astype(BF16), w_ref[...], preferred_element_type=F32)
    pr_ref[0] = p[:, :RWKV_COLS]
    pt_ref[0] = p[:, RWKV_COLS:]


def _in_proj(x, ln_g, ln_b, s1, sh1, w_in_bf16):
    b, n, d = x.shape
    tm = min(IN_PROJ_ROWS, n)
    cols = w_in_bf16.shape[1]
    return pl.pallas_call(
        _in_proj_kernel,
        grid=(b, n // tm),
        in_specs=[pl.BlockSpec((1, tm, d), lambda bi, i: (bi, i, 0)),
                  pl.BlockSpec((1, d), lambda bi, i: (0, 0)),
                  pl.BlockSpec((1, d), lambda bi, i: (0, 0)),
                  pl.BlockSpec((1, 1, d), lambda bi, i: (bi, 0, 0)),
                  pl.BlockSpec((1, 1, d), lambda bi, i: (bi, 0, 0)),
                  pl.BlockSpec((d, cols), lambda bi, i: (0, 0))],
        out_specs=[pl.BlockSpec((1, tm, RWKV_COLS), lambda bi, i: (bi, i, 0)),
                   pl.BlockSpec((1, tm, RET_COLS), lambda bi, i: (bi, i, 0))],
        out_shape=[jax.ShapeDtypeStruct((b, n, RWKV_COLS), F32),
                   jax.ShapeDtypeStruct((b, n, RET_COLS), F32)],
        compiler_params=_cparams(("arbitrary", "arbitrary")),
        name="in_proj",
    )(x, ln_g, ln_b, s1, sh1, w_in_bf16)


def _rwkv_prepare_kernel(cur_ref, prev_ref, next_ref, mu_ref, w0_ref, w2_ref, a0_ref, a2_ref, g2_ref,
                         kk_scale_ref, ka_ref, rk_ref, ones_ref,
                         r_ref, v_ref, kk_ref, w_ref, kd_ref, bb_ref, g_ref, bonus_ref,
                         *, grid_shift, n_tok):
    cur = cur_ref[0]
    t, c = cur.shape
    row = lax.broadcasted_iota(jnp.int32, (t, c), 0)
    lane = lax.broadcasted_iota(jnp.int32, (t, c), 1)
    prev_tok = pltpu.roll(cur, 1, 0)
    next_tok = pltpu.roll(cur, t - 1, 0)
    if grid_shift:
        col = row & (GRID_W - 1)
        tok = row + pl.program_id(1) * t
        left = jnp.where(col > 0, prev_tok, 0.0)
        right = jnp.where(col < GRID_W - 1, next_tok, 0.0)
        up = jnp.where(tok >= GRID_W, jnp.concatenate([prev_ref[0], cur[:t - GRID_W]], axis=0), 0.0)
        down = jnp.where(tok < n_tok - GRID_W, jnp.concatenate([cur[GRID_W:], next_ref[0]], axis=0), 0.0)
        cm = lane & 3
        shifted = jnp.where(cm == 0, left, jnp.where(cm == 1, right, jnp.where(cm == 2, up, down)))
    else:
        prev_tok = jnp.where(row > 0, prev_tok, 0.0)
        next_tok = jnp.where(row < t - 1, next_tok, 0.0)
        shifted = jnp.where((lane & 1) == 0, prev_tok, next_tok)
    pm = cur + mu_ref[...] * (shifted - cur)

    r = pm[:, 0:D_RWKV]
    k = pm[:, D_RWKV:2 * D_RWKV]
    v = pm[:, 2 * D_RWKV:3 * D_RWKV]
    o = 3 * D_RWKV
    lw = pm[:, o:o + 2 * DECAY_LORA]
    la = pm[:, o + 2 * DECAY_LORA:o + 2 * (DECAY_LORA + AAA_LORA)]
    lg = pm[:, o + 2 * (DECAY_LORA + AAA_LORA):]

    w = w0_ref[...] + _dot_split(jnp.tanh(lw), w2_ref[0], w2_ref[1])
    log_decay = -EXP_NEG_HALF * _sigmoid(w)
    a = _sigmoid(a0_ref[...] + _dot_split(la, a2_ref[0], a2_ref[1]))
    gate = _dot_split(_sigmoid(lg), g2_ref[0], g2_ref[1])

    ones = ones_ref[...]
    kk_raw = k * kk_scale_ref[...]
    kk = kk_raw / jnp.maximum(jnp.sqrt(_segsum(kk_raw * kk_raw, ones)), 1e-12)
    ka = ka_ref[...]
    a0 = a[:, :D_RWKV]
    a1 = a[:, D_RWKV:]
    kd0 = k * (1.0 + (a0 - 1.0) * ka)
    kd1 = k * (1.0 + (a1 - 1.0) * ka)
    bonus = _segsum(r * (kd0 + kd1) * rk_ref[...], ones) * v

    r_ref[0] = r
    v_ref[0] = v
    kk_ref[0] = kk
    w_ref[0] = log_decay
    kd_ref[0] = jnp.concatenate([kd0, kd1], axis=-1)
    bb_ref[0] = jnp.concatenate([kk * a0, kk * a1], axis=-1)
    g_ref[0] = gate
    bonus_ref[0] = bonus


def _rwkv_prepare(pr, params, grid_shift):
    b, n, c = pr.shape
    t = 256
    if not grid_shift:
        assert n == t, "sequence token shift is written for a single tile"
    halo_blocks = n // GRID_W
    per_tile = t // GRID_W
    small = lambda shape: pl.BlockSpec(shape, lambda bi, i: (0,) * len(shape))
    tok_spec = lambda width: pl.BlockSpec((1, t, width), lambda bi, i: (bi, i, 0))
    out_widths = (D_RWKV, D_RWKV, D_RWKV, 2 * D_RWKV, 2 * D_RWKV, 2 * D_RWKV, D_RWKV, D_RWKV)
    kernel = functools.partial(_rwkv_prepare_kernel, grid_shift=grid_shift, n_tok=n)
    return pl.pallas_call(
        kernel,
        grid=(b, n // t),
        in_specs=[tok_spec(c),
                  pl.BlockSpec((1, GRID_W, c), lambda bi, i: (bi, jnp.maximum(i * per_tile - 1, 0), 0)),
                  pl.BlockSpec((1, GRID_W, c),
                               lambda bi, i: (bi, jnp.minimum((i + 1) * per_tile, halo_blocks - 1), 0)),
                  small((1, c)), small((1, 2 * D_RWKV)), small((2, 2 * DECAY_LORA, 2 * D_RWKV)),
                  small((1, 2 * D_RWKV)), small((2, 2 * AAA_LORA, 2 * D_RWKV)), small((2, GATE_LORA, D_RWKV)),
                  small((1, D_RWKV)), small((1, D_RWKV)), small((1, D_RWKV)), small((D_RWKV, D_RWKV))],
        out_specs=[tok_spec(wd) for wd in out_widths],
        out_shape=[jax.ShapeDtypeStruct((b, n, wd), F32) for wd in out_widths],
        compiler_params=_cparams(("arbitrary", "arbitrary")),
        name="rwkv_prepare",
    )(pr, pr, pr, *params)


def _bdot(a, b):
    return jnp.dot(a.astype(BF16), b.astype(BF16), preferred_element_type=F32)


def _bdot_nt(a, b):
    return lax.dot_general(a.astype(BF16), b.astype(BF16), (((1,), (1,)), ((), ())), preferred_element_type=F32)


def _bdot_tn(a, b):
    return lax.dot_general(a.astype(BF16), b.astype(BF16), (((0,), (0,)), ((), ())), preferred_element_type=F32)


def _wkv7_chunk_kernel(*refs, n_ctx_chunks):
    c = WKV_CHUNK
    p = 2 * c
    n_in = 12
    in_refs = (refs[:n_in], refs[n_in:2 * n_in])
    y_refs = refs[2 * n_in:2 * n_in + 2]
    state_ref = refs[2 * n_in + 2]
    n = pl.program_id(0)
    n_batch = in_refs[0][0].shape[0]
    pairs_per_batch = RWKV_HEADS // 2
    pairs_per_dir = n_batch * pairs_per_batch

    @pl.when(n == 0)
    def _():
        state_ref[...] = jnp.zeros_like(state_ref)

    is_ctx = n < n_ctx_chunks
    ti = lax.broadcasted_iota(jnp.int32, (c, c), 0)
    tj = lax.broadcasted_iota(jnp.int32, (c, c), 1)
    ri = lax.broadcasted_iota(jnp.int32, (p, p), 0)
    ci = lax.broadcasted_iota(jnp.int32, (p, p), 1)
    same_head = (ri >= c) == (ci >= c)
    ii = ri & (c - 1)
    jj = ci & (c - 1)
    eye = (ri == ci).astype(F32)
    first = lax.broadcasted_iota(jnp.int32, (c, p), 1) < RWKV_HEAD

    def stack(x):
        return jnp.concatenate([jnp.where(first, x, 0.0), jnp.where(first, 0.0, x)], axis=0)

    def unstack(x):
        return x[:c] + x[c:]

    a_st, r_st, k_st, b_st, k2_st, b2_st, v_st, g_chunk, earlier, upto_self = ([] for _ in range(10))
    for d in range(2):
        r_l, v_l, kk_l, lw_l, kd_l, bb_l, r_c, v_c, kk_c, lw_c, kd_c, bb_c = in_refs[d]
        pick = lambda xc, xl: jnp.concatenate(
            [jnp.where(is_ctx, xc[bi], xl[bi]) for bi in range(n_batch)], axis=-1)
        r, v, kk, lw, kd, bb = (pick(r_c, r_l), pick(v_c, v_l), pick(kk_c, kk_l), pick(lw_c, lw_l),
                                pick(kd_c, kd_l), pick(bb_c, bb_l))
        before = (tj < ti) if d == 0 else (tj > ti)
        upto = (before | (ti == tj)).astype(BF16)
        hi = lw.astype(BF16)
        r1 = lw - hi.astype(F32)
        mid = r1.astype(BF16)
        lo = (r1 - mid.astype(F32)).astype(BF16)
        cum = (jnp.dot(upto, hi, preferred_element_type=F32) + jnp.dot(upto, mid, preferred_element_type=F32)
               + jnp.dot(upto, lo, preferred_element_type=F32))
        tot = jnp.sum(lw, axis=0, keepdims=True)
        e_neg = jnp.exp(-cum)
        e_rem = jnp.exp(tot - cum)
        alpha = kk * jnp.exp(cum - lw)
        rho = r * jnp.exp(cum)
        beta = bb * e_neg
        kappa = kd * e_neg
        kappa_rem = kd * e_rem
        beta_rem = bb * e_rem
        g_all = jnp.exp(tot)
        pair_before = same_head & ((jj < ii) if d == 0 else (jj > ii))
        pair_upto = pair_before | (ri == ci)
        for hp in range(pairs_per_dir):
            sl = slice(hp * p, (hp + 1) * p)
            a_st.append(stack(alpha[:, sl]))
            r_st.append(stack(rho[:, sl]))
            k_st.append(stack(kappa[:, sl]))
            b_st.append(stack(beta[:, sl]))
            k2_st.append(stack(kappa_rem[:, sl]))
            b2_st.append(stack(beta_rem[:, sl]))
            v_st.append(stack(v[:, sl]))
            g_chunk.append(g_all[:, sl])
            earlier.append(pair_before)
            upto_self.append(pair_upto)

    pairs = range(2 * pairs_per_dir)
    g = [_bdot_nt(jnp.concatenate([a_st[h], r_st[h]], axis=0), jnp.concatenate([k_st[h], b_st[h]], axis=0))
         for h in pairs]
    m1 = [jnp.where(earlier[h], g[h][:p, :p], 0.0) for h in pairs]
    m2 = [jnp.where(earlier[h], g[h][:p, p:], 0.0) for h in pairs]
    n1 = [jnp.where(upto_self[h], g[h][p:, :p], 0.0) for h in pairs]
    n2 = [jnp.where(upto_self[h], g[h][p:, p:], 0.0) for h in pairs]

    in_block = (ii >> 3) == (jj >> 3)
    pw = [-jnp.where(in_block, m2[h], 0.0) for h in pairs]
    inv = [eye + pw[h] for h in pairs]
    pw = [_bdot(pw[h], pw[h]) for h in pairs]
    both = [_bdot(jnp.concatenate([inv[h], pw[h]], axis=0), pw[h]) for h in pairs]
    inv = [inv[h] + both[h][:p] for h in pairs]
    inv = [inv[h] + _bdot(inv[h], both[h][p:]) for h in pairs]
    for sh in (3, 4, 5):
        off = ((ii >> (sh + 1)) == (jj >> (sh + 1))) & ((ii >> sh) != (jj >> sh))
        left = [_bdot(inv[h], jnp.where(off, m2[h], 0.0)) for h in pairs]
        inv = [inv[h] - _bdot(left[h], inv[h]) for h in pairs]

    mnv = [_bdot(jnp.concatenate([m1[h], n1[h]], axis=0), v_st[h]) for h in pairs]
    m1v = [mnv[h][:p] for h in pairs]
    n1v = [mnv[h][p:] for h in pairs]
    au = [_bdot(inv[h], jnp.concatenate([a_st[h], m1v[h]], axis=1)) for h in pairs]
    nn = [_bdot(n2[h], au[h]) for h in pairs]
    pc = [_bdot_tn(b2_st[h], au[h][:, :p]) for h in pairs]
    qc_t = [_bdot_tn(jnp.concatenate([v_st[h], -au[h][:, p:]], axis=0),
                     jnp.concatenate([k2_st[h], b2_st[h]], axis=0)) for h in pairs]
    s0 = [state_ref[h] for h in pairs]
    y = [_bdot_nt(unstack(r_st[h] - nn[h][:, :p]), s0[h]) + unstack(n1v[h] - nn[h][:, p:]) for h in pairs]
    s_dec = [_bdot_nt(s0[h], pc[h]) for h in pairs]
    for h in pairs:
        d, hp = divmod(h, pairs_per_dir)
        bi, hpb = divmod(hp, pairs_per_batch)
        y_refs[d][bi, :, hpb * p:(hpb + 1) * p] = y[h]
        state_ref[h] = s0[h] * g_chunk[h] - s_dec[h] + qc_t[h]


def _wkv7(lat, ctx, b, n_tok, n_ctx):
    c = WKV_CHUNK
    ncx = n_ctx // c
    nl = n_tok // c
    lat_idx = (lambda n: jnp.maximum(n - ncx, 0), lambda n: nl - 1 - jnp.maximum(n - ncx, 0))
    ctx_idx = (lambda n: jnp.minimum(n, ncx - 1), lambda n: ncx - 1 - jnp.minimum(n, ncx - 1))

    def specs(idx, d):
        shared = pl.BlockSpec((b, c, D_RWKV), lambda n: (0, idx(n), 0))
        per_dir = pl.BlockSpec((b, c, D_RWKV), lambda n: (0, idx(n), d))
        return [shared, shared, shared, per_dir, per_dir, per_dir]

    in_specs, args = [], []
    for d in range(2):
        in_specs += specs(lat_idx[d], d) + specs(ctx_idx[d], d)
        args += list(lat) + list(ctx)
    return pl.pallas_call(
        functools.partial(_wkv7_chunk_kernel, n_ctx_chunks=ncx),
        grid=(ncx + nl,),
        in_specs=in_specs,
        out_specs=[pl.BlockSpec((b, c, D_RWKV), lambda n, d=d: (0, lat_idx[d](n), 0)) for d in range(2)],
        out_shape=[jax.ShapeDtypeStruct((b, n_tok, D_RWKV), F32)] * 2,
        scratch_shapes=[pltpu.VMEM((2 * b * RWKV_HEADS // 2, 2 * RWKV_HEAD, 2 * RWKV_HEAD), F32)],
        compiler_params=_cparams(("arbitrary",)),
        name="wkv7_chunk",
    )(*args)


def _rope(z, cos_t, sin_t):
    lane = lax.broadcasted_iota(jnp.int32, z.shape, 1)
    half = RET_HEAD // 4
    partner = jnp.where((lane & (2 * half - 1)) < half, pltpu.roll(z, RET_HEAD - half, 1), pltpu.roll(z, half, 1))
    return z * cos_t + partner * sin_t


def _retention_kernel(dec_ref, fwd_ref, bwd_ref, ctx_ref, cosf_ref, sinf_ref, cosb_ref, sinb_ref,
                      yf_ref, yb_ref, state_ref, dmat_ref, tail_ref, head_ref, cdec_ref):
    c = RET_CHUNK
    scale = RET_HEAD ** -0.5
    ii = lax.broadcasted_iota(jnp.int32, (c, c), 0)
    jj = lax.broadcasted_iota(jnp.int32, (c, c), 1)
    pos = lax.broadcasted_iota(jnp.int32, (c, RET_HEAD), 0).astype(F32)
    n_ctx_chunks = ctx_ref.shape[1] // c

    def head_slices(ref_val, h):
        q = ref_val[:, h * RET_HEAD:(h + 1) * RET_HEAD]
        k = ref_val[:, D_RET + h * RET_HEAD:D_RET + (h + 1) * RET_HEAD]
        v = ref_val[:, 2 * D_RET + h * RET_HEAD:2 * D_RET + (h + 1) * RET_HEAD]
        return q, k, v

    n_batch = fwd_ref.shape[0]
    heads = [(d, h) for d in range(2) for h in range(RET_HEADS)]
    chains = [(bi, d, h) for bi in range(n_batch) for d, h in heads]

    @pl.when(pl.program_id(0) == 0)
    def _():
        for d, h in heads:
            x = jnp.full((1, RET_HEAD), dec_ref[d, h], F32)
            lg = -(jnp.maximum(x, 0.0) + jnp.log(1.0 + jnp.exp(-jnp.abs(x))))
            chunk_decay = jnp.exp(lg * float(c))
            tail = jnp.exp(lg * ((c - 1.0 - pos) if d == 0 else pos))
            rel = (ii - jj) if d == 0 else (jj - ii)
            mask = (rel >= 0) if d == 0 else (rel > 0)
            dmat_ref[d, h] = jnp.where(mask, jnp.exp(lg * jnp.maximum(rel, 0).astype(F32)), 0.0)
            tail_ref[d, h] = tail
            head_ref[d, h] = jnp.exp(lg * ((pos + 1.0) if d == 0 else (c - pos)))
            cdec_ref[d, h] = jnp.broadcast_to(chunk_decay, (SUBLANES, RET_HEAD))
            order = range(n_ctx_chunks) if d == 0 else range(n_ctx_chunks - 1, -1, -1)
            for bi in range(n_batch):
                s = jnp.zeros((RET_HEAD, RET_HEAD), F32)
                for cc in order:
                    _, kc, vc = head_slices(ctx_ref[bi, cc * c:(cc + 1) * c, :], h)
                    s = s * chunk_decay + _bdot_tn(kc * scale * tail, vc)
                state_ref[bi, d, h] = s

    qkv = []
    for bi, d, h in chains:
        blk = fwd_ref[bi] if d == 0 else bwd_ref[bi]
        cos_t = cosf_ref[...] if d == 0 else cosb_ref[...]
        sin_t = sinf_ref[...] if d == 0 else sinb_ref[...]
        q, k, v = head_slices(blk, h)
        qkv.append((_rope(q, cos_t, sin_t), _rope(k, cos_t, sin_t) * scale, v.astype(BF16)))
    s0 = [state_ref[bi, d, h] for bi, d, h in chains]
    scores = [_bdot_nt(q, k) for q, k, _ in qkv]
    inner = [_bdot(scores[i] * dmat_ref[d, h], qkv[i][2]) for i, (_, d, h) in enumerate(chains)]
    cross = [_bdot(qkv[i][0] * head_ref[d, h], s0[i]) for i, (_, d, h) in enumerate(chains)]
    upd = [_bdot_tn(qkv[i][1] * tail_ref[d, h], qkv[i][2]) for i, (_, d, h) in enumerate(chains)]
    for i, (bi, d, h) in enumerate(chains):
        state_ref[bi, d, h] = s0[i] * cdec_ref[d, h, 0:1, :] + upd[i]
        out_ref = yf_ref if d == 0 else yb_ref
        out_ref[bi, :, h * RET_HEAD:(h + 1) * RET_HEAD] = inner[i] + cross[i]


def _retention(pt, pt_ctx, ret_decay, cos_t, sin_t):
    b, n, _ = pt.shape
    c = RET_CHUNK
    nc = n // c
    qkv = 3 * D_RET
    fwd = lambda i: (0, i, 0)
    bwd = lambda i: (0, nc - 1 - i, 0)
    return pl.pallas_call(
        _retention_kernel,
        grid=(nc,),
        in_specs=[pl.BlockSpec(memory_space=pltpu.SMEM),
                  pl.BlockSpec((b, c, qkv), fwd),
                  pl.BlockSpec((b, c, qkv), bwd),
                  pl.BlockSpec((b, pt_ctx.shape[1], qkv), lambda i: (0, 0, 0)),
                  pl.BlockSpec((c, RET_HEAD), lambda i: (i, 0)),
                  pl.BlockSpec((c, RET_HEAD), lambda i: (i, 0)),
                  pl.BlockSpec((c, RET_HEAD), lambda i: (nc - 1 - i, 0)),
                  pl.BlockSpec((c, RET_HEAD), lambda i: (nc - 1 - i, 0))],
        out_specs=[pl.BlockSpec((b, c, D_RET), fwd), pl.BlockSpec((b, c, D_RET), bwd)],
        out_shape=[jax.ShapeDtypeStruct((b, n, D_RET), F32), jax.ShapeDtypeStruct((b, n, D_RET), F32)],
        scratch_shapes=[pltpu.VMEM((b, 2, RET_HEADS, RET_HEAD, RET_HEAD), F32),
                        pltpu.VMEM((2, RET_HEADS, c, c), F32),
                        pltpu.VMEM((2, RET_HEADS, c, RET_HEAD), F32),
                        pltpu.VMEM((2, RET_HEADS, c, RET_HEAD), F32),
                        pltpu.VMEM((2, RET_HEADS, SUBLANES, RET_HEAD), F32)],
        compiler_params=_cparams(("arbitrary",)),
        name="retention",
    )(ret_decay, pt, pt, pt_ctx, cos_t, sin_t, cos_t, sin_t)


def _rope_tables(n_tok):
    nf = RET_HEAD // 4
    lane = np.arange(RET_HEAD)
    inv = ROPE_BASE ** (-jnp.arange(nf, dtype=F32) / nf)
    t = jnp.arange(n_tok)
    pos = jnp.where((lane // (2 * nf) == 0)[None, :], (t // GRID_W)[:, None], (t % GRID_W)[:, None]).astype(F32)
    ang = pos * inv[lane % nf][None, :]
    sign = jnp.where((lane % (2 * nf)) < nf, -1.0, 1.0).astype(F32)
    return jnp.cos(ang), jnp.sin(ang) * sign[None, :]


def _group_norm(y, ones, seg, eps, g, b):
    mu = _segsum(y, ones) * (1.0 / seg)
    yc = y - mu
    var = _segsum(yc * yc, ones) * (1.0 / seg)
    return yc * lax.rsqrt(var + eps) * g + b


def _out_proj_kernel(x_ref, yf_ref, yb_ref, bonus_ref, gate_ref, tf_ref, tb_ref, gt_ref,
                     embg_ref, embb_ref, g1_ref, s2_ref, sh2_ref, rgn_g_ref, rgn_b_ref, tgn_g_ref, tgn_b_ref,
                     ones_r_ref, wout_ref, ln1g_ref, ln1b_ref, wrh_ref, wrl_ref, br_ref,
                     h1_ref, u2_ref, route_ref, route_t_ref, count_ref, carry_ref):
    @pl.when((pl.program_id(0) == 0) & (pl.program_id(1) == 0))
    def _():
        carry_ref[...] = jnp.zeros_like(carry_ref)

    y = yf_ref[0] + yb_ref[0]
    o_rwkv = _group_norm(y, ones_r_ref[...], RWKV_HEAD, RWKV_GN_EPS, rgn_g_ref[...], rgn_b_ref[...])
    o_rwkv = (o_rwkv + bonus_ref[0]) * gate_ref[0]
    yt = tf_ref[0] + tb_ref[0]
    gt = gt_ref[0]
    tgn_g = tgn_g_ref[...]
    tgn_b = tgn_b_ref[...]
    o_ret = jnp.concatenate(
        [_layer_norm(yt[:, h * RET_HEAD:(h + 1) * RET_HEAD], tgn_g[:, h * RET_HEAD:(h + 1) * RET_HEAD],
                     tgn_b[:, h * RET_HEAD:(h + 1) * RET_HEAD], RET_GN_EPS) for h in range(RET_HEADS)], axis=-1)
    o_ret = o_ret * (gt * _sigmoid(gt))
    cat = jnp.concatenate([o_rwkv, o_ret], axis=-1).astype(BF16)
    mix = jnp.dot(cat, wout_ref[...], preferred_element_type=F32)
    h = _layer_norm(x_ref[0], embg_ref[...], embb_ref[...])
    h1 = _layer_norm(DEEPNORM_ALPHA * h + g1_ref[0] * mix, ln1g_ref[...], ln1b_ref[...])
    u2 = h1 * (1.0 + s2_ref[0]) + sh2_ref[0]
    h1_ref[0] = h1
    _store_token_slabs(u2_ref, _pack_bf16_pairs(u2))
    route = _route_tile(_dot_split(u2, wrh_ref[...], wrl_ref[...]) + br_ref[...], carry_ref)
    route_ref[...] = route
    route_t_ref[...] = route.T[:SUBLANES]
    count_ref[...] = carry_ref[...]


def _out_proj(x, y_f, y_b, bonus, gate, t_f, t_b, pt, vecs, mats):
    b, n, d = x.shape
    t = OUT_PROJ_ROWS
    tok = lambda width: pl.BlockSpec((1, t, width), lambda bi, i: (bi, i, 0))
    per_b = pl.BlockSpec((1, 1, d), lambda bi, i: (bi, 0, 0))
    small = lambda arr: pl.BlockSpec(arr.shape, lambda bi, i: (0,) * arr.ndim)
    (embg, embb, g1, s2, sh2, rgn_g, rgn_b, tgn_g, tgn_b, ln1g, ln1b, br) = vecs
    (ones_r, wout, wr_hi, wr_lo) = mats
    gt_spec = pl.BlockSpec((1, t, D_RET), lambda bi, i: (bi, i, 3))
    tiles = n // t
    flat = lambda bi, i: (bi * tiles + i, 0)
    args = (x, y_f, y_b, bonus, gate, t_f, t_b, pt, embg, embb, g1, s2, sh2, rgn_g, rgn_b, tgn_g, tgn_b,
            ones_r, wout, ln1g, ln1b, wr_hi, wr_lo, br)
    in_specs = [tok(d)] + [tok(D_RWKV)] * 6 + [gt_spec, small(embg), small(embb), per_b, per_b, per_b,
                                                small(rgn_g), small(rgn_b), small(tgn_g), small(tgn_b),
                                                small(ones_r), small(wout), small(ln1g),
                                                small(ln1b), small(wr_hi), small(wr_lo), small(br)]
    return pl.pallas_call(
        _out_proj_kernel,
        grid=(b, n // t),
        in_specs=in_specs,
        out_specs=[tok(d), pl.BlockSpec((t * TOKEN_SLAB, LANES), flat),
                   pl.BlockSpec((t, LANES), flat),
                   pl.BlockSpec((SUBLANES, t), lambda bi, i: (0, bi * tiles + i)),
                   pl.BlockSpec((SUBLANES, LANES), lambda bi, i: (0, 0))],
        out_shape=[jax.ShapeDtypeStruct((b, n, d), F32),
                   jax.ShapeDtypeStruct((b * n * TOKEN_SLAB, LANES), jnp.uint32),
                   jax.ShapeDtypeStruct((b * n, LANES), F32),
                   jax.ShapeDtypeStruct((SUBLANES, b * n), F32),
                   jax.ShapeDtypeStruct((SUBLANES, LANES), F32)],
        scratch_shapes=[pltpu.VMEM((SUBLANES, LANES), F32)],
        compiler_params=_cparams(("arbitrary", "arbitrary")),
        name="out_proj",
    )(*args)


ROUTE_E1, ROUTE_E2, ROUTE_G1, ROUTE_G2, ROUTE_RANK1, ROUTE_RANK2 = range(6)


def _lane_argmax(x, valid, lane):
    m = jnp.max(jnp.where(valid, x, -jnp.inf), axis=-1, keepdims=True)
    idx = jnp.min(jnp.where(valid & (x == m), lane, float(LANES)), axis=-1, keepdims=True)
    return m, idx


def _route_tile(lg, carry_ref):
    t = lg.shape[0]
    lane = lax.broadcasted_iota(jnp.int32, lg.shape, 1).astype(F32)
    gmask = lane < N_GROUPS
    gmax = jnp.max(jnp.where(gmask, lg, -jnp.inf), axis=-1, keepdims=True)
    gexp = jnp.where(gmask, jnp.exp(lg - gmax), 0.0)
    gp = gexp / jnp.sum(gexp, axis=-1, keepdims=True)
    g_w, g_i = _lane_argmax(gp, gmask, lane)

    lo = N_GROUPS + EXPERTS_PER_GROUP * g_i
    emask = (lane >= lo) & (lane < lo + EXPERTS_PER_GROUP)
    emax = jnp.max(jnp.where(emask, lg, -jnp.inf), axis=-1, keepdims=True)
    eexp = jnp.where(emask, jnp.exp(lg - emax), 0.0)
    ep = eexp / jnp.sum(eexp, axis=-1, keepdims=True)
    p1, i1 = _lane_argmax(ep, emask, lane)
    p2, i2 = _lane_argmax(ep, emask & (lane != i1), lane)
    denom = p1 + p2
    gate1 = g_w * p1 / denom
    gate2 = g_w * p2 / denom
    e1 = i1 - N_GROUPS
    e2 = i2 - N_GROUPS

    oh1 = (lane == e1).astype(F32)
    oh2 = (lane == e2).astype(F32)
    cnt = oh1 + oh2
    ri = lax.broadcasted_iota(jnp.int32, (t, t), 0)
    ci = lax.broadcasted_iota(jnp.int32, (t, t), 1)
    before = (ci < ri).astype(BF16)
    seen = jnp.dot(before, cnt.astype(BF16), preferred_element_type=F32) + carry_ref[0:1, :]
    rank1 = jnp.sum(oh1 * seen, axis=-1, keepdims=True)
    rank2 = jnp.sum(oh2 * seen, axis=-1, keepdims=True)
    carry_ref[0:1, :] = carry_ref[0:1, :] + jnp.sum(cnt, axis=0, keepdims=True)

    out = jnp.zeros(lg.shape, F32)
    for slot, val in ((ROUTE_E1, e1.astype(F32)), (ROUTE_E2, e2.astype(F32)), (ROUTE_G1, gate1),
                      (ROUTE_G2, gate2), (ROUTE_RANK1, rank1), (ROUTE_RANK2, rank2)):
        out = jnp.where(lane == slot, val, out)
    return out


def _tile_gather_copy(src_hbm, idx_ref, buf, sem, slot, r):
    src = src_hbm.at[pl.ds(pl.multiple_of(idx_ref[0, 0, r] * TOKEN_SLAB, TOKEN_SLAB), TOKEN_SLAB), :]
    dst = buf.at[slot, pl.ds(pl.multiple_of(r * TOKEN_SLAB, TOKEN_SLAB), TOKEN_SLAB), :]
    return pltpu.make_async_copy(src, dst, sem.at[slot])


def _start_tile_gather(src_hbm, idx_ref, buf, sem, slot, rows, priorities):
    def body(g, carry):
        for j in range(GATHER_UNROLL):
            copy = _tile_gather_copy(src_hbm, idx_ref, buf, sem, slot, g * GATHER_UNROLL + j)
            copy.start(priority=priorities[j % len(priorities)])
        return carry
    lax.fori_loop(0, rows // GATHER_UNROLL, body, 0)


def _wait_tile_gather(src_hbm, idx_ref, buf, sem, slot, rows):
    del idx_ref
    whole = src_hbm.at[pl.ds(0, rows * TOKEN_SLAB), :]
    pltpu.make_async_copy(whole, buf.at[slot], sem.at[slot]).wait()


def _dispatch_kernel(d1_ref, d2_ref, u_ref, x_init_hbm, x_hbm, sem):
    del x_init_hbm
    rows = d1_ref.shape[2]

    def slab(ref, index):
        return ref.at[pl.ds(pl.multiple_of(index * TOKEN_SLAB, TOKEN_SLAB), TOKEN_SLAB), :]

    def body(g, carry):
        for j in range(GATHER_UNROLL):
            r = g * GATHER_UNROLL + j
            src = slab(u_ref, r)
            pltpu.make_async_copy(src, slab(x_hbm, d1_ref[0, 0, r]), sem).start(priority=j % 2)
            pltpu.make_async_copy(src, slab(x_hbm, d2_ref[0, 0, r]), sem).start(priority=(j + 1) % 2)
        return carry

    lax.fori_loop(0, rows // GATHER_UNROLL, body, 0)
    whole = x_hbm.at[pl.ds(0, rows * TOKEN_SLAB), :]
    pltpu.make_async_copy(u_ref, whole, sem).wait()
    pltpu.make_async_copy(u_ref, whole, sem).wait()


def _dispatch(u2_slabs, dest1, dest2, n_slot):
    n = dest1.shape[0]
    t = DISPATCH_ROWS
    nt = n // t
    idx = pl.BlockSpec((1, 1, t), lambda i: (i, 0, 0), memory_space=pltpu.SMEM)
    any_spec = pl.BlockSpec(memory_space=pl.ANY)
    x_init = jnp.zeros((n_slot * TOKEN_SLAB, LANES), jnp.uint32)
    return pl.pallas_call(
        _dispatch_kernel,
        grid=(nt,),
        in_specs=[idx, idx, pl.BlockSpec((t * TOKEN_SLAB, LANES), lambda i: (i, 0)), any_spec],
        out_specs=any_spec,
        out_shape=jax.ShapeDtypeStruct(x_init.shape, x_init.dtype),
        scratch_shapes=[pltpu.SemaphoreType.DMA(())],
        input_output_aliases={3: 0},
        compiler_params=_cparams(("arbitrary",)),
        name="dispatch",
    )(dest1.reshape(nt, 1, t), dest2.reshape(nt, 1, t), u2_slabs, x_init)


def _expert_kernel(blk_expert_ref, n_used_ref, x_ref, wg_ref, wu_ref, wd_ref, y_ref, wg_s, wu_s, wd_s):
    i = pl.program_id(0)
    n_used = n_used_ref[0]

    @pl.when(i >= n_used)
    def _():
        y_ref[...] = jnp.zeros_like(y_ref)

    @pl.when(i < n_used)
    def _():
        @pl.when((i == 0) | (blk_expert_ref[i] != blk_expert_ref[jnp.maximum(i - 1, 0)]))
        def _():
            wg_s[...] = wg_ref[0].astype(BF16)
            wu_s[...] = wu_ref[0].astype(BF16)
            wd_s[...] = wd_ref[0].astype(BF16)

        x = _unpack_bf16_pairs(_load_token_slabs(x_ref, EXPERT_ROWS)).astype(BF16)
        hg = jnp.dot(x, wg_s[...], preferred_element_type=F32)
        hu = jnp.dot(x, wu_s[...], preferred_element_type=F32)
        act = (hg * _sigmoid(hg) * hu).astype(BF16)
        _store_token_slabs(y_ref, _pack_bf16_pairs(jnp.dot(act, wd_s[...], preferred_element_type=F32)))


def _expert_mlp(x_slabs, block_expert, n_used, w_gate, w_up, w_down):
    n_blk = block_expert.shape[0]
    d = w_gate.shape[1]
    hdim = w_gate.shape[2]
    rows = EXPERT_ROWS
    used = lambda i, nu: jnp.minimum(i, nu[0] - 1)
    weight = lambda i, be, nu: (be[used(i, nu)], 0, 0)
    grid_spec = pltpu.PrefetchScalarGridSpec(
        num_scalar_prefetch=2,
        grid=(n_blk,),
        in_specs=[pl.BlockSpec((rows * TOKEN_SLAB, LANES), lambda i, be, nu: (used(i, nu), 0)),
                  pl.BlockSpec((1, d, hdim), weight),
                  pl.BlockSpec((1, d, hdim), weight),
                  pl.BlockSpec((1, hdim, d), weight)],
        out_specs=pl.BlockSpec((rows * TOKEN_SLAB, LANES), lambda i, be, nu: (i, 0)),
        scratch_shapes=[pltpu.VMEM((d, hdim), BF16), pltpu.VMEM((d, hdim), BF16), pltpu.VMEM((hdim, d), BF16)],
    )
    return pl.pallas_call(
        _expert_kernel,
        grid_spec=grid_spec,
        out_shape=jax.ShapeDtypeStruct((n_blk * rows * TOKEN_SLAB, LANES), jnp.uint32),
        compiler_params=_cparams(("arbitrary",)),
        name="expert_mlp",
    )(block_expert, n_used, x_slabs, w_gate, w_up, w_down)


def _combine_kernel(d1_ref, d2_ref, d1n_ref, d2n_ref, y_hbm, route_ref, h1_ref, g2_ref, lng_ref, lnb_ref,
                    o_ref, abuf, bbuf, sem_a, sem_b):
    i = pl.program_id(0)
    n = pl.num_programs(0)
    slot = i % 2
    rows = o_ref.shape[0]

    @pl.when(i == 0)
    def _():
        _start_tile_gather(y_hbm, d1_ref, abuf, sem_a, 0, rows, GATHER_PRIORITIES)
        _start_tile_gather(y_hbm, d2_ref, bbuf, sem_b, 0, rows, GATHER_PRIORITIES)

    @pl.when(i + 1 < n)
    def _():
        _start_tile_gather(y_hbm, d1n_ref, abuf, sem_a, 1 - slot, rows, GATHER_PRIORITIES)
        _start_tile_gather(y_hbm, d2n_ref, bbuf, sem_b, 1 - slot, rows, GATHER_PRIORITIES)

    _wait_tile_gather(y_hbm, d1_ref, abuf, sem_a, slot, rows)
    _wait_tile_gather(y_hbm, d2_ref, bbuf, sem_b, slot, rows)
    route = route_ref[...]
    f = (_unpack_bf16_pairs(_load_token_slabs(abuf.at[slot], rows)) * route[:, ROUTE_G1:ROUTE_G1 + 1]
         + _unpack_bf16_pairs(_load_token_slabs(bbuf.at[slot], rows)) * route[:, ROUTE_G2:ROUTE_G2 + 1])
    o_ref[...] = _layer_norm(DEEPNORM_ALPHA * h1_ref[...] + g2_ref[0] * f, lng_ref[...], lnb_ref[...])


def _combine(y_tiles, dest1, dest2, route, h1, g2, ln_g, ln_b, tokens_per_batch):
    n, d = h1.shape
    t = COMBINE_ROWS
    nt = n // t
    per_b = tokens_per_batch // t
    d1 = dest1.reshape(nt, 1, t)
    d2 = dest2.reshape(nt, 1, t)
    cur = pl.BlockSpec((1, 1, t), lambda i: (i, 0, 0), memory_space=pltpu.SMEM)
    nxt = pl.BlockSpec((1, 1, t), lambda i: (jnp.minimum(i + 1, nt - 1), 0, 0), memory_space=pltpu.SMEM)
    small = lambda arr: pl.BlockSpec(arr.shape, lambda i: (0,) * arr.ndim)
    return pl.pallas_call(
        _combine_kernel,
        grid=(nt,),
        in_specs=[cur, cur, nxt, nxt, pl.BlockSpec(memory_space=pl.ANY),
                  pl.BlockSpec((t, LANES), lambda i: (i, 0)),
                  pl.BlockSpec((t, d), lambda i: (i, 0)),
                  pl.BlockSpec((1, 1, d), lambda i: (i // per_b, 0, 0)),
                  small(ln_g), small(ln_b)],
        out_specs=pl.BlockSpec((t, d), lambda i: (i, 0)),
        out_shape=jax.ShapeDtypeStruct((n, d), F32),
        scratch_shapes=[pltpu.VMEM((2, t * TOKEN_SLAB, LANES), jnp.uint32),
                        pltpu.VMEM((2, t * TOKEN_SLAB, LANES), jnp.uint32),
                        pltpu.SemaphoreType.DMA((2,)), pltpu.SemaphoreType.DMA((2,))],
        compiler_params=_cparams(("arbitrary",)),
        name="combine",
    )(d1, d2, d1, d2, y_tiles, route, h1, g2, ln_g, ln_b)


def _hi_lo(w):
    hi = w.astype(BF16)
    return jnp.stack([hi, (w - hi.astype(F32)).astype(BF16)])


def _block_diag2(w):
    z = jnp.zeros_like(w[0])
    return jnp.concatenate([jnp.concatenate([w[0], z], axis=1), jnp.concatenate([z, w[1]], axis=1)], axis=0)


def kernel(x, c, ctx, c_ctx, emb_ln_g, emb_ln_b, w_mod, b_mod, w_in, tshift_mu, rwkv_w0, rwkv_w2, rwkv_a0, rwkv_a2, rwkv_g2, rwkv_k_k, rwkv_k_a, rwkv_r_k, rwkv_gn_g, rwkv_gn_b, ret_decay, ret_gn_g, ret_gn_b, w_out, ln1_g, ln1_b, router_group, router_group_bias, router_expert, router_expert_bias, expert_w_gate, expert_w_up, expert_w_down, ln2_g, ln2_b):
    assert w_mod.shape[0] == 1, "written for DEPTH == 1 (context outputs are never emitted)"
    b, n_tok, d = x.shape
    n_ctx = ctx.shape[1]
    row = lambda v: v.reshape(1, -1)

    c_rows = jnp.zeros((SUBLANES, d), F32).at[:b].set(c).at[b].set(c_ctx)
    mod = _modulation(c_rows, w_mod[0], row(b_mod[0]))
    sh1, s1, g1, sh2, s2, g2 = [mod[:b, j * d:(j + 1) * d].reshape(b, 1, d) for j in range(6)]
    sh1c, s1c = [jnp.broadcast_to(mod[b, j * d:(j + 1) * d].reshape(1, 1, d), (b, 1, d)) for j in range(2)]

    w_in_bf16 = w_in[0].astype(BF16)
    pr, pt = _in_proj(x, row(emb_ln_g), row(emb_ln_b), s1, sh1, w_in_bf16)
    pr_c, pt_c = _in_proj(ctx, row(emb_ln_g), row(emb_ln_b), s1c, sh1c, w_in_bf16)

    prep_params = (row(tshift_mu[0]), row(rwkv_w0[0]), _hi_lo(_block_diag2(rwkv_w2[0])), row(rwkv_a0[0]),
                   _hi_lo(_block_diag2(rwkv_a2[0])), _hi_lo(rwkv_g2[0]), row(rwkv_k_k[0]), row(rwkv_k_a[0]),
                   row(rwkv_r_k[0]),
                   _segment_ones(D_RWKV, RWKV_HEAD))
    lat = _rwkv_prepare(pr, prep_params, grid_shift=True)
    cx = _rwkv_prepare(pr_c, prep_params, grid_shift=False)
    r_l, v_l, kk_l, w_l, kd_l, bb_l, gate_l, bonus_l = lat
    r_c, v_c, kk_c, w_c, kd_c, bb_c, _, _ = cx

    y_f, y_b = _wkv7((r_l, v_l, kk_l, w_l, kd_l, bb_l), (r_c, v_c, kk_c, w_c, kd_c, bb_c), b, n_tok, n_ctx)

    cos_t, sin_t = _rope_tables(n_tok)
    t_f, t_b = _retention(pt, pt_c, ret_decay[0], cos_t, sin_t)

    wr = jnp.zeros((d, LANES), F32).at[:, :N_GROUPS].set(router_group[0])
    wr = wr.at[:, N_GROUPS:N_GROUPS + N_EXPERTS].set(router_expert[0])
    br = jnp.zeros((1, LANES), F32).at[0, :N_GROUPS].set(router_group_bias[0])
    br = br.at[0, N_GROUPS:N_GROUPS + N_EXPERTS].set(router_expert_bias[0].reshape(-1))
    vecs = (row(emb_ln_g), row(emb_ln_b), g1, s2, sh2, row(rwkv_gn_g[0]), row(rwkv_gn_b[0]),
            row(ret_gn_g[0]), row(ret_gn_b[0]), row(ln1_g[0]), row(ln1_b[0]), br)
    wr_hi_lo = _hi_lo(wr)
    mats = (_segment_ones(D_RWKV, RWKV_HEAD), w_out[0].astype(BF16), wr_hi_lo[0], wr_hi_lo[1])
    h1, u2, route, route_t, counts = _out_proj(x, y_f, y_b, bonus_l, gate_l, t_f, t_b, pt, vecs, mats)

    n_all = b * n_tok

    e1 = route_t[ROUTE_E1].astype(jnp.int32)
    e2 = route_t[ROUTE_E2].astype(jnp.int32)
    cnt = counts[0, :N_EXPERTS].astype(jnp.int32)
    padded = ((cnt + EXPERT_ROWS - 1) // EXPERT_ROWS) * EXPERT_ROWS
    pends = jnp.cumsum(padded)
    pstarts = pends - padded
    expert_ids = jnp.arange(N_EXPERTS, dtype=jnp.int32)
    start_of = lambda e: jnp.sum(jnp.where(e[:, None] == expert_ids[None, :], pstarts[None, :], 0), axis=1)
    dest1 = start_of(e1) + route_t[ROUTE_RANK1].astype(jnp.int32)
    dest2 = start_of(e2) + route_t[ROUTE_RANK2].astype(jnp.int32)
    n_blk = -(-(n_all * 2) // EXPERT_ROWS) + N_EXPERTS
    block_start = jnp.arange(n_blk, dtype=jnp.int32) * EXPERT_ROWS
    block_expert = jnp.minimum(jnp.sum((block_start[:, None] >= pends[None, :]).astype(jnp.int32), axis=1),
                               N_EXPERTS - 1)

    n_used = (pends[N_EXPERTS - 1:] // EXPERT_ROWS).astype(jnp.int32)
    x_slabs = _dispatch(u2, dest1, dest2, n_blk * EXPERT_ROWS)
    y_tiles = _expert_mlp(x_slabs, block_expert, n_used, expert_w_gate[0], expert_w_up[0], expert_w_down[0])
    out = _combine(y_tiles, dest1, dest2, route, h1.reshape(n_all, d), g2, row(ln2_g[0]), row(ln2_b[0]), n_tok)
    return out.reshape(b, n_tok, d)
```

```python
import functools
import math

import jax
import jax.numpy as jnp
import numpy as np
from jax import lax
from jax.experimental import pallas as pl
from jax.experimental.pallas import tpu as pltpu

F32 = jnp.float32
BF16 = jnp.bfloat16
HIGHEST = lax.Precision.HIGHEST

GRID_W = 64
D_RWKV = 512
RWKV_HEAD = 64
RWKV_HEADS = D_RWKV // RWKV_HEAD
DECAY_LORA = 64
AAA_LORA = 64
GATE_LORA = 128
D_RET = 512
RET_HEADS = 4
RET_HEAD = D_RET // RET_HEADS
RET_CHUNK = 128
RET_CHUNKS_PER_STEP = 4
RWKV_COLS = 3 * D_RWKV + 2 * (DECAY_LORA + AAA_LORA) + GATE_LORA
RET_COLS = 4 * D_RET
N_GROUPS = 4
EXPERTS_PER_GROUP = 8
N_EXPERTS = N_GROUPS * EXPERTS_PER_GROUP
EXPERT_HIDDEN = 512
MOE_BLOCK = 128
ROPE_BASE = 10000.0
LN_EPS = 1e-5
RWKV_GN_EPS = 64e-5
RET_GN_EPS = 1e-5
DEEPNORM_ALPHA = 2.0 ** 0.25
EXP_NEG_HALF = math.exp(-0.5)

LANES = 128
SUBLANES = 8
VMEM_LIMIT_BYTES = 56 * 1024 * 1024

WKV_CHUNK = 64


IN_PROJ_ROWS = 512
OUT_PROJ_ROWS = 512

TOKEN_SLAB = 4
EXPERT_ROWS = 512
COMBINE_ROWS = 256
DISPATCH_ROWS = 256
GATHER_UNROLL = 16
GATHER_PRIORITIES = (0, 1)


def _pack_bf16_pairs(x):
    half = x.shape[1] // 2

    def bf16_bits(v):
        b = lax.bitcast_convert_type(v, jnp.uint32)
        return (b + jnp.uint32(0x7FFF) + ((b >> 16) & jnp.uint32(1))) >> 16

    return bf16_bits(x[:, :half]) | (bf16_bits(x[:, half:]) << 16)


def _unpack_bf16_pairs(p):
    lo = lax.bitcast_convert_type(p << 16, F32)
    hi = lax.bitcast_convert_type(p & jnp.uint32(0xFFFF0000), F32)
    return jnp.concatenate([lo, hi], axis=-1)


def _store_token_slabs(ref, x):
    rows = x.shape[0]
    for j in range(TOKEN_SLAB):
        ref[pl.ds(j, rows, stride=TOKEN_SLAB), :] = x[:, j * LANES:(j + 1) * LANES]


def _load_token_slabs(ref, rows):
    return jnp.concatenate([ref[pl.ds(j, rows, stride=TOKEN_SLAB), :] for j in range(TOKEN_SLAB)], axis=-1)


def _cparams(sem):
    return pltpu.CompilerParams(dimension_semantics=sem, vmem_limit_bytes=VMEM_LIMIT_BYTES)


def _layer_norm(x, g, b, eps=LN_EPS):
    mu = jnp.mean(x, axis=-1, keepdims=True)
    xc = x - mu
    var = jnp.mean(xc * xc, axis=-1, keepdims=True)
    return xc * lax.rsqrt(var + eps) * g + b


def _sigmoid(x):
    return 1.0 / (1.0 + jnp.exp(-x))


def _split_bf16(x):
    hi = x.astype(BF16)
    return hi, (x - hi.astype(F32)).astype(BF16)


def _segsum(x, ones_bf16):
    t = x.shape[0]
    s = jnp.dot(jnp.concatenate(_split_bf16(x), axis=0), ones_bf16, preferred_element_type=F32)
    return s[:t] + s[t:]


def _dot_split(x, w_hi, w_lo):
    hi, lo = _split_bf16(x)
    acc = jnp.dot(hi, w_hi, preferred_element_type=F32)
    acc = acc + jnp.dot(lo, w_hi, preferred_element_type=F32)
    return acc + jnp.dot(hi, w_lo, preferred_element_type=F32)


def _segment_ones(width, seg):
    idx = np.arange(width) // seg
    return jnp.asarray(idx[:, None] == idx[None, :], dtype=BF16)


def _mod_kernel(c_ref, w_ref, b_ref, o_ref):
    c = c_ref[...]
    sc = c * _sigmoid(c)
    o_ref[...] = jnp.dot(sc, w_ref[...], precision=HIGHEST, preferred_element_type=F32) + b_ref[...]


def _modulation(c_rows, w_mod, b_mod):
    rows, d = c_rows.shape
    n = w_mod.shape[1]
    tn = 1536
    return pl.pallas_call(
        _mod_kernel,
        grid=(n // tn,),
        in_specs=[pl.BlockSpec((rows, d), lambda j: (0, 0)),
                  pl.BlockSpec((d, tn), lambda j: (0, j)),
                  pl.BlockSpec((1, tn), lambda j: (0, j))],
        out_specs=pl.BlockSpec((rows, tn), lambda j: (0, j)),
        out_shape=jax.ShapeDtypeStruct((rows, n), F32),
        compiler_params=_cparams(("arbitrary",)),
        name="modulation",
    )(c_rows, w_mod, b_mod)


def _in_proj_kernel(x_ref, g_ref, b_ref, s_ref, sh_ref, w_ref, pr_ref, pt_ref):
    h = _layer_norm(x_ref[0], g_ref[...], b_ref[...])
    u = h * (1.0 + s_ref[0]) + sh_ref[0]
    p = jnp.dot(u.astype(BF16), w_ref[...], preferred_element_type=F32)
    pr_ref[0] = p[:, :RWKV_COLS]
    pt_ref[0] = p[:, RWKV_COLS:]


def _in_proj(x, ln_g, ln_b, s1, sh1, w_in_bf16):
    b, n, d = x.shape
    tm = min(IN_PROJ_ROWS, n)
    cols = w_in_bf16.shape[1]
    return pl.pallas_call(
        _in_proj_kernel,
        grid=(b, n // tm),
        in_specs=[pl.BlockSpec((1, tm, d), lambda bi, i: (bi, i, 0)),
                  pl.BlockSpec((1, d), lambda bi, i: (0, 0)),
                  pl.BlockSpec((1, d), lambda bi, i: (0, 0)),
                  pl.BlockSpec((1, 1, d), lambda bi, i: (bi, 0, 0)),
                  pl.BlockSpec((1, 1, d), lambda bi, i: (bi, 0, 0)),
                  pl.BlockSpec((d, cols), lambda bi, i: (0, 0))],
        out_specs=[pl.BlockSpec((1, tm, RWKV_COLS), lambda bi, i: (bi, i, 0)),
                   pl.BlockSpec((1, tm, RET_COLS), lambda bi, i: (bi, i, 0))],
        out_shape=[jax.ShapeDtypeStruct((b, n, RWKV_COLS), F32),
                   jax.ShapeDtypeStruct((b, n, RET_COLS), F32)],
        compiler_params=_cparams(("arbitrary", "arbitrary")),
        name="in_proj",
    )(x, ln_g, ln_b, s1, sh1, w_in_bf16)


def _rwkv_prepare_kernel(cur_ref, prev_ref, next_ref, mu_ref, w0_ref, w2_ref, a0_ref, a2_ref, g2_ref,
                         kk_scale_ref, ka_ref, rk_ref, ones_ref,
                         r_ref, v_ref, kk_ref, w_ref, kd_ref, bb_ref, g_ref, bonus_ref,
                         *, grid_shift, n_tok):
    cur = cur_ref[0]
    t, c = cur.shape
    row = lax.broadcasted_iota(jnp.int32, (t, c), 0)
    lane = lax.broadcasted_iota(jnp.int32, (t, c), 1)
    prev_tok = pltpu.roll(cur, 1, 0)
    next_tok = pltpu.roll(cur, t - 1, 0)
    if grid_shift:
        col = row & (GRID_W - 1)
        tok = row + pl.program_id(1) * t
        left = jnp.where(col > 0, prev_tok, 0.0)
        right = jnp.where(col < GRID_W - 1, next_tok, 0.0)
        up = jnp.where(tok >= GRID_W, jnp.concatenate([prev_ref[0], cur[:t - GRID_W]], axis=0), 0.0)
        down = jnp.where(tok < n_tok - GRID_W, jnp.concatenate([cur[GRID_W:], next_ref[0]], axis=0), 0.0)
        cm = lane & 3
        shifted = jnp.where(cm == 0, left, jnp.where(cm == 1, right, jnp.where(cm == 2, up, down)))
    else:
        prev_tok = jnp.where(row > 0, prev_tok, 0.0)
        next_tok = jnp.where(row < t - 1, next_tok, 0.0)
        shifted = jnp.where((lane & 1) == 0, prev_tok, next_tok)
    pm = cur + mu_ref[...] * (shifted - cur)

    r = pm[:, 0:D_RWKV]
    k = pm[:, D_RWKV:2 * D_RWKV]
    v = pm[:, 2 * D_RWKV:3 * D_RWKV]
    o = 3 * D_RWKV
    lw = pm[:, o:o + 2 * DECAY_LORA]
    la = pm[:, o + 2 * DECAY_LORA:o + 2 * (DECAY_LORA + AAA_LORA)]
    lg = pm[:, o + 2 * (DECAY_LORA + AAA_LORA):]

    w = w0_ref[...] + _dot_split(jnp.tanh(lw), w2_ref[0], w2_ref[1])
    log_decay = -EXP_NEG_HALF * _sigmoid(w)
    a = _sigmoid(a0_ref[...] + _dot_split(la, a2_ref[0], a2_ref[1]))
    gate = _dot_split(_sigmoid(lg), g2_ref[0], g2_ref[1])

    ones = ones_ref[...]
    kk_raw = k * kk_scale_ref[...]
    kk = kk_raw / jnp.maximum(jnp.sqrt(_segsum(kk_raw * kk_raw, ones)), 1e-12)
    ka = ka_ref[...]
    a0 = a[:, :D_RWKV]
    a1 = a[:, D_RWKV:]
    kd0 = k * (1.0 + (a0 - 1.0) * ka)
    kd1 = k * (1.0 + (a1 - 1.0) * ka)
    bonus = _segsum(r * (kd0 + kd1) * rk_ref[...], ones) * v

    r_ref[0] = r
    v_ref[0] = v
    kk_ref[0] = kk
    w_ref[0] = log_decay
    kd_ref[0] = jnp.concatenate([kd0, kd1], axis=-1)
    bb_ref[0] = jnp.concatenate([kk * a0, kk * a1], axis=-1)
    g_ref[0] = gate
    bonus_ref[0] = bonus


def _rwkv_prepare(pr, params, grid_shift):
    b, n, c = pr.shape
    t = 256
    if not grid_shift:
        assert n == t, "sequence token shift is written for a single tile"
    halo_blocks = n // GRID_W
    per_tile = t // GRID_W
    small = lambda shape: pl.BlockSpec(shape, lambda bi, i: (0,) * len(shape))
    tok_spec = lambda width: pl.BlockSpec((1, t, width), lambda bi, i: (bi, i, 0))
    out_widths = (D_RWKV, D_RWKV, D_RWKV, 2 * D_RWKV, 2 * D_RWKV, 2 * D_RWKV, D_RWKV, D_RWKV)
    kernel = functools.partial(_rwkv_prepare_kernel, grid_shift=grid_shift, n_tok=n)
    return pl.pallas_call(
        kernel,
        grid=(b, n // t),
        in_specs=[tok_spec(c),
                  pl.BlockSpec((1, GRID_W, c), lambda bi, i: (bi, jnp.maximum(i * per_tile - 1, 0), 0)),
                  pl.BlockSpec((1, GRID_W, c),
                               lambda bi, i: (bi, jnp.minimum((i + 1) * per_tile, halo_blocks - 1), 0)),
                  small((1, c)), small((1, 2 * D_RWKV)), small((2, 2 * DECAY_LORA, 2 * D_RWKV)),
                  small((1, 2 * D_RWKV)), small((2, 2 * AAA_LORA, 2 * D_RWKV)), small((2, GATE_LORA, D_RWKV)),
                  small((1, D_RWKV)), small((1, D_RWKV)), small((1, D_RWKV)), small((D_RWKV, D_RWKV))],
        out_specs=[tok_spec(wd) for wd in out_widths],
        out_shape=[jax.ShapeDtypeStruct((b, n, wd), F32) for wd in out_widths],
        compiler_params=_cparams(("arbitrary", "arbitrary")),
        name="rwkv_prepare",
    )(pr, pr, pr, *params)


def _bdot(a, b):
    return jnp.dot(a.astype(BF16), b.astype(BF16), preferred_element_type=F32)


def _bdot_nt(a, b):
    return lax.dot_general(a.astype(BF16), b.astype(BF16), (((1,), (1,)), ((), ())), preferred_element_type=F32)


def _bdot_tn(a, b):
    return lax.dot_general(a.astype(BF16), b.astype(BF16), (((0,), (0,)), ((), ())), preferred_element_type=F32)


def _wkv7_chunk_kernel(*refs, n_ctx_chunks):
    c = WKV_CHUNK
    p = 2 * c
    n_in = 12
    in_refs = (refs[:n_in], refs[n_in:2 * n_in])
    y_refs = refs[2 * n_in:2 * n_in + 2]
    state_ref = refs[2 * n_in + 2]
    n = pl.program_id(0)
    n_batch = in_refs[0][0].shape[0]
    pairs_per_batch = RWKV_HEADS // 2
    pairs_per_dir = n_batch * pairs_per_batch

    @pl.when(n == 0)
    def _():
        state_ref[...] = jnp.zeros_like(state_ref)

    is_ctx = n < n_ctx_chunks
    ti = lax.broadcasted_iota(jnp.int32, (c, c), 0)
    tj = lax.broadcasted_iota(jnp.int32, (c, c), 1)
    ri = lax.broadcasted_iota(jnp.int32, (p, p), 0)
    ci = lax.broadcasted_iota(jnp.int32, (p, p), 1)
    same_head = (ri >= c) == (ci >= c)
    ii = ri & (c - 1)
    jj = ci & (c - 1)
    eye = (ri == ci).astype(F32)
    first = lax.broadcasted_iota(jnp.int32, (c, p), 1) < RWKV_HEAD

    def stack(x):
        return jnp.concatenate([jnp.where(first, x, 0.0), jnp.where(first, 0.0, x)], axis=0)

    def unstack(x):
        return x[:c] + x[c:]

    a_st, r_st, k_st, b_st, k2_st, b2_st, v_st, g_chunk, earlier, upto_self = ([] for _ in range(10))
    for d in range(2):
        r_l, v_l, kk_l, lw_l, kd_l, bb_l, r_c, v_c, kk_c, lw_c, kd_c, bb_c = in_refs[d]
        pick = lambda xc, xl: jnp.concatenate(
            [jnp.where(is_ctx, xc[bi], xl[bi]) for bi in range(n_batch)], axis=-1)
        r, v, kk, lw, kd, bb = (pick(r_c, r_l), pick(v_c, v_l), pick(kk_c, kk_l), pick(lw_c, lw_l),
                                pick(kd_c, kd_l), pick(bb_c, bb_l))
        before = (tj < ti) if d == 0 else (tj > ti)
        upto = (before | (ti == tj)).astype(BF16)
        hi = lw.astype(BF16)
        r1 = lw - hi.astype(F32)
        mid = r1.astype(BF16)
        lo = (r1 - mid.astype(F32)).astype(BF16)
        cum = (jnp.dot(upto, hi, preferred_element_type=F32) + jnp.dot(upto, mid, preferred_element_type=F32)
               + jnp.dot(upto, lo, preferred_element_type=F32))
        tot = jnp.sum(lw, axis=0, keepdims=True)
        e_neg = jnp.exp(-cum)
        e_rem = jnp.exp(tot - cum)
        alpha = kk * jnp.exp(cum - lw)
        rho = r * jnp.exp(cum)
        beta = bb * e_neg
        kappa = kd * e_neg
        kappa_rem = kd * e_rem
        beta_rem = bb * e_rem
        g_all = jnp.exp(tot)
        pair_before = same_head & ((jj < ii) if d == 0 else (jj > ii))
        pair_upto = pair_before | (ri == ci)
        for hp in range(pairs_per_dir):
            sl = slice(hp * p, (hp + 1) * p)
            a_st.append(stack(alpha[:, sl]))
            r_st.append(stack(rho[:, sl]))
            k_st.append(stack(kappa[:, sl]))
            b_st.append(stack(beta[:, sl]))
            k2_st.append(stack(kappa_rem[:, sl]))
            b2_st.append(stack(beta_rem[:, sl]))
            v_st.append(stack(v[:, sl]))
            g_chunk.append(g_all[:, sl])
            earlier.append(pair_before)
            upto_self.append(pair_upto)

    pairs = range(2 * pairs_per_dir)
    g = [_bdot_nt(jnp.concatenate([a_st[h], r_st[h]], axis=0), jnp.concatenate([k_st[h], b_st[h]], axis=0))
         for h in pairs]
    m1 = [jnp.where(earlier[h], g[h][:p, :p], 0.0) for h in pairs]
    m2 = [jnp.where(earlier[h], g[h][:p, p:], 0.0) for h in pairs]
    n1 = [jnp.where(upto_self[h], g[h][p:, :p], 0.0) for h in pairs]
    n2 = [jnp.where(upto_self[h], g[h][p:, p:], 0.0) for h in pairs]

    in_block = (ii >> 3) == (jj >> 3)
    pw = [-jnp.where(in_block, m2[h], 0.0) for h in pairs]
    inv = [eye + pw[h] for h in pairs]
    pw = [_bdot(pw[h], pw[h]) for h in pairs]
    both = [_bdot(jnp.concatenate([inv[h], pw[h]], axis=0), pw[h]) for h in pairs]
    inv = [inv[h] + both[h][:p] for h in pairs]
    inv = [inv[h] + _bdot(inv[h], both[h][p:]) for h in pairs]
    for sh in (3, 4, 5):
        off = ((ii >> (sh + 1)) == (jj >> (sh + 1))) & ((ii >> sh) != (jj >> sh))
        left = [_bdot(inv[h], jnp.where(off, m2[h], 0.0)) for h in pairs]
        inv = [inv[h] - _bdot(left[h], inv[h]) for h in pairs]

    mnv = [_bdot(jnp.concatenate([m1[h], n1[h]], axis=0), v_st[h]) for h in pairs]
    m1v = [mnv[h][:p] for h in pairs]
    n1v = [mnv[h][p:] for h in pairs]
    au = [_bdot(inv[h], jnp.concatenate([a_st[h], m1v[h]], axis=1)) for h in pairs]
    nn = [_bdot(n2[h], au[h]) for h in pairs]
    pc = [_bdot_tn(b2_st[h], au[h][:, :p]) for h in pairs]
    qc_t = [_bdot_tn(jnp.concatenate([v_st[h], -au[h][:, p:]], axis=0),
                     jnp.concatenate([k2_st[h], b2_st[h]], axis=0)) for h in pairs]
    s0 = [state_ref[h] for h in pairs]
    y = [_bdot_nt(unstack(r_st[h] - nn[h][:, :p]), s0[h]) + unstack(n1v[h] - nn[h][:, p:]) for h in pairs]
    s_dec = [_bdot_nt(s0[h], pc[h]) for h in pairs]
    for h in pairs:
        d, hp = divmod(h, pairs_per_dir)
        bi, hpb = divmod(hp, pairs_per_batch)
        y_refs[d][bi, :, hpb * p:(hpb + 1) * p] = y[h]
        state_ref[h] = s0[h] * g_chunk[h] - s_dec[h] + qc_t[h]


def _wkv7(lat, ctx, b, n_tok, n_ctx):
    c = WKV_CHUNK
    ncx = n_ctx // c
    nl = n_tok // c
    lat_idx = (lambda n: jnp.maximum(n - ncx, 0), lambda n: nl - 1 - jnp.maximum(n - ncx, 0))
    ctx_idx = (lambda n: jnp.minimum(n, ncx - 1), lambda n: ncx - 1 - jnp.minimum(n, ncx - 1))

    def specs(idx, d):
        shared = pl.BlockSpec((b, c, D_RWKV), lambda n: (0, idx(n), 0))
        per_dir = pl.BlockSpec((b, c, D_RWKV), lambda n: (0, idx(n), d))
        return [shared, shared, shared, per_dir, per_dir, per_dir]

    in_specs, args = [], []
    for d in range(2):
        in_specs += specs(lat_idx[d], d) + specs(ctx_idx[d], d)
        args += list(lat) + list(ctx)
    return pl.pallas_call(
        functools.partial(_wkv7_chunk_kernel, n_ctx_chunks=ncx),
        grid=(ncx + nl,),
        in_specs=in_specs,
        out_specs=[pl.BlockSpec((b, c, D_RWKV), lambda n, d=d: (0, lat_idx[d](n), 0)) for d in range(2)],
        out_shape=[jax.ShapeDtypeStruct((b, n_tok, D_RWKV), F32)] * 2,
        scratch_shapes=[pltpu.VMEM((2 * b * RWKV_HEADS // 2, 2 * RWKV_HEAD, 2 * RWKV_HEAD), F32)],
        compiler_params=_cparams(("arbitrary",)),
        name="wkv7_chunk",
    )(*args)


def _rope(z, cos_t, sin_t):
    lane = lax.broadcasted_iota(jnp.int32, z.shape, 1)
    half = RET_HEAD // 4
    partner = jnp.where((lane & (2 * half - 1)) < half, pltpu.roll(z, RET_HEAD - half, 1), pltpu.roll(z, half, 1))
    return z * cos_t + partner * sin_t


def _retention_kernel(dec_ref, fwd_ref, bwd_ref, ctx_ref, cosf_ref, sinf_ref, cosb_ref, sinb_ref,
                      yf_ref, yb_ref, state_ref, dmat_ref, tail_ref, head_ref, cdec_ref):
    c = RET_CHUNK
    scale = RET_HEAD ** -0.5
    ii = lax.broadcasted_iota(jnp.int32, (c, c), 0)
    jj = lax.broadcasted_iota(jnp.int32, (c, c), 1)
    pos = lax.broadcasted_iota(jnp.int32, (c, RET_HEAD), 0).astype(F32)
    n_ctx_chunks = ctx_ref.shape[1] // c

    def head_slices(ref_val, h):
        q = ref_val[:, h * RET_HEAD:(h + 1) * RET_HEAD]
        k = ref_val[:, D_RET + h * RET_HEAD:D_RET + (h + 1) * RET_HEAD]
        v = ref_val[:, 2 * D_RET + h * RET_HEAD:2 * D_RET + (h + 1) * RET_HEAD]
        return q, k, v

    n_batch = fwd_ref.shape[0]
    heads = [(d, h) for d in range(2) for h in range(RET_HEADS)]
    chains = [(bi, d, h) for bi in range(n_batch) for d, h in heads]

    @pl.when(pl.program_id(0) == 0)
    def _():
        for d, h in heads:
            x = jnp.full((1, RET_HEAD), dec_ref[d, h], F32)
            lg = -(jnp.maximum(x, 0.0) + jnp.log(1.0 + jnp.exp(-jnp.abs(x))))
            chunk_decay = jnp.exp(lg * float(c))
            tail = jnp.exp(lg * ((c - 1.0 - pos) if d == 0 else pos))
            rel = (ii - jj) if d == 0 else (jj - ii)
            mask = (rel >= 0) if d == 0 else (rel > 0)
            dmat_ref[d, h] = jnp.where(mask, jnp.exp(lg * jnp.maximum(rel, 0).astype(F32)), 0.0)
            tail_ref[d, h] = tail
            head_ref[d, h] = jnp.exp(lg * ((pos + 1.0) if d == 0 else (c - pos)))
            cdec_ref[d, h] = jnp.broadcast_to(chunk_decay, (SUBLANES, RET_HEAD))
            order = range(n_ctx_chunks) if d == 0 else range(n_ctx_chunks - 1, -1, -1)
            for bi in range(n_batch):
                s = jnp.zeros((RET_HEAD, RET_HEAD), F32)
                for cc in order:
                    _, kc, vc = head_slices(ctx_ref[bi, cc * c:(cc + 1) * c, :], h)
                    s = s * chunk_decay + _bdot_tn(kc * scale * tail, vc)
                state_ref[bi, d, h] = s

    n_sub = RET_CHUNKS_PER_STEP

    def rows(d, sub):
        first = sub * c if d == 0 else (n_sub - 1 - sub) * c
        return slice(first, first + c)

    work = [(bi, d, h, sub) for sub in range(n_sub) for bi, d, h in chains]
    qkv = {}
    for bi, d, h, sub in work:
        blk = (fwd_ref if d == 0 else bwd_ref)[bi, rows(d, sub), :]
        cos_t = (cosf_ref if d == 0 else cosb_ref)[rows(d, sub), :]
        sin_t = (sinf_ref if d == 0 else sinb_ref)[rows(d, sub), :]
        q, k, v = head_slices(blk, h)
        qkv[bi, d, h, sub] = (_rope(q, cos_t, sin_t), _rope(k, cos_t, sin_t) * scale, v.astype(BF16))
    scores = {w: _bdot_nt(qkv[w][0], qkv[w][1]) for w in work}
    inner = {w: _bdot(scores[w] * dmat_ref[w[1], w[2]], qkv[w][2]) for w in work}
    upd = {w: _bdot_tn(qkv[w][1] * tail_ref[w[1], w[2]], qkv[w][2]) for w in work}
    state = {ch: state_ref[ch] for ch in chains}
    for sub in range(n_sub):
        cross = {ch: _bdot(qkv[ch + (sub,)][0] * head_ref[ch[1], ch[2]], state[ch]) for ch in chains}
        for bi, d, h in chains:
            ch = (bi, d, h)
            state[ch] = state[ch] * cdec_ref[d, h, 0:1, :] + upd[ch + (sub,)]
            out_ref = yf_ref if d == 0 else yb_ref
            out_ref[bi, rows(d, sub), h * RET_HEAD:(h + 1) * RET_HEAD] = inner[ch + (sub,)] + cross[ch]
    for ch in chains:
        state_ref[ch] = state[ch]


def _retention(pt, pt_ctx, ret_decay, cos_t, sin_t):
    b, n, _ = pt.shape
    c = RET_CHUNK
    rows = RET_CHUNKS_PER_STEP * c
    steps = n // rows
    qkv = 3 * D_RET
    fwd = lambda i: (0, i, 0)
    bwd = lambda i: (0, steps - 1 - i, 0)
    return pl.pallas_call(
        _retention_kernel,
        grid=(steps,),
        in_specs=[pl.BlockSpec(memory_space=pltpu.SMEM),
                  pl.BlockSpec((b, rows, qkv), fwd),
                  pl.BlockSpec((b, rows, qkv), bwd),
                  pl.BlockSpec((b, pt_ctx.shape[1], qkv), lambda i: (0, 0, 0)),
                  pl.BlockSpec((rows, RET_HEAD), lambda i: (i, 0)),
                  pl.BlockSpec((rows, RET_HEAD), lambda i: (i, 0)),
                  pl.BlockSpec((rows, RET_HEAD), lambda i: (steps - 1 - i, 0)),
                  pl.BlockSpec((rows, RET_HEAD), lambda i: (steps - 1 - i, 0))],
        out_specs=[pl.BlockSpec((b, rows, D_RET), fwd), pl.BlockSpec((b, rows, D_RET), bwd)],
        out_shape=[jax.ShapeDtypeStruct((b, n, D_RET), F32), jax.ShapeDtypeStruct((b, n, D_RET), F32)],
        scratch_shapes=[pltpu.VMEM((b, 2, RET_HEADS, RET_HEAD, RET_HEAD), F32),
                        pltpu.VMEM((2, RET_HEADS, c, c), F32),
                        pltpu.VMEM((2, RET_HEADS, c, RET_HEAD), F32),
                        pltpu.VMEM((2, RET_HEADS, c, RET_HEAD), F32),
                        pltpu.VMEM((2, RET_HEADS, SUBLANES, RET_HEAD), F32)],
        compiler_params=_cparams(("arbitrary",)),
        name="retention",
    )(ret_decay, pt, pt, pt_ctx, cos_t, sin_t, cos_t, sin_t)


def _rope_tables(n_tok):
    nf = RET_HEAD // 4
    lane = np.arange(RET_HEAD)
    inv = ROPE_BASE ** (-jnp.arange(nf, dtype=F32) / nf)
    t = jnp.arange(n_tok)
    pos = jnp.where((lane // (2 * nf) == 0)[None, :], (t // GRID_W)[:, None], (t % GRID_W)[:, None]).astype(F32)
    ang = pos * inv[lane % nf][None, :]
    sign = jnp.where((lane % (2 * nf)) < nf, -1.0, 1.0).astype(F32)
    return jnp.cos(ang), jnp.sin(ang) * sign[None, :]


def _group_norm(y, ones, seg, eps, g, b):
    mu = _segsum(y, ones) * (1.0 / seg)
    yc = y - mu
    var = _segsum(yc * yc, ones) * (1.0 / seg)
    return yc * lax.rsqrt(var + eps) * g + b


def _out_proj_kernel(x_ref, yf_ref, yb_ref, bonus_ref, gate_ref, tf_ref, tb_ref, gt_ref,
                     embg_ref, embb_ref, g1_ref, s2_ref, sh2_ref, rgn_g_ref, rgn_b_ref, tgn_g_ref, tgn_b_ref,
                     ones_r_ref, wout_ref, ln1g_ref, ln1b_ref, wrh_ref, wrl_ref, br_ref,
                     h1_ref, u2_ref, route_ref, route_t_ref, count_ref, carry_ref):
    @pl.when((pl.program_id(0) == 0) & (pl.program_id(1) == 0))
    def _():
        carry_ref[...] = jnp.zeros_like(carry_ref)

    y = yf_ref[0] + yb_ref[0]
    o_rwkv = _group_norm(y, ones_r_ref[...], RWKV_HEAD, RWKV_GN_EPS, rgn_g_ref[...], rgn_b_ref[...])
    o_rwkv = (o_rwkv + bonus_ref[0]) * gate_ref[0]
    yt = tf_ref[0] + tb_ref[0]
    gt = gt_ref[0]
    tgn_g = tgn_g_ref[...]
    tgn_b = tgn_b_ref[...]
    o_ret = jnp.concatenate(
        [_layer_norm(yt[:, h * RET_HEAD:(h + 1) * RET_HEAD], tgn_g[:, h * RET_HEAD:(h + 1) * RET_HEAD],
                     tgn_b[:, h * RET_HEAD:(h + 1) * RET_HEAD], RET_GN_EPS) for h in range(RET_HEADS)], axis=-1)
    o_ret = o_ret * (gt * _sigmoid(gt))
    cat = jnp.concatenate([o_rwkv, o_ret], axis=-1).astype(BF16)
    mix = jnp.dot(cat, wout_ref[...], preferred_element_type=F32)
    h = _layer_norm(x_ref[0], embg_ref[...], embb_ref[...])
    h1 = _layer_norm(DEEPNORM_ALPHA * h + g1_ref[0] * mix, ln1g_ref[...], ln1b_ref[...])
    u2 = h1 * (1.0 + s2_ref[0]) + sh2_ref[0]
    h1_ref[0] = h1
    _store_token_slabs(u2_ref, _pack_bf16_pairs(u2))
    route = _route_tile(_dot_split(u2, wrh_ref[...], wrl_ref[...]) + br_ref[...], carry_ref)
    route_ref[...] = route
    route_t_ref[...] = route.T[:SUBLANES]
    count_ref[...] = carry_ref[...]


def _out_proj(x, y_f, y_b, bonus, gate, t_f, t_b, pt, vecs, mats):
    b, n, d = x.shape
    t = OUT_PROJ_ROWS
    tok = lambda width: pl.BlockSpec((1, t, width), lambda bi, i: (bi, i, 0))
    per_b = pl.BlockSpec((1, 1, d), lambda bi, i: (bi, 0, 0))
    small = lambda arr: pl.BlockSpec(arr.shape, lambda bi, i: (0,) * arr.ndim)
    (embg, embb, g1, s2, sh2, rgn_g, rgn_b, tgn_g, tgn_b, ln1g, ln1b, br) = vecs
    (ones_r, wout, wr_hi, wr_lo) = mats
    gt_spec = pl.BlockSpec((1, t, D_RET), lambda bi, i: (bi, i, 3))
    tiles = n // t
    flat = lambda bi, i: (bi * tiles + i, 0)
    args = (x, y_f, y_b, bonus, gate, t_f, t_b, pt, embg, embb, g1, s2, sh2, rgn_g, rgn_b, tgn_g, tgn_b,
            ones_r, wout, ln1g, ln1b, wr_hi, wr_lo, br)
    in_specs = [tok(d)] + [tok(D_RWKV)] * 6 + [gt_spec, small(embg), small(embb), per_b, per_b, per_b,
                                                small(rgn_g), small(rgn_b), small(tgn_g), small(tgn_b),
                                                small(ones_r), small(wout), small(ln1g),
                                                small(ln1b), small(wr_hi), small(wr_lo), small(br)]
    return pl.pallas_call(
        _out_proj_kernel,
        grid=(b, n // t),
        in_specs=in_specs,
        out_specs=[tok(d), pl.BlockSpec((t * TOKEN_SLAB, LANES), flat),
                   pl.BlockSpec((t, LANES), flat),
                   pl.BlockSpec((SUBLANES, t), lambda bi, i: (0, bi * tiles + i)),
                   pl.BlockSpec((SUBLANES, LANES), lambda bi, i: (0, 0))],
        out_shape=[jax.ShapeDtypeStruct((b, n, d), F32),
                   jax.ShapeDtypeStruct((b * n * TOKEN_SLAB, LANES), jnp.uint32),
                   jax.ShapeDtypeStruct((b * n, LANES), F32),
                   jax.ShapeDtypeStruct((SUBLANES, b * n), F32),
                   jax.ShapeDtypeStruct((SUBLANES, LANES), F32)],
        scratch_shapes=[pltpu.VMEM((SUBLANES, LANES), F32)],
        compiler_params=_cparams(("arbitrary", "arbitrary")),
        name="out_proj",
    )(*args)


ROUTE_E1, ROUTE_E2, ROUTE_G1, ROUTE_G2, ROUTE_RANK1, ROUTE_RANK2 = range(6)


def _lane_argmax(x, valid, lane):
    m = jnp.max(jnp.where(valid, x, -jnp.inf), axis=-1, keepdims=True)
    idx = jnp.min(jnp.where(valid & (x == m), lane, float(LANES)), axis=-1, keepdims=True)
    return m, idx


def _route_tile(lg, carry_ref):
    t = lg.shape[0]
    lane = lax.broadcasted_iota(jnp.int32, lg.shape, 1).astype(F32)
    gmask = lane < N_GROUPS
    gmax = jnp.max(jnp.where(gmask, lg, -jnp.inf), axis=-1, keepdims=True)
    gexp = jnp.where(gmask, jnp.exp(lg - gmax), 0.0)
    gp = gexp / jnp.sum(gexp, axis=-1, keepdims=True)
    g_w, g_i = _lane_argmax(gp, gmask, lane)

    lo = N_GROUPS + EXPERTS_PER_GROUP * g_i
    emask = (lane >= lo) & (lane < lo + EXPERTS_PER_GROUP)
    emax = jnp.max(jnp.where(emask, lg, -jnp.inf), axis=-1, keepdims=True)
    eexp = jnp.where(emask, jnp.exp(lg - emax), 0.0)
    ep = eexp / jnp.sum(eexp, axis=-1, keepdims=True)
    p1, i1 = _lane_argmax(ep, emask, lane)
    p2, i2 = _lane_argmax(ep, emask & (lane != i1), lane)
    denom = p1 + p2
    gate1 = g_w * p1 / denom
    gate2 = g_w * p2 / denom
    e1 = i1 - N_GROUPS
    e2 = i2 - N_GROUPS

    oh1 = (lane == e1).astype(F32)
    oh2 = (lane == e2).astype(F32)
    cnt = oh1 + oh2
    ri = lax.broadcasted_iota(jnp.int32, (t, t), 0)
    ci = lax.broadcasted_iota(jnp.int32, (t, t), 1)
    before = (ci < ri).astype(BF16)
    seen = jnp.dot(before, cnt.astype(BF16), preferred_element_type=F32) + carry_ref[0:1, :]
    rank1 = jnp.sum(oh1 * seen, axis=-1, keepdims=True)
    rank2 = jnp.sum(oh2 * seen, axis=-1, keepdims=True)
    carry_ref[0:1, :] = carry_ref[0:1, :] + jnp.sum(cnt, axis=0, keepdims=True)

    out = jnp.zeros(lg.shape, F32)
    for slot, val in ((ROUTE_E1, e1.astype(F32)), (ROUTE_E2, e2.astype(F32)), (ROUTE_G1, gate1),
                      (ROUTE_G2, gate2), (ROUTE_RANK1, rank1), (ROUTE_RANK2, rank2)):
        out = jnp.where(lane == slot, val, out)
    return out


def _tile_gather_copy(src_hbm, idx_ref, buf, sem, slot, r):
    src = src_hbm.at[pl.ds(pl.multiple_of(idx_ref[0, 0, r] * TOKEN_SLAB, TOKEN_SLAB), TOKEN_SLAB), :]
    dst = buf.at[slot, pl.ds(pl.multiple_of(r * TOKEN_SLAB, TOKEN_SLAB), TOKEN_SLAB), :]
    return pltpu.make_async_copy(src, dst, sem.at[slot])


def _start_tile_gather(src_hbm, idx_ref, buf, sem, slot, rows, priorities):
    def body(g, carry):
        for j in range(GATHER_UNROLL):
            copy = _tile_gather_copy(src_hbm, idx_ref, buf, sem, slot, g * GATHER_UNROLL + j)
            copy.start(priority=priorities[j % len(priorities)])
        return carry
    lax.fori_loop(0, rows // GATHER_UNROLL, body, 0)


def _wait_tile_gather(src_hbm, idx_ref, buf, sem, slot, rows):
    del idx_ref
    whole = src_hbm.at[pl.ds(0, rows * TOKEN_SLAB), :]
    pltpu.make_async_copy(whole, buf.at[slot], sem.at[slot]).wait()


def _dispatch_kernel(d1_ref, d2_ref, u_ref, x_init_hbm, x_hbm, sem):
    del x_init_hbm
    rows = d1_ref.shape[2]

    def slab(ref, index):
        return ref.at[pl.ds(pl.multiple_of(index * TOKEN_SLAB, TOKEN_SLAB), TOKEN_SLAB), :]

    def body(g, carry):
        for j in range(GATHER_UNROLL):
            r = g * GATHER_UNROLL + j
            src = slab(u_ref, r)
            pltpu.make_async_copy(src, slab(x_hbm, d1_ref[0, 0, r]), sem).start(priority=j % 2)
            pltpu.make_async_copy(src, slab(x_hbm, d2_ref[0, 0, r]), sem).start(priority=(j + 1) % 2)
        return carry

    lax.fori_loop(0, rows // GATHER_UNROLL, body, 0)
    whole = x_hbm.at[pl.ds(0, rows * TOKEN_SLAB), :]
    pltpu.make_async_copy(u_ref, whole, sem).wait()
    pltpu.make_async_copy(u_ref, whole, sem).wait()


def _dispatch(u2_slabs, dest1, dest2, n_slot):
    n = dest1.shape[0]
    t = DISPATCH_ROWS
    nt = n // t
    idx = pl.BlockSpec((1, 1, t), lambda i: (i, 0, 0), memory_space=pltpu.SMEM)
    any_spec = pl.BlockSpec(memory_space=pl.ANY)
    x_init = jnp.zeros((n_slot * TOKEN_SLAB, LANES), jnp.uint32)
    return pl.pallas_call(
        _dispatch_kernel,
        grid=(nt,),
        in_specs=[idx, idx, pl.BlockSpec((t * TOKEN_SLAB, LANES), lambda i: (i, 0)), any_spec],
        out_specs=any_spec,
        out_shape=jax.ShapeDtypeStruct(x_init.shape, x_init.dtype),
        scratch_shapes=[pltpu.SemaphoreType.DMA(())],
        input_output_aliases={3: 0},
        compiler_params=_cparams(("arbitrary",)),
        name="dispatch",
    )(dest1.reshape(nt, 1, t), dest2.reshape(nt, 1, t), u2_slabs, x_init)


def _expert_kernel(blk_expert_ref, n_used_ref, x_ref, wg_ref, wu_ref, wd_ref, y_ref, wg_s, wu_s, wd_s):
    i = pl.program_id(0)
    n_used = n_used_ref[0]

    @pl.when(i >= n_used)
    def _():
        y_ref[...] = jnp.zeros_like(y_ref)

    @pl.when(i < n_used)
    def _():
        @pl.when((i == 0) | (blk_expert_ref[i] != blk_expert_ref[jnp.maximum(i - 1, 0)]))
        def _():
            wg_s[...] = wg_ref[0].astype(BF16)
            wu_s[...] = wu_ref[0].astype(BF16)
            wd_s[...] = wd_ref[0].astype(BF16)

        x = _unpack_bf16_pairs(_load_token_slabs(x_ref, EXPERT_ROWS)).astype(BF16)
        hg = jnp.dot(x, wg_s[...], preferred_element_type=F32)
        hu = jnp.dot(x, wu_s[...], preferred_element_type=F32)
        act = (hg * _sigmoid(hg) * hu).astype(BF16)
        _store_token_slabs(y_ref, _pack_bf16_pairs(jnp.dot(act, wd_s[...], preferred_element_type=F32)))


def _expert_mlp(x_slabs, block_expert, n_used, w_gate, w_up, w_down):
    n_blk = block_expert.shape[0]
    d = w_gate.shape[1]
    hdim = w_gate.shape[2]
    rows = EXPERT_ROWS
    used = lambda i, nu: jnp.minimum(i, nu[0] - 1)
    weight = lambda i, be, nu: (be[used(i, nu)], 0, 0)
    grid_spec = pltpu.PrefetchScalarGridSpec(
        num_scalar_prefetch=2,
        grid=(n_blk,),
        in_specs=[pl.BlockSpec((rows * TOKEN_SLAB, LANES), lambda i, be, nu: (used(i, nu), 0)),
                  pl.BlockSpec((1, d, hdim), weight),
                  pl.BlockSpec((1, d, hdim), weight),
                  pl.BlockSpec((1, hdim, d), weight)],
        out_specs=pl.BlockSpec((rows * TOKEN_SLAB, LANES), lambda i, be, nu: (i, 0)),
        scratch_shapes=[pltpu.VMEM((d, hdim), BF16), pltpu.VMEM((d, hdim), BF16), pltpu.VMEM((hdim, d), BF16)],
    )
    return pl.pallas_call(
        _expert_kernel,
        grid_spec=grid_spec,
        out_shape=jax.ShapeDtypeStruct((n_blk * rows * TOKEN_SLAB, LANES), jnp.uint32),
        compiler_params=_cparams(("arbitrary",)),
        name="expert_mlp",
    )(block_expert, n_used, x_slabs, w_gate, w_up, w_down)


def _combine_kernel(d1_ref, d2_ref, d1n_ref, d2n_ref, y_hbm, route_ref, h1_ref, g2_ref, lng_ref, lnb_ref,
                    o_ref, abuf, bbuf, sem_a, sem_b):
    i = pl.program_id(0)
    n = pl.num_programs(0)
    slot = i % 2
    rows = o_ref.shape[0]

    @pl.when(i == 0)
    def _():
        _start_tile_gather(y_hbm, d1_ref, abuf, sem_a, 0, rows, GATHER_PRIORITIES)
        _start_tile_gather(y_hbm, d2_ref, bbuf, sem_b, 0, rows, GATHER_PRIORITIES)

    @pl.when(i + 1 < n)
    def _():
        _start_tile_gather(y_hbm, d1n_ref, abuf, sem_a, 1 - slot, rows, GATHER_PRIORITIES)
        _start_tile_gather(y_hbm, d2n_ref, bbuf, sem_b, 1 - slot, rows, GATHER_PRIORITIES)

    _wait_tile_gather(y_hbm, d1_ref, abuf, sem_a, slot, rows)
    _wait_tile_gather(y_hbm, d2_ref, bbuf, sem_b, slot, rows)
    route = route_ref[...]
    f = (_unpack_bf16_pairs(_load_token_slabs(abuf.at[slot], rows)) * route[:, ROUTE_G1:ROUTE_G1 + 1]
         + _unpack_bf16_pairs(_load_token_slabs(bbuf.at[slot], rows)) * route[:, ROUTE_G2:ROUTE_G2 + 1])
    o_ref[...] = _layer_norm(DEEPNORM_ALPHA * h1_ref[...] + g2_ref[0] * f, lng_ref[...], lnb_ref[...])


def _combine(y_tiles, dest1, dest2, route, h1, g2, ln_g, ln_b, tokens_per_batch):
    n, d = h1.shape
    t = COMBINE_ROWS
    nt = n // t
    per_b = tokens_per_batch // t
    d1 = dest1.reshape(nt, 1, t)
    d2 = dest2.reshape(nt, 1, t)
    cur = pl.BlockSpec((1, 1, t), lambda i: (i, 0, 0), memory_space=pltpu.SMEM)
    nxt = pl.BlockSpec((1, 1, t), lambda i: (jnp.minimum(i + 1, nt - 1), 0, 0), memory_space=pltpu.SMEM)
    small = lambda arr: pl.BlockSpec(arr.shape, lambda i: (0,) * arr.ndim)
    return pl.pallas_call(
        _combine_kernel,
        grid=(nt,),
        in_specs=[cur, cur, nxt, nxt, pl.BlockSpec(memory_space=pl.ANY),
                  pl.BlockSpec((t, LANES), lambda i: (i, 0)),
                  pl.BlockSpec((t, d), lambda i: (i, 0)),
                  pl.BlockSpec((1, 1, d), lambda i: (i // per_b, 0, 0)),
                  small(ln_g), small(ln_b)],
        out_specs=pl.BlockSpec((t, d), lambda i: (i, 0)),
        out_shape=jax.ShapeDtypeStruct((n, d), F32),
        scratch_shapes=[pltpu.VMEM((2, t * TOKEN_SLAB, LANES), jnp.uint32),
                        pltpu.VMEM((2, t * TOKEN_SLAB, LANES), jnp.uint32),
                        pltpu.SemaphoreType.DMA((2,)), pltpu.SemaphoreType.DMA((2,))],
        compiler_params=_cparams(("arbitrary",)),
        name="combine",
    )(d1, d2, d1, d2, y_tiles, route, h1, g2, ln_g, ln_b)


def _hi_lo(w):
    hi = w.astype(BF16)
    return jnp.stack([hi, (w - hi.astype(F32)).astype(BF16)])


def _block_diag2(w):
    z = jnp.zeros_like(w[0])
    return jnp.concatenate([jnp.concatenate([w[0], z], axis=1), jnp.concatenate([z, w[1]], axis=1)], axis=0)


def kernel(x, c, ctx, c_ctx, emb_ln_g, emb_ln_b, w_mod, b_mod, w_in, tshift_mu, rwkv_w0, rwkv_w2, rwkv_a0, rwkv_a2, rwkv_g2, rwkv_k_k, rwkv_k_a, rwkv_r_k, rwkv_gn_g, rwkv_gn_b, ret_decay, ret_gn_g, ret_gn_b, w_out, ln1_g, ln1_b, router_group, router_group_bias, router_expert, router_expert_bias, expert_w_gate, expert_w_up, expert_w_down, ln2_g, ln2_b):
    assert w_mod.shape[0] == 1, "written for DEPTH == 1 (context outputs are never emitted)"
    b, n_tok, d = x.shape
    n_ctx = ctx.shape[1]
    row = lambda v: v.reshape(1, -1)

    c_rows = jnp.zeros((SUBLANES, d), F32).at[:b].set(c).at[b].set(c_ctx)
    mod = _modulation(c_rows, w_mod[0], row(b_mod[0]))
    sh1, s1, g1, sh2, s2, g2 = [mod[:b, j * d:(j + 1) * d].reshape(b, 1, d) for j in range(6)]
    sh1c, s1c = [jnp.broadcast_to(mod[b, j * d:(j + 1) * d].reshape(1, 1, d), (b, 1, d)) for j in range(2)]

    w_in_bf16 = w_in[0].astype(BF16)
    pr, pt = _in_proj(x, row(emb_ln_g), row(emb_ln_b), s1, sh1, w_in_bf16)
    pr_c, pt_c = _in_proj(ctx, row(emb_ln_g), row(emb_ln_b), s1c, sh1c, w_in_bf16)

    prep_params = (row(tshift_mu[0]), row(rwkv_w0[0]), _hi_lo(_block_diag2(rwkv_w2[0])), row(rwkv_a0[0]),
                   _hi_lo(_block_diag2(rwkv_a2[0])), _hi_lo(rwkv_g2[0]), row(rwkv_k_k[0]), row(rwkv_k_a[0]),
                   row(rwkv_r_k[0]),
                   _segment_ones(D_RWKV, RWKV_HEAD))
    lat = _rwkv_prepare(pr, prep_params, grid_shift=True)
    cx = _rwkv_prepare(pr_c, prep_params, grid_shift=False)
    r_l, v_l, kk_l, w_l, kd_l, bb_l, gate_l, bonus_l = lat
    r_c, v_c, kk_c, w_c, kd_c, bb_c, _, _ = cx

    y_f, y_b = _wkv7((r_l, v_l, kk_l, w_l, kd_l, bb_l), (r_c, v_c, kk_c, w_c, kd_c, bb_c), b, n_tok, n_ctx)

    cos_t, sin_t = _rope_tables(n_tok)
    t_f, t_b = _retention(pt, pt_c, ret_decay[0], cos_t, sin_t)

    wr = jnp.zeros((d, LANES), F32).at[:, :N_GROUPS].set(router_group[0])
    wr = wr.at[:, N_GROUPS:N_GROUPS + N_EXPERTS].set(router_expert[0])
    br = jnp.zeros((1, LANES), F32).at[0, :N_GROUPS].set(router_group_bias[0])
    br = br.at[0, N_GROUPS:N_GROUPS + N_EXPERTS].set(router_expert_bias[0].reshape(-1))
    vecs = (row(emb_ln_g), row(emb_ln_b), g1, s2, sh2, row(rwkv_gn_g[0]), row(rwkv_gn_b[0]),
            row(ret_gn_g[0]), row(ret_gn_b[0]), row(ln1_g[0]), row(ln1_b[0]), br)
    wr_hi_lo = _hi_lo(wr)
    mats = (_segment_ones(D_RWKV, RWKV_HEAD), w_out[0].astype(BF16), wr_hi_lo[0], wr_hi_lo[1])
    h1, u2, route, route_t, counts = _out_proj(x, y_f, y_b, bonus_l, gate_l, t_f, t_b, pt, vecs, mats)

    n_all = b * n_tok

    e1 = route_t[ROUTE_E1].astype(jnp.int32)
    e2 = route_t[ROUTE_E2].astype(jnp.int32)
    cnt = counts[0, :N_EXPERTS].astype(jnp.int32)
    padded = ((cnt + EXPERT_ROWS - 1) // EXPERT_ROWS) * EXPERT_ROWS
    pends = jnp.cumsum(padded)
    pstarts = pends - padded
    expert_ids = jnp.arange(N_EXPERTS, dtype=jnp.int32)
    start_of = lambda e: jnp.sum(jnp.where(e[:, None] == expert_ids[None, :], pstarts[None, :], 0), axis=1)
    dest1 = start_of(e1) + route_t[ROUTE_RANK1].astype(jnp.int32)
    dest2 = start_of(e2) + route_t[ROUTE_RANK2].astype(jnp.int32)
    n_blk = -(-(n_all * 2) // EXPERT_ROWS) + N_EXPERTS
    block_start = jnp.arange(n_blk, dtype=jnp.int32) * EXPERT_ROWS
    block_expert = jnp.minimum(jnp.sum((block_start[:, None] >= pends[None, :]).astype(jnp.int32), axis=1),
                               N_EXPERTS - 1)

    n_used = (pends[N_EXPERTS - 1:] // EXPERT_ROWS).astype(jnp.int32)
    x_slabs = _dispatch(u2, dest1, dest2, n_blk * EXPERT_ROWS)
    y_tiles = _expert_mlp(x_slabs, block_expert, n_used, expert_w_gate[0], expert_w_up[0], expert_w_down[0])
    out = _combine(y_tiles, dest1, dest2, route, h1.reshape(n_all, d), g2, row(ln2_g[0]), row(ln2_b[0]), n_tok)
    return out.reshape(b, n_tok, d)
```

```python
import functools
import math

import jax
import jax.numpy as jnp
import numpy as np
from jax import lax
from jax.experimental import pallas as pl
from jax.experimental.pallas import tpu as pltpu

F32 = jnp.float32
BF16 = jnp.bfloat16
HIGHEST = lax.Precision.HIGHEST

GRID_W = 64
D_RWKV = 512
RWKV_HEAD = 64
RWKV_HEADS = D_RWKV // RWKV_HEAD
DECAY_LORA = 64
AAA_LORA = 64
GATE_LORA = 128
D_RET = 512
RET_HEADS = 4
RET_HEAD = D_RET // RET_HEADS
RET_CHUNK = 128
RET_CHUNKS_PER_STEP = 4
RWKV_COLS = 3 * D_RWKV + 2 * (DECAY_LORA + AAA_LORA) + GATE_LORA
RET_COLS = 4 * D_RET
N_GROUPS = 4
EXPERTS_PER_GROUP = 8
N_EXPERTS = N_GROUPS * EXPERTS_PER_GROUP
EXPERT_HIDDEN = 512
MOE_BLOCK = 128
ROPE_BASE = 10000.0
LN_EPS = 1e-5
RWKV_GN_EPS = 64e-5
RET_GN_EPS = 1e-5
DEEPNORM_ALPHA = 2.0 ** 0.25
EXP_NEG_HALF = math.exp(-0.5)

LANES = 128
SUBLANES = 8
VMEM_LIMIT_BYTES = 56 * 1024 * 1024

WKV_CHUNK = 64


IN_PROJ_ROWS = 512
OUT_PROJ_ROWS = 512

TOKEN_SLAB = 4
EXPERT_ROWS = 512
COMBINE_ROWS = 512
DISPATCH_ROWS = 512
GATHER_UNROLL = 16
GATHER_PRIORITIES = (0, 1)


def _pack_bf16_pairs(x):
    half = x.shape[1] // 2

    def bf16_bits(v):
        b = lax.bitcast_convert_type(v, jnp.uint32)
        return (b + jnp.uint32(0x7FFF) + ((b >> 16) & jnp.uint32(1))) >> 16

    return bf16_bits(x[:, :half]) | (bf16_bits(x[:, half:]) << 16)


def _unpack_bf16_pairs(p):
    lo = lax.bitcast_convert_type(p << 16, F32)
    hi = lax.bitcast_convert_type(p & jnp.uint32(0xFFFF0000), F32)
    return jnp.concatenate([lo, hi], axis=-1)


def _store_token_slabs(ref, x):
    rows = x.shape[0]
    for j in range(TOKEN_SLAB):
        ref[pl.ds(j, rows, stride=TOKEN_SLAB), :] = x[:, j * LANES:(j + 1) * LANES]


def _load_token_slabs(ref, rows):
    return jnp.concatenate([ref[pl.ds(j, rows, stride=TOKEN_SLAB), :] for j in range(TOKEN_SLAB)], axis=-1)


def _cparams(sem):
    return pltpu.CompilerParams(dimension_semantics=sem, vmem_limit_bytes=VMEM_LIMIT_BYTES)


def _layer_norm(x, g, b, eps=LN_EPS):
    mu = jnp.mean(x, axis=-1, keepdims=True)
    xc = x - mu
    var = jnp.mean(xc * xc, axis=-1, keepdims=True)
    return xc * lax.rsqrt(var + eps) * g + b


def _sigmoid(x):
    return 1.0 / (1.0 + jnp.exp(-x))


def _split_bf16(x):
    hi = x.astype(BF16)
    return hi, (x - hi.astype(F32)).astype(BF16)


def _segsum(x, ones_bf16):
    t = x.shape[0]
    s = jnp.dot(jnp.concatenate(_split_bf16(x), axis=0), ones_bf16, preferred_element_type=F32)
    return s[:t] + s[t:]


def _dot_split(x, w_hi, w_lo):
    hi, lo = _split_bf16(x)
    acc = jnp.dot(hi, w_hi, preferred_element_type=F32)
    acc = acc + jnp.dot(lo, w_hi, preferred_element_type=F32)
    return acc + jnp.dot(hi, w_lo, preferred_element_type=F32)


def _segment_ones(width, seg):
    idx = np.arange(width) // seg
    return jnp.asarray(idx[:, None] == idx[None, :], dtype=BF16)


def _mod_kernel(c_ref, w_ref, b_ref, o_ref):
    c = c_ref[...]
    sc = c * _sigmoid(c)
    o_ref[...] = jnp.dot(sc, w_ref[...], precision=HIGHEST, preferred_element_type=F32) + b_ref[...]


def _modulation(c_rows, w_mod, b_mod):
    rows, d = c_rows.shape
    n = w_mod.shape[1]
    tn = 1536
    return pl.pallas_call(
        _mod_kernel,
        grid=(n // tn,),
        in_specs=[pl.BlockSpec((rows, d), lambda j: (0, 0)),
                  pl.BlockSpec((d, tn), lambda j: (0, j)),
                  pl.BlockSpec((1, tn), lambda j: (0, j))],
        out_specs=pl.BlockSpec((rows, tn), lambda j: (0, j)),
        out_shape=jax.ShapeDtypeStruct((rows, n), F32),
        compiler_params=_cparams(("arbitrary",)),
        name="modulation",
    )(c_rows, w_mod, b_mod)


def _in_proj_kernel(x_ref, g_ref, b_ref, s_ref, sh_ref, w_ref, pr_ref, pt_ref):
    h = _layer_norm(x_ref[0], g_ref[...], b_ref[...])
    u = h * (1.0 + s_ref[0]) + sh_ref[0]
    p = jnp.dot(u.astype(BF16), w_ref[...], preferred_element_type=F32)
    pr_ref[0] = p[:, :RWKV_COLS]
    pt_ref[0] = p[:, RWKV_COLS:]


def _in_proj(x, ln_g, ln_b, s1, sh1, w_in_bf16):
    b, n, d = x.shape
    tm = min(IN_PROJ_ROWS, n)
    cols = w_in_bf16.shape[1]
    return pl.pallas_call(
        _in_proj_kernel,
        grid=(b, n // tm),
        in_specs=[pl.BlockSpec((1, tm, d), lambda bi, i: (bi, i, 0)),
                  pl.BlockSpec((1, d), lambda bi, i: (0, 0)),
                  pl.BlockSpec((1, d), lambda bi, i: (0, 0)),
                  pl.BlockSpec((1, 1, d), lambda bi, i: (bi, 0, 0)),
                  pl.BlockSpec((1, 1, d), lambda bi, i: (bi, 0, 0)),
                  pl.BlockSpec((d, cols), lambda bi, i: (0, 0))],
        out_specs=[pl.BlockSpec((1, tm, RWKV_COLS), lambda bi, i: (bi, i, 0)),
                   pl.BlockSpec((1, tm, RET_COLS), lambda bi, i: (bi, i, 0))],
        out_shape=[jax.ShapeDtypeStruct((b, n, RWKV_COLS), F32),
                   jax.ShapeDtypeStruct((b, n, RET_COLS), F32)],
        compiler_params=_cparams(("arbitrary", "arbitrary")),
        name="in_proj",
    )(x, ln_g, ln_b, s1, sh1, w_in_bf16)


def _rwkv_prepare_kernel(cur_ref, prev_ref, next_ref, mu_ref, w0_ref, w2_ref, a0_ref, a2_ref, g2_ref,
                         kk_scale_ref, ka_ref, rk_ref, ones_ref,
                         r_ref, v_ref, kk_ref, w_ref, kd_ref, bb_ref, g_ref, bonus_ref,
                         *, grid_shift, n_tok):
    cur = cur_ref[0]
    t, c = cur.shape
    row = lax.broadcasted_iota(jnp.int32, (t, c), 0)
    lane = lax.broadcasted_iota(jnp.int32, (t, c), 1)
    prev_tok = pltpu.roll(cur, 1, 0)
    next_tok = pltpu.roll(cur, t - 1, 0)
    if grid_shift:
        col = row & (GRID_W - 1)
        tok = row + pl.program_id(1) * t
        left = jnp.where(col > 0, prev_tok, 0.0)
        right = jnp.where(col < GRID_W - 1, next_tok, 0.0)
        up = jnp.where(tok >= GRID_W, jnp.concatenate([prev_ref[0], cur[:t - GRID_W]], axis=0), 0.0)
        down = jnp.where(tok < n_tok - GRID_W, jnp.concatenate([cur[GRID_W:], next_ref[0]], axis=0), 0.0)
        cm = lane & 3
        shifted = jnp.where(cm == 0, left, jnp.where(cm == 1, right, jnp.where(cm == 2, up, down)))
    else:
        prev_tok = jnp.where(row > 0, prev_tok, 0.0)
        next_tok = jnp.where(row < t - 1, next_tok, 0.0)
        shifted = jnp.where((lane & 1) == 0, prev_tok, next_tok)
    pm = cur + mu_ref[...] * (shifted - cur)

    r = pm[:, 0:D_RWKV]
    k = pm[:, D_RWKV:2 * D_RWKV]
    v = pm[:, 2 * D_RWKV:3 * D_RWKV]
    o = 3 * D_RWKV
    lw = pm[:, o:o + 2 * DECAY_LORA]
    la = pm[:, o + 2 * DECAY_LORA:o + 2 * (DECAY_LORA + AAA_LORA)]
    lg = pm[:, o + 2 * (DECAY_LORA + AAA_LORA):]

    w = w0_ref[...] + _dot_split(jnp.tanh(lw), w2_ref[0], w2_ref[1])
    log_decay = -EXP_NEG_HALF * _sigmoid(w)
    a = _sigmoid(a0_ref[...] + _dot_split(la, a2_ref[0], a2_ref[1]))
    gate = _dot_split(_sigmoid(lg), g2_ref[0], g2_ref[1])

    ones = ones_ref[...]
    kk_raw = k * kk_scale_ref[...]
    kk = kk_raw / jnp.maximum(jnp.sqrt(_segsum(kk_raw * kk_raw, ones)), 1e-12)
    ka = ka_ref[...]
    a0 = a[:, :D_RWKV]
    a1 = a[:, D_RWKV:]
    kd0 = k * (1.0 + (a0 - 1.0) * ka)
    kd1 = k * (1.0 + (a1 - 1.0) * ka)
    bonus = _segsum(r * (kd0 + kd1) * rk_ref[...], ones) * v

    r_ref[0] = r
    v_ref[0] = v
    kk_ref[0] = kk
    w_ref[0] = log_decay
    kd_ref[0] = jnp.concatenate([kd0, kd1], axis=-1)
    bb_ref[0] = jnp.concatenate([kk * a0, kk * a1], axis=-1)
    g_ref[0] = gate
    bonus_ref[0] = bonus


def _rwkv_prepare(pr, params, grid_shift):
    b, n, c = pr.shape
    t = 256
    if not grid_shift:
        assert n == t, "sequence token shift is written for a single tile"
    halo_blocks = n // GRID_W
    per_tile = t // GRID_W
    small = lambda shape: pl.BlockSpec(shape, lambda bi, i: (0,) * len(shape))
    tok_spec = lambda width: pl.BlockSpec((1, t, width), lambda bi, i: (bi, i, 0))
    out_widths = (D_RWKV, D_RWKV, D_RWKV, 2 * D_RWKV, 2 * D_RWKV, 2 * D_RWKV, D_RWKV, D_RWKV)
    kernel = functools.partial(_rwkv_prepare_kernel, grid_shift=grid_shift, n_tok=n)
    return pl.pallas_call(
        kernel,
        grid=(b, n // t),
        in_specs=[tok_spec(c),
                  pl.BlockSpec((1, GRID_W, c), lambda bi, i: (bi, jnp.maximum(i * per_tile - 1, 0), 0)),
                  pl.BlockSpec((1, GRID_W, c),
                               lambda bi, i: (bi, jnp.minimum((i + 1) * per_tile, halo_blocks - 1), 0)),
                  small((1, c)), small((1, 2 * D_RWKV)), small((2, 2 * DECAY_LORA, 2 * D_RWKV)),
                  small((1, 2 * D_RWKV)), small((2, 2 * AAA_LORA, 2 * D_RWKV)), small((2, GATE_LORA, D_RWKV)),
                  small((1, D_RWKV)), small((1, D_RWKV)), small((1, D_RWKV)), small((D_RWKV, D_RWKV))],
        out_specs=[tok_spec(wd) for wd in out_widths],
        out_shape=[jax.ShapeDtypeStruct((b, n, wd), F32) for wd in out_widths],
        compiler_params=_cparams(("arbitrary", "arbitrary")),
        name="rwkv_prepare",
    )(pr, pr, pr, *params)


def _bdot(a, b):
    return jnp.dot(a.astype(BF16), b.astype(BF16), preferred_element_type=F32)


def _bdot_nt(a, b):
    return lax.dot_general(a.astype(BF16), b.astype(BF16), (((1,), (1,)), ((), ())), preferred_element_type=F32)


def _bdot_tn(a, b):
    return lax.dot_general(a.astype(BF16), b.astype(BF16), (((0,), (0,)), ((), ())), preferred_element_type=F32)


def _wkv7_chunk_kernel(*refs, n_ctx_chunks):
    c = WKV_CHUNK
    p = 2 * c
    n_in = 12
    in_refs = (refs[:n_in], refs[n_in:2 * n_in])
    y_refs = refs[2 * n_in:2 * n_in + 2]
    state_ref = refs[2 * n_in + 2]
    n = pl.program_id(0)
    n_batch = in_refs[0][0].shape[0]
    pairs_per_batch = RWKV_HEADS // 2
    pairs_per_dir = n_batch * pairs_per_batch

    @pl.when(n == 0)
    def _():
        state_ref[...] = jnp.zeros_like(state_ref)

    is_ctx = n < n_ctx_chunks
    ti = lax.broadcasted_iota(jnp.int32, (c, c), 0)
    tj = lax.broadcasted_iota(jnp.int32, (c, c), 1)
    ri = lax.broadcasted_iota(jnp.int32, (p, p), 0)
    ci = lax.broadcasted_iota(jnp.int32, (p, p), 1)
    same_head = (ri >= c) == (ci >= c)
    ii = ri & (c - 1)
    jj = ci & (c - 1)
    eye = (ri == ci).astype(F32)
    first = lax.broadcasted_iota(jnp.int32, (c, p), 1) < RWKV_HEAD

    def stack(x):
        return jnp.concatenate([jnp.where(first, x, 0.0), jnp.where(first, 0.0, x)], axis=0)

    def unstack(x):
        return x[:c] + x[c:]

    a_st, r_st, k_st, b_st, k2_st, b2_st, v_st, g_chunk, earlier, upto_self = ([] for _ in range(10))
    for d in range(2):
        r_l, v_l, kk_l, lw_l, kd_l, bb_l, r_c, v_c, kk_c, lw_c, kd_c, bb_c = in_refs[d]
        pick = lambda xc, xl: jnp.concatenate(
            [jnp.where(is_ctx, xc[bi], xl[bi]) for bi in range(n_batch)], axis=-1)
        r, v, kk, lw, kd, bb = (pick(r_c, r_l), pick(v_c, v_l), pick(kk_c, kk_l), pick(lw_c, lw_l),
                                pick(kd_c, kd_l), pick(bb_c, bb_l))
        before = (tj < ti) if d == 0 else (tj > ti)
        upto = (before | (ti == tj)).astype(BF16)
        hi = lw.astype(BF16)
        r1 = lw - hi.astype(F32)
        mid = r1.astype(BF16)
        lo = (r1 - mid.astype(F32)).astype(BF16)
        cum = (jnp.dot(upto, hi, preferred_element_type=F32) + jnp.dot(upto, mid, preferred_element_type=F32)
               + jnp.dot(upto, lo, preferred_element_type=F32))
        tot = jnp.sum(lw, axis=0, keepdims=True)
        e_neg = jnp.exp(-cum)
        e_rem = jnp.exp(tot - cum)
        alpha = kk * jnp.exp(cum - lw)
        rho = r * jnp.exp(cum)
        beta = bb * e_neg
        kappa = kd * e_neg
        kappa_rem = kd * e_rem
        beta_rem = bb * e_rem
        g_all = jnp.exp(tot)
        pair_before = same_head & ((jj < ii) if d == 0 else (jj > ii))
        pair_upto = pair_before | (ri == ci)
        for hp in range(pairs_per_dir):
            sl = slice(hp * p, (hp + 1) * p)
            a_st.append(stack(alpha[:, sl]))
            r_st.append(stack(rho[:, sl]))
            k_st.append(stack(kappa[:, sl]))
            b_st.append(stack(beta[:, sl]))
            k2_st.append(stack(kappa_rem[:, sl]))
            b2_st.append(stack(beta_rem[:, sl]))
            v_st.append(stack(v[:, sl]))
            g_chunk.append(g_all[:, sl])
            earlier.append(pair_before)
            upto_self.append(pair_upto)

    pairs = range(2 * pairs_per_dir)
    g = [_bdot_nt(jnp.concatenate([a_st[h], r_st[h]], axis=0), jnp.concatenate([k_st[h], b_st[h]], axis=0))
         for h in pairs]
    m1 = [jnp.where(earlier[h], g[h][:p, :p], 0.0) for h in pairs]
    m2 = [jnp.where(earlier[h], g[h][:p, p:], 0.0) for h in pairs]
    n1 = [jnp.where(upto_self[h], g[h][p:, :p], 0.0) for h in pairs]
    n2 = [jnp.where(upto_self[h], g[h][p:, p:], 0.0) for h in pairs]

    in_block = (ii >> 3) == (jj >> 3)
    pw = [-jnp.where(in_block, m2[h], 0.0) for h in pairs]
    inv = [eye + pw[h] for h in pairs]
    pw = [_bdot(pw[h], pw[h]) for h in pairs]
    both = [_bdot(jnp.concatenate([inv[h], pw[h]], axis=0), pw[h]) for h in pairs]
    inv = [inv[h] + both[h][:p] for h in pairs]
    inv = [inv[h] + _bdot(inv[h], both[h][p:]) for h in pairs]
    for sh in (3, 4, 5):
        off = ((ii >> (sh + 1)) == (jj >> (sh + 1))) & ((ii >> sh) != (jj >> sh))
        left = [_bdot(inv[h], jnp.where(off, m2[h], 0.0)) for h in pairs]
        inv = [inv[h] - _bdot(left[h], inv[h]) for h in pairs]

    mnv = [_bdot(jnp.concatenate([m1[h], n1[h]], axis=0), v_st[h]) for h in pairs]
    m1v = [mnv[h][:p] for h in pairs]
    n1v = [mnv[h][p:] for h in pairs]
    au = [_bdot(inv[h], jnp.concatenate([a_st[h], m1v[h]], axis=1)) for h in pairs]
    nn = [_bdot(n2[h], au[h]) for h in pairs]
    pc = [_bdot_tn(b2_st[h], au[h][:, :p]) for h in pairs]
    qc_t = [_bdot_tn(jnp.concatenate([v_st[h], -au[h][:, p:]], axis=0),
                     jnp.concatenate([k2_st[h], b2_st[h]], axis=0)) for h in pairs]
    s0 = [state_ref[h] for h in pairs]
    y = [_bdot_nt(unstack(r_st[h] - nn[h][:, :p]), s0[h]) + unstack(n1v[h] - nn[h][:, p:]) for h in pairs]
    s_dec = [_bdot_nt(s0[h], pc[h]) for h in pairs]
    for h in pairs:
        d, hp = divmod(h, pairs_per_dir)
        bi, hpb = divmod(hp, pairs_per_batch)
        y_refs[d][bi, :, hpb * p:(hpb + 1) * p] = y[h]
        state_ref[h] = s0[h] * g_chunk[h] - s_dec[h] + qc_t[h]


def _wkv7(lat, ctx, b, n_tok, n_ctx):
    c = WKV_CHUNK
    ncx = n_ctx // c
    nl = n_tok // c
    lat_idx = (lambda n: jnp.maximum(n - ncx, 0), lambda n: nl - 1 - jnp.maximum(n - ncx, 0))
    ctx_idx = (lambda n: jnp.minimum(n, ncx - 1), lambda n: ncx - 1 - jnp.minimum(n, ncx - 1))

    def specs(idx, d):
        shared = pl.BlockSpec((b, c, D_RWKV), lambda n: (0, idx(n), 0))
        per_dir = pl.BlockSpec((b, c, D_RWKV), lambda n: (0, idx(n), d))
        return [shared, shared, shared, per_dir, per_dir, per_dir]

    in_specs, args = [], []
    for d in range(2):
        in_specs += specs(lat_idx[d], d) + specs(ctx_idx[d], d)
        args += list(lat) + list(ctx)
    return pl.pallas_call(
        functools.partial(_wkv7_chunk_kernel, n_ctx_chunks=ncx),
        grid=(ncx + nl,),
        in_specs=in_specs,
        out_specs=[pl.BlockSpec((b, c, D_RWKV), lambda n, d=d: (0, lat_idx[d](n), 0)) for d in range(2)],
        out_shape=[jax.ShapeDtypeStruct((b, n_tok, D_RWKV), F32)] * 2,
        scratch_shapes=[pltpu.VMEM((2 * b * RWKV_HEADS // 2, 2 * RWKV_HEAD, 2 * RWKV_HEAD), F32)],
        compiler_params=_cparams(("arbitrary",)),
        name="wkv7_chunk",
    )(*args)


def _rope(z, cos_t, sin_t):
    lane = lax.broadcasted_iota(jnp.int32, z.shape, 1)
    half = RET_HEAD // 4
    partner = jnp.where((lane & (2 * half - 1)) < half, pltpu.roll(z, RET_HEAD - half, 1), pltpu.roll(z, half, 1))
    return z * cos_t + partner * sin_t


def _retention_kernel(dec_ref, fwd_ref, bwd_ref, ctx_ref, cosf_ref, sinf_ref, cosb_ref, sinb_ref,
                      yf_ref, yb_ref, state_ref, dmat_ref, tail_ref, head_ref, cdec_ref):
    c = RET_CHUNK
    scale = RET_HEAD ** -0.5
    ii = lax.broadcasted_iota(jnp.int32, (c, c), 0)
    jj = lax.broadcasted_iota(jnp.int32, (c, c), 1)
    pos = lax.broadcasted_iota(jnp.int32, (c, RET_HEAD), 0).astype(F32)
    n_ctx_chunks = ctx_ref.shape[1] // c

    def head_slices(ref_val, h):
        q = ref_val[:, h * RET_HEAD:(h + 1) * RET_HEAD]
        k = ref_val[:, D_RET + h * RET_HEAD:D_RET + (h + 1) * RET_HEAD]
        v = ref_val[:, 2 * D_RET + h * RET_HEAD:2 * D_RET + (h + 1) * RET_HEAD]
        return q, k, v

    n_batch = fwd_ref.shape[0]
    heads = [(d, h) for d in range(2) for h in range(RET_HEADS)]
    chains = [(bi, d, h) for bi in range(n_batch) for d, h in heads]

    @pl.when(pl.program_id(0) == 0)
    def _():
        for d, h in heads:
            x = jnp.full((1, RET_HEAD), dec_ref[d, h], F32)
            lg = -(jnp.maximum(x, 0.0) + jnp.log(1.0 + jnp.exp(-jnp.abs(x))))
            chunk_decay = jnp.exp(lg * float(c))
            tail = jnp.exp(lg * ((c - 1.0 - pos) if d == 0 else pos))
            rel = (ii - jj) if d == 0 else (jj - ii)
            mask = (rel >= 0) if d == 0 else (rel > 0)
            dmat_ref[d, h] = jnp.where(mask, jnp.exp(lg * jnp.maximum(rel, 0).astype(F32)), 0.0)
            tail_ref[d, h] = tail
            head_ref[d, h] = jnp.exp(lg * ((pos + 1.0) if d == 0 else (c - pos)))
            cdec_ref[d, h] = jnp.broadcast_to(chunk_decay, (SUBLANES, RET_HEAD))
            order = range(n_ctx_chunks) if d == 0 else range(n_ctx_chunks - 1, -1, -1)
            for bi in range(n_batch):
                s = jnp.zeros((RET_HEAD, RET_HEAD), F32)
                for cc in order:
                    _, kc, vc = head_slices(ctx_ref[bi, cc * c:(cc + 1) * c, :], h)
                    s = s * chunk_decay + _bdot_tn(kc * scale * tail, vc)
                state_ref[bi, d, h] = s

    n_sub = RET_CHUNKS_PER_STEP

    def rows(d, sub):
        first = sub * c if d == 0 else (n_sub - 1 - sub) * c
        return slice(first, first + c)

    work = [(bi, d, h, sub) for sub in range(n_sub) for bi, d, h in chains]
    qkv = {}
    for bi, d, h, sub in work:
        blk = (fwd_ref if d == 0 else bwd_ref)[bi, rows(d, sub), :]
        cos_t = (cosf_ref if d == 0 else cosb_ref)[rows(d, sub), :]
        sin_t = (sinf_ref if d == 0 else sinb_ref)[rows(d, sub), :]
        q, k, v = head_slices(blk, h)
        qkv[bi, d, h, sub] = (_rope(q, cos_t, sin_t), _rope(k, cos_t, sin_t) * scale, v.astype(BF16))
    scores = {w: _bdot_nt(qkv[w][0], qkv[w][1]) for w in work}
    inner = {w: _bdot(scores[w] * dmat_ref[w[1], w[2]], qkv[w][2]) for w in work}
    upd = {w: _bdot_tn(qkv[w][1] * tail_ref[w[1], w[2]], qkv[w][2]) for w in work}
    state = {ch: state_ref[ch] for ch in chains}
    for sub in range(n_sub):
        cross = {ch: _bdot(qkv[ch + (sub,)][0] * head_ref[ch[1], ch[2]], state[ch]) for ch in chains}
        for bi, d, h in chains:
            ch = (bi, d, h)
            state[ch] = state[ch] * cdec_ref[d, h, 0:1, :] + upd[ch + (sub,)]
            out_ref = yf_ref if d == 0 else yb_ref
            out_ref[bi, rows(d, sub), h * RET_HEAD:(h + 1) * RET_HEAD] = inner[ch + (sub,)] + cross[ch]
    for ch in chains:
        state_ref[ch] = state[ch]


def _retention(pt, pt_ctx, ret_decay, cos_t, sin_t):
    b, n, _ = pt.shape
    c = RET_CHUNK
    rows = RET_CHUNKS_PER_STEP * c
    steps = n // rows
    qkv = 3 * D_RET
    fwd = lambda i: (0, i, 0)
    bwd = lambda i: (0, steps - 1 - i, 0)
    return pl.pallas_call(
        _retention_kernel,
        grid=(steps,),
        in_specs=[pl.BlockSpec(memory_space=pltpu.SMEM),
                  pl.BlockSpec((b, rows, qkv), fwd),
                  pl.BlockSpec((b, rows, qkv), bwd),
                  pl.BlockSpec((b, pt_ctx.shape[1], qkv), lambda i: (0, 0, 0)),
                  pl.BlockSpec((rows, RET_HEAD), lambda i: (i, 0)),
                  pl.BlockSpec((rows, RET_HEAD), lambda i: (i, 0)),
                  pl.BlockSpec((rows, RET_HEAD), lambda i: (steps - 1 - i, 0)),
                  pl.BlockSpec((rows, RET_HEAD), lambda i: (steps - 1 - i, 0))],
        out_specs=[pl.BlockSpec((b, rows, D_RET), fwd), pl.BlockSpec((b, rows, D_RET), bwd)],
        out_shape=[jax.ShapeDtypeStruct((b, n, D_RET), F32), jax.ShapeDtypeStruct((b, n, D_RET), F32)],
        scratch_shapes=[pltpu.VMEM((b, 2, RET_HEADS, RET_HEAD, RET_HEAD), F32),
                        pltpu.VMEM((2, RET_HEADS, c, c), F32),
                        pltpu.VMEM((2, RET_HEADS, c, RET_HEAD), F32),
                        pltpu.VMEM((2, RET_HEADS, c, RET_HEAD), F32),
                        pltpu.VMEM((2, RET_HEADS, SUBLANES, RET_HEAD), F32)],
        compiler_params=_cparams(("arbitrary",)),
        name="retention",
    )(ret_decay, pt, pt, pt_ctx, cos_t, sin_t, cos_t, sin_t)


def _rope_tables(n_tok):
    nf = RET_HEAD // 4
    lane = np.arange(RET_HEAD)
    inv = ROPE_BASE ** (-jnp.arange(nf, dtype=F32) / nf)
    t = jnp.arange(n_tok)
    pos = jnp.where((lane // (2 * nf) == 0)[None, :], (t // GRID_W)[:, None], (t % GRID_W)[:, None]).astype(F32)
    ang = pos * inv[lane % nf][None, :]
    sign = jnp.where((lane % (2 * nf)) < nf, -1.0, 1.0).astype(F32)
    return jnp.cos(ang), jnp.sin(ang) * sign[None, :]


def _group_norm(y, ones, seg, eps, g, b):
    mu = _segsum(y, ones) * (1.0 / seg)
    yc = y - mu
    var = _segsum(yc * yc, ones) * (1.0 / seg)
    return yc * lax.rsqrt(var + eps) * g + b


def _out_proj_kernel(x_ref, yf_ref, yb_ref, bonus_ref, gate_ref, tf_ref, tb_ref, gt_ref,
                     embg_ref, embb_ref, g1_ref, s2_ref, sh2_ref, rgn_g_ref, rgn_b_ref, tgn_g_ref, tgn_b_ref,
                     ones_r_ref, wout_ref, ln1g_ref, ln1b_ref, wrh_ref, wrl_ref, br_ref,
                     h1_ref, u2_ref, route_ref, route_t_ref, count_ref, carry_ref):
    @pl.when((pl.program_id(0) == 0) & (pl.program_id(1) == 0))
    def _():
        carry_ref[...] = jnp.zeros_like(carry_ref)

    y = yf_ref[0] + yb_ref[0]
    o_rwkv = _group_norm(y, ones_r_ref[...], RWKV_HEAD, RWKV_GN_EPS, rgn_g_ref[...], rgn_b_ref[...])
    o_rwkv = (o_rwkv + bonus_ref[0]) * gate_ref[0]
    yt = tf_ref[0] + tb_ref[0]
    gt = gt_ref[0]
    tgn_g = tgn_g_ref[...]
    tgn_b = tgn_b_ref[...]
    o_ret = jnp.concatenate(
        [_layer_norm(yt[:, h * RET_HEAD:(h + 1) * RET_HEAD], tgn_g[:, h * RET_HEAD:(h + 1) * RET_HEAD],
                     tgn_b[:, h * RET_HEAD:(h + 1) * RET_HEAD], RET_GN_EPS) for h in range(RET_HEADS)], axis=-1)
    o_ret = o_ret * (gt * _sigmoid(gt))
    cat = jnp.concatenate([o_rwkv, o_ret], axis=-1).astype(BF16)
    mix = jnp.dot(cat, wout_ref[...], preferred_element_type=F32)
    h = _layer_norm(x_ref[0], embg_ref[...], embb_ref[...])
    h1 = _layer_norm(DEEPNORM_ALPHA * h + g1_ref[0] * mix, ln1g_ref[...], ln1b_ref[...])
    u2 = h1 * (1.0 + s2_ref[0]) + sh2_ref[0]
    h1_ref[0] = h1
    _store_token_slabs(u2_ref, _pack_bf16_pairs(u2))
    route = _route_tile(_dot_split(u2, wrh_ref[...], wrl_ref[...]) + br_ref[...], carry_ref)
    route_ref[...] = route
    route_t_ref[...] = route.T[:SUBLANES]
    count_ref[...] = carry_ref[...]


def _out_proj(x, y_f, y_b, bonus, gate, t_f, t_b, pt, vecs, mats):
    b, n, d = x.shape
    t = OUT_PROJ_ROWS
    tok = lambda width: pl.BlockSpec((1, t, width), lambda bi, i: (bi, i, 0))
    per_b = pl.BlockSpec((1, 1, d), lambda bi, i: (bi, 0, 0))
    small = lambda arr: pl.BlockSpec(arr.shape, lambda bi, i: (0,) * arr.ndim)
    (embg, embb, g1, s2, sh2, rgn_g, rgn_b, tgn_g, tgn_b, ln1g, ln1b, br) = vecs
    (ones_r, wout, wr_hi, wr_lo) = mats
    gt_spec = pl.BlockSpec((1, t, D_RET), lambda bi, i: (bi, i, 3))
    tiles = n // t
    flat = lambda bi, i: (bi * tiles + i, 0)
    args = (x, y_f, y_b, bonus, gate, t_f, t_b, pt, embg, embb, g1, s2, sh2, rgn_g, rgn_b, tgn_g, tgn_b,
            ones_r, wout, ln1g, ln1b, wr_hi, wr_lo, br)
    in_specs = [tok(d)] + [tok(D_RWKV)] * 6 + [gt_spec, small(embg), small(embb), per_b, per_b, per_b,
                                                small(rgn_g), small(rgn_b), small(tgn_g), small(tgn_b),
                                                small(ones_r), small(wout), small(ln1g),
                                                small(ln1b), small(wr_hi), small(wr_lo), small(br)]
    return pl.pallas_call(
        _out_proj_kernel,
        grid=(b, n // t),
        in_specs=in_specs,
        out_specs=[tok(d), pl.BlockSpec((t * TOKEN_SLAB, LANES), flat),
                   pl.BlockSpec((t, LANES), flat),
                   pl.BlockSpec((SUBLANES, t), lambda bi, i: (0, bi * tiles + i)),
                   pl.BlockSpec((SUBLANES, LANES), lambda bi, i: (0, 0))],
        out_shape=[jax.ShapeDtypeStruct((b, n, d), F32),
                   jax.ShapeDtypeStruct((b * n * TOKEN_SLAB, LANES), jnp.uint32),
                   jax.ShapeDtypeStruct((b * n, LANES), F32),
                   jax.ShapeDtypeStruct((SUBLANES, b * n), F32),
                   jax.ShapeDtypeStruct((SUBLANES, LANES), F32)],
        scratch_shapes=[pltpu.VMEM((SUBLANES, LANES), F32)],
        compiler_params=_cparams(("arbitrary", "arbitrary")),
        name="out_proj",
    )(*args)


ROUTE_E1, ROUTE_E2, ROUTE_G1, ROUTE_G2, ROUTE_RANK1, ROUTE_RANK2 = range(6)


def _lane_argmax(x, valid, lane):
    m = jnp.max(jnp.where(valid, x, -jnp.inf), axis=-1, keepdims=True)
    idx = jnp.min(jnp.where(valid & (x == m), lane, float(LANES)), axis=-1, keepdims=True)
    return m, idx


def _route_tile(lg, carry_ref):
    t = lg.shape[0]
    lane = lax.broadcasted_iota(jnp.int32, lg.shape, 1).astype(F32)
    gmask = lane < N_GROUPS
    gmax = jnp.max(jnp.where(gmask, lg, -jnp.inf), axis=-1, keepdims=True)
    gexp = jnp.where(gmask, jnp.exp(lg - gmax), 0.0)
    gp = gexp / jnp.sum(gexp, axis=-1, keepdims=True)
    g_w, g_i = _lane_argmax(gp, gmask, lane)

    lo = N_GROUPS + EXPERTS_PER_GROUP * g_i
    emask = (lane >= lo) & (lane < lo + EXPERTS_PER_GROUP)
    emax = jnp.max(jnp.where(emask, lg, -jnp.inf), axis=-1, keepdims=True)
    eexp = jnp.where(emask, jnp.exp(lg - emax), 0.0)
    ep = eexp / jnp.sum(eexp, axis=-1, keepdims=True)
    p1, i1 = _lane_argmax(ep, emask, lane)
    p2, i2 = _lane_argmax(ep, emask & (lane != i1), lane)
    denom = p1 + p2
    gate1 = g_w * p1 / denom
    gate2 = g_w * p2 / denom
    e1 = i1 - N_GROUPS
    e2 = i2 - N_GROUPS

    oh1 = (lane == e1).astype(F32)
    oh2 = (lane == e2).astype(F32)
    cnt = oh1 + oh2
    ri = lax.broadcasted_iota(jnp.int32, (t, t), 0)
    ci = lax.broadcasted_iota(jnp.int32, (t, t), 1)
    before = (ci < ri).astype(BF16)
    seen = jnp.dot(before, cnt.astype(BF16), preferred_element_type=F32) + carry_ref[0:1, :]
    rank1 = jnp.sum(oh1 * seen, axis=-1, keepdims=True)
    rank2 = jnp.sum(oh2 * seen, axis=-1, keepdims=True)
    carry_ref[0:1, :] = carry_ref[0:1, :] + jnp.sum(cnt, axis=0, keepdims=True)

    out = jnp.zeros(lg.shape, F32)
    for slot, val in ((ROUTE_E1, e1.astype(F32)), (ROUTE_E2, e2.astype(F32)), (ROUTE_G1, gate1),
                      (ROUTE_G2, gate2), (ROUTE_RANK1, rank1), (ROUTE_RANK2, rank2)):
        out = jnp.where(lane == slot, val, out)
    return out


def _tile_gather_copy(src_hbm, idx_ref, buf, sem, slot, r):
    src = src_hbm.at[pl.ds(pl.multiple_of(idx_ref[0, 0, r] * TOKEN_SLAB, TOKEN_SLAB), TOKEN_SLAB), :]
    dst = buf.at[slot, pl.ds(pl.multiple_of(r * TOKEN_SLAB, TOKEN_SLAB), TOKEN_SLAB), :]
    return pltpu.make_async_copy(src, dst, sem.at[slot])


def _start_tile_gather(src_hbm, idx_ref, buf, sem, slot, rows, priorities):
    def body(g, carry):
        for j in range(GATHER_UNROLL):
            copy = _tile_gather_copy(src_hbm, idx_ref, buf, sem, slot, g * GATHER_UNROLL + j)
            copy.start(priority=priorities[j % len(priorities)])
        return carry
    lax.fori_loop(0, rows // GATHER_UNROLL, body, 0)


def _wait_tile_gather(src_hbm, idx_ref, buf, sem, slot, rows):
    del idx_ref
    whole = src_hbm.at[pl.ds(0, rows * TOKEN_SLAB), :]
    pltpu.make_async_copy(whole, buf.at[slot], sem.at[slot]).wait()


def _dispatch_kernel(d1_ref, d2_ref, u_ref, x_init_hbm, x_hbm, sem):
    del x_init_hbm
    rows = d1_ref.shape[2]

    def slab(ref, index):
        return ref.at[pl.ds(pl.multiple_of(index * TOKEN_SLAB, TOKEN_SLAB), TOKEN_SLAB), :]

    def body(g, carry):
        for j in range(GATHER_UNROLL):
            r = g * GATHER_UNROLL + j
            src = slab(u_ref, r)
            pltpu.make_async_copy(src, slab(x_hbm, d1_ref[0, 0, r]), sem).start(priority=j % 2)
            pltpu.make_async_copy(src, slab(x_hbm, d2_ref[0, 0, r]), sem).start(priority=(j + 1) % 2)
        return carry

    lax.fori_loop(0, rows // GATHER_UNROLL, body, 0)
    whole = x_hbm.at[pl.ds(0, rows * TOKEN_SLAB), :]
    pltpu.make_async_copy(u_ref, whole, sem).wait()
    pltpu.make_async_copy(u_ref, whole, sem).wait()


def _dispatch(u2_slabs, dest1, dest2, n_slot):
    n = dest1.shape[0]
    t = DISPATCH_ROWS
    nt = n // t
    idx = pl.BlockSpec((1, 1, t), lambda i: (i, 0, 0), memory_space=pltpu.SMEM)
    any_spec = pl.BlockSpec(memory_space=pl.ANY)
    x_init = jnp.zeros((n_slot * TOKEN_SLAB, LANES), jnp.uint32)
    return pl.pallas_call(
        _dispatch_kernel,
        grid=(nt,),
        in_specs=[idx, idx, pl.BlockSpec((t * TOKEN_SLAB, LANES), lambda i: (i, 0)), any_spec],
        out_specs=any_spec,
        out_shape=jax.ShapeDtypeStruct(x_init.shape, x_init.dtype),
        scratch_shapes=[pltpu.SemaphoreType.DMA(())],
        input_output_aliases={3: 0},
        compiler_params=_cparams(("arbitrary",)),
        name="dispatch",
    )(dest1.reshape(nt, 1, t), dest2.reshape(nt, 1, t), u2_slabs, x_init)


def _expert_kernel(blk_expert_ref, n_used_ref, x_ref, wg_ref, wu_ref, wd_ref, y_ref, wg_s, wu_s, wd_s):
    i = pl.program_id(0)
    n_used = n_used_ref[0]

    @pl.when(i >= n_used)
    def _():
        y_ref[...] = jnp.zeros_like(y_ref)

    @pl.when(i < n_used)
    def _():
        @pl.when((i == 0) | (blk_expert_ref[i] != blk_expert_ref[jnp.maximum(i - 1, 0)]))
        def _():
            wg_s[...] = wg_ref[0].astype(BF16)
            wu_s[...] = wu_ref[0].astype(BF16)
            wd_s[...] = wd_ref[0].astype(BF16)

        x = _unpack_bf16_pairs(_load_token_slabs(x_ref, EXPERT_ROWS)).astype(BF16)
        hg = jnp.dot(x, wg_s[...], preferred_element_type=F32)
        hu = jnp.dot(x, wu_s[...], preferred_element_type=F32)
        act = (hg * _sigmoid(hg) * hu).astype(BF16)
        _store_token_slabs(y_ref, _pack_bf16_pairs(jnp.dot(act, wd_s[...], preferred_element_type=F32)))


def _expert_mlp(x_slabs, block_expert, n_used, w_gate, w_up, w_down):
    n_blk = block_expert.shape[0]
    d = w_gate.shape[1]
    hdim = w_gate.shape[2]
    rows = EXPERT_ROWS
    used = lambda i, nu: jnp.minimum(i, nu[0] - 1)
    weight = lambda i, be, nu: (be[used(i, nu)], 0, 0)
    grid_spec = pltpu.PrefetchScalarGridSpec(
        num_scalar_prefetch=2,
        grid=(n_blk,),
        in_specs=[pl.BlockSpec((rows * TOKEN_SLAB, LANES), lambda i, be, nu: (used(i, nu), 0)),
                  pl.BlockSpec((1, d, hdim), weight),
                  pl.BlockSpec((1, d, hdim), weight),
                  pl.BlockSpec((1, hdim, d), weight)],
        out_specs=pl.BlockSpec((rows * TOKEN_SLAB, LANES), lambda i, be, nu: (i, 0)),
        scratch_shapes=[pltpu.VMEM((d, hdim), BF16), pltpu.VMEM((d, hdim), BF16), pltpu.VMEM((hdim, d), BF16)],
    )
    return pl.pallas_call(
        _expert_kernel,
        grid_spec=grid_spec,
        out_shape=jax.ShapeDtypeStruct((n_blk * rows * TOKEN_SLAB, LANES), jnp.uint32),
        compiler_params=_cparams(("arbitrary",)),
        name="expert_mlp",
    )(block_expert, n_used, x_slabs, w_gate, w_up, w_down)


def _combine_kernel(d1_ref, d2_ref, d1n_ref, d2n_ref, y_hbm, route_ref, h1_ref, g2_ref, lng_ref, lnb_ref,
                    o_ref, abuf, bbuf, sem_a, sem_b):
    i = pl.program_id(0)
    n = pl.num_programs(0)
    slot = i % 2
    rows = o_ref.shape[0]

    @pl.when(i == 0)
    def _():
        _start_tile_gather(y_hbm, d1_ref, abuf, sem_a, 0, rows, GATHER_PRIORITIES)
        _start_tile_gather(y_hbm, d2_ref, bbuf, sem_b, 0, rows, GATHER_PRIORITIES)

    @pl.when(i + 1 < n)
    def _():
        _start_tile_gather(y_hbm, d1n_ref, abuf, sem_a, 1 - slot, rows, GATHER_PRIORITIES)
        _start_tile_gather(y_hbm, d2n_ref, bbuf, sem_b, 1 - slot, rows, GATHER_PRIORITIES)

    _wait_tile_gather(y_hbm, d1_ref, abuf, sem_a, slot, rows)
    _wait_tile_gather(y_hbm, d2_ref, bbuf, sem_b, slot, rows)
    route = route_ref[...]
    f = (_unpack_bf16_pairs(_load_token_slabs(abuf.at[slot], rows)) * route[:, ROUTE_G1:ROUTE_G1 + 1]
         + _unpack_bf16_pairs(_load_token_slabs(bbuf.at[slot], rows)) * route[:, ROUTE_G2:ROUTE_G2 + 1])
    o_ref[...] = _layer_norm(DEEPNORM_ALPHA * h1_ref[...] + g2_ref[0] * f, lng_ref[...], lnb_ref[...])


def _combine(y_tiles, dest1, dest2, route, h1, g2, ln_g, ln_b, tokens_per_batch):
    n, d = h1.shape
    t = COMBINE_ROWS
    nt = n // t
    per_b = tokens_per_batch // t
    d1 = dest1.reshape(nt, 1, t)
    d2 = dest2.reshape(nt, 1, t)
    cur = pl.BlockSpec((1, 1, t), lambda i: (i, 0, 0), memory_space=pltpu.SMEM)
    nxt = pl.BlockSpec((1, 1, t), lambda i: (jnp.minimum(i + 1, nt - 1), 0, 0), memory_space=pltpu.SMEM)
    small = lambda arr: pl.BlockSpec(arr.shape, lambda i: (0,) * arr.ndim)
    return pl.pallas_call(
        _combine_kernel,
        grid=(nt,),
        in_specs=[cur, cur, nxt, nxt, pl.BlockSpec(memory_space=pl.ANY),
                  pl.BlockSpec((t, LANES), lambda i: (i, 0)),
                  pl.BlockSpec((t, d), lambda i: (i, 0)),
                  pl.BlockSpec((1, 1, d), lambda i: (i // per_b, 0, 0)),
                  small(ln_g), small(ln_b)],
        out_specs=pl.BlockSpec((t, d), lambda i: (i, 0)),
        out_shape=jax.ShapeDtypeStruct((n, d), F32),
        scratch_shapes=[pltpu.VMEM((2, t * TOKEN_SLAB, LANES), jnp.uint32),
                        pltpu.VMEM((2, t * TOKEN_SLAB, LANES), jnp.uint32),
                        pltpu.SemaphoreType.DMA((2,)), pltpu.SemaphoreType.DMA((2,))],
        compiler_params=_cparams(("arbitrary",)),
        name="combine",
    )(d1, d2, d1, d2, y_tiles, route, h1, g2, ln_g, ln_b)


def _hi_lo(w):
    hi = w.astype(BF16)
    return jnp.stack([hi, (w - hi.astype(F32)).astype(BF16)])


def _block_diag2(w):
    z = jnp.zeros_like(w[0])
    return jnp.concatenate([jnp.concatenate([w[0], z], axis=1), jnp.concatenate([z, w[1]], axis=1)], axis=0)


def kernel(x, c, ctx, c_ctx, emb_ln_g, emb_ln_b, w_mod, b_mod, w_in, tshift_mu, rwkv_w0, rwkv_w2, rwkv_a0, rwkv_a2, rwkv_g2, rwkv_k_k, rwkv_k_a, rwkv_r_k, rwkv_gn_g, rwkv_gn_b, ret_decay, ret_gn_g, ret_gn_b, w_out, ln1_g, ln1_b, router_group, router_group_bias, router_expert, router_expert_bias, expert_w_gate, expert_w_up, expert_w_down, ln2_g, ln2_b):
    assert w_mod.shape[0] == 1, "written for DEPTH == 1 (context outputs are never emitted)"
    b, n_tok, d = x.shape
    n_ctx = ctx.shape[1]
    row = lambda v: v.reshape(1, -1)

    c_rows = jnp.zeros((SUBLANES, d), F32).at[:b].set(c).at[b].set(c_ctx)
    mod = _modulation(c_rows, w_mod[0], row(b_mod[0]))
    sh1, s1, g1, sh2, s2, g2 = [mod[:b, j * d:(j + 1) * d].reshape(b, 1, d) for j in range(6)]
    sh1c, s1c = [jnp.broadcast_to(mod[b, j * d:(j + 1) * d].reshape(1, 1, d), (b, 1, d)) for j in range(2)]

    w_in_bf16 = w_in[0].astype(BF16)
    pr, pt = _in_proj(x, row(emb_ln_g), row(emb_ln_b), s1, sh1, w_in_bf16)
    pr_c, pt_c = _in_proj(ctx, row(emb_ln_g), row(emb_ln_b), s1c, sh1c, w_in_bf16)

    prep_params = (row(tshift_mu[0]), row(rwkv_w0[0]), _hi_lo(_block_diag2(rwkv_w2[0])), row(rwkv_a0[0]),
                   _hi_lo(_block_diag2(rwkv_a2[0])), _hi_lo(rwkv_g2[0]), row(rwkv_k_k[0]), row(rwkv_k_a[0]),
                   row(rwkv_r_k[0]),
                   _segment_ones(D_RWKV, RWKV_HEAD))
    lat = _rwkv_prepare(pr, prep_params, grid_shift=True)
    cx = _rwkv_prepare(pr_c, prep_params, grid_shift=False)
    r_l, v_l, kk_l, w_l, kd_l, bb_l, gate_l, bonus_l = lat
    r_c, v_c, kk_c, w_c, kd_c, bb_c, _, _ = cx

    y_f, y_b = _wkv7((r_l, v_l, kk_l, w_l, kd_l, bb_l), (r_c, v_c, kk_c, w_c, kd_c, bb_c), b, n_tok, n_ctx)

    cos_t, sin_t = _rope_tables(n_tok)
    t_f, t_b = _retention(pt, pt_c, ret_decay[0], cos_t, sin_t)

    wr = jnp.zeros((d, LANES), F32).at[:, :N_GROUPS].set(router_group[0])
    wr = wr.at[:, N_GROUPS:N_GROUPS + N_EXPERTS].set(router_expert[0])
    br = jnp.zeros((1, LANES), F32).at[0, :N_GROUPS].set(router_group_bias[0])
    br = br.at[0, N_GROUPS:N_GROUPS + N_EXPERTS].set(router_expert_bias[0].reshape(-1))
    vecs = (row(emb_ln_g), row(emb_ln_b), g1, s2, sh2, row(rwkv_gn_g[0]), row(rwkv_gn_b[0]),
            row(ret_gn_g[0]), row(ret_gn_b[0]), row(ln1_g[0]), row(ln1_b[0]), br)
    wr_hi_lo = _hi_lo(wr)
    mats = (_segment_ones(D_RWKV, RWKV_HEAD), w_out[0].astype(BF16), wr_hi_lo[0], wr_hi_lo[1])
    h1, u2, route, route_t, counts = _out_proj(x, y_f, y_b, bonus_l, gate_l, t_f, t_b, pt, vecs, mats)

    n_all = b * n_tok

    e1 = route_t[ROUTE_E1].astype(jnp.int32)
    e2 = route_t[ROUTE_E2].astype(jnp.int32)
    cnt = counts[0, :N_EXPERTS].astype(jnp.int32)
    padded = ((cnt + EXPERT_ROWS - 1) // EXPERT_ROWS) * EXPERT_ROWS
    pends = jnp.cumsum(padded)
    pstarts = pends - padded
    expert_ids = jnp.arange(N_EXPERTS, dtype=jnp.int32)
    start_of = lambda e: jnp.sum(jnp.where(e[:, None] == expert_ids[None, :], pstarts[None, :], 0), axis=1)
    dest1 = start_of(e1) + route_t[ROUTE_RANK1].astype(jnp.int32)
    dest2 = start_of(e2) + route_t[ROUTE_RANK2].astype(jnp.int32)
    n_blk = -(-(n_all * 2) // EXPERT_ROWS) + N_EXPERTS
    block_start = jnp.arange(n_blk, dtype=jnp.int32) * EXPERT_ROWS
    block_expert = jnp.minimum(jnp.sum((block_start[:, None] >= pends[None, :]).astype(jnp.int32), axis=1),
                               N_EXPERTS - 1)

    n_used = (pends[N_EXPERTS - 1:] // EXPERT_ROWS).astype(jnp.int32)
    x_slabs = _dispatch(u2, dest1, dest2, n_blk * EXPERT_ROWS)
    y_tiles = _expert_mlp(x_slabs, block_expert, n_used, expert_w_gate[0], expert_w_up[0], expert_w_down[0])
    out = _combine(y_tiles, dest1, dest2, route, h1.reshape(n_all, d), g2, row(ln2_g[0]), row(ln2_b[0]), n_tok)
    return out.reshape(b, n_tok, d)
```

```python
import functools
import math

import jax
import jax.numpy as jnp
import numpy as np
from jax import lax
from jax.experimental import pallas as pl
from jax.experimental.pallas import tpu as pltpu

F32 = jnp.float32
BF16 = jnp.bfloat16
HIGHEST = lax.Precision.HIGHEST

GRID_W = 64
D_RWKV = 512
RWKV_HEAD = 64
RWKV_HEADS = D_RWKV // RWKV_HEAD
DECAY_LORA = 64
AAA_LORA = 64
GATE_LORA = 128
D_RET = 512
RET_HEADS = 4
RET_HEAD = D_RET // RET_HEADS
RET_CHUNK = 128
RET_CHUNKS_PER_STEP = 4
RWKV_COLS = 3 * D_RWKV + 2 * (DECAY_LORA + AAA_LORA) + GATE_LORA
RET_COLS = 4 * D_RET
N_GROUPS = 4
EXPERTS_PER_GROUP = 8
N_EXPERTS = N_GROUPS * EXPERTS_PER_GROUP
EXPERT_HIDDEN = 512
ROPE_BASE = 10000.0
LN_EPS = 1e-5
RWKV_GN_EPS = 64e-5
RET_GN_EPS = 1e-5
DEEPNORM_ALPHA = 2.0 ** 0.25
EXP_NEG_HALF = math.exp(-0.5)

LANES = 128
SUBLANES = 8
VMEM_LIMIT_BYTES = 56 * 1024 * 1024

PREPARE_ROWS = 256
MODULATION_COLS = 1536

WKV_CHUNK = 64


IN_PROJ_ROWS = 512
OUT_PROJ_ROWS = 512

TOKEN_SLAB = 4
EXPERT_ROWS = 512
COMBINE_ROWS = 512
DISPATCH_ROWS = 512
GATHER_UNROLL = 16
GATHER_PRIORITIES = (0, 1)


def _pack_bf16_pairs(x):
    half = x.shape[1] // 2

    def bf16_bits(v):
        b = lax.bitcast_convert_type(v, jnp.uint32)
        return (b + jnp.uint32(0x7FFF) + ((b >> 16) & jnp.uint32(1))) >> 16

    return bf16_bits(x[:, :half]) | (bf16_bits(x[:, half:]) << 16)


def _unpack_bf16_pairs(p):
    lo = lax.bitcast_convert_type(p << 16, F32)
    hi = lax.bitcast_convert_type(p & jnp.uint32(0xFFFF0000), F32)
    return jnp.concatenate([lo, hi], axis=-1)


def _store_token_slabs(ref, x):
    rows = x.shape[0]
    for j in range(TOKEN_SLAB):
        ref[pl.ds(j, rows, stride=TOKEN_SLAB), :] = x[:, j * LANES:(j + 1) * LANES]


def _load_token_slabs(ref, rows):
    return jnp.concatenate([ref[pl.ds(j, rows, stride=TOKEN_SLAB), :] for j in range(TOKEN_SLAB)], axis=-1)


def _cparams(sem):
    return pltpu.CompilerParams(dimension_semantics=sem, vmem_limit_bytes=VMEM_LIMIT_BYTES)


def _layer_norm(x, g, b, eps=LN_EPS):
    mu = jnp.mean(x, axis=-1, keepdims=True)
    xc = x - mu
    var = jnp.mean(xc * xc, axis=-1, keepdims=True)
    return xc * lax.rsqrt(var + eps) * g + b


def _sigmoid(x):
    return 1.0 / (1.0 + jnp.exp(-x))


def _split_bf16(x):
    hi = x.astype(BF16)
    return hi, (x - hi.astype(F32)).astype(BF16)


def _segsum(x, ones_bf16):
    t = x.shape[0]
    s = jnp.dot(jnp.concatenate(_split_bf16(x), axis=0), ones_bf16, preferred_element_type=F32)
    return s[:t] + s[t:]


def _dot_split(x, w_hi, w_lo):
    hi, lo = _split_bf16(x)
    acc = jnp.dot(hi, w_hi, preferred_element_type=F32)
    acc = acc + jnp.dot(lo, w_hi, preferred_element_type=F32)
    return acc + jnp.dot(hi, w_lo, preferred_element_type=F32)


def _segment_ones(width, seg):
    idx = np.arange(width) // seg
    return jnp.asarray(idx[:, None] == idx[None, :], dtype=BF16)


def _mod_kernel(c_ref, w_ref, b_ref, o_ref):
    c = c_ref[...]
    sc = c * _sigmoid(c)
    o_ref[...] = jnp.dot(sc, w_ref[...], precision=HIGHEST, preferred_element_type=F32) + b_ref[...]


def _modulation(c_rows, w_mod, b_mod):
    rows, d = c_rows.shape
    n = w_mod.shape[1]
    tn = MODULATION_COLS
    return pl.pallas_call(
        _mod_kernel,
        grid=(n // tn,),
        in_specs=[pl.BlockSpec((rows, d), lambda j: (0, 0)),
                  pl.BlockSpec((d, tn), lambda j: (0, j)),
                  pl.BlockSpec((1, tn), lambda j: (0, j))],
        out_specs=pl.BlockSpec((rows, tn), lambda j: (0, j)),
        out_shape=jax.ShapeDtypeStruct((rows, n), F32),
        compiler_params=_cparams(("arbitrary",)),
        name="modulation",
    )(c_rows, w_mod, b_mod)


def _in_proj_kernel(x_ref, g_ref, b_ref, s_ref, sh_ref, w_ref, pr_ref, pt_ref):
    h = _layer_norm(x_ref[0], g_ref[...], b_ref[...])
    u = h * (1.0 + s_ref[0]) + sh_ref[0]
    p = jnp.dot(u.astype(BF16), w_ref[...], preferred_element_type=F32)
    pr_ref[0] = p[:, :RWKV_COLS]
    pt_ref[0] = p[:, RWKV_COLS:]


def _in_proj(x, ln_g, ln_b, s1, sh1, w_in_bf16):
    b, n, d = x.shape
    tm = min(IN_PROJ_ROWS, n)
    cols = w_in_bf16.shape[1]
    return pl.pallas_call(
        _in_proj_kernel,
        grid=(b, n // tm),
        in_specs=[pl.BlockSpec((1, tm, d), lambda bi, i: (bi, i, 0)),
                  pl.BlockSpec((1, d), lambda bi, i: (0, 0)),
                  pl.BlockSpec((1, d), lambda bi, i: (0, 0)),
                  pl.BlockSpec((1, 1, d), lambda bi, i: (bi, 0, 0)),
                  pl.BlockSpec((1, 1, d), lambda bi, i: (bi, 0, 0)),
                  pl.BlockSpec((d, cols), lambda bi, i: (0, 0))],
        out_specs=[pl.BlockSpec((1, tm, RWKV_COLS), lambda bi, i: (bi, i, 0)),
                   pl.BlockSpec((1, tm, RET_COLS), lambda bi, i: (bi, i, 0))],
        out_shape=[jax.ShapeDtypeStruct((b, n, RWKV_COLS), F32),
                   jax.ShapeDtypeStruct((b, n, RET_COLS), F32)],
        compiler_params=_cparams(("arbitrary", "arbitrary")),
        name="in_proj",
    )(x, ln_g, ln_b, s1, sh1, w_in_bf16)


def _rwkv_prepare_kernel(cur_ref, prev_ref, next_ref, mu_ref, w0_ref, w2_ref, a0_ref, a2_ref, g2_ref,
                         kk_scale_ref, ka_ref, rk_ref, ones_ref,
                         r_ref, v_ref, kk_ref, w_ref, kd_ref, bb_ref, g_ref, bonus_ref,
                         *, grid_shift, n_tok):
    cur = cur_ref[0]
    t, c = cur.shape
    row = lax.broadcasted_iota(jnp.int32, (t, c), 0)
    lane = lax.broadcasted_iota(jnp.int32, (t, c), 1)
    prev_tok = pltpu.roll(cur, 1, 0)
    next_tok = pltpu.roll(cur, t - 1, 0)
    if grid_shift:
        col = row & (GRID_W - 1)
        tok = row + pl.program_id(1) * t
        left = jnp.where(col > 0, prev_tok, 0.0)
        right = jnp.where(col < GRID_W - 1, next_tok, 0.0)
        up = jnp.where(tok >= GRID_W, jnp.concatenate([prev_ref[0], cur[:t - GRID_W]], axis=0), 0.0)
        down = jnp.where(tok < n_tok - GRID_W, jnp.concatenate([cur[GRID_W:], next_ref[0]], axis=0), 0.0)
        cm = lane & 3
        shifted = jnp.where(cm == 0, left, jnp.where(cm == 1, right, jnp.where(cm == 2, up, down)))
    else:
        prev_tok = jnp.where(row > 0, prev_tok, 0.0)
        next_tok = jnp.where(row < t - 1, next_tok, 0.0)
        shifted = jnp.where((lane & 1) == 0, prev_tok, next_tok)
    pm = cur + mu_ref[...] * (shifted - cur)

    r = pm[:, 0:D_RWKV]
    k = pm[:, D_RWKV:2 * D_RWKV]
    v = pm[:, 2 * D_RWKV:3 * D_RWKV]
    o = 3 * D_RWKV
    lw = pm[:, o:o + 2 * DECAY_LORA]
    la = pm[:, o + 2 * DECAY_LORA:o + 2 * (DECAY_LORA + AAA_LORA)]
    lg = pm[:, o + 2 * (DECAY_LORA + AAA_LORA):]

    w = w0_ref[...] + _dot_split(jnp.tanh(lw), w2_ref[0], w2_ref[1])
    log_decay = -EXP_NEG_HALF * _sigmoid(w)
    a = _sigmoid(a0_ref[...] + _dot_split(la, a2_ref[0], a2_ref[1]))
    gate = _dot_split(_sigmoid(lg), g2_ref[0], g2_ref[1])

    ones = ones_ref[...]
    kk_raw = k * kk_scale_ref[...]
    kk = kk_raw / jnp.maximum(jnp.sqrt(_segsum(kk_raw * kk_raw, ones)), 1e-12)
    ka = ka_ref[...]
    a0 = a[:, :D_RWKV]
    a1 = a[:, D_RWKV:]
    kd0 = k * (1.0 + (a0 - 1.0) * ka)
    kd1 = k * (1.0 + (a1 - 1.0) * ka)
    bonus = _segsum(r * (kd0 + kd1) * rk_ref[...], ones) * v

    r_ref[0] = r
    v_ref[0] = v
    kk_ref[0] = kk
    w_ref[0] = log_decay
    kd_ref[0] = jnp.concatenate([kd0, kd1], axis=-1)
    bb_ref[0] = jnp.concatenate([kk * a0, kk * a1], axis=-1)
    g_ref[0] = gate
    bonus_ref[0] = bonus


def _rwkv_prepare(pr, params, grid_shift):
    b, n, c = pr.shape
    t = PREPARE_ROWS
    if not grid_shift:
        assert n == t, "sequence token shift is written for a single tile"
    halo_blocks = n // GRID_W
    per_tile = t // GRID_W
    small = lambda shape: pl.BlockSpec(shape, lambda bi, i: (0,) * len(shape))
    tok_spec = lambda width: pl.BlockSpec((1, t, width), lambda bi, i: (bi, i, 0))
    out_widths = (D_RWKV, D_RWKV, D_RWKV, 2 * D_RWKV, 2 * D_RWKV, 2 * D_RWKV, D_RWKV, D_RWKV)
    kernel = functools.partial(_rwkv_prepare_kernel, grid_shift=grid_shift, n_tok=n)
    return pl.pallas_call(
        kernel,
        grid=(b, n // t),
        in_specs=[tok_spec(c),
                  pl.BlockSpec((1, GRID_W, c), lambda bi, i: (bi, jnp.maximum(i * per_tile - 1, 0), 0)),
                  pl.BlockSpec((1, GRID_W, c),
                               lambda bi, i: (bi, jnp.minimum((i + 1) * per_tile, halo_blocks - 1), 0)),
                  small((1, c)), small((1, 2 * D_RWKV)), small((2, 2 * DECAY_LORA, 2 * D_RWKV)),
                  small((1, 2 * D_RWKV)), small((2, 2 * AAA_LORA, 2 * D_RWKV)), small((2, GATE_LORA, D_RWKV)),
                  small((1, D_RWKV)), small((1, D_RWKV)), small((1, D_RWKV)), small((D_RWKV, D_RWKV))],
        out_specs=[tok_spec(wd) for wd in out_widths],
        out_shape=[jax.ShapeDtypeStruct((b, n, wd), F32) for wd in out_widths],
        compiler_params=_cparams(("arbitrary", "arbitrary")),
        name="rwkv_prepare",
    )(pr, pr, pr, *params)


def _bdot(a, b):
    return jnp.dot(a.astype(BF16), b.astype(BF16), preferred_element_type=F32)


def _bdot_nt(a, b):
    return lax.dot_general(a.astype(BF16), b.astype(BF16), (((1,), (1,)), ((), ())), preferred_element_type=F32)


def _bdot_tn(a, b):
    return lax.dot_general(a.astype(BF16), b.astype(BF16), (((0,), (0,)), ((), ())), preferred_element_type=F32)


def _wkv7_chunk_kernel(*refs, n_ctx_chunks):
    c = WKV_CHUNK
    p = 2 * c
    n_in = 12
    in_refs = (refs[:n_in], refs[n_in:2 * n_in])
    y_refs = refs[2 * n_in:2 * n_in + 2]
    state_ref = refs[2 * n_in + 2]
    n = pl.program_id(0)
    n_batch = in_refs[0][0].shape[0]
    pairs_per_batch = RWKV_HEADS // 2
    pairs_per_dir = n_batch * pairs_per_batch

    @pl.when(n == 0)
    def _():
        state_ref[...] = jnp.zeros_like(state_ref)

    is_ctx = n < n_ctx_chunks
    ti = lax.broadcasted_iota(jnp.int32, (c, c), 0)
    tj = lax.broadcasted_iota(jnp.int32, (c, c), 1)
    ri = lax.broadcasted_iota(jnp.int32, (p, p), 0)
    ci = lax.broadcasted_iota(jnp.int32, (p, p), 1)
    same_head = (ri >= c) == (ci >= c)
    ii = ri & (c - 1)
    jj = ci & (c - 1)
    eye = (ri == ci).astype(F32)
    first = lax.broadcasted_iota(jnp.int32, (c, p), 1) < RWKV_HEAD

    def stack(x):
        return jnp.concatenate([jnp.where(first, x, 0.0), jnp.where(first, 0.0, x)], axis=0)

    def unstack(x):
        return x[:c] + x[c:]

    a_st, r_st, k_st, b_st, k2_st, b2_st, v_st, g_chunk, earlier, upto_self = ([] for _ in range(10))
    for d in range(2):
        r_l, v_l, kk_l, lw_l, kd_l, bb_l, r_c, v_c, kk_c, lw_c, kd_c, bb_c = in_refs[d]
        pick = lambda xc, xl: jnp.concatenate(
            [jnp.where(is_ctx, xc[bi], xl[bi]) for bi in range(n_batch)], axis=-1)
        r, v, kk, lw, kd, bb = (pick(r_c, r_l), pick(v_c, v_l), pick(kk_c, kk_l), pick(lw_c, lw_l),
                                pick(kd_c, kd_l), pick(bb_c, bb_l))
        before = (tj < ti) if d == 0 else (tj > ti)
        upto = (before | (ti == tj)).astype(BF16)
        hi = lw.astype(BF16)
        r1 = lw - hi.astype(F32)
        mid = r1.astype(BF16)
        lo = (r1 - mid.astype(F32)).astype(BF16)
        cum = (jnp.dot(upto, hi, preferred_element_type=F32) + jnp.dot(upto, mid, preferred_element_type=F32)
               + jnp.dot(upto, lo, preferred_element_type=F32))
        tot = jnp.sum(lw, axis=0, keepdims=True)
        e_neg = jnp.exp(-cum)
        e_rem = jnp.exp(tot - cum)
        alpha = kk * jnp.exp(cum - lw)
        rho = r * jnp.exp(cum)
        beta = bb * e_neg
        kappa = kd * e_neg
        kappa_rem = kd * e_rem
        beta_rem = bb * e_rem
        g_all = jnp.exp(tot)
        pair_before = same_head & ((jj < ii) if d == 0 else (jj > ii))
        pair_upto = pair_before | (ri == ci)
        for hp in range(pairs_per_dir):
            sl = slice(hp * p, (hp + 1) * p)
            a_st.append(stack(alpha[:, sl]))
            r_st.append(stack(rho[:, sl]))
            k_st.append(stack(kappa[:, sl]))
            b_st.append(stack(beta[:, sl]))
            k2_st.append(stack(kappa_rem[:, sl]))
            b2_st.append(stack(beta_rem[:, sl]))
            v_st.append(stack(v[:, sl]))
            g_chunk.append(g_all[:, sl])
            earlier.append(pair_before)
            upto_self.append(pair_upto)

    pairs = range(2 * pairs_per_dir)
    g = [_bdot_nt(jnp.concatenate([a_st[h], r_st[h]], axis=0), jnp.concatenate([k_st[h], b_st[h]], axis=0))
         for h in pairs]
    m1 = [jnp.where(earlier[h], g[h][:p, :p], 0.0) for h in pairs]
    m2 = [jnp.where(earlier[h], g[h][:p, p:], 0.0) for h in pairs]
    n1 = [jnp.where(upto_self[h], g[h][p:, :p], 0.0) for h in pairs]
    n2 = [jnp.where(upto_self[h], g[h][p:, p:], 0.0) for h in pairs]

    in_block = (ii >> 3) == (jj >> 3)
    pw = [-jnp.where(in_block, m2[h], 0.0) for h in pairs]
    inv = [eye + pw[h] for h in pairs]
    pw = [_bdot(pw[h], pw[h]) for h in pairs]
    both = [_bdot(jnp.concatenate([inv[h], pw[h]], axis=0), pw[h]) for h in pairs]
    inv = [inv[h] + both[h][:p] for h in pairs]
    inv = [inv[h] + _bdot(inv[h], both[h][p:]) for h in pairs]
    for sh in (3, 4, 5):
        off = ((ii >> (sh + 1)) == (jj >> (sh + 1))) & ((ii >> sh) != (jj >> sh))
        left = [_bdot(inv[h], jnp.where(off, m2[h], 0.0)) for h in pairs]
        inv = [inv[h] - _bdot(left[h], inv[h]) for h in pairs]

    mnv = [_bdot(jnp.concatenate([m1[h], n1[h]], axis=0), v_st[h]) for h in pairs]
    m1v = [mnv[h][:p] for h in pairs]
    n1v = [mnv[h][p:] for h in pairs]
    au = [_bdot(inv[h], jnp.concatenate([a_st[h], m1v[h]], axis=1)) for h in pairs]
    nn = [_bdot(n2[h], au[h]) for h in pairs]
    pc = [_bdot_tn(b2_st[h], au[h][:, :p]) for h in pairs]
    qc_t = [_bdot_tn(jnp.concatenate([v_st[h], -au[h][:, p:]], axis=0),
                     jnp.concatenate([k2_st[h], b2_st[h]], axis=0)) for h in pairs]
    s0 = [state_ref[h] for h in pairs]
    y = [_bdot_nt(unstack(r_st[h] - nn[h][:, :p]), s0[h]) + unstack(n1v[h] - nn[h][:, p:]) for h in pairs]
    s_dec = [_bdot_nt(s0[h], pc[h]) for h in pairs]
    for h in pairs:
        d, hp = divmod(h, pairs_per_dir)
        bi, hpb = divmod(hp, pairs_per_batch)
        y_refs[d][bi, :, hpb * p:(hpb + 1) * p] = y[h]
        state_ref[h] = s0[h] * g_chunk[h] - s_dec[h] + qc_t[h]


def _wkv7(lat, ctx, b, n_tok, n_ctx):
    c = WKV_CHUNK
    ncx = n_ctx // c
    nl = n_tok // c
    lat_idx = (lambda n: jnp.maximum(n - ncx, 0), lambda n: nl - 1 - jnp.maximum(n - ncx, 0))
    ctx_idx = (lambda n: jnp.minimum(n, ncx - 1), lambda n: ncx - 1 - jnp.minimum(n, ncx - 1))

    def specs(idx, d):
        shared = pl.BlockSpec((b, c, D_RWKV), lambda n: (0, idx(n), 0))
        per_dir = pl.BlockSpec((b, c, D_RWKV), lambda n: (0, idx(n), d))
        return [shared, shared, shared, per_dir, per_dir, per_dir]

    in_specs, args = [], []
    for d in range(2):
        in_specs += specs(lat_idx[d], d) + specs(ctx_idx[d], d)
        args += list(lat) + list(ctx)
    return pl.pallas_call(
        functools.partial(_wkv7_chunk_kernel, n_ctx_chunks=ncx),
        grid=(ncx + nl,),
        in_specs=in_specs,
        out_specs=[pl.BlockSpec((b, c, D_RWKV), lambda n, d=d: (0, lat_idx[d](n), 0)) for d in range(2)],
        out_shape=[jax.ShapeDtypeStruct((b, n_tok, D_RWKV), F32)] * 2,
        scratch_shapes=[pltpu.VMEM((2 * b * RWKV_HEADS // 2, 2 * RWKV_HEAD, 2 * RWKV_HEAD), F32)],
        compiler_params=_cparams(("arbitrary",)),
        name="wkv7_chunk",
    )(*args)


def _rope(z, cos_t, sin_t):
    lane = lax.broadcasted_iota(jnp.int32, z.shape, 1)
    half = RET_HEAD // 4
    partner = jnp.where((lane & (2 * half - 1)) < half, pltpu.roll(z, RET_HEAD - half, 1), pltpu.roll(z, half, 1))
    return z * cos_t + partner * sin_t


def _retention_kernel(dec_ref, fwd_ref, bwd_ref, ctx_ref, cosf_ref, sinf_ref, cosb_ref, sinb_ref,
                      yf_ref, yb_ref, state_ref, dmat_ref, tail_ref, head_ref, cdec_ref):
    c = RET_CHUNK
    scale = RET_HEAD ** -0.5
    ii = lax.broadcasted_iota(jnp.int32, (c, c), 0)
    jj = lax.broadcasted_iota(jnp.int32, (c, c), 1)
    pos = lax.broadcasted_iota(jnp.int32, (c, RET_HEAD), 0).astype(F32)
    n_ctx_chunks = ctx_ref.shape[1] // c

    def head_slices(ref_val, h):
        q = ref_val[:, h * RET_HEAD:(h + 1) * RET_HEAD]
        k = ref_val[:, D_RET + h * RET_HEAD:D_RET + (h + 1) * RET_HEAD]
        v = ref_val[:, 2 * D_RET + h * RET_HEAD:2 * D_RET + (h + 1) * RET_HEAD]
        return q, k, v

    n_batch = fwd_ref.shape[0]
    heads = [(d, h) for d in range(2) for h in range(RET_HEADS)]
    chains = [(bi, d, h) for bi in range(n_batch) for d, h in heads]

    @pl.when(pl.program_id(0) == 0)
    def _():
        for d, h in heads:
            x = jnp.full((1, RET_HEAD), dec_ref[d, h], F32)
            lg = -(jnp.maximum(x, 0.0) + jnp.log(1.0 + jnp.exp(-jnp.abs(x))))
            chunk_decay = jnp.exp(lg * float(c))
            tail = jnp.exp(lg * ((c - 1.0 - pos) if d == 0 else pos))
            rel = (ii - jj) if d == 0 else (jj - ii)
            mask = (rel >= 0) if d == 0 else (rel > 0)
            dmat_ref[d, h] = jnp.where(mask, jnp.exp(lg * jnp.maximum(rel, 0).astype(F32)), 0.0)
            tail_ref[d, h] = tail
            head_ref[d, h] = jnp.exp(lg * ((pos + 1.0) if d == 0 else (c - pos)))
            cdec_ref[d, h] = jnp.broadcast_to(chunk_decay, (SUBLANES, RET_HEAD))
            order = range(n_ctx_chunks) if d == 0 else range(n_ctx_chunks - 1, -1, -1)
            for bi in range(n_batch):
                s = jnp.zeros((RET_HEAD, RET_HEAD), F32)
                for cc in order:
                    _, kc, vc = head_slices(ctx_ref[bi, cc * c:(cc + 1) * c, :], h)
                    s = s * chunk_decay + _bdot_tn(kc * scale * tail, vc)
                state_ref[bi, d, h] = s

    n_sub = RET_CHUNKS_PER_STEP

    def rows(d, sub):
        first = sub * c if d == 0 else (n_sub - 1 - sub) * c
        return slice(first, first + c)

    work = [(bi, d, h, sub) for sub in range(n_sub) for bi, d, h in chains]
    qkv = {}
    for bi, d, h, sub in work:
        blk = (fwd_ref if d == 0 else bwd_ref)[bi, rows(d, sub), :]
        cos_t = (cosf_ref if d == 0 else cosb_ref)[rows(d, sub), :]
        sin_t = (sinf_ref if d == 0 else sinb_ref)[rows(d, sub), :]
        q, k, v = head_slices(blk, h)
        qkv[bi, d, h, sub] = (_rope(q, cos_t, sin_t), _rope(k, cos_t, sin_t) * scale, v.astype(BF16))
    scores = {w: _bdot_nt(qkv[w][0], qkv[w][1]) for w in work}
    inner = {w: _bdot(scores[w] * dmat_ref[w[1], w[2]], qkv[w][2]) for w in work}
    upd = {w: _bdot_tn(qkv[w][1] * tail_ref[w[1], w[2]], qkv[w][2]) for w in work}
    state = {ch: state_ref[ch] for ch in chains}
    for sub in range(n_sub):
        cross = {ch: _bdot(qkv[ch + (sub,)][0] * head_ref[ch[1], ch[2]], state[ch]) for ch in chains}
        for bi, d, h in chains:
            ch = (bi, d, h)
            state[ch] = state[ch] * cdec_ref[d, h, 0:1, :] + upd[ch + (sub,)]
            out_ref = yf_ref if d == 0 else yb_ref
            out_ref[bi, rows(d, sub), h * RET_HEAD:(h + 1) * RET_HEAD] = inner[ch + (sub,)] + cross[ch]
    for ch in chains:
        state_ref[ch] = state[ch]


def _retention(pt, pt_ctx, ret_decay, cos_t, sin_t):
    b, n, _ = pt.shape
    c = RET_CHUNK
    rows = RET_CHUNKS_PER_STEP * c
    steps = n // rows
    qkv = 3 * D_RET
    fwd = lambda i: (0, i, 0)
    bwd = lambda i: (0, steps - 1 - i, 0)
    return pl.pallas_call(
        _retention_kernel,
        grid=(steps,),
        in_specs=[pl.BlockSpec(memory_space=pltpu.SMEM),
                  pl.BlockSpec((b, rows, qkv), fwd),
                  pl.BlockSpec((b, rows, qkv), bwd),
                  pl.BlockSpec((b, pt_ctx.shape[1], qkv), lambda i: (0, 0, 0)),
                  pl.BlockSpec((rows, RET_HEAD), lambda i: (i, 0)),
                  pl.BlockSpec((rows, RET_HEAD), lambda i: (i, 0)),
                  pl.BlockSpec((rows, RET_HEAD), lambda i: (steps - 1 - i, 0)),
                  pl.BlockSpec((rows, RET_HEAD), lambda i: (steps - 1 - i, 0))],
        out_specs=[pl.BlockSpec((b, rows, D_RET), fwd), pl.BlockSpec((b, rows, D_RET), bwd)],
        out_shape=[jax.ShapeDtypeStruct((b, n, D_RET), F32), jax.ShapeDtypeStruct((b, n, D_RET), F32)],
        scratch_shapes=[pltpu.VMEM((b, 2, RET_HEADS, RET_HEAD, RET_HEAD), F32),
                        pltpu.VMEM((2, RET_HEADS, c, c), F32),
                        pltpu.VMEM((2, RET_HEADS, c, RET_HEAD), F32),
                        pltpu.VMEM((2, RET_HEADS, c, RET_HEAD), F32),
                        pltpu.VMEM((2, RET_HEADS, SUBLANES, RET_HEAD), F32)],
        compiler_params=_cparams(("arbitrary",)),
        name="retention",
    )(ret_decay, pt, pt, pt_ctx, cos_t, sin_t, cos_t, sin_t)


def _rope_tables(n_tok):
    nf = RET_HEAD // 4
    lane = np.arange(RET_HEAD)
    inv = ROPE_BASE ** (-jnp.arange(nf, dtype=F32) / nf)
    t = jnp.arange(n_tok)
    pos = jnp.where((lane // (2 * nf) == 0)[None, :], (t // GRID_W)[:, None], (t % GRID_W)[:, None]).astype(F32)
    ang = pos * inv[lane % nf][None, :]
    sign = jnp.where((lane % (2 * nf)) < nf, -1.0, 1.0).astype(F32)
    return jnp.cos(ang), jnp.sin(ang) * sign[None, :]


def _group_norm(y, ones, seg, eps, g, b):
    mu = _segsum(y, ones) * (1.0 / seg)
    yc = y - mu
    var = _segsum(yc * yc, ones) * (1.0 / seg)
    return yc * lax.rsqrt(var + eps) * g + b


def _out_proj_kernel(x_ref, yf_ref, yb_ref, bonus_ref, gate_ref, tf_ref, tb_ref, gt_ref,
                     embg_ref, embb_ref, g1_ref, s2_ref, sh2_ref, rgn_g_ref, rgn_b_ref, tgn_g_ref, tgn_b_ref,
                     ones_r_ref, wout_ref, ln1g_ref, ln1b_ref, wrh_ref, wrl_ref, br_ref,
                     h1_ref, u2_ref, route_ref, route_t_ref, count_ref, carry_ref):
    @pl.when((pl.program_id(0) == 0) & (pl.program_id(1) == 0))
    def _():
        carry_ref[...] = jnp.zeros_like(carry_ref)

    y = yf_ref[0] + yb_ref[0]
    o_rwkv = _group_norm(y, ones_r_ref[...], RWKV_HEAD, RWKV_GN_EPS, rgn_g_ref[...], rgn_b_ref[...])
    o_rwkv = (o_rwkv + bonus_ref[0]) * gate_ref[0]
    yt = tf_ref[0] + tb_ref[0]
    gt = gt_ref[0]
    tgn_g = tgn_g_ref[...]
    tgn_b = tgn_b_ref[...]
    o_ret = jnp.concatenate(
        [_layer_norm(yt[:, h * RET_HEAD:(h + 1) * RET_HEAD], tgn_g[:, h * RET_HEAD:(h + 1) * RET_HEAD],
                     tgn_b[:, h * RET_HEAD:(h + 1) * RET_HEAD], RET_GN_EPS) for h in range(RET_HEADS)], axis=-1)
    o_ret = o_ret * (gt * _sigmoid(gt))
    cat = jnp.concatenate([o_rwkv, o_ret], axis=-1).astype(BF16)
    mix = jnp.dot(cat, wout_ref[...], preferred_element_type=F32)
    h = _layer_norm(x_ref[0], embg_ref[...], embb_ref[...])
    h1 = _layer_norm(DEEPNORM_ALPHA * h + g1_ref[0] * mix, ln1g_ref[...], ln1b_ref[...])
    u2 = h1 * (1.0 + s2_ref[0]) + sh2_ref[0]
    h1_ref[0] = h1
    _store_token_slabs(u2_ref, _pack_bf16_pairs(u2))
    route = _route_tile(_dot_split(u2, wrh_ref[...], wrl_ref[...]) + br_ref[...], carry_ref)
    route_ref[...] = route
    route_t_ref[...] = route.T[:SUBLANES]
    count_ref[...] = carry_ref[...]


def _out_proj(x, y_f, y_b, bonus, gate, t_f, t_b, pt, vecs, mats):
    b, n, d = x.shape
    t = OUT_PROJ_ROWS
    tok = lambda width: pl.BlockSpec((1, t, width), lambda bi, i: (bi, i, 0))
    per_b = pl.BlockSpec((1, 1, d), lambda bi, i: (bi, 0, 0))
    small = lambda arr: pl.BlockSpec(arr.shape, lambda bi, i: (0,) * arr.ndim)
    (embg, embb, g1, s2, sh2, rgn_g, rgn_b, tgn_g, tgn_b, ln1g, ln1b, br) = vecs
    (ones_r, wout, wr_hi, wr_lo) = mats
    gt_spec = pl.BlockSpec((1, t, D_RET), lambda bi, i: (bi, i, 3))
    tiles = n // t
    flat = lambda bi, i: (bi * tiles + i, 0)
    args = (x, y_f, y_b, bonus, gate, t_f, t_b, pt, embg, embb, g1, s2, sh2, rgn_g, rgn_b, tgn_g, tgn_b,
            ones_r, wout, ln1g, ln1b, wr_hi, wr_lo, br)
    in_specs = [tok(d)] + [tok(D_RWKV)] * 6 + [gt_spec, small(embg), small(embb), per_b, per_b, per_b,
                                                small(rgn_g), small(rgn_b), small(tgn_g), small(tgn_b),
                                                small(ones_r), small(wout), small(ln1g),
                                                small(ln1b), small(wr_hi), small(wr_lo), small(br)]
    return pl.pallas_call(
        _out_proj_kernel,
        grid=(b, n // t),
        in_specs=in_specs,
        out_specs=[tok(d), pl.BlockSpec((t * TOKEN_SLAB, LANES), flat),
                   pl.BlockSpec((t, LANES), flat),
                   pl.BlockSpec((SUBLANES, t), lambda bi, i: (0, bi * tiles + i)),
                   pl.BlockSpec((SUBLANES, LANES), lambda bi, i: (0, 0))],
        out_shape=[jax.ShapeDtypeStruct((b, n, d), F32),
                   jax.ShapeDtypeStruct((b * n * TOKEN_SLAB, LANES), jnp.uint32),
                   jax.ShapeDtypeStruct((b * n, LANES), F32),
                   jax.ShapeDtypeStruct((SUBLANES, b * n), F32),
                   jax.ShapeDtypeStruct((SUBLANES, LANES), F32)],
        scratch_shapes=[pltpu.VMEM((SUBLANES, LANES), F32)],
        compiler_params=_cparams(("arbitrary", "arbitrary")),
        name="out_proj",
    )(*args)


ROUTE_E1, ROUTE_E2, ROUTE_G1, ROUTE_G2, ROUTE_RANK1, ROUTE_RANK2 = range(6)


def _lane_argmax(x, valid, lane):
    m = jnp.max(jnp.where(valid, x, -jnp.inf), axis=-1, keepdims=True)
    idx = jnp.min(jnp.where(valid & (x == m), lane, float(LANES)), axis=-1, keepdims=True)
    return m, idx


def _route_tile(lg, carry_ref):
    t = lg.shape[0]
    lane = lax.broadcasted_iota(jnp.int32, lg.shape, 1).astype(F32)
    gmask = lane < N_GROUPS
    gmax = jnp.max(jnp.where(gmask, lg, -jnp.inf), axis=-1, keepdims=True)
    gexp = jnp.where(gmask, jnp.exp(lg - gmax), 0.0)
    gp = gexp / jnp.sum(gexp, axis=-1, keepdims=True)
    g_w, g_i = _lane_argmax(gp, gmask, lane)

    lo = N_GROUPS + EXPERTS_PER_GROUP * g_i
    emask = (lane >= lo) & (lane < lo + EXPERTS_PER_GROUP)
    emax = jnp.max(jnp.where(emask, lg, -jnp.inf), axis=-1, keepdims=True)
    eexp = jnp.where(emask, jnp.exp(lg - emax), 0.0)
    ep = eexp / jnp.sum(eexp, axis=-1, keepdims=True)
    p1, i1 = _lane_argmax(ep, emask, lane)
    p2, i2 = _lane_argmax(ep, emask & (lane != i1), lane)
    denom = p1 + p2
    gate1 = g_w * p1 / denom
    gate2 = g_w * p2 / denom
    e1 = i1 - N_GROUPS
    e2 = i2 - N_GROUPS

    oh1 = (lane == e1).astype(F32)
    oh2 = (lane == e2).astype(F32)
    cnt = oh1 + oh2
    ri = lax.broadcasted_iota(jnp.int32, (t, t), 0)
    ci = lax.broadcasted_iota(jnp.int32, (t, t), 1)
    before = (ci < ri).astype(BF16)
    seen = jnp.dot(before, cnt.astype(BF16), preferred_element_type=F32) + carry_ref[0:1, :]
    rank1 = jnp.sum(oh1 * seen, axis=-1, keepdims=True)
    rank2 = jnp.sum(oh2 * seen, axis=-1, keepdims=True)
    carry_ref[0:1, :] = carry_ref[0:1, :] + jnp.sum(cnt, axis=0, keepdims=True)

    out = jnp.zeros(lg.shape, F32)
    for slot, val in ((ROUTE_E1, e1.astype(F32)), (ROUTE_E2, e2.astype(F32)), (ROUTE_G1, gate1),
                      (ROUTE_G2, gate2), (ROUTE_RANK1, rank1), (ROUTE_RANK2, rank2)):
        out = jnp.where(lane == slot, val, out)
    return out


def _tile_gather_copy(src_hbm, idx_ref, buf, sem, slot, r):
    src = src_hbm.at[pl.ds(pl.multiple_of(idx_ref[0, 0, r], TOKEN_SLAB), TOKEN_SLAB), :]
    dst = buf.at[slot, pl.ds(pl.multiple_of(r * TOKEN_SLAB, TOKEN_SLAB), TOKEN_SLAB), :]
    return pltpu.make_async_copy(src, dst, sem.at[slot])


def _start_tile_gather(src_hbm, idx_ref, buf, sem, slot, rows, priorities):
    def body(g, carry):
        for j in range(GATHER_UNROLL):
            copy = _tile_gather_copy(src_hbm, idx_ref, buf, sem, slot, g * GATHER_UNROLL + j)
            copy.start(priority=priorities[j % len(priorities)])
        return carry
    lax.fori_loop(0, rows // GATHER_UNROLL, body, 0)


def _wait_tile_gather(src_hbm, buf, sem, slot, rows):
    whole = src_hbm.at[pl.ds(0, rows * TOKEN_SLAB), :]
    pltpu.make_async_copy(whole, buf.at[slot], sem.at[slot]).wait()


def _dispatch_kernel(d1_ref, d2_ref, u_ref, x_init_hbm, x_hbm, sem):
    del x_init_hbm
    rows = d1_ref.shape[2]

    def slab(ref, first_row):
        return ref.at[pl.ds(pl.multiple_of(first_row, TOKEN_SLAB), TOKEN_SLAB), :]

    def body(g, carry):
        for j in range(GATHER_UNROLL):
            r = g * GATHER_UNROLL + j
            src = slab(u_ref, r * TOKEN_SLAB)
            pltpu.make_async_copy(src, slab(x_hbm, d1_ref[0, 0, r]), sem).start(priority=j % 2)
            pltpu.make_async_copy(src, slab(x_hbm, d2_ref[0, 0, r]), sem).start(priority=(j + 1) % 2)
        return carry

    lax.fori_loop(0, rows // GATHER_UNROLL, body, 0)
    whole = x_hbm.at[pl.ds(0, rows * TOKEN_SLAB), :]
    pltpu.make_async_copy(u_ref, whole, sem).wait()
    pltpu.make_async_copy(u_ref, whole, sem).wait()


def _dispatch(u2_slabs, dest1, dest2, n_slot):
    n = dest1.shape[0]
    t = DISPATCH_ROWS
    nt = n // t
    idx = pl.BlockSpec((1, 1, t), lambda i: (i, 0, 0), memory_space=pltpu.SMEM)
    any_spec = pl.BlockSpec(memory_space=pl.ANY)
    x_init = jnp.zeros((n_slot * TOKEN_SLAB, LANES), jnp.uint32)
    return pl.pallas_call(
        _dispatch_kernel,
        grid=(nt,),
        in_specs=[idx, idx, pl.BlockSpec((t * TOKEN_SLAB, LANES), lambda i: (i, 0)), any_spec],
        out_specs=any_spec,
        out_shape=jax.ShapeDtypeStruct(x_init.shape, x_init.dtype),
        scratch_shapes=[pltpu.SemaphoreType.DMA(())],
        input_output_aliases={3: 0},
        compiler_params=_cparams(("arbitrary",)),
        name="dispatch",
    )(dest1.reshape(nt, 1, t), dest2.reshape(nt, 1, t), u2_slabs, x_init)


def _expert_kernel(blk_expert_ref, n_used_ref, x_ref, wg_ref, wu_ref, wd_ref, y_ref, wg_s, wu_s, wd_s):
    i = pl.program_id(0)
    n_used = n_used_ref[0]

    @pl.when(i >= n_used)
    def _():
        y_ref[...] = jnp.zeros_like(y_ref)

    @pl.when(i < n_used)
    def _():
        @pl.when((i == 0) | (blk_expert_ref[i] != blk_expert_ref[jnp.maximum(i - 1, 0)]))
        def _():
            wg_s[...] = wg_ref[0].astype(BF16)
            wu_s[...] = wu_ref[0].astype(BF16)
            wd_s[...] = wd_ref[0].astype(BF16)

        x = _unpack_bf16_pairs(_load_token_slabs(x_ref, EXPERT_ROWS)).astype(BF16)
        hg = jnp.dot(x, wg_s[...], preferred_element_type=F32)
        hu = jnp.dot(x, wu_s[...], preferred_element_type=F32)
        act = (hg * _sigmoid(hg) * hu).astype(BF16)
        _store_token_slabs(y_ref, _pack_bf16_pairs(jnp.dot(act, wd_s[...], preferred_element_type=F32)))


def _expert_mlp(x_slabs, block_expert, n_used, w_gate, w_up, w_down):
    n_blk = block_expert.shape[0]
    d = w_gate.shape[1]
    hdim = w_gate.shape[2]
    rows = EXPERT_ROWS
    used = lambda i, nu: jnp.minimum(i, nu[0] - 1)
    weight = lambda i, be, nu: (be[used(i, nu)], 0, 0)
    grid_spec = pltpu.PrefetchScalarGridSpec(
        num_scalar_prefetch=2,
        grid=(n_blk,),
        in_specs=[pl.BlockSpec((rows * TOKEN_SLAB, LANES), lambda i, be, nu: (used(i, nu), 0)),
                  pl.BlockSpec((1, d, hdim), weight),
                  pl.BlockSpec((1, d, hdim), weight),
                  pl.BlockSpec((1, hdim, d), weight)],
        out_specs=pl.BlockSpec((rows * TOKEN_SLAB, LANES), lambda i, be, nu: (i, 0)),
        scratch_shapes=[pltpu.VMEM((d, hdim), BF16), pltpu.VMEM((d, hdim), BF16), pltpu.VMEM((hdim, d), BF16)],
    )
    return pl.pallas_call(
        _expert_kernel,
        grid_spec=grid_spec,
        out_shape=jax.ShapeDtypeStruct((n_blk * rows * TOKEN_SLAB, LANES), jnp.uint32),
        compiler_params=_cparams(("arbitrary",)),
        name="expert_mlp",
    )(block_expert, n_used, x_slabs, w_gate, w_up, w_down)


def _combine_kernel(d1_ref, d2_ref, d1n_ref, d2n_ref, y_hbm, route_ref, h1_ref, g2_ref, lng_ref, lnb_ref,
                    o_ref, abuf, bbuf, sem_a, sem_b):
    i = pl.program_id(0)
    n = pl.num_programs(0)
    slot = i % 2
    rows = o_ref.shape[0]

    @pl.when(i == 0)
    def _():
        _start_tile_gather(y_hbm, d1_ref, abuf, sem_a, 0, rows, GATHER_PRIORITIES)
        _start_tile_gather(y_hbm, d2_ref, bbuf, sem_b, 0, rows, GATHER_PRIORITIES)

    @pl.when(i + 1 < n)
    def _():
        _start_tile_gather(y_hbm, d1n_ref, abuf, sem_a, 1 - slot, rows, GATHER_PRIORITIES)
        _start_tile_gather(y_hbm, d2n_ref, bbuf, sem_b, 1 - slot, rows, GATHER_PRIORITIES)

    _wait_tile_gather(y_hbm, abuf, sem_a, slot, rows)
    _wait_tile_gather(y_hbm, bbuf, sem_b, slot, rows)
    route = route_ref[...]
    f = (_unpack_bf16_pairs(_load_token_slabs(abuf.at[slot], rows)) * route[:, ROUTE_G1:ROUTE_G1 + 1]
         + _unpack_bf16_pairs(_load_token_slabs(bbuf.at[slot], rows)) * route[:, ROUTE_G2:ROUTE_G2 + 1])
    o_ref[...] = _layer_norm(DEEPNORM_ALPHA * h1_ref[...] + g2_ref[0] * f, lng_ref[...], lnb_ref[...])


def _combine(y_tiles, dest1, dest2, route, h1, g2, ln_g, ln_b, tokens_per_batch):
    n, d = h1.shape
    t = COMBINE_ROWS
    nt = n // t
    per_b = tokens_per_batch // t
    d1 = dest1.reshape(nt, 1, t)
    d2 = dest2.reshape(nt, 1, t)
    cur = pl.BlockSpec((1, 1, t), lambda i: (i, 0, 0), memory_space=pltpu.SMEM)
    nxt = pl.BlockSpec((1, 1, t), lambda i: (jnp.minimum(i + 1, nt - 1), 0, 0), memory_space=pltpu.SMEM)
    small = lambda arr: pl.BlockSpec(arr.shape, lambda i: (0,) * arr.ndim)
    return pl.pallas_call(
        _combine_kernel,
        grid=(nt,),
        in_specs=[cur, cur, nxt, nxt, pl.BlockSpec(memory_space=pl.ANY),
                  pl.BlockSpec((t, LANES), lambda i: (i, 0)),
                  pl.BlockSpec((t, d), lambda i: (i, 0)),
                  pl.BlockSpec((1, 1, d), lambda i: (i // per_b, 0, 0)),
                  small(ln_g), small(ln_b)],
        out_specs=pl.BlockSpec((t, d), lambda i: (i, 0)),
        out_shape=jax.ShapeDtypeStruct((n, d), F32),
        scratch_shapes=[pltpu.VMEM((2, t * TOKEN_SLAB, LANES), jnp.uint32),
                        pltpu.VMEM((2, t * TOKEN_SLAB, LANES), jnp.uint32),
                        pltpu.SemaphoreType.DMA((2,)), pltpu.SemaphoreType.DMA((2,))],
        compiler_params=_cparams(("arbitrary",)),
        name="combine",
    )(d1, d2, d1, d2, y_tiles, route, h1, g2, ln_g, ln_b)


def _hi_lo(w):
    hi = w.astype(BF16)
    return jnp.stack([hi, (w - hi.astype(F32)).astype(BF16)])


def _block_diag2(w):
    z = jnp.zeros_like(w[0])
    return jnp.concatenate([jnp.concatenate([w[0], z], axis=1), jnp.concatenate([z, w[1]], axis=1)], axis=0)


def kernel(x, c, ctx, c_ctx, emb_ln_g, emb_ln_b, w_mod, b_mod, w_in, tshift_mu, rwkv_w0, rwkv_w2, rwkv_a0, rwkv_a2, rwkv_g2, rwkv_k_k, rwkv_k_a, rwkv_r_k, rwkv_gn_g, rwkv_gn_b, ret_decay, ret_gn_g, ret_gn_b, w_out, ln1_g, ln1_b, router_group, router_group_bias, router_expert, router_expert_bias, expert_w_gate, expert_w_up, expert_w_down, ln2_g, ln2_b):
    assert w_mod.shape[0] == 1, "written for DEPTH == 1 (context outputs are never emitted)"
    b, n_tok, d = x.shape
    n_ctx = ctx.shape[1]
    row = lambda v: v.reshape(1, -1)

    c_rows = jnp.zeros((SUBLANES, d), F32).at[:b].set(c).at[b].set(c_ctx)
    mod = _modulation(c_rows, w_mod[0], row(b_mod[0]))
    sh1, s1, g1, sh2, s2, g2 = [mod[:b, j * d:(j + 1) * d].reshape(b, 1, d) for j in range(6)]
    sh1c, s1c = [jnp.broadcast_to(mod[b, j * d:(j + 1) * d].reshape(1, 1, d), (b, 1, d)) for j in range(2)]

    w_in_bf16 = w_in[0].astype(BF16)
    pr, pt = _in_proj(x, row(emb_ln_g), row(emb_ln_b), s1, sh1, w_in_bf16)
    pr_c, pt_c = _in_proj(ctx, row(emb_ln_g), row(emb_ln_b), s1c, sh1c, w_in_bf16)

    prep_params = (row(tshift_mu[0]), row(rwkv_w0[0]), _hi_lo(_block_diag2(rwkv_w2[0])), row(rwkv_a0[0]),
                   _hi_lo(_block_diag2(rwkv_a2[0])), _hi_lo(rwkv_g2[0]), row(rwkv_k_k[0]), row(rwkv_k_a[0]),
                   row(rwkv_r_k[0]),
                   _segment_ones(D_RWKV, RWKV_HEAD))
    lat = _rwkv_prepare(pr, prep_params, grid_shift=True)
    cx = _rwkv_prepare(pr_c, prep_params, grid_shift=False)
    r_l, v_l, kk_l, w_l, kd_l, bb_l, gate_l, bonus_l = lat
    r_c, v_c, kk_c, w_c, kd_c, bb_c, _, _ = cx

    y_f, y_b = _wkv7((r_l, v_l, kk_l, w_l, kd_l, bb_l), (r_c, v_c, kk_c, w_c, kd_c, bb_c), b, n_tok, n_ctx)

    cos_t, sin_t = _rope_tables(n_tok)
    t_f, t_b = _retention(pt, pt_c, ret_decay[0], cos_t, sin_t)

    wr = jnp.zeros((d, LANES), F32).at[:, :N_GROUPS].set(router_group[0])
    wr = wr.at[:, N_GROUPS:N_GROUPS + N_EXPERTS].set(router_expert[0])
    br = jnp.zeros((1, LANES), F32).at[0, :N_GROUPS].set(router_group_bias[0])
    br = br.at[0, N_GROUPS:N_GROUPS + N_EXPERTS].set(router_expert_bias[0].reshape(-1))
    vecs = (row(emb_ln_g), row(emb_ln_b), g1, s2, sh2, row(rwkv_gn_g[0]), row(rwkv_gn_b[0]),
            row(ret_gn_g[0]), row(ret_gn_b[0]), row(ln1_g[0]), row(ln1_b[0]), br)
    wr_hi_lo = _hi_lo(wr)
    mats = (_segment_ones(D_RWKV, RWKV_HEAD), w_out[0].astype(BF16), wr_hi_lo[0], wr_hi_lo[1])
    h1, u2, route, route_t, counts = _out_proj(x, y_f, y_b, bonus_l, gate_l, t_f, t_b, pt, vecs, mats)

    n_all = b * n_tok

    e1 = route_t[ROUTE_E1].astype(jnp.int32)
    e2 = route_t[ROUTE_E2].astype(jnp.int32)
    cnt = counts[0, :N_EXPERTS].astype(jnp.int32)
    padded = ((cnt + EXPERT_ROWS - 1) // EXPERT_ROWS) * EXPERT_ROWS
    pends = jnp.cumsum(padded)
    pstarts = pends - padded
    expert_ids = jnp.arange(N_EXPERTS, dtype=jnp.int32)
    start_of = lambda e: jnp.sum(jnp.where(e[:, None] == expert_ids[None, :], pstarts[None, :], 0), axis=1)
    dest1 = (start_of(e1) + route_t[ROUTE_RANK1].astype(jnp.int32)) * TOKEN_SLAB
    dest2 = (start_of(e2) + route_t[ROUTE_RANK2].astype(jnp.int32)) * TOKEN_SLAB
    n_blk = -(-(n_all * 2) // EXPERT_ROWS) + N_EXPERTS
    block_start = jnp.arange(n_blk, dtype=jnp.int32) * EXPERT_ROWS
    block_expert = jnp.minimum(jnp.sum((block_start[:, None] >= pends[None, :]).astype(jnp.int32), axis=1),
                               N_EXPERTS - 1)

    n_used = (pends[N_EXPERTS - 1:] // EXPERT_ROWS).astype(jnp.int32)
    x_slabs = _dispatch(u2, dest1, dest2, n_blk * EXPERT_ROWS)
    y_tiles = _expert_mlp(x_slabs, block_expert, n_used, expert_w_gate[0], expert_w_up[0], expert_w_down[0])
    out = _combine(y_tiles, dest1, dest2, route, h1.reshape(n_all, d), g2, row(ln2_g[0]), row(ln2_b[0]), n_tok)
    return out.reshape(b, n_tok, d)
```

```python
import functools
import math

import jax
import jax.numpy as jnp
import numpy as np
from jax import lax
from jax.experimental import pallas as pl
from jax.experimental.pallas import tpu as pltpu

F32 = jnp.float32
BF16 = jnp.bfloat16
HIGHEST = lax.Precision.HIGHEST

GRID_W = 64
D_RWKV = 512
RWKV_HEAD = 64
RWKV_HEADS = D_RWKV // RWKV_HEAD
DECAY_LORA = 64
AAA_LORA = 64
GATE_LORA = 128
D_RET = 512
RET_HEADS = 4
RET_HEAD = D_RET // RET_HEADS
RET_CHUNK = 128
RET_CHUNKS_PER_STEP = 4
RWKV_COLS = 3 * D_RWKV + 2 * (DECAY_LORA + AAA_LORA) + GATE_LORA
RET_COLS = 4 * D_RET
N_GROUPS = 4
EXPERTS_PER_GROUP = 8
N_EXPERTS = N_GROUPS * EXPERTS_PER_GROUP
EXPERT_HIDDEN = 512
ROPE_BASE = 10000.0
LN_EPS = 1e-5
RWKV_GN_EPS = 64e-5
RET_GN_EPS = 1e-5
DEEPNORM_ALPHA = 2.0 ** 0.25
EXP_NEG_HALF = math.exp(-0.5)

LANES = 128
SUBLANES = 8
VMEM_LIMIT_BYTES = 56 * 1024 * 1024

PREPARE_ROWS = 256
MODULATION_COLS = 1536

WKV_CHUNK = 64
WKV_CHUNKS_PER_STEP = 2


IN_PROJ_ROWS = 512
OUT_PROJ_ROWS = 512

TOKEN_SLAB = 4
EXPERT_ROWS = 512
COMBINE_ROWS = 512
DISPATCH_ROWS = 512
GATHER_UNROLL = 16
GATHER_PRIORITIES = (0, 1)


def _pack_bf16_pairs(x):
    half = x.shape[1] // 2

    def bf16_bits(v):
        b = lax.bitcast_convert_type(v, jnp.uint32)
        return (b + jnp.uint32(0x7FFF) + ((b >> 16) & jnp.uint32(1))) >> 16

    return bf16_bits(x[:, :half]) | (bf16_bits(x[:, half:]) << 16)


def _unpack_bf16_pairs(p):
    lo = lax.bitcast_convert_type(p << 16, F32)
    hi = lax.bitcast_convert_type(p & jnp.uint32(0xFFFF0000), F32)
    return jnp.concatenate([lo, hi], axis=-1)


def _store_token_slabs(ref, x):
    rows = x.shape[0]
    for j in range(TOKEN_SLAB):
        ref[pl.ds(j, rows, stride=TOKEN_SLAB), :] = x[:, j * LANES:(j + 1) * LANES]


def _load_token_slabs(ref, rows):
    return jnp.concatenate([ref[pl.ds(j, rows, stride=TOKEN_SLAB), :] for j in range(TOKEN_SLAB)], axis=-1)


def _cparams(sem):
    return pltpu.CompilerParams(dimension_semantics=sem, vmem_limit_bytes=VMEM_LIMIT_BYTES)


def _layer_norm(x, g, b, eps=LN_EPS):
    mu = jnp.mean(x, axis=-1, keepdims=True)
    xc = x - mu
    var = jnp.mean(xc * xc, axis=-1, keepdims=True)
    return xc * lax.rsqrt(var + eps) * g + b


def _sigmoid(x):
    return 1.0 / (1.0 + jnp.exp(-x))


def _split_bf16(x):
    hi = x.astype(BF16)
    return hi, (x - hi.astype(F32)).astype(BF16)


def _segsum(x, ones_bf16):
    t = x.shape[0]
    s = jnp.dot(jnp.concatenate(_split_bf16(x), axis=0), ones_bf16, preferred_element_type=F32)
    return s[:t] + s[t:]


def _dot_split(x, w_hi, w_lo):
    hi, lo = _split_bf16(x)
    acc = jnp.dot(hi, w_hi, preferred_element_type=F32)
    acc = acc + jnp.dot(lo, w_hi, preferred_element_type=F32)
    return acc + jnp.dot(hi, w_lo, preferred_element_type=F32)


def _segment_ones(width, seg):
    idx = np.arange(width) // seg
    return jnp.asarray(idx[:, None] == idx[None, :], dtype=BF16)


def _mod_kernel(c_ref, w_ref, b_ref, o_ref):
    c = c_ref[...]
    sc = c * _sigmoid(c)
    o_ref[...] = jnp.dot(sc, w_ref[...], precision=HIGHEST, preferred_element_type=F32) + b_ref[...]


def _modulation(c_rows, w_mod, b_mod):
    rows, d = c_rows.shape
    n = w_mod.shape[1]
    tn = MODULATION_COLS
    return pl.pallas_call(
        _mod_kernel,
        grid=(n // tn,),
        in_specs=[pl.BlockSpec((rows, d), lambda j: (0, 0)),
                  pl.BlockSpec((d, tn), lambda j: (0, j)),
                  pl.BlockSpec((1, tn), lambda j: (0, j))],
        out_specs=pl.BlockSpec((rows, tn), lambda j: (0, j)),
        out_shape=jax.ShapeDtypeStruct((rows, n), F32),
        compiler_params=_cparams(("arbitrary",)),
        name="modulation",
    )(c_rows, w_mod, b_mod)


def _in_proj_kernel(x_ref, g_ref, b_ref, s_ref, sh_ref, w_ref, pr_ref, pt_ref):
    h = _layer_norm(x_ref[0], g_ref[...], b_ref[...])
    u = h * (1.0 + s_ref[0]) + sh_ref[0]
    p = jnp.dot(u.astype(BF16), w_ref[...], preferred_element_type=F32)
    pr_ref[0] = p[:, :RWKV_COLS]
    pt_ref[0] = p[:, RWKV_COLS:]


def _in_proj(x, ln_g, ln_b, s1, sh1, w_in_bf16):
    b, n, d = x.shape
    tm = min(IN_PROJ_ROWS, n)
    cols = w_in_bf16.shape[1]
    return pl.pallas_call(
        _in_proj_kernel,
        grid=(b, n // tm),
        in_specs=[pl.BlockSpec((1, tm, d), lambda bi, i: (bi, i, 0)),
                  pl.BlockSpec((1, d), lambda bi, i: (0, 0)),
                  pl.BlockSpec((1, d), lambda bi, i: (0, 0)),
                  pl.BlockSpec((1, 1, d), lambda bi, i: (bi, 0, 0)),
                  pl.BlockSpec((1, 1, d), lambda bi, i: (bi, 0, 0)),
                  pl.BlockSpec((d, cols), lambda bi, i: (0, 0))],
        out_specs=[pl.BlockSpec((1, tm, RWKV_COLS), lambda bi, i: (bi, i, 0)),
                   pl.BlockSpec((1, tm, RET_COLS), lambda bi, i: (bi, i, 0))],
        out_shape=[jax.ShapeDtypeStruct((b, n, RWKV_COLS), F32),
                   jax.ShapeDtypeStruct((b, n, RET_COLS), F32)],
        compiler_params=_cparams(("arbitrary", "arbitrary")),
        name="in_proj",
    )(x, ln_g, ln_b, s1, sh1, w_in_bf16)


def _rwkv_prepare_kernel(cur_ref, prev_ref, next_ref, mu_ref, w0_ref, w2_ref, a0_ref, a2_ref, g2_ref,
                         kk_scale_ref, ka_ref, rk_ref, ones_ref,
                         r_ref, v_ref, kk_ref, w_ref, kd_ref, bb_ref, g_ref, bonus_ref,
                         *, grid_shift, n_tok):
    cur = cur_ref[0]
    t, c = cur.shape
    row = lax.broadcasted_iota(jnp.int32, (t, c), 0)
    lane = lax.broadcasted_iota(jnp.int32, (t, c), 1)
    prev_tok = pltpu.roll(cur, 1, 0)
    next_tok = pltpu.roll(cur, t - 1, 0)
    if grid_shift:
        col = row & (GRID_W - 1)
        tok = row + pl.program_id(1) * t
        left = jnp.where(col > 0, prev_tok, 0.0)
        right = jnp.where(col < GRID_W - 1, next_tok, 0.0)
        up = jnp.where(tok >= GRID_W, jnp.concatenate([prev_ref[0], cur[:t - GRID_W]], axis=0), 0.0)
        down = jnp.where(tok < n_tok - GRID_W, jnp.concatenate([cur[GRID_W:], next_ref[0]], axis=0), 0.0)
        cm = lane & 3
        shifted = jnp.where(cm == 0, left, jnp.where(cm == 1, right, jnp.where(cm == 2, up, down)))
    else:
        prev_tok = jnp.where(row > 0, prev_tok, 0.0)
        next_tok = jnp.where(row < t - 1, next_tok, 0.0)
        shifted = jnp.where((lane & 1) == 0, prev_tok, next_tok)
    pm = cur + mu_ref[...] * (shifted - cur)

    r = pm[:, 0:D_RWKV]
    k = pm[:, D_RWKV:2 * D_RWKV]
    v = pm[:, 2 * D_RWKV:3 * D_RWKV]
    o = 3 * D_RWKV
    lw = pm[:, o:o + 2 * DECAY_LORA]
    la = pm[:, o + 2 * DECAY_LORA:o + 2 * (DECAY_LORA + AAA_LORA)]
    lg = pm[:, o + 2 * (DECAY_LORA + AAA_LORA):]

    w = w0_ref[...] + _dot_split(jnp.tanh(lw), w2_ref[0], w2_ref[1])
    log_decay = -EXP_NEG_HALF * _sigmoid(w)
    a = _sigmoid(a0_ref[...] + _dot_split(la, a2_ref[0], a2_ref[1]))
    gate = _dot_split(_sigmoid(lg), g2_ref[0], g2_ref[1])

    ones = ones_ref[...]
    kk_raw = k * kk_scale_ref[...]
    kk = kk_raw / jnp.maximum(jnp.sqrt(_segsum(kk_raw * kk_raw, ones)), 1e-12)
    ka = ka_ref[...]
    a0 = a[:, :D_RWKV]
    a1 = a[:, D_RWKV:]
    kd0 = k * (1.0 + (a0 - 1.0) * ka)
    kd1 = k * (1.0 + (a1 - 1.0) * ka)
    bonus = _segsum(r * (kd0 + kd1) * rk_ref[...], ones) * v

    r_ref[0] = r
    v_ref[0] = v
    kk_ref[0] = kk
    w_ref[0] = log_decay
    kd_ref[0] = jnp.concatenate([kd0, kd1], axis=-1)
    bb_ref[0] = jnp.concatenate([kk * a0, kk * a1], axis=-1)
    g_ref[0] = gate
    bonus_ref[0] = bonus


def _rwkv_prepare(pr, params, grid_shift):
    b, n, c = pr.shape
    t = PREPARE_ROWS
    if not grid_shift:
        assert n == t, "sequence token shift is written for a single tile"
    halo_blocks = n // GRID_W
    per_tile = t // GRID_W
    small = lambda shape: pl.BlockSpec(shape, lambda bi, i: (0,) * len(shape))
    tok_spec = lambda width: pl.BlockSpec((1, t, width), lambda bi, i: (bi, i, 0))
    out_widths = (D_RWKV, D_RWKV, D_RWKV, 2 * D_RWKV, 2 * D_RWKV, 2 * D_RWKV, D_RWKV, D_RWKV)
    kernel = functools.partial(_rwkv_prepare_kernel, grid_shift=grid_shift, n_tok=n)
    return pl.pallas_call(
        kernel,
        grid=(b, n // t),
        in_specs=[tok_spec(c),
                  pl.BlockSpec((1, GRID_W, c), lambda bi, i: (bi, jnp.maximum(i * per_tile - 1, 0), 0)),
                  pl.BlockSpec((1, GRID_W, c),
                               lambda bi, i: (bi, jnp.minimum((i + 1) * per_tile, halo_blocks - 1), 0)),
                  small((1, c)), small((1, 2 * D_RWKV)), small((2, 2 * DECAY_LORA, 2 * D_RWKV)),
                  small((1, 2 * D_RWKV)), small((2, 2 * AAA_LORA, 2 * D_RWKV)), small((2, GATE_LORA, D_RWKV)),
                  small((1, D_RWKV)), small((1, D_RWKV)), small((1, D_RWKV)), small((D_RWKV, D_RWKV))],
        out_specs=[tok_spec(wd) for wd in out_widths],
        out_shape=[jax.ShapeDtypeStruct((b, n, wd), F32) for wd in out_widths],
        compiler_params=_cparams(("arbitrary", "arbitrary")),
        name="rwkv_prepare",
    )(pr, pr, pr, *params)


def _bdot(a, b):
    return jnp.dot(a.astype(BF16), b.astype(BF16), preferred_element_type=F32)


def _bdot_nt(a, b):
    return lax.dot_general(a.astype(BF16), b.astype(BF16), (((1,), (1,)), ((), ())), preferred_element_type=F32)


def _bdot_tn(a, b):
    return lax.dot_general(a.astype(BF16), b.astype(BF16), (((0,), (0,)), ((), ())), preferred_element_type=F32)


def _wkv7_chunk_kernel(*refs, n_ctx_chunks):
    c = WKV_CHUNK
    p = 2 * c
    n_in = 12
    in_refs = (refs[:n_in], refs[n_in:2 * n_in])
    y_refs = refs[2 * n_in:2 * n_in + 2]
    state_ref = refs[2 * n_in + 2]
    n = pl.program_id(0)
    n_batch = in_refs[0][0].shape[0]
    pairs_per_batch = RWKV_HEADS // 2
    pairs_per_dir = n_batch * pairs_per_batch

    @pl.when(n == 0)
    def _():
        state_ref[...] = jnp.zeros_like(state_ref)

    is_ctx = n < n_ctx_chunks
    ti = lax.broadcasted_iota(jnp.int32, (c, c), 0)
    tj = lax.broadcasted_iota(jnp.int32, (c, c), 1)
    ri = lax.broadcasted_iota(jnp.int32, (p, p), 0)
    ci = lax.broadcasted_iota(jnp.int32, (p, p), 1)
    same_head = (ri >= c) == (ci >= c)
    ii = ri & (c - 1)
    jj = ci & (c - 1)
    eye = (ri == ci).astype(F32)
    first = lax.broadcasted_iota(jnp.int32, (c, p), 1) < RWKV_HEAD

    def stack(x):
        return jnp.concatenate([jnp.where(first, x, 0.0), jnp.where(first, 0.0, x)], axis=0)

    def unstack(x):
        return x[:c] + x[c:]

    n_sub = WKV_CHUNKS_PER_STEP

    def rows(d, sub):
        first_row = sub * c if d == 0 else (n_sub - 1 - sub) * c
        return slice(first_row, first_row + c)

    a_st, r_st, k_st, b_st, k2_st, b2_st, v_st, g_chunk, earlier, upto_self = ([] for _ in range(10))
    for d, sub in [(d, sub) for d in range(2) for sub in range(n_sub)]:
        r_l, v_l, kk_l, lw_l, kd_l, bb_l, r_c, v_c, kk_c, lw_c, kd_c, bb_c = in_refs[d]
        rs = rows(d, sub)
        pick = lambda xc, xl: jnp.concatenate(
            [jnp.where(is_ctx, xc[bi, rs, :], xl[bi, rs, :]) for bi in range(n_batch)], axis=-1)
        r, v, kk, lw, kd, bb = (pick(r_c, r_l), pick(v_c, v_l), pick(kk_c, kk_l), pick(lw_c, lw_l),
                                pick(kd_c, kd_l), pick(bb_c, bb_l))
        before = (tj < ti) if d == 0 else (tj > ti)
        upto = (before | (ti == tj)).astype(BF16)
        hi = lw.astype(BF16)
        r1 = lw - hi.astype(F32)
        mid = r1.astype(BF16)
        lo = (r1 - mid.astype(F32)).astype(BF16)
        cum = (jnp.dot(upto, hi, preferred_element_type=F32) + jnp.dot(upto, mid, preferred_element_type=F32)
               + jnp.dot(upto, lo, preferred_element_type=F32))
        tot = jnp.sum(lw, axis=0, keepdims=True)
        e_neg = jnp.exp(-cum)
        e_rem = jnp.exp(tot - cum)
        alpha = kk * jnp.exp(cum - lw)
        rho = r * jnp.exp(cum)
        beta = bb * e_neg
        kappa = kd * e_neg
        kappa_rem = kd * e_rem
        beta_rem = bb * e_rem
        g_all = jnp.exp(tot)
        pair_before = same_head & ((jj < ii) if d == 0 else (jj > ii))
        pair_upto = pair_before | (ri == ci)
        for hp in range(pairs_per_dir):
            sl = slice(hp * p, (hp + 1) * p)
            a_st.append(stack(alpha[:, sl]))
            r_st.append(stack(rho[:, sl]))
            k_st.append(stack(kappa[:, sl]))
            b_st.append(stack(beta[:, sl]))
            k2_st.append(stack(kappa_rem[:, sl]))
            b2_st.append(stack(beta_rem[:, sl]))
            v_st.append(stack(v[:, sl]))
            g_chunk.append(g_all[:, sl])
            earlier.append(pair_before)
            upto_self.append(pair_upto)

    pairs = range(2 * n_sub * pairs_per_dir)
    g = [_bdot_nt(jnp.concatenate([a_st[h], r_st[h]], axis=0), jnp.concatenate([k_st[h], b_st[h]], axis=0))
         for h in pairs]
    m1 = [jnp.where(earlier[h], g[h][:p, :p], 0.0) for h in pairs]
    m2 = [jnp.where(earlier[h], g[h][:p, p:], 0.0) for h in pairs]
    n1 = [jnp.where(upto_self[h], g[h][p:, :p], 0.0) for h in pairs]
    n2 = [jnp.where(upto_self[h], g[h][p:, p:], 0.0) for h in pairs]

    in_block = (ii >> 3) == (jj >> 3)
    pw = [-jnp.where(in_block, m2[h], 0.0) for h in pairs]
    inv = [eye + pw[h] for h in pairs]
    pw = [_bdot(pw[h], pw[h]) for h in pairs]
    both = [_bdot(jnp.concatenate([inv[h], pw[h]], axis=0), pw[h]) for h in pairs]
    inv = [inv[h] + both[h][:p] for h in pairs]
    inv = [inv[h] + _bdot(inv[h], both[h][p:]) for h in pairs]
    for sh in (3, 4, 5):
        off = ((ii >> (sh + 1)) == (jj >> (sh + 1))) & ((ii >> sh) != (jj >> sh))
        left = [_bdot(inv[h], jnp.where(off, m2[h], 0.0)) for h in pairs]
        inv = [inv[h] - _bdot(left[h], inv[h]) for h in pairs]

    mnv = [_bdot(jnp.concatenate([m1[h], n1[h]], axis=0), v_st[h]) for h in pairs]
    m1v = [mnv[h][:p] for h in pairs]
    n1v = [mnv[h][p:] for h in pairs]
    au = [_bdot(inv[h], jnp.concatenate([a_st[h], m1v[h]], axis=1)) for h in pairs]
    nn = [_bdot(n2[h], au[h]) for h in pairs]
    pc = [_bdot_tn(b2_st[h], au[h][:, :p]) for h in pairs]
    qc_t = [_bdot_tn(jnp.concatenate([v_st[h], -au[h][:, p:]], axis=0),
                     jnp.concatenate([k2_st[h], b2_st[h]], axis=0)) for h in pairs]
    rho_t = [unstack(r_st[h] - nn[h][:, :p]) for h in pairs]
    y_t = [unstack(n1v[h] - nn[h][:, p:]) for h in pairs]
    chains = [(d, hp) for d in range(2) for hp in range(pairs_per_dir)]
    state = [state_ref[ch] for ch in range(len(chains))]
    for sub in range(n_sub):
        item = lambda d, hp: (d * n_sub + sub) * pairs_per_dir + hp
        y = [_bdot_nt(rho_t[item(d, hp)], state[ch]) + y_t[item(d, hp)] for ch, (d, hp) in enumerate(chains)]
        s_dec = [_bdot_nt(state[ch], pc[item(d, hp)]) for ch, (d, hp) in enumerate(chains)]
        for ch, (d, hp) in enumerate(chains):
            h = item(d, hp)
            bi, hpb = divmod(hp, pairs_per_batch)
            y_refs[d][bi, rows(d, sub), hpb * p:(hpb + 1) * p] = y[ch]
            state[ch] = state[ch] * g_chunk[h] - s_dec[ch] + qc_t[h]
    for ch in range(len(chains)):
        state_ref[ch] = state[ch]


def _wkv7(lat, ctx, b, n_tok, n_ctx):
    c = WKV_CHUNKS_PER_STEP * WKV_CHUNK
    ncx = n_ctx // c
    nl = n_tok // c
    assert ncx * c == n_ctx and nl * c == n_tok
    lat_idx = (lambda n: jnp.maximum(n - ncx, 0), lambda n: nl - 1 - jnp.maximum(n - ncx, 0))
    ctx_idx = (lambda n: jnp.minimum(n, ncx - 1), lambda n: ncx - 1 - jnp.minimum(n, ncx - 1))

    def specs(idx, d):
        shared = pl.BlockSpec((b, c, D_RWKV), lambda n: (0, idx(n), 0))
        per_dir = pl.BlockSpec((b, c, D_RWKV), lambda n: (0, idx(n), d))
        return [shared, shared, shared, per_dir, per_dir, per_dir]

    in_specs, args = [], []
    for d in range(2):
        in_specs += specs(lat_idx[d], d) + specs(ctx_idx[d], d)
        args += list(lat) + list(ctx)
    return pl.pallas_call(
        functools.partial(_wkv7_chunk_kernel, n_ctx_chunks=ncx),
        grid=(ncx + nl,),
        in_specs=in_specs,
        out_specs=[pl.BlockSpec((b, c, D_RWKV), lambda n, d=d: (0, lat_idx[d](n), 0)) for d in range(2)],
        out_shape=[jax.ShapeDtypeStruct((b, n_tok, D_RWKV), F32)] * 2,
        scratch_shapes=[pltpu.VMEM((2 * b * RWKV_HEADS // 2, 2 * RWKV_HEAD, 2 * RWKV_HEAD), F32)],
        compiler_params=_cparams(("arbitrary",)),
        name="wkv7_chunk",
    )(*args)


def _rope(z, cos_t, sin_t):
    lane = lax.broadcasted_iota(jnp.int32, z.shape, 1)
    half = RET_HEAD // 4
    partner = jnp.where((lane & (2 * half - 1)) < half, pltpu.roll(z, RET_HEAD - half, 1), pltpu.roll(z, half, 1))
    return z * cos_t + partner * sin_t


def _retention_kernel(dec_ref, fwd_ref, bwd_ref, ctx_ref, cosf_ref, sinf_ref, cosb_ref, sinb_ref,
                      yf_ref, yb_ref, state_ref, dmat_ref, tail_ref, head_ref, cdec_ref):
    c = RET_CHUNK
    scale = RET_HEAD ** -0.5
    ii = lax.broadcasted_iota(jnp.int32, (c, c), 0)
    jj = lax.broadcasted_iota(jnp.int32, (c, c), 1)
    pos = lax.broadcasted_iota(jnp.int32, (c, RET_HEAD), 0).astype(F32)
    n_ctx_chunks = ctx_ref.shape[1] // c

    def head_slices(ref_val, h):
        q = ref_val[:, h * RET_HEAD:(h + 1) * RET_HEAD]
        k = ref_val[:, D_RET + h * RET_HEAD:D_RET + (h + 1) * RET_HEAD]
        v = ref_val[:, 2 * D_RET + h * RET_HEAD:2 * D_RET + (h + 1) * RET_HEAD]
        return q, k, v

    n_batch = fwd_ref.shape[0]
    heads = [(d, h) for d in range(2) for h in range(RET_HEADS)]
    chains = [(bi, d, h) for bi in range(n_batch) for d, h in heads]

    @pl.when(pl.program_id(0) == 0)
    def _():
        for d, h in heads:
            x = jnp.full((1, RET_HEAD), dec_ref[d, h], F32)
            lg = -(jnp.maximum(x, 0.0) + jnp.log(1.0 + jnp.exp(-jnp.abs(x))))
            chunk_decay = jnp.exp(lg * float(c))
            tail = jnp.exp(lg * ((c - 1.0 - pos) if d == 0 else pos))
            rel = (ii - jj) if d == 0 else (jj - ii)
            mask = (rel >= 0) if d == 0 else (rel > 0)
            dmat_ref[d, h] = jnp.where(mask, jnp.exp(lg * jnp.maximum(rel, 0).astype(F32)), 0.0)
            tail_ref[d, h] = tail
            head_ref[d, h] = jnp.exp(lg * ((pos + 1.0) if d == 0 else (c - pos)))
            cdec_ref[d, h] = jnp.broadcast_to(chunk_decay, (SUBLANES, RET_HEAD))
            order = range(n_ctx_chunks) if d == 0 else range(n_ctx_chunks - 1, -1, -1)
            for bi in range(n_batch):
                s = jnp.zeros((RET_HEAD, RET_HEAD), F32)
                for cc in order:
                    _, kc, vc = head_slices(ctx_ref[bi, cc * c:(cc + 1) * c, :], h)
                    s = s * chunk_decay + _bdot_tn(kc * scale * tail, vc)
                state_ref[bi, d, h] = s

    n_sub = RET_CHUNKS_PER_STEP

    def rows(d, sub):
        first = sub * c if d == 0 else (n_sub - 1 - sub) * c
        return slice(first, first + c)

    work = [(bi, d, h, sub) for sub in range(n_sub) for bi, d, h in chains]
    qkv = {}
    for bi, d, h, sub in work:
        blk = (fwd_ref if d == 0 else bwd_ref)[bi, rows(d, sub), :]
        cos_t = (cosf_ref if d == 0 else cosb_ref)[rows(d, sub), :]
        sin_t = (sinf_ref if d == 0 else sinb_ref)[rows(d, sub), :]
        q, k, v = head_slices(blk, h)
        qkv[bi, d, h, sub] = (_rope(q, cos_t, sin_t), _rope(k, cos_t, sin_t) * scale, v.astype(BF16))
    scores = {w: _bdot_nt(qkv[w][0], qkv[w][1]) for w in work}
    inner = {w: _bdot(scores[w] * dmat_ref[w[1], w[2]], qkv[w][2]) for w in work}
    upd = {w: _bdot_tn(qkv[w][1] * tail_ref[w[1], w[2]], qkv[w][2]) for w in work}
    state = {ch: state_ref[ch] for ch in chains}
    for sub in range(n_sub):
        cross = {ch: _bdot(qkv[ch + (sub,)][0] * head_ref[ch[1], ch[2]], state[ch]) for ch in chains}
        for bi, d, h in chains:
            ch = (bi, d, h)
            state[ch] = state[ch] * cdec_ref[d, h, 0:1, :] + upd[ch + (sub,)]
            out_ref = yf_ref if d == 0 else yb_ref
            out_ref[bi, rows(d, sub), h * RET_HEAD:(h + 1) * RET_HEAD] = inner[ch + (sub,)] + cross[ch]
    for ch in chains:
        state_ref[ch] = state[ch]


def _retention(pt, pt_ctx, ret_decay, cos_t, sin_t):
    b, n, _ = pt.shape
    c = RET_CHUNK
    rows = RET_CHUNKS_PER_STEP * c
    steps = n // rows
    qkv = 3 * D_RET
    fwd = lambda i: (0, i, 0)
    bwd = lambda i: (0, steps - 1 - i, 0)
    return pl.pallas_call(
        _retention_kernel,
        grid=(steps,),
        in_specs=[pl.BlockSpec(memory_space=pltpu.SMEM),
                  pl.BlockSpec((b, rows, qkv), fwd),
                  pl.BlockSpec((b, rows, qkv), bwd),
                  pl.BlockSpec((b, pt_ctx.shape[1], qkv), lambda i: (0, 0, 0)),
                  pl.BlockSpec((rows, RET_HEAD), lambda i: (i, 0)),
                  pl.BlockSpec((rows, RET_HEAD), lambda i: (i, 0)),
                  pl.BlockSpec((rows, RET_HEAD), lambda i: (steps - 1 - i, 0)),
                  pl.BlockSpec((rows, RET_HEAD), lambda i: (steps - 1 - i, 0))],
        out_specs=[pl.BlockSpec((b, rows, D_RET), fwd), pl.BlockSpec((b, rows, D_RET), bwd)],
        out_shape=[jax.ShapeDtypeStruct((b, n, D_RET), F32), jax.ShapeDtypeStruct((b, n, D_RET), F32)],
        scratch_shapes=[pltpu.VMEM((b, 2, RET_HEADS, RET_HEAD, RET_HEAD), F32),
                        pltpu.VMEM((2, RET_HEADS, c, c), F32),
                        pltpu.VMEM((2, RET_HEADS, c, RET_HEAD), F32),
                        pltpu.VMEM((2, RET_HEADS, c, RET_HEAD), F32),
                        pltpu.VMEM((2, RET_HEADS, SUBLANES, RET_HEAD), F32)],
        compiler_params=_cparams(("arbitrary",)),
        name="retention",
    )(ret_decay, pt, pt, pt_ctx, cos_t, sin_t, cos_t, sin_t)


def _rope_tables(n_tok):
    nf = RET_HEAD // 4
    lane = np.arange(RET_HEAD)
    inv = ROPE_BASE ** (-jnp.arange(nf, dtype=F32) / nf)
    t = jnp.arange(n_tok)
    pos = jnp.where((lane // (2 * nf) == 0)[None, :], (t // GRID_W)[:, None], (t % GRID_W)[:, None]).astype(F32)
    ang = pos * inv[lane % nf][None, :]
    sign = jnp.where((lane % (2 * nf)) < nf, -1.0, 1.0).astype(F32)
    return jnp.cos(ang), jnp.sin(ang) * sign[None, :]


def _group_norm(y, ones, seg, eps, g, b):
    mu = _segsum(y, ones) * (1.0 / seg)
    yc = y - mu
    var = _segsum(yc * yc, ones) * (1.0 / seg)
    return yc * lax.rsqrt(var + eps) * g + b


def _out_proj_kernel(x_ref, yf_ref, yb_ref, bonus_ref, gate_ref, tf_ref, tb_ref, gt_ref,
                     embg_ref, embb_ref, g1_ref, s2_ref, sh2_ref, rgn_g_ref, rgn_b_ref, tgn_g_ref, tgn_b_ref,
                     ones_r_ref, wout_ref, ln1g_ref, ln1b_ref, wrh_ref, wrl_ref, br_ref,
                     h1_ref, u2_ref, route_ref, route_t_ref, count_ref, carry_ref):
    @pl.when((pl.program_id(0) == 0) & (pl.program_id(1) == 0))
    def _():
        carry_ref[...] = jnp.zeros_like(carry_ref)

    y = yf_ref[0] + yb_ref[0]
    o_rwkv = _group_norm(y, ones_r_ref[...], RWKV_HEAD, RWKV_GN_EPS, rgn_g_ref[...], rgn_b_ref[...])
    o_rwkv = (o_rwkv + bonus_ref[0]) * gate_ref[0]
    yt = tf_ref[0] + tb_ref[0]
    gt = gt_ref[0]
    tgn_g = tgn_g_ref[...]
    tgn_b = tgn_b_ref[...]
    o_ret = jnp.concatenate(
        [_layer_norm(yt[:, h * RET_HEAD:(h + 1) * RET_HEAD], tgn_g[:, h * RET_HEAD:(h + 1) * RET_HEAD],
                     tgn_b[:, h * RET_HEAD:(h + 1) * RET_HEAD], RET_GN_EPS) for h in range(RET_HEADS)], axis=-1)
    o_ret = o_ret * (gt * _sigmoid(gt))
    cat = jnp.concatenate([o_rwkv, o_ret], axis=-1).astype(BF16)
    mix = jnp.dot(cat, wout_ref[...], preferred_element_type=F32)
    h = _layer_norm(x_ref[0], embg_ref[...], embb_ref[...])
    h1 = _layer_norm(DEEPNORM_ALPHA * h + g1_ref[0] * mix, ln1g_ref[...], ln1b_ref[...])
    u2 = h1 * (1.0 + s2_ref[0]) + sh2_ref[0]
    h1_ref[0] = h1
    _store_token_slabs(u2_ref, _pack_bf16_pairs(u2))
    route = _route_tile(_dot_split(u2, wrh_ref[...], wrl_ref[...]) + br_ref[...], carry_ref)
    route_ref[...] = route
    route_t_ref[...] = route.T[:SUBLANES]
    count_ref[...] = carry_ref[...]


def _out_proj(x, y_f, y_b, bonus, gate, t_f, t_b, pt, vecs, mats):
    b, n, d = x.shape
    t = OUT_PROJ_ROWS
    tok = lambda width: pl.BlockSpec((1, t, width), lambda bi, i: (bi, i, 0))
    per_b = pl.BlockSpec((1, 1, d), lambda bi, i: (bi, 0, 0))
    small = lambda arr: pl.BlockSpec(arr.shape, lambda bi, i: (0,) * arr.ndim)
    (embg, embb, g1, s2, sh2, rgn_g, rgn_b, tgn_g, tgn_b, ln1g, ln1b, br) = vecs
    (ones_r, wout, wr_hi, wr_lo) = mats
    gt_spec = pl.BlockSpec((1, t, D_RET), lambda bi, i: (bi, i, 3))
    tiles = n // t
    flat = lambda bi, i: (bi * tiles + i, 0)
    args = (x, y_f, y_b, bonus, gate, t_f, t_b, pt, embg, embb, g1, s2, sh2, rgn_g, rgn_b, tgn_g, tgn_b,
            ones_r, wout, ln1g, ln1b, wr_hi, wr_lo, br)
    in_specs = [tok(d)] + [tok(D_RWKV)] * 6 + [gt_spec, small(embg), small(embb), per_b, per_b, per_b,
                                                small(rgn_g), small(rgn_b), small(tgn_g), small(tgn_b),
                                                small(ones_r), small(wout), small(ln1g),
                                                small(ln1b), small(wr_hi), small(wr_lo), small(br)]
    return pl.pallas_call(
        _out_proj_kernel,
        grid=(b, n // t),
        in_specs=in_specs,
        out_specs=[tok(d), pl.BlockSpec((t * TOKEN_SLAB, LANES), flat),
                   pl.BlockSpec((t, LANES), flat),
                   pl.BlockSpec((SUBLANES, t), lambda bi, i: (0, bi * tiles + i)),
                   pl.BlockSpec((SUBLANES, LANES), lambda bi, i: (0, 0))],
        out_shape=[jax.ShapeDtypeStruct((b, n, d), F32),
                   jax.ShapeDtypeStruct((b * n * TOKEN_SLAB, LANES), jnp.uint32),
                   jax.ShapeDtypeStruct((b * n, LANES), F32),
                   jax.ShapeDtypeStruct((SUBLANES, b * n), F32),
                   jax.ShapeDtypeStruct((SUBLANES, LANES), F32)],
        scratch_shapes=[pltpu.VMEM((SUBLANES, LANES), F32)],
        compiler_params=_cparams(("arbitrary", "arbitrary")),
        name="out_proj",
    )(*args)


ROUTE_E1, ROUTE_E2, ROUTE_G1, ROUTE_G2, ROUTE_RANK1, ROUTE_RANK2 = range(6)


def _lane_argmax(x, valid, lane):
    m = jnp.max(jnp.where(valid, x, -jnp.inf), axis=-1, keepdims=True)
    idx = jnp.min(jnp.where(valid & (x == m), lane, float(LANES)), axis=-1, keepdims=True)
    return m, idx


def _route_tile(lg, carry_ref):
    t = lg.shape[0]
    lane = lax.broadcasted_iota(jnp.int32, lg.shape, 1).astype(F32)
    gmask = lane < N_GROUPS
    gmax = jnp.max(jnp.where(gmask, lg, -jnp.inf), axis=-1, keepdims=True)
    gexp = jnp.where(gmask, jnp.exp(lg - gmax), 0.0)
    gp = gexp / jnp.sum(gexp, axis=-1, keepdims=True)
    g_w, g_i = _lane_argmax(gp, gmask, lane)

    lo = N_GROUPS + EXPERTS_PER_GROUP * g_i
    emask = (lane >= lo) & (lane < lo + EXPERTS_PER_GROUP)
    emax = jnp.max(jnp.where(emask, lg, -jnp.inf), axis=-1, keepdims=True)
    eexp = jnp.where(emask, jnp.exp(lg - emax), 0.0)
    ep = eexp / jnp.sum(eexp, axis=-1, keepdims=True)
    p1, i1 = _lane_argmax(ep, emask, lane)
    p2, i2 = _lane_argmax(ep, emask & (lane != i1), lane)
    denom = p1 + p2
    gate1 = g_w * p1 / denom
    gate2 = g_w * p2 / denom
    e1 = i1 - N_GROUPS
    e2 = i2 - N_GROUPS

    oh1 = (lane == e1).astype(F32)
    oh2 = (lane == e2).astype(F32)
    cnt = oh1 + oh2
    ri = lax.broadcasted_iota(jnp.int32, (t, t), 0)
    ci = lax.broadcasted_iota(jnp.int32, (t, t), 1)
    before = (ci < ri).astype(BF16)
    seen = jnp.dot(before, cnt.astype(BF16), preferred_element_type=F32) + carry_ref[0:1, :]
    rank1 = jnp.sum(oh1 * seen, axis=-1, keepdims=True)
    rank2 = jnp.sum(oh2 * seen, axis=-1, keepdims=True)
    carry_ref[0:1, :] = carry_ref[0:1, :] + jnp.sum(cnt, axis=0, keepdims=True)

    out = jnp.zeros(lg.shape, F32)
    for slot, val in ((ROUTE_E1, e1.astype(F32)), (ROUTE_E2, e2.astype(F32)), (ROUTE_G1, gate1),
                      (ROUTE_G2, gate2), (ROUTE_RANK1, rank1), (ROUTE_RANK2, rank2)):
        out = jnp.where(lane == slot, val, out)
    return out


def _tile_gather_copy(src_hbm, idx_ref, buf, sem, slot, r):
    src = src_hbm.at[pl.ds(pl.multiple_of(idx_ref[0, 0, r], TOKEN_SLAB), TOKEN_SLAB), :]
    dst = buf.at[slot, pl.ds(pl.multiple_of(r * TOKEN_SLAB, TOKEN_SLAB), TOKEN_SLAB), :]
    return pltpu.make_async_copy(src, dst, sem.at[slot])


def _start_tile_gather(src_hbm, idx_ref, buf, sem, slot, rows, priorities):
    def body(g, carry):
        for j in range(GATHER_UNROLL):
            copy = _tile_gather_copy(src_hbm, idx_ref, buf, sem, slot, g * GATHER_UNROLL + j)
            copy.start(priority=priorities[j % len(priorities)])
        return carry
    lax.fori_loop(0, rows // GATHER_UNROLL, body, 0)


def _wait_tile_gather(src_hbm, buf, sem, slot, rows):
    whole = src_hbm.at[pl.ds(0, rows * TOKEN_SLAB), :]
    pltpu.make_async_copy(whole, buf.at[slot], sem.at[slot]).wait()


def _dispatch_kernel(d1_ref, d2_ref, u_ref, x_init_hbm, x_hbm, sem):
    del x_init_hbm
    rows = d1_ref.shape[2]

    def slab(ref, first_row):
        return ref.at[pl.ds(pl.multiple_of(first_row, TOKEN_SLAB), TOKEN_SLAB), :]

    def body(g, carry):
        for j in range(GATHER_UNROLL):
            r = g * GATHER_UNROLL + j
            src = slab(u_ref, r * TOKEN_SLAB)
            pltpu.make_async_copy(src, slab(x_hbm, d1_ref[0, 0, r]), sem).start(priority=j % 2)
            pltpu.make_async_copy(src, slab(x_hbm, d2_ref[0, 0, r]), sem).start(priority=(j + 1) % 2)
        return carry

    lax.fori_loop(0, rows // GATHER_UNROLL, body, 0)
    whole = x_hbm.at[pl.ds(0, rows * TOKEN_SLAB), :]
    pltpu.make_async_copy(u_ref, whole, sem).wait()
    pltpu.make_async_copy(u_ref, whole, sem).wait()


def _dispatch(u2_slabs, dest1, dest2, n_slot):
    n = dest1.shape[0]
    t = DISPATCH_ROWS
    nt = n // t
    idx = pl.BlockSpec((1, 1, t), lambda i: (i, 0, 0), memory_space=pltpu.SMEM)
    any_spec = pl.BlockSpec(memory_space=pl.ANY)
    x_init = jnp.zeros((n_slot * TOKEN_SLAB, LANES), jnp.uint32)
    return pl.pallas_call(
        _dispatch_kernel,
        grid=(nt,),
        in_specs=[idx, idx, pl.BlockSpec((t * TOKEN_SLAB, LANES), lambda i: (i, 0)), any_spec],
        out_specs=any_spec,
        out_shape=jax.ShapeDtypeStruct(x_init.shape, x_init.dtype),
        scratch_shapes=[pltpu.SemaphoreType.DMA(())],
        input_output_aliases={3: 0},
        compiler_params=_cparams(("arbitrary",)),
        name="dispatch",
    )(dest1.reshape(nt, 1, t), dest2.reshape(nt, 1, t), u2_slabs, x_init)


def _expert_kernel(blk_expert_ref, n_used_ref, x_ref, wg_ref, wu_ref, wd_ref, y_ref, wg_s, wu_s, wd_s):
    i = pl.program_id(0)
    n_used = n_used_ref[0]

    @pl.when(i >= n_used)
    def _():
        y_ref[...] = jnp.zeros_like(y_ref)

    @pl.when(i < n_used)
    def _():
        @pl.when((i == 0) | (blk_expert_ref[i] != blk_expert_ref[jnp.maximum(i - 1, 0)]))
        def _():
            wg_s[...] = wg_ref[0].astype(BF16)
            wu_s[...] = wu_ref[0].astype(BF16)
            wd_s[...] = wd_ref[0].astype(BF16)

        x = _unpack_bf16_pairs(_load_token_slabs(x_ref, EXPERT_ROWS)).astype(BF16)
        hg = jnp.dot(x, wg_s[...], preferred_element_type=F32)
        hu = jnp.dot(x, wu_s[...], preferred_element_type=F32)
        act = (hg * _sigmoid(hg) * hu).astype(BF16)
        _store_token_slabs(y_ref, _pack_bf16_pairs(jnp.dot(act, wd_s[...], preferred_element_type=F32)))


def _expert_mlp(x_slabs, block_expert, n_used, w_gate, w_up, w_down):
    n_blk = block_expert.shape[0]
    d = w_gate.shape[1]
    hdim = w_gate.shape[2]
    rows = EXPERT_ROWS
    used = lambda i, nu: jnp.minimum(i, nu[0] - 1)
    weight = lambda i, be, nu: (be[used(i, nu)], 0, 0)
    grid_spec = pltpu.PrefetchScalarGridSpec(
        num_scalar_prefetch=2,
        grid=(n_blk,),
        in_specs=[pl.BlockSpec((rows * TOKEN_SLAB, LANES), lambda i, be, nu: (used(i, nu), 0)),
                  pl.BlockSpec((1, d, hdim), weight),
                  pl.BlockSpec((1, d, hdim), weight),
                  pl.BlockSpec((1, hdim, d), weight)],
        out_specs=pl.BlockSpec((rows * TOKEN_SLAB, LANES), lambda i, be, nu: (i, 0)),
        scratch_shapes=[pltpu.VMEM((d, hdim), BF16), pltpu.VMEM((d, hdim), BF16), pltpu.VMEM((hdim, d), BF16)],
    )
    return pl.pallas_call(
        _expert_kernel,
        grid_spec=grid_spec,
        out_shape=jax.ShapeDtypeStruct((n_blk * rows * TOKEN_SLAB, LANES), jnp.uint32),
        compiler_params=_cparams(("arbitrary",)),
        name="expert_mlp",
    )(block_expert, n_used, x_slabs, w_gate, w_up, w_down)


def _combine_kernel(d1_ref, d2_ref, d1n_ref, d2n_ref, y_hbm, route_ref, h1_ref, g2_ref, lng_ref, lnb_ref,
                    o_ref, abuf, bbuf, sem_a, sem_b):
    i = pl.program_id(0)
    n = pl.num_programs(0)
    slot = i % 2
    rows = o_ref.shape[0]

    @pl.when(i == 0)
    def _():
        _start_tile_gather(y_hbm, d1_ref, abuf, sem_a, 0, rows, GATHER_PRIORITIES)
        _start_tile_gather(y_hbm, d2_ref, bbuf, sem_b, 0, rows, GATHER_PRIORITIES)

    @pl.when(i + 1 < n)
    def _():
        _start_tile_gather(y_hbm, d1n_ref, abuf, sem_a, 1 - slot, rows, GATHER_PRIORITIES)
        _start_tile_gather(y_hbm, d2n_ref, bbuf, sem_b, 1 - slot, rows, GATHER_PRIORITIES)

    _wait_tile_gather(y_hbm, abuf, sem_a, slot, rows)
    _wait_tile_gather(y_hbm, bbuf, sem_b, slot, rows)
    route = route_ref[...]
    f = (_unpack_bf16_pairs(_load_token_slabs(abuf.at[slot], rows)) * route[:, ROUTE_G1:ROUTE_G1 + 1]
         + _unpack_bf16_pairs(_load_token_slabs(bbuf.at[slot], rows)) * route[:, ROUTE_G2:ROUTE_G2 + 1])
    o_ref[...] = _layer_norm(DEEPNORM_ALPHA * h1_ref[...] + g2_ref[0] * f, lng_ref[...], lnb_ref[...])


def _combine(y_tiles, dest1, dest2, route, h1, g2, ln_g, ln_b, tokens_per_batch):
    n, d = h1.shape
    t = COMBINE_ROWS
    nt = n // t
    per_b = tokens_per_batch // t
    d1 = dest1.reshape(nt, 1, t)
    d2 = dest2.reshape(nt, 1, t)
    cur = pl.BlockSpec((1, 1, t), lambda i: (i, 0, 0), memory_space=pltpu.SMEM)
    nxt = pl.BlockSpec((1, 1, t), lambda i: (jnp.minimum(i + 1, nt - 1), 0, 0), memory_space=pltpu.SMEM)
    small = lambda arr: pl.BlockSpec(arr.shape, lambda i: (0,) * arr.ndim)
    return pl.pallas_call(
        _combine_kernel,
        grid=(nt,),
        in_specs=[cur, cur, nxt, nxt, pl.BlockSpec(memory_space=pl.ANY),
                  pl.BlockSpec((t, LANES), lambda i: (i, 0)),
                  pl.BlockSpec((t, d), lambda i: (i, 0)),
                  pl.BlockSpec((1, 1, d), lambda i: (i // per_b, 0, 0)),
                  small(ln_g), small(ln_b)],
        out_specs=pl.BlockSpec((t, d), lambda i: (i, 0)),
        out_shape=jax.ShapeDtypeStruct((n, d), F32),
        scratch_shapes=[pltpu.VMEM((2, t * TOKEN_SLAB, LANES), jnp.uint32),
                        pltpu.VMEM((2, t * TOKEN_SLAB, LANES), jnp.uint32),
                        pltpu.SemaphoreType.DMA((2,)), pltpu.SemaphoreType.DMA((2,))],
        compiler_params=_cparams(("arbitrary",)),
        name="combine",
    )(d1, d2, d1, d2, y_tiles, route, h1, g2, ln_g, ln_b)


def _hi_lo(w):
    hi = w.astype(BF16)
    return jnp.stack([hi, (w - hi.astype(F32)).astype(BF16)])


def _block_diag2(w):
    z = jnp.zeros_like(w[0])
    return jnp.concatenate([jnp.concatenate([w[0], z], axis=1), jnp.concatenate([z, w[1]], axis=1)], axis=0)


def kernel(x, c, ctx, c_ctx, emb_ln_g, emb_ln_b, w_mod, b_mod, w_in, tshift_mu, rwkv_w0, rwkv_w2, rwkv_a0, rwkv_a2, rwkv_g2, rwkv_k_k, rwkv_k_a, rwkv_r_k, rwkv_gn_g, rwkv_gn_b, ret_decay, ret_gn_g, ret_gn_b, w_out, ln1_g, ln1_b, router_group, router_group_bias, router_expert, router_expert_bias, expert_w_gate, expert_w_up, expert_w_down, ln2_g, ln2_b):
    assert w_mod.shape[0] == 1, "written for DEPTH == 1 (context outputs are never emitted)"
    b, n_tok, d = x.shape
    n_ctx = ctx.shape[1]
    row = lambda v: v.reshape(1, -1)

    c_rows = jnp.zeros((SUBLANES, d), F32).at[:b].set(c).at[b].set(c_ctx)
    mod = _modulation(c_rows, w_mod[0], row(b_mod[0]))
    sh1, s1, g1, sh2, s2, g2 = [mod[:b, j * d:(j + 1) * d].reshape(b, 1, d) for j in range(6)]
    sh1c, s1c = [jnp.broadcast_to(mod[b, j * d:(j + 1) * d].reshape(1, 1, d), (b, 1, d)) for j in range(2)]

    w_in_bf16 = w_in[0].astype(BF16)
    pr, pt = _in_proj(x, row(emb_ln_g), row(emb_ln_b), s1, sh1, w_in_bf16)
    pr_c, pt_c = _in_proj(ctx, row(emb_ln_g), row(emb_ln_b), s1c, sh1c, w_in_bf16)

    prep_params = (row(tshift_mu[0]), row(rwkv_w0[0]), _hi_lo(_block_diag2(rwkv_w2[0])), row(rwkv_a0[0]),
                   _hi_lo(_block_diag2(rwkv_a2[0])), _hi_lo(rwkv_g2[0]), row(rwkv_k_k[0]), row(rwkv_k_a[0]),
                   row(rwkv_r_k[0]),
                   _segment_ones(D_RWKV, RWKV_HEAD))
    lat = _rwkv_prepare(pr, prep_params, grid_shift=True)
    cx = _rwkv_prepare(pr_c, prep_params, grid_shift=False)
    r_l, v_l, kk_l, w_l, kd_l, bb_l, gate_l, bonus_l = lat
    r_c, v_c, kk_c, w_c, kd_c, bb_c, _, _ = cx

    y_f, y_b = _wkv7((r_l, v_l, kk_l, w_l, kd_l, bb_l), (r_c, v_c, kk_c, w_c, kd_c, bb_c), b, n_tok, n_ctx)

    cos_t, sin_t = _rope_tables(n_tok)
    t_f, t_b = _retention(pt, pt_c, ret_decay[0], cos_t, sin_t)

    wr = jnp.zeros((d, LANES), F32).at[:, :N_GROUPS].set(router_group[0])
    wr = wr.at[:, N_GROUPS:N_GROUPS + N_EXPERTS].set(router_expert[0])
    br = jnp.zeros((1, LANES), F32).at[0, :N_GROUPS].set(router_group_bias[0])
    br = br.at[0, N_GROUPS:N_GROUPS + N_EXPERTS].set(router_expert_bias[0].reshape(-1))
    vecs = (row(emb_ln_g), row(emb_ln_b), g1, s2, sh2, row(rwkv_gn_g[0]), row(rwkv_gn_b[0]),
            row(ret_gn_g[0]), row(ret_gn_b[0]), row(ln1_g[0]), row(ln1_b[0]), br)
    wr_hi_lo = _hi_lo(wr)
    mats = (_segment_ones(D_RWKV, RWKV_HEAD), w_out[0].astype(BF16), wr_hi_lo[0], wr_hi_lo[1])
    h1, u2, route, route_t, counts = _out_proj(x, y_f, y_b, bonus_l, gate_l, t_f, t_b, pt, vecs, mats)

    n_all = b * n_tok

    e1 = route_t[ROUTE_E1].astype(jnp.int32)
    e2 = route_t[ROUTE_E2].astype(jnp.int32)
    cnt = counts[0, :N_EXPERTS].astype(jnp.int32)
    padded = ((cnt + EXPERT_ROWS - 1) // EXPERT_ROWS) * EXPERT_ROWS
    pends = jnp.cumsum(padded)
    pstarts = pends - padded
    expert_ids = jnp.arange(N_EXPERTS, dtype=jnp.int32)
    start_of = lambda e: jnp.sum(jnp.where(e[:, None] == expert_ids[None, :], pstarts[None, :], 0), axis=1)
    dest1 = (start_of(e1) + route_t[ROUTE_RANK1].astype(jnp.int32)) * TOKEN_SLAB
    dest2 = (start_of(e2) + route_t[ROUTE_RANK2].astype(jnp.int32)) * TOKEN_SLAB
    n_blk = -(-(n_all * 2) // EXPERT_ROWS) + N_EXPERTS
    block_start = jnp.arange(n_blk, dtype=jnp.int32) * EXPERT_ROWS
    block_expert = jnp.minimum(jnp.sum((block_start[:, None] >= pends[None, :]).astype(jnp.int32), axis=1),
                               N_EXPERTS - 1)

    n_used = (pends[N_EXPERTS - 1:] // EXPERT_ROWS).astype(jnp.int32)
    x_slabs = _dispatch(u2, dest1, dest2, n_blk * EXPERT_ROWS)
    y_tiles = _expert_mlp(x_slabs, block_expert, n_used, expert_w_gate[0], expert_w_up[0], expert_w_down[0])
    out = _combine(y_tiles, dest1, dest2, route, h1.reshape(n_all, d), g2, row(ln2_g[0]), row(ln2_b[0]), n_tok)
    return out.reshape(b, n_tok, d)
```

```python
import functools
import math

import jax
import jax.numpy as jnp
import numpy as np
from jax import lax
from jax.experimental import pallas as pl
from jax.experimental.pallas import tpu as pltpu

F32 = jnp.float32
BF16 = jnp.bfloat16
HIGHEST = lax.Precision.HIGHEST

GRID_W = 64
D_RWKV = 512
RWKV_HEAD = 64
RWKV_HEADS = D_RWKV // RWKV_HEAD
DECAY_LORA = 64
AAA_LORA = 64
GATE_LORA = 128
D_RET = 512
RET_HEADS = 4
RET_HEAD = D_RET // RET_HEADS
RET_CHUNK = 128
RET_CHUNKS_PER_STEP = 4
RWKV_COLS = 3 * D_RWKV + 2 * (DECAY_LORA + AAA_LORA) + GATE_LORA
RET_COLS = 4 * D_RET
N_GROUPS = 4
EXPERTS_PER_GROUP = 8
N_EXPERTS = N_GROUPS * EXPERTS_PER_GROUP
EXPERT_HIDDEN = 512
ROPE_BASE = 10000.0
LN_EPS = 1e-5
RWKV_GN_EPS = 64e-5
RET_GN_EPS = 1e-5
DEEPNORM_ALPHA = 2.0 ** 0.25
EXP_NEG_HALF = math.exp(-0.5)

LANES = 128
SUBLANES = 8
VMEM_LIMIT_BYTES = 56 * 1024 * 1024

PREPARE_ROWS = 256
MODULATION_COLS = 1536

WKV_CHUNK = 64
WKV_CHUNKS_PER_STEP = 2


IN_PROJ_ROWS = 512
OUT_PROJ_ROWS = 512

TOKEN_SLAB = 4
EXPERT_ROWS = 512
COMBINE_ROWS = 512
COMBINE_CHUNK = 64
DISPATCH_ROWS = 512
GATHER_UNROLL = 16
GATHER_PRIORITIES = (0, 1)


def _pack_bf16_pairs(x):
    half = x.shape[1] // 2

    def bf16_bits(v):
        b = lax.bitcast_convert_type(v, jnp.uint32)
        return (b + jnp.uint32(0x7FFF) + ((b >> 16) & jnp.uint32(1))) >> 16

    return bf16_bits(x[:, :half]) | (bf16_bits(x[:, half:]) << 16)


def _unpack_bf16_pairs(p):
    lo = lax.bitcast_convert_type(p << 16, F32)
    hi = lax.bitcast_convert_type(p & jnp.uint32(0xFFFF0000), F32)
    return jnp.concatenate([lo, hi], axis=-1)


def _store_token_slabs(ref, x):
    rows = x.shape[0]
    for j in range(TOKEN_SLAB):
        ref[pl.ds(j, rows, stride=TOKEN_SLAB), :] = x[:, j * LANES:(j + 1) * LANES]


def _load_token_slabs(ref, rows):
    return jnp.concatenate([ref[pl.ds(j, rows, stride=TOKEN_SLAB), :] for j in range(TOKEN_SLAB)], axis=-1)


def _cparams(sem):
    return pltpu.CompilerParams(dimension_semantics=sem, vmem_limit_bytes=VMEM_LIMIT_BYTES)


def _layer_norm(x, g, b, eps=LN_EPS):
    mu = jnp.mean(x, axis=-1, keepdims=True)
    xc = x - mu
    var = jnp.mean(xc * xc, axis=-1, keepdims=True)
    return xc * lax.rsqrt(var + eps) * g + b


def _sigmoid(x):
    return 1.0 / (1.0 + jnp.exp(-x))


def _split_bf16(x):
    hi = x.astype(BF16)
    return hi, (x - hi.astype(F32)).astype(BF16)


def _segsum(x, ones_bf16):
    t = x.shape[0]
    s = jnp.dot(jnp.concatenate(_split_bf16(x), axis=0), ones_bf16, preferred_element_type=F32)
    return s[:t] + s[t:]


def _dot_split(x, w_hi, w_lo):
    hi, lo = _split_bf16(x)
    acc = jnp.dot(hi, w_hi, preferred_element_type=F32)
    acc = acc + jnp.dot(lo, w_hi, preferred_element_type=F32)
    return acc + jnp.dot(hi, w_lo, preferred_element_type=F32)


def _segment_ones(width, seg):
    idx = np.arange(width) // seg
    return jnp.asarray(idx[:, None] == idx[None, :], dtype=BF16)


def _mod_kernel(c_ref, w_ref, b_ref, o_ref):
    c = c_ref[...]
    sc = c * _sigmoid(c)
    o_ref[...] = jnp.dot(sc, w_ref[...], precision=HIGHEST, preferred_element_type=F32) + b_ref[...]


def _modulation(c_rows, w_mod, b_mod):
    rows, d = c_rows.shape
    n = w_mod.shape[1]
    tn = MODULATION_COLS
    return pl.pallas_call(
        _mod_kernel,
        grid=(n // tn,),
        in_specs=[pl.BlockSpec((rows, d), lambda j: (0, 0)),
                  pl.BlockSpec((d, tn), lambda j: (0, j)),
                  pl.BlockSpec((1, tn), lambda j: (0, j))],
        out_specs=pl.BlockSpec((rows, tn), lambda j: (0, j)),
        out_shape=jax.ShapeDtypeStruct((rows, n), F32),
        compiler_params=_cparams(("arbitrary",)),
        name="modulation",
    )(c_rows, w_mod, b_mod)


def _in_proj_kernel(x_ref, g_ref, b_ref, s_ref, sh_ref, w_ref, pr_ref, pt_ref):
    h = _layer_norm(x_ref[0], g_ref[...], b_ref[...])
    u = h * (1.0 + s_ref[0]) + sh_ref[0]
    p = jnp.dot(u.astype(BF16), w_ref[...], preferred_element_type=F32)
    pr_ref[0] = p[:, :RWKV_COLS]
    pt_ref[0] = p[:, RWKV_COLS:]


def _in_proj(x, ln_g, ln_b, s1, sh1, w_in_bf16):
    b, n, d = x.shape
    tm = min(IN_PROJ_ROWS, n)
    cols = w_in_bf16.shape[1]
    return pl.pallas_call(
        _in_proj_kernel,
        grid=(b, n // tm),
        in_specs=[pl.BlockSpec((1, tm, d), lambda bi, i: (bi, i, 0)),
                  pl.BlockSpec((1, d), lambda bi, i: (0, 0)),
                  pl.BlockSpec((1, d), lambda bi, i: (0, 0)),
                  pl.BlockSpec((1, 1, d), lambda bi, i: (bi, 0, 0)),
                  pl.BlockSpec((1, 1, d), lambda bi, i: (bi, 0, 0)),
                  pl.BlockSpec((d, cols), lambda bi, i: (0, 0))],
        out_specs=[pl.BlockSpec((1, tm, RWKV_COLS), lambda bi, i: (bi, i, 0)),
                   pl.BlockSpec((1, tm, RET_COLS), lambda bi, i: (bi, i, 0))],
        out_shape=[jax.ShapeDtypeStruct((b, n, RWKV_COLS), F32),
                   jax.ShapeDtypeStruct((b, n, RET_COLS), F32)],
        compiler_params=_cparams(("arbitrary", "arbitrary")),
        name="in_proj",
    )(x, ln_g, ln_b, s1, sh1, w_in_bf16)


def _rwkv_prepare_kernel(cur_ref, prev_ref, next_ref, mu_ref, w0_ref, w2_ref, a0_ref, a2_ref, g2_ref,
                         kk_scale_ref, ka_ref, rk_ref, ones_ref,
                         r_ref, v_ref, kk_ref, w_ref, kd_ref, bb_ref, g_ref, bonus_ref,
                         *, grid_shift, n_tok):
    cur = cur_ref[0]
    t, c = cur.shape
    row = lax.broadcasted_iota(jnp.int32, (t, c), 0)
    lane = lax.broadcasted_iota(jnp.int32, (t, c), 1)
    prev_tok = pltpu.roll(cur, 1, 0)
    next_tok = pltpu.roll(cur, t - 1, 0)
    if grid_shift:
        col = row & (GRID_W - 1)
        tok = row + pl.program_id(1) * t
        left = jnp.where(col > 0, prev_tok, 0.0)
        right = jnp.where(col < GRID_W - 1, next_tok, 0.0)
        up = jnp.where(tok >= GRID_W, jnp.concatenate([prev_ref[0], cur[:t - GRID_W]], axis=0), 0.0)
        down = jnp.where(tok < n_tok - GRID_W, jnp.concatenate([cur[GRID_W:], next_ref[0]], axis=0), 0.0)
        cm = lane & 3
        shifted = jnp.where(cm == 0, left, jnp.where(cm == 1, right, jnp.where(cm == 2, up, down)))
    else:
        prev_tok = jnp.where(row > 0, prev_tok, 0.0)
        next_tok = jnp.where(row < t - 1, next_tok, 0.0)
        shifted = jnp.where((lane & 1) == 0, prev_tok, next_tok)
    pm = cur + mu_ref[...] * (shifted - cur)

    r = pm[:, 0:D_RWKV]
    k = pm[:, D_RWKV:2 * D_RWKV]
    v = pm[:, 2 * D_RWKV:3 * D_RWKV]
    o = 3 * D_RWKV
    lw = pm[:, o:o + 2 * DECAY_LORA]
    la = pm[:, o + 2 * DECAY_LORA:o + 2 * (DECAY_LORA + AAA_LORA)]
    lg = pm[:, o + 2 * (DECAY_LORA + AAA_LORA):]

    w = w0_ref[...] + _dot_split(jnp.tanh(lw), w2_ref[0], w2_ref[1])
    log_decay = -EXP_NEG_HALF * _sigmoid(w)
    a = _sigmoid(a0_ref[...] + _dot_split(la, a2_ref[0], a2_ref[1]))
    gate = _dot_split(_sigmoid(lg), g2_ref[0], g2_ref[1])

    ones = ones_ref[...]
    kk_raw = k * kk_scale_ref[...]
    kk = kk_raw / jnp.maximum(jnp.sqrt(_segsum(kk_raw * kk_raw, ones)), 1e-12)
    ka = ka_ref[...]
    a0 = a[:, :D_RWKV]
    a1 = a[:, D_RWKV:]
    kd0 = k * (1.0 + (a0 - 1.0) * ka)
    kd1 = k * (1.0 + (a1 - 1.0) * ka)
    bonus = _segsum(r * (kd0 + kd1) * rk_ref[...], ones) * v

    r_ref[0] = r
    v_ref[0] = v
    kk_ref[0] = kk
    w_ref[0] = log_decay
    kd_ref[0] = jnp.concatenate([kd0, kd1], axis=-1)
    bb_ref[0] = jnp.concatenate([kk * a0, kk * a1], axis=-1)
    g_ref[0] = gate
    bonus_ref[0] = bonus


def _rwkv_prepare(pr, params, grid_shift):
    b, n, c = pr.shape
    t = PREPARE_ROWS
    if not grid_shift:
        assert n == t, "sequence token shift is written for a single tile"
    halo_blocks = n // GRID_W
    per_tile = t // GRID_W
    small = lambda shape: pl.BlockSpec(shape, lambda bi, i: (0,) * len(shape))
    tok_spec = lambda width: pl.BlockSpec((1, t, width), lambda bi, i: (bi, i, 0))
    out_widths = (D_RWKV, D_RWKV, D_RWKV, 2 * D_RWKV, 2 * D_RWKV, 2 * D_RWKV, D_RWKV, D_RWKV)
    kernel = functools.partial(_rwkv_prepare_kernel, grid_shift=grid_shift, n_tok=n)
    return pl.pallas_call(
        kernel,
        grid=(b, n // t),
        in_specs=[tok_spec(c),
                  pl.BlockSpec((1, GRID_W, c), lambda bi, i: (bi, jnp.maximum(i * per_tile - 1, 0), 0)),
                  pl.BlockSpec((1, GRID_W, c),
                               lambda bi, i: (bi, jnp.minimum((i + 1) * per_tile, halo_blocks - 1), 0)),
                  small((1, c)), small((1, 2 * D_RWKV)), small((2, 2 * DECAY_LORA, 2 * D_RWKV)),
                  small((1, 2 * D_RWKV)), small((2, 2 * AAA_LORA, 2 * D_RWKV)), small((2, GATE_LORA, D_RWKV)),
                  small((1, D_RWKV)), small((1, D_RWKV)), small((1, D_RWKV)), small((D_RWKV, D_RWKV))],
        out_specs=[tok_spec(wd) for wd in out_widths],
        out_shape=[jax.ShapeDtypeStruct((b, n, wd), F32) for wd in out_widths],
        compiler_params=_cparams(("arbitrary", "arbitrary")),
        name="rwkv_prepare",
    )(pr, pr, pr, *params)


def _bdot(a, b):
    return jnp.dot(a.astype(BF16), b.astype(BF16), preferred_element_type=F32)


def _bdot_nt(a, b):
    return lax.dot_general(a.astype(BF16), b.astype(BF16), (((1,), (1,)), ((), ())), preferred_element_type=F32)


def _bdot_tn(a, b):
    return lax.dot_general(a.astype(BF16), b.astype(BF16), (((0,), (0,)), ((), ())), preferred_element_type=F32)


def _wkv7_chunk_kernel(*refs, n_ctx_chunks):
    c = WKV_CHUNK
    p = 2 * c
    n_in = 12
    in_refs = (refs[:n_in], refs[n_in:2 * n_in])
    y_refs = refs[2 * n_in:2 * n_in + 2]
    state_ref = refs[2 * n_in + 2]
    n = pl.program_id(0)
    n_batch = in_refs[0][0].shape[0]
    pairs_per_batch = RWKV_HEADS // 2
    pairs_per_dir = n_batch * pairs_per_batch

    @pl.when(n == 0)
    def _():
        state_ref[...] = jnp.zeros_like(state_ref)

    is_ctx = n < n_ctx_chunks
    ti = lax.broadcasted_iota(jnp.int32, (c, c), 0)
    tj = lax.broadcasted_iota(jnp.int32, (c, c), 1)
    ri = lax.broadcasted_iota(jnp.int32, (p, p), 0)
    ci = lax.broadcasted_iota(jnp.int32, (p, p), 1)
    same_head = (ri >= c) == (ci >= c)
    ii = ri & (c - 1)
    jj = ci & (c - 1)
    eye = (ri == ci).astype(F32)
    first = lax.broadcasted_iota(jnp.int32, (c, p), 1) < RWKV_HEAD

    def stack(x):
        return jnp.concatenate([jnp.where(first, x, 0.0), jnp.where(first, 0.0, x)], axis=0)

    def unstack(x):
        return x[:c] + x[c:]

    n_sub = WKV_CHUNKS_PER_STEP

    def rows(d, sub):
        first_row = sub * c if d == 0 else (n_sub - 1 - sub) * c
        return slice(first_row, first_row + c)

    a_st, r_st, k_st, b_st, k2_st, b2_st, v_st, g_chunk, earlier, upto_self = ([] for _ in range(10))
    for d, sub in [(d, sub) for d in range(2) for sub in range(n_sub)]:
        r_l, v_l, kk_l, lw_l, kd_l, bb_l, r_c, v_c, kk_c, lw_c, kd_c, bb_c = in_refs[d]
        rs = rows(d, sub)
        pick = lambda xc, xl: jnp.concatenate(
            [jnp.where(is_ctx, xc[bi, rs, :], xl[bi, rs, :]) for bi in range(n_batch)], axis=-1)
        r, v, kk, lw, kd, bb = (pick(r_c, r_l), pick(v_c, v_l), pick(kk_c, kk_l), pick(lw_c, lw_l),
                                pick(kd_c, kd_l), pick(bb_c, bb_l))
        before = (tj < ti) if d == 0 else (tj > ti)
        upto = (before | (ti == tj)).astype(BF16)
        hi = lw.astype(BF16)
        r1 = lw - hi.astype(F32)
        mid = r1.astype(BF16)
        lo = (r1 - mid.astype(F32)).astype(BF16)
        cum = (jnp.dot(upto, hi, preferred_element_type=F32) + jnp.dot(upto, mid, preferred_element_type=F32)
               + jnp.dot(upto, lo, preferred_element_type=F32))
        tot = jnp.sum(lw, axis=0, keepdims=True)
        e_neg = jnp.exp(-cum)
        e_rem = jnp.exp(tot - cum)
        alpha = kk * jnp.exp(cum - lw)
        rho = r * jnp.exp(cum)
        beta = bb * e_neg
        kappa = kd * e_neg
        kappa_rem = kd * e_rem
        beta_rem = bb * e_rem
        g_all = jnp.exp(tot)
        pair_before = same_head & ((jj < ii) if d == 0 else (jj > ii))
        pair_upto = pair_before | (ri == ci)
        for hp in range(pairs_per_dir):
            sl = slice(hp * p, (hp + 1) * p)
            a_st.append(stack(alpha[:, sl]))
            r_st.append(stack(rho[:, sl]))
            k_st.append(stack(kappa[:, sl]))
            b_st.append(stack(beta[:, sl]))
            k2_st.append(stack(kappa_rem[:, sl]))
            b2_st.append(stack(beta_rem[:, sl]))
            v_st.append(stack(v[:, sl]))
            g_chunk.append(g_all[:, sl])
            earlier.append(pair_before)
            upto_self.append(pair_upto)

    pairs = range(2 * n_sub * pairs_per_dir)
    g = [_bdot_nt(jnp.concatenate([a_st[h], r_st[h]], axis=0), jnp.concatenate([k_st[h], b_st[h]], axis=0))
         for h in pairs]
    m1 = [jnp.where(earlier[h], g[h][:p, :p], 0.0) for h in pairs]
    m2 = [jnp.where(earlier[h], g[h][:p, p:], 0.0) for h in pairs]
    n1 = [jnp.where(upto_self[h], g[h][p:, :p], 0.0) for h in pairs]
    n2 = [jnp.where(upto_self[h], g[h][p:, p:], 0.0) for h in pairs]

    in_block = (ii >> 3) == (jj >> 3)
    pw = [-jnp.where(in_block, m2[h], 0.0) for h in pairs]
    inv = [eye + pw[h] for h in pairs]
    pw = [_bdot(pw[h], pw[h]) for h in pairs]
    both = [_bdot(jnp.concatenate([inv[h], pw[h]], axis=0), pw[h]) for h in pairs]
    inv = [inv[h] + both[h][:p] for h in pairs]
    inv = [inv[h] + _bdot(inv[h], both[h][p:]) for h in pairs]
    for sh in (3, 4, 5):
        off = ((ii >> (sh + 1)) == (jj >> (sh + 1))) & ((ii >> sh) != (jj >> sh))
        left = [_bdot(inv[h], jnp.where(off, m2[h], 0.0)) for h in pairs]
        inv = [inv[h] - _bdot(left[h], inv[h]) for h in pairs]

    mnv = [_bdot(jnp.concatenate([m1[h], n1[h]], axis=0), v_st[h]) for h in pairs]
    m1v = [mnv[h][:p] for h in pairs]
    n1v = [mnv[h][p:] for h in pairs]
    au = [_bdot(inv[h], jnp.concatenate([a_st[h], m1v[h]], axis=1)) for h in pairs]
    nn = [_bdot(n2[h], au[h]) for h in pairs]
    pc = [_bdot_tn(b2_st[h], au[h][:, :p]) for h in pairs]
    qc_t = [_bdot_tn(jnp.concatenate([v_st[h], -au[h][:, p:]], axis=0),
                     jnp.concatenate([k2_st[h], b2_st[h]], axis=0)) for h in pairs]
    rho_t = [unstack(r_st[h] - nn[h][:, :p]) for h in pairs]
    y_t = [unstack(n1v[h] - nn[h][:, p:]) for h in pairs]
    chains = [(d, hp) for d in range(2) for hp in range(pairs_per_dir)]
    state = [state_ref[ch] for ch in range(len(chains))]
    for sub in range(n_sub):
        item = lambda d, hp: (d * n_sub + sub) * pairs_per_dir + hp
        y = [_bdot_nt(rho_t[item(d, hp)], state[ch]) + y_t[item(d, hp)] for ch, (d, hp) in enumerate(chains)]
        s_dec = [_bdot_nt(state[ch], pc[item(d, hp)]) for ch, (d, hp) in enumerate(chains)]
        for ch, (d, hp) in enumerate(chains):
            h = item(d, hp)
            bi, hpb = divmod(hp, pairs_per_batch)
            y_refs[d][bi, rows(d, sub), hpb * p:(hpb + 1) * p] = y[ch]
            state[ch] = state[ch] * g_chunk[h] - s_dec[ch] + qc_t[h]
    for ch in range(len(chains)):
        state_ref[ch] = state[ch]


def _wkv7(lat, ctx, b, n_tok, n_ctx):
    c = WKV_CHUNKS_PER_STEP * WKV_CHUNK
    ncx = n_ctx // c
    nl = n_tok // c
    assert ncx * c == n_ctx and nl * c == n_tok
    lat_idx = (lambda n: jnp.maximum(n - ncx, 0), lambda n: nl - 1 - jnp.maximum(n - ncx, 0))
    ctx_idx = (lambda n: jnp.minimum(n, ncx - 1), lambda n: ncx - 1 - jnp.minimum(n, ncx - 1))

    def specs(idx, d):
        shared = pl.BlockSpec((b, c, D_RWKV), lambda n: (0, idx(n), 0))
        per_dir = pl.BlockSpec((b, c, D_RWKV), lambda n: (0, idx(n), d))
        return [shared, shared, shared, per_dir, per_dir, per_dir]

    in_specs, args = [], []
    for d in range(2):
        in_specs += specs(lat_idx[d], d) + specs(ctx_idx[d], d)
        args += list(lat) + list(ctx)
    return pl.pallas_call(
        functools.partial(_wkv7_chunk_kernel, n_ctx_chunks=ncx),
        grid=(ncx + nl,),
        in_specs=in_specs,
        out_specs=[pl.BlockSpec((b, c, D_RWKV), lambda n, d=d: (0, lat_idx[d](n), 0)) for d in range(2)],
        out_shape=[jax.ShapeDtypeStruct((b, n_tok, D_RWKV), F32)] * 2,
        scratch_shapes=[pltpu.VMEM((2 * b * RWKV_HEADS // 2, 2 * RWKV_HEAD, 2 * RWKV_HEAD), F32)],
        compiler_params=_cparams(("arbitrary",)),
        name="wkv7_chunk",
    )(*args)


def _rope(z, cos_t, sin_t):
    lane = lax.broadcasted_iota(jnp.int32, z.shape, 1)
    half = RET_HEAD // 4
    partner = jnp.where((lane & (2 * half - 1)) < half, pltpu.roll(z, RET_HEAD - half, 1), pltpu.roll(z, half, 1))
    return z * cos_t + partner * sin_t


def _retention_kernel(dec_ref, fwd_ref, bwd_ref, ctx_ref, cosf_ref, sinf_ref, cosb_ref, sinb_ref,
                      yf_ref, yb_ref, state_ref, dmat_ref, tail_ref, head_ref, cdec_ref):
    c = RET_CHUNK
    scale = RET_HEAD ** -0.5
    ii = lax.broadcasted_iota(jnp.int32, (c, c), 0)
    jj = lax.broadcasted_iota(jnp.int32, (c, c), 1)
    pos = lax.broadcasted_iota(jnp.int32, (c, RET_HEAD), 0).astype(F32)
    n_ctx_chunks = ctx_ref.shape[1] // c

    def head_slices(ref_val, h):
        q = ref_val[:, h * RET_HEAD:(h + 1) * RET_HEAD]
        k = ref_val[:, D_RET + h * RET_HEAD:D_RET + (h + 1) * RET_HEAD]
        v = ref_val[:, 2 * D_RET + h * RET_HEAD:2 * D_RET + (h + 1) * RET_HEAD]
        return q, k, v

    n_batch = fwd_ref.shape[0]
    heads = [(d, h) for d in range(2) for h in range(RET_HEADS)]
    chains = [(bi, d, h) for bi in range(n_batch) for d, h in heads]

    @pl.when(pl.program_id(0) == 0)
    def _():
        for d, h in heads:
            x = jnp.full((1, RET_HEAD), dec_ref[d, h], F32)
            lg = -(jnp.maximum(x, 0.0) + jnp.log(1.0 + jnp.exp(-jnp.abs(x))))
            chunk_decay = jnp.exp(lg * float(c))
            tail = jnp.exp(lg * ((c - 1.0 - pos) if d == 0 else pos))
            rel = (ii - jj) if d == 0 else (jj - ii)
            mask = (rel >= 0) if d == 0 else (rel > 0)
            dmat_ref[d, h] = jnp.where(mask, jnp.exp(lg * jnp.maximum(rel, 0).astype(F32)), 0.0)
            tail_ref[d, h] = tail
            head_ref[d, h] = jnp.exp(lg * ((pos + 1.0) if d == 0 else (c - pos)))
            cdec_ref[d, h] = jnp.broadcast_to(chunk_decay, (SUBLANES, RET_HEAD))
            order = range(n_ctx_chunks) if d == 0 else range(n_ctx_chunks - 1, -1, -1)
            for bi in range(n_batch):
                s = jnp.zeros((RET_HEAD, RET_HEAD), F32)
                for cc in order:
                    _, kc, vc = head_slices(ctx_ref[bi, cc * c:(cc + 1) * c, :], h)
                    s = s * chunk_decay + _bdot_tn(kc * scale * tail, vc)
                state_ref[bi, d, h] = s

    n_sub = RET_CHUNKS_PER_STEP

    def rows(d, sub):
        first = sub * c if d == 0 else (n_sub - 1 - sub) * c
        return slice(first, first + c)

    work = [(bi, d, h, sub) for sub in range(n_sub) for bi, d, h in chains]
    qkv = {}
    for bi, d, h, sub in work:
        blk = (fwd_ref if d == 0 else bwd_ref)[bi, rows(d, sub), :]
        cos_t = (cosf_ref if d == 0 else cosb_ref)[rows(d, sub), :]
        sin_t = (sinf_ref if d == 0 else sinb_ref)[rows(d, sub), :]
        q, k, v = head_slices(blk, h)
        qkv[bi, d, h, sub] = (_rope(q, cos_t, sin_t), _rope(k, cos_t, sin_t) * scale, v.astype(BF16))
    scores = {w: _bdot_nt(qkv[w][0], qkv[w][1]) for w in work}
    inner = {w: _bdot(scores[w] * dmat_ref[w[1], w[2]], qkv[w][2]) for w in work}
    upd = {w: _bdot_tn(qkv[w][1] * tail_ref[w[1], w[2]], qkv[w][2]) for w in work}
    state = {ch: state_ref[ch] for ch in chains}
    for sub in range(n_sub):
        cross = {ch: _bdot(qkv[ch + (sub,)][0] * head_ref[ch[1], ch[2]], state[ch]) for ch in chains}
        for bi, d, h in chains:
            ch = (bi, d, h)
            state[ch] = state[ch] * cdec_ref[d, h, 0:1, :] + upd[ch + (sub,)]
            out_ref = yf_ref if d == 0 else yb_ref
            out_ref[bi, rows(d, sub), h * RET_HEAD:(h + 1) * RET_HEAD] = inner[ch + (sub,)] + cross[ch]
    for ch in chains:
        state_ref[ch] = state[ch]


def _retention(pt, pt_ctx, ret_decay, cos_t, sin_t):
    b, n, _ = pt.shape
    c = RET_CHUNK
    rows = RET_CHUNKS_PER_STEP * c
    steps = n // rows
    qkv = 3 * D_RET
    fwd = lambda i: (0, i, 0)
    bwd = lambda i: (0, steps - 1 - i, 0)
    return pl.pallas_call(
        _retention_kernel,
        grid=(steps,),
        in_specs=[pl.BlockSpec(memory_space=pltpu.SMEM),
                  pl.BlockSpec((b, rows, qkv), fwd),
                  pl.BlockSpec((b, rows, qkv), bwd),
                  pl.BlockSpec((b, pt_ctx.shape[1], qkv), lambda i: (0, 0, 0)),
                  pl.BlockSpec((rows, RET_HEAD), lambda i: (i, 0)),
                  pl.BlockSpec((rows, RET_HEAD), lambda i: (i, 0)),
                  pl.BlockSpec((rows, RET_HEAD), lambda i: (steps - 1 - i, 0)),
                  pl.BlockSpec((rows, RET_HEAD), lambda i: (steps - 1 - i, 0))],
        out_specs=[pl.BlockSpec((b, rows, D_RET), fwd), pl.BlockSpec((b, rows, D_RET), bwd)],
        out_shape=[jax.ShapeDtypeStruct((b, n, D_RET), F32), jax.ShapeDtypeStruct((b, n, D_RET), F32)],
        scratch_shapes=[pltpu.VMEM((b, 2, RET_HEADS, RET_HEAD, RET_HEAD), F32),
                        pltpu.VMEM((2, RET_HEADS, c, c), F32),
                        pltpu.VMEM((2, RET_HEADS, c, RET_HEAD), F32),
                        pltpu.VMEM((2, RET_HEADS, c, RET_HEAD), F32),
                        pltpu.VMEM((2, RET_HEADS, SUBLANES, RET_HEAD), F32)],
        compiler_params=_cparams(("arbitrary",)),
        name="retention",
    )(ret_decay, pt, pt, pt_ctx, cos_t, sin_t, cos_t, sin_t)


def _rope_tables(n_tok):
    nf = RET_HEAD // 4
    lane = np.arange(RET_HEAD)
    inv = ROPE_BASE ** (-jnp.arange(nf, dtype=F32) / nf)
    t = jnp.arange(n_tok)
    pos = jnp.where((lane // (2 * nf) == 0)[None, :], (t // GRID_W)[:, None], (t % GRID_W)[:, None]).astype(F32)
    ang = pos * inv[lane % nf][None, :]
    sign = jnp.where((lane % (2 * nf)) < nf, -1.0, 1.0).astype(F32)
    return jnp.cos(ang), jnp.sin(ang) * sign[None, :]


def _group_norm(y, ones, seg, eps, g, b):
    mu = _segsum(y, ones) * (1.0 / seg)
    yc = y - mu
    var = _segsum(yc * yc, ones) * (1.0 / seg)
    return yc * lax.rsqrt(var + eps) * g + b


def _out_proj_kernel(x_ref, yf_ref, yb_ref, bonus_ref, gate_ref, tf_ref, tb_ref, gt_ref,
                     embg_ref, embb_ref, g1_ref, s2_ref, sh2_ref, rgn_g_ref, rgn_b_ref, tgn_g_ref, tgn_b_ref,
                     ones_r_ref, wout_ref, ln1g_ref, ln1b_ref, wrh_ref, wrl_ref, br_ref,
                     h1_ref, u2_ref, route_ref, route_t_ref, count_ref, carry_ref):
    @pl.when((pl.program_id(0) == 0) & (pl.program_id(1) == 0))
    def _():
        carry_ref[...] = jnp.zeros_like(carry_ref)

    y = yf_ref[0] + yb_ref[0]
    o_rwkv = _group_norm(y, ones_r_ref[...], RWKV_HEAD, RWKV_GN_EPS, rgn_g_ref[...], rgn_b_ref[...])
    o_rwkv = (o_rwkv + bonus_ref[0]) * gate_ref[0]
    yt = tf_ref[0] + tb_ref[0]
    gt = gt_ref[0]
    tgn_g = tgn_g_ref[...]
    tgn_b = tgn_b_ref[...]
    o_ret = jnp.concatenate(
        [_layer_norm(yt[:, h * RET_HEAD:(h + 1) * RET_HEAD], tgn_g[:, h * RET_HEAD:(h + 1) * RET_HEAD],
                     tgn_b[:, h * RET_HEAD:(h + 1) * RET_HEAD], RET_GN_EPS) for h in range(RET_HEADS)], axis=-1)
    o_ret = o_ret * (gt * _sigmoid(gt))
    cat = jnp.concatenate([o_rwkv, o_ret], axis=-1).astype(BF16)
    mix = jnp.dot(cat, wout_ref[...], preferred_element_type=F32)
    h = _layer_norm(x_ref[0], embg_ref[...], embb_ref[...])
    h1 = _layer_norm(DEEPNORM_ALPHA * h + g1_ref[0] * mix, ln1g_ref[...], ln1b_ref[...])
    u2 = h1 * (1.0 + s2_ref[0]) + sh2_ref[0]
    h1_ref[0] = h1
    _store_token_slabs(u2_ref, _pack_bf16_pairs(u2))
    route = _route_tile(_dot_split(u2, wrh_ref[...], wrl_ref[...]) + br_ref[...], carry_ref)
    route_ref[...] = route
    route_t_ref[...] = route.T[:SUBLANES]
    count_ref[...] = carry_ref[...]


def _out_proj(x, y_f, y_b, bonus, gate, t_f, t_b, pt, vecs, mats):
    b, n, d = x.shape
    t = OUT_PROJ_ROWS
    tok = lambda width: pl.BlockSpec((1, t, width), lambda bi, i: (bi, i, 0))
    per_b = pl.BlockSpec((1, 1, d), lambda bi, i: (bi, 0, 0))
    small = lambda arr: pl.BlockSpec(arr.shape, lambda bi, i: (0,) * arr.ndim)
    (embg, embb, g1, s2, sh2, rgn_g, rgn_b, tgn_g, tgn_b, ln1g, ln1b, br) = vecs
    (ones_r, wout, wr_hi, wr_lo) = mats
    gt_spec = pl.BlockSpec((1, t, D_RET), lambda bi, i: (bi, i, 3))
    tiles = n // t
    flat = lambda bi, i: (bi * tiles + i, 0)
    args = (x, y_f, y_b, bonus, gate, t_f, t_b, pt, embg, embb, g1, s2, sh2, rgn_g, rgn_b, tgn_g, tgn_b,
            ones_r, wout, ln1g, ln1b, wr_hi, wr_lo, br)
    in_specs = [tok(d)] + [tok(D_RWKV)] * 6 + [gt_spec, small(embg), small(embb), per_b, per_b, per_b,
                                                small(rgn_g), small(rgn_b), small(tgn_g), small(tgn_b),
                                                small(ones_r), small(wout), small(ln1g),
                                                small(ln1b), small(wr_hi), small(wr_lo), small(br)]
    return pl.pallas_call(
        _out_proj_kernel,
        grid=(b, n // t),
        in_specs=in_specs,
        out_specs=[tok(d), pl.BlockSpec((t * TOKEN_SLAB, LANES), flat),
                   pl.BlockSpec((t, LANES), flat),
                   pl.BlockSpec((SUBLANES, t), lambda bi, i: (0, bi * tiles + i)),
                   pl.BlockSpec((SUBLANES, LANES), lambda bi, i: (0, 0))],
        out_shape=[jax.ShapeDtypeStruct((b, n, d), F32),
                   jax.ShapeDtypeStruct((b * n * TOKEN_SLAB, LANES), jnp.uint32),
                   jax.ShapeDtypeStruct((b * n, LANES), F32),
                   jax.ShapeDtypeStruct((SUBLANES, b * n), F32),
                   jax.ShapeDtypeStruct((SUBLANES, LANES), F32)],
        scratch_shapes=[pltpu.VMEM((SUBLANES, LANES), F32)],
        compiler_params=_cparams(("arbitrary", "arbitrary")),
        name="out_proj",
    )(*args)


ROUTE_E1, ROUTE_E2, ROUTE_G1, ROUTE_G2, ROUTE_RANK1, ROUTE_RANK2 = range(6)


def _lane_argmax(x, valid, lane):
    m = jnp.max(jnp.where(valid, x, -jnp.inf), axis=-1, keepdims=True)
    idx = jnp.min(jnp.where(valid & (x == m), lane, float(LANES)), axis=-1, keepdims=True)
    return m, idx


def _route_tile(lg, carry_ref):
    t = lg.shape[0]
    lane = lax.broadcasted_iota(jnp.int32, lg.shape, 1).astype(F32)
    gmask = lane < N_GROUPS
    gmax = jnp.max(jnp.where(gmask, lg, -jnp.inf), axis=-1, keepdims=True)
    gexp = jnp.where(gmask, jnp.exp(lg - gmax), 0.0)
    gp = gexp / jnp.sum(gexp, axis=-1, keepdims=True)
    g_w, g_i = _lane_argmax(gp, gmask, lane)

    lo = N_GROUPS + EXPERTS_PER_GROUP * g_i
    emask = (lane >= lo) & (lane < lo + EXPERTS_PER_GROUP)
    emax = jnp.max(jnp.where(emask, lg, -jnp.inf), axis=-1, keepdims=True)
    eexp = jnp.where(emask, jnp.exp(lg - emax), 0.0)
    ep = eexp / jnp.sum(eexp, axis=-1, keepdims=True)
    p1, i1 = _lane_argmax(ep, emask, lane)
    p2, i2 = _lane_argmax(ep, emask & (lane != i1), lane)
    denom = p1 + p2
    gate1 = g_w * p1 / denom
    gate2 = g_w * p2 / denom
    e1 = i1 - N_GROUPS
    e2 = i2 - N_GROUPS

    oh1 = (lane == e1).astype(F32)
    oh2 = (lane == e2).astype(F32)
    cnt = oh1 + oh2
    ri = lax.broadcasted_iota(jnp.int32, (t, t), 0)
    ci = lax.broadcasted_iota(jnp.int32, (t, t), 1)
    before = (ci < ri).astype(BF16)
    seen = jnp.dot(before, cnt.astype(BF16), preferred_element_type=F32) + carry_ref[0:1, :]
    rank1 = jnp.sum(oh1 * seen, axis=-1, keepdims=True)
    rank2 = jnp.sum(oh2 * seen, axis=-1, keepdims=True)
    carry_ref[0:1, :] = carry_ref[0:1, :] + jnp.sum(cnt, axis=0, keepdims=True)

    out = jnp.zeros(lg.shape, F32)
    for slot, val in ((ROUTE_E1, e1.astype(F32)), (ROUTE_E2, e2.astype(F32)), (ROUTE_G1, gate1),
                      (ROUTE_G2, gate2), (ROUTE_RANK1, rank1), (ROUTE_RANK2, rank2)):
        out = jnp.where(lane == slot, val, out)
    return out


def _tile_gather_copy(src_hbm, idx_ref, buf, sem, slot, r):
    src = src_hbm.at[pl.ds(pl.multiple_of(idx_ref[0, 0, r], TOKEN_SLAB), TOKEN_SLAB), :]
    dst = buf.at[slot, pl.ds(pl.multiple_of(r * TOKEN_SLAB, TOKEN_SLAB), TOKEN_SLAB), :]
    return pltpu.make_async_copy(src, dst, sem.at[slot])


def _start_tile_gather(src_hbm, idx_ref, buf, sem, slot, rows, priorities):
    def body(g, carry):
        for j in range(GATHER_UNROLL):
            copy = _tile_gather_copy(src_hbm, idx_ref, buf, sem, slot, g * GATHER_UNROLL + j)
            copy.start(priority=priorities[j % len(priorities)])
        return carry
    lax.fori_loop(0, rows // GATHER_UNROLL, body, 0)


def _wait_tile_gather(src_hbm, buf, sem, slot, rows):
    whole = src_hbm.at[pl.ds(0, rows * TOKEN_SLAB), :]
    pltpu.make_async_copy(whole, buf.at[slot], sem.at[slot]).wait()


def _dispatch_kernel(d1_ref, d2_ref, u_ref, x_init_hbm, x_hbm, sem):
    del x_init_hbm
    rows = d1_ref.shape[2]

    def slab(ref, first_row):
        return ref.at[pl.ds(pl.multiple_of(first_row, TOKEN_SLAB), TOKEN_SLAB), :]

    def body(g, carry):
        for j in range(GATHER_UNROLL):
            r = g * GATHER_UNROLL + j
            src = slab(u_ref, r * TOKEN_SLAB)
            pltpu.make_async_copy(src, slab(x_hbm, d1_ref[0, 0, r]), sem).start(priority=j % 2)
            pltpu.make_async_copy(src, slab(x_hbm, d2_ref[0, 0, r]), sem).start(priority=(j + 1) % 2)
        return carry

    lax.fori_loop(0, rows // GATHER_UNROLL, body, 0)
    whole = x_hbm.at[pl.ds(0, rows * TOKEN_SLAB), :]
    pltpu.make_async_copy(u_ref, whole, sem).wait()
    pltpu.make_async_copy(u_ref, whole, sem).wait()


def _dispatch(u2_slabs, dest1, dest2, n_slot):
    n = dest1.shape[0]
    t = DISPATCH_ROWS
    nt = n // t
    idx = pl.BlockSpec((1, 1, t), lambda i: (i, 0, 0), memory_space=pltpu.SMEM)
    any_spec = pl.BlockSpec(memory_space=pl.ANY)
    x_init = jnp.zeros((n_slot * TOKEN_SLAB, LANES), jnp.uint32)
    return pl.pallas_call(
        _dispatch_kernel,
        grid=(nt,),
        in_specs=[idx, idx, pl.BlockSpec((t * TOKEN_SLAB, LANES), lambda i: (i, 0)), any_spec],
        out_specs=any_spec,
        out_shape=jax.ShapeDtypeStruct(x_init.shape, x_init.dtype),
        scratch_shapes=[pltpu.SemaphoreType.DMA(())],
        input_output_aliases={3: 0},
        compiler_params=_cparams(("arbitrary",)),
        name="dispatch",
    )(dest1.reshape(nt, 1, t), dest2.reshape(nt, 1, t), u2_slabs, x_init)


def _expert_kernel(blk_expert_ref, n_used_ref, x_ref, wg_ref, wu_ref, wd_ref, y_ref, wg_s, wu_s, wd_s):
    i = pl.program_id(0)
    n_used = n_used_ref[0]

    @pl.when(i >= n_used)
    def _():
        y_ref[...] = jnp.zeros_like(y_ref)

    @pl.when(i < n_used)
    def _():
        @pl.when((i == 0) | (blk_expert_ref[i] != blk_expert_ref[jnp.maximum(i - 1, 0)]))
        def _():
            wg_s[...] = wg_ref[0].astype(BF16)
            wu_s[...] = wu_ref[0].astype(BF16)
            wd_s[...] = wd_ref[0].astype(BF16)

        x = _unpack_bf16_pairs(_load_token_slabs(x_ref, EXPERT_ROWS)).astype(BF16)
        hg = jnp.dot(x, wg_s[...], preferred_element_type=F32)
        hu = jnp.dot(x, wu_s[...], preferred_element_type=F32)
        act = (hg * _sigmoid(hg) * hu).astype(BF16)
        _store_token_slabs(y_ref, _pack_bf16_pairs(jnp.dot(act, wd_s[...], preferred_element_type=F32)))


def _expert_mlp(x_slabs, block_expert, n_used, w_gate, w_up, w_down):
    n_blk = block_expert.shape[0]
    d = w_gate.shape[1]
    hdim = w_gate.shape[2]
    rows = EXPERT_ROWS
    used = lambda i, nu: jnp.minimum(i, nu[0] - 1)
    weight = lambda i, be, nu: (be[used(i, nu)], 0, 0)
    grid_spec = pltpu.PrefetchScalarGridSpec(
        num_scalar_prefetch=2,
        grid=(n_blk,),
        in_specs=[pl.BlockSpec((rows * TOKEN_SLAB, LANES), lambda i, be, nu: (used(i, nu), 0)),
                  pl.BlockSpec((1, d, hdim), weight),
                  pl.BlockSpec((1, d, hdim), weight),
                  pl.BlockSpec((1, hdim, d), weight)],
        out_specs=pl.BlockSpec((rows * TOKEN_SLAB, LANES), lambda i, be, nu: (i, 0)),
        scratch_shapes=[pltpu.VMEM((d, hdim), BF16), pltpu.VMEM((d, hdim), BF16), pltpu.VMEM((hdim, d), BF16)],
    )
    return pl.pallas_call(
        _expert_kernel,
        grid_spec=grid_spec,
        out_shape=jax.ShapeDtypeStruct((n_blk * rows * TOKEN_SLAB, LANES), jnp.uint32),
        compiler_params=_cparams(("arbitrary",)),
        name="expert_mlp",
    )(block_expert, n_used, x_slabs, w_gate, w_up, w_down)


def _combine_kernel(d1_ref, d2_ref, d1n_ref, d2n_ref, y_hbm, route_ref, h1_ref, g2_ref, lng_ref, lnb_ref,
                    o_ref, abuf, bbuf, sem_a, sem_b):
    i = pl.program_id(0)
    n = pl.num_programs(0)
    slot = i % 2
    rows = o_ref.shape[0]

    @pl.when(i == 0)
    def _():
        _start_tile_gather(y_hbm, d1_ref, abuf, sem_a, 0, rows, GATHER_PRIORITIES)
        _start_tile_gather(y_hbm, d2_ref, bbuf, sem_b, 0, rows, GATHER_PRIORITIES)

    _wait_tile_gather(y_hbm, abuf, sem_a, slot, rows)
    _wait_tile_gather(y_hbm, bbuf, sem_b, slot, rows)
    g2 = g2_ref[0]
    ln_g = lng_ref[...]
    ln_b = lnb_ref[...]
    ch = COMBINE_CHUNK

    def slabs(buf, k):
        first = pl.multiple_of(k * (ch * TOKEN_SLAB), ch * TOKEN_SLAB)
        cur = buf.at[slot]
        packed = jnp.concatenate([cur[pl.ds(first + j, ch, stride=TOKEN_SLAB), :] for j in range(TOKEN_SLAB)],
                                 axis=-1)
        return _unpack_bf16_pairs(packed)

    def body(k, carry):
        rs = pl.ds(pl.multiple_of(k * ch, ch), ch)
        route = route_ref[rs, :]
        f = slabs(abuf, k) * route[:, ROUTE_G1:ROUTE_G1 + 1] + slabs(bbuf, k) * route[:, ROUTE_G2:ROUTE_G2 + 1]
        x = DEEPNORM_ALPHA * h1_ref[rs, :] + g2 * f
        for j in range(ch):
            r = k * ch + j
            _tile_gather_copy(y_hbm, d1n_ref, abuf, sem_a, 1 - slot, r).start(priority=j % 2)
            _tile_gather_copy(y_hbm, d2n_ref, bbuf, sem_b, 1 - slot, r).start(priority=(j + 1) % 2)
        o_ref[rs, :] = _layer_norm(x, ln_g, ln_b)
        return carry

    lax.fori_loop(0, rows // ch, body, 0)

    @pl.when(i == n - 1)
    def _():
        _wait_tile_gather(y_hbm, abuf, sem_a, 1 - slot, rows)
        _wait_tile_gather(y_hbm, bbuf, sem_b, 1 - slot, rows)


def _combine(y_tiles, dest1, dest2, route, h1, g2, ln_g, ln_b, tokens_per_batch):
    n, d = h1.shape
    t = COMBINE_ROWS
    nt = n // t
    per_b = tokens_per_batch // t
    d1 = dest1.reshape(nt, 1, t)
    d2 = dest2.reshape(nt, 1, t)
    cur = pl.BlockSpec((1, 1, t), lambda i: (i, 0, 0), memory_space=pltpu.SMEM)
    nxt = pl.BlockSpec((1, 1, t), lambda i: (jnp.minimum(i + 1, nt - 1), 0, 0), memory_space=pltpu.SMEM)
    small = lambda arr: pl.BlockSpec(arr.shape, lambda i: (0,) * arr.ndim)
    return pl.pallas_call(
        _combine_kernel,
        grid=(nt,),
        in_specs=[cur, cur, nxt, nxt, pl.BlockSpec(memory_space=pl.ANY),
                  pl.BlockSpec((t, LANES), lambda i: (i, 0)),
                  pl.BlockSpec((t, d), lambda i: (i, 0)),
                  pl.BlockSpec((1, 1, d), lambda i: (i // per_b, 0, 0)),
                  small(ln_g), small(ln_b)],
        out_specs=pl.BlockSpec((t, d), lambda i: (i, 0)),
        out_shape=jax.ShapeDtypeStruct((n, d), F32),
        scratch_shapes=[pltpu.VMEM((2, t * TOKEN_SLAB, LANES), jnp.uint32),
                        pltpu.VMEM((2, t * TOKEN_SLAB, LANES), jnp.uint32),
                        pltpu.SemaphoreType.DMA((2,)), pltpu.SemaphoreType.DMA((2,))],
        compiler_params=_cparams(("arbitrary",)),
        name="combine",
    )(d1, d2, d1, d2, y_tiles, route, h1, g2, ln_g, ln_b)


def _hi_lo(w):
    hi = w.astype(BF16)
    return jnp.stack([hi, (w - hi.astype(F32)).astype(BF16)])


def _block_diag2(w):
    z = jnp.zeros_like(w[0])
    return jnp.concatenate([jnp.concatenate([w[0], z], axis=1), jnp.concatenate([z, w[1]], axis=1)], axis=0)


def kernel(x, c, ctx, c_ctx, emb_ln_g, emb_ln_b, w_mod, b_mod, w_in, tshift_mu, rwkv_w0, rwkv_w2, rwkv_a0, rwkv_a2, rwkv_g2, rwkv_k_k, rwkv_k_a, rwkv_r_k, rwkv_gn_g, rwkv_gn_b, ret_decay, ret_gn_g, ret_gn_b, w_out, ln1_g, ln1_b, router_group, router_group_bias, router_expert, router_expert_bias, expert_w_gate, expert_w_up, expert_w_down, ln2_g, ln2_b):
    assert w_mod.shape[0] == 1, "written for DEPTH == 1 (context outputs are never emitted)"
    b, n_tok, d = x.shape
    n_ctx = ctx.shape[1]
    row = lambda v: v.reshape(1, -1)

    c_rows = jnp.zeros((SUBLANES, d), F32).at[:b].set(c).at[b].set(c_ctx)
    mod = _modulation(c_rows, w_mod[0], row(b_mod[0]))
    sh1, s1, g1, sh2, s2, g2 = [mod[:b, j * d:(j + 1) * d].reshape(b, 1, d) for j in range(6)]
    sh1c, s1c = [jnp.broadcast_to(mod[b, j * d:(j + 1) * d].reshape(1, 1, d), (b, 1, d)) for j in range(2)]

    w_in_bf16 = w_in[0].astype(BF16)
    pr, pt = _in_proj(x, row(emb_ln_g), row(emb_ln_b), s1, sh1, w_in_bf16)
    pr_c, pt_c = _in_proj(ctx, row(emb_ln_g), row(emb_ln_b), s1c, sh1c, w_in_bf16)

    prep_params = (row(tshift_mu[0]), row(rwkv_w0[0]), _hi_lo(_block_diag2(rwkv_w2[0])), row(rwkv_a0[0]),
                   _hi_lo(_block_diag2(rwkv_a2[0])), _hi_lo(rwkv_g2[0]), row(rwkv_k_k[0]), row(rwkv_k_a[0]),
                   row(rwkv_r_k[0]),
                   _segment_ones(D_RWKV, RWKV_HEAD))
    lat = _rwkv_prepare(pr, prep_params, grid_shift=True)
    cx = _rwkv_prepare(pr_c, prep_params, grid_shift=False)
    r_l, v_l, kk_l, w_l, kd_l, bb_l, gate_l, bonus_l = lat
    r_c, v_c, kk_c, w_c, kd_c, bb_c, _, _ = cx

    y_f, y_b = _wkv7((r_l, v_l, kk_l, w_l, kd_l, bb_l), (r_c, v_c, kk_c, w_c, kd_c, bb_c), b, n_tok, n_ctx)

    cos_t, sin_t = _rope_tables(n_tok)
    t_f, t_b = _retention(pt, pt_c, ret_decay[0], cos_t, sin_t)

    wr = jnp.zeros((d, LANES), F32).at[:, :N_GROUPS].set(router_group[0])
    wr = wr.at[:, N_GROUPS:N_GROUPS + N_EXPERTS].set(router_expert[0])
    br = jnp.zeros((1, LANES), F32).at[0, :N_GROUPS].set(router_group_bias[0])
    br = br.at[0, N_GROUPS:N_GROUPS + N_EXPERTS].set(router_expert_bias[0].reshape(-1))
    vecs = (row(emb_ln_g), row(emb_ln_b), g1, s2, sh2, row(rwkv_gn_g[0]), row(rwkv_gn_b[0]),
            row(ret_gn_g[0]), row(ret_gn_b[0]), row(ln1_g[0]), row(ln1_b[0]), br)
    wr_hi_lo = _hi_lo(wr)
    mats = (_segment_ones(D_RWKV, RWKV_HEAD), w_out[0].astype(BF16), wr_hi_lo[0], wr_hi_lo[1])
    h1, u2, route, route_t, counts = _out_proj(x, y_f, y_b, bonus_l, gate_l, t_f, t_b, pt, vecs, mats)

    n_all = b * n_tok

    e1 = route_t[ROUTE_E1].astype(jnp.int32)
    e2 = route_t[ROUTE_E2].astype(jnp.int32)
    cnt = counts[0, :N_EXPERTS].astype(jnp.int32)
    padded = ((cnt + EXPERT_ROWS - 1) // EXPERT_ROWS) * EXPERT_ROWS
    pends = jnp.cumsum(padded)
    pstarts = pends - padded
    expert_ids = jnp.arange(N_EXPERTS, dtype=jnp.int32)
    start_of = lambda e: jnp.sum(jnp.where(e[:, None] == expert_ids[None, :], pstarts[None, :], 0), axis=1)
    dest1 = (start_of(e1) + route_t[ROUTE_RANK1].astype(jnp.int32)) * TOKEN_SLAB
    dest2 = (start_of(e2) + route_t[ROUTE_RANK2].astype(jnp.int32)) * TOKEN_SLAB
    n_blk = -(-(n_all * 2) // EXPERT_ROWS) + N_EXPERTS
    block_start = jnp.arange(n_blk, dtype=jnp.int32) * EXPERT_ROWS
    block_expert = jnp.minimum(jnp.sum((block_start[:, None] >= pends[None, :]).astype(jnp.int32), axis=1),
                               N_EXPERTS - 1)

    n_used = (pends[N_EXPERTS - 1:] // EXPERT_ROWS).astype(jnp.int32)
    x_slabs = _dispatch(u2, dest1, dest2, n_blk * EXPERT_ROWS)
    y_tiles = _expert_mlp(x_slabs, block_expert, n_used, expert_w_gate[0], expert_w_up[0], expert_w_down[0])
    out = _combine(y_tiles, dest1, dest2, route, h1.reshape(n_all, d), g2, row(ln2_g[0]), row(ln2_b[0]), n_tok)
    return out.reshape(b, n_tok, d)
```

```python
import functools
import math

import jax
import jax.numpy as jnp
import numpy as np
from jax import lax
from jax.experimental import pallas as pl
from jax.experimental.pallas import tpu as pltpu

F32 = jnp.float32
BF16 = jnp.bfloat16
HIGHEST = lax.Precision.HIGHEST

GRID_W = 64
D_RWKV = 512
RWKV_HEAD = 64
RWKV_HEADS = D_RWKV // RWKV_HEAD
DECAY_LORA = 64
AAA_LORA = 64
GATE_LORA = 128
D_RET = 512
RET_HEADS = 4
RET_HEAD = D_RET // RET_HEADS
RET_CHUNK = 128
RET_CHUNKS_PER_STEP = 4
RWKV_COLS = 3 * D_RWKV + 2 * (DECAY_LORA + AAA_LORA) + GATE_LORA
RET_COLS = 4 * D_RET
N_GROUPS = 4
EXPERTS_PER_GROUP = 8
N_EXPERTS = N_GROUPS * EXPERTS_PER_GROUP
EXPERT_HIDDEN = 512
ROPE_BASE = 10000.0
LN_EPS = 1e-5
RWKV_GN_EPS = 64e-5
RET_GN_EPS = 1e-5
DEEPNORM_ALPHA = 2.0 ** 0.25
EXP_NEG_HALF = math.exp(-0.5)

LANES = 128
SUBLANES = 8
VMEM_LIMIT_BYTES = 56 * 1024 * 1024

PREPARE_ROWS = 256
MODULATION_COLS = 1536

WKV_CHUNK = 64
WKV_CHUNKS_PER_STEP = 2


IN_PROJ_ROWS = 512
OUT_PROJ_ROWS = 512

TOKEN_SLAB = 4
EXPERT_ROWS = 512
COMBINE_ROWS = 512
DISPATCH_ROWS = 512
DISPATCH_BUFFERS = 3
GATHER_UNROLL = 16
GATHER_PRIORITIES = (0, 1)


def _pack_bf16_pairs(x):
    half = x.shape[1] // 2

    def bf16_bits(v):
        b = lax.bitcast_convert_type(v, jnp.uint32)
        return (b + jnp.uint32(0x7FFF) + ((b >> 16) & jnp.uint32(1))) >> 16

    return bf16_bits(x[:, :half]) | (bf16_bits(x[:, half:]) << 16)


def _unpack_bf16_pairs(p):
    lo = lax.bitcast_convert_type(p << 16, F32)
    hi = lax.bitcast_convert_type(p & jnp.uint32(0xFFFF0000), F32)
    return jnp.concatenate([lo, hi], axis=-1)


def _store_token_slabs(ref, x):
    rows = x.shape[0]
    for j in range(TOKEN_SLAB):
        ref[pl.ds(j, rows, stride=TOKEN_SLAB), :] = x[:, j * LANES:(j + 1) * LANES]


def _load_token_slabs(ref, rows):
    return jnp.concatenate([ref[pl.ds(j, rows, stride=TOKEN_SLAB), :] for j in range(TOKEN_SLAB)], axis=-1)


def _cparams(sem):
    return pltpu.CompilerParams(dimension_semantics=sem, vmem_limit_bytes=VMEM_LIMIT_BYTES)


def _layer_norm(x, g, b, eps=LN_EPS):
    mu = jnp.mean(x, axis=-1, keepdims=True)
    xc = x - mu
    var = jnp.mean(xc * xc, axis=-1, keepdims=True)
    return xc * lax.rsqrt(var + eps) * g + b


def _sigmoid(x):
    return 1.0 / (1.0 + jnp.exp(-x))


def _split_bf16(x):
    hi = x.astype(BF16)
    return hi, (x - hi.astype(F32)).astype(BF16)


def _segsum(x, ones_bf16):
    t = x.shape[0]
    s = jnp.dot(jnp.concatenate(_split_bf16(x), axis=0), ones_bf16, preferred_element_type=F32)
    return s[:t] + s[t:]


def _dot_split(x, w_hi, w_lo):
    hi, lo = _split_bf16(x)
    acc = jnp.dot(hi, w_hi, preferred_element_type=F32)
    acc = acc + jnp.dot(lo, w_hi, preferred_element_type=F32)
    return acc + jnp.dot(hi, w_lo, preferred_element_type=F32)


def _segment_ones(width, seg):
    idx = np.arange(width) // seg
    return jnp.asarray(idx[:, None] == idx[None, :], dtype=BF16)


def _mod_kernel(c_ref, w_ref, b_ref, o_ref):
    c = c_ref[...]
    sc = c * _sigmoid(c)
    o_ref[...] = jnp.dot(sc, w_ref[...], precision=HIGHEST, preferred_element_type=F32) + b_ref[...]


def _modulation(c_rows, w_mod, b_mod):
    rows, d = c_rows.shape
    n = w_mod.shape[1]
    tn = MODULATION_COLS
    return pl.pallas_call(
        _mod_kernel,
        grid=(n // tn,),
        in_specs=[pl.BlockSpec((rows, d), lambda j: (0, 0)),
                  pl.BlockSpec((d, tn), lambda j: (0, j)),
                  pl.BlockSpec((1, tn), lambda j: (0, j))],
        out_specs=pl.BlockSpec((rows, tn), lambda j: (0, j)),
        out_shape=jax.ShapeDtypeStruct((rows, n), F32),
        compiler_params=_cparams(("arbitrary",)),
        name="modulation",
    )(c_rows, w_mod, b_mod)


def _in_proj_kernel(x_ref, g_ref, b_ref, s_ref, sh_ref, w_ref, pr_ref, pt_ref):
    h = _layer_norm(x_ref[0], g_ref[...], b_ref[...])
    u = h * (1.0 + s_ref[0]) + sh_ref[0]
    p = jnp.dot(u.astype(BF16), w_ref[...], preferred_element_type=F32)
    pr_ref[0] = p[:, :RWKV_COLS]
    pt_ref[0] = p[:, RWKV_COLS:]


def _in_proj(x, ln_g, ln_b, s1, sh1, w_in_bf16):
    b, n, d = x.shape
    tm = min(IN_PROJ_ROWS, n)
    cols = w_in_bf16.shape[1]
    return pl.pallas_call(
        _in_proj_kernel,
        grid=(b, n // tm),
        in_specs=[pl.BlockSpec((1, tm, d), lambda bi, i: (bi, i, 0)),
                  pl.BlockSpec((1, d), lambda bi, i: (0, 0)),
                  pl.BlockSpec((1, d), lambda bi, i: (0, 0)),
                  pl.BlockSpec((1, 1, d), lambda bi, i: (bi, 0, 0)),
                  pl.BlockSpec((1, 1, d), lambda bi, i: (bi, 0, 0)),
                  pl.BlockSpec((d, cols), lambda bi, i: (0, 0))],
        out_specs=[pl.BlockSpec((1, tm, RWKV_COLS), lambda bi, i: (bi, i, 0)),
                   pl.BlockSpec((1, tm, RET_COLS), lambda bi, i: (bi, i, 0))],
        out_shape=[jax.ShapeDtypeStruct((b, n, RWKV_COLS), F32),
                   jax.ShapeDtypeStruct((b, n, RET_COLS), F32)],
        compiler_params=_cparams(("arbitrary", "arbitrary")),
        name="in_proj",
    )(x, ln_g, ln_b, s1, sh1, w_in_bf16)


def _rwkv_prepare_kernel(cur_ref, prev_ref, next_ref, mu_ref, w0_ref, w2_ref, a0_ref, a2_ref, g2_ref,
                         kk_scale_ref, ka_ref, rk_ref, ones_ref,
                         r_ref, v_ref, kk_ref, w_ref, kd_ref, bb_ref, g_ref, bonus_ref,
                         *, grid_shift, n_tok):
    cur = cur_ref[0]
    t, c = cur.shape
    row = lax.broadcasted_iota(jnp.int32, (t, c), 0)
    lane = lax.broadcasted_iota(jnp.int32, (t, c), 1)
    prev_tok = pltpu.roll(cur, 1, 0)
    next_tok = pltpu.roll(cur, t - 1, 0)
    if grid_shift:
        col = row & (GRID_W - 1)
        tok = row + pl.program_id(1) * t
        left = jnp.where(col > 0, prev_tok, 0.0)
        right = jnp.where(col < GRID_W - 1, next_tok, 0.0)
        up = jnp.where(tok >= GRID_W, jnp.concatenate([prev_ref[0], cur[:t - GRID_W]], axis=0), 0.0)
        down = jnp.where(tok < n_tok - GRID_W, jnp.concatenate([cur[GRID_W:], next_ref[0]], axis=0), 0.0)
        cm = lane & 3
        shifted = jnp.where(cm == 0, left, jnp.where(cm == 1, right, jnp.where(cm == 2, up, down)))
    else:
        prev_tok = jnp.where(row > 0, prev_tok, 0.0)
        next_tok = jnp.where(row < t - 1, next_tok, 0.0)
        shifted = jnp.where((lane & 1) == 0, prev_tok, next_tok)
    pm = cur + mu_ref[...] * (shifted - cur)

    r = pm[:, 0:D_RWKV]
    k = pm[:, D_RWKV:2 * D_RWKV]
    v = pm[:, 2 * D_RWKV:3 * D_RWKV]
    o = 3 * D_RWKV
    lw = pm[:, o:o + 2 * DECAY_LORA]
    la = pm[:, o + 2 * DECAY_LORA:o + 2 * (DECAY_LORA + AAA_LORA)]
    lg = pm[:, o + 2 * (DECAY_LORA + AAA_LORA):]

    w = w0_ref[...] + _dot_split(jnp.tanh(lw), w2_ref[0], w2_ref[1])
    log_decay = -EXP_NEG_HALF * _sigmoid(w)
    a = _sigmoid(a0_ref[...] + _dot_split(la, a2_ref[0], a2_ref[1]))
    gate = _dot_split(_sigmoid(lg), g2_ref[0], g2_ref[1])

    ones = ones_ref[...]
    kk_raw = k * kk_scale_ref[...]
    kk = kk_raw / jnp.maximum(jnp.sqrt(_segsum(kk_raw * kk_raw, ones)), 1e-12)
    ka = ka_ref[...]
    a0 = a[:, :D_RWKV]
    a1 = a[:, D_RWKV:]
    kd0 = k * (1.0 + (a0 - 1.0) * ka)
    kd1 = k * (1.0 + (a1 - 1.0) * ka)
    bonus = _segsum(r * (kd0 + kd1) * rk_ref[...], ones) * v

    r_ref[0] = r
    v_ref[0] = v
    kk_ref[0] = kk
    w_ref[0] = log_decay
    kd_ref[0] = jnp.concatenate([kd0, kd1], axis=-1)
    bb_ref[0] = jnp.concatenate([kk * a0, kk * a1], axis=-1)
    g_ref[0] = gate
    bonus_ref[0] = bonus


def _rwkv_prepare(pr, params, grid_shift):
    b, n, c = pr.shape
    t = PREPARE_ROWS
    if not grid_shift:
        assert n == t, "sequence token shift is written for a single tile"
    halo_blocks = n // GRID_W
    per_tile = t // GRID_W
    small = lambda shape: pl.BlockSpec(shape, lambda bi, i: (0,) * len(shape))
    tok_spec = lambda width: pl.BlockSpec((1, t, width), lambda bi, i: (bi, i, 0))
    out_widths = (D_RWKV, D_RWKV, D_RWKV, 2 * D_RWKV, 2 * D_RWKV, 2 * D_RWKV, D_RWKV, D_RWKV)
    kernel = functools.partial(_rwkv_prepare_kernel, grid_shift=grid_shift, n_tok=n)
    return pl.pallas_call(
        kernel,
        grid=(b, n // t),
        in_specs=[tok_spec(c),
                  pl.BlockSpec((1, GRID_W, c), lambda bi, i: (bi, jnp.maximum(i * per_tile - 1, 0), 0)),
                  pl.BlockSpec((1, GRID_W, c),
                               lambda bi, i: (bi, jnp.minimum((i + 1) * per_tile, halo_blocks - 1), 0)),
                  small((1, c)), small((1, 2 * D_RWKV)), small((2, 2 * DECAY_LORA, 2 * D_RWKV)),
                  small((1, 2 * D_RWKV)), small((2, 2 * AAA_LORA, 2 * D_RWKV)), small((2, GATE_LORA, D_RWKV)),
                  small((1, D_RWKV)), small((1, D_RWKV)), small((1, D_RWKV)), small((D_RWKV, D_RWKV))],
        out_specs=[tok_spec(wd) for wd in out_widths],
        out_shape=[jax.ShapeDtypeStruct((b, n, wd), F32) for wd in out_widths],
        compiler_params=_cparams(("arbitrary", "arbitrary")),
        name="rwkv_prepare",
    )(pr, pr, pr, *params)


def _bdot(a, b):
    return jnp.dot(a.astype(BF16), b.astype(BF16), preferred_element_type=F32)


def _bdot_nt(a, b):
    return lax.dot_general(a.astype(BF16), b.astype(BF16), (((1,), (1,)), ((), ())), preferred_element_type=F32)


def _bdot_tn(a, b):
    return lax.dot_general(a.astype(BF16), b.astype(BF16), (((0,), (0,)), ((), ())), preferred_element_type=F32)


def _wkv7_chunk_kernel(*refs, n_ctx_chunks):
    c = WKV_CHUNK
    p = 2 * c
    n_in = 12
    in_refs = (refs[:n_in], refs[n_in:2 * n_in])
    y_refs = refs[2 * n_in:2 * n_in + 2]
    state_ref = refs[2 * n_in + 2]
    n = pl.program_id(0)
    n_batch = in_refs[0][0].shape[0]
    pairs_per_batch = RWKV_HEADS // 2
    pairs_per_dir = n_batch * pairs_per_batch

    @pl.when(n == 0)
    def _():
        state_ref[...] = jnp.zeros_like(state_ref)

    is_ctx = n < n_ctx_chunks
    ti = lax.broadcasted_iota(jnp.int32, (c, c), 0)
    tj = lax.broadcasted_iota(jnp.int32, (c, c), 1)
    ri = lax.broadcasted_iota(jnp.int32, (p, p), 0)
    ci = lax.broadcasted_iota(jnp.int32, (p, p), 1)
    same_head = (ri >= c) == (ci >= c)
    ii = ri & (c - 1)
    jj = ci & (c - 1)
    eye = (ri == ci).astype(F32)
    first = lax.broadcasted_iota(jnp.int32, (c, p), 1) < RWKV_HEAD

    def stack(x):
        return jnp.concatenate([jnp.where(first, x, 0.0), jnp.where(first, 0.0, x)], axis=0)

    def unstack(x):
        return x[:c] + x[c:]

    n_sub = WKV_CHUNKS_PER_STEP

    def rows(d, sub):
        first_row = sub * c if d == 0 else (n_sub - 1 - sub) * c
        return slice(first_row, first_row + c)

    a_st, r_st, k_st, b_st, k2_st, b2_st, v_st, g_chunk, earlier, upto_self = ([] for _ in range(10))
    for d, sub in [(d, sub) for d in range(2) for sub in range(n_sub)]:
        r_l, v_l, kk_l, lw_l, kd_l, bb_l, r_c, v_c, kk_c, lw_c, kd_c, bb_c = in_refs[d]
        rs = rows(d, sub)
        pick = lambda xc, xl: jnp.concatenate(
            [jnp.where(is_ctx, xc[bi, rs, :], xl[bi, rs, :]) for bi in range(n_batch)], axis=-1)
        r, v, kk, lw, kd, bb = (pick(r_c, r_l), pick(v_c, v_l), pick(kk_c, kk_l), pick(lw_c, lw_l),
                                pick(kd_c, kd_l), pick(bb_c, bb_l))
        before = (tj < ti) if d == 0 else (tj > ti)
        upto = (before | (ti == tj)).astype(BF16)
        hi = lw.astype(BF16)
        r1 = lw - hi.astype(F32)
        mid = r1.astype(BF16)
        lo = (r1 - mid.astype(F32)).astype(BF16)
        cum = (jnp.dot(upto, hi, preferred_element_type=F32) + jnp.dot(upto, mid, preferred_element_type=F32)
               + jnp.dot(upto, lo, preferred_element_type=F32))
        tot = jnp.sum(lw, axis=0, keepdims=True)
        e_neg = jnp.exp(-cum)
        e_rem = jnp.exp(tot - cum)
        alpha = kk * jnp.exp(cum - lw)
        rho = r * jnp.exp(cum)
        beta = bb * e_neg
        kappa = kd * e_neg
        kappa_rem = kd * e_rem
        beta_rem = bb * e_rem
        g_all = jnp.exp(tot)
        pair_before = same_head & ((jj < ii) if d == 0 else (jj > ii))
        pair_upto = pair_before | (ri == ci)
        for hp in range(pairs_per_dir):
            sl = slice(hp * p, (hp + 1) * p)
            a_st.append(stack(alpha[:, sl]))
            r_st.append(stack(rho[:, sl]))
            k_st.append(stack(kappa[:, sl]))
            b_st.append(stack(beta[:, sl]))
            k2_st.append(stack(kappa_rem[:, sl]))
            b2_st.append(stack(beta_rem[:, sl]))
            v_st.append(stack(v[:, sl]))
            g_chunk.append(g_all[:, sl])
            earlier.append(pair_before)
            upto_self.append(pair_upto)

    pairs = range(2 * n_sub * pairs_per_dir)
    g = [_bdot_nt(jnp.concatenate([a_st[h], r_st[h]], axis=0), jnp.concatenate([k_st[h], b_st[h]], axis=0))
         for h in pairs]
    m1 = [jnp.where(earlier[h], g[h][:p, :p], 0.0) for h in pairs]
    m2 = [jnp.where(earlier[h], g[h][:p, p:], 0.0) for h in pairs]
    n1 = [jnp.where(upto_self[h], g[h][p:, :p], 0.0) for h in pairs]
    n2 = [jnp.where(upto_self[h], g[h][p:, p:], 0.0) for h in pairs]

    in_block = (ii >> 3) == (jj >> 3)
    pw = [-jnp.where(in_block, m2[h], 0.0) for h in pairs]
    inv = [eye + pw[h] for h in pairs]
    pw = [_bdot(pw[h], pw[h]) for h in pairs]
    both = [_bdot(jnp.concatenate([inv[h], pw[h]], axis=0), pw[h]) for h in pairs]
    inv = [inv[h] + both[h][:p] for h in pairs]
    inv = [inv[h] + _bdot(inv[h], both[h][p:]) for h in pairs]
    for sh in (3, 4, 5):
        off = ((ii >> (sh + 1)) == (jj >> (sh + 1))) & ((ii >> sh) != (jj >> sh))
        left = [_bdot(inv[h], jnp.where(off, m2[h], 0.0)) for h in pairs]
        inv = [inv[h] - _bdot(left[h], inv[h]) for h in pairs]

    mnv = [_bdot(jnp.concatenate([m1[h], n1[h]], axis=0), v_st[h]) for h in pairs]
    m1v = [mnv[h][:p] for h in pairs]
    n1v = [mnv[h][p:] for h in pairs]
    au = [_bdot(inv[h], jnp.concatenate([a_st[h], m1v[h]], axis=1)) for h in pairs]
    nn = [_bdot(n2[h], au[h]) for h in pairs]
    pc = [_bdot_tn(b2_st[h], au[h][:, :p]) for h in pairs]
    qc_t = [_bdot_tn(jnp.concatenate([v_st[h], -au[h][:, p:]], axis=0),
                     jnp.concatenate([k2_st[h], b2_st[h]], axis=0)) for h in pairs]
    rho_t = [unstack(r_st[h] - nn[h][:, :p]) for h in pairs]
    y_t = [unstack(n1v[h] - nn[h][:, p:]) for h in pairs]
    chains = [(d, hp) for d in range(2) for hp in range(pairs_per_dir)]
    state = [state_ref[ch] for ch in range(len(chains))]
    for sub in range(n_sub):
        item = lambda d, hp: (d * n_sub + sub) * pairs_per_dir + hp
        y = [_bdot_nt(rho_t[item(d, hp)], state[ch]) + y_t[item(d, hp)] for ch, (d, hp) in enumerate(chains)]
        s_dec = [_bdot_nt(state[ch], pc[item(d, hp)]) for ch, (d, hp) in enumerate(chains)]
        for ch, (d, hp) in enumerate(chains):
            h = item(d, hp)
            bi, hpb = divmod(hp, pairs_per_batch)
            y_refs[d][bi, rows(d, sub), hpb * p:(hpb + 1) * p] = y[ch]
            state[ch] = state[ch] * g_chunk[h] - s_dec[ch] + qc_t[h]
    for ch in range(len(chains)):
        state_ref[ch] = state[ch]


def _wkv7(lat, ctx, b, n_tok, n_ctx):
    c = WKV_CHUNKS_PER_STEP * WKV_CHUNK
    ncx = n_ctx // c
    nl = n_tok // c
    assert ncx * c == n_ctx and nl * c == n_tok
    lat_idx = (lambda n: jnp.maximum(n - ncx, 0), lambda n: nl - 1 - jnp.maximum(n - ncx, 0))
    ctx_idx = (lambda n: jnp.minimum(n, ncx - 1), lambda n: ncx - 1 - jnp.minimum(n, ncx - 1))

    def specs(idx, d):
        shared = pl.BlockSpec((b, c, D_RWKV), lambda n: (0, idx(n), 0))
        per_dir = pl.BlockSpec((b, c, D_RWKV), lambda n: (0, idx(n), d))
        return [shared, shared, shared, per_dir, per_dir, per_dir]

    in_specs, args = [], []
    for d in range(2):
        in_specs += specs(lat_idx[d], d) + specs(ctx_idx[d], d)
        args += list(lat) + list(ctx)
    return pl.pallas_call(
        functools.partial(_wkv7_chunk_kernel, n_ctx_chunks=ncx),
        grid=(ncx + nl,),
        in_specs=in_specs,
        out_specs=[pl.BlockSpec((b, c, D_RWKV), lambda n, d=d: (0, lat_idx[d](n), 0)) for d in range(2)],
        out_shape=[jax.ShapeDtypeStruct((b, n_tok, D_RWKV), F32)] * 2,
        scratch_shapes=[pltpu.VMEM((2 * b * RWKV_HEADS // 2, 2 * RWKV_HEAD, 2 * RWKV_HEAD), F32)],
        compiler_params=_cparams(("arbitrary",)),
        name="wkv7_chunk",
    )(*args)


def _rope(z, cos_t, sin_t):
    lane = lax.broadcasted_iota(jnp.int32, z.shape, 1)
    half = RET_HEAD // 4
    partner = jnp.where((lane & (2 * half - 1)) < half, pltpu.roll(z, RET_HEAD - half, 1), pltpu.roll(z, half, 1))
    return z * cos_t + partner * sin_t


def _retention_kernel(dec_ref, fwd_ref, bwd_ref, ctx_ref, cosf_ref, sinf_ref, cosb_ref, sinb_ref,
                      yf_ref, yb_ref, state_ref, dmat_ref, tail_ref, head_ref, cdec_ref):
    c = RET_CHUNK
    scale = RET_HEAD ** -0.5
    ii = lax.broadcasted_iota(jnp.int32, (c, c), 0)
    jj = lax.broadcasted_iota(jnp.int32, (c, c), 1)
    pos = lax.broadcasted_iota(jnp.int32, (c, RET_HEAD), 0).astype(F32)
    n_ctx_chunks = ctx_ref.shape[1] // c

    def head_slices(ref_val, h):
        q = ref_val[:, h * RET_HEAD:(h + 1) * RET_HEAD]
        k = ref_val[:, D_RET + h * RET_HEAD:D_RET + (h + 1) * RET_HEAD]
        v = ref_val[:, 2 * D_RET + h * RET_HEAD:2 * D_RET + (h + 1) * RET_HEAD]
        return q, k, v

    n_batch = fwd_ref.shape[0]
    heads = [(d, h) for d in range(2) for h in range(RET_HEADS)]
    chains = [(bi, d, h) for bi in range(n_batch) for d, h in heads]

    @pl.when(pl.program_id(0) == 0)
    def _():
        for d, h in heads:
            x = jnp.full((1, RET_HEAD), dec_ref[d, h], F32)
            lg = -(jnp.maximum(x, 0.0) + jnp.log(1.0 + jnp.exp(-jnp.abs(x))))
            chunk_decay = jnp.exp(lg * float(c))
            tail = jnp.exp(lg * ((c - 1.0 - pos) if d == 0 else pos))
            rel = (ii - jj) if d == 0 else (jj - ii)
            mask = (rel >= 0) if d == 0 else (rel > 0)
            dmat_ref[d, h] = jnp.where(mask, jnp.exp(lg * jnp.maximum(rel, 0).astype(F32)), 0.0)
            tail_ref[d, h] = tail
            head_ref[d, h] = jnp.exp(lg * ((pos + 1.0) if d == 0 else (c - pos)))
            cdec_ref[d, h] = jnp.broadcast_to(chunk_decay, (SUBLANES, RET_HEAD))
            order = range(n_ctx_chunks) if d == 0 else range(n_ctx_chunks - 1, -1, -1)
            for bi in range(n_batch):
                s = jnp.zeros((RET_HEAD, RET_HEAD), F32)
                for cc in order:
                    _, kc, vc = head_slices(ctx_ref[bi, cc * c:(cc + 1) * c, :], h)
                    s = s * chunk_decay + _bdot_tn(kc * scale * tail, vc)
                state_ref[bi, d, h] = s

    n_sub = RET_CHUNKS_PER_STEP

    def rows(d, sub):
        first = sub * c if d == 0 else (n_sub - 1 - sub) * c
        return slice(first, first + c)

    work = [(bi, d, h, sub) for sub in range(n_sub) for bi, d, h in chains]
    qkv = {}
    for bi, d, h, sub in work:
        blk = (fwd_ref if d == 0 else bwd_ref)[bi, rows(d, sub), :]
        cos_t = (cosf_ref if d == 0 else cosb_ref)[rows(d, sub), :]
        sin_t = (sinf_ref if d == 0 else sinb_ref)[rows(d, sub), :]
        q, k, v = head_slices(blk, h)
        qkv[bi, d, h, sub] = (_rope(q, cos_t, sin_t), _rope(k, cos_t, sin_t) * scale, v.astype(BF16))
    scores = {w: _bdot_nt(qkv[w][0], qkv[w][1]) for w in work}
    inner = {w: _bdot(scores[w] * dmat_ref[w[1], w[2]], qkv[w][2]) for w in work}
    upd = {w: _bdot_tn(qkv[w][1] * tail_ref[w[1], w[2]], qkv[w][2]) for w in work}
    state = {ch: state_ref[ch] for ch in chains}
    for sub in range(n_sub):
        cross = {ch: _bdot(qkv[ch + (sub,)][0] * head_ref[ch[1], ch[2]], state[ch]) for ch in chains}
        for bi, d, h in chains:
            ch = (bi, d, h)
            state[ch] = state[ch] * cdec_ref[d, h, 0:1, :] + upd[ch + (sub,)]
            out_ref = yf_ref if d == 0 else yb_ref
            out_ref[bi, rows(d, sub), h * RET_HEAD:(h + 1) * RET_HEAD] = inner[ch + (sub,)] + cross[ch]
    for ch in chains:
        state_ref[ch] = state[ch]


def _retention(pt, pt_ctx, ret_decay, cos_t, sin_t):
    b, n, _ = pt.shape
    c = RET_CHUNK
    rows = RET_CHUNKS_PER_STEP * c
    steps = n // rows
    qkv = 3 * D_RET
    fwd = lambda i: (0, i, 0)
    bwd = lambda i: (0, steps - 1 - i, 0)
    return pl.pallas_call(
        _retention_kernel,
        grid=(steps,),
        in_specs=[pl.BlockSpec(memory_space=pltpu.SMEM),
                  pl.BlockSpec((b, rows, qkv), fwd),
                  pl.BlockSpec((b, rows, qkv), bwd),
                  pl.BlockSpec((b, pt_ctx.shape[1], qkv), lambda i: (0, 0, 0)),
                  pl.BlockSpec((rows, RET_HEAD), lambda i: (i, 0)),
                  pl.BlockSpec((rows, RET_HEAD), lambda i: (i, 0)),
                  pl.BlockSpec((rows, RET_HEAD), lambda i: (steps - 1 - i, 0)),
                  pl.BlockSpec((rows, RET_HEAD), lambda i: (steps - 1 - i, 0))],
        out_specs=[pl.BlockSpec((b, rows, D_RET), fwd), pl.BlockSpec((b, rows, D_RET), bwd)],
        out_shape=[jax.ShapeDtypeStruct((b, n, D_RET), F32), jax.ShapeDtypeStruct((b, n, D_RET), F32)],
        scratch_shapes=[pltpu.VMEM((b, 2, RET_HEADS, RET_HEAD, RET_HEAD), F32),
                        pltpu.VMEM((2, RET_HEADS, c, c), F32),
                        pltpu.VMEM((2, RET_HEADS, c, RET_HEAD), F32),
                        pltpu.VMEM((2, RET_HEADS, c, RET_HEAD), F32),
                        pltpu.VMEM((2, RET_HEADS, SUBLANES, RET_HEAD), F32)],
        compiler_params=_cparams(("arbitrary",)),
        name="retention",
    )(ret_decay, pt, pt, pt_ctx, cos_t, sin_t, cos_t, sin_t)


def _rope_tables(n_tok):
    nf = RET_HEAD // 4
    lane = np.arange(RET_HEAD)
    inv = ROPE_BASE ** (-jnp.arange(nf, dtype=F32) / nf)
    t = jnp.arange(n_tok)
    pos = jnp.where((lane // (2 * nf) == 0)[None, :], (t // GRID_W)[:, None], (t % GRID_W)[:, None]).astype(F32)
    ang = pos * inv[lane % nf][None, :]
    sign = jnp.where((lane % (2 * nf)) < nf, -1.0, 1.0).astype(F32)
    return jnp.cos(ang), jnp.sin(ang) * sign[None, :]


def _group_norm(y, ones, seg, eps, g, b):
    mu = _segsum(y, ones) * (1.0 / seg)
    yc = y - mu
    var = _segsum(yc * yc, ones) * (1.0 / seg)
    return yc * lax.rsqrt(var + eps) * g + b


def _out_proj_kernel(x_ref, yf_ref, yb_ref, bonus_ref, gate_ref, tf_ref, tb_ref, gt_ref,
                     embg_ref, embb_ref, g1_ref, s2_ref, sh2_ref, rgn_g_ref, rgn_b_ref, tgn_g_ref, tgn_b_ref,
                     ones_r_ref, wout_ref, ln1g_ref, ln1b_ref, wrh_ref, wrl_ref, br_ref,
                     h1_ref, u2_ref, route_ref, route_t_ref, count_ref, carry_ref):
    @pl.when((pl.program_id(0) == 0) & (pl.program_id(1) == 0))
    def _():
        carry_ref[...] = jnp.zeros_like(carry_ref)

    y = yf_ref[0] + yb_ref[0]
    o_rwkv = _group_norm(y, ones_r_ref[...], RWKV_HEAD, RWKV_GN_EPS, rgn_g_ref[...], rgn_b_ref[...])
    o_rwkv = (o_rwkv + bonus_ref[0]) * gate_ref[0]
    yt = tf_ref[0] + tb_ref[0]
    gt = gt_ref[0]
    tgn_g = tgn_g_ref[...]
    tgn_b = tgn_b_ref[...]
    o_ret = jnp.concatenate(
        [_layer_norm(yt[:, h * RET_HEAD:(h + 1) * RET_HEAD], tgn_g[:, h * RET_HEAD:(h + 1) * RET_HEAD],
                     tgn_b[:, h * RET_HEAD:(h + 1) * RET_HEAD], RET_GN_EPS) for h in range(RET_HEADS)], axis=-1)
    o_ret = o_ret * (gt * _sigmoid(gt))
    cat = jnp.concatenate([o_rwkv, o_ret], axis=-1).astype(BF16)
    mix = jnp.dot(cat, wout_ref[...], preferred_element_type=F32)
    h = _layer_norm(x_ref[0], embg_ref[...], embb_ref[...])
    h1 = _layer_norm(DEEPNORM_ALPHA * h + g1_ref[0] * mix, ln1g_ref[...], ln1b_ref[...])
    u2 = h1 * (1.0 + s2_ref[0]) + sh2_ref[0]
    h1_ref[0] = h1
    _store_token_slabs(u2_ref, _pack_bf16_pairs(u2))
    route = _route_tile(_dot_split(u2, wrh_ref[...], wrl_ref[...]) + br_ref[...], carry_ref)
    route_ref[...] = route
    route_t_ref[...] = route.T[:SUBLANES]
    count_ref[...] = carry_ref[...]


def _out_proj(x, y_f, y_b, bonus, gate, t_f, t_b, pt, vecs, mats):
    b, n, d = x.shape
    t = OUT_PROJ_ROWS
    tok = lambda width: pl.BlockSpec((1, t, width), lambda bi, i: (bi, i, 0))
    per_b = pl.BlockSpec((1, 1, d), lambda bi, i: (bi, 0, 0))
    small = lambda arr: pl.BlockSpec(arr.shape, lambda bi, i: (0,) * arr.ndim)
    (embg, embb, g1, s2, sh2, rgn_g, rgn_b, tgn_g, tgn_b, ln1g, ln1b, br) = vecs
    (ones_r, wout, wr_hi, wr_lo) = mats
    gt_spec = pl.BlockSpec((1, t, D_RET), lambda bi, i: (bi, i, 3))
    tiles = n // t
    flat = lambda bi, i: (bi * tiles + i, 0)
    args = (x, y_f, y_b, bonus, gate, t_f, t_b, pt, embg, embb, g1, s2, sh2, rgn_g, rgn_b, tgn_g, tgn_b,
            ones_r, wout, ln1g, ln1b, wr_hi, wr_lo, br)
    in_specs = [tok(d)] + [tok(D_RWKV)] * 6 + [gt_spec, small(embg), small(embb), per_b, per_b, per_b,
                                                small(rgn_g), small(rgn_b), small(tgn_g), small(tgn_b),
                                                small(ones_r), small(wout), small(ln1g),
                                                small(ln1b), small(wr_hi), small(wr_lo), small(br)]
    return pl.pallas_call(
        _out_proj_kernel,
        grid=(b, n // t),
        in_specs=in_specs,
        out_specs=[tok(d), pl.BlockSpec((t * TOKEN_SLAB, LANES), flat),
                   pl.BlockSpec((t, LANES), flat),
                   pl.BlockSpec((SUBLANES, t), lambda bi, i: (0, bi * tiles + i)),
                   pl.BlockSpec((SUBLANES, LANES), lambda bi, i: (0, 0))],
        out_shape=[jax.ShapeDtypeStruct((b, n, d), F32),
                   jax.ShapeDtypeStruct((b * n * TOKEN_SLAB, LANES), jnp.uint32),
                   jax.ShapeDtypeStruct((b * n, LANES), F32),
                   jax.ShapeDtypeStruct((SUBLANES, b * n), F32),
                   jax.ShapeDtypeStruct((SUBLANES, LANES), F32)],
        scratch_shapes=[pltpu.VMEM((SUBLANES, LANES), F32)],
        compiler_params=_cparams(("arbitrary", "arbitrary")),
        name="out_proj",
    )(*args)


ROUTE_E1, ROUTE_E2, ROUTE_G1, ROUTE_G2, ROUTE_RANK1, ROUTE_RANK2 = range(6)


def _lane_argmax(x, valid, lane):
    m = jnp.max(jnp.where(valid, x, -jnp.inf), axis=-1, keepdims=True)
    idx = jnp.min(jnp.where(valid & (x == m), lane, float(LANES)), axis=-1, keepdims=True)
    return m, idx


def _route_tile(lg, carry_ref):
    t = lg.shape[0]
    lane = lax.broadcasted_iota(jnp.int32, lg.shape, 1).astype(F32)
    gmask = lane < N_GROUPS
    gmax = jnp.max(jnp.where(gmask, lg, -jnp.inf), axis=-1, keepdims=True)
    gexp = jnp.where(gmask, jnp.exp(lg - gmax), 0.0)
    gp = gexp / jnp.sum(gexp, axis=-1, keepdims=True)
    g_w, g_i = _lane_argmax(gp, gmask, lane)

    lo = N_GROUPS + EXPERTS_PER_GROUP * g_i
    emask = (lane >= lo) & (lane < lo + EXPERTS_PER_GROUP)
    emax = jnp.max(jnp.where(emask, lg, -jnp.inf), axis=-1, keepdims=True)
    eexp = jnp.where(emask, jnp.exp(lg - emax), 0.0)
    ep = eexp / jnp.sum(eexp, axis=-1, keepdims=True)
    p1, i1 = _lane_argmax(ep, emask, lane)
    p2, i2 = _lane_argmax(ep, emask & (lane != i1), lane)
    denom = p1 + p2
    gate1 = g_w * p1 / denom
    gate2 = g_w * p2 / denom
    e1 = i1 - N_GROUPS
    e2 = i2 - N_GROUPS

    oh1 = (lane == e1).astype(F32)
    oh2 = (lane == e2).astype(F32)
    cnt = oh1 + oh2
    ri = lax.broadcasted_iota(jnp.int32, (t, t), 0)
    ci = lax.broadcasted_iota(jnp.int32, (t, t), 1)
    before = (ci < ri).astype(BF16)
    seen = jnp.dot(before, cnt.astype(BF16), preferred_element_type=F32) + carry_ref[0:1, :]
    rank1 = jnp.sum(oh1 * seen, axis=-1, keepdims=True)
    rank2 = jnp.sum(oh2 * seen, axis=-1, keepdims=True)
    carry_ref[0:1, :] = carry_ref[0:1, :] + jnp.sum(cnt, axis=0, keepdims=True)

    out = jnp.zeros(lg.shape, F32)
    for slot, val in ((ROUTE_E1, e1.astype(F32)), (ROUTE_E2, e2.astype(F32)), (ROUTE_G1, gate1),
                      (ROUTE_G2, gate2), (ROUTE_RANK1, rank1), (ROUTE_RANK2, rank2)):
        out = jnp.where(lane == slot, val, out)
    return out


def _tile_gather_copy(src_hbm, idx_ref, buf, sem, slot, r):
    src = src_hbm.at[pl.ds(pl.multiple_of(idx_ref[0, 0, r], TOKEN_SLAB), TOKEN_SLAB), :]
    dst = buf.at[slot, pl.ds(pl.multiple_of(r * TOKEN_SLAB, TOKEN_SLAB), TOKEN_SLAB), :]
    return pltpu.make_async_copy(src, dst, sem.at[slot])


def _start_tile_gather(src_hbm, idx_ref, buf, sem, slot, rows, priorities):
    def body(g, carry):
        for j in range(GATHER_UNROLL):
            copy = _tile_gather_copy(src_hbm, idx_ref, buf, sem, slot, g * GATHER_UNROLL + j)
            copy.start(priority=priorities[j % len(priorities)])
        return carry
    lax.fori_loop(0, rows // GATHER_UNROLL, body, 0)


def _wait_tile_gather(src_hbm, buf, sem, slot, rows):
    whole = src_hbm.at[pl.ds(0, rows * TOKEN_SLAB), :]
    pltpu.make_async_copy(whole, buf.at[slot], sem.at[slot]).wait()


def _dispatch_kernel(d1_ref, d2_ref, u_hbm, x_init_hbm, x_hbm, ubuf, sem_in, sem_out):
    del x_init_hbm
    i = pl.program_id(0)
    n = pl.num_programs(0)
    rows = d1_ref.shape[2]
    tile_rows = rows * TOKEN_SLAB
    slot = i % DISPATCH_BUFFERS
    nxt = (i + 1) % DISPATCH_BUFFERS

    def load(tile, s):
        src = u_hbm.at[pl.ds(pl.multiple_of(tile * tile_rows, tile_rows), tile_rows), :]
        return pltpu.make_async_copy(src, ubuf.at[s], sem_in.at[s])

    def wait_copies(s):
        whole = x_hbm.at[pl.ds(0, tile_rows), :]
        pltpu.make_async_copy(ubuf.at[s], whole, sem_out.at[s]).wait()
        pltpu.make_async_copy(ubuf.at[s], whole, sem_out.at[s]).wait()

    @pl.when(i == 0)
    def _():
        load(0, 0).start()

    @pl.when(i + 1 >= DISPATCH_BUFFERS)
    def _():
        wait_copies(nxt)

    @pl.when(i + 1 < n)
    def _():
        load(i + 1, nxt).start()

    load(i, slot).wait()

    def slab(ref, first_row):
        return ref.at[pl.ds(pl.multiple_of(first_row, TOKEN_SLAB), TOKEN_SLAB), :]

    cur = ubuf.at[slot]

    def body(g, carry):
        for j in range(GATHER_UNROLL):
            r = g * GATHER_UNROLL + j
            src = slab(cur, r * TOKEN_SLAB)
            pltpu.make_async_copy(src, slab(x_hbm, d1_ref[0, 0, r]), sem_out.at[slot]).start(priority=j % 2)
            pltpu.make_async_copy(src, slab(x_hbm, d2_ref[0, 0, r]), sem_out.at[slot]).start(priority=(j + 1) % 2)
        return carry

    lax.fori_loop(0, rows // GATHER_UNROLL, body, 0)

    @pl.when(i == n - 1)
    def _():
        for back in range(DISPATCH_BUFFERS - 1):
            @pl.when(i - back >= 0)
            def _():
                wait_copies((i - back) % DISPATCH_BUFFERS)


def _dispatch(u2_slabs, dest1, dest2, n_slot):
    n = dest1.shape[0]
    t = DISPATCH_ROWS
    nt = n // t
    idx = pl.BlockSpec((1, 1, t), lambda i: (i, 0, 0), memory_space=pltpu.SMEM)
    any_spec = pl.BlockSpec(memory_space=pl.ANY)
    x_init = jnp.zeros((n_slot * TOKEN_SLAB, LANES), jnp.uint32)
    return pl.pallas_call(
        _dispatch_kernel,
        grid=(nt,),
        in_specs=[idx, idx, any_spec, any_spec],
        out_specs=any_spec,
        out_shape=jax.ShapeDtypeStruct(x_init.shape, x_init.dtype),
        scratch_shapes=[pltpu.VMEM((DISPATCH_BUFFERS, t * TOKEN_SLAB, LANES), jnp.uint32),
                        pltpu.SemaphoreType.DMA((DISPATCH_BUFFERS,)),
                        pltpu.SemaphoreType.DMA((DISPATCH_BUFFERS,))],
        input_output_aliases={3: 0},
        compiler_params=_cparams(("arbitrary",)),
        name="dispatch",
    )(dest1.reshape(nt, 1, t), dest2.reshape(nt, 1, t), u2_slabs, x_init)


def _expert_kernel(blk_expert_ref, n_used_ref, x_ref, wg_ref, wu_ref, wd_ref, y_ref, wg_s, wu_s, wd_s):
    i = pl.program_id(0)
    n_used = n_used_ref[0]

    @pl.when(i >= n_used)
    def _():
        y_ref[...] = jnp.zeros_like(y_ref)

    @pl.when(i < n_used)
    def _():
        @pl.when((i == 0) | (blk_expert_ref[i] != blk_expert_ref[jnp.maximum(i - 1, 0)]))
        def _():
            wg_s[...] = wg_ref[0].astype(BF16)
            wu_s[...] = wu_ref[0].astype(BF16)
            wd_s[...] = wd_ref[0].astype(BF16)

        x = _unpack_bf16_pairs(_load_token_slabs(x_ref, EXPERT_ROWS)).astype(BF16)
        hg = jnp.dot(x, wg_s[...], preferred_element_type=F32)
        hu = jnp.dot(x, wu_s[...], preferred_element_type=F32)
        act = (hg * _sigmoid(hg) * hu).astype(BF16)
        _store_token_slabs(y_ref, _pack_bf16_pairs(jnp.dot(act, wd_s[...], preferred_element_type=F32)))


def _expert_mlp(x_slabs, block_expert, n_used, w_gate, w_up, w_down):
    n_blk = block_expert.shape[0]
    d = w_gate.shape[1]
    hdim = w_gate.shape[2]
    rows = EXPERT_ROWS
    used = lambda i, nu: jnp.minimum(i, nu[0] - 1)
    weight = lambda i, be, nu: (be[used(i, nu)], 0, 0)
    grid_spec = pltpu.PrefetchScalarGridSpec(
        num_scalar_prefetch=2,
        grid=(n_blk,),
        in_specs=[pl.BlockSpec((rows * TOKEN_SLAB, LANES), lambda i, be, nu: (used(i, nu), 0)),
                  pl.BlockSpec((1, d, hdim), weight),
                  pl.BlockSpec((1, d, hdim), weight),
                  pl.BlockSpec((1, hdim, d), weight)],
        out_specs=pl.BlockSpec((rows * TOKEN_SLAB, LANES), lambda i, be, nu: (i, 0)),
        scratch_shapes=[pltpu.VMEM((d, hdim), BF16), pltpu.VMEM((d, hdim), BF16), pltpu.VMEM((hdim, d), BF16)],
    )
    return pl.pallas_call(
        _expert_kernel,
        grid_spec=grid_spec,
        out_shape=jax.ShapeDtypeStruct((n_blk * rows * TOKEN_SLAB, LANES), jnp.uint32),
        compiler_params=_cparams(("arbitrary",)),
        name="expert_mlp",
    )(block_expert, n_used, x_slabs, w_gate, w_up, w_down)


def _combine_kernel(d1_ref, d2_ref, d1n_ref, d2n_ref, y_hbm, route_ref, h1_ref, g2_ref, lng_ref, lnb_ref,
                    o_ref, abuf, bbuf, sem_a, sem_b):
    i = pl.program_id(0)
    n = pl.num_programs(0)
    slot = i % 2
    rows = o_ref.shape[0]

    @pl.when(i == 0)
    def _():
        _start_tile_gather(y_hbm, d1_ref, abuf, sem_a, 0, rows, GATHER_PRIORITIES)
        _start_tile_gather(y_hbm, d2_ref, bbuf, sem_b, 0, rows, GATHER_PRIORITIES)

    @pl.when(i + 1 < n)
    def _():
        _start_tile_gather(y_hbm, d1n_ref, abuf, sem_a, 1 - slot, rows, GATHER_PRIORITIES)
        _start_tile_gather(y_hbm, d2n_ref, bbuf, sem_b, 1 - slot, rows, GATHER_PRIORITIES)

    _wait_tile_gather(y_hbm, abuf, sem_a, slot, rows)
    _wait_tile_gather(y_hbm, bbuf, sem_b, slot, rows)
    route = route_ref[...]
    f = (_unpack_bf16_pairs(_load_token_slabs(abuf.at[slot], rows)) * route[:, ROUTE_G1:ROUTE_G1 + 1]
         + _unpack_bf16_pairs(_load_token_slabs(bbuf.at[slot], rows)) * route[:, ROUTE_G2:ROUTE_G2 + 1])
    o_ref[...] = _layer_norm(DEEPNORM_ALPHA * h1_ref[...] + g2_ref[0] * f, lng_ref[...], lnb_ref[...])


def _combine(y_tiles, dest1, dest2, route, h1, g2, ln_g, ln_b, tokens_per_batch):
    n, d = h1.shape
    t = COMBINE_ROWS
    nt = n // t
    per_b = tokens_per_batch // t
    d1 = dest1.reshape(nt, 1, t)
    d2 = dest2.reshape(nt, 1, t)
    cur = pl.BlockSpec((1, 1, t), lambda i: (i, 0, 0), memory_space=pltpu.SMEM)
    nxt = pl.BlockSpec((1, 1, t), lambda i: (jnp.minimum(i + 1, nt - 1), 0, 0), memory_space=pltpu.SMEM)
    small = lambda arr: pl.BlockSpec(arr.shape, lambda i: (0,) * arr.ndim)
    return pl.pallas_call(
        _combine_kernel,
        grid=(nt,),
        in_specs=[cur, cur, nxt, nxt, pl.BlockSpec(memory_space=pl.ANY),
                  pl.BlockSpec((t, LANES), lambda i: (i, 0)),
                  pl.BlockSpec((t, d), lambda i: (i, 0)),
                  pl.BlockSpec((1, 1, d), lambda i: (i // per_b, 0, 0)),
                  small(ln_g), small(ln_b)],
        out_specs=pl.BlockSpec((t, d), lambda i: (i, 0)),
        out_shape=jax.ShapeDtypeStruct((n, d), F32),
        scratch_shapes=[pltpu.VMEM((2, t * TOKEN_SLAB, LANES), jnp.uint32),
                        pltpu.VMEM((2, t * TOKEN_SLAB, LANES), jnp.uint32),
                        pltpu.SemaphoreType.DMA((2,)), pltpu.SemaphoreType.DMA((2,))],
        compiler_params=_cparams(("arbitrary",)),
        name="combine",
    )(d1, d2, d1, d2, y_tiles, route, h1, g2, ln_g, ln_b)


def _hi_lo(w):
    hi = w.astype(BF16)
    return jnp.stack([hi, (w - hi.astype(F32)).astype(BF16)])


def _block_diag2(w):
    z = jnp.zeros_like(w[0])
    return jnp.concatenate([jnp.concatenate([w[0], z], axis=1), jnp.concatenate([z, w[1]], axis=1)], axis=0)


def kernel(x, c, ctx, c_ctx, emb_ln_g, emb_ln_b, w_mod, b_mod, w_in, tshift_mu, rwkv_w0, rwkv_w2, rwkv_a0, rwkv_a2, rwkv_g2, rwkv_k_k, rwkv_k_a, rwkv_r_k, rwkv_gn_g, rwkv_gn_b, ret_decay, ret_gn_g, ret_gn_b, w_out, ln1_g, ln1_b, router_group, router_group_bias, router_expert, router_expert_bias, expert_w_gate, expert_w_up, expert_w_down, ln2_g, ln2_b):
    assert w_mod.shape[0] == 1, "written for DEPTH == 1 (context outputs are never emitted)"
    b, n_tok, d = x.shape
    n_ctx = ctx.shape[1]
    row = lambda v: v.reshape(1, -1)

    c_rows = jnp.zeros((SUBLANES, d), F32).at[:b].set(c).at[b].set(c_ctx)
    mod = _modulation(c_rows, w_mod[0], row(b_mod[0]))
    sh1, s1, g1, sh2, s2, g2 = [mod[:b, j * d:(j + 1) * d].reshape(b, 1, d) for j in range(6)]
    sh1c, s1c = [jnp.broadcast_to(mod[b, j * d:(j + 1) * d].reshape(1, 1, d), (b, 1, d)) for j in range(2)]

    w_in_bf16 = w_in[0].astype(BF16)
    pr, pt = _in_proj(x, row(emb_ln_g), row(emb_ln_b), s1, sh1, w_in_bf16)
    pr_c, pt_c = _in_proj(ctx, row(emb_ln_g), row(emb_ln_b), s1c, sh1c, w_in_bf16)

    prep_params = (row(tshift_mu[0]), row(rwkv_w0[0]), _hi_lo(_block_diag2(rwkv_w2[0])), row(rwkv_a0[0]),
                   _hi_lo(_block_diag2(rwkv_a2[0])), _hi_lo(rwkv_g2[0]), row(rwkv_k_k[0]), row(rwkv_k_a[0]),
                   row(rwkv_r_k[0]),
                   _segment_ones(D_RWKV, RWKV_HEAD))
    lat = _rwkv_prepare(pr, prep_params, grid_shift=True)
    cx = _rwkv_prepare(pr_c, prep_params, grid_shift=False)
    r_l, v_l, kk_l, w_l, kd_l, bb_l, gate_l, bonus_l = lat
    r_c, v_c, kk_c, w_c, kd_c, bb_c, _, _ = cx

    y_f, y_b = _wkv7((r_l, v_l, kk_l, w_l, kd_l, bb_l), (r_c, v_c, kk_c, w_c, kd_c, bb_c), b, n_tok, n_ctx)

    cos_t, sin_t = _rope_tables(n_tok)
    t_f, t_b = _retention(pt, pt_c, ret_decay[0], cos_t, sin_t)

    wr = jnp.zeros((d, LANES), F32).at[:, :N_GROUPS].set(router_group[0])
    wr = wr.at[:, N_GROUPS:N_GROUPS + N_EXPERTS].set(router_expert[0])
    br = jnp.zeros((1, LANES), F32).at[0, :N_GROUPS].set(router_group_bias[0])
    br = br.at[0, N_GROUPS:N_GROUPS + N_EXPERTS].set(router_expert_bias[0].reshape(-1))
    vecs = (row(emb_ln_g), row(emb_ln_b), g1, s2, sh2, row(rwkv_gn_g[0]), row(rwkv_gn_b[0]),
            row(ret_gn_g[0]), row(ret_gn_b[0]), row(ln1_g[0]), row(ln1_b[0]), br)
    wr_hi_lo = _hi_lo(wr)
    mats = (_segment_ones(D_RWKV, RWKV_HEAD), w_out[0].astype(BF16), wr_hi_lo[0], wr_hi_lo[1])
    h1, u2, route, route_t, counts = _out_proj(x, y_f, y_b, bonus_l, gate_l, t_f, t_b, pt, vecs, mats)

    n_all = b * n_tok

    e1 = route_t[ROUTE_E1].astype(jnp.int32)
    e2 = route_t[ROUTE_E2].astype(jnp.int32)
    cnt = counts[0, :N_EXPERTS].astype(jnp.int32)
    padded = ((cnt + EXPERT_ROWS - 1) // EXPERT_ROWS) * EXPERT_ROWS
    pends = jnp.cumsum(padded)
    pstarts = pends - padded
    expert_ids = jnp.arange(N_EXPERTS, dtype=jnp.int32)
    start_of = lambda e: jnp.sum(jnp.where(e[:, None] == expert_ids[None, :], pstarts[None, :], 0), axis=1)
    dest1 = (start_of(e1) + route_t[ROUTE_RANK1].astype(jnp.int32)) * TOKEN_SLAB
    dest2 = (start_of(e2) + route_t[ROUTE_RANK2].astype(jnp.int32)) * TOKEN_SLAB
    n_blk = -(-(n_all * 2) // EXPERT_ROWS) + N_EXPERTS
    block_start = jnp.arange(n_blk, dtype=jnp.int32) * EXPERT_ROWS
    block_expert = jnp.minimum(jnp.sum((block_start[:, None] >= pends[None, :]).astype(jnp.int32), axis=1),
                               N_EXPERTS - 1)

    n_used = (pends[N_EXPERTS - 1:] // EXPERT_ROWS).astype(jnp.int32)
    x_slabs = _dispatch(u2, dest1, dest2, n_blk * EXPERT_ROWS)
    y_tiles = _expert_mlp(x_slabs, block_expert, n_used, expert_w_gate[0], expert_w_up[0], expert_w_down[0])
    out = _combine(y_tiles, dest1, dest2, route, h1.reshape(n_all, d), g2, row(ln2_g[0]), row(ln2_b[0]), n_tok)
    return out.reshape(b, n_tok, d)
```

```python
import functools
import math

import jax
import jax.numpy as jnp
import numpy as np
from jax import lax
from jax.experimental import pallas as pl
from jax.experimental.pallas import tpu as pltpu

F32 = jnp.float32
BF16 = jnp.bfloat16
HIGHEST = lax.Precision.HIGHEST

GRID_W = 64
D_RWKV = 512
RWKV_HEAD = 64
RWKV_HEADS = D_RWKV // RWKV_HEAD
DECAY_LORA = 64
AAA_LORA = 64
GATE_LORA = 128
D_RET = 512
RET_HEADS = 4
RET_HEAD = D_RET // RET_HEADS
RET_CHUNK = 128
RET_CHUNKS_PER_STEP = 4
RWKV_COLS = 3 * D_RWKV + 2 * (DECAY_LORA + AAA_LORA) + GATE_LORA
RET_COLS = 4 * D_RET
N_GROUPS = 4
EXPERTS_PER_GROUP = 8
N_EXPERTS = N_GROUPS * EXPERTS_PER_GROUP
EXPERT_HIDDEN = 512
ROPE_BASE = 10000.0
LN_EPS = 1e-5
RWKV_GN_EPS = 64e-5
RET_GN_EPS = 1e-5
DEEPNORM_ALPHA = 2.0 ** 0.25
EXP_NEG_HALF = math.exp(-0.5)

LANES = 128
SUBLANES = 8
VMEM_LIMIT_BYTES = 56 * 1024 * 1024

PREPARE_ROWS = 256
MODULATION_COLS = 1536

WKV_CHUNK = 64
WKV_CHUNKS_PER_STEP = 2


IN_PROJ_ROWS = 512
OUT_PROJ_ROWS = 512

TOKEN_SLAB = 4
EXPERT_ROWS = 512
COMBINE_ROWS = 512
DISPATCH_ROWS = 512
DISPATCH_BUFFERS = 3
GATHER_UNROLL = 16
GATHER_PRIORITIES = (0, 1)


def _pack_bf16_pairs(x):
    half = x.shape[1] // 2

    def bf16_bits(v):
        b = lax.bitcast_convert_type(v, jnp.uint32)
        return (b + jnp.uint32(0x7FFF) + ((b >> 16) & jnp.uint32(1))) >> 16

    return bf16_bits(x[:, :half]) | (bf16_bits(x[:, half:]) << 16)


def _unpack_bf16_pairs(p):
    lo = lax.bitcast_convert_type(p << 16, F32)
    hi = lax.bitcast_convert_type(p & jnp.uint32(0xFFFF0000), F32)
    return jnp.concatenate([lo, hi], axis=-1)


def _store_token_slabs(ref, x):
    rows = x.shape[0]
    for j in range(TOKEN_SLAB):
        ref[pl.ds(j, rows, stride=TOKEN_SLAB), :] = x[:, j * LANES:(j + 1) * LANES]


def _load_token_slabs(ref, rows):
    return jnp.concatenate([ref[pl.ds(j, rows, stride=TOKEN_SLAB), :] for j in range(TOKEN_SLAB)], axis=-1)


def _cparams(sem):
    return pltpu.CompilerParams(dimension_semantics=sem, vmem_limit_bytes=VMEM_LIMIT_BYTES)


def _layer_norm(x, g, b, eps=LN_EPS):
    mu = jnp.mean(x, axis=-1, keepdims=True)
    xc = x - mu
    var = jnp.mean(xc * xc, axis=-1, keepdims=True)
    return xc * lax.rsqrt(var + eps) * g + b


def _sigmoid(x):
    return 1.0 / (1.0 + jnp.exp(-x))


def _split_bf16(x):
    hi = x.astype(BF16)
    return hi, (x - hi.astype(F32)).astype(BF16)


def _segsum(x, ones_bf16):
    t = x.shape[0]
    s = jnp.dot(jnp.concatenate(_split_bf16(x), axis=0), ones_bf16, preferred_element_type=F32)
    return s[:t] + s[t:]


def _dot_split(x, w_hi, w_lo):
    hi, lo = _split_bf16(x)
    acc = jnp.dot(hi, w_hi, preferred_element_type=F32)
    acc = acc + jnp.dot(lo, w_hi, preferred_element_type=F32)
    return acc + jnp.dot(hi, w_lo, preferred_element_type=F32)


def _segment_ones(width, seg):
    idx = np.arange(width) // seg
    return jnp.asarray(idx[:, None] == idx[None, :], dtype=BF16)


def _mod_kernel(c_ref, w_ref, b_ref, o_ref):
    c = c_ref[...]
    sc = c * _sigmoid(c)
    o_ref[...] = jnp.dot(sc, w_ref[...], precision=HIGHEST, preferred_element_type=F32) + b_ref[...]


def _modulation(c_rows, w_mod, b_mod):
    rows, d = c_rows.shape
    n = w_mod.shape[1]
    tn = MODULATION_COLS
    return pl.pallas_call(
        _mod_kernel,
        grid=(n // tn,),
        in_specs=[pl.BlockSpec((rows, d), lambda j: (0, 0)),
                  pl.BlockSpec((d, tn), lambda j: (0, j)),
                  pl.BlockSpec((1, tn), lambda j: (0, j))],
        out_specs=pl.BlockSpec((rows, tn), lambda j: (0, j)),
        out_shape=jax.ShapeDtypeStruct((rows, n), F32),
        compiler_params=_cparams(("arbitrary",)),
        name="modulation",
    )(c_rows, w_mod, b_mod)


def _in_proj_kernel(x_ref, g_ref, b_ref, s_ref, sh_ref, w_ref, pr_ref, pt_ref):
    h = _layer_norm(x_ref[0], g_ref[...], b_ref[...])
    u = h * (1.0 + s_ref[0]) + sh_ref[0]
    p = jnp.dot(u.astype(BF16), w_ref[...], preferred_element_type=F32)
    pr_ref[0] = p[:, :RWKV_COLS]
    pt_ref[0] = p[:, RWKV_COLS:]


def _in_proj(x, ln_g, ln_b, s1, sh1, w_in_bf16):
    b, n, d = x.shape
    tm = min(IN_PROJ_ROWS, n)
    cols = w_in_bf16.shape[1]
    return pl.pallas_call(
        _in_proj_kernel,
        grid=(b, n // tm),
        in_specs=[pl.BlockSpec((1, tm, d), lambda bi, i: (bi, i, 0)),
                  pl.BlockSpec((1, d), lambda bi, i: (0, 0)),
                  pl.BlockSpec((1, d), lambda bi, i: (0, 0)),
                  pl.BlockSpec((1, 1, d), lambda bi, i: (bi, 0, 0)),
                  pl.BlockSpec((1, 1, d), lambda bi, i: (bi, 0, 0)),
                  pl.BlockSpec((d, cols), lambda bi, i: (0, 0))],
        out_specs=[pl.BlockSpec((1, tm, RWKV_COLS), lambda bi, i: (bi, i, 0)),
                   pl.BlockSpec((1, tm, RET_COLS), lambda bi, i: (bi, i, 0))],
        out_shape=[jax.ShapeDtypeStruct((b, n, RWKV_COLS), F32),
                   jax.ShapeDtypeStruct((b, n, RET_COLS), F32)],
        compiler_params=_cparams(("arbitrary", "arbitrary")),
        name="in_proj",
    )(x, ln_g, ln_b, s1, sh1, w_in_bf16)


def _rwkv_prepare_kernel(cur_ref, prev_ref, next_ref, mu_ref, w0_ref, w2_ref, a0_ref, a2_ref, g2_ref,
                         kk_scale_ref, ka_ref, rk_ref, ones_ref,
                         r_ref, v_ref, kk_ref, w_ref, kd_ref, bb_ref, g_ref, bonus_ref,
                         *, grid_shift, n_tok):
    cur = cur_ref[0]
    t, c = cur.shape
    row = lax.broadcasted_iota(jnp.int32, (t, c), 0)
    lane = lax.broadcasted_iota(jnp.int32, (t, c), 1)
    prev_tok = pltpu.roll(cur, 1, 0)
    next_tok = pltpu.roll(cur, t - 1, 0)
    if grid_shift:
        col = row & (GRID_W - 1)
        tok = row + pl.program_id(1) * t
        left = jnp.where(col > 0, prev_tok, 0.0)
        right = jnp.where(col < GRID_W - 1, next_tok, 0.0)
        up = jnp.where(tok >= GRID_W, jnp.concatenate([prev_ref[0], cur[:t - GRID_W]], axis=0), 0.0)
        down = jnp.where(tok < n_tok - GRID_W, jnp.concatenate([cur[GRID_W:], next_ref[0]], axis=0), 0.0)
        cm = lane & 3
        shifted = jnp.where(cm == 0, left, jnp.where(cm == 1, right, jnp.where(cm == 2, up, down)))
    else:
        prev_tok = jnp.where(row > 0, prev_tok, 0.0)
        next_tok = jnp.where(row < t - 1, next_tok, 0.0)
        shifted = jnp.where((lane & 1) == 0, prev_tok, next_tok)
    pm = cur + mu_ref[...] * (shifted - cur)

    r = pm[:, 0:D_RWKV]
    k = pm[:, D_RWKV:2 * D_RWKV]
    v = pm[:, 2 * D_RWKV:3 * D_RWKV]
    o = 3 * D_RWKV
    lw = pm[:, o:o + 2 * DECAY_LORA]
    la = pm[:, o + 2 * DECAY_LORA:o + 2 * (DECAY_LORA + AAA_LORA)]
    lg = pm[:, o + 2 * (DECAY_LORA + AAA_LORA):]

    w = w0_ref[...] + _dot_split(jnp.tanh(lw), w2_ref[0], w2_ref[1])
    log_decay = -EXP_NEG_HALF * _sigmoid(w)
    a = _sigmoid(a0_ref[...] + _dot_split(la, a2_ref[0], a2_ref[1]))
    gate = _dot_split(_sigmoid(lg), g2_ref[0], g2_ref[1])

    ones = ones_ref[...]
    kk_raw = k * kk_scale_ref[...]
    kk = kk_raw / jnp.maximum(jnp.sqrt(_segsum(kk_raw * kk_raw, ones)), 1e-12)
    ka = ka_ref[...]
    a0 = a[:, :D_RWKV]
    a1 = a[:, D_RWKV:]
    kd0 = k * (1.0 + (a0 - 1.0) * ka)
    kd1 = k * (1.0 + (a1 - 1.0) * ka)
    bonus = _segsum(r * (kd0 + kd1) * rk_ref[...], ones) * v

    r_ref[0] = r
    v_ref[0] = v
    kk_ref[0] = kk
    w_ref[0] = log_decay
    kd_ref[0] = jnp.concatenate([kd0, kd1], axis=-1)
    bb_ref[0] = jnp.concatenate([kk * a0, kk * a1], axis=-1)
    g_ref[0] = gate
    bonus_ref[0] = bonus


def _rwkv_prepare(pr, params, grid_shift):
    b, n, c = pr.shape
    t = PREPARE_ROWS
    if not grid_shift:
        assert n == t, "sequence token shift is written for a single tile"
    halo_blocks = n // GRID_W
    per_tile = t // GRID_W
    small = lambda shape: pl.BlockSpec(shape, lambda bi, i: (0,) * len(shape))
    tok_spec = lambda width: pl.BlockSpec((1, t, width), lambda bi, i: (bi, i, 0))
    out_widths = (D_RWKV, D_RWKV, D_RWKV, 2 * D_RWKV, 2 * D_RWKV, 2 * D_RWKV, D_RWKV, D_RWKV)
    kernel = functools.partial(_rwkv_prepare_kernel, grid_shift=grid_shift, n_tok=n)
    return pl.pallas_call(
        kernel,
        grid=(b, n // t),
        in_specs=[tok_spec(c),
                  pl.BlockSpec((1, GRID_W, c), lambda bi, i: (bi, jnp.maximum(i * per_tile - 1, 0), 0)),
                  pl.BlockSpec((1, GRID_W, c),
                               lambda bi, i: (bi, jnp.minimum((i + 1) * per_tile, halo_blocks - 1), 0)),
                  small((1, c)), small((1, 2 * D_RWKV)), small((2, 2 * DECAY_LORA, 2 * D_RWKV)),
                  small((1, 2 * D_RWKV)), small((2, 2 * AAA_LORA, 2 * D_RWKV)), small((2, GATE_LORA, D_RWKV)),
                  small((1, D_RWKV)), small((1, D_RWKV)), small((1, D_RWKV)), small((D_RWKV, D_RWKV))],
        out_specs=[tok_spec(wd) for wd in out_widths],
        out_shape=[jax.ShapeDtypeStruct((b, n, wd), F32) for wd in out_widths],
        compiler_params=_cparams(("arbitrary", "arbitrary")),
        name="rwkv_prepare",
    )(pr, pr, pr, *params)


def _bdot(a, b):
    return jnp.dot(a.astype(BF16), b.astype(BF16), preferred_element_type=F32)


def _bdot_nt(a, b):
    return lax.dot_general(a.astype(BF16), b.astype(BF16), (((1,), (1,)), ((), ())), preferred_element_type=F32)


def _bdot_tn(a, b):
    return lax.dot_general(a.astype(BF16), b.astype(BF16), (((0,), (0,)), ((), ())), preferred_element_type=F32)


def _wkv7_chunk_kernel(*refs, n_ctx_chunks):
    c = WKV_CHUNK
    p = 2 * c
    n_in = 12
    in_refs = (refs[:n_in], refs[n_in:2 * n_in])
    y_refs = refs[2 * n_in:2 * n_in + 2]
    state_ref = refs[2 * n_in + 2]
    n = pl.program_id(0)
    n_batch = in_refs[0][0].shape[0]
    pairs_per_batch = RWKV_HEADS // 2
    pairs_per_dir = n_batch * pairs_per_batch

    @pl.when(n == 0)
    def _():
        state_ref[...] = jnp.zeros_like(state_ref)

    is_ctx = n < n_ctx_chunks
    ti = lax.broadcasted_iota(jnp.int32, (c, c), 0)
    tj = lax.broadcasted_iota(jnp.int32, (c, c), 1)
    ri = lax.broadcasted_iota(jnp.int32, (p, p), 0)
    ci = lax.broadcasted_iota(jnp.int32, (p, p), 1)
    same_head = (ri >= c) == (ci >= c)
    ii = ri & (c - 1)
    jj = ci & (c - 1)
    eye = (ri == ci).astype(F32)
    first = lax.broadcasted_iota(jnp.int32, (c, p), 1) < RWKV_HEAD

    def stack(x):
        return jnp.concatenate([jnp.where(first, x, 0.0), jnp.where(first, 0.0, x)], axis=0)

    def unstack(x):
        return x[:c] + x[c:]

    n_sub = WKV_CHUNKS_PER_STEP

    def rows(d, sub):
        first_row = sub * c if d == 0 else (n_sub - 1 - sub) * c
        return slice(first_row, first_row + c)

    a_st, r_st, k_st, b_st, k2_st, b2_st, v_st, g_chunk, earlier, upto_self = ([] for _ in range(10))
    for d, sub in [(d, sub) for d in range(2) for sub in range(n_sub)]:
        r_l, v_l, kk_l, lw_l, kd_l, bb_l, r_c, v_c, kk_c, lw_c, kd_c, bb_c = in_refs[d]
        rs = rows(d, sub)
        pick = lambda xc, xl: jnp.concatenate(
            [jnp.where(is_ctx, xc[bi, rs, :], xl[bi, rs, :]) for bi in range(n_batch)], axis=-1)
        r, v, kk, lw, kd, bb = (pick(r_c, r_l), pick(v_c, v_l), pick(kk_c, kk_l), pick(lw_c, lw_l),
                                pick(kd_c, kd_l), pick(bb_c, bb_l))
        before = (tj < ti) if d == 0 else (tj > ti)
        upto = (before | (ti == tj)).astype(BF16)
        hi = lw.astype(BF16)
        r1 = lw - hi.astype(F32)
        mid = r1.astype(BF16)
        lo = (r1 - mid.astype(F32)).astype(BF16)
        cum = (jnp.dot(upto, hi, preferred_element_type=F32) + jnp.dot(upto, mid, preferred_element_type=F32)
               + jnp.dot(upto, lo, preferred_element_type=F32))
        tot = jnp.sum(lw, axis=0, keepdims=True)
        e_neg = jnp.exp(-cum)
        e_rem = jnp.exp(tot - cum)
        alpha = kk * jnp.exp(cum - lw)
        rho = r * jnp.exp(cum)
        beta = bb * e_neg
        kappa = kd * e_neg
        kappa_rem = kd * e_rem
        beta_rem = bb * e_rem
        g_all = jnp.exp(tot)
        pair_before = same_head & ((jj < ii) if d == 0 else (jj > ii))
        pair_upto = pair_before | (ri == ci)
        for hp in range(pairs_per_dir):
            sl = slice(hp * p, (hp + 1) * p)
            a_st.append(stack(alpha[:, sl]))
            r_st.append(stack(rho[:, sl]))
            k_st.append(stack(kappa[:, sl]))
            b_st.append(stack(beta[:, sl]))
            k2_st.append(stack(kappa_rem[:, sl]))
            b2_st.append(stack(beta_rem[:, sl]))
            v_st.append(stack(v[:, sl]))
            g_chunk.append(g_all[:, sl])
            earlier.append(pair_before)
            upto_self.append(pair_upto)

    pairs = range(2 * n_sub * pairs_per_dir)
    g = [_bdot_nt(jnp.concatenate([a_st[h], r_st[h]], axis=0), jnp.concatenate([k_st[h], b_st[h]], axis=0))
         for h in pairs]
    m1 = [jnp.where(earlier[h], g[h][:p, :p], 0.0) for h in pairs]
    m2 = [jnp.where(earlier[h], g[h][:p, p:], 0.0) for h in pairs]
    n1 = [jnp.where(upto_self[h], g[h][p:, :p], 0.0) for h in pairs]
    n2 = [jnp.where(upto_self[h], g[h][p:, p:], 0.0) for h in pairs]

    in_block = (ii >> 3) == (jj >> 3)
    pw = [-jnp.where(in_block, m2[h], 0.0) for h in pairs]
    inv = [eye + pw[h] for h in pairs]
    pw = [_bdot(pw[h], pw[h]) for h in pairs]
    both = [_bdot(jnp.concatenate([inv[h], pw[h]], axis=0), pw[h]) for h in pairs]
    inv = [inv[h] + both[h][:p] for h in pairs]
    inv = [inv[h] + _bdot(inv[h], both[h][p:]) for h in pairs]
    for sh in (3, 4, 5):
        off = ((ii >> (sh + 1)) == (jj >> (sh + 1))) & ((ii >> sh) != (jj >> sh))
        left = [_bdot(inv[h], jnp.where(off, m2[h], 0.0)) for h in pairs]
        inv = [inv[h] - _bdot(left[h], inv[h]) for h in pairs]

    mnv = [_bdot(jnp.concatenate([m1[h], n1[h]], axis=0), v_st[h]) for h in pairs]
    m1v = [mnv[h][:p] for h in pairs]
    n1v = [mnv[h][p:] for h in pairs]
    au = [_bdot(inv[h], jnp.concatenate([a_st[h], m1v[h]], axis=1)) for h in pairs]
    nn = [_bdot(n2[h], au[h]) for h in pairs]
    pc = [_bdot_tn(b2_st[h], au[h][:, :p]) for h in pairs]
    qc_t = [_bdot_tn(jnp.concatenate([v_st[h], -au[h][:, p:]], axis=0),
                     jnp.concatenate([k2_st[h], b2_st[h]], axis=0)) for h in pairs]
    rho_t = [unstack(r_st[h] - nn[h][:, :p]) for h in pairs]
    y_t = [unstack(n1v[h] - nn[h][:, p:]) for h in pairs]
    chains = [(d, hp) for d in range(2) for hp in range(pairs_per_dir)]
    state = [state_ref[ch] for ch in range(len(chains))]
    for sub in range(n_sub):
        item = lambda d, hp: (d * n_sub + sub) * pairs_per_dir + hp
        y = [_bdot_nt(rho_t[item(d, hp)], state[ch]) + y_t[item(d, hp)] for ch, (d, hp) in enumerate(chains)]
        s_dec = [_bdot_nt(state[ch], pc[item(d, hp)]) for ch, (d, hp) in enumerate(chains)]
        for ch, (d, hp) in enumerate(chains):
            h = item(d, hp)
            bi, hpb = divmod(hp, pairs_per_batch)
            y_refs[d][bi, rows(d, sub), hpb * p:(hpb + 1) * p] = y[ch]
            state[ch] = state[ch] * g_chunk[h] - s_dec[ch] + qc_t[h]
    for ch in range(len(chains)):
        state_ref[ch] = state[ch]


def _wkv7(lat, ctx, b, n_tok, n_ctx):
    c = WKV_CHUNKS_PER_STEP * WKV_CHUNK
    ncx = n_ctx // c
    nl = n_tok // c
    assert ncx * c == n_ctx and nl * c == n_tok
    lat_idx = (lambda n: jnp.maximum(n - ncx, 0), lambda n: nl - 1 - jnp.maximum(n - ncx, 0))
    ctx_idx = (lambda n: jnp.minimum(n, ncx - 1), lambda n: ncx - 1 - jnp.minimum(n, ncx - 1))

    def specs(idx, d):
        shared = pl.BlockSpec((b, c, D_RWKV), lambda n: (0, idx(n), 0))
        per_dir = pl.BlockSpec((b, c, D_RWKV), lambda n: (0, idx(n), d))
        return [shared, shared, shared, per_dir, per_dir, per_dir]

    in_specs, args = [], []
    for d in range(2):
        in_specs += specs(lat_idx[d], d) + specs(ctx_idx[d], d)
        args += list(lat) + list(ctx)
    return pl.pallas_call(
        functools.partial(_wkv7_chunk_kernel, n_ctx_chunks=ncx),
        grid=(ncx + nl,),
        in_specs=in_specs,
        out_specs=[pl.BlockSpec((b, c, D_RWKV), lambda n, d=d: (0, lat_idx[d](n), 0)) for d in range(2)],
        out_shape=[jax.ShapeDtypeStruct((b, n_tok, D_RWKV), F32)] * 2,
        scratch_shapes=[pltpu.VMEM((2 * b * RWKV_HEADS // 2, 2 * RWKV_HEAD, 2 * RWKV_HEAD), F32)],
        compiler_params=_cparams(("arbitrary",)),
        name="wkv7_chunk",
    )(*args)


def _rope(z, cos_t, sin_t):
    lane = lax.broadcasted_iota(jnp.int32, z.shape, 1)
    half = RET_HEAD // 4
    partner = jnp.where((lane & (2 * half - 1)) < half, pltpu.roll(z, RET_HEAD - half, 1), pltpu.roll(z, half, 1))
    return z * cos_t + partner * sin_t


def _retention_kernel(dec_ref, fwd_ref, bwd_ref, ctx_ref, cosf_ref, sinf_ref, cosb_ref, sinb_ref,
                      yf_ref, yb_ref, state_ref, dmat_ref, tail_ref, head_ref, cdec_ref):
    c = RET_CHUNK
    scale = RET_HEAD ** -0.5
    ii = lax.broadcasted_iota(jnp.int32, (c, c), 0)
    jj = lax.broadcasted_iota(jnp.int32, (c, c), 1)
    pos = lax.broadcasted_iota(jnp.int32, (c, RET_HEAD), 0).astype(F32)
    n_ctx_chunks = ctx_ref.shape[1] // c

    def head_slices(ref_val, h):
        q = ref_val[:, h * RET_HEAD:(h + 1) * RET_HEAD]
        k = ref_val[:, D_RET + h * RET_HEAD:D_RET + (h + 1) * RET_HEAD]
        v = ref_val[:, 2 * D_RET + h * RET_HEAD:2 * D_RET + (h + 1) * RET_HEAD]
        return q, k, v

    n_batch = fwd_ref.shape[0]
    heads = [(d, h) for d in range(2) for h in range(RET_HEADS)]
    chains = [(bi, d, h) for bi in range(n_batch) for d, h in heads]

    @pl.when(pl.program_id(0) == 0)
    def _():
        for d, h in heads:
            x = jnp.full((1, RET_HEAD), dec_ref[d, h], F32)
            lg = -(jnp.maximum(x, 0.0) + jnp.log(1.0 + jnp.exp(-jnp.abs(x))))
            chunk_decay = jnp.exp(lg * float(c))
            tail = jnp.exp(lg * ((c - 1.0 - pos) if d == 0 else pos))
            rel = (ii - jj) if d == 0 else (jj - ii)
            mask = (rel >= 0) if d == 0 else (rel > 0)
            dmat_ref[d, h] = jnp.where(mask, jnp.exp(lg * jnp.maximum(rel, 0).astype(F32)), 0.0)
            tail_ref[d, h] = tail
            head_ref[d, h] = jnp.exp(lg * ((pos + 1.0) if d == 0 else (c - pos)))
            cdec_ref[d, h] = jnp.broadcast_to(chunk_decay, (SUBLANES, RET_HEAD))
            order = range(n_ctx_chunks) if d == 0 else range(n_ctx_chunks - 1, -1, -1)
            for bi in range(n_batch):
                s = jnp.zeros((RET_HEAD, RET_HEAD), F32)
                for cc in order:
                    _, kc, vc = head_slices(ctx_ref[bi, cc * c:(cc + 1) * c, :], h)
                    s = s * chunk_decay + _bdot_tn(kc * scale * tail, vc)
                state_ref[bi, d, h] = s

    n_sub = RET_CHUNKS_PER_STEP

    def rows(d, sub):
        first = sub * c if d == 0 else (n_sub - 1 - sub) * c
        return slice(first, first + c)

    work = [(bi, d, h, sub) for sub in range(n_sub) for bi, d, h in chains]
    qkv = {}
    for bi, d, h, sub in work:
        blk = (fwd_ref if d == 0 else bwd_ref)[bi, rows(d, sub), :]
        cos_t = (cosf_ref if d == 0 else cosb_ref)[rows(d, sub), :]
        sin_t = (sinf_ref if d == 0 else sinb_ref)[rows(d, sub), :]
        q, k, v = head_slices(blk, h)
        qkv[bi, d, h, sub] = (_rope(q, cos_t, sin_t), _rope(k, cos_t, sin_t) * scale, v.astype(BF16))
    scores = {w: _bdot_nt(qkv[w][0], qkv[w][1]) for w in work}
    inner = {w: _bdot(scores[w] * dmat_ref[w[1], w[2]], qkv[w][2]) for w in work}
    upd = {w: _bdot_tn(qkv[w][1] * tail_ref[w[1], w[2]], qkv[w][2]) for w in work}
    state = {ch: state_ref[ch] for ch in chains}
    for sub in range(n_sub):
        cross = {ch: _bdot(qkv[ch + (sub,)][0] * head_ref[ch[1], ch[2]], state[ch]) for ch in chains}
        for bi, d, h in chains:
            ch = (bi, d, h)
            state[ch] = state[ch] * cdec_ref[d, h, 0:1, :] + upd[ch + (sub,)]
            out_ref = yf_ref if d == 0 else yb_ref
            out_ref[bi, rows(d, sub), h * RET_HEAD:(h + 1) * RET_HEAD] = inner[ch + (sub,)] + cross[ch]
    for ch in chains:
        state_ref[ch] = state[ch]


def _retention(pt, pt_ctx, ret_decay, cos_t, sin_t):
    b, n, _ = pt.shape
    c = RET_CHUNK
    rows = RET_CHUNKS_PER_STEP * c
    steps = n // rows
    qkv = 3 * D_RET
    fwd = lambda i: (0, i, 0)
    bwd = lambda i: (0, steps - 1 - i, 0)
    return pl.pallas_call(
        _retention_kernel,
        grid=(steps,),
        in_specs=[pl.BlockSpec(memory_space=pltpu.SMEM),
                  pl.BlockSpec((b, rows, qkv), fwd),
                  pl.BlockSpec((b, rows, qkv), bwd),
                  pl.BlockSpec((b, pt_ctx.shape[1], qkv), lambda i: (0, 0, 0)),
                  pl.BlockSpec((rows, RET_HEAD), lambda i: (i, 0)),
                  pl.BlockSpec((rows, RET_HEAD), lambda i: (i, 0)),
                  pl.BlockSpec((rows, RET_HEAD), lambda i: (steps - 1 - i, 0)),
                  pl.BlockSpec((rows, RET_HEAD), lambda i: (steps - 1 - i, 0))],
        out_specs=[pl.BlockSpec((b, rows, D_RET), fwd), pl.BlockSpec((b, rows, D_RET), bwd)],
        out_shape=[jax.ShapeDtypeStruct((b, n, D_RET), F32), jax.ShapeDtypeStruct((b, n, D_RET), F32)],
        scratch_shapes=[pltpu.VMEM((b, 2, RET_HEADS, RET_HEAD, RET_HEAD), F32),
                        pltpu.VMEM((2, RET_HEADS, c, c), F32),
                        pltpu.VMEM((2, RET_HEADS, c, RET_HEAD), F32),
                        pltpu.VMEM((2, RET_HEADS, c, RET_HEAD), F32),
                        pltpu.VMEM((2, RET_HEADS, SUBLANES, RET_HEAD), F32)],
        compiler_params=_cparams(("arbitrary",)),
        name="retention",
    )(ret_decay, pt, pt, pt_ctx, cos_t, sin_t, cos_t, sin_t)


def _rope_tables(n_tok):
    nf = RET_HEAD // 4
    lane = np.arange(RET_HEAD)
    inv = ROPE_BASE ** (-jnp.arange(nf, dtype=F32) / nf)
    t = jnp.arange(n_tok)
    pos = jnp.where((lane // (2 * nf) == 0)[None, :], (t // GRID_W)[:, None], (t % GRID_W)[:, None]).astype(F32)
    ang = pos * inv[lane % nf][None, :]
    sign = jnp.where((lane % (2 * nf)) < nf, -1.0, 1.0).astype(F32)
    return jnp.cos(ang), jnp.sin(ang) * sign[None, :]


def _group_norm(y, ones, seg, eps, g, b):
    mu = _segsum(y, ones) * (1.0 / seg)
    yc = y - mu
    var = _segsum(yc * yc, ones) * (1.0 / seg)
    return yc * lax.rsqrt(var + eps) * g + b


def _out_proj_kernel(x_ref, yf_ref, yb_ref, bonus_ref, gate_ref, tf_ref, tb_ref, gt_ref,
                     embg_ref, embb_ref, g1_ref, s2_ref, sh2_ref, rgn_g_ref, rgn_b_ref, tgn_g_ref, tgn_b_ref,
                     ones_r_ref, wout_ref, ln1g_ref, ln1b_ref, wrh_ref, wrl_ref, br_ref,
                     h1_ref, u2_ref, route_ref, route_t_ref, count_ref, carry_ref):
    @pl.when((pl.program_id(0) == 0) & (pl.program_id(1) == 0))
    def _():
        carry_ref[...] = jnp.zeros_like(carry_ref)

    y = yf_ref[0] + yb_ref[0]
    o_rwkv = _group_norm(y, ones_r_ref[...], RWKV_HEAD, RWKV_GN_EPS, rgn_g_ref[...], rgn_b_ref[...])
    o_rwkv = (o_rwkv + bonus_ref[0]) * gate_ref[0]
    yt = tf_ref[0] + tb_ref[0]
    gt = gt_ref[0]
    tgn_g = tgn_g_ref[...]
    tgn_b = tgn_b_ref[...]
    o_ret = jnp.concatenate(
        [_layer_norm(yt[:, h * RET_HEAD:(h + 1) * RET_HEAD], tgn_g[:, h * RET_HEAD:(h + 1) * RET_HEAD],
                     tgn_b[:, h * RET_HEAD:(h + 1) * RET_HEAD], RET_GN_EPS) for h in range(RET_HEADS)], axis=-1)
    o_ret = o_ret * (gt * _sigmoid(gt))
    cat = jnp.concatenate([o_rwkv, o_ret], axis=-1).astype(BF16)
    mix = jnp.dot(cat, wout_ref[...], preferred_element_type=F32)
    h = _layer_norm(x_ref[0], embg_ref[...], embb_ref[...])
    h1 = _layer_norm(DEEPNORM_ALPHA * h + g1_ref[0] * mix, ln1g_ref[...], ln1b_ref[...])
    u2 = h1 * (1.0 + s2_ref[0]) + sh2_ref[0]
    h1_ref[0] = h1
    _store_token_slabs(u2_ref, _pack_bf16_pairs(u2))
    route = _route_tile(_dot_split(u2, wrh_ref[...], wrl_ref[...]) + br_ref[...], carry_ref)
    route_ref[...] = route
    route_t_ref[...] = route.T[:SUBLANES]
    count_ref[...] = carry_ref[...]


def _out_proj(x, y_f, y_b, bonus, gate, t_f, t_b, pt, vecs, mats):
    b, n, d = x.shape
    t = OUT_PROJ_ROWS
    tok = lambda width: pl.BlockSpec((1, t, width), lambda bi, i: (bi, i, 0))
    per_b = pl.BlockSpec((1, 1, d), lambda bi, i: (bi, 0, 0))
    small = lambda arr: pl.BlockSpec(arr.shape, lambda bi, i: (0,) * arr.ndim)
    (embg, embb, g1, s2, sh2, rgn_g, rgn_b, tgn_g, tgn_b, ln1g, ln1b, br) = vecs
    (ones_r, wout, wr_hi, wr_lo) = mats
    gt_spec = pl.BlockSpec((1, t, D_RET), lambda bi, i: (bi, i, 3))
    tiles = n // t
    flat = lambda bi, i: (bi * tiles + i, 0)
    args = (x, y_f, y_b, bonus, gate, t_f, t_b, pt, embg, embb, g1, s2, sh2, rgn_g, rgn_b, tgn_g, tgn_b,
            ones_r, wout, ln1g, ln1b, wr_hi, wr_lo, br)
    in_specs = [tok(d)] + [tok(D_RWKV)] * 6 + [gt_spec, small(embg), small(embb), per_b, per_b, per_b,
                                                small(rgn_g), small(rgn_b), small(tgn_g), small(tgn_b),
                                                small(ones_r), small(wout), small(ln1g),
                                                small(ln1b), small(wr_hi), small(wr_lo), small(br)]
    return pl.pallas_call(
        _out_proj_kernel,
        grid=(b, n // t),
        in_specs=in_specs,
        out_specs=[tok(d), pl.BlockSpec((t * TOKEN_SLAB, LANES), flat),
                   pl.BlockSpec((t, LANES), flat),
                   pl.BlockSpec((SUBLANES, t), lambda bi, i: (0, bi * tiles + i)),
                   pl.BlockSpec((SUBLANES, LANES), lambda bi, i: (0, 0))],
        out_shape=[jax.ShapeDtypeStruct((b, n, d), F32),
                   jax.ShapeDtypeStruct((b * n * TOKEN_SLAB, LANES), jnp.uint32),
                   jax.ShapeDtypeStruct((b * n, LANES), F32),
                   jax.ShapeDtypeStruct((SUBLANES, b * n), F32),
                   jax.ShapeDtypeStruct((SUBLANES, LANES), F32)],
        scratch_shapes=[pltpu.VMEM((SUBLANES, LANES), F32)],
        compiler_params=_cparams(("arbitrary", "arbitrary")),
        name="out_proj",
    )(*args)


ROUTE_E1, ROUTE_E2, ROUTE_G1, ROUTE_G2, ROUTE_RANK1, ROUTE_RANK2 = range(6)


def _lane_argmax(x, valid, lane):
    m = jnp.max(jnp.where(valid, x, -jnp.inf), axis=-1, keepdims=True)
    idx = jnp.min(jnp.where(valid & (x == m), lane, float(LANES)), axis=-1, keepdims=True)
    return m, idx


def _route_tile(lg, carry_ref):
    t = lg.shape[0]
    lane = lax.broadcasted_iota(jnp.int32, lg.shape, 1).astype(F32)
    gmask = lane < N_GROUPS
    gmax = jnp.max(jnp.where(gmask, lg, -jnp.inf), axis=-1, keepdims=True)
    gexp = jnp.where(gmask, jnp.exp(lg - gmax), 0.0)
    gp = gexp / jnp.sum(gexp, axis=-1, keepdims=True)
    g_w, g_i = _lane_argmax(gp, gmask, lane)

    lo = N_GROUPS + EXPERTS_PER_GROUP * g_i
    emask = (lane >= lo) & (lane < lo + EXPERTS_PER_GROUP)
    emax = jnp.max(jnp.where(emask, lg, -jnp.inf), axis=-1, keepdims=True)
    eexp = jnp.where(emask, jnp.exp(lg - emax), 0.0)
    ep = eexp / jnp.sum(eexp, axis=-1, keepdims=True)
    p1, i1 = _lane_argmax(ep, emask, lane)
    p2, i2 = _lane_argmax(ep, emask & (lane != i1), lane)
    denom = p1 + p2
    gate1 = g_w * p1 / denom
    gate2 = g_w * p2 / denom
    e1 = i1 - N_GROUPS
    e2 = i2 - N_GROUPS

    oh1 = (lane == e1).astype(F32)
    oh2 = (lane == e2).astype(F32)
    cnt = oh1 + oh2
    ri = lax.broadcasted_iota(jnp.int32, (t, t), 0)
    ci = lax.broadcasted_iota(jnp.int32, (t, t), 1)
    before = (ci < ri).astype(BF16)
    seen = jnp.dot(before, cnt.astype(BF16), preferred_element_type=F32) + carry_ref[0:1, :]
    rank1 = jnp.sum(oh1 * seen, axis=-1, keepdims=True)
    rank2 = jnp.sum(oh2 * seen, axis=-1, keepdims=True)
    carry_ref[0:1, :] = carry_ref[0:1, :] + jnp.sum(cnt, axis=0, keepdims=True)

    out = jnp.zeros(lg.shape, F32)
    for slot, val in ((ROUTE_E1, e1.astype(F32)), (ROUTE_E2, e2.astype(F32)), (ROUTE_G1, gate1),
                      (ROUTE_G2, gate2), (ROUTE_RANK1, rank1), (ROUTE_RANK2, rank2)):
        out = jnp.where(lane == slot, val, out)
    return out


def _tile_gather_copy(src_hbm, idx_ref, buf, sem, slot, r):
    src = src_hbm.at[pl.ds(pl.multiple_of(idx_ref[0, 0, r], TOKEN_SLAB), TOKEN_SLAB), :]
    dst = buf.at[slot, pl.ds(pl.multiple_of(r * TOKEN_SLAB, TOKEN_SLAB), TOKEN_SLAB), :]
    return pltpu.make_async_copy(src, dst, sem.at[slot])


def _start_tile_gather(src_hbm, idx_ref, buf, sem, slot, rows, priorities):
    def body(g, carry):
        for j in range(GATHER_UNROLL):
            copy = _tile_gather_copy(src_hbm, idx_ref, buf, sem, slot, g * GATHER_UNROLL + j)
            copy.start(priority=priorities[j % len(priorities)])
        return carry
    lax.fori_loop(0, rows // GATHER_UNROLL, body, 0)


def _wait_tile_gather(src_hbm, buf, sem, slot, rows):
    whole = src_hbm.at[pl.ds(0, rows * TOKEN_SLAB), :]
    pltpu.make_async_copy(whole, buf.at[slot], sem.at[slot]).wait()


def _dispatch_kernel(d1_ref, d2_ref, u_hbm, x_init_hbm, x_hbm, ubuf, sem_in, sem_out):
    del x_init_hbm
    i = pl.program_id(0)
    n = pl.num_programs(0)
    rows = d1_ref.shape[2]
    tile_rows = rows * TOKEN_SLAB
    slot = i % DISPATCH_BUFFERS
    nxt = (i + 1) % DISPATCH_BUFFERS

    def load(tile, s):
        src = u_hbm.at[pl.ds(pl.multiple_of(tile * tile_rows, tile_rows), tile_rows), :]
        return pltpu.make_async_copy(src, ubuf.at[s], sem_in.at[s])

    def wait_copies(s):
        whole = x_hbm.at[pl.ds(0, tile_rows), :]
        pltpu.make_async_copy(ubuf.at[s], whole, sem_out.at[s]).wait()
        pltpu.make_async_copy(ubuf.at[s], whole, sem_out.at[s]).wait()

    @pl.when(i == 0)
    def _():
        load(0, 0).start()

    @pl.when(i + 1 >= DISPATCH_BUFFERS)
    def _():
        wait_copies(nxt)

    @pl.when(i + 1 < n)
    def _():
        load(i + 1, nxt).start()

    load(i, slot).wait()

    def slab(ref, first_row):
        return ref.at[pl.ds(pl.multiple_of(first_row, TOKEN_SLAB), TOKEN_SLAB), :]

    cur = ubuf.at[slot]

    def body(g, carry):
        for j in range(GATHER_UNROLL):
            r = g * GATHER_UNROLL + j
            src = slab(cur, r * TOKEN_SLAB)
            pltpu.make_async_copy(src, slab(x_hbm, d1_ref[0, 0, r]), sem_out.at[slot]).start(priority=j % 2)
            pltpu.make_async_copy(src, slab(x_hbm, d2_ref[0, 0, r]), sem_out.at[slot]).start(priority=(j + 1) % 2)
        return carry

    lax.fori_loop(0, rows // GATHER_UNROLL, body, 0)

    @pl.when(i == n - 1)
    def _():
        for back in range(DISPATCH_BUFFERS - 1):
            @pl.when(i - back >= 0)
            def _():
                wait_copies((i - back) % DISPATCH_BUFFERS)


def _dispatch(u2_slabs, dest1, dest2, n_slot):
    n = dest1.shape[0]
    t = DISPATCH_ROWS
    nt = n // t
    idx = pl.BlockSpec((1, 1, t), lambda i: (i, 0, 0), memory_space=pltpu.SMEM)
    any_spec = pl.BlockSpec(memory_space=pl.ANY)
    x_init = jnp.zeros((n_slot * TOKEN_SLAB, LANES), jnp.uint32)
    return pl.pallas_call(
        _dispatch_kernel,
        grid=(nt,),
        in_specs=[idx, idx, any_spec, any_spec],
        out_specs=any_spec,
        out_shape=jax.ShapeDtypeStruct(x_init.shape, x_init.dtype),
        scratch_shapes=[pltpu.VMEM((DISPATCH_BUFFERS, t * TOKEN_SLAB, LANES), jnp.uint32),
                        pltpu.SemaphoreType.DMA((DISPATCH_BUFFERS,)),
                        pltpu.SemaphoreType.DMA((DISPATCH_BUFFERS,))],
        input_output_aliases={3: 0},
        compiler_params=_cparams(("arbitrary",)),
        name="dispatch",
    )(dest1.reshape(nt, 1, t), dest2.reshape(nt, 1, t), u2_slabs, x_init)


def _expert_kernel(blk_expert_ref, n_used_ref, x_ref, wg_ref, wu_ref, wd_ref, y_ref, wg_s, wu_s, wd_s):
    i = pl.program_id(0)
    n_used = n_used_ref[0]

    @pl.when(i >= n_used)
    def _():
        y_ref[...] = jnp.zeros_like(y_ref)

    @pl.when(i < n_used)
    def _():
        @pl.when((i == 0) | (blk_expert_ref[i] != blk_expert_ref[jnp.maximum(i - 1, 0)]))
        def _():
            wg_s[...] = wg_ref[0].astype(BF16)
            wu_s[...] = wu_ref[0].astype(BF16)
            wd_s[...] = wd_ref[0].astype(BF16)

        x = _unpack_bf16_pairs(_load_token_slabs(x_ref, EXPERT_ROWS)).astype(BF16)
        hg = jnp.dot(x, wg_s[...], preferred_element_type=F32)
        hu = jnp.dot(x, wu_s[...], preferred_element_type=F32)
        act = (hg * _sigmoid(hg) * hu).astype(BF16)
        _store_token_slabs(y_ref, _pack_bf16_pairs(jnp.dot(act, wd_s[...], preferred_element_type=F32)))


def _expert_mlp(x_slabs, block_expert, n_used, w_gate, w_up, w_down):
    n_blk = block_expert.shape[0]
    d = w_gate.shape[1]
    hdim = w_gate.shape[2]
    rows = EXPERT_ROWS
    used = lambda i, nu: jnp.minimum(i, nu[0] - 1)
    weight = lambda i, be, nu: (be[used(i, nu)], 0, 0)
    grid_spec = pltpu.PrefetchScalarGridSpec(
        num_scalar_prefetch=2,
        grid=(n_blk,),
        in_specs=[pl.BlockSpec((rows * TOKEN_SLAB, LANES), lambda i, be, nu: (used(i, nu), 0)),
                  pl.BlockSpec((1, d, hdim), weight),
                  pl.BlockSpec((1, d, hdim), weight),
                  pl.BlockSpec((1, hdim, d), weight)],
        out_specs=pl.BlockSpec((rows * TOKEN_SLAB, LANES), lambda i, be, nu: (i, 0)),
        scratch_shapes=[pltpu.VMEM((d, hdim), BF16), pltpu.VMEM((d, hdim), BF16), pltpu.VMEM((hdim, d), BF16)],
    )
    return pl.pallas_call(
        _expert_kernel,
        grid_spec=grid_spec,
        out_shape=jax.ShapeDtypeStruct((n_blk * rows * TOKEN_SLAB, LANES), jnp.uint32),
        compiler_params=_cparams(("arbitrary",)),
        name="expert_mlp",
    )(block_expert, n_used, x_slabs, w_gate, w_up, w_down)


def _combine_kernel(d1_ref, d2_ref, d1n_ref, d2n_ref, y_hbm, route_ref, h1_ref, g2_ref, lng_ref, lnb_ref,
                    o_ref, abuf, bbuf, sem_a, sem_b):
    i = pl.program_id(0)
    n = pl.num_programs(0)
    slot = i % 2
    rows = o_ref.shape[0]

    @pl.when(i == 0)
    def _():
        _start_tile_gather(y_hbm, d1_ref, abuf, sem_a, 0, rows, GATHER_PRIORITIES)
        _start_tile_gather(y_hbm, d2_ref, bbuf, sem_b, 0, rows, GATHER_PRIORITIES)

    @pl.when(i + 1 < n)
    def _():
        _start_tile_gather(y_hbm, d1n_ref, abuf, sem_a, 1 - slot, rows, GATHER_PRIORITIES)
        _start_tile_gather(y_hbm, d2n_ref, bbuf, sem_b, 1 - slot, rows, GATHER_PRIORITIES)

    _wait_tile_gather(y_hbm, abuf, sem_a, slot, rows)
    _wait_tile_gather(y_hbm, bbuf, sem_b, slot, rows)
    route = route_ref[...]
    f = (_unpack_bf16_pairs(_load_token_slabs(abuf.at[slot], rows)) * route[:, ROUTE_G1:ROUTE_G1 + 1]
         + _unpack_bf16_pairs(_load_token_slabs(bbuf.at[slot], rows)) * route[:, ROUTE_G2:ROUTE_G2 + 1])
    o_ref[...] = _layer_norm(DEEPNORM_ALPHA * h1_ref[...] + g2_ref[0] * f, lng_ref[...], lnb_ref[...])


def _combine(y_tiles, dest1, dest2, route, h1, g2, ln_g, ln_b, tokens_per_batch):
    n, d = h1.shape
    t = COMBINE_ROWS
    nt = n // t
    per_b = tokens_per_batch // t
    d1 = dest1.reshape(nt, 1, t)
    d2 = dest2.reshape(nt, 1, t)
    cur = pl.BlockSpec((1, 1, t), lambda i: (i, 0, 0), memory_space=pltpu.SMEM)
    nxt = pl.BlockSpec((1, 1, t), lambda i: (jnp.minimum(i + 1, nt - 1), 0, 0), memory_space=pltpu.SMEM)
    small = lambda arr: pl.BlockSpec(arr.shape, lambda i: (0,) * arr.ndim)
    return pl.pallas_call(
        _combine_kernel,
        grid=(nt,),
        in_specs=[cur, cur, nxt, nxt, pl.BlockSpec(memory_space=pl.ANY),
                  pl.BlockSpec((t, LANES), lambda i: (i, 0)),
                  pl.BlockSpec((t, d), lambda i: (i, 0)),
                  pl.BlockSpec((1, 1, d), lambda i: (i // per_b, 0, 0)),
                  small(ln_g), small(ln_b)],
        out_specs=pl.BlockSpec((t, d), lambda i: (i, 0)),
        out_shape=jax.ShapeDtypeStruct((n, d), F32),
        scratch_shapes=[pltpu.VMEM((2, t * TOKEN_SLAB, LANES), jnp.uint32),
                        pltpu.VMEM((2, t * TOKEN_SLAB, LANES), jnp.uint32),
                        pltpu.SemaphoreType.DMA((2,)), pltpu.SemaphoreType.DMA((2,))],
        compiler_params=_cparams(("arbitrary",)),
        name="combine",
    )(d1, d2, d1, d2, y_tiles, route, h1, g2, ln_g, ln_b)


def _hi_lo(w):
    hi = w.astype(BF16)
    return jnp.stack([hi, (w - hi.astype(F32)).astype(BF16)])


def _block_diag2(w):
    z = jnp.zeros_like(w[0])
    return jnp.concatenate([jnp.concatenate([w[0], z], axis=1), jnp.concatenate([z, w[1]], axis=1)], axis=0)


def kernel(x, c, ctx, c_ctx, emb_ln_g, emb_ln_b, w_mod, b_mod, w_in, tshift_mu, rwkv_w0, rwkv_w2, rwkv_a0, rwkv_a2, rwkv_g2, rwkv_k_k, rwkv_k_a, rwkv_r_k, rwkv_gn_g, rwkv_gn_b, ret_decay, ret_gn_g, ret_gn_b, w_out, ln1_g, ln1_b, router_group, router_group_bias, router_expert, router_expert_bias, expert_w_gate, expert_w_up, expert_w_down, ln2_g, ln2_b):
    assert w_mod.shape[0] == 1, "written for DEPTH == 1 (context outputs are never emitted)"
    b, n_tok, d = x.shape
    n_ctx = ctx.shape[1]
    row = lambda v: v.reshape(1, -1)

    c_rows = jnp.pad(jnp.concatenate([c, c_ctx[None, :]], axis=0), ((0, SUBLANES - b - 1), (0, 0)))
    mod = _modulation(c_rows, w_mod[0], row(b_mod[0]))
    sh1, s1, g1, sh2, s2, g2 = [mod[:b, j * d:(j + 1) * d].reshape(b, 1, d) for j in range(6)]
    sh1c, s1c = [jnp.broadcast_to(mod[b, j * d:(j + 1) * d].reshape(1, 1, d), (b, 1, d)) for j in range(2)]

    w_in_bf16 = w_in[0].astype(BF16)
    pr, pt = _in_proj(x, row(emb_ln_g), row(emb_ln_b), s1, sh1, w_in_bf16)
    pr_c, pt_c = _in_proj(ctx, row(emb_ln_g), row(emb_ln_b), s1c, sh1c, w_in_bf16)

    prep_params = (row(tshift_mu[0]), row(rwkv_w0[0]), _hi_lo(_block_diag2(rwkv_w2[0])), row(rwkv_a0[0]),
                   _hi_lo(_block_diag2(rwkv_a2[0])), _hi_lo(rwkv_g2[0]), row(rwkv_k_k[0]), row(rwkv_k_a[0]),
                   row(rwkv_r_k[0]),
                   _segment_ones(D_RWKV, RWKV_HEAD))
    lat = _rwkv_prepare(pr, prep_params, grid_shift=True)
    cx = _rwkv_prepare(pr_c, prep_params, grid_shift=False)
    r_l, v_l, kk_l, w_l, kd_l, bb_l, gate_l, bonus_l = lat
    r_c, v_c, kk_c, w_c, kd_c, bb_c, _, _ = cx

    y_f, y_b = _wkv7((r_l, v_l, kk_l, w_l, kd_l, bb_l), (r_c, v_c, kk_c, w_c, kd_c, bb_c), b, n_tok, n_ctx)

    cos_t, sin_t = _rope_tables(n_tok)
    t_f, t_b = _retention(pt, pt_c, ret_decay[0], cos_t, sin_t)

    lane_pad = LANES - N_GROUPS - N_EXPERTS
    wr = jnp.pad(jnp.concatenate([router_group[0], router_expert[0]], axis=1), ((0, 0), (0, lane_pad)))
    br = jnp.pad(jnp.concatenate([router_group_bias[0], router_expert_bias[0].reshape(-1)]), (0, lane_pad))
    br = br.reshape(1, LANES)
    vecs = (row(emb_ln_g), row(emb_ln_b), g1, s2, sh2, row(rwkv_gn_g[0]), row(rwkv_gn_b[0]),
            row(ret_gn_g[0]), row(ret_gn_b[0]), row(ln1_g[0]), row(ln1_b[0]), br)
    wr_hi_lo = _hi_lo(wr)
    mats = (_segment_ones(D_RWKV, RWKV_HEAD), w_out[0].astype(BF16), wr_hi_lo[0], wr_hi_lo[1])
    h1, u2, route, route_t, counts = _out_proj(x, y_f, y_b, bonus_l, gate_l, t_f, t_b, pt, vecs, mats)

    n_all = b * n_tok

    e1 = route_t[ROUTE_E1].astype(jnp.int32)
    e2 = route_t[ROUTE_E2].astype(jnp.int32)
    cnt = counts[0, :N_EXPERTS].astype(jnp.int32)
    padded = ((cnt + EXPERT_ROWS - 1) // EXPERT_ROWS) * EXPERT_ROWS
    pends = jnp.cumsum(padded)
    pstarts = pends - padded
    expert_ids = jnp.arange(N_EXPERTS, dtype=jnp.int32)
    start_of = lambda e: jnp.sum(jnp.where(expert_ids[:, None] == e[None, :], pstarts[:, None], 0), axis=0)
    dest1 = (start_of(e1) + route_t[ROUTE_RANK1].astype(jnp.int32)) * TOKEN_SLAB
    dest2 = (start_of(e2) + route_t[ROUTE_RANK2].astype(jnp.int32)) * TOKEN_SLAB
    n_blk = -(-(n_all * 2) // EXPERT_ROWS) + N_EXPERTS
    block_start = jnp.arange(n_blk, dtype=jnp.int32) * EXPERT_ROWS
    block_expert = jnp.minimum(jnp.sum((block_start[:, None] >= pends[None, :]).astype(jnp.int32), axis=1),
                               N_EXPERTS - 1)

    n_used = (pends[N_EXPERTS - 1:] // EXPERT_ROWS).astype(jnp.int32)
    x_slabs = _dispatch(u2, dest1, dest2, n_blk * EXPERT_ROWS)
    y_tiles = _expert_mlp(x_slabs, block_expert, n_used, expert_w_gate[0], expert_w_up[0], expert_w_down[0])
    out = _combine(y_tiles, dest1, dest2, route, h1.reshape(n_all, d), g2, row(ln2_g[0]), row(ln2_b[0]), n_tok)
    return out.reshape(b, n_tok, d)
```
